```python
import math
import jax
import jax.numpy as jnp
from jax import lax
import numpy as np


D_MODEL = 1024
BATCH = 4
SEQ = 4096
DEPTH = 4

HEAD_DIM = 64
N_HEADS_A = 8
N_HEADS_B = 8
N_HEADS_C = 8
N_HEADS_D = 8
MOBA_BLOCK = 256
MOBA_TOPK = 3
MOBA_Q_CHUNK = 32
DILATED_BRANCHES = ((128, 1), (512, 4), (2048, 16))
Q_BLOCK = 128
MLA_Q_RANK = 256
MLA_KV_RANK = 256
MLA_NOPE_DIM = 64
MLA_ROPE_DIM = 32
MLA_V_DIM = 64
ROPE_THETA = 10000.0
REL_BUCKETS = 32
REL_MAX_DIST = 2048
D_FF = 2816
CONV_WIDTH = 3
NORM_EPS = 1e-6
N_EVEN = (DEPTH + 1) // 2
N_ODD = DEPTH // 2
AB_IN = 3 * HEAD_DIM * (N_HEADS_A + N_HEADS_B)
AB_OUT = HEAD_DIM * (N_HEADS_A + N_HEADS_B)
CD_IN = MLA_Q_RANK + MLA_KV_RANK + MLA_ROPE_DIM + 3 * HEAD_DIM * N_HEADS_D
CD_OUT = MLA_V_DIM * N_HEADS_C + HEAD_DIM * N_HEADS_D

kernel_name = 'hybrid_moba_dilated_mla_stickbreaking'


def rmsnorm(x, g):
    x32 = x.astype(jnp.float32)
    y = x32 * lax.rsqrt(jnp.mean(x32 * x32, axis=-1, keepdims=True) + NORM_EPS)
    return (y * g.astype(jnp.float32)).astype(x.dtype)


def t5_bucket(dist):
    dist = jnp.maximum(dist, 0)
    max_exact = REL_BUCKETS // 2
    d_f = jnp.maximum(dist, max_exact).astype(jnp.float32)
    large = max_exact + (jnp.log(d_f / max_exact) / math.log(REL_MAX_DIST / max_exact)
                         * (REL_BUCKETS - max_exact)).astype(jnp.int32)
    large = jnp.minimum(large, REL_BUCKETS - 1)
    return jnp.where(dist < max_exact, dist, large)


def split_cols(t, sizes):
    return jnp.split(t, [int(o) for o in np.cumsum(sizes)[:-1]], axis=-1)


def to_heads(t, n_heads):
    b, s, _ = t.shape
    return t.reshape(b, s, n_heads, -1).transpose(0, 2, 1, 3)


def from_heads(t):
    b, h, s, d = t.shape
    return t.transpose(0, 2, 1, 3).reshape(b, s, h * d)


def query_blocks(q, qb):
    b, h, s, d = q.shape
    return q.reshape(b, h, s // qb, qb, d).transpose(2, 0, 1, 3, 4)


def merge_blocks(o):
    n, b, h, qb, d = o.shape
    return o.transpose(1, 2, 0, 3, 4).reshape(b, h, n * qb, d)


def rope_tables(s):
    inv_freq = 1.0 / (ROPE_THETA ** (jnp.arange(0, MLA_ROPE_DIM, 2, dtype=jnp.float32) / MLA_ROPE_DIM))
    ang = jnp.arange(s, dtype=jnp.float32)[:, None] * inv_freq[None, :]
    return jnp.cos(ang), jnp.sin(ang)


def apply_rope(x, cos, sin):
    cos = cos[:, None, :].astype(x.dtype)
    sin = sin[:, None, :].astype(x.dtype)
    x1, x2 = jnp.split(x, 2, axis=-1)
    return jnp.concatenate([x1 * cos - x2 * sin, x1 * sin + x2 * cos], axis=-1)


def moba_attention(q, k, v, bias_table):
    b, h, s, dh = q.shape
    nb = -(-s // MOBA_BLOCK)
    pad = nb * MOBA_BLOCK - s
    kp = jnp.pad(k, ((0, 0), (0, 0), (0, pad), (0, 0)))
    vp = jnp.pad(v, ((0, 0), (0, 0), (0, pad), (0, 0)))
    kb = kp.reshape(b, h, nb, MOBA_BLOCK, dh)
    vb = vp.reshape(b, h, nb, MOBA_BLOCK, dh)
    k_mean = jnp.mean(kb, axis=3)
    scale = dh ** -0.5
    topk = min(MOBA_TOPK, nb)
    head_idx = jnp.arange(h)[:, None, None, None]
    blk_offsets = jnp.arange(MOBA_BLOCK)
    gather_blocks = jax.vmap(jax.vmap(lambda arr, idx: arr[idx]))

    def chunk_fn(args):
        qc, ci = args
        t = ci * MOBA_Q_CHUNK + jnp.arange(MOBA_Q_CHUNK)
        own = t[0] // MOBA_BLOCK
        gate = jnp.einsum('bhqd,bhnd->bhqn', qc, k_mean).astype(jnp.float32)
        past = jnp.arange(nb)[None, :] < (t // MOBA_BLOCK)[:, None]
        gate = jnp.where(past, gate, -jnp.inf)
        _, sel = lax.top_k(gate, topk)
        sel_valid = sel < own
        k_sel = gather_blocks(kb, sel)
        v_sel = gather_blocks(vb, sel)
        pos_sel = sel[..., None] * MOBA_BLOCK + blk_offsets
        logit_sel = (jnp.einsum('bhqd,bhqkld->bhqkl', qc, k_sel).astype(jnp.float32) * scale
                     + bias_table[head_idx, t5_bucket(t[:, None, None] - pos_sel)].astype(jnp.float32))
        logit_sel = jnp.where(sel_valid[..., None], logit_sel, -jnp.inf)
        k_own = lax.dynamic_slice_in_dim(kp, own * MOBA_BLOCK, MOBA_BLOCK, axis=2)
        v_own = lax.dynamic_slice_in_dim(vp, own * MOBA_BLOCK, MOBA_BLOCK, axis=2)
        pos_own = own * MOBA_BLOCK + blk_offsets
        logit_own = (jnp.einsum('bhqd,bhld->bhql', qc, k_own).astype(jnp.float32) * scale
                     + bias_table[:, t5_bucket(t[:, None] - pos_own[None, :])].astype(jnp.float32))
        logit_own = jnp.where(pos_own[None, :] <= t[:, None], logit_own, -jnp.inf)
        logits = jnp.concatenate([logit_sel.reshape(b, h, MOBA_Q_CHUNK, topk * MOBA_BLOCK), logit_own], axis=-1)
        p = jax.nn.softmax(logits, axis=-1)
        p_sel = p[..., :topk * MOBA_BLOCK].reshape(b, h, MOBA_Q_CHUNK, topk, MOBA_BLOCK).astype(v.dtype)
        p_own = p[..., topk * MOBA_BLOCK:].astype(v.dtype)
        return (jnp.einsum('bhqkl,bhqkld->bhqd', p_sel, v_sel)
                + jnp.einsum('bhql,bhld->bhqd', p_own, v_own))

    n_chunks = s // MOBA_Q_CHUNK
    return merge_blocks(lax.map(chunk_fn, (query_blocks(q, MOBA_Q_CHUNK), jnp.arange(n_chunks))))


def dilated_attention(q, k, v, bias_table):
    b, h, s, dh = q.shape
    scale = dh ** -0.5

    def chunk_fn(args):
        qc, ci = args
        t = ci * Q_BLOCK + jnp.arange(Q_BLOCK)
        lses, outs = [], []
        for window, dil in DILATED_BRANCHES:
            dist = dil * jnp.arange(window // dil + 1)
            pos = t[:, None] - dist[None, :]
            valid = pos >= 0
            pos_c = jnp.maximum(pos, 0)
            k_g = k[:, :, pos_c]
            v_g = v[:, :, pos_c]
            logit = (jnp.einsum('bhqd,bhqmd->bhqm', qc, k_g).astype(jnp.float32) * scale
                     + bias_table[:, t5_bucket(dist)][:, None, :].astype(jnp.float32))
            logit = jnp.where(valid, logit, -jnp.inf)
            lse = jax.nn.logsumexp(logit, axis=-1, keepdims=True)
            p = jnp.exp(logit - lse).astype(v.dtype)
            outs.append(jnp.einsum('bhqm,bhqmd->bhqd', p, v_g))
            lses.append(lse)
        w = jax.nn.softmax(jnp.concatenate(lses, axis=-1), axis=-1).astype(v.dtype)
        return jnp.einsum('bhqr,rbhqd->bhqd', w, jnp.stack(outs))

    n_chunks = s // Q_BLOCK
    return merge_blocks(lax.map(chunk_fn, (query_blocks(q, Q_BLOCK), jnp.arange(n_chunks))))


def causal_softmax_attention(q, k, v):
    s = q.shape[2]
    scale = q.shape[-1] ** -0.5
    kpos = jnp.arange(s)

    def block_fn(args):
        qc, bi = args
        t = bi * Q_BLOCK + jnp.arange(Q_BLOCK)
        logits = jnp.einsum('bhqd,bhsd->bhqs', qc, k).astype(jnp.float32) * scale
        logits = jnp.where(kpos[None, :] <= t[:, None], logits, -jnp.inf)
        p = jax.nn.softmax(logits, axis=-1).astype(v.dtype)
        return jnp.einsum('bhqs,bhsd->bhqd', p, v)

    return merge_blocks(lax.map(block_fn, (query_blocks(q, Q_BLOCK), jnp.arange(s // Q_BLOCK))))


def mla_attention(c_q, c_kv, k_rope, q_norm_g, kv_norm_g, w_uq, w_ukv, cos, sin):
    b, s, _ = c_q.shape
    q = (rmsnorm(c_q, q_norm_g) @ w_uq).reshape(b, s, N_HEADS_C, MLA_NOPE_DIM + MLA_ROPE_DIM)
    kv = (rmsnorm(c_kv, kv_norm_g) @ w_ukv).reshape(b, s, N_HEADS_C, MLA_NOPE_DIM + MLA_V_DIM)
    q_nope, q_pe = q[..., :MLA_NOPE_DIM], q[..., MLA_NOPE_DIM:]
    k_nope, v = kv[..., :MLA_NOPE_DIM], kv[..., MLA_NOPE_DIM:]
    q_pe = apply_rope(q_pe, cos, sin)
    k_pe = apply_rope(k_rope[:, :, None, :], cos, sin)
    k_pe = jnp.broadcast_to(k_pe, (b, s, N_HEADS_C, MLA_ROPE_DIM))
    q_full = jnp.concatenate([q_nope, q_pe], axis=-1).transpose(0, 2, 1, 3)
    k_full = jnp.concatenate([k_nope, k_pe], axis=-1).transpose(0, 2, 1, 3)
    return causal_softmax_attention(q_full, k_full, v.transpose(0, 2, 1, 3))


def stick_breaking_attention(q, k, v):
    s = q.shape[2]
    scale = q.shape[-1] ** -0.5
    kpos = jnp.arange(s)

    def block_fn(args):
        qc, bi = args
        t = bi * Q_BLOCK + jnp.arange(Q_BLOCK)
        z = jnp.einsum('bhqd,bhsd->bhqs', qc, k).astype(jnp.float32) * scale
        strict = kpos[None, :] < t[:, None]
        log_keep = jnp.where(strict, jax.nn.log_sigmoid(-z), 0.0)
        rev = lax.cumsum(log_keep, axis=3, reverse=True)
        after = jnp.concatenate([rev[..., 1:], jnp.zeros_like(rev[..., :1])], axis=-1)
        a = jnp.where(strict, jnp.exp(jax.nn.log_sigmoid(z) + after), 0.0).astype(v.dtype)
        return jnp.einsum('bhqs,bhsd->bhqd', a, v)

    return merge_blocks(lax.map(block_fn, (query_blocks(q, Q_BLOCK), jnp.arange(s // Q_BLOCK))))


def even_mixer(h, w_in, w_out, rel_bias):
    proj = h @ w_in
    wa = N_HEADS_A * HEAD_DIM
    wb = N_HEADS_B * HEAD_DIM
    qa, ka, va, qb, kb, vb = split_cols(proj, [wa, wa, wa, wb, wb, wb])
    o_a = moba_attention(to_heads(qa, N_HEADS_A), to_heads(ka, N_HEADS_A), to_heads(va, N_HEADS_A),
                         rel_bias[:N_HEADS_A])
    o_b = dilated_attention(to_heads(qb, N_HEADS_B), to_heads(kb, N_HEADS_B), to_heads(vb, N_HEADS_B),
                            rel_bias[N_HEADS_A:])
    return jnp.concatenate([from_heads(o_a), from_heads(o_b)], axis=-1) @ w_out


def odd_mixer(h, w_in, q_norm_g, kv_norm_g, w_uq, w_ukv, w_out, cos, sin):
    proj = h @ w_in
    wd = N_HEADS_D * HEAD_DIM
    c_q, c_kv, k_rope, qd, kd, vd = split_cols(proj, [MLA_Q_RANK, MLA_KV_RANK, MLA_ROPE_DIM, wd, wd, wd])
    o_c = mla_attention(c_q, c_kv, k_rope, q_norm_g, kv_norm_g, w_uq, w_ukv, cos, sin)
    o_d = stick_breaking_attention(to_heads(qd, N_HEADS_D), to_heads(kd, N_HEADS_D), to_heads(vd, N_HEADS_D))
    return jnp.concatenate([from_heads(o_c), from_heads(o_d)], axis=-1) @ w_out


def conv_ffn(h, w_up, conv_w, conv_b, w_down):
    s = h.shape[1]
    u = h @ w_up
    up = jnp.pad(u, ((0, 0), (CONV_WIDTH - 1, 0), (0, 0)))
    u = sum(conv_w[j] * up[:, j:j + s] for j in range(CONV_WIDTH)) + conv_b
    gate, val = jnp.split(u, 2, axis=-1)
    return (jax.nn.gelu(gate, approximate=True) * val) @ w_down


def setup_inputs(seed: int = 0) -> dict:
    key = jax.random.key(seed)
    ks = jax.random.split(key, 24)
    f32 = jnp.float32

    def dense(k, shape, fan_in, gain=1.0):
        return jax.random.normal(k, shape, f32) * (gain * fan_in ** -0.5)

    def gain(k, shape):
        return 1.0 + 0.05 * jax.random.normal(k, shape, f32)

    return {
        'x': jax.random.normal(ks[0], (BATCH, SEQ, D_MODEL), f32),
        'c': jax.random.normal(ks[1], (BATCH, D_MODEL), f32),
        'rel_bias': 0.5 * jax.random.normal(ks[2], (N_HEADS_A + N_HEADS_B, REL_BUCKETS), f32),
        'ada_w': dense(ks[3], (DEPTH, D_MODEL, 6 * D_MODEL), D_MODEL, 0.5),
        'ada_b': 0.02 * jax.random.normal(ks[4], (DEPTH, 6 * D_MODEL), f32),
        'mix_pre_g': gain(ks[5], (DEPTH, D_MODEL)),
        'mix_post_g': gain(ks[6], (DEPTH, D_MODEL)),
        'ffn_pre_g': gain(ks[7], (DEPTH, D_MODEL)),
        'ffn_post_g': gain(ks[8], (DEPTH, D_MODEL)),
        'ab_w_in': dense(ks[9], (N_EVEN, D_MODEL, AB_IN), D_MODEL),
        'ab_w_out': dense(ks[10], (N_EVEN, AB_OUT, D_MODEL), AB_OUT),
        'cd_w_in': dense(ks[11], (N_ODD, D_MODEL, CD_IN), D_MODEL),
        'mla_q_norm_g': gain(ks[12], (N_ODD, MLA_Q_RANK)),
        'mla_kv_norm_g': gain(ks[13], (N_ODD, MLA_KV_RANK)),
        'mla_w_uq': dense(ks[14], (N_ODD, MLA_Q_RANK, N_HEADS_C * (MLA_NOPE_DIM + MLA_ROPE_DIM)), MLA_Q_RANK),
        'mla_w_ukv': dense(ks[15], (N_ODD, MLA_KV_RANK, N_HEADS_C * (MLA_NOPE_DIM + MLA_V_DIM)), MLA_KV_RANK),
        'cd_w_out': dense(ks[16], (N_ODD, CD_OUT, D_MODEL), CD_OUT),
        'ffn_w_up': dense(ks[17], (DEPTH, D_MODEL, 2 * D_FF), D_MODEL),
        'ffn_conv_w': dense(ks[18], (DEPTH, CONV_WIDTH, 2 * D_FF), CONV_WIDTH),
        'ffn_conv_b': 0.02 * jax.random.normal(ks[19], (DEPTH, 2 * D_FF), f32),
        'ffn_w_down': dense(ks[20], (DEPTH, D_FF, D_MODEL), D_FF),
    }


def reference(x, c, rel_bias, ada_w, ada_b, mix_pre_g, mix_post_g, ffn_pre_g, ffn_post_g,
              ab_w_in, ab_w_out, cd_w_in, mla_q_norm_g, mla_kv_norm_g, mla_w_uq, mla_w_ukv, cd_w_out,
              ffn_w_up, ffn_conv_w, ffn_conv_b, ffn_w_down):
    s = x.shape[1]
    cos, sin = rope_tables(s)
    cond = jax.nn.silu(c)
    for layer in range(DEPTH):
        mod = cond @ ada_w[layer] + ada_b[layer]
        shift_m, scale_m, gate_m, shift_f, scale_f, gate_f = [m[:, None, :] for m in jnp.split(mod, 6, axis=-1)]
        h = rmsnorm(x, mix_pre_g[layer]) * (1.0 + scale_m) + shift_m
        i = layer // 2
        if layer % 2 == 0:
            y = even_mixer(h, ab_w_in[i], ab_w_out[i], rel_bias)
        else:
            y = odd_mixer(h, cd_w_in[i], mla_q_norm_g[i], mla_kv_norm_g[i], mla_w_uq[i], mla_w_ukv[i],
                          cd_w_out[i], cos, sin)
        x = x + gate_m * rmsnorm(y, mix_post_g[layer])
        h = rmsnorm(x, ffn_pre_g[layer]) * (1.0 + scale_f) + shift_f
        y = conv_ffn(h, ffn_w_up[layer], ffn_conv_w[layer], ffn_conv_b[layer], ffn_w_down[layer])
        x = x + gate_f * rmsnorm(y, ffn_post_g[layer])
    return x
```

```python
import functools
import math

import jax
import jax.numpy as jnp
from jax import lax
from jax.experimental import pallas as pl
from jax.experimental.pallas import tpu as pltpu

F32 = jnp.float32
BF16 = jnp.bfloat16

D_MODEL = 1024
DEPTH = 4
HEAD_DIM = 64
N_HEADS_A = 8
N_HEADS_B = 8
N_HEADS_C = 8
N_HEADS_D = 8
MOBA_BLOCK = 256
MOBA_TOPK = 3
DILATED_BRANCHES = ((128, 1), (512, 4), (2048, 16))
MLA_Q_RANK = 256
MLA_KV_RANK = 256
MLA_NOPE_DIM = 64
MLA_ROPE_DIM = 32
MLA_V_DIM = 64
ROPE_THETA = 10000.0
REL_BUCKETS = 32
REL_MAX_DIST = 2048
D_FF = 2816
CONV_WIDTH = 3
NORM_EPS = 1e-6

LANES = 128
BF16_SUBLANES = 16
VMEM_LIMIT_BYTES = 56 * 1024 * 1024

ATT_TILE = MOBA_BLOCK
ROW_TILE = 512
FF_CHUNK = 256
N_FF_CHUNKS = D_FF // FF_CHUNK
CONV_HALO = BF16_SUBLANES
MOBA_BIAS_TILES = REL_MAX_DIST // ATT_TILE + 2
DIL_BIAS_TILES = DILATED_BRANCHES[-1][0] // ATT_TILE + 1

_NT = (((1,), (1,)), ((), ()))


def _bucket_lower_bounds():
    max_exact = REL_BUCKETS // 2
    ratio = REL_MAX_DIST // max_exact
    n_log = REL_BUCKETS - max_exact
    lows = list(range(max_exact + 1))
    for k in range(1, n_log):
        d = lows[-1]
        while d ** n_log < (max_exact ** n_log) * (ratio ** k):
            d += 1
        lows.append(d)
    return lows


_BUCKET_LOW = _bucket_lower_bounds()


def _dot(a, b):
    return jnp.dot(a, b, preferred_element_type=F32)


def _dot_nt(a, b):
    return lax.dot_general(a, b, _NT, preferred_element_type=F32)


def _split_bf16(x):
    hi = x.astype(BF16)
    lo = (x - hi.astype(F32)).astype(BF16)
    return hi, lo


def _rms(x, g):
    return (x * lax.rsqrt(jnp.mean(x * x, axis=-1, keepdims=True) + NORM_EPS)) * g


def _prenorm(x, g, scale, shift):
    return _rms(x, g) * (1.0 + scale) + shift


def _params(n_grid_dims):
    return pltpu.CompilerParams(dimension_semantics=("arbitrary",) * n_grid_dims,
                                vmem_limit_bytes=VMEM_LIMIT_BYTES)


def _mods_kernel(c_ref, w_ref, b_ref, o_ref):
    c = c_ref[...]
    cond = c * jax.nn.sigmoid(c)
    c_hi, c_lo = _split_bf16(cond)
    w_hi, w_lo = _split_bf16(w_ref[0])
    o_ref[0] = _dot(c_hi, w_hi) + _dot(c_hi, w_lo) + _dot(c_lo, w_hi) + b_ref[0]


def _mods(c, ada_w, ada_b):
    b, d = c.shape
    rows = BF16_SUBLANES
    n_out = ada_w.shape[-1]
    tn = n_out // 4
    c_pad = jnp.zeros((rows, d), F32).at[:b].set(c)
    out = pl.pallas_call(
        _mods_kernel,
        grid=(DEPTH, n_out // tn),
        in_specs=[pl.BlockSpec((rows, d), lambda l, j: (0, 0)),
                  pl.BlockSpec((1, d, tn), lambda l, j: (l, 0, j)),
                  pl.BlockSpec((1, 1, tn), lambda l, j: (l, 0, j))],
        out_specs=pl.BlockSpec((1, rows, tn), lambda l, j: (l, 0, j)),
        out_shape=jax.ShapeDtypeStruct((DEPTH, rows, n_out), F32),
        compiler_params=_params(2),
        name="ada_mods",
    )(c_pad, ada_w, ada_b.reshape(DEPTH, 1, n_out))
    return out[:, :b]


def _bias_tiles_kernel(tab_ref, o_ref, *, head_off, dilated):
    h = pl.program_id(0) + head_off
    d = pl.program_id(1)
    t = ATT_TILE
    r = lax.broadcasted_iota(jnp.int32, (t, t), 0)
    c = lax.broadcasted_iota(jnp.int32, (t, t), 1)
    dist = d * t + r - c
    val = jnp.full((t, t), tab_ref[h, 0], F32)
    for b in range(1, REL_BUCKETS):
        val = jnp.where(dist >= _BUCKET_LOW[b], tab_ref[h, b], val)
    if dilated:
        mult = jnp.zeros((t, t), F32)
        for window, dil in DILATED_BRANCHES:
            hit = jnp.where(dist <= window, jnp.where((dist & (dil - 1)) == 0, 1.0, 0.0), 0.0)
            mult = mult + hit
        log_mult = jnp.where(mult > 2.5, math.log(3.0), jnp.where(mult > 1.5, math.log(2.0), 0.0))
        val = jnp.where(mult > 0.5, val + log_mult, -jnp.inf)
    o_ref[0, 0] = jnp.where(dist >= 0, val, -jnp.inf)


def _bias_tiles(rel_bias, head_off, n_heads, n_tiles, dilated):
    t = ATT_TILE
    return pl.pallas_call(
        functools.partial(_bias_tiles_kernel, head_off=head_off, dilated=dilated),
        grid=(n_heads, n_tiles),
        in_specs=[pl.BlockSpec(memory_space=pltpu.SMEM)],
        out_specs=pl.BlockSpec((1, 1, t, t), lambda h, d: (h, d, 0, 0)),
        out_shape=jax.ShapeDtypeStruct((n_heads, n_tiles, t, t), F32),
        compiler_params=_params(2),
        name="dilated_bias_tiles" if dilated else "moba_bias_tiles",
    )(rel_bias)


def _even_proj_kernel(x_ref, mod_ref, g_ref, w_ref, o_ref, km_ref):
    d = D_MODEL
    mod = mod_ref[0]
    h = _prenorm(x_ref[...], g_ref[...], mod[:, d:2 * d], mod[:, 0:d])
    p = _dot(h.astype(BF16), w_ref[...])
    o_ref[...] = p.astype(BF16)
    wa = N_HEADS_A * HEAD_DIM
    ka = p[:, wa:2 * wa]
    nb = ka.shape[0] // MOBA_BLOCK
    km_ref[0] = jnp.mean(ka.reshape(nb, MOBA_BLOCK, wa), axis=1)


def _even_proj(x, mod, g, w_in, seq):
    rows, d = x.shape
    tm = ROW_TILE
    n = w_in.shape[1]
    wa = N_HEADS_A * HEAD_DIM
    per_seq = seq // tm
    proj, km = pl.pallas_call(
        _even_proj_kernel,
        grid=(rows // tm,),
        in_specs=[pl.BlockSpec((tm, d), lambda i: (i, 0)),
                  pl.BlockSpec((1, 1, 6 * d), lambda i: (i // per_seq, 0, 0)),
                  pl.BlockSpec((1, d), lambda i: (0, 0)),
                  pl.BlockSpec((d, n), lambda i: (0, 0))],
        out_specs=[pl.BlockSpec((tm, n), lambda i: (i, 0)),
                   pl.BlockSpec((1, tm // MOBA_BLOCK, wa), lambda i: (i, 0, 0))],
        out_shape=[jax.ShapeDtypeStruct((rows, n), BF16),
                   jax.ShapeDtypeStruct((rows // tm, tm // MOBA_BLOCK, wa), F32)],
        compiler_params=_params(1),
        name="even_proj",
    )(x, mod, g, w_in)
    return proj, km


def _head_masks(shape):
    lane = lax.broadcasted_iota(jnp.int32, shape, 1)
    return lane < HEAD_DIM


def _split_heads(q2):
    first = _head_masks(q2.shape)
    zero = jnp.zeros_like(q2)
    return jnp.where(first, q2, zero), jnp.where(first, zero, q2)


def _moba_select(qh, km_ref, sel_ref, qi):
    t = ATT_TILE
    km_hi, km_lo = _split_bf16(km_ref[0])
    nb = km_hi.shape[0]
    blk = lax.broadcasted_iota(jnp.int32, (nb, t), 0)
    past = blk < qi
    for h in range(2):
        gate = _dot_nt(km_hi, qh[h]) + _dot_nt(km_lo, qh[h])
        gate = jnp.where(past, gate, -jnp.inf)
        beaten = jnp.zeros((nb, t), F32)
        for other in range(nb):
            row = gate[other:other + 1, :]
            wins = jnp.where(row > gate, 1.0, jnp.where(row == gate, jnp.where(blk > other, 1.0, 0.0), 0.0))
            beaten = beaten + wins
        sel = jnp.where(past, jnp.where(beaten < MOBA_TOPK, 1.0, 0.0), 0.0)
        sel = jnp.concatenate([sel, jnp.zeros((LANES - nb, t), F32)], axis=0)
        sel_ref[h] = sel.T


def _toeplitz_attn_kernel(*refs, moba, max_dist_tiles):
    if moba:
        q_ref, k_ref, v_ref, bias_ref, km_ref, o_ref, m_ref, l_ref, acc_ref, sel_ref = refs
    else:
        q_ref, k_ref, v_ref, bias_ref, o_ref, m_ref, l_ref, acc_ref = refs
    t = ATT_TILE
    qi = pl.program_id(2)
    qh = _split_heads(q_ref[0])
    n_bias = bias_ref.shape[1]

    def kv_tile(j):
        start = pl.multiple_of(j * t, t)
        return k_ref[0, pl.ds(start, t), :], v_ref[0, pl.ds(start, t), :]

    kb, vb = kv_tile(qi)
    for h in range(2):
        s = _dot_nt(qh[h], kb) + bias_ref[h, 0]
        m = jnp.max(s, axis=1, keepdims=True)
        p = jnp.exp(s - m)
        m_ref[h] = m
        l_ref[h] = jnp.sum(p, axis=1, keepdims=True)
        acc_ref[h] = _dot(p.astype(BF16), vb)

    if moba:
        _moba_select(qh, km_ref, sel_ref, qi)
        lane = lax.broadcasted_iota(jnp.int32, (t, LANES), 1)

    def past_tile(dt, carry):
        j = qi - dt
        kb, vb = kv_tile(j)
        bidx = jnp.minimum(dt, n_bias - 1)
        for h in range(2):
            s = _dot_nt(qh[h], kb) + bias_ref[h, bidx]
            m_old = m_ref[h]
            m_new = jnp.maximum(m_old, jnp.max(s, axis=1, keepdims=True))
            if moba:
                on = jnp.sum(jnp.where(lane == j, sel_ref[h], 0.0), axis=1, keepdims=True) > 0.5
                m_new = jnp.where(on, m_new, m_old)
                p = jnp.exp(s - jnp.where(on, m_new, jnp.inf))
            else:
                p = jnp.exp(s - m_new)
            alpha = jnp.exp(m_old - m_new)
            m_ref[h] = m_new
            l_ref[h] = alpha * l_ref[h] + jnp.sum(p, axis=1, keepdims=True)
            acc_ref[h] = alpha * acc_ref[h] + _dot(p.astype(BF16), vb)
        return carry

    n_past = qi if max_dist_tiles is None else jnp.minimum(qi, max_dist_tiles)
    lax.fori_loop(1, n_past + 1, past_tile, 0)

    first = _head_masks((t, LANES))
    out = jnp.where(first, acc_ref[0] / l_ref[0], acc_ref[1] / l_ref[1])
    o_ref[0] = out.astype(BF16)


def _toeplitz_attn(proj, bias, km, batch, seq, col0, moba):
    t = ATT_TILE
    n_pairs = N_HEADS_A // 2
    n_bias = bias.shape[1]
    in_specs = [pl.BlockSpec((1, t, LANES), lambda b, p, i: (b, i, col0 + p)),
                pl.BlockSpec((1, seq, LANES), lambda b, p, i: (b, 0, col0 + n_pairs + p)),
                pl.BlockSpec((1, seq, LANES), lambda b, p, i: (b, 0, col0 + 2 * n_pairs + p)),
                pl.BlockSpec((2, n_bias, t, t), lambda b, p, i: (p, 0, 0, 0))]
    args = [proj, proj, proj, bias]
    scratch = [pltpu.VMEM((2, t, 1), F32), pltpu.VMEM((2, t, 1), F32), pltpu.VMEM((2, t, LANES), F32)]
    if moba:
        in_specs.append(pl.BlockSpec((1, seq // MOBA_BLOCK, LANES), lambda b, p, i: (b, 0, p)))
        args.append(km)
        scratch.append(pltpu.VMEM((2, t, LANES), F32))
    return pl.pallas_call(
        functools.partial(_toeplitz_attn_kernel, moba=moba,
                          max_dist_tiles=None if moba else DIL_BIAS_TILES - 1),
        grid=(batch, n_pairs, seq // t),
        in_specs=in_specs,
        out_specs=pl.BlockSpec((1, t, LANES), lambda b, p, i: (b, i, p)),
        out_shape=jax.ShapeDtypeStruct((batch, seq, n_pairs * LANES), BF16),
        scratch_shapes=scratch,
        compiler_params=_params(3),
        name="moba_attn" if moba else "dilated_attn",
    )(*args)


def _odd_proj_kernel(x_ref, mod_ref, g_ref, win_ref, gq_ref, gkv_ref, wq_ref, wkv_ref,
                     cq_ref, sq_ref, ck_ref, sk_ref, qm_ref, km_ref, vm_ref, sb_ref):
    d = D_MODEL
    mod = mod_ref[0]
    h = _prenorm(x_ref[...], g_ref[...], mod[:, d:2 * d], mod[:, 0:d])
    p = _dot(h.astype(BF16), win_ref[...])
    o = MLA_Q_RANK + MLA_KV_RANK
    c_q, c_kv = p[:, 0:MLA_Q_RANK], p[:, MLA_Q_RANK:o]
    k_rope, k_rope_swapped = p[:, o:o + LANES], p[:, o + LANES:o + 2 * LANES]
    sb_ref[...] = p[:, o + 2 * LANES:].astype(BF16)
    q12 = _dot(_rms(c_q, gq_ref[...]).astype(BF16), wq_ref[...])
    kv = _dot(_rms(c_kv, gkv_ref[...]).astype(BF16), wkv_ref[...])
    k_pe = k_rope * ck_ref[...] + k_rope_swapped * sk_ref[...]
    cq, sq = cq_ref[...], sq_ref[...]
    half = N_HEADS_C * LANES
    for hh in range(N_HEADS_C):
        cols = slice(hh * LANES, (hh + 1) * LANES)
        swapped = slice(half + hh * LANES, half + (hh + 1) * LANES)
        qm_ref[:, cols] = (q12[:, cols] * cq + q12[:, swapped] * sq).astype(BF16)
        km_ref[:, cols] = (kv[:, cols] + k_pe).astype(BF16)
    vm_ref[...] = kv[:, half:].astype(BF16)


def _odd_proj(x, mod, g, w_in, gq, gkv, wq, wkv, tables, seq):
    rows, d = x.shape
    tm = ROW_TILE
    per_seq = seq // tm
    n_in = w_in.shape[1]
    n_sb = 3 * N_HEADS_D * HEAD_DIM
    n_q = N_HEADS_C * LANES
    n_v = N_HEADS_C * MLA_V_DIM
    const = lambda i: (0, 0)
    table_spec = pl.BlockSpec((tm, LANES), lambda i: (i % per_seq, 0))
    return pl.pallas_call(
        _odd_proj_kernel,
        grid=(rows // tm,),
        in_specs=[pl.BlockSpec((tm, d), lambda i: (i, 0)),
                  pl.BlockSpec((1, 1, 6 * d), lambda i: (i // per_seq, 0, 0)),
                  pl.BlockSpec((1, d), const),
                  pl.BlockSpec((d, n_in), const),
                  pl.BlockSpec((1, MLA_Q_RANK), const),
                  pl.BlockSpec((1, MLA_KV_RANK), const),
                  pl.BlockSpec(wq.shape, const),
                  pl.BlockSpec(wkv.shape, const),
                  table_spec, table_spec, table_spec, table_spec],
        out_specs=[pl.BlockSpec((tm, n_q), lambda i: (i, 0)),
                   pl.BlockSpec((tm, n_q), lambda i: (i, 0)),
                   pl.BlockSpec((tm, n_v), lambda i: (i, 0)),
                   pl.BlockSpec((tm, n_sb), lambda i: (i, 0))],
        out_shape=[jax.ShapeDtypeStruct((rows, n_q), BF16),
                   jax.ShapeDtypeStruct((rows, n_q), BF16),
                   jax.ShapeDtypeStruct((rows, n_v), BF16),
                   jax.ShapeDtypeStruct((rows, n_sb), BF16)],
        compiler_params=_params(1),
        name="odd_proj",
    )(x, mod, g, w_in, gq, gkv, wq, wkv, *tables)


def _mla_attn_kernel(q_ref, k_ref, v_ref, o_ref, m_ref, l_ref, acc_ref):
    t = ATT_TILE
    qi = pl.program_id(2)
    q2 = q_ref[0]
    qh = (q2[:, 0:LANES], q2[:, LANES:2 * LANES])

    def kv_tile(j):
        start = pl.multiple_of(j * t, t)
        k2 = k_ref[0, pl.ds(start, t), :]
        return (k2[:, 0:LANES], k2[:, LANES:2 * LANES]), v_ref[0, pl.ds(start, t), :]

    r = lax.broadcasted_iota(jnp.int32, (t, t), 0)
    c = lax.broadcasted_iota(jnp.int32, (t, t), 1)
    kb, vb = kv_tile(qi)
    for h in range(2):
        s = jnp.where(c <= r, _dot_nt(qh[h], kb[h]), -jnp.inf)
        m = jnp.max(s, axis=1, keepdims=True)
        p = jnp.exp(s - m)
        m_ref[h] = m
        l_ref[h] = jnp.sum(p, axis=1, keepdims=True)
        acc_ref[h] = _dot(p.astype(BF16), vb)

    def past_tile(dt, carry):
        kb, vb = kv_tile(qi - dt)
        for h in range(2):
            s = _dot_nt(qh[h], kb[h])
            m_old = m_ref[h]
            m_new = jnp.maximum(m_old, jnp.max(s, axis=1, keepdims=True))
            p = jnp.exp(s - m_new)
            alpha = jnp.exp(m_old - m_new)
            m_ref[h] = m_new
            l_ref[h] = alpha * l_ref[h] + jnp.sum(p, axis=1, keepdims=True)
            acc_ref[h] = alpha * acc_ref[h] + _dot(p.astype(BF16), vb)
        return carry

    lax.fori_loop(1, qi + 1, past_tile, 0)
    first = _head_masks((t, LANES))
    out = jnp.where(first, acc_ref[0] / l_ref[0], acc_ref[1] / l_ref[1])
    o_ref[0] = out.astype(BF16)


def _mla_attn(qm, km, vm, batch, seq):
    t = ATT_TILE
    n_pairs = N_HEADS_C // 2
    return pl.pallas_call(
        _mla_attn_kernel,
        grid=(batch, n_pairs, seq // t),
        in_specs=[pl.BlockSpec((1, t, 2 * LANES), lambda b, p, i: (b, i, p)),
                  pl.BlockSpec((1, seq, 2 * LANES), lambda b, p, i: (b, 0, p)),
                  pl.BlockSpec((1, seq, LANES), lambda b, p, i: (b, 0, p))],
        out_specs=pl.BlockSpec((1, t, LANES), lambda b, p, i: (b, i, p)),
        out_shape=jax.ShapeDtypeStruct((batch, seq, n_pairs * LANES), BF16),
        scratch_shapes=[pltpu.VMEM((2, t, 1), F32), pltpu.VMEM((2, t, 1), F32),
                        pltpu.VMEM((2, t, LANES), F32)],
        compiler_params=_params(3),
        name="mla_attn",
    )(qm, km, vm)


def _stick_attn_kernel(q_ref, k_ref, v_ref, o_ref, carry_ref, acc_ref):
    t = ATT_TILE
    qi = pl.program_id(2)
    qh = _split_heads(q_ref[0])
    r = lax.broadcasted_iota(jnp.int32, (t, t), 0)
    c = lax.broadcasted_iota(jnp.int32, (t, t), 1)
    later = jnp.where(r > c, 1.0, 0.0).astype(BF16)
    strict = c < r

    def tile(j, diagonal):
        start = pl.multiple_of(j * t, t)
        kb = k_ref[0, pl.ds(start, t), :]
        vb = v_ref[0, pl.ds(start, t), :]
        for h in range(2):
            z = _dot_nt(qh[h], kb)
            soft = jnp.log(1.0 + jnp.exp(-jnp.abs(z)))
            log_beta = jnp.minimum(z, 0.0) - soft
            log_keep = jnp.minimum(-z, 0.0) - soft
            if diagonal:
                log_keep = jnp.where(strict, log_keep, 0.0)
            keep_hi, keep_lo = _split_bf16(log_keep)
            carry = 0.0 if diagonal else carry_ref[h]
            after = _dot(keep_hi, later) + _dot(keep_lo, later) + carry
            a = jnp.exp(log_beta + after)
            if diagonal:
                a = jnp.where(strict, a, 0.0)
            pv = _dot(a.astype(BF16), vb)
            row_sum = jnp.sum(log_keep, axis=1, keepdims=True)
            if diagonal:
                acc_ref[h] = pv
                carry_ref[h] = row_sum
            else:
                acc_ref[h] = acc_ref[h] + pv
                carry_ref[h] = carry_ref[h] + row_sum

    tile(qi, True)

    def past_tile(dt, carry):
        tile(qi - dt, False)
        return carry

    lax.fori_loop(1, qi + 1, past_tile, 0)
    first = _head_masks((t, LANES))
    o_ref[0] = jnp.where(first, acc_ref[0], acc_ref[1]).astype(BF16)


def _stick_attn(sb, batch, seq):
    t = ATT_TILE
    n_pairs = N_HEADS_D // 2
    return pl.pallas_call(
        _stick_attn_kernel,
        grid=(batch, n_pairs, seq // t),
        in_specs=[pl.BlockSpec((1, t, LANES), lambda b, p, i: (b, i, p)),
                  pl.BlockSpec((1, seq, LANES), lambda b, p, i: (b, 0, n_pairs + p)),
                  pl.BlockSpec((1, seq, LANES), lambda b, p, i: (b, 0, 2 * n_pairs + p))],
        out_specs=pl.BlockSpec((1, t, LANES), lambda b, p, i: (b, i, p)),
        out_shape=jax.ShapeDtypeStruct((batch, seq, n_pairs * LANES), BF16),
        scratch_shapes=[pltpu.VMEM((2, t, 1), F32), pltpu.VMEM((2, t, LANES), F32)],
        compiler_params=_params(3),
        name="stick_attn",
    )(sb, sb, sb)


def _mix_out_kernel(oa_ref, ob_ref, x_ref, mod_ref, g_ref, w_ref, o_ref):
    d = D_MODEL
    half = oa_ref.shape[1]
    y = _dot(oa_ref[...], w_ref[0:half, :]) + _dot(ob_ref[...], w_ref[half:2 * half, :])
    gate = mod_ref[0][:, 2 * d:3 * d]
    o_ref[...] = x_ref[...] + gate * _rms(y, g_ref[...])


def _mix_out(oa, ob, x, mod, g, w_out, seq):
    rows, d = x.shape
    tm = ROW_TILE
    per_seq = seq // tm
    half = oa.shape[1]
    return pl.pallas_call(
        _mix_out_kernel,
        grid=(rows // tm,),
        in_specs=[pl.BlockSpec((tm, half), lambda i: (i, 0)),
                  pl.BlockSpec((tm, half), lambda i: (i, 0)),
                  pl.BlockSpec((tm, d), lambda i: (i, 0)),
                  pl.BlockSpec((1, 1, 6 * d), lambda i: (i // per_seq, 0, 0)),
                  pl.BlockSpec((1, d), lambda i: (0, 0)),
                  pl.BlockSpec((2 * half, d), lambda i: (0, 0))],
        out_specs=pl.BlockSpec((tm, d), lambda i: (i, 0)),
        out_shape=jax.ShapeDtypeStruct((rows, d), F32),
        compiler_params=_params(1),
        name="mix_out",
    )(oa, ob, x, mod, g, w_out)


def _ffn_kernel(x_ref, xh_ref, mod_ref, gpre_ref, gpost_ref, wg_ref, wv_ref, cwg_ref, cwv_ref,
                cbg_ref, cbv_ref, wd_ref, o_ref, h_ref, ug_ref, uv_ref, acc_ref, *, per_seq):
    d = D_MODEL
    tm = ROW_TILE
    halo = CONV_HALO
    i = pl.program_id(0)
    ch = pl.program_id(1)
    mod = mod_ref[0]

    @pl.when(ch == 0)
    def _():
        shift, scale = mod[:, 3 * d:4 * d], mod[:, 4 * d:5 * d]
        g = gpre_ref[...]
        ahead = _prenorm(xh_ref[...], g, scale, shift)
        ahead = jnp.where(i % per_seq == 0, 0.0, ahead)
        h_ref[0:halo, :] = ahead.astype(BF16)
        h_ref[halo:halo + tm, :] = _prenorm(x_ref[...], g, scale, shift).astype(BF16)
        acc_ref[...] = jnp.zeros_like(acc_ref)

    h = h_ref[...]
    ug_ref[...] = _dot(h, wg_ref[0])
    uv_ref[...] = _dot(h, wv_ref[0])

    def conv(u_ref, cw_ref, cb_ref):
        w = cw_ref[0]
        out = w[0:1, :] * u_ref[halo - 2:halo - 2 + tm, :]
        out = out + w[1:2, :] * u_ref[halo - 1:halo - 1 + tm, :]
        out = out + w[2:3, :] * u_ref[halo:halo + tm, :]
        return out + cb_ref[0]

    gate = conv(ug_ref, cwg_ref, cbg_ref)
    val = conv(uv_ref, cwv_ref, cbv_ref)
    act = jax.nn.gelu(gate, approximate=True) * val
    acc_ref[...] += _dot(act.astype(BF16), wd_ref[0])

    @pl.when(ch == N_FF_CHUNKS - 1)
    def _():
        gate_f = mod[:, 5 * d:6 * d]
        o_ref[...] = x_ref[...] + gate_f * _rms(acc_ref[...], gpost_ref[...])


def _ffn(x, mod, g_pre, g_post, wg, wv, cwg, cwv, cbg, cbv, wd, seq):
    rows, d = x.shape
    tm = ROW_TILE
    halo = CONV_HALO
    cw = FF_CHUNK
    per_seq = seq // tm
    chunk = lambda i, c: (c, 0, 0)
    return pl.pallas_call(
        functools.partial(_ffn_kernel, per_seq=per_seq),
        grid=(rows // tm, N_FF_CHUNKS),
        in_specs=[pl.BlockSpec((tm, d), lambda i, c: (i, 0)),
                  pl.BlockSpec((halo, d), lambda i, c: (jnp.maximum(i * (tm // halo) - 1, 0), 0)),
                  pl.BlockSpec((1, 1, 6 * d), lambda i, c: (i // per_seq, 0, 0)),
                  pl.BlockSpec((1, d), lambda i, c: (0, 0)),
                  pl.BlockSpec((1, d), lambda i, c: (0, 0)),
                  pl.BlockSpec((1, d, cw), chunk),
                  pl.BlockSpec((1, d, cw), chunk),
                  pl.BlockSpec((1, CONV_WIDTH, cw), chunk),
                  pl.BlockSpec((1, CONV_WIDTH, cw), chunk),
                  pl.BlockSpec((1, 1, cw), chunk),
                  pl.BlockSpec((1, 1, cw), chunk),
                  pl.BlockSpec((1, cw, d), chunk)],
        out_specs=pl.BlockSpec((tm, d), lambda i, c: (i, 0)),
        out_shape=jax.ShapeDtypeStruct((rows, d), F32),
        scratch_shapes=[pltpu.VMEM((tm + halo, d), BF16),
                        pltpu.VMEM((tm + halo, cw), F32),
                        pltpu.VMEM((tm + halo, cw), F32),
                        pltpu.VMEM((tm, d), F32)],
        compiler_params=_params(2),
        name="conv_ffn",
    )(x, x, mod, g_pre, g_post, wg, wv, cwg, cwv, cbg, cbv, wd)


def _chunk_cols(w):
    k = w.shape[0]
    return w.reshape(k, N_FF_CHUNKS, FF_CHUNK).transpose(1, 0, 2)


def _rotate_half_cols(w):
    half = w.shape[-1] // 2
    return jnp.concatenate([-w[..., half:], w[..., :half]], axis=-1)


def _pad_cols(w, left, total):
    return jnp.pad(w, ((0, 0), (left, total - left - w.shape[1])))


def _rope_tables(seq):
    inv_freq = 1.0 / (ROPE_THETA ** (jnp.arange(0, MLA_ROPE_DIM, 2, dtype=F32) / MLA_ROPE_DIM))
    ang = jnp.arange(seq, dtype=F32)[:, None] * inv_freq[None, :]
    cos, sin = jnp.cos(ang), jnp.sin(ang)
    cos2 = _pad_cols(jnp.concatenate([cos, cos], axis=1), MLA_NOPE_DIM, LANES)
    sin2 = _pad_cols(jnp.concatenate([sin, sin], axis=1), MLA_NOPE_DIM, LANES)
    scale = (MLA_NOPE_DIM + MLA_ROPE_DIM) ** -0.5
    nope_ones = _pad_cols(jnp.ones((seq, MLA_NOPE_DIM), F32), 0, LANES)
    return (scale * (cos2 + nope_ones), scale * sin2, cos2, sin2)


def _odd_weights(w_in, w_uq, w_ukv):
    d = w_in.shape[0]
    o = MLA_Q_RANK + MLA_KV_RANK
    w_rope = w_in[:, o:o + MLA_ROPE_DIM]
    scale_d = HEAD_DIM ** -0.5
    wd = N_HEADS_D * HEAD_DIM
    sb0 = o + MLA_ROPE_DIM
    w_in2 = jnp.concatenate([
        w_in[:, :o],
        _pad_cols(w_rope, MLA_NOPE_DIM, LANES),
        _pad_cols(_rotate_half_cols(w_rope), MLA_NOPE_DIM, LANES),
        w_in[:, sb0:sb0 + wd] * scale_d,
        w_in[:, sb0 + wd:],
    ], axis=1).astype(BF16)
    qd = MLA_NOPE_DIM + MLA_ROPE_DIM
    uq = w_uq.reshape(MLA_Q_RANK, N_HEADS_C, qd)
    plain = jnp.pad(uq, ((0, 0), (0, 0), (0, LANES - qd)))
    swapped = jnp.pad(_rotate_half_cols(uq[..., MLA_NOPE_DIM:]),
                      ((0, 0), (0, 0), (MLA_NOPE_DIM, LANES - qd)))
    wq = jnp.concatenate([plain.reshape(MLA_Q_RANK, -1), swapped.reshape(MLA_Q_RANK, -1)], axis=1).astype(BF16)
    ukv = w_ukv.reshape(MLA_KV_RANK, N_HEADS_C, MLA_NOPE_DIM + MLA_V_DIM)
    k_nope = jnp.pad(ukv[..., :MLA_NOPE_DIM], ((0, 0), (0, 0), (0, LANES - MLA_NOPE_DIM)))
    v = ukv[..., MLA_NOPE_DIM:]
    wkv = jnp.concatenate([k_nope.reshape(MLA_KV_RANK, -1), v.reshape(MLA_KV_RANK, -1)], axis=1).astype(BF16)
    return w_in2, wq, wkv


def _even_weights(w_in):
    wa = N_HEADS_A * HEAD_DIM
    scale = HEAD_DIM ** -0.5
    col = jnp.arange(w_in.shape[1])
    is_q = (col < wa) | ((col >= 3 * wa) & (col < 4 * wa))
    return (w_in * jnp.where(is_q, scale, 1.0)).astype(BF16)


def kernel(x, c, rel_bias, ada_w, ada_b, mix_pre_g, mix_post_g, ffn_pre_g, ffn_post_g, ab_w_in, ab_w_out,
           cd_w_in, mla_q_norm_g, mla_kv_norm_g, mla_w_uq, mla_w_ukv, cd_w_out, ffn_w_up, ffn_conv_w,
           ffn_conv_b, ffn_w_down):
    batch, seq, d = x.shape
    assert d == D_MODEL and seq % ROW_TILE == 0 and seq % ATT_TILE == 0
    assert seq // MOBA_BLOCK <= LANES
    rows = batch * seq
    xf = x.reshape(rows, d)

    mods = _mods(c, ada_w, ada_b)
    bias_a = _bias_tiles(rel_bias, 0, N_HEADS_A, MOBA_BIAS_TILES, dilated=False)
    bias_b = _bias_tiles(rel_bias, N_HEADS_A, N_HEADS_B, DIL_BIAS_TILES, dilated=True)
    tables = _rope_tables(seq)
    pair_blocks = N_HEADS_A // 2

    for layer in range(DEPTH):
        mod = mods[layer].reshape(batch, 1, 6 * d)
        i = layer // 2
        if layer % 2 == 0:
            proj, km = _even_proj(xf, mod, mix_pre_g[layer].reshape(1, d), _even_weights(ab_w_in[i]), seq)
            proj = proj.reshape(batch, seq, -1)
            km = km.reshape(batch, seq // MOBA_BLOCK, -1)
            o_first = _toeplitz_attn(proj, bias_a, km, batch, seq, 0, moba=True)
            o_second = _toeplitz_attn(proj, bias_b, None, batch, seq, 3 * pair_blocks, moba=False)
            w_out = ab_w_out[i]
        else:
            w_in2, wq, wkv = _odd_weights(cd_w_in[i], mla_w_uq[i], mla_w_ukv[i])
            qm, km, vm, sb = _odd_proj(xf, mod, mix_pre_g[layer].reshape(1, d), w_in2,
                                       mla_q_norm_g[i].reshape(1, -1), mla_kv_norm_g[i].reshape(1, -1),
                                       wq, wkv, tables, seq)
            o_first = _mla_attn(qm.reshape(batch, seq, -1), km.reshape(batch, seq, -1),
                                vm.reshape(batch, seq, -1), batch, seq)
            o_second = _stick_attn(sb.reshape(batch, seq, -1), batch, seq)
            w_out = cd_w_out[i]
        xf = _mix_out(o_first.reshape(rows, -1), o_second.reshape(rows, -1), xf, mod,
                      mix_post_g[layer].reshape(1, d), w_out.astype(BF16), seq)
        w_up = ffn_w_up[layer].astype(BF16)
        conv_w, conv_b = ffn_conv_w[layer], ffn_conv_b[layer].reshape(1, -1)
        xf = _ffn(xf, mod, ffn_pre_g[layer].reshape(1, d), ffn_post_g[layer].reshape(1, d),
                  _chunk_cols(w_up[:, :D_FF]), _chunk_cols(w_up[:, D_FF:]),
                  _chunk_cols(conv_w[:, :D_FF]), _chunk_cols(conv_w[:, D_FF:]),
                  _chunk_cols(conv_b[:, :D_FF]), _chunk_cols(conv_b[:, D_FF:]),
                  ffn_w_down[layer].astype(BF16).reshape(N_FF_CHUNKS, FF_CHUNK, d), seq)
    return xf.reshape(batch, seq, d)
```

```python
import functools
import math

import jax
import jax.numpy as jnp
from jax import lax
from jax.experimental import pallas as pl
from jax.experimental.pallas import tpu as pltpu

F32 = jnp.float32
BF16 = jnp.bfloat16

D_MODEL = 1024
DEPTH = 4
HEAD_DIM = 64
N_HEADS_A = 8
N_HEADS_B = 8
N_HEADS_C = 8
N_HEADS_D = 8
MOBA_BLOCK = 256
MOBA_TOPK = 3
DILATED_BRANCHES = ((128, 1), (512, 4), (2048, 16))
MLA_Q_RANK = 256
MLA_KV_RANK = 256
MLA_NOPE_DIM = 64
MLA_ROPE_DIM = 32
MLA_V_DIM = 64
ROPE_THETA = 10000.0
REL_BUCKETS = 32
REL_MAX_DIST = 2048
D_FF = 2816
CONV_WIDTH = 3
NORM_EPS = 1e-6

LANES = 128
BF16_SUBLANES = 16
VMEM_LIMIT_BYTES = 56 * 1024 * 1024

ATT_TILE = MOBA_BLOCK
ROW_TILE = 512
FF_CHUNK = 256
N_FF_CHUNKS = D_FF // FF_CHUNK
CONV_HALO = BF16_SUBLANES
MOBA_BIAS_TILES = REL_MAX_DIST // ATT_TILE + 2
DIL_BIAS_TILES = DILATED_BRANCHES[-1][0] // ATT_TILE + 1

_NT = (((1,), (1,)), ((), ()))


def _bucket_lower_bounds():
    max_exact = REL_BUCKETS // 2
    ratio = REL_MAX_DIST // max_exact
    n_log = REL_BUCKETS - max_exact
    lows = list(range(max_exact + 1))
    for k in range(1, n_log):
        d = lows[-1]
        while d ** n_log < (max_exact ** n_log) * (ratio ** k):
            d += 1
        lows.append(d)
    return lows


_BUCKET_LOW = _bucket_lower_bounds()


def _dot(a, b):
    return jnp.dot(a, b, preferred_element_type=F32)


def _dot_nt(a, b):
    return lax.dot_general(a, b, _NT, preferred_element_type=F32)


def _split_bf16(x):
    hi = x.astype(BF16)
    lo = (x - hi.astype(F32)).astype(BF16)
    return hi, lo


def _rms(x, g):
    return (x * lax.rsqrt(jnp.mean(x * x, axis=-1, keepdims=True) + NORM_EPS)) * g


def _prenorm(x, g, scale, shift):
    return _rms(x, g) * (1.0 + scale) + shift


def _params(n_grid_dims):
    return pltpu.CompilerParams(dimension_semantics=("arbitrary",) * n_grid_dims,
                                vmem_limit_bytes=VMEM_LIMIT_BYTES)


def _mods_kernel(c_ref, w_ref, b_ref, o_ref):
    c = c_ref[...]
    cond = c * jax.nn.sigmoid(c)
    c_hi, c_lo = _split_bf16(cond)
    w_hi, w_lo = _split_bf16(w_ref[0])
    o_ref[0] = _dot(c_hi, w_hi) + _dot(c_hi, w_lo) + _dot(c_lo, w_hi) + b_ref[0]


def _mods(c, ada_w, ada_b):
    b, d = c.shape
    rows = BF16_SUBLANES
    n_out = ada_w.shape[-1]
    tn = n_out // 4
    c_pad = jnp.zeros((rows, d), F32).at[:b].set(c)
    out = pl.pallas_call(
        _mods_kernel,
        grid=(DEPTH, n_out // tn),
        in_specs=[pl.BlockSpec((rows, d), lambda l, j: (0, 0)),
                  pl.BlockSpec((1, d, tn), lambda l, j: (l, 0, j)),
                  pl.BlockSpec((1, 1, tn), lambda l, j: (l, 0, j))],
        out_specs=pl.BlockSpec((1, rows, tn), lambda l, j: (l, 0, j)),
        out_shape=jax.ShapeDtypeStruct((DEPTH, rows, n_out), F32),
        compiler_params=_params(2),
        name="ada_mods",
    )(c_pad, ada_w, ada_b.reshape(DEPTH, 1, n_out))
    return out[:, :b]


def _bias_tiles_kernel(tab_ref, o_ref, *, head_off, dilated):
    h = pl.program_id(0) + head_off
    d = pl.program_id(1)
    t = ATT_TILE
    key = lax.broadcasted_iota(jnp.int32, (t, t), 0)
    query = lax.broadcasted_iota(jnp.int32, (t, t), 1)
    dist = d * t + query - key
    val = jnp.full((t, t), tab_ref[h, 0], F32)
    for b in range(1, REL_BUCKETS):
        val = jnp.where(dist >= _BUCKET_LOW[b], tab_ref[h, b], val)
    if dilated:
        mult = jnp.zeros((t, t), F32)
        for window, dil in DILATED_BRANCHES:
            hit = jnp.where(dist <= window, jnp.where((dist & (dil - 1)) == 0, 1.0, 0.0), 0.0)
            mult = mult + hit
        log_mult = jnp.where(mult > 2.5, math.log(3.0), jnp.where(mult > 1.5, math.log(2.0), 0.0))
        val = jnp.where(mult > 0.5, val + log_mult, -jnp.inf)
    o_ref[0, 0] = jnp.where(dist >= 0, val, -jnp.inf)


def _bias_tiles(rel_bias, head_off, n_heads, n_tiles, dilated):
    t = ATT_TILE
    return pl.pallas_call(
        functools.partial(_bias_tiles_kernel, head_off=head_off, dilated=dilated),
        grid=(n_heads, n_tiles),
        in_specs=[pl.BlockSpec(memory_space=pltpu.SMEM)],
        out_specs=pl.BlockSpec((1, 1, t, t), lambda h, d: (h, d, 0, 0)),
        out_shape=jax.ShapeDtypeStruct((n_heads, n_tiles, t, t), F32),
        compiler_params=_params(2),
        name="dilated_bias_tiles" if dilated else "moba_bias_tiles",
    )(rel_bias)


def _store_vt_tiles(vt_ref, v):
    t = ATT_TILE
    vt = v.T.astype(BF16)
    for p in range(vt.shape[0] // LANES):
        for jj in range(vt.shape[1] // t):
            vt_ref[0, p, jj] = vt[p * LANES:(p + 1) * LANES, jj * t:(jj + 1) * t]


def _vt_spec(tm, per_seq, n_pairs):
    t = ATT_TILE
    return pl.BlockSpec((1, n_pairs, tm // t, LANES, t), lambda i: (i // per_seq, 0, i % per_seq, 0, 0))


def _vt_shape(batch, seq, n_pairs):
    return jax.ShapeDtypeStruct((batch, n_pairs, seq // ATT_TILE, LANES, ATT_TILE), BF16)


def _even_proj_kernel(x_ref, mod_ref, g_ref, w_ref, o_ref, km_ref, vta_ref, vtb_ref):
    d = D_MODEL
    mod = mod_ref[0]
    h = _prenorm(x_ref[...], g_ref[...], mod[:, d:2 * d], mod[:, 0:d])
    p = _dot(h.astype(BF16), w_ref[...])
    o_ref[...] = p.astype(BF16)
    wa = N_HEADS_A * HEAD_DIM
    ka = p[:, wa:2 * wa]
    nb = ka.shape[0] // MOBA_BLOCK
    km_ref[0] = jnp.mean(ka.reshape(nb, MOBA_BLOCK, wa), axis=1)
    _store_vt_tiles(vta_ref, p[:, 2 * wa:3 * wa])
    _store_vt_tiles(vtb_ref, p[:, 5 * wa:6 * wa])


def _even_proj(x, mod, g, w_in, seq):
    rows, d = x.shape
    tm = ROW_TILE
    n = w_in.shape[1]
    wa = N_HEADS_A * HEAD_DIM
    per_seq = seq // tm
    n_pairs = N_HEADS_A // 2
    vt_spec = _vt_spec(tm, per_seq, n_pairs)
    vt_shape = _vt_shape(rows // seq, seq, n_pairs)
    return pl.pallas_call(
        _even_proj_kernel,
        grid=(rows // tm,),
        in_specs=[pl.BlockSpec((tm, d), lambda i: (i, 0)),
                  pl.BlockSpec((1, 1, 6 * d), lambda i: (i // per_seq, 0, 0)),
                  pl.BlockSpec((1, d), lambda i: (0, 0)),
                  pl.BlockSpec((d, n), lambda i: (0, 0))],
        out_specs=[pl.BlockSpec((tm, n), lambda i: (i, 0)),
                   pl.BlockSpec((1, tm // MOBA_BLOCK, wa), lambda i: (i, 0, 0)),
                   vt_spec, vt_spec],
        out_shape=[jax.ShapeDtypeStruct((rows, n), BF16),
                   jax.ShapeDtypeStruct((rows // tm, tm // MOBA_BLOCK, wa), F32),
                   vt_shape, vt_shape],
        compiler_params=_params(1),
        name="even_proj",
    )(x, mod, g, w_in)


def _split_heads(q2):
    first = lax.broadcasted_iota(jnp.int32, q2.shape, 1) < HEAD_DIM
    zero = jnp.zeros_like(q2)
    return jnp.where(first, q2, zero), jnp.where(first, zero, q2)


def _head_rows(vt, h):
    return vt[h * HEAD_DIM:(h + 1) * HEAD_DIM]


def _store_pair_output(o_ref, out_t0, out_t1):
    o_ref[0] = jnp.concatenate([out_t0, out_t1], axis=0).T.astype(BF16)


def _softmax_first(s, vt_h, acc_ref, h):
    m = jnp.max(s, axis=0, keepdims=True)
    p = jnp.exp(s - m)
    acc_ref[h] = _dot(vt_h, p.astype(BF16))
    return m, jnp.sum(p, axis=0, keepdims=True)


def _softmax_next(s, vt_h, acc_ref, h, m_old, l_old, on=None):
    m_new = jnp.maximum(m_old, jnp.max(s, axis=0, keepdims=True))
    if on is None:
        p = jnp.exp(s - m_new)
    else:
        m_new = jnp.where(on, m_new, m_old)
        p = jnp.exp(s - jnp.where(on, m_new, jnp.inf))
    alpha = jnp.exp(m_old - m_new)
    acc_ref[h] = alpha * acc_ref[h] + _dot(vt_h, p.astype(BF16))
    return m_new, alpha * l_old + jnp.sum(p, axis=0, keepdims=True)


def _moba_select(qh, km_ref, sel_ref, qi):
    t = ATT_TILE
    km_hi, km_lo = _split_bf16(km_ref[0])
    nb = km_hi.shape[0]
    blk = lax.broadcasted_iota(jnp.int32, (nb, t), 0)
    past = blk < qi
    for h in range(2):
        gate = _dot_nt(km_hi, qh[h]) + _dot_nt(km_lo, qh[h])
        gate = jnp.where(past, gate, -jnp.inf)
        beaten = jnp.zeros((nb, t), F32)
        for other in range(nb):
            row = gate[other:other + 1, :]
            wins = jnp.where(row > gate, 1.0, jnp.where(row == gate, jnp.where(blk > other, 1.0, 0.0), 0.0))
            beaten = beaten + wins
        sel_ref[h] = jnp.where(past, jnp.where(beaten < MOBA_TOPK, 1.0, 0.0), 0.0)


def _toeplitz_attn_kernel(*refs, moba, max_dist_tiles):
    if moba:
        q_ref, k_ref, vt_ref, bias_ref, km_ref, o_ref, acc_ref, sel_ref = refs
    else:
        q_ref, k_ref, vt_ref, bias_ref, o_ref, acc_ref = refs
    t = ATT_TILE
    qi = pl.program_id(2)
    qh = _split_heads(q_ref[0])
    n_bias = bias_ref.shape[1]

    def kv_tile(j):
        start = pl.multiple_of(j * t, t)
        return k_ref[0, pl.ds(start, t), :], vt_ref[0, 0, j]

    kb, vtb = kv_tile(qi)
    stats = ()
    for h in range(2):
        s = _dot_nt(kb, qh[h]) + bias_ref[h, 0]
        stats += _softmax_first(s, _head_rows(vtb, h), acc_ref, h)

    if moba:
        _moba_select(qh, km_ref, sel_ref, qi)

    def past_tile(dt, stats):
        j = qi - dt
        kb, vtb = kv_tile(j)
        bidx = jnp.minimum(dt, n_bias - 1)
        new = ()
        for h in range(2):
            s = _dot_nt(kb, qh[h]) + bias_ref[h, bidx]
            on = sel_ref[h, pl.ds(j, 1), :] > 0.5 if moba else None
            new += _softmax_next(s, _head_rows(vtb, h), acc_ref, h, stats[2 * h], stats[2 * h + 1], on)
        return new

    n_past = qi if max_dist_tiles is None else jnp.minimum(qi, max_dist_tiles)
    stats = lax.fori_loop(1, n_past + 1, past_tile, stats)
    _store_pair_output(o_ref, acc_ref[0] / stats[1], acc_ref[1] / stats[3])


def _toeplitz_attn(proj, vt, bias, km, batch, seq, col0, moba):
    t = ATT_TILE
    n_pairs = N_HEADS_A // 2
    n_bias = bias.shape[1]
    n_tiles = seq // t
    in_specs = [pl.BlockSpec((1, t, LANES), lambda b, p, i: (b, i, col0 + p)),
                pl.BlockSpec((1, seq, LANES), lambda b, p, i: (b, 0, col0 + n_pairs + p)),
                pl.BlockSpec((1, 1, n_tiles, LANES, t), lambda b, p, i: (b, p, 0, 0, 0)),
                pl.BlockSpec((2, n_bias, t, t), lambda b, p, i: (p, 0, 0, 0))]
    args = [proj, proj, vt, bias]
    scratch = [pltpu.VMEM((2, HEAD_DIM, t), F32)]
    if moba:
        in_specs.append(pl.BlockSpec((1, seq // MOBA_BLOCK, LANES), lambda b, p, i: (b, 0, p)))
        args.append(km)
        scratch.append(pltpu.VMEM((2, seq // MOBA_BLOCK, t), F32))
    return pl.pallas_call(
        functools.partial(_toeplitz_attn_kernel, moba=moba,
                          max_dist_tiles=None if moba else DIL_BIAS_TILES - 1),
        grid=(batch, n_pairs, n_tiles),
        in_specs=in_specs,
        out_specs=pl.BlockSpec((1, t, LANES), lambda b, p, i: (b, i, p)),
        out_shape=jax.ShapeDtypeStruct((batch, seq, n_pairs * LANES), BF16),
        scratch_shapes=scratch,
        compiler_params=_params(3),
        name="moba_attn" if moba else "dilated_attn",
    )(*args)


def _odd_proj_kernel(x_ref, mod_ref, g_ref, win_ref, gq_ref, gkv_ref, wq_ref, wkv_ref,
                     cq_ref, sq_ref, ck_ref, sk_ref, qm_ref, km_ref, vtm_ref, sb_ref, vts_ref):
    d = D_MODEL
    mod = mod_ref[0]
    h = _prenorm(x_ref[...], g_ref[...], mod[:, d:2 * d], mod[:, 0:d])
    p = _dot(h.astype(BF16), win_ref[...])
    o = MLA_Q_RANK + MLA_KV_RANK
    c_q, c_kv = p[:, 0:MLA_Q_RANK], p[:, MLA_Q_RANK:o]
    k_rope, k_rope_swapped = p[:, o:o + LANES], p[:, o + LANES:o + 2 * LANES]
    sb0 = o + 2 * LANES
    n_qk = 2 * N_HEADS_D * HEAD_DIM
    sb_ref[...] = p[:, sb0:sb0 + n_qk].astype(BF16)
    _store_vt_tiles(vts_ref, p[:, sb0 + n_qk:])
    q12 = _dot(_rms(c_q, gq_ref[...]).astype(BF16), wq_ref[...])
    kv = _dot(_rms(c_kv, gkv_ref[...]).astype(BF16), wkv_ref[...])
    k_pe = k_rope * ck_ref[...] + k_rope_swapped * sk_ref[...]
    cq, sq = cq_ref[...], sq_ref[...]
    half = N_HEADS_C * LANES
    for hh in range(N_HEADS_C):
        cols = slice(hh * LANES, (hh + 1) * LANES)
        swapped = slice(half + hh * LANES, half + (hh + 1) * LANES)
        qm_ref[:, cols] = (q12[:, cols] * cq + q12[:, swapped] * sq).astype(BF16)
        km_ref[:, cols] = (kv[:, cols] + k_pe).astype(BF16)
    _store_vt_tiles(vtm_ref, kv[:, half:])


def _odd_proj(x, mod, g, w_in, gq, gkv, wq, wkv, tables, seq):
    rows, d = x.shape
    tm = ROW_TILE
    per_seq = seq // tm
    n_in = w_in.shape[1]
    n_sb = 2 * N_HEADS_D * HEAD_DIM
    n_q = N_HEADS_C * LANES
    n_pairs = N_HEADS_C // 2
    const = lambda i: (0, 0)
    table_spec = pl.BlockSpec((tm, LANES), lambda i: (i % per_seq, 0))
    vt_spec = _vt_spec(tm, per_seq, n_pairs)
    vt_shape = _vt_shape(rows // seq, seq, n_pairs)
    return pl.pallas_call(
        _odd_proj_kernel,
        grid=(rows // tm,),
        in_specs=[pl.BlockSpec((tm, d), lambda i: (i, 0)),
                  pl.BlockSpec((1, 1, 6 * d), lambda i: (i // per_seq, 0, 0)),
                  pl.BlockSpec((1, d), const),
                  pl.BlockSpec((d, n_in), const),
                  pl.BlockSpec((1, MLA_Q_RANK), const),
                  pl.BlockSpec((1, MLA_KV_RANK), const),
                  pl.BlockSpec(wq.shape, const),
                  pl.BlockSpec(wkv.shape, const),
                  table_spec, table_spec, table_spec, table_spec],
        out_specs=[pl.BlockSpec((tm, n_q), lambda i: (i, 0)),
                   pl.BlockSpec((tm, n_q), lambda i: (i, 0)),
                   vt_spec,
                   pl.BlockSpec((tm, n_sb), lambda i: (i, 0)),
                   vt_spec],
        out_shape=[jax.ShapeDtypeStruct((rows, n_q), BF16),
                   jax.ShapeDtypeStruct((rows, n_q), BF16),
                   vt_shape,
                   jax.ShapeDtypeStruct((rows, n_sb), BF16),
                   vt_shape],
        compiler_params=_params(1),
        name="odd_proj",
    )(x, mod, g, w_in, gq, gkv, wq, wkv, *tables)


def _mla_attn_kernel(q_ref, k_ref, vt_ref, o_ref, acc_ref):
    t = ATT_TILE
    qi = pl.program_id(2)
    q2 = q_ref[0]
    qh = (q2[:, 0:LANES], q2[:, LANES:2 * LANES])

    def kv_tile(j):
        start = pl.multiple_of(j * t, t)
        k2 = k_ref[0, pl.ds(start, t), :]
        return (k2[:, 0:LANES], k2[:, LANES:2 * LANES]), vt_ref[0, 0, j]

    key = lax.broadcasted_iota(jnp.int32, (t, t), 0)
    query = lax.broadcasted_iota(jnp.int32, (t, t), 1)
    kb, vtb = kv_tile(qi)
    stats = ()
    for h in range(2):
        s = jnp.where(key <= query, _dot_nt(kb[h], qh[h]), -jnp.inf)
        stats += _softmax_first(s, _head_rows(vtb, h), acc_ref, h)

    def past_tile(dt, stats):
        kb, vtb = kv_tile(qi - dt)
        new = ()
        for h in range(2):
            s = _dot_nt(kb[h], qh[h])
            new += _softmax_next(s, _head_rows(vtb, h), acc_ref, h, stats[2 * h], stats[2 * h + 1])
        return new

    stats = lax.fori_loop(1, qi + 1, past_tile, stats)
    _store_pair_output(o_ref, acc_ref[0] / stats[1], acc_ref[1] / stats[3])


def _mla_attn(qm, km, vt, batch, seq):
    t = ATT_TILE
    n_pairs = N_HEADS_C // 2
    n_tiles = seq // t
    return pl.pallas_call(
        _mla_attn_kernel,
        grid=(batch, n_pairs, n_tiles),
        in_specs=[pl.BlockSpec((1, t, 2 * LANES), lambda b, p, i: (b, i, p)),
                  pl.BlockSpec((1, seq, 2 * LANES), lambda b, p, i: (b, 0, p)),
                  pl.BlockSpec((1, 1, n_tiles, LANES, t), lambda b, p, i: (b, p, 0, 0, 0))],
        out_specs=pl.BlockSpec((1, t, LANES), lambda b, p, i: (b, i, p)),
        out_shape=jax.ShapeDtypeStruct((batch, seq, n_pairs * LANES), BF16),
        scratch_shapes=[pltpu.VMEM((2, HEAD_DIM, t), F32)],
        compiler_params=_params(3),
        name="mla_attn",
    )(qm, km, vt)


def _stick_attn_kernel(q_ref, k_ref, vt_ref, o_ref, acc_ref):
    t = ATT_TILE
    qi = pl.program_id(2)
    qh = _split_heads(q_ref[0])
    key = lax.broadcasted_iota(jnp.int32, (t, t), 0)
    other = lax.broadcasted_iota(jnp.int32, (t, t), 1)
    later = jnp.where(other > key, 1.0, 0.0).astype(BF16)
    strict = key < other

    def tile(j, diagonal, carries):
        start = pl.multiple_of(j * t, t)
        kb = k_ref[0, pl.ds(start, t), :]
        vtb = vt_ref[0, 0, j]
        new = ()
        for h in range(2):
            z = _dot_nt(kb, qh[h])
            soft = jnp.log(1.0 + jnp.exp(-jnp.abs(z)))
            log_beta = jnp.minimum(z, 0.0) - soft
            log_keep = jnp.minimum(-z, 0.0) - soft
            if diagonal:
                log_keep = jnp.where(strict, log_keep, 0.0)
            keep_hi, keep_lo = _split_bf16(log_keep)
            after = _dot(later, keep_hi) + _dot(later, keep_lo)
            if not diagonal:
                after = after + carries[h]
            a = jnp.exp(log_beta + after)
            if diagonal:
                a = jnp.where(strict, a, 0.0)
            pv = _dot(_head_rows(vtb, h), a.astype(BF16))
            col_sum = jnp.sum(log_keep, axis=0, keepdims=True)
            if diagonal:
                acc_ref[h] = pv
                new += (col_sum,)
            else:
                acc_ref[h] = acc_ref[h] + pv
                new += (carries[h] + col_sum,)
        return new

    carries = tile(qi, True, None)
    lax.fori_loop(1, qi + 1, lambda dt, carries: tile(qi - dt, False, carries), carries)
    _store_pair_output(o_ref, acc_ref[0], acc_ref[1])


def _stick_attn(sb, vt, batch, seq):
    t = ATT_TILE
    n_pairs = N_HEADS_D // 2
    n_tiles = seq // t
    return pl.pallas_call(
        _stick_attn_kernel,
        grid=(batch, n_pairs, n_tiles),
        in_specs=[pl.BlockSpec((1, t, LANES), lambda b, p, i: (b, i, p)),
                  pl.BlockSpec((1, seq, LANES), lambda b, p, i: (b, 0, n_pairs + p)),
                  pl.BlockSpec((1, 1, n_tiles, LANES, t), lambda b, p, i: (b, p, 0, 0, 0))],
        out_specs=pl.BlockSpec((1, t, LANES), lambda b, p, i: (b, i, p)),
        out_shape=jax.ShapeDtypeStruct((batch, seq, n_pairs * LANES), BF16),
        scratch_shapes=[pltpu.VMEM((2, HEAD_DIM, t), F32)],
        compiler_params=_params(3),
        name="stick_attn",
    )(sb, sb, vt)


def _mix_out_kernel(oa_ref, ob_ref, x_ref, mod_ref, g_ref, w_ref, o_ref):
    d = D_MODEL
    half = oa_ref.shape[1]
    y = _dot(oa_ref[...], w_ref[0:half, :]) + _dot(ob_ref[...], w_ref[half:2 * half, :])
    gate = mod_ref[0][:, 2 * d:3 * d]
    o_ref[...] = x_ref[...] + gate * _rms(y, g_ref[...])


def _mix_out(oa, ob, x, mod, g, w_out, seq):
    rows, d = x.shape
    tm = ROW_TILE
    per_seq = seq // tm
    half = oa.shape[1]
    return pl.pallas_call(
        _mix_out_kernel,
        grid=(rows // tm,),
        in_specs=[pl.BlockSpec((tm, half), lambda i: (i, 0)),
                  pl.BlockSpec((tm, half), lambda i: (i, 0)),
                  pl.BlockSpec((tm, d), lambda i: (i, 0)),
                  pl.BlockSpec((1, 1, 6 * d), lambda i: (i // per_seq, 0, 0)),
                  pl.BlockSpec((1, d), lambda i: (0, 0)),
                  pl.BlockSpec((2 * half, d), lambda i: (0, 0))],
        out_specs=pl.BlockSpec((tm, d), lambda i: (i, 0)),
        out_shape=jax.ShapeDtypeStruct((rows, d), F32),
        compiler_params=_params(1),
        name="mix_out",
    )(oa, ob, x, mod, g, w_out)


def _ffn_kernel(x_ref, xh_ref, mod_ref, gpre_ref, gpost_ref, wg_ref, wv_ref, cwg_ref, cwv_ref,
                cbg_ref, cbv_ref, wd_ref, o_ref, h_ref, ug_ref, uv_ref, acc_ref, *, per_seq):
    d = D_MODEL
    tm = ROW_TILE
    halo = CONV_HALO
    i = pl.program_id(0)
    ch = pl.program_id(1)
    mod = mod_ref[0]

    @pl.when(ch == 0)
    def _():
        shift, scale = mod[:, 3 * d:4 * d], mod[:, 4 * d:5 * d]
        g = gpre_ref[...]
        ahead = _prenorm(xh_ref[...], g, scale, shift)
        ahead = jnp.where(i % per_seq == 0, 0.0, ahead)
        h_ref[0:halo, :] = ahead.astype(BF16)
        h_ref[halo:halo + tm, :] = _prenorm(x_ref[...], g, scale, shift).astype(BF16)
        acc_ref[...] = jnp.zeros_like(acc_ref)

    h = h_ref[...]
    ug_ref[...] = _dot(h, wg_ref[0])
    uv_ref[...] = _dot(h, wv_ref[0])

    def conv(u_ref, cw_ref, cb_ref):
        w = cw_ref[0]
        out = w[0:1, :] * u_ref[halo - 2:halo - 2 + tm, :]
        out = out + w[1:2, :] * u_ref[halo - 1:halo - 1 + tm, :]
        out = out + w[2:3, :] * u_ref[halo:halo + tm, :]
        return out + cb_ref[0]

    gate = conv(ug_ref, cwg_ref, cbg_ref)
    val = conv(uv_ref, cwv_ref, cbv_ref)
    act = jax.nn.gelu(gate, approximate=True) * val
    acc_ref[...] += _dot(act.astype(BF16), wd_ref[0])

    @pl.when(ch == N_FF_CHUNKS - 1)
    def _():
        gate_f = mod[:, 5 * d:6 * d]
        o_ref[...] = x_ref[...] + gate_f * _rms(acc_ref[...], gpost_ref[...])


def _ffn(x, mod, g_pre, g_post, wg, wv, cwg, cwv, cbg, cbv, wd, seq):
    rows, d = x.shape
    tm = ROW_TILE
    halo = CONV_HALO
    cw = FF_CHUNK
    per_seq = seq // tm
    chunk = lambda i, c: (c, 0, 0)
    return pl.pallas_call(
        functools.partial(_ffn_kernel, per_seq=per_seq),
        grid=(rows // tm, N_FF_CHUNKS),
        in_specs=[pl.BlockSpec((tm, d), lambda i, c: (i, 0)),
                  pl.BlockSpec((halo, d), lambda i, c: (jnp.maximum(i * (tm // halo) - 1, 0), 0)),
                  pl.BlockSpec((1, 1, 6 * d), lambda i, c: (i // per_seq, 0, 0)),
                  pl.BlockSpec((1, d), lambda i, c: (0, 0)),
                  pl.BlockSpec((1, d), lambda i, c: (0, 0)),
                  pl.BlockSpec((1, d, cw), chunk),
                  pl.BlockSpec((1, d, cw), chunk),
                  pl.BlockSpec((1, CONV_WIDTH, cw), chunk),
                  pl.BlockSpec((1, CONV_WIDTH, cw), chunk),
                  pl.BlockSpec((1, 1, cw), chunk),
                  pl.BlockSpec((1, 1, cw), chunk),
                  pl.BlockSpec((1, cw, d), chunk)],
        out_specs=pl.BlockSpec((tm, d), lambda i, c: (i, 0)),
        out_shape=jax.ShapeDtypeStruct((rows, d), F32),
        scratch_shapes=[pltpu.VMEM((tm + halo, d), BF16),
                        pltpu.VMEM((tm + halo, cw), F32),
                        pltpu.VMEM((tm + halo, cw), F32),
                        pltpu.VMEM((tm, d), F32)],
        compiler_params=_params(2),
        name="conv_ffn",
    )(x, x, mod, g_pre, g_post, wg, wv, cwg, cwv, cbg, cbv, wd)


def _chunk_cols(w):
    k = w.shape[0]
    return w.reshape(k, N_FF_CHUNKS, FF_CHUNK).transpose(1, 0, 2)


def _rotate_half_cols(w):
    half = w.shape[-1] // 2
    return jnp.concatenate([-w[..., half:], w[..., :half]], axis=-1)


def _pad_cols(w, left, total):
    return jnp.pad(w, ((0, 0), (left, total - left - w.shape[1])))


def _rope_tables(seq):
    inv_freq = 1.0 / (ROPE_THETA ** (jnp.arange(0, MLA_ROPE_DIM, 2, dtype=F32) / MLA_ROPE_DIM))
    ang = jnp.arange(seq, dtype=F32)[:, None] * inv_freq[None, :]
    cos, sin = jnp.cos(ang), jnp.sin(ang)
    cos2 = _pad_cols(jnp.concatenate([cos, cos], axis=1), MLA_NOPE_DIM, LANES)
    sin2 = _pad_cols(jnp.concatenate([sin, sin], axis=1), MLA_NOPE_DIM, LANES)
    scale = (MLA_NOPE_DIM + MLA_ROPE_DIM) ** -0.5
    nope_ones = _pad_cols(jnp.ones((seq, MLA_NOPE_DIM), F32), 0, LANES)
    return (scale * (cos2 + nope_ones), scale * sin2, cos2, sin2)


def _odd_weights(w_in, w_uq, w_ukv):
    o = MLA_Q_RANK + MLA_KV_RANK
    w_rope = w_in[:, o:o + MLA_ROPE_DIM]
    scale_d = HEAD_DIM ** -0.5
    wd = N_HEADS_D * HEAD_DIM
    sb0 = o + MLA_ROPE_DIM
    w_in2 = jnp.concatenate([
        w_in[:, :o],
        _pad_cols(w_rope, MLA_NOPE_DIM, LANES),
        _pad_cols(_rotate_half_cols(w_rope), MLA_NOPE_DIM, LANES),
        w_in[:, sb0:sb0 + wd] * scale_d,
        w_in[:, sb0 + wd:],
    ], axis=1).astype(BF16)
    qd = MLA_NOPE_DIM + MLA_ROPE_DIM
    uq = w_uq.reshape(MLA_Q_RANK, N_HEADS_C, qd)
    plain = jnp.pad(uq, ((0, 0), (0, 0), (0, LANES - qd)))
    swapped = jnp.pad(_rotate_half_cols(uq[..., MLA_NOPE_DIM:]),
                      ((0, 0), (0, 0), (MLA_NOPE_DIM, LANES - qd)))
    wq = jnp.concatenate([plain.reshape(MLA_Q_RANK, -1), swapped.reshape(MLA_Q_RANK, -1)], axis=1).astype(BF16)
    ukv = w_ukv.reshape(MLA_KV_RANK, N_HEADS_C, MLA_NOPE_DIM + MLA_V_DIM)
    k_nope = jnp.pad(ukv[..., :MLA_NOPE_DIM], ((0, 0), (0, 0), (0, LANES - MLA_NOPE_DIM)))
    v = ukv[..., MLA_NOPE_DIM:]
    wkv = jnp.concatenate([k_nope.reshape(MLA_KV_RANK, -1), v.reshape(MLA_KV_RANK, -1)], axis=1).astype(BF16)
    return w_in2, wq, wkv


def _even_weights(w_in):
    wa = N_HEADS_A * HEAD_DIM
    scale = HEAD_DIM ** -0.5
    col = jnp.arange(w_in.shape[1])
    is_q = (col < wa) | ((col >= 3 * wa) & (col < 4 * wa))
    return (w_in * jnp.where(is_q, scale, 1.0)).astype(BF16)


def kernel(x, c, rel_bias, ada_w, ada_b, mix_pre_g, mix_post_g, ffn_pre_g, ffn_post_g, ab_w_in, ab_w_out,
           cd_w_in, mla_q_norm_g, mla_kv_norm_g, mla_w_uq, mla_w_ukv, cd_w_out, ffn_w_up, ffn_conv_w,
           ffn_conv_b, ffn_w_down):
    batch, seq, d = x.shape
    assert d == D_MODEL and seq % ROW_TILE == 0 and seq % ATT_TILE == 0
    rows = batch * seq
    xf = x.reshape(rows, d)

    mods = _mods(c, ada_w, ada_b)
    bias_a = _bias_tiles(rel_bias, 0, N_HEADS_A, MOBA_BIAS_TILES, dilated=False)
    bias_b = _bias_tiles(rel_bias, N_HEADS_A, N_HEADS_B, DIL_BIAS_TILES, dilated=True)
    tables = _rope_tables(seq)
    pair_blocks = N_HEADS_A // 2

    for layer in range(DEPTH):
        mod = mods[layer].reshape(batch, 1, 6 * d)
        i = layer // 2
        if layer % 2 == 0:
            proj, km, vta, vtb = _even_proj(xf, mod, mix_pre_g[layer].reshape(1, d),
                                            _even_weights(ab_w_in[i]), seq)
            proj = proj.reshape(batch, seq, -1)
            km = km.reshape(batch, seq // MOBA_BLOCK, -1)
            o_first = _toeplitz_attn(proj, vta, bias_a, km, batch, seq, 0, moba=True)
            o_second = _toeplitz_attn(proj, vtb, bias_b, None, batch, seq, 3 * pair_blocks, moba=False)
            w_out = ab_w_out[i]
        else:
            w_in2, wq, wkv = _odd_weights(cd_w_in[i], mla_w_uq[i], mla_w_ukv[i])
            qm, km, vtm, sb, vts = _odd_proj(xf, mod, mix_pre_g[layer].reshape(1, d), w_in2,
                                             mla_q_norm_g[i].reshape(1, -1), mla_kv_norm_g[i].reshape(1, -1),
                                             wq, wkv, tables, seq)
            o_first = _mla_attn(qm.reshape(batch, seq, -1), km.reshape(batch, seq, -1), vtm, batch, seq)
            o_second = _stick_attn(sb.reshape(batch, seq, -1), vts, batch, seq)
            w_out = cd_w_out[i]
        xf = _mix_out(o_first.reshape(rows, -1), o_second.reshape(rows, -1), xf, mod,
                      mix_post_g[layer].reshape(1, d), w_out.astype(BF16), seq)
        w_up = ffn_w_up[layer].astype(BF16)
        conv_w, conv_b = ffn_conv_w[layer], ffn_conv_b[layer].reshape(1, -1)
        xf = _ffn(xf, mod, ffn_pre_g[layer].reshape(1, d), ffn_post_g[layer].reshape(1, d),
                  _chunk_cols(w_up[:, :D_FF]), _chunk_cols(w_up[:, D_FF:]),
                  _chunk_cols(conv_w[:, :D_FF]), _chunk_cols(conv_w[:, D_FF:]),
                  _chunk_cols(conv_b[:, :D_FF]), _chunk_cols(conv_b[:, D_FF:]),
                  ffn_w_down[layer].astype(BF16).reshape(N_FF_CHUNKS, FF_CHUNK, d), seq)
    return xf.reshape(batch, seq, d)
```

```python
import functools
import math

import jax
import jax.numpy as jnp
from jax import lax
from jax.experimental import pallas as pl
from jax.experimental.pallas import tpu as pltpu

F32 = jnp.float32
BF16 = jnp.bfloat16

D_MODEL = 1024
DEPTH = 4
HEAD_DIM = 64
N_HEADS_A = 8
N_HEADS_B = 8
N_HEADS_C = 8
N_HEADS_D = 8
MOBA_BLOCK = 256
MOBA_TOPK = 3
DILATED_BRANCHES = ((128, 1), (512, 4), (2048, 16))
MLA_Q_RANK = 256
MLA_KV_RANK = 256
MLA_NOPE_DIM = 64
MLA_ROPE_DIM = 32
MLA_V_DIM = 64
ROPE_THETA = 10000.0
REL_BUCKETS = 32
REL_MAX_DIST = 2048
D_FF = 2816
CONV_WIDTH = 3
NORM_EPS = 1e-6

LANES = 128
BF16_SUBLANES = 16
VMEM_LIMIT_BYTES = 56 * 1024 * 1024

ATT_TILE = MOBA_BLOCK
KEY_GROUP = 4
GROUP_KEYS = KEY_GROUP * ATT_TILE
ROW_TILE = 512
FF_CHUNK = 256
N_FF_CHUNKS = D_FF // FF_CHUNK
CONV_HALO = BF16_SUBLANES
MOBA_BIAS_TILES = REL_MAX_DIST // ATT_TILE + 2
DIL_MAX_TILE_DIST = DILATED_BRANCHES[-1][0] // ATT_TILE
DIL_BIAS_TILES = DIL_MAX_TILE_DIST + 1

_NT = (((1,), (1,)), ((), ()))


def _bucket_lower_bounds():
    max_exact = REL_BUCKETS // 2
    ratio = REL_MAX_DIST // max_exact
    n_log = REL_BUCKETS - max_exact
    lows = list(range(max_exact + 1))
    for k in range(1, n_log):
        d = lows[-1]
        while d ** n_log < (max_exact ** n_log) * (ratio ** k):
            d += 1
        lows.append(d)
    return lows


_BUCKET_LOW = _bucket_lower_bounds()


def _dot(a, b):
    return jnp.dot(a, b, preferred_element_type=F32)


def _dot_nt(a, b):
    return lax.dot_general(a, b, _NT, preferred_element_type=F32)


def _split_bf16(x):
    hi = x.astype(BF16)
    lo = (x - hi.astype(F32)).astype(BF16)
    return hi, lo


def _rms(x, g):
    return (x * lax.rsqrt(jnp.mean(x * x, axis=-1, keepdims=True) + NORM_EPS)) * g


def _prenorm(x, g, scale, shift):
    return _rms(x, g) * (1.0 + scale) + shift


def _params(n_grid_dims):
    return pltpu.CompilerParams(dimension_semantics=("arbitrary",) * n_grid_dims,
                                vmem_limit_bytes=VMEM_LIMIT_BYTES)


def _mods_kernel(c_ref, w_ref, b_ref, o_ref):
    c = c_ref[...]
    cond = c * jax.nn.sigmoid(c)
    c_hi, c_lo = _split_bf16(cond)
    w_hi, w_lo = _split_bf16(w_ref[0])
    o_ref[0] = _dot(c_hi, w_hi) + _dot(c_hi, w_lo) + _dot(c_lo, w_hi) + b_ref[0]


def _mods(c, ada_w, ada_b):
    b, d = c.shape
    rows = BF16_SUBLANES
    n_out = ada_w.shape[-1]
    tn = n_out // 4
    c_pad = jnp.zeros((rows, d), F32).at[:b].set(c)
    out = pl.pallas_call(
        _mods_kernel,
        grid=(DEPTH, n_out // tn),
        in_specs=[pl.BlockSpec((rows, d), lambda l, j: (0, 0)),
                  pl.BlockSpec((1, d, tn), lambda l, j: (l, 0, j)),
                  pl.BlockSpec((1, 1, tn), lambda l, j: (l, 0, j))],
        out_specs=pl.BlockSpec((1, rows, tn), lambda l, j: (l, 0, j)),
        out_shape=jax.ShapeDtypeStruct((DEPTH, rows, n_out), F32),
        compiler_params=_params(2),
        name="ada_mods",
    )(c_pad, ada_w, ada_b.reshape(DEPTH, 1, n_out))
    return out[:, :b]


def _bias_tiles_kernel(tab_ref, o_ref, *, head_off, dilated):
    h = pl.program_id(0) + head_off
    d = pl.program_id(1)
    t = ATT_TILE
    key = lax.broadcasted_iota(jnp.int32, (t, t), 0)
    query = lax.broadcasted_iota(jnp.int32, (t, t), 1)
    dist = d * t + query - key
    val = jnp.full((t, t), tab_ref[h, 0], F32)
    for b in range(1, REL_BUCKETS):
        val = jnp.where(dist >= _BUCKET_LOW[b], tab_ref[h, b], val)
    if dilated:
        mult = jnp.zeros((t, t), F32)
        for window, dil in DILATED_BRANCHES:
            hit = jnp.where(dist <= window, jnp.where((dist & (dil - 1)) == 0, 1.0, 0.0), 0.0)
            mult = mult + hit
        log_mult = jnp.where(mult > 2.5, math.log(3.0), jnp.where(mult > 1.5, math.log(2.0), 0.0))
        val = jnp.where(mult > 0.5, val + log_mult, -jnp.inf)
    o_ref[0, 0] = jnp.where(dist >= 0, val, -jnp.inf)


def _bias_tiles(rel_bias, head_off, n_heads, n_tiles, dilated):
    t = ATT_TILE
    return pl.pallas_call(
        functools.partial(_bias_tiles_kernel, head_off=head_off, dilated=dilated),
        grid=(n_heads, n_tiles),
        in_specs=[pl.BlockSpec(memory_space=pltpu.SMEM)],
        out_specs=pl.BlockSpec((1, 1, t, t), lambda h, d: (h, d, 0, 0)),
        out_shape=jax.ShapeDtypeStruct((n_heads, n_tiles, t, t), F32),
        compiler_params=_params(2),
        name="dilated_bias_tiles" if dilated else "moba_bias_tiles",
    )(rel_bias)


def _store_vt(vt_ref, v):
    vt = v.T.astype(BF16)
    for p in range(vt.shape[0] // LANES):
        vt_ref[0, p, 0] = vt[p * LANES:(p + 1) * LANES, :]


def _vt_spec(tm, per_seq, n_pairs):
    per_group = GROUP_KEYS // tm
    return pl.BlockSpec((1, n_pairs, 1, LANES, tm),
                        lambda i: (i // per_seq, 0, (i % per_seq) // per_group, 0, (i % per_seq) % per_group))


def _vt_shape(batch, seq, n_pairs):
    return jax.ShapeDtypeStruct((batch, n_pairs, seq // GROUP_KEYS, LANES, GROUP_KEYS), BF16)


def _even_proj_kernel(x_ref, mod_ref, g_ref, w_ref, o_ref, km_ref, vta_ref, vtb_ref):
    d = D_MODEL
    mod = mod_ref[0]
    h = _prenorm(x_ref[...], g_ref[...], mod[:, d:2 * d], mod[:, 0:d])
    p = _dot(h.astype(BF16), w_ref[...])
    o_ref[...] = p.astype(BF16)
    wa = N_HEADS_A * HEAD_DIM
    ka = p[:, wa:2 * wa]
    nb = ka.shape[0] // MOBA_BLOCK
    km_ref[0] = jnp.mean(ka.reshape(nb, MOBA_BLOCK, wa), axis=1)
    _store_vt(vta_ref, p[:, 2 * wa:3 * wa])
    _store_vt(vtb_ref, p[:, 5 * wa:6 * wa])


def _even_proj(x, mod, g, w_in, seq):
    rows, d = x.shape
    tm = ROW_TILE
    n = w_in.shape[1]
    wa = N_HEADS_A * HEAD_DIM
    per_seq = seq // tm
    n_pairs = N_HEADS_A // 2
    vt_spec = _vt_spec(tm, per_seq, n_pairs)
    vt_shape = _vt_shape(rows // seq, seq, n_pairs)
    return pl.pallas_call(
        _even_proj_kernel,
        grid=(rows // tm,),
        in_specs=[pl.BlockSpec((tm, d), lambda i: (i, 0)),
                  pl.BlockSpec((1, 1, 6 * d), lambda i: (i // per_seq, 0, 0)),
                  pl.BlockSpec((1, d), lambda i: (0, 0)),
                  pl.BlockSpec((d, n), lambda i: (0, 0))],
        out_specs=[pl.BlockSpec((tm, n), lambda i: (i, 0)),
                   pl.BlockSpec((1, tm // MOBA_BLOCK, wa), lambda i: (i, 0, 0)),
                   vt_spec, vt_spec],
        out_shape=[jax.ShapeDtypeStruct((rows, n), BF16),
                   jax.ShapeDtypeStruct((rows // tm, tm // MOBA_BLOCK, wa), F32),
                   vt_shape, vt_shape],
        compiler_params=_params(1),
        name="even_proj",
    )(x, mod, g, w_in)


def _split_heads(q2):
    first = lax.broadcasted_iota(jnp.int32, q2.shape, 1) < HEAD_DIM
    zero = jnp.zeros_like(q2)
    return jnp.where(first, q2, zero), jnp.where(first, zero, q2)


def _head_rows(vt, h):
    return vt[h * HEAD_DIM:(h + 1) * HEAD_DIM]


def _key_tiles(s):
    t = ATT_TILE
    return [s[c * t:(c + 1) * t] for c in range(KEY_GROUP)]


def _group_start(g):
    return pl.multiple_of(g * GROUP_KEYS, GROUP_KEYS)


def _store_pair_output(o_ref, out_t0, out_t1):
    o_ref[0] = jnp.concatenate([out_t0, out_t1], axis=0).T.astype(BF16)


def _softmax_group(s_tiles, ons, vt_h, acc_ref, h, stats, first):
    maxes = []
    for s, on in zip(s_tiles, ons):
        mx = jnp.max(s, axis=0, keepdims=True)
        maxes.append(mx if on is None else jnp.where(on, mx, -jnp.inf))
    m_new = functools.reduce(jnp.maximum, maxes)
    if not first:
        m_old, l_old = stats
        m_new = jnp.maximum(m_new, m_old)
    ps = []
    l_add = None
    for s, on in zip(s_tiles, ons):
        p = jnp.exp(s - (m_new if on is None else jnp.where(on, m_new, jnp.inf)))
        p_sum = jnp.sum(p, axis=0, keepdims=True)
        l_add = p_sum if l_add is None else l_add + p_sum
        ps.append(p.astype(BF16))
    pv = _dot(vt_h, jnp.concatenate(ps, axis=0))
    if first:
        acc_ref[h] = pv
        return m_new, l_add
    alpha = jnp.exp(m_old - m_new)
    acc_ref[h] = alpha * acc_ref[h] + pv
    return m_new, alpha * l_old + l_add


def _moba_select(qh, km_ref, sel_ref, qi):
    t = ATT_TILE
    km_hi, km_lo = _split_bf16(km_ref[0])
    nb = km_hi.shape[0]
    blk = lax.broadcasted_iota(jnp.int32, (nb, t), 0)
    past = blk < qi
    for h in range(2):
        gate = _dot_nt(km_hi, qh[h]) + _dot_nt(km_lo, qh[h])
        gate = jnp.where(past, gate, -jnp.inf)
        beaten = jnp.zeros((nb, t), F32)
        for other in range(nb):
            row = gate[other:other + 1, :]
            wins = jnp.where(row > gate, 1.0, jnp.where(row == gate, jnp.where(blk > other, 1.0, 0.0), 0.0))
            beaten = beaten + wins
        sel_ref[h] = jnp.where(past, jnp.where(beaten < MOBA_TOPK, 1.0, 0.0), 0.0)


def _toeplitz_attn_kernel(*refs, moba):
    if moba:
        q_ref, k_ref, vt_ref, bias_ref, km_ref, o_ref, acc_ref, sel_ref = refs
    else:
        q_ref, k_ref, vt_ref, bias_ref, o_ref, acc_ref = refs
    qi = pl.program_id(2)
    qh = _split_heads(q_ref[0])
    n_bias = bias_ref.shape[1]
    if moba:
        _moba_select(qh, km_ref, sel_ref, qi)

    def group(g, stats, first):
        kg = k_ref[0, pl.ds(_group_start(g), GROUP_KEYS), :]
        vtg = vt_ref[0, 0, g]
        new = ()
        scores = [_dot_nt(kg, qh[h]) for h in range(2)]
        for h in range(2):
            s_tiles = _key_tiles(scores[h])
            ons = []
            for c in range(KEY_GROUP):
                j = g * KEY_GROUP + c
                dt = qi - j
                s_tiles[c] = s_tiles[c] + bias_ref[h, jnp.clip(dt, 0, n_bias - 1)]
                if moba:
                    chosen = sel_ref[h, pl.ds(j, 1), :] > 0.5
                    ons.append(jnp.logical_or(chosen, dt == 0) if first else chosen)
                else:
                    ons.append(jnp.logical_and(dt >= 0, dt <= DIL_MAX_TILE_DIST))
            new += _softmax_group(s_tiles, ons, _head_rows(vtg, h), acc_ref, h,
                                  None if first else stats[2 * h:2 * h + 2], first)
        return new

    g_own = qi // KEY_GROUP
    g_last = 0 if moba else jnp.maximum(qi - DIL_MAX_TILE_DIST, 0) // KEY_GROUP
    stats = group(g_own, None, True)
    stats = lax.fori_loop(0, g_own - g_last, lambda n, st: group(g_own - 1 - n, st, False), stats)
    _store_pair_output(o_ref, acc_ref[0] / stats[1], acc_ref[1] / stats[3])


def _toeplitz_attn(proj, vt, bias, km, batch, seq, col0, moba):
    t = ATT_TILE
    n_pairs = N_HEADS_A // 2
    n_bias = bias.shape[1]
    in_specs = [pl.BlockSpec((1, t, LANES), lambda b, p, i: (b, i, col0 + p)),
                pl.BlockSpec((1, seq, LANES), lambda b, p, i: (b, 0, col0 + n_pairs + p)),
                pl.BlockSpec((1, 1) + vt.shape[2:], lambda b, p, i: (b, p, 0, 0, 0)),
                pl.BlockSpec((2, n_bias, t, t), lambda b, p, i: (p, 0, 0, 0))]
    args = [proj, proj, vt, bias]
    scratch = [pltpu.VMEM((2, HEAD_DIM, t), F32)]
    if moba:
        in_specs.append(pl.BlockSpec((1, seq // MOBA_BLOCK, LANES), lambda b, p, i: (b, 0, p)))
        args.append(km)
        scratch.append(pltpu.VMEM((2, seq // MOBA_BLOCK, t), F32))
    return pl.pallas_call(
        functools.partial(_toeplitz_attn_kernel, moba=moba),
        grid=(batch, n_pairs, seq // t),
        in_specs=in_specs,
        out_specs=pl.BlockSpec((1, t, LANES), lambda b, p, i: (b, i, p)),
        out_shape=jax.ShapeDtypeStruct((batch, seq, n_pairs * LANES), BF16),
        scratch_shapes=scratch,
        compiler_params=_params(3),
        name="moba_attn" if moba else "dilated_attn",
    )(*args)


def _odd_proj_kernel(x_ref, mod_ref, g_ref, win_ref, gq_ref, gkv_ref, wq_ref, wkv_ref,
                     cq_ref, sq_ref, ck_ref, sk_ref, qm_ref, km_ref, vtm_ref, sb_ref, vts_ref):
    d = D_MODEL
    mod = mod_ref[0]
    h = _prenorm(x_ref[...], g_ref[...], mod[:, d:2 * d], mod[:, 0:d])
    p = _dot(h.astype(BF16), win_ref[...])
    o = MLA_Q_RANK + MLA_KV_RANK
    c_q, c_kv = p[:, 0:MLA_Q_RANK], p[:, MLA_Q_RANK:o]
    k_rope, k_rope_swapped = p[:, o:o + LANES], p[:, o + LANES:o + 2 * LANES]
    sb0 = o + 2 * LANES
    n_qk = 2 * N_HEADS_D * HEAD_DIM
    sb_ref[...] = p[:, sb0:sb0 + n_qk].astype(BF16)
    _store_vt(vts_ref, p[:, sb0 + n_qk:])
    q12 = _dot(_rms(c_q, gq_ref[...]).astype(BF16), wq_ref[...])
    kv = _dot(_rms(c_kv, gkv_ref[...]).astype(BF16), wkv_ref[...])
    k_pe = k_rope * ck_ref[...] + k_rope_swapped * sk_ref[...]
    cq, sq = cq_ref[...], sq_ref[...]
    half = N_HEADS_C * LANES
    for hh in range(N_HEADS_C):
        cols = slice(hh * LANES, (hh + 1) * LANES)
        swapped = slice(half + hh * LANES, half + (hh + 1) * LANES)
        qm_ref[:, cols] = (q12[:, cols] * cq + q12[:, swapped] * sq).astype(BF16)
        km_ref[:, cols] = (kv[:, cols] + k_pe).astype(BF16)
    _store_vt(vtm_ref, kv[:, half:])


def _odd_proj(x, mod, g, w_in, gq, gkv, wq, wkv, tables, seq):
    rows, d = x.shape
    tm = ROW_TILE
    per_seq = seq // tm
    n_in = w_in.shape[1]
    n_sb = 2 * N_HEADS_D * HEAD_DIM
    n_q = N_HEADS_C * LANES
    n_pairs = N_HEADS_C // 2
    const = lambda i: (0, 0)
    table_spec = pl.BlockSpec((tm, LANES), lambda i: (i % per_seq, 0))
    vt_spec = _vt_spec(tm, per_seq, n_pairs)
    vt_shape = _vt_shape(rows // seq, seq, n_pairs)
    return pl.pallas_call(
        _odd_proj_kernel,
        grid=(rows // tm,),
        in_specs=[pl.BlockSpec((tm, d), lambda i: (i, 0)),
                  pl.BlockSpec((1, 1, 6 * d), lambda i: (i // per_seq, 0, 0)),
                  pl.BlockSpec((1, d), const),
                  pl.BlockSpec((d, n_in), const),
                  pl.BlockSpec((1, MLA_Q_RANK), const),
                  pl.BlockSpec((1, MLA_KV_RANK), const),
                  pl.BlockSpec(wq.shape, const),
                  pl.BlockSpec(wkv.shape, const),
                  table_spec, table_spec, table_spec, table_spec],
        out_specs=[pl.BlockSpec((tm, n_q), lambda i: (i, 0)),
                   pl.BlockSpec((tm, n_q), lambda i: (i, 0)),
                   vt_spec,
                   pl.BlockSpec((tm, n_sb), lambda i: (i, 0)),
                   vt_spec],
        out_shape=[jax.ShapeDtypeStruct((rows, n_q), BF16),
                   jax.ShapeDtypeStruct((rows, n_q), BF16),
                   vt_shape,
                   jax.ShapeDtypeStruct((rows, n_sb), BF16),
                   vt_shape],
        compiler_params=_params(1),
        name="odd_proj",
    )(x, mod, g, w_in, gq, gkv, wq, wkv, *tables)


def _causal_masks(qi, g, strict):
    t = ATT_TILE
    key = lax.broadcasted_iota(jnp.int32, (GROUP_KEYS, t), 0) + (g * KEY_GROUP - qi) * t
    query = lax.broadcasted_iota(jnp.int32, (GROUP_KEYS, t), 1)
    return key < query if strict else key <= query


def _mla_attn_kernel(q_ref, k_ref, vt_ref, o_ref, acc_ref):
    qi = pl.program_id(2)
    q2 = q_ref[0]
    qh = (q2[:, 0:LANES], q2[:, LANES:2 * LANES])
    no_switch = [None] * KEY_GROUP

    def group(g, stats, first):
        k2 = k_ref[0, pl.ds(_group_start(g), GROUP_KEYS), :]
        vtg = vt_ref[0, 0, g]
        new = ()
        scores = [_dot_nt(k2[:, h * LANES:(h + 1) * LANES], qh[h]) for h in range(2)]
        for h in range(2):
            s = scores[h]
            if first:
                s = jnp.where(_causal_masks(qi, g, strict=False), s, -jnp.inf)
            new += _softmax_group(_key_tiles(s), no_switch, _head_rows(vtg, h), acc_ref, h,
                                  None if first else stats[2 * h:2 * h + 2], first)
        return new

    g_own = qi // KEY_GROUP
    stats = group(g_own, None, True)
    stats = lax.fori_loop(0, g_own, lambda n, st: group(g_own - 1 - n, st, False), stats)
    _store_pair_output(o_ref, acc_ref[0] / stats[1], acc_ref[1] / stats[3])


def _mla_attn(qm, km, vt, batch, seq):
    t = ATT_TILE
    n_pairs = N_HEADS_C // 2
    return pl.pallas_call(
        _mla_attn_kernel,
        grid=(batch, n_pairs, seq // t),
        in_specs=[pl.BlockSpec((1, t, 2 * LANES), lambda b, p, i: (b, i, p)),
                  pl.BlockSpec((1, seq, 2 * LANES), lambda b, p, i: (b, 0, p)),
                  pl.BlockSpec((1, 1) + vt.shape[2:], lambda b, p, i: (b, p, 0, 0, 0))],
        out_specs=pl.BlockSpec((1, t, LANES), lambda b, p, i: (b, i, p)),
        out_shape=jax.ShapeDtypeStruct((batch, seq, n_pairs * LANES), BF16),
        scratch_shapes=[pltpu.VMEM((2, HEAD_DIM, t), F32)],
        compiler_params=_params(3),
        name="mla_attn",
    )(qm, km, vt)


def _stick_attn_kernel(q_ref, k_ref, vt_ref, o_ref, acc_ref):
    t = ATT_TILE
    qi = pl.program_id(2)
    qh = _split_heads(q_ref[0])
    key = lax.broadcasted_iota(jnp.int32, (t, t), 0)
    other = lax.broadcasted_iota(jnp.int32, (t, t), 1)
    later = jnp.where(other > key, 1.0, 0.0).astype(BF16)

    def group(g, carries, first):
        kg = k_ref[0, pl.ds(_group_start(g), GROUP_KEYS), :]
        vtg = vt_ref[0, 0, g]
        new = ()
        scores = [_dot_nt(kg, qh[h]) for h in range(2)]
        for h in range(2):
            z = scores[h]
            soft = jnp.log(1.0 + jnp.exp(-jnp.abs(z)))
            log_beta = jnp.minimum(z, 0.0) - soft
            log_keep = jnp.minimum(-z, 0.0) - soft
            if first:
                strict = _causal_masks(qi, g, strict=True)
                log_keep = jnp.where(strict, log_keep, 0.0)
            beta_tiles, keep_tiles = _key_tiles(log_beta), _key_tiles(log_keep)
            run = None if first else carries[h]
            a_tiles = [None] * KEY_GROUP
            for c in reversed(range(KEY_GROUP)):
                keep_hi, keep_lo = _split_bf16(keep_tiles[c])
                after = _dot(later, keep_hi) + _dot(later, keep_lo)
                if run is not None:
                    after = after + run
                a_tiles[c] = jnp.exp(beta_tiles[c] + after)
                col_sum = jnp.sum(keep_tiles[c], axis=0, keepdims=True)
                run = col_sum if run is None else run + col_sum
            a = jnp.concatenate(a_tiles, axis=0)
            if first:
                a = jnp.where(strict, a, 0.0)
            pv = _dot(_head_rows(vtg, h), a.astype(BF16))
            acc_ref[h] = pv if first else acc_ref[h] + pv
            new += (run,)
        return new

    g_own = qi // KEY_GROUP
    carries = group(g_own, None, True)
    lax.fori_loop(0, g_own, lambda n, cr: group(g_own - 1 - n, cr, False), carries)
    _store_pair_output(o_ref, acc_ref[0], acc_ref[1])


def _stick_attn(sb, vt, batch, seq):
    t = ATT_TILE
    n_pairs = N_HEADS_D // 2
    return pl.pallas_call(
        _stick_attn_kernel,
        grid=(batch, n_pairs, seq // t),
        in_specs=[pl.BlockSpec((1, t, LANES), lambda b, p, i: (b, i, p)),
                  pl.BlockSpec((1, seq, LANES), lambda b, p, i: (b, 0, n_pairs + p)),
                  pl.BlockSpec((1, 1) + vt.shape[2:], lambda b, p, i: (b, p, 0, 0, 0))],
        out_specs=pl.BlockSpec((1, t, LANES), lambda b, p, i: (b, i, p)),
        out_shape=jax.ShapeDtypeStruct((batch, seq, n_pairs * LANES), BF16),
        scratch_shapes=[pltpu.VMEM((2, HEAD_DIM, t), F32)],
        compiler_params=_params(3),
        name="stick_attn",
    )(sb, sb, vt)


def _mix_out_kernel(oa_ref, ob_ref, x_ref, mod_ref, g_ref, w_ref, o_ref):
    d = D_MODEL
    half = oa_ref.shape[1]
    y = _dot(oa_ref[...], w_ref[0:half, :]) + _dot(ob_ref[...], w_ref[half:2 * half, :])
    gate = mod_ref[0][:, 2 * d:3 * d]
    o_ref[...] = x_ref[...] + gate * _rms(y, g_ref[...])


def _mix_out(oa, ob, x, mod, g, w_out, seq):
    rows, d = x.shape
    tm = ROW_TILE
    per_seq = seq // tm
    half = oa.shape[1]
    return pl.pallas_call(
        _mix_out_kernel,
        grid=(rows // tm,),
        in_specs=[pl.BlockSpec((tm, half), lambda i: (i, 0)),
                  pl.BlockSpec((tm, half), lambda i: (i, 0)),
                  pl.BlockSpec((tm, d), lambda i: (i, 0)),
                  pl.BlockSpec((1, 1, 6 * d), lambda i: (i // per_seq, 0, 0)),
                  pl.BlockSpec((1, d), lambda i: (0, 0)),
                  pl.BlockSpec((2 * half, d), lambda i: (0, 0))],
        out_specs=pl.BlockSpec((tm, d), lambda i: (i, 0)),
        out_shape=jax.ShapeDtypeStruct((rows, d), F32),
        compiler_params=_params(1),
        name="mix_out",
    )(oa, ob, x, mod, g, w_out)


def _ffn_kernel(x_ref, xh_ref, mod_ref, gpre_ref, gpost_ref, wg_ref, wv_ref, cwg_ref, cwv_ref,
                cbg_ref, cbv_ref, wd_ref, o_ref, h_ref, ug_ref, uv_ref, acc_ref, *, per_seq):
    d = D_MODEL
    tm = ROW_TILE
    halo = CONV_HALO
    i = pl.program_id(0)
    ch = pl.program_id(1)
    mod = mod_ref[0]

    @pl.when(ch == 0)
    def _():
        shift, scale = mod[:, 3 * d:4 * d], mod[:, 4 * d:5 * d]
        g = gpre_ref[...]
        ahead = _prenorm(xh_ref[...], g, scale, shift)
        ahead = jnp.where(i % per_seq == 0, 0.0, ahead)
        h_ref[0:halo, :] = ahead.astype(BF16)
        h_ref[halo:halo + tm, :] = _prenorm(x_ref[...], g, scale, shift).astype(BF16)
        acc_ref[...] = jnp.zeros_like(acc_ref)

    h = h_ref[...]
    ug_ref[...] = _dot(h, wg_ref[0])
    uv_ref[...] = _dot(h, wv_ref[0])

    def conv(u_ref, cw_ref, cb_ref):
        w = cw_ref[0]
        out = w[0:1, :] * u_ref[halo - 2:halo - 2 + tm, :]
        out = out + w[1:2, :] * u_ref[halo - 1:halo - 1 + tm, :]
        out = out + w[2:3, :] * u_ref[halo:halo + tm, :]
        return out + cb_ref[0]

    gate = conv(ug_ref, cwg_ref, cbg_ref)
    val = conv(uv_ref, cwv_ref, cbv_ref)
    act = jax.nn.gelu(gate, approximate=True) * val
    acc_ref[...] += _dot(act.astype(BF16), wd_ref[0])

    @pl.when(ch == N_FF_CHUNKS - 1)
    def _():
        gate_f = mod[:, 5 * d:6 * d]
        o_ref[...] = x_ref[...] + gate_f * _rms(acc_ref[...], gpost_ref[...])


def _ffn(x, mod, g_pre, g_post, wg, wv, cwg, cwv, cbg, cbv, wd, seq):
    rows, d = x.shape
    tm = ROW_TILE
    halo = CONV_HALO
    cw = FF_CHUNK
    per_seq = seq // tm
    chunk = lambda i, c: (c, 0, 0)
    return pl.pallas_call(
        functools.partial(_ffn_kernel, per_seq=per_seq),
        grid=(rows // tm, N_FF_CHUNKS),
        in_specs=[pl.BlockSpec((tm, d), lambda i, c: (i, 0)),
                  pl.BlockSpec((halo, d), lambda i, c: (jnp.maximum(i * (tm // halo) - 1, 0), 0)),
                  pl.BlockSpec((1, 1, 6 * d), lambda i, c: (i // per_seq, 0, 0)),
                  pl.BlockSpec((1, d), lambda i, c: (0, 0)),
                  pl.BlockSpec((1, d), lambda i, c: (0, 0)),
                  pl.BlockSpec((1, d, cw), chunk),
                  pl.BlockSpec((1, d, cw), chunk),
                  pl.BlockSpec((1, CONV_WIDTH, cw), chunk),
                  pl.BlockSpec((1, CONV_WIDTH, cw), chunk),
                  pl.BlockSpec((1, 1, cw), chunk),
                  pl.BlockSpec((1, 1, cw), chunk),
                  pl.BlockSpec((1, cw, d), chunk)],
        out_specs=pl.BlockSpec((tm, d), lambda i, c: (i, 0)),
        out_shape=jax.ShapeDtypeStruct((rows, d), F32),
        scratch_shapes=[pltpu.VMEM((tm + halo, d), BF16),
                        pltpu.VMEM((tm + halo, cw), F32),
                        pltpu.VMEM((tm + halo, cw), F32),
                        pltpu.VMEM((tm, d), F32)],
        compiler_params=_params(2),
        name="conv_ffn",
    )(x, x, mod, g_pre, g_post, wg, wv, cwg, cwv, cbg, cbv, wd)


def _chunk_cols(w):
    k = w.shape[0]
    return w.reshape(k, N_FF_CHUNKS, FF_CHUNK).transpose(1, 0, 2)


def _rotate_half_cols(w):
    half = w.shape[-1] // 2
    return jnp.concatenate([-w[..., half:], w[..., :half]], axis=-1)


def _pad_cols(w, left, total):
    return jnp.pad(w, ((0, 0), (left, total - left - w.shape[1])))


def _rope_tables(seq):
    inv_freq = 1.0 / (ROPE_THETA ** (jnp.arange(0, MLA_ROPE_DIM, 2, dtype=F32) / MLA_ROPE_DIM))
    ang = jnp.arange(seq, dtype=F32)[:, None] * inv_freq[None, :]
    cos, sin = jnp.cos(ang), jnp.sin(ang)
    cos2 = _pad_cols(jnp.concatenate([cos, cos], axis=1), MLA_NOPE_DIM, LANES)
    sin2 = _pad_cols(jnp.concatenate([sin, sin], axis=1), MLA_NOPE_DIM, LANES)
    scale = (MLA_NOPE_DIM + MLA_ROPE_DIM) ** -0.5
    nope_ones = _pad_cols(jnp.ones((seq, MLA_NOPE_DIM), F32), 0, LANES)
    return (scale * (cos2 + nope_ones), scale * sin2, cos2, sin2)


def _odd_weights(w_in, w_uq, w_ukv):
    o = MLA_Q_RANK + MLA_KV_RANK
    w_rope = w_in[:, o:o + MLA_ROPE_DIM]
    scale_d = HEAD_DIM ** -0.5
    wd = N_HEADS_D * HEAD_DIM
    sb0 = o + MLA_ROPE_DIM
    w_in2 = jnp.concatenate([
        w_in[:, :o],
        _pad_cols(w_rope, MLA_NOPE_DIM, LANES),
        _pad_cols(_rotate_half_cols(w_rope), MLA_NOPE_DIM, LANES),
        w_in[:, sb0:sb0 + wd] * scale_d,
        w_in[:, sb0 + wd:],
    ], axis=1).astype(BF16)
    qd = MLA_NOPE_DIM + MLA_ROPE_DIM
    uq = w_uq.reshape(MLA_Q_RANK, N_HEADS_C, qd)
    plain = jnp.pad(uq, ((0, 0), (0, 0), (0, LANES - qd)))
    swapped = jnp.pad(_rotate_half_cols(uq[..., MLA_NOPE_DIM:]),
                      ((0, 0), (0, 0), (MLA_NOPE_DIM, LANES - qd)))
    wq = jnp.concatenate([plain.reshape(MLA_Q_RANK, -1), swapped.reshape(MLA_Q_RANK, -1)], axis=1).astype(BF16)
    ukv = w_ukv.reshape(MLA_KV_RANK, N_HEADS_C, MLA_NOPE_DIM + MLA_V_DIM)
    k_nope = jnp.pad(ukv[..., :MLA_NOPE_DIM], ((0, 0), (0, 0), (0, LANES - MLA_NOPE_DIM)))
    v = ukv[..., MLA_NOPE_DIM:]
    wkv = jnp.concatenate([k_nope.reshape(MLA_KV_RANK, -1), v.reshape(MLA_KV_RANK, -1)], axis=1).astype(BF16)
    return w_in2, wq, wkv


def _even_weights(w_in):
    wa = N_HEADS_A * HEAD_DIM
    scale = HEAD_DIM ** -0.5
    col = jnp.arange(w_in.shape[1])
    is_q = (col < wa) | ((col >= 3 * wa) & (col < 4 * wa))
    return (w_in * jnp.where(is_q, scale, 1.0)).astype(BF16)


def kernel(x, c, rel_bias, ada_w, ada_b, mix_pre_g, mix_post_g, ffn_pre_g, ffn_post_g, ab_w_in, ab_w_out,
           cd_w_in, mla_q_norm_g, mla_kv_norm_g, mla_w_uq, mla_w_ukv, cd_w_out, ffn_w_up, ffn_conv_w,
           ffn_conv_b, ffn_w_down):
    batch, seq, d = x.shape
    assert d == D_MODEL and seq % GROUP_KEYS == 0 and GROUP_KEYS % ROW_TILE == 0
    rows = batch * seq
    xf = x.reshape(rows, d)

    mods = _mods(c, ada_w, ada_b)
    bias_a = _bias_tiles(rel_bias, 0, N_HEADS_A, MOBA_BIAS_TILES, dilated=False)
    bias_b = _bias_tiles(rel_bias, N_HEADS_A, N_HEADS_B, DIL_BIAS_TILES, dilated=True)
    tables = _rope_tables(seq)
    pair_blocks = N_HEADS_A // 2

    for layer in range(DEPTH):
        mod = mods[layer].reshape(batch, 1, 6 * d)
        i = layer // 2
        if layer % 2 == 0:
            proj, km, vta, vtb = _even_proj(xf, mod, mix_pre_g[layer].reshape(1, d),
                                            _even_weights(ab_w_in[i]), seq)
            proj = proj.reshape(batch, seq, -1)
            km = km.reshape(batch, seq // MOBA_BLOCK, -1)
            o_first = _toeplitz_attn(proj, vta, bias_a, km, batch, seq, 0, moba=True)
            o_second = _toeplitz_attn(proj, vtb, bias_b, None, batch, seq, 3 * pair_blocks, moba=False)
            w_out = ab_w_out[i]
        else:
            w_in2, wq, wkv = _odd_weights(cd_w_in[i], mla_w_uq[i], mla_w_ukv[i])
            qm, km, vtm, sb, vts = _odd_proj(xf, mod, mix_pre_g[layer].reshape(1, d), w_in2,
                                             mla_q_norm_g[i].reshape(1, -1), mla_kv_norm_g[i].reshape(1, -1),
                                             wq, wkv, tables, seq)
            o_first = _mla_attn(qm.reshape(batch, seq, -1), km.reshape(batch, seq, -1), vtm, batch, seq)
            o_second = _stick_attn(sb.reshape(batch, seq, -1), vts, batch, seq)
            w_out = cd_w_out[i]
        xf = _mix_out(o_first.reshape(rows, -1), o_second.reshape(rows, -1), xf, mod,
                      mix_post_g[layer].reshape(1, d), w_out.astype(BF16), seq)
        w_up = ffn_w_up[layer].astype(BF16)
        conv_w, conv_b = ffn_conv_w[layer], ffn_conv_b[layer].reshape(1, -1)
        xf = _ffn(xf, mod, ffn_pre_g[layer].reshape(1, d), ffn_post_g[layer].reshape(1, d),
                  _chunk_cols(w_up[:, :D_FF]), _chunk_cols(w_up[:, D_FF:]),
                  _chunk_cols(conv_w[:, :D_FF]), _chunk_cols(conv_w[:, D_FF:]),
                  _chunk_cols(conv_b[:, :D_FF]), _chunk_cols(conv_b[:, D_FF:]),
                  ffn_w_down[layer].astype(BF16).reshape(N_FF_CHUNKS, FF_CHUNK, d), seq)
    return xf.reshape(batch, seq, d)
```

```python
import functools
import math

import jax
import jax.numpy as jnp
from jax import lax
from jax.experimental import pallas as pl
from jax.experimental.pallas import tpu as pltpu

F32 = jnp.float32
BF16 = jnp.bfloat16

D_MODEL = 1024
DEPTH = 4
HEAD_DIM = 64
N_HEADS_A = 8
N_HEADS_B = 8
N_HEADS_C = 8
N_HEADS_D = 8
MOBA_BLOCK = 256
MOBA_TOPK = 3
DILATED_BRANCHES = ((128, 1), (512, 4), (2048, 16))
MLA_Q_RANK = 256
MLA_KV_RANK = 256
MLA_NOPE_DIM = 64
MLA_ROPE_DIM = 32
MLA_V_DIM = 64
ROPE_THETA = 10000.0
REL_BUCKETS = 32
REL_MAX_DIST = 2048
D_FF = 2816
CONV_WIDTH = 3
NORM_EPS = 1e-6

LANES = 128
BF16_SUBLANES = 16
VMEM_LIMIT_BYTES = 56 * 1024 * 1024

ATT_TILE = MOBA_BLOCK
KEY_GROUP = 4
GROUP_KEYS = KEY_GROUP * ATT_TILE
ROW_TILE = 512
FF_CHUNK = 256
N_FF_CHUNKS = D_FF // FF_CHUNK
CONV_HALO = BF16_SUBLANES
MOBA_BIAS_TILES = REL_MAX_DIST // ATT_TILE + 2
DIL_MAX_TILE_DIST = DILATED_BRANCHES[-1][0] // ATT_TILE
DIL_BIAS_TILES = DIL_MAX_TILE_DIST + 1

_NT = (((1,), (1,)), ((), ()))


def _bucket_lower_bounds():
    max_exact = REL_BUCKETS // 2
    ratio = REL_MAX_DIST // max_exact
    n_log = REL_BUCKETS - max_exact
    lows = list(range(max_exact + 1))
    for k in range(1, n_log):
        d = lows[-1]
        while d ** n_log < (max_exact ** n_log) * (ratio ** k):
            d += 1
        lows.append(d)
    return lows


_BUCKET_LOW = _bucket_lower_bounds()


def _dot(a, b):
    return jnp.dot(a, b, preferred_element_type=F32)


def _dot_nt(a, b):
    return lax.dot_general(a, b, _NT, preferred_element_type=F32)


def _split_bf16(x):
    hi = x.astype(BF16)
    lo = (x - hi.astype(F32)).astype(BF16)
    return hi, lo


def _rms(x, g):
    return (x * lax.rsqrt(jnp.mean(x * x, axis=-1, keepdims=True) + NORM_EPS)) * g


def _prenorm(x, g, scale, shift):
    return _rms(x, g) * (1.0 + scale) + shift


def _params(n_grid_dims):
    return pltpu.CompilerParams(dimension_semantics=("arbitrary",) * n_grid_dims,
                                vmem_limit_bytes=VMEM_LIMIT_BYTES)


def _mods_kernel(c_ref, w_ref, b_ref, o_ref):
    c = c_ref[...]
    cond = c * jax.nn.sigmoid(c)
    c_hi, c_lo = _split_bf16(cond)
    w_hi, w_lo = _split_bf16(w_ref[0])
    o_ref[0] = _dot(c_hi, w_hi) + _dot(c_hi, w_lo) + _dot(c_lo, w_hi) + b_ref[0]


def _mods(c, ada_w, ada_b):
    b, d = c.shape
    rows = BF16_SUBLANES
    n_out = ada_w.shape[-1]
    tn = n_out // 4
    c_pad = jnp.zeros((rows, d), F32).at[:b].set(c)
    out = pl.pallas_call(
        _mods_kernel,
        grid=(DEPTH, n_out // tn),
        in_specs=[pl.BlockSpec((rows, d), lambda l, j: (0, 0)),
                  pl.BlockSpec((1, d, tn), lambda l, j: (l, 0, j)),
                  pl.BlockSpec((1, 1, tn), lambda l, j: (l, 0, j))],
        out_specs=pl.BlockSpec((1, rows, tn), lambda l, j: (l, 0, j)),
        out_shape=jax.ShapeDtypeStruct((DEPTH, rows, n_out), F32),
        compiler_params=_params(2),
        name="ada_mods",
    )(c_pad, ada_w, ada_b.reshape(DEPTH, 1, n_out))
    return out[:, :b]


def _bias_tiles_kernel(tab_ref, o_ref, *, head_off, dilated):
    h = pl.program_id(0) + head_off
    d = pl.program_id(1)
    t = ATT_TILE
    key = lax.broadcasted_iota(jnp.int32, (t, t), 0)
    query = lax.broadcasted_iota(jnp.int32, (t, t), 1)
    dist = d * t + query - key
    val = jnp.full((t, t), tab_ref[h, 0], F32)
    for b in range(1, REL_BUCKETS):
        val = jnp.where(dist >= _BUCKET_LOW[b], tab_ref[h, b], val)
    if dilated:
        mult = jnp.zeros((t, t), F32)
        for window, dil in DILATED_BRANCHES:
            hit = jnp.where(dist <= window, jnp.where((dist & (dil - 1)) == 0, 1.0, 0.0), 0.0)
            mult = mult + hit
        log_mult = jnp.where(mult > 2.5, math.log(3.0), jnp.where(mult > 1.5, math.log(2.0), 0.0))
        val = jnp.where(mult > 0.5, val + log_mult, -jnp.inf)
    o_ref[0, 0] = jnp.where(dist >= 0, val, -jnp.inf)


def _bias_tiles(rel_bias, head_off, n_heads, n_tiles, dilated):
    t = ATT_TILE
    return pl.pallas_call(
        functools.partial(_bias_tiles_kernel, head_off=head_off, dilated=dilated),
        grid=(n_heads, n_tiles),
        in_specs=[pl.BlockSpec(memory_space=pltpu.SMEM)],
        out_specs=pl.BlockSpec((1, 1, t, t), lambda h, d: (h, d, 0, 0)),
        out_shape=jax.ShapeDtypeStruct((n_heads, n_tiles, t, t), F32),
        compiler_params=_params(2),
        name="dilated_bias_tiles" if dilated else "moba_bias_tiles",
    )(rel_bias)


def _store_vt(vt_ref, v):
    vt = v.T.astype(BF16)
    for p in range(vt.shape[0] // LANES):
        vt_ref[0, p, 0] = vt[p * LANES:(p + 1) * LANES, :]


def _vt_spec(tm, per_seq, n_pairs):
    per_group = GROUP_KEYS // tm
    return pl.BlockSpec((1, n_pairs, 1, LANES, tm),
                        lambda i: (i // per_seq, 0, (i % per_seq) // per_group, 0, (i % per_seq) % per_group))


def _vt_shape(batch, seq, n_pairs):
    return jax.ShapeDtypeStruct((batch, n_pairs, seq // GROUP_KEYS, LANES, GROUP_KEYS), BF16)


def _even_proj_kernel(x_ref, mod_ref, g_ref, w_ref, o_ref, km_ref, vta_ref, vtb_ref):
    d = D_MODEL
    mod = mod_ref[0]
    h = _prenorm(x_ref[...], g_ref[...], mod[:, d:2 * d], mod[:, 0:d])
    p = _dot(h.astype(BF16), w_ref[...])
    o_ref[...] = p.astype(BF16)
    wa = N_HEADS_A * HEAD_DIM
    ka = p[:, wa:2 * wa]
    nb = ka.shape[0] // MOBA_BLOCK
    km_ref[0] = jnp.mean(ka.reshape(nb, MOBA_BLOCK, wa), axis=1)
    _store_vt(vta_ref, p[:, 2 * wa:3 * wa])
    _store_vt(vtb_ref, p[:, 5 * wa:6 * wa])


def _even_proj(x, mod, g, w_in, seq):
    rows, d = x.shape
    tm = ROW_TILE
    n = w_in.shape[1]
    wa = N_HEADS_A * HEAD_DIM
    per_seq = seq // tm
    n_pairs = N_HEADS_A // 2
    vt_spec = _vt_spec(tm, per_seq, n_pairs)
    vt_shape = _vt_shape(rows // seq, seq, n_pairs)
    return pl.pallas_call(
        _even_proj_kernel,
        grid=(rows // tm,),
        in_specs=[pl.BlockSpec((tm, d), lambda i: (i, 0)),
                  pl.BlockSpec((1, 1, 6 * d), lambda i: (i // per_seq, 0, 0)),
                  pl.BlockSpec((1, d), lambda i: (0, 0)),
                  pl.BlockSpec((d, n), lambda i: (0, 0))],
        out_specs=[pl.BlockSpec((tm, n), lambda i: (i, 0)),
                   pl.BlockSpec((1, tm // MOBA_BLOCK, wa), lambda i: (i, 0, 0)),
                   vt_spec, vt_spec],
        out_shape=[jax.ShapeDtypeStruct((rows, n), BF16),
                   jax.ShapeDtypeStruct((rows // tm, tm // MOBA_BLOCK, wa), F32),
                   vt_shape, vt_shape],
        compiler_params=_params(1),
        name="even_proj",
    )(x, mod, g, w_in)


def _split_heads(q2):
    first = lax.broadcasted_iota(jnp.int32, q2.shape, 1) < HEAD_DIM
    zero = jnp.zeros_like(q2)
    return jnp.where(first, q2, zero), jnp.where(first, zero, q2)


def _head_rows(vt, h):
    return vt[h * HEAD_DIM:(h + 1) * HEAD_DIM]


def _key_tiles(s):
    t = ATT_TILE
    return [s[c * t:(c + 1) * t] for c in range(KEY_GROUP)]


def _group_start(g):
    return pl.multiple_of(g * GROUP_KEYS, GROUP_KEYS)


def _store_pair_output(o_ref, out_t0, out_t1):
    o_ref[0] = jnp.concatenate([out_t0, out_t1], axis=0).T.astype(BF16)


def _softmax_group(s_tiles, ons, vt_h, acc_ref, h, stats, first):
    maxes = []
    for s, on in zip(s_tiles, ons):
        mx = jnp.max(s, axis=0, keepdims=True)
        maxes.append(mx if on is None else jnp.where(on, mx, -jnp.inf))
    m_new = functools.reduce(jnp.maximum, maxes)
    if not first:
        m_old, l_old = stats
        m_new = jnp.maximum(m_new, m_old)
    ps = []
    l_add = None
    for s, on in zip(s_tiles, ons):
        p = jnp.exp(s - (m_new if on is None else jnp.where(on, m_new, jnp.inf)))
        p_sum = jnp.sum(p, axis=0, keepdims=True)
        l_add = p_sum if l_add is None else l_add + p_sum
        ps.append(p.astype(BF16))
    pv = _dot(vt_h, jnp.concatenate(ps, axis=0))
    if first:
        acc_ref[h] = pv
        return m_new, l_add
    alpha = jnp.exp(m_old - m_new)
    acc_ref[h] = alpha * acc_ref[h] + pv
    return m_new, alpha * l_old + l_add


def _moba_select(qh, km_ref, sel_ref, qi):
    t = ATT_TILE
    km_hi, km_lo = _split_bf16(km_ref[0])
    nb = km_hi.shape[0]
    blk = lax.broadcasted_iota(jnp.int32, (nb, t), 0)
    past = blk < qi
    for h in range(2):
        gate = _dot_nt(km_hi, qh[h]) + _dot_nt(km_lo, qh[h])
        gate = jnp.where(past, gate, -jnp.inf)
        beaten = jnp.zeros((nb, t), F32)
        for other in range(nb):
            row = gate[other:other + 1, :]
            wins = jnp.where(row > gate, 1.0, jnp.where(row == gate, jnp.where(blk > other, 1.0, 0.0), 0.0))
            beaten = beaten + wins
        sel_ref[h] = jnp.where(past, jnp.where(beaten < MOBA_TOPK, 1.0, 0.0), 0.0)


def _toeplitz_attn_kernel(*refs, moba):
    if moba:
        q_ref, k_ref, vt_ref, bias_ref, km_ref, o_ref, acc_ref, sel_ref = refs
    else:
        q_ref, k_ref, vt_ref, bias_ref, o_ref, acc_ref = refs
    qi = pl.program_id(2)
    qh = _split_heads(q_ref[0])
    n_bias = bias_ref.shape[1]
    if moba:
        _moba_select(qh, km_ref, sel_ref, qi)

    def group(g, stats, first):
        kg = k_ref[0, pl.ds(_group_start(g), GROUP_KEYS), :]
        vtg = vt_ref[0, 0, g]
        new = ()
        scores = [_dot_nt(kg, qh[h]) for h in range(2)]
        for h in range(2):
            s_tiles = _key_tiles(scores[h])
            ons = []
            for c in range(KEY_GROUP):
                j = g * KEY_GROUP + c
                dt = qi - j
                s_tiles[c] = s_tiles[c] + bias_ref[h, jnp.clip(dt, 0, n_bias - 1)]
                if moba:
                    chosen = sel_ref[h, pl.ds(j, 1), :] > 0.5
                    ons.append(jnp.logical_or(chosen, dt == 0) if first else chosen)
                else:
                    ons.append(jnp.logical_and(dt >= 0, dt <= DIL_MAX_TILE_DIST))
            new += _softmax_group(s_tiles, ons, _head_rows(vtg, h), acc_ref, h,
                                  None if first else stats[2 * h:2 * h + 2], first)
        return new

    g_own = qi // KEY_GROUP
    g_last = 0 if moba else jnp.maximum(qi - DIL_MAX_TILE_DIST, 0) // KEY_GROUP
    stats = group(g_own, None, True)
    stats = lax.fori_loop(0, g_own - g_last, lambda n, st: group(g_own - 1 - n, st, False), stats)
    _store_pair_output(o_ref, acc_ref[0] / stats[1], acc_ref[1] / stats[3])


def _toeplitz_attn(proj, vt, bias, km, batch, seq, col0, moba):
    t = ATT_TILE
    n_pairs = N_HEADS_A // 2
    n_bias = bias.shape[1]
    in_specs = [pl.BlockSpec((1, t, LANES), lambda b, p, i: (b, i, col0 + p)),
                pl.BlockSpec((1, seq, LANES), lambda b, p, i: (b, 0, col0 + n_pairs + p)),
                pl.BlockSpec((1, 1) + vt.shape[2:], lambda b, p, i: (b, p, 0, 0, 0)),
                pl.BlockSpec((2, n_bias, t, t), lambda b, p, i: (p, 0, 0, 0))]
    args = [proj, proj, vt, bias]
    scratch = [pltpu.VMEM((2, HEAD_DIM, t), F32)]
    if moba:
        in_specs.append(pl.BlockSpec((1, seq // MOBA_BLOCK, LANES), lambda b, p, i: (b, 0, p)))
        args.append(km)
        scratch.append(pltpu.VMEM((2, seq // MOBA_BLOCK, t), F32))
    return pl.pallas_call(
        functools.partial(_toeplitz_attn_kernel, moba=moba),
        grid=(batch, n_pairs, seq // t),
        in_specs=in_specs,
        out_specs=pl.BlockSpec((1, t, LANES), lambda b, p, i: (b, i, p)),
        out_shape=jax.ShapeDtypeStruct((batch, seq, n_pairs * LANES), BF16),
        scratch_shapes=scratch,
        compiler_params=_params(3),
        name="moba_attn" if moba else "dilated_attn",
    )(*args)


def _odd_proj_kernel(x_ref, mod_ref, g_ref, win_ref, gq_ref, gkv_ref, wq_ref, wkv_ref,
                     cq_ref, sq_ref, ck_ref, sk_ref, qm_ref, km_ref, vtm_ref, sb_ref, vts_ref):
    d = D_MODEL
    mod = mod_ref[0]
    h = _prenorm(x_ref[...], g_ref[...], mod[:, d:2 * d], mod[:, 0:d])
    p = _dot(h.astype(BF16), win_ref[...])
    o = MLA_Q_RANK + MLA_KV_RANK
    c_q, c_kv = p[:, 0:MLA_Q_RANK], p[:, MLA_Q_RANK:o]
    k_rope, k_rope_swapped = p[:, o:o + LANES], p[:, o + LANES:o + 2 * LANES]
    sb0 = o + 2 * LANES
    n_qk = 2 * N_HEADS_D * HEAD_DIM
    sb_ref[...] = p[:, sb0:sb0 + n_qk].astype(BF16)
    _store_vt(vts_ref, p[:, sb0 + n_qk:])
    q12 = _dot(_rms(c_q, gq_ref[...]).astype(BF16), wq_ref[...])
    kv = _dot(_rms(c_kv, gkv_ref[...]).astype(BF16), wkv_ref[...])
    k_pe = k_rope * ck_ref[...] + k_rope_swapped * sk_ref[...]
    cq, sq = cq_ref[...], sq_ref[...]
    half = N_HEADS_C * LANES
    for hh in range(N_HEADS_C):
        cols = slice(hh * LANES, (hh + 1) * LANES)
        swapped = slice(half + hh * LANES, half + (hh + 1) * LANES)
        qm_ref[:, cols] = (q12[:, cols] * cq + q12[:, swapped] * sq).astype(BF16)
        km_ref[:, cols] = (kv[:, cols] + k_pe).astype(BF16)
    _store_vt(vtm_ref, kv[:, half:])


def _odd_proj(x, mod, g, w_in, gq, gkv, wq, wkv, tables, seq):
    rows, d = x.shape
    tm = ROW_TILE
    per_seq = seq // tm
    n_in = w_in.shape[1]
    n_sb = 2 * N_HEADS_D * HEAD_DIM
    n_q = N_HEADS_C * LANES
    n_pairs = N_HEADS_C // 2
    const = lambda i: (0, 0)
    table_spec = pl.BlockSpec((tm, LANES), lambda i: (i % per_seq, 0))
    vt_spec = _vt_spec(tm, per_seq, n_pairs)
    vt_shape = _vt_shape(rows // seq, seq, n_pairs)
    return pl.pallas_call(
        _odd_proj_kernel,
        grid=(rows // tm,),
        in_specs=[pl.BlockSpec((tm, d), lambda i: (i, 0)),
                  pl.BlockSpec((1, 1, 6 * d), lambda i: (i // per_seq, 0, 0)),
                  pl.BlockSpec((1, d), const),
                  pl.BlockSpec((d, n_in), const),
                  pl.BlockSpec((1, MLA_Q_RANK), const),
                  pl.BlockSpec((1, MLA_KV_RANK), const),
                  pl.BlockSpec(wq.shape, const),
                  pl.BlockSpec(wkv.shape, const),
                  table_spec, table_spec, table_spec, table_spec],
        out_specs=[pl.BlockSpec((tm, n_q), lambda i: (i, 0)),
                   pl.BlockSpec((tm, n_q), lambda i: (i, 0)),
                   vt_spec,
                   pl.BlockSpec((tm, n_sb), lambda i: (i, 0)),
                   vt_spec],
        out_shape=[jax.ShapeDtypeStruct((rows, n_q), BF16),
                   jax.ShapeDtypeStruct((rows, n_q), BF16),
                   vt_shape,
                   jax.ShapeDtypeStruct((rows, n_sb), BF16),
                   vt_shape],
        compiler_params=_params(1),
        name="odd_proj",
    )(x, mod, g, w_in, gq, gkv, wq, wkv, *tables)


def _causal_masks(qi, g, strict):
    t = ATT_TILE
    key = lax.broadcasted_iota(jnp.int32, (GROUP_KEYS, t), 0) + (g * KEY_GROUP - qi) * t
    query = lax.broadcasted_iota(jnp.int32, (GROUP_KEYS, t), 1)
    return key < query if strict else key <= query


def _mla_attn_kernel(q_ref, k_ref, vt_ref, o_ref, acc_ref):
    qi = pl.program_id(2)
    q2 = q_ref[0]
    qh = (q2[:, 0:LANES], q2[:, LANES:2 * LANES])
    no_switch = [None] * KEY_GROUP

    def group(g, stats, first):
        k2 = k_ref[0, pl.ds(_group_start(g), GROUP_KEYS), :]
        vtg = vt_ref[0, 0, g]
        new = ()
        scores = [_dot_nt(k2[:, h * LANES:(h + 1) * LANES], qh[h]) for h in range(2)]
        for h in range(2):
            s = scores[h]
            if first:
                s = jnp.where(_causal_masks(qi, g, strict=False), s, -jnp.inf)
            new += _softmax_group(_key_tiles(s), no_switch, _head_rows(vtg, h), acc_ref, h,
                                  None if first else stats[2 * h:2 * h + 2], first)
        return new

    g_own = qi // KEY_GROUP
    stats = group(g_own, None, True)
    stats = lax.fori_loop(0, g_own, lambda n, st: group(g_own - 1 - n, st, False), stats)
    _store_pair_output(o_ref, acc_ref[0] / stats[1], acc_ref[1] / stats[3])


def _mla_attn(qm, km, vt, batch, seq):
    t = ATT_TILE
    n_pairs = N_HEADS_C // 2
    return pl.pallas_call(
        _mla_attn_kernel,
        grid=(batch, n_pairs, seq // t),
        in_specs=[pl.BlockSpec((1, t, 2 * LANES), lambda b, p, i: (b, i, p)),
                  pl.BlockSpec((1, seq, 2 * LANES), lambda b, p, i: (b, 0, p)),
                  pl.BlockSpec((1, 1) + vt.shape[2:], lambda b, p, i: (b, p, 0, 0, 0))],
        out_specs=pl.BlockSpec((1, t, LANES), lambda b, p, i: (b, i, p)),
        out_shape=jax.ShapeDtypeStruct((batch, seq, n_pairs * LANES), BF16),
        scratch_shapes=[pltpu.VMEM((2, HEAD_DIM, t), F32)],
        compiler_params=_params(3),
        name="mla_attn",
    )(qm, km, vt)


def _stick_attn_kernel(q_ref, k_ref, vt_ref, o_ref, acc_ref):
    t = ATT_TILE
    qi = pl.program_id(2)
    qh = _split_heads(q_ref[0])
    key = lax.broadcasted_iota(jnp.int32, (t, t), 0)
    other = lax.broadcasted_iota(jnp.int32, (t, t), 1)
    later = jnp.where(other > key, 1.0, 0.0).astype(BF16)

    def group(g, carries, first):
        kg = k_ref[0, pl.ds(_group_start(g), GROUP_KEYS), :]
        vtg = vt_ref[0, 0, g]
        new = ()
        scores = [_dot_nt(kg, qh[h]) for h in range(2)]
        for h in range(2):
            z = scores[h]
            soft = jnp.log(1.0 + jnp.exp(-jnp.abs(z)))
            log_beta = jnp.minimum(z, 0.0) - soft
            log_keep = jnp.minimum(-z, 0.0) - soft
            if first:
                strict = _causal_masks(qi, g, strict=True)
                log_keep = jnp.where(strict, log_keep, 0.0)
            beta_tiles, keep_tiles = _key_tiles(log_beta), _key_tiles(log_keep)
            run = None if first else carries[h]
            a_tiles = [None] * KEY_GROUP
            for c in reversed(range(KEY_GROUP)):
                keep_hi, keep_lo = _split_bf16(keep_tiles[c])
                after = _dot(later, keep_hi) + _dot(later, keep_lo)
                if run is not None:
                    after = after + run
                a_tiles[c] = jnp.exp(beta_tiles[c] + after)
                col_sum = jnp.sum(keep_tiles[c], axis=0, keepdims=True)
                run = col_sum if run is None else run + col_sum
            a = jnp.concatenate(a_tiles, axis=0)
            if first:
                a = jnp.where(strict, a, 0.0)
            pv = _dot(_head_rows(vtg, h), a.astype(BF16))
            acc_ref[h] = pv if first else acc_ref[h] + pv
            new += (run,)
        return new

    g_own = qi // KEY_GROUP
    carries = group(g_own, None, True)
    lax.fori_loop(0, g_own, lambda n, cr: group(g_own - 1 - n, cr, False), carries)
    _store_pair_output(o_ref, acc_ref[0], acc_ref[1])


def _stick_attn(sb, vt, batch, seq):
    t = ATT_TILE
    n_pairs = N_HEADS_D // 2
    return pl.pallas_call(
        _stick_attn_kernel,
        grid=(batch, n_pairs, seq // t),
        in_specs=[pl.BlockSpec((1, t, LANES), lambda b, p, i: (b, i, p)),
                  pl.BlockSpec((1, seq, LANES), lambda b, p, i: (b, 0, n_pairs + p)),
                  pl.BlockSpec((1, 1) + vt.shape[2:], lambda b, p, i: (b, p, 0, 0, 0))],
        out_specs=pl.BlockSpec((1, t, LANES), lambda b, p, i: (b, i, p)),
        out_shape=jax.ShapeDtypeStruct((batch, seq, n_pairs * LANES), BF16),
        scratch_shapes=[pltpu.VMEM((2, HEAD_DIM, t), F32)],
        compiler_params=_params(3),
        name="stick_attn",
    )(sb, sb, vt)


def _mix_out_kernel(oa_ref, ob_ref, x_ref, mod_ref, g_ref, w_ref, o_ref):
    d = D_MODEL
    half = oa_ref.shape[1]
    y = _dot(oa_ref[...], w_ref[0:half, :]) + _dot(ob_ref[...], w_ref[half:2 * half, :])
    gate = mod_ref[0][:, 2 * d:3 * d]
    o_ref[...] = x_ref[...] + gate * _rms(y, g_ref[...])


def _mix_out(oa, ob, x, mod, g, w_out, seq):
    rows, d = x.shape
    tm = ROW_TILE
    per_seq = seq // tm
    half = oa.shape[1]
    return pl.pallas_call(
        _mix_out_kernel,
        grid=(rows // tm,),
        in_specs=[pl.BlockSpec((tm, half), lambda i: (i, 0)),
                  pl.BlockSpec((tm, half), lambda i: (i, 0)),
                  pl.BlockSpec((tm, d), lambda i: (i, 0)),
                  pl.BlockSpec((1, 1, 6 * d), lambda i: (i // per_seq, 0, 0)),
                  pl.BlockSpec((1, d), lambda i: (0, 0)),
                  pl.BlockSpec((2 * half, d), lambda i: (0, 0))],
        out_specs=pl.BlockSpec((tm, d), lambda i: (i, 0)),
        out_shape=jax.ShapeDtypeStruct((rows, d), F32),
        compiler_params=_params(1),
        name="mix_out",
    )(oa, ob, x, mod, g, w_out)


def _ffn_kernel(x_ref, xh_ref, mod_ref, gpre_ref, gpost_ref, wup_ref, cw_ref, cb_ref, wd_ref, o_ref,
                h_ref, u_ref, acc_ref, *, per_seq):
    d = D_MODEL
    tm = ROW_TILE
    halo = CONV_HALO
    cw = FF_CHUNK
    i = pl.program_id(0)
    mod = mod_ref[0]
    shift, scale = mod[:, 3 * d:4 * d], mod[:, 4 * d:5 * d]
    g = gpre_ref[...]
    ahead = _prenorm(xh_ref[...], g, scale, shift)
    ahead = jnp.where(i % per_seq == 0, 0.0, ahead)
    h_ref[0:halo, :] = ahead.astype(BF16)
    h_ref[halo:halo + tm, :] = _prenorm(x_ref[...], g, scale, shift).astype(BF16)

    def up(ch):
        slot = ch % 2
        for half in range(2):
            cols = slice(half * D_FF + ch * cw, half * D_FF + (ch + 1) * cw)
            u_ref[slot, half] = _dot(h_ref[...], wup_ref[:, cols])

    def conv(slot, half, ch):
        cols = slice(half * D_FF + ch * cw, half * D_FF + (ch + 1) * cw)
        w = cw_ref[:, cols]
        out = w[0:1, :] * u_ref[slot, half, halo - 2:halo - 2 + tm, :]
        out = out + w[1:2, :] * u_ref[slot, half, halo - 1:halo - 1 + tm, :]
        out = out + w[2:3, :] * u_ref[slot, half, halo:halo + tm, :]
        return out + cb_ref[:, cols]

    up(0)
    for ch in range(N_FF_CHUNKS):
        if ch + 1 < N_FF_CHUNKS:
            up(ch + 1)
        slot = ch % 2
        act = jax.nn.gelu(conv(slot, 0, ch), approximate=True) * conv(slot, 1, ch)
        down = _dot(act.astype(BF16), wd_ref[ch * cw:(ch + 1) * cw, :])
        if ch == 0:
            acc_ref[...] = down
        else:
            acc_ref[...] += down

    gate_f = mod[:, 5 * d:6 * d]
    o_ref[...] = x_ref[...] + gate_f * _rms(acc_ref[...], gpost_ref[...])


def _ffn(x, mod, g_pre, g_post, w_up, conv_w, conv_b, w_down, seq):
    rows, d = x.shape
    tm = ROW_TILE
    halo = CONV_HALO
    per_seq = seq // tm
    const = lambda i: (0, 0)
    return pl.pallas_call(
        functools.partial(_ffn_kernel, per_seq=per_seq),
        grid=(rows // tm,),
        in_specs=[pl.BlockSpec((tm, d), lambda i: (i, 0)),
                  pl.BlockSpec((halo, d), lambda i: (jnp.maximum(i * (tm // halo) - 1, 0), 0)),
                  pl.BlockSpec((1, 1, 6 * d), lambda i: (i // per_seq, 0, 0)),
                  pl.BlockSpec((1, d), const),
                  pl.BlockSpec((1, d), const),
                  pl.BlockSpec(w_up.shape, const, pipeline_mode=pl.Buffered(1)),
                  pl.BlockSpec(conv_w.shape, const),
                  pl.BlockSpec(conv_b.shape, const),
                  pl.BlockSpec(w_down.shape, const, pipeline_mode=pl.Buffered(1))],
        out_specs=pl.BlockSpec((tm, d), lambda i: (i, 0)),
        out_shape=jax.ShapeDtypeStruct((rows, d), F32),
        scratch_shapes=[pltpu.VMEM((tm + halo, d), BF16),
                        pltpu.VMEM((2, 2, tm + halo, FF_CHUNK), F32),
                        pltpu.VMEM((tm, d), F32)],
        compiler_params=_params(1),
        name="conv_ffn",
    )(x, x, mod, g_pre, g_post, w_up, conv_w, conv_b, w_down)


def _rotate_half_cols(w):
    half = w.shape[-1] // 2
    return jnp.concatenate([-w[..., half:], w[..., :half]], axis=-1)


def _pad_cols(w, left, total):
    return jnp.pad(w, ((0, 0), (left, total - left - w.shape[1])))


def _rope_tables(seq):
    inv_freq = 1.0 / (ROPE_THETA ** (jnp.arange(0, MLA_ROPE_DIM, 2, dtype=F32) / MLA_ROPE_DIM))
    ang = jnp.arange(seq, dtype=F32)[:, None] * inv_freq[None, :]
    cos, sin = jnp.cos(ang), jnp.sin(ang)
    cos2 = _pad_cols(jnp.concatenate([cos, cos], axis=1), MLA_NOPE_DIM, LANES)
    sin2 = _pad_cols(jnp.concatenate([sin, sin], axis=1), MLA_NOPE_DIM, LANES)
    scale = (MLA_NOPE_DIM + MLA_ROPE_DIM) ** -0.5
    nope_ones = _pad_cols(jnp.ones((seq, MLA_NOPE_DIM), F32), 0, LANES)
    return (scale * (cos2 + nope_ones), scale * sin2, cos2, sin2)


def _odd_weights(w_in, w_uq, w_ukv):
    o = MLA_Q_RANK + MLA_KV_RANK
    w_rope = w_in[:, o:o + MLA_ROPE_DIM]
    scale_d = HEAD_DIM ** -0.5
    wd = N_HEADS_D * HEAD_DIM
    sb0 = o + MLA_ROPE_DIM
    w_in2 = jnp.concatenate([
        w_in[:, :o],
        _pad_cols(w_rope, MLA_NOPE_DIM, LANES),
        _pad_cols(_rotate_half_cols(w_rope), MLA_NOPE_DIM, LANES),
        w_in[:, sb0:sb0 + wd] * scale_d,
        w_in[:, sb0 + wd:],
    ], axis=1).astype(BF16)
    qd = MLA_NOPE_DIM + MLA_ROPE_DIM
    uq = w_uq.reshape(MLA_Q_RANK, N_HEADS_C, qd)
    plain = jnp.pad(uq, ((0, 0), (0, 0), (0, LANES - qd)))
    swapped = jnp.pad(_rotate_half_cols(uq[..., MLA_NOPE_DIM:]),
                      ((0, 0), (0, 0), (MLA_NOPE_DIM, LANES - qd)))
    wq = jnp.concatenate([plain.reshape(MLA_Q_RANK, -1), swapped.reshape(MLA_Q_RANK, -1)], axis=1).astype(BF16)
    ukv = w_ukv.reshape(MLA_KV_RANK, N_HEADS_C, MLA_NOPE_DIM + MLA_V_DIM)
    k_nope = jnp.pad(ukv[..., :MLA_NOPE_DIM], ((0, 0), (0, 0), (0, LANES - MLA_NOPE_DIM)))
    v = ukv[..., MLA_NOPE_DIM:]
    wkv = jnp.concatenate([k_nope.reshape(MLA_KV_RANK, -1), v.reshape(MLA_KV_RANK, -1)], axis=1).astype(BF16)
    return w_in2, wq, wkv


def _even_weights(w_in):
    wa = N_HEADS_A * HEAD_DIM
    scale = HEAD_DIM ** -0.5
    col = jnp.arange(w_in.shape[1])
    is_q = (col < wa) | ((col >= 3 * wa) & (col < 4 * wa))
    return (w_in * jnp.where(is_q, scale, 1.0)).astype(BF16)


def kernel(x, c, rel_bias, ada_w, ada_b, mix_pre_g, mix_post_g, ffn_pre_g, ffn_post_g, ab_w_in, ab_w_out,
           cd_w_in, mla_q_norm_g, mla_kv_norm_g, mla_w_uq, mla_w_ukv, cd_w_out, ffn_w_up, ffn_conv_w,
           ffn_conv_b, ffn_w_down):
    batch, seq, d = x.shape
    assert d == D_MODEL and seq % GROUP_KEYS == 0 and GROUP_KEYS % ROW_TILE == 0
    rows = batch * seq
    xf = x.reshape(rows, d)

    mods = _mods(c, ada_w, ada_b)
    bias_a = _bias_tiles(rel_bias, 0, N_HEADS_A, MOBA_BIAS_TILES, dilated=False)
    bias_b = _bias_tiles(rel_bias, N_HEADS_A, N_HEADS_B, DIL_BIAS_TILES, dilated=True)
    tables = _rope_tables(seq)
    pair_blocks = N_HEADS_A // 2

    for layer in range(DEPTH):
        mod = mods[layer].reshape(batch, 1, 6 * d)
        i = layer // 2
        if layer % 2 == 0:
            proj, km, vta, vtb = _even_proj(xf, mod, mix_pre_g[layer].reshape(1, d),
                                            _even_weights(ab_w_in[i]), seq)
            proj = proj.reshape(batch, seq, -1)
            km = km.reshape(batch, seq // MOBA_BLOCK, -1)
            o_first = _toeplitz_attn(proj, vta, bias_a, km, batch, seq, 0, moba=True)
            o_second = _toeplitz_attn(proj, vtb, bias_b, None, batch, seq, 3 * pair_blocks, moba=False)
            w_out = ab_w_out[i]
        else:
            w_in2, wq, wkv = _odd_weights(cd_w_in[i], mla_w_uq[i], mla_w_ukv[i])
            qm, km, vtm, sb, vts = _odd_proj(xf, mod, mix_pre_g[layer].reshape(1, d), w_in2,
                                             mla_q_norm_g[i].reshape(1, -1), mla_kv_norm_g[i].reshape(1, -1),
                                             wq, wkv, tables, seq)
            o_first = _mla_attn(qm.reshape(batch, seq, -1), km.reshape(batch, seq, -1), vtm, batch, seq)
            o_second = _stick_attn(sb.reshape(batch, seq, -1), vts, batch, seq)
            w_out = cd_w_out[i]
        xf = _mix_out(o_first.reshape(rows, -1), o_second.reshape(rows, -1), xf, mod,
                      mix_post_g[layer].reshape(1, d), w_out.astype(BF16), seq)
        xf = _ffn(xf, mod, ffn_pre_g[layer].reshape(1, d), ffn_post_g[layer].reshape(1, d),
                  ffn_w_up[layer].astype(BF16), ffn_conv_w[layer], ffn_conv_b[layer].reshape(1, -1),
                  ffn_w_down[layer].astype(BF16), seq)
    return xf.reshape(batch, seq, d)
```

```python
import functools
import math

import jax
import jax.numpy as jnp
from jax import lax
from jax.experimental import pallas as pl
from jax.experimental.pallas import tpu as pltpu

F32 = jnp.float32
BF16 = jnp.bfloat16

D_MODEL = 1024
DEPTH = 4
HEAD_DIM = 64
N_HEADS_A = 8
N_HEADS_B = 8
N_HEADS_C = 8
N_HEADS_D = 8
MOBA_BLOCK = 256
MOBA_TOPK = 3
DILATED_BRANCHES = ((128, 1), (512, 4), (2048, 16))
MLA_Q_RANK = 256
MLA_KV_RANK = 256
MLA_NOPE_DIM = 64
MLA_ROPE_DIM = 32
MLA_V_DIM = 64
ROPE_THETA = 10000.0
REL_BUCKETS = 32
REL_MAX_DIST = 2048
D_FF = 2816
CONV_WIDTH = 3
NORM_EPS = 1e-6

LANES = 128
BF16_SUBLANES = 16
VMEM_LIMIT_BYTES = 56 * 1024 * 1024

ATT_TILE = MOBA_BLOCK
KEY_GROUP = 4
GROUP_KEYS = KEY_GROUP * ATT_TILE
PAIRS_PER_STEP = 2
HEADS_PER_STEP = 2 * PAIRS_PER_STEP
STEP_LANES = PAIRS_PER_STEP * LANES
ROW_TILE = 512
FF_CHUNK = 256
N_FF_CHUNKS = D_FF // FF_CHUNK
CONV_HALO = BF16_SUBLANES
MOBA_BIAS_TILES = REL_MAX_DIST // ATT_TILE + 2
DIL_MAX_TILE_DIST = DILATED_BRANCHES[-1][0] // ATT_TILE
DIL_BIAS_TILES = DIL_MAX_TILE_DIST + 1

_NT = (((1,), (1,)), ((), ()))


def _bucket_lower_bounds():
    max_exact = REL_BUCKETS // 2
    ratio = REL_MAX_DIST // max_exact
    n_log = REL_BUCKETS - max_exact
    lows = list(range(max_exact + 1))
    for k in range(1, n_log):
        d = lows[-1]
        while d ** n_log < (max_exact ** n_log) * (ratio ** k):
            d += 1
        lows.append(d)
    return lows


_BUCKET_LOW = _bucket_lower_bounds()


def _dot(a, b):
    return jnp.dot(a, b, preferred_element_type=F32)


def _dot_nt(a, b):
    return lax.dot_general(a, b, _NT, preferred_element_type=F32)


def _split_bf16(x):
    hi = x.astype(BF16)
    lo = (x - hi.astype(F32)).astype(BF16)
    return hi, lo


def _rms(x, g):
    return (x * lax.rsqrt(jnp.mean(x * x, axis=-1, keepdims=True) + NORM_EPS)) * g


def _prenorm(x, g, scale, shift):
    return _rms(x, g) * (1.0 + scale) + shift


def _params(n_grid_dims):
    return pltpu.CompilerParams(dimension_semantics=("arbitrary",) * n_grid_dims,
                                vmem_limit_bytes=VMEM_LIMIT_BYTES)


def _mods_kernel(c_ref, w_ref, b_ref, o_ref):
    c = c_ref[...]
    cond = c * jax.nn.sigmoid(c)
    c_hi, c_lo = _split_bf16(cond)
    w_hi, w_lo = _split_bf16(w_ref[0])
    o_ref[0] = _dot(c_hi, w_hi) + _dot(c_hi, w_lo) + _dot(c_lo, w_hi) + b_ref[0]


def _mods(c, ada_w, ada_b):
    b, d = c.shape
    rows = BF16_SUBLANES
    n_out = ada_w.shape[-1]
    tn = n_out // 4
    c_pad = jnp.zeros((rows, d), F32).at[:b].set(c)
    out = pl.pallas_call(
        _mods_kernel,
        grid=(DEPTH, n_out // tn),
        in_specs=[pl.BlockSpec((rows, d), lambda l, j: (0, 0)),
                  pl.BlockSpec((1, d, tn), lambda l, j: (l, 0, j)),
                  pl.BlockSpec((1, 1, tn), lambda l, j: (l, 0, j))],
        out_specs=pl.BlockSpec((1, rows, tn), lambda l, j: (l, 0, j)),
        out_shape=jax.ShapeDtypeStruct((DEPTH, rows, n_out), F32),
        compiler_params=_params(2),
        name="ada_mods",
    )(c_pad, ada_w, ada_b.reshape(DEPTH, 1, n_out))
    return out[:, :b]


def _bias_tiles_kernel(tab_ref, o_ref, *, head_off, dilated):
    h = pl.program_id(0) + head_off
    d = pl.program_id(1)
    t = ATT_TILE
    key = lax.broadcasted_iota(jnp.int32, (t, t), 0)
    query = lax.broadcasted_iota(jnp.int32, (t, t), 1)
    dist = d * t + query - key
    val = jnp.full((t, t), tab_ref[h, 0], F32)
    for b in range(1, REL_BUCKETS):
        val = jnp.where(dist >= _BUCKET_LOW[b], tab_ref[h, b], val)
    if dilated:
        mult = jnp.zeros((t, t), F32)
        for window, dil in DILATED_BRANCHES:
            hit = jnp.where(dist <= window, jnp.where((dist & (dil - 1)) == 0, 1.0, 0.0), 0.0)
            mult = mult + hit
        log_mult = jnp.where(mult > 2.5, math.log(3.0), jnp.where(mult > 1.5, math.log(2.0), 0.0))
        val = jnp.where(mult > 0.5, val + log_mult, -jnp.inf)
    o_ref[0, 0] = jnp.where(dist >= 0, val, -jnp.inf)


def _bias_tiles(rel_bias, head_off, n_heads, n_tiles, dilated):
    t = ATT_TILE
    return pl.pallas_call(
        functools.partial(_bias_tiles_kernel, head_off=head_off, dilated=dilated),
        grid=(n_heads, n_tiles),
        in_specs=[pl.BlockSpec(memory_space=pltpu.SMEM)],
        out_specs=pl.BlockSpec((1, 1, t, t), lambda h, d: (h, d, 0, 0)),
        out_shape=jax.ShapeDtypeStruct((n_heads, n_tiles, t, t), F32),
        compiler_params=_params(2),
        name="dilated_bias_tiles" if dilated else "moba_bias_tiles",
    )(rel_bias)


def _store_vt(vt_ref, v):
    vt = v.T.astype(BF16)
    for p in range(vt.shape[0] // LANES):
        vt_ref[0, p, 0] = vt[p * LANES:(p + 1) * LANES, :]


def _vt_spec(tm, per_seq, n_pairs):
    per_group = GROUP_KEYS // tm
    return pl.BlockSpec((1, n_pairs, 1, LANES, tm),
                        lambda i: (i // per_seq, 0, (i % per_seq) // per_group, 0, (i % per_seq) % per_group))


def _vt_shape(batch, seq, n_pairs):
    return jax.ShapeDtypeStruct((batch, n_pairs, seq // GROUP_KEYS, LANES, GROUP_KEYS), BF16)


def _even_proj_kernel(x_ref, mod_ref, g_ref, w_ref, o_ref, km_ref, vta_ref, vtb_ref):
    d = D_MODEL
    mod = mod_ref[0]
    h = _prenorm(x_ref[...], g_ref[...], mod[:, d:2 * d], mod[:, 0:d])
    p = _dot(h.astype(BF16), w_ref[...])
    o_ref[...] = p.astype(BF16)
    wa = N_HEADS_A * HEAD_DIM
    ka = p[:, wa:2 * wa]
    nb = ka.shape[0] // MOBA_BLOCK
    km_ref[0] = jnp.mean(ka.reshape(nb, MOBA_BLOCK, wa), axis=1)
    _store_vt(vta_ref, p[:, 2 * wa:3 * wa])
    _store_vt(vtb_ref, p[:, 5 * wa:6 * wa])


def _even_proj(x, mod, g, w_in, seq):
    rows, d = x.shape
    tm = ROW_TILE
    n = w_in.shape[1]
    wa = N_HEADS_A * HEAD_DIM
    per_seq = seq // tm
    n_pairs = N_HEADS_A // 2
    vt_spec = _vt_spec(tm, per_seq, n_pairs)
    vt_shape = _vt_shape(rows // seq, seq, n_pairs)
    return pl.pallas_call(
        _even_proj_kernel,
        grid=(rows // tm,),
        in_specs=[pl.BlockSpec((tm, d), lambda i: (i, 0)),
                  pl.BlockSpec((1, 1, 6 * d), lambda i: (i // per_seq, 0, 0)),
                  pl.BlockSpec((1, d), lambda i: (0, 0)),
                  pl.BlockSpec((d, n), lambda i: (0, 0))],
        out_specs=[pl.BlockSpec((tm, n), lambda i: (i, 0)),
                   pl.BlockSpec((1, tm // MOBA_BLOCK, wa), lambda i: (i, 0, 0)),
                   vt_spec, vt_spec],
        out_shape=[jax.ShapeDtypeStruct((rows, n), BF16),
                   jax.ShapeDtypeStruct((rows // tm, tm // MOBA_BLOCK, wa), F32),
                   vt_shape, vt_shape],
        compiler_params=_params(1),
        name="even_proj",
    )(x, mod, g, w_in)


def _pair_lanes(x, h):
    pair = h // 2
    return x[:, pair * LANES:(pair + 1) * LANES]


def _split_heads(q_step):
    first = lax.broadcasted_iota(jnp.int32, (q_step.shape[0], LANES), 1) < HEAD_DIM
    heads = []
    for pair in range(PAIRS_PER_STEP):
        q2 = q_step[:, pair * LANES:(pair + 1) * LANES]
        zero = jnp.zeros_like(q2)
        heads += [jnp.where(first, q2, zero), jnp.where(first, zero, q2)]
    return heads


def _head_rows(vt_ref, g, h):
    return vt_ref[0, h // 2, g, (h % 2) * HEAD_DIM:(h % 2 + 1) * HEAD_DIM, :]


def _key_tiles(s):
    t = ATT_TILE
    return [s[c * t:(c + 1) * t] for c in range(KEY_GROUP)]


def _group_start(g):
    return pl.multiple_of(g * GROUP_KEYS, GROUP_KEYS)


def _store_step_output(o_ref, outs_t):
    o_ref[0] = jnp.concatenate(outs_t, axis=0).T.astype(BF16)


def _softmax_group(s_tiles, ons, vt_h, acc_ref, h, stats, first):
    maxes = []
    for s, on in zip(s_tiles, ons):
        mx = jnp.max(s, axis=0, keepdims=True)
        maxes.append(mx if on is None else jnp.where(on, mx, -jnp.inf))
    m_new = functools.reduce(jnp.maximum, maxes)
    if not first:
        m_old, l_old = stats
        m_new = jnp.maximum(m_new, m_old)
    ps = []
    l_add = None
    for s, on in zip(s_tiles, ons):
        p = jnp.exp(s - (m_new if on is None else jnp.where(on, m_new, jnp.inf)))
        p_sum = jnp.sum(p, axis=0, keepdims=True)
        l_add = p_sum if l_add is None else l_add + p_sum
        ps.append(p.astype(BF16))
    pv = _dot(vt_h, jnp.concatenate(ps, axis=0))
    if first:
        acc_ref[h] = pv
        return m_new, l_add
    alpha = jnp.exp(m_old - m_new)
    acc_ref[h] = alpha * acc_ref[h] + pv
    return m_new, alpha * l_old + l_add


def _moba_select(qh, km_ref, sel_ref, qi):
    t = ATT_TILE
    km_hi, km_lo = _split_bf16(km_ref[0])
    nb = km_hi.shape[0]
    blk = lax.broadcasted_iota(jnp.int32, (nb, t), 0)
    past = blk < qi
    for h in range(HEADS_PER_STEP):
        gate = _dot_nt(_pair_lanes(km_hi, h), qh[h]) + _dot_nt(_pair_lanes(km_lo, h), qh[h])
        gate = jnp.where(past, gate, -jnp.inf)
        beaten = jnp.zeros((nb, t), F32)
        for other in range(nb):
            row = gate[other:other + 1, :]
            wins = jnp.where(row > gate, 1.0, jnp.where(row == gate, jnp.where(blk > other, 1.0, 0.0), 0.0))
            beaten = beaten + wins
        sel_ref[h] = jnp.where(past, jnp.where(beaten < MOBA_TOPK, 1.0, 0.0), 0.0)


def _toeplitz_attn_kernel(*refs, moba):
    if moba:
        q_ref, k_ref, vt_ref, bias_ref, km_ref, o_ref, acc_ref, sel_ref = refs
    else:
        q_ref, k_ref, vt_ref, bias_ref, o_ref, acc_ref = refs
    qi = pl.program_id(2)
    qh = _split_heads(q_ref[0])
    n_bias = bias_ref.shape[1]
    if moba:
        _moba_select(qh, km_ref, sel_ref, qi)

    def group(g, stats, first):
        kg = k_ref[0, pl.ds(_group_start(g), GROUP_KEYS), :]
        new = ()
        scores = [_dot_nt(_pair_lanes(kg, h), qh[h]) for h in range(HEADS_PER_STEP)]
        for h in range(HEADS_PER_STEP):
            s_tiles = _key_tiles(scores[h])
            ons = []
            for c in range(KEY_GROUP):
                j = g * KEY_GROUP + c
                dt = qi - j
                s_tiles[c] = s_tiles[c] + bias_ref[h, jnp.clip(dt, 0, n_bias - 1)]
                if moba:
                    chosen = sel_ref[h, pl.ds(j, 1), :] > 0.5
                    ons.append(jnp.logical_or(chosen, dt == 0) if first else chosen)
                else:
                    ons.append(jnp.logical_and(dt >= 0, dt <= DIL_MAX_TILE_DIST))
            new += _softmax_group(s_tiles, ons, _head_rows(vt_ref, g, h), acc_ref, h,
                                  None if first else stats[2 * h:2 * h + 2], first)
        return new

    g_own = qi // KEY_GROUP
    g_last = 0 if moba else jnp.maximum(qi - DIL_MAX_TILE_DIST, 0) // KEY_GROUP
    stats = group(g_own, None, True)
    stats = lax.fori_loop(0, g_own - g_last, lambda n, st: group(g_own - 1 - n, st, False), stats)
    _store_step_output(o_ref, [acc_ref[h] / stats[2 * h + 1] for h in range(HEADS_PER_STEP)])


def _toeplitz_attn(proj, vt, bias, km, batch, seq, col0, moba):
    t = ATT_TILE
    n_steps = N_HEADS_A // HEADS_PER_STEP
    n_bias = bias.shape[1]
    in_specs = [pl.BlockSpec((1, t, STEP_LANES), lambda b, p, i: (b, i, col0 + p)),
                pl.BlockSpec((1, seq, STEP_LANES), lambda b, p, i: (b, 0, col0 + n_steps + p)),
                pl.BlockSpec((1, PAIRS_PER_STEP) + vt.shape[2:], lambda b, p, i: (b, p, 0, 0, 0)),
                pl.BlockSpec((HEADS_PER_STEP, n_bias, t, t), lambda b, p, i: (p, 0, 0, 0))]
    args = [proj, proj, vt, bias]
    scratch = [pltpu.VMEM((HEADS_PER_STEP, HEAD_DIM, t), F32)]
    if moba:
        in_specs.append(pl.BlockSpec((1, seq // MOBA_BLOCK, STEP_LANES), lambda b, p, i: (b, 0, p)))
        args.append(km)
        scratch.append(pltpu.VMEM((HEADS_PER_STEP, seq // MOBA_BLOCK, t), F32))
    return pl.pallas_call(
        functools.partial(_toeplitz_attn_kernel, moba=moba),
        grid=(batch, n_steps, seq // t),
        in_specs=in_specs,
        out_specs=pl.BlockSpec((1, t, STEP_LANES), lambda b, p, i: (b, i, p)),
        out_shape=jax.ShapeDtypeStruct((batch, seq, N_HEADS_A * HEAD_DIM), BF16),
        scratch_shapes=scratch,
        compiler_params=_params(3),
        name="moba_attn" if moba else "dilated_attn",
    )(*args)


def _odd_proj_kernel(x_ref, mod_ref, g_ref, win_ref, gq_ref, gkv_ref, wq_ref, wkv_ref,
                     cq_ref, sq_ref, ck_ref, sk_ref, qm_ref, km_ref, vtm_ref, sb_ref, vts_ref):
    d = D_MODEL
    mod = mod_ref[0]
    h = _prenorm(x_ref[...], g_ref[...], mod[:, d:2 * d], mod[:, 0:d])
    p = _dot(h.astype(BF16), win_ref[...])
    o = MLA_Q_RANK + MLA_KV_RANK
    c_q, c_kv = p[:, 0:MLA_Q_RANK], p[:, MLA_Q_RANK:o]
    k_rope, k_rope_swapped = p[:, o:o + LANES], p[:, o + LANES:o + 2 * LANES]
    sb0 = o + 2 * LANES
    n_qk = 2 * N_HEADS_D * HEAD_DIM
    sb_ref[...] = p[:, sb0:sb0 + n_qk].astype(BF16)
    _store_vt(vts_ref, p[:, sb0 + n_qk:])
    q12 = _dot(_rms(c_q, gq_ref[...]).astype(BF16), wq_ref[...])
    kv = _dot(_rms(c_kv, gkv_ref[...]).astype(BF16), wkv_ref[...])
    k_pe = k_rope * ck_ref[...] + k_rope_swapped * sk_ref[...]
    cq, sq = cq_ref[...], sq_ref[...]
    half = N_HEADS_C * LANES
    for hh in range(N_HEADS_C):
        cols = slice(hh * LANES, (hh + 1) * LANES)
        swapped = slice(half + hh * LANES, half + (hh + 1) * LANES)
        qm_ref[:, cols] = (q12[:, cols] * cq + q12[:, swapped] * sq).astype(BF16)
        km_ref[:, cols] = (kv[:, cols] + k_pe).astype(BF16)
    _store_vt(vtm_ref, kv[:, half:])


def _odd_proj(x, mod, g, w_in, gq, gkv, wq, wkv, tables, seq):
    rows, d = x.shape
    tm = ROW_TILE
    per_seq = seq // tm
    n_in = w_in.shape[1]
    n_sb = 2 * N_HEADS_D * HEAD_DIM
    n_q = N_HEADS_C * LANES
    n_pairs = N_HEADS_C // 2
    const = lambda i: (0, 0)
    table_spec = pl.BlockSpec((tm, LANES), lambda i: (i % per_seq, 0))
    vt_spec = _vt_spec(tm, per_seq, n_pairs)
    vt_shape = _vt_shape(rows // seq, seq, n_pairs)
    return pl.pallas_call(
        _odd_proj_kernel,
        grid=(rows // tm,),
        in_specs=[pl.BlockSpec((tm, d), lambda i: (i, 0)),
                  pl.BlockSpec((1, 1, 6 * d), lambda i: (i // per_seq, 0, 0)),
                  pl.BlockSpec((1, d), const),
                  pl.BlockSpec((d, n_in), const),
                  pl.BlockSpec((1, MLA_Q_RANK), const),
                  pl.BlockSpec((1, MLA_KV_RANK), const),
                  pl.BlockSpec(wq.shape, const),
                  pl.BlockSpec(wkv.shape, const),
                  table_spec, table_spec, table_spec, table_spec],
        out_specs=[pl.BlockSpec((tm, n_q), lambda i: (i, 0)),
                   pl.BlockSpec((tm, n_q), lambda i: (i, 0)),
                   vt_spec,
                   pl.BlockSpec((tm, n_sb), lambda i: (i, 0)),
                   vt_spec],
        out_shape=[jax.ShapeDtypeStruct((rows, n_q), BF16),
                   jax.ShapeDtypeStruct((rows, n_q), BF16),
                   vt_shape,
                   jax.ShapeDtypeStruct((rows, n_sb), BF16),
                   vt_shape],
        compiler_params=_params(1),
        name="odd_proj",
    )(x, mod, g, w_in, gq, gkv, wq, wkv, *tables)


def _causal_masks(qi, g, strict):
    t = ATT_TILE
    key = lax.broadcasted_iota(jnp.int32, (GROUP_KEYS, t), 0) + (g * KEY_GROUP - qi) * t
    query = lax.broadcasted_iota(jnp.int32, (GROUP_KEYS, t), 1)
    return key < query if strict else key <= query


def _mla_attn_kernel(q_ref, k_ref, vt_ref, o_ref, acc_ref):
    qi = pl.program_id(2)
    q_step = q_ref[0]
    qh = [q_step[:, h * LANES:(h + 1) * LANES] for h in range(HEADS_PER_STEP)]
    no_switch = [None] * KEY_GROUP

    def group(g, stats, first):
        kg = k_ref[0, pl.ds(_group_start(g), GROUP_KEYS), :]
        new = ()
        scores = [_dot_nt(kg[:, h * LANES:(h + 1) * LANES], qh[h]) for h in range(HEADS_PER_STEP)]
        for h in range(HEADS_PER_STEP):
            s = scores[h]
            if first:
                s = jnp.where(_causal_masks(qi, g, strict=False), s, -jnp.inf)
            new += _softmax_group(_key_tiles(s), no_switch, _head_rows(vt_ref, g, h), acc_ref, h,
                                  None if first else stats[2 * h:2 * h + 2], first)
        return new

    g_own = qi // KEY_GROUP
    stats = group(g_own, None, True)
    stats = lax.fori_loop(0, g_own, lambda n, st: group(g_own - 1 - n, st, False), stats)
    _store_step_output(o_ref, [acc_ref[h] / stats[2 * h + 1] for h in range(HEADS_PER_STEP)])


def _mla_attn(qm, km, vt, batch, seq):
    t = ATT_TILE
    n_steps = N_HEADS_C // HEADS_PER_STEP
    qk_lanes = HEADS_PER_STEP * LANES
    return pl.pallas_call(
        _mla_attn_kernel,
        grid=(batch, n_steps, seq // t),
        in_specs=[pl.BlockSpec((1, t, qk_lanes), lambda b, p, i: (b, i, p)),
                  pl.BlockSpec((1, seq, qk_lanes), lambda b, p, i: (b, 0, p)),
                  pl.BlockSpec((1, PAIRS_PER_STEP) + vt.shape[2:], lambda b, p, i: (b, p, 0, 0, 0))],
        out_specs=pl.BlockSpec((1, t, STEP_LANES), lambda b, p, i: (b, i, p)),
        out_shape=jax.ShapeDtypeStruct((batch, seq, N_HEADS_C * MLA_V_DIM), BF16),
        scratch_shapes=[pltpu.VMEM((HEADS_PER_STEP, HEAD_DIM, t), F32)],
        compiler_params=_params(3),
        name="mla_attn",
    )(qm, km, vt)


def _stick_attn_kernel(q_ref, k_ref, vt_ref, o_ref, acc_ref):
    t = ATT_TILE
    qi = pl.program_id(2)
    qh = _split_heads(q_ref[0])
    key = lax.broadcasted_iota(jnp.int32, (t, t), 0)
    other = lax.broadcasted_iota(jnp.int32, (t, t), 1)
    later = jnp.where(other > key, 1.0, 0.0).astype(BF16)

    def group(g, carries, first):
        kg = k_ref[0, pl.ds(_group_start(g), GROUP_KEYS), :]
        new = ()
        scores = [_dot_nt(_pair_lanes(kg, h), qh[h]) for h in range(HEADS_PER_STEP)]
        for h in range(HEADS_PER_STEP):
            z = scores[h]
            log_keep = jnp.minimum(-z, 0.0) - jnp.log(1.0 + jnp.exp(-jnp.abs(z)))
            log_beta = log_keep + z
            if first:
                strict = _causal_masks(qi, g, strict=True)
                log_keep = jnp.where(strict, log_keep, 0.0)
            beta_tiles, keep_tiles = _key_tiles(log_beta), _key_tiles(log_keep)
            run = None if first else carries[h]
            a_tiles = [None] * KEY_GROUP
            for c in reversed(range(KEY_GROUP)):
                keep_hi, keep_lo = _split_bf16(keep_tiles[c])
                after = _dot(later, keep_hi) + _dot(later, keep_lo)
                if run is not None:
                    after = after + run
                a_tiles[c] = jnp.exp(beta_tiles[c] + after)
                col_sum = jnp.sum(keep_tiles[c], axis=0, keepdims=True)
                run = col_sum if run is None else run + col_sum
            a = jnp.concatenate(a_tiles, axis=0)
            if first:
                a = jnp.where(strict, a, 0.0)
            pv = _dot(_head_rows(vt_ref, g, h), a.astype(BF16))
            acc_ref[h] = pv if first else acc_ref[h] + pv
            new += (run,)
        return new

    g_own = qi // KEY_GROUP
    carries = group(g_own, None, True)
    lax.fori_loop(0, g_own, lambda n, cr: group(g_own - 1 - n, cr, False), carries)
    _store_step_output(o_ref, [acc_ref[h] for h in range(HEADS_PER_STEP)])


def _stick_attn(sb, vt, batch, seq):
    t = ATT_TILE
    n_steps = N_HEADS_D // HEADS_PER_STEP
    return pl.pallas_call(
        _stick_attn_kernel,
        grid=(batch, n_steps, seq // t),
        in_specs=[pl.BlockSpec((1, t, STEP_LANES), lambda b, p, i: (b, i, p)),
                  pl.BlockSpec((1, seq, STEP_LANES), lambda b, p, i: (b, 0, n_steps + p)),
                  pl.BlockSpec((1, PAIRS_PER_STEP) + vt.shape[2:], lambda b, p, i: (b, p, 0, 0, 0))],
        out_specs=pl.BlockSpec((1, t, STEP_LANES), lambda b, p, i: (b, i, p)),
        out_shape=jax.ShapeDtypeStruct((batch, seq, N_HEADS_D * HEAD_DIM), BF16),
        scratch_shapes=[pltpu.VMEM((HEADS_PER_STEP, HEAD_DIM, t), F32)],
        compiler_params=_params(3),
        name="stick_attn",
    )(sb, sb, vt)


def _mix_out_kernel(oa_ref, ob_ref, x_ref, mod_ref, g_ref, w_ref, o_ref):
    d = D_MODEL
    half = oa_ref.shape[1]
    y = _dot(oa_ref[...], w_ref[0:half, :]) + _dot(ob_ref[...], w_ref[half:2 * half, :])
    gate = mod_ref[0][:, 2 * d:3 * d]
    o_ref[...] = x_ref[...] + gate * _rms(y, g_ref[...])


def _mix_out(oa, ob, x, mod, g, w_out, seq):
    rows, d = x.shape
    tm = ROW_TILE
    per_seq = seq // tm
    half = oa.shape[1]
    return pl.pallas_call(
        _mix_out_kernel,
        grid=(rows // tm,),
        in_specs=[pl.BlockSpec((tm, half), lambda i: (i, 0)),
                  pl.BlockSpec((tm, half), lambda i: (i, 0)),
                  pl.BlockSpec((tm, d), lambda i: (i, 0)),
                  pl.BlockSpec((1, 1, 6 * d), lambda i: (i // per_seq, 0, 0)),
                  pl.BlockSpec((1, d), lambda i: (0, 0)),
                  pl.BlockSpec((2 * half, d), lambda i: (0, 0))],
        out_specs=pl.BlockSpec((tm, d), lambda i: (i, 0)),
        out_shape=jax.ShapeDtypeStruct((rows, d), F32),
        compiler_params=_params(1),
        name="mix_out",
    )(oa, ob, x, mod, g, w_out)


def _ffn_kernel(x_ref, xh_ref, mod_ref, gpre_ref, gpost_ref, wup_ref, cw_ref, cb_ref, wd_ref, o_ref,
                h_ref, u_ref, acc_ref, *, per_seq):
    d = D_MODEL
    tm = ROW_TILE
    halo = CONV_HALO
    cw = FF_CHUNK
    i = pl.program_id(0)
    mod = mod_ref[0]
    shift, scale = mod[:, 3 * d:4 * d], mod[:, 4 * d:5 * d]
    g = gpre_ref[...]
    ahead = _prenorm(xh_ref[...], g, scale, shift)
    ahead = jnp.where(i % per_seq == 0, 0.0, ahead)
    h_ref[0:halo, :] = ahead.astype(BF16)
    h_ref[halo:halo + tm, :] = _prenorm(x_ref[...], g, scale, shift).astype(BF16)

    def up(ch):
        slot = ch % 2
        for half in range(2):
            cols = slice(half * D_FF + ch * cw, half * D_FF + (ch + 1) * cw)
            u_ref[slot, half] = _dot(h_ref[...], wup_ref[:, cols])

    def conv(slot, half, ch):
        cols = slice(half * D_FF + ch * cw, half * D_FF + (ch + 1) * cw)
        w = cw_ref[:, cols]
        out = w[0:1, :] * u_ref[slot, half, halo - 2:halo - 2 + tm, :]
        out = out + w[1:2, :] * u_ref[slot, half, halo - 1:halo - 1 + tm, :]
        out = out + w[2:3, :] * u_ref[slot, half, halo:halo + tm, :]
        return out + cb_ref[:, cols]

    up(0)
    for ch in range(N_FF_CHUNKS):
        if ch + 1 < N_FF_CHUNKS:
            up(ch + 1)
        slot = ch % 2
        act = jax.nn.gelu(conv(slot, 0, ch), approximate=True) * conv(slot, 1, ch)
        down = _dot(act.astype(BF16), wd_ref[ch * cw:(ch + 1) * cw, :])
        if ch == 0:
            acc_ref[...] = down
        else:
            acc_ref[...] += down

    gate_f = mod[:, 5 * d:6 * d]
    o_ref[...] = x_ref[...] + gate_f * _rms(acc_ref[...], gpost_ref[...])


def _ffn(x, mod, g_pre, g_post, w_up, conv_w, conv_b, w_down, seq):
    rows, d = x.shape
    tm = ROW_TILE
    halo = CONV_HALO
    per_seq = seq // tm
    const = lambda i: (0, 0)
    return pl.pallas_call(
        functools.partial(_ffn_kernel, per_seq=per_seq),
        grid=(rows // tm,),
        in_specs=[pl.BlockSpec((tm, d), lambda i: (i, 0)),
                  pl.BlockSpec((halo, d), lambda i: (jnp.maximum(i * (tm // halo) - 1, 0), 0)),
                  pl.BlockSpec((1, 1, 6 * d), lambda i: (i // per_seq, 0, 0)),
                  pl.BlockSpec((1, d), const),
                  pl.BlockSpec((1, d), const),
                  pl.BlockSpec(w_up.shape, const, pipeline_mode=pl.Buffered(1)),
                  pl.BlockSpec(conv_w.shape, const),
                  pl.BlockSpec(conv_b.shape, const),
                  pl.BlockSpec(w_down.shape, const, pipeline_mode=pl.Buffered(1))],
        out_specs=pl.BlockSpec((tm, d), lambda i: (i, 0)),
        out_shape=jax.ShapeDtypeStruct((rows, d), F32),
        scratch_shapes=[pltpu.VMEM((tm + halo, d), BF16),
                        pltpu.VMEM((2, 2, tm + halo, FF_CHUNK), F32),
                        pltpu.VMEM((tm, d), F32)],
        compiler_params=_params(1),
        name="conv_ffn",
    )(x, x, mod, g_pre, g_post, w_up, conv_w, conv_b, w_down)


def _rotate_half_cols(w):
    half = w.shape[-1] // 2
    return jnp.concatenate([-w[..., half:], w[..., :half]], axis=-1)


def _pad_cols(w, left, total):
    return jnp.pad(w, ((0, 0), (left, total - left - w.shape[1])))


def _rope_tables(seq):
    inv_freq = 1.0 / (ROPE_THETA ** (jnp.arange(0, MLA_ROPE_DIM, 2, dtype=F32) / MLA_ROPE_DIM))
    ang = jnp.arange(seq, dtype=F32)[:, None] * inv_freq[None, :]
    cos, sin = jnp.cos(ang), jnp.sin(ang)
    cos2 = _pad_cols(jnp.concatenate([cos, cos], axis=1), MLA_NOPE_DIM, LANES)
    sin2 = _pad_cols(jnp.concatenate([sin, sin], axis=1), MLA_NOPE_DIM, LANES)
    scale = (MLA_NOPE_DIM + MLA_ROPE_DIM) ** -0.5
    nope_ones = _pad_cols(jnp.ones((seq, MLA_NOPE_DIM), F32), 0, LANES)
    return (scale * (cos2 + nope_ones), scale * sin2, cos2, sin2)


def _odd_weights(w_in, w_uq, w_ukv):
    o = MLA_Q_RANK + MLA_KV_RANK
    w_rope = w_in[:, o:o + MLA_ROPE_DIM]
    scale_d = HEAD_DIM ** -0.5
    wd = N_HEADS_D * HEAD_DIM
    sb0 = o + MLA_ROPE_DIM
    w_in2 = jnp.concatenate([
        w_in[:, :o],
        _pad_cols(w_rope, MLA_NOPE_DIM, LANES),
        _pad_cols(_rotate_half_cols(w_rope), MLA_NOPE_DIM, LANES),
        w_in[:, sb0:sb0 + wd] * scale_d,
        w_in[:, sb0 + wd:],
    ], axis=1).astype(BF16)
    qd = MLA_NOPE_DIM + MLA_ROPE_DIM
    uq = w_uq.reshape(MLA_Q_RANK, N_HEADS_C, qd)
    plain = jnp.pad(uq, ((0, 0), (0, 0), (0, LANES - qd)))
    swapped = jnp.pad(_rotate_half_cols(uq[..., MLA_NOPE_DIM:]),
                      ((0, 0), (0, 0), (MLA_NOPE_DIM, LANES - qd)))
    wq = jnp.concatenate([plain.reshape(MLA_Q_RANK, -1), swapped.reshape(MLA_Q_RANK, -1)], axis=1).astype(BF16)
    ukv = w_ukv.reshape(MLA_KV_RANK, N_HEADS_C, MLA_NOPE_DIM + MLA_V_DIM)
    k_nope = jnp.pad(ukv[..., :MLA_NOPE_DIM], ((0, 0), (0, 0), (0, LANES - MLA_NOPE_DIM)))
    v = ukv[..., MLA_NOPE_DIM:]
    wkv = jnp.concatenate([k_nope.reshape(MLA_KV_RANK, -1), v.reshape(MLA_KV_RANK, -1)], axis=1).astype(BF16)
    return w_in2, wq, wkv


def _even_weights(w_in):
    wa = N_HEADS_A * HEAD_DIM
    scale = HEAD_DIM ** -0.5
    col = jnp.arange(w_in.shape[1])
    is_q = (col < wa) | ((col >= 3 * wa) & (col < 4 * wa))
    return (w_in * jnp.where(is_q, scale, 1.0)).astype(BF16)


def kernel(x, c, rel_bias, ada_w, ada_b, mix_pre_g, mix_post_g, ffn_pre_g, ffn_post_g, ab_w_in, ab_w_out,
           cd_w_in, mla_q_norm_g, mla_kv_norm_g, mla_w_uq, mla_w_ukv, cd_w_out, ffn_w_up, ffn_conv_w,
           ffn_conv_b, ffn_w_down):
    batch, seq, d = x.shape
    assert d == D_MODEL and seq % GROUP_KEYS == 0 and GROUP_KEYS % ROW_TILE == 0
    rows = batch * seq
    xf = x.reshape(rows, d)

    mods = _mods(c, ada_w, ada_b)
    bias_a = _bias_tiles(rel_bias, 0, N_HEADS_A, MOBA_BIAS_TILES, dilated=False)
    bias_b = _bias_tiles(rel_bias, N_HEADS_A, N_HEADS_B, DIL_BIAS_TILES, dilated=True)
    tables = _rope_tables(seq)
    step_blocks = N_HEADS_A // HEADS_PER_STEP

    for layer in range(DEPTH):
        mod = mods[layer].reshape(batch, 1, 6 * d)
        i = layer // 2
        if layer % 2 == 0:
            proj, km, vta, vtb = _even_proj(xf, mod, mix_pre_g[layer].reshape(1, d),
                                            _even_weights(ab_w_in[i]), seq)
            proj = proj.reshape(batch, seq, -1)
            km = km.reshape(batch, seq // MOBA_BLOCK, -1)
            o_first = _toeplitz_attn(proj, vta, bias_a, km, batch, seq, 0, moba=True)
            o_second = _toeplitz_attn(proj, vtb, bias_b, None, batch, seq, 3 * step_blocks, moba=False)
            w_out = ab_w_out[i]
        else:
            w_in2, wq, wkv = _odd_weights(cd_w_in[i], mla_w_uq[i], mla_w_ukv[i])
            qm, km, vtm, sb, vts = _odd_proj(xf, mod, mix_pre_g[layer].reshape(1, d), w_in2,
                                             mla_q_norm_g[i].reshape(1, -1), mla_kv_norm_g[i].reshape(1, -1),
                                             wq, wkv, tables, seq)
            o_first = _mla_attn(qm.reshape(batch, seq, -1), km.reshape(batch, seq, -1), vtm, batch, seq)
            o_second = _stick_attn(sb.reshape(batch, seq, -1), vts, batch, seq)
            w_out = cd_w_out[i]
        xf = _mix_out(o_first.reshape(rows, -1), o_second.reshape(rows, -1), xf, mod,
                      mix_post_g[layer].reshape(1, d), w_out.astype(BF16), seq)
        xf = _ffn(xf, mod, ffn_pre_g[layer].reshape(1, d), ffn_post_g[layer].reshape(1, d),
                  ffn_w_up[layer].astype(BF16), ffn_conv_w[layer], ffn_conv_b[layer].reshape(1, -1),
                  ffn_w_down[layer].astype(BF16), seq)
    return xf.reshape(batch, seq, d)
```

```python
import functools
import math

import jax
import jax.numpy as jnp
from jax import lax
from jax.experimental import pallas as pl
from jax.experimental.pallas import tpu as pltpu

F32 = jnp.float32
BF16 = jnp.bfloat16

D_MODEL = 1024
DEPTH = 4
HEAD_DIM = 64
N_HEADS_A = 8
N_HEADS_B = 8
N_HEADS_C = 8
N_HEADS_D = 8
MOBA_BLOCK = 256
MOBA_TOPK = 3
DILATED_BRANCHES = ((128, 1), (512, 4), (2048, 16))
MLA_Q_RANK = 256
MLA_KV_RANK = 256
MLA_NOPE_DIM = 64
MLA_ROPE_DIM = 32
MLA_V_DIM = 64
ROPE_THETA = 10000.0
REL_BUCKETS = 32
REL_MAX_DIST = 2048
D_FF = 2816
CONV_WIDTH = 3
NORM_EPS = 1e-6

LANES = 128
BF16_SUBLANES = 16
VMEM_LIMIT_BYTES = 56 * 1024 * 1024

ATT_TILE = MOBA_BLOCK
KEY_GROUP = 4
GROUP_KEYS = KEY_GROUP * ATT_TILE
PAIRS_PER_STEP = 2
HEADS_PER_STEP = 2 * PAIRS_PER_STEP
STEP_LANES = PAIRS_PER_STEP * LANES
ROW_TILE = 512
FF_CHUNK = 256
N_FF_CHUNKS = D_FF // FF_CHUNK
CONV_HALO = BF16_SUBLANES
MOBA_BIAS_TILES = REL_MAX_DIST // ATT_TILE + 2
DIL_MAX_TILE_DIST = DILATED_BRANCHES[-1][0] // ATT_TILE
DIL_BIAS_TILES = DIL_MAX_TILE_DIST + 1

_NT = (((1,), (1,)), ((), ()))


def _bucket_lower_bounds():
    max_exact = REL_BUCKETS // 2
    ratio = REL_MAX_DIST // max_exact
    n_log = REL_BUCKETS - max_exact
    lows = list(range(max_exact + 1))
    for k in range(1, n_log):
        d = lows[-1]
        while d ** n_log < (max_exact ** n_log) * (ratio ** k):
            d += 1
        lows.append(d)
    return lows


_BUCKET_LOW = _bucket_lower_bounds()


def _dot(a, b):
    return jnp.dot(a, b, preferred_element_type=F32)


def _dot_nt(a, b):
    return lax.dot_general(a, b, _NT, preferred_element_type=F32)


def _split_bf16(x):
    hi = x.astype(BF16)
    lo = (x - hi.astype(F32)).astype(BF16)
    return hi, lo


def _rms(x, g):
    return (x * lax.rsqrt(jnp.mean(x * x, axis=-1, keepdims=True) + NORM_EPS)) * g


def _prenorm(x, g, scale, shift):
    return _rms(x, g) * (1.0 + scale) + shift


def _params(n_grid_dims):
    return pltpu.CompilerParams(dimension_semantics=("arbitrary",) * n_grid_dims,
                                vmem_limit_bytes=VMEM_LIMIT_BYTES)


def _mods_kernel(c_ref, w_ref, b_ref, o_ref):
    c = c_ref[...]
    cond = c * jax.nn.sigmoid(c)
    c_hi, c_lo = _split_bf16(cond)
    w_hi, w_lo = _split_bf16(w_ref[0])
    o_ref[0] = _dot(c_hi, w_hi) + _dot(c_hi, w_lo) + _dot(c_lo, w_hi) + b_ref[0]


def _mods(c, ada_w, ada_b):
    b, d = c.shape
    rows = BF16_SUBLANES
    n_out = ada_w.shape[-1]
    tn = n_out // 4
    c_pad = jnp.zeros((rows, d), F32).at[:b].set(c)
    out = pl.pallas_call(
        _mods_kernel,
        grid=(DEPTH, n_out // tn),
        in_specs=[pl.BlockSpec((rows, d), lambda l, j: (0, 0)),
                  pl.BlockSpec((1, d, tn), lambda l, j: (l, 0, j)),
                  pl.BlockSpec((1, 1, tn), lambda l, j: (l, 0, j))],
        out_specs=pl.BlockSpec((1, rows, tn), lambda l, j: (l, 0, j)),
        out_shape=jax.ShapeDtypeStruct((DEPTH, rows, n_out), F32),
        compiler_params=_params(2),
        name="ada_mods",
    )(c_pad, ada_w, ada_b.reshape(DEPTH, 1, n_out))
    return out[:, :b]


def _bias_tiles_kernel(tab_ref, o_ref, *, head_off, dilated):
    h = pl.program_id(0) + head_off
    d = pl.program_id(1)
    t = ATT_TILE
    key = lax.broadcasted_iota(jnp.int32, (t, t), 0)
    query = lax.broadcasted_iota(jnp.int32, (t, t), 1)
    dist = d * t + query - key
    val = jnp.full((t, t), tab_ref[h, 0], F32)
    for b in range(1, REL_BUCKETS):
        val = jnp.where(dist >= _BUCKET_LOW[b], tab_ref[h, b], val)
    if dilated:
        mult = jnp.zeros((t, t), F32)
        for window, dil in DILATED_BRANCHES:
            hit = jnp.where(dist <= window, jnp.where((dist & (dil - 1)) == 0, 1.0, 0.0), 0.0)
            mult = mult + hit
        log_mult = jnp.where(mult > 2.5, math.log(3.0), jnp.where(mult > 1.5, math.log(2.0), 0.0))
        val = jnp.where(mult > 0.5, val + log_mult, -jnp.inf)
    o_ref[0, 0] = jnp.where(dist >= 0, val, -jnp.inf)


def _bias_tiles(rel_bias, head_off, n_heads, n_tiles, dilated):
    t = ATT_TILE
    return pl.pallas_call(
        functools.partial(_bias_tiles_kernel, head_off=head_off, dilated=dilated),
        grid=(n_heads, n_tiles),
        in_specs=[pl.BlockSpec(memory_space=pltpu.SMEM)],
        out_specs=pl.BlockSpec((1, 1, t, t), lambda h, d: (h, d, 0, 0)),
        out_shape=jax.ShapeDtypeStruct((n_heads, n_tiles, t, t), F32),
        compiler_params=_params(2),
        name="dilated_bias_tiles" if dilated else "moba_bias_tiles",
    )(rel_bias)


def _store_vt(vt_ref, v):
    vt = v.T.astype(BF16)
    for p in range(vt.shape[0] // LANES):
        vt_ref[0, p, 0] = vt[p * LANES:(p + 1) * LANES, :]


def _vt_spec(tm, per_seq, n_pairs):
    per_group = GROUP_KEYS // tm
    return pl.BlockSpec((1, n_pairs, 1, LANES, tm),
                        lambda i: (i // per_seq, 0, (i % per_seq) // per_group, 0, (i % per_seq) % per_group))


def _vt_shape(batch, seq, n_pairs):
    return jax.ShapeDtypeStruct((batch, n_pairs, seq // GROUP_KEYS, LANES, GROUP_KEYS), BF16)


def _even_proj_kernel(x_ref, mod_ref, g_ref, w_ref, o_ref, km_ref, vta_ref, vtb_ref):
    d = D_MODEL
    mod = mod_ref[0]
    h = _prenorm(x_ref[...], g_ref[...], mod[:, d:2 * d], mod[:, 0:d])
    p = _dot(h.astype(BF16), w_ref[...])
    o_ref[...] = p.astype(BF16)
    wa = N_HEADS_A * HEAD_DIM
    ka = p[:, wa:2 * wa]
    nb = ka.shape[0] // MOBA_BLOCK
    km_ref[0] = jnp.mean(ka.reshape(nb, MOBA_BLOCK, wa), axis=1)
    _store_vt(vta_ref, p[:, 2 * wa:3 * wa])
    _store_vt(vtb_ref, p[:, 5 * wa:6 * wa])


def _even_proj(x, mod, g, w_in, seq):
    rows, d = x.shape
    tm = ROW_TILE
    n = w_in.shape[1]
    wa = N_HEADS_A * HEAD_DIM
    per_seq = seq // tm
    n_pairs = N_HEADS_A // 2
    vt_spec = _vt_spec(tm, per_seq, n_pairs)
    vt_shape = _vt_shape(rows // seq, seq, n_pairs)
    return pl.pallas_call(
        _even_proj_kernel,
        grid=(rows // tm,),
        in_specs=[pl.BlockSpec((tm, d), lambda i: (i, 0)),
                  pl.BlockSpec((1, 1, 6 * d), lambda i: (i // per_seq, 0, 0)),
                  pl.BlockSpec((1, d), lambda i: (0, 0)),
                  pl.BlockSpec((d, n), lambda i: (0, 0))],
        out_specs=[pl.BlockSpec((tm, n), lambda i: (i, 0)),
                   pl.BlockSpec((1, tm // MOBA_BLOCK, wa), lambda i: (i, 0, 0)),
                   vt_spec, vt_spec],
        out_shape=[jax.ShapeDtypeStruct((rows, n), BF16),
                   jax.ShapeDtypeStruct((rows // tm, tm // MOBA_BLOCK, wa), F32),
                   vt_shape, vt_shape],
        compiler_params=_params(1),
        name="even_proj",
    )(x, mod, g, w_in)


def _pair_lanes(x, h):
    pair = h // 2
    return x[:, pair * LANES:(pair + 1) * LANES]


def _split_heads(q_step):
    first = lax.broadcasted_iota(jnp.int32, (q_step.shape[0], LANES), 1) < HEAD_DIM
    heads = []
    for pair in range(PAIRS_PER_STEP):
        q2 = q_step[:, pair * LANES:(pair + 1) * LANES]
        zero = jnp.zeros_like(q2)
        heads += [jnp.where(first, q2, zero), jnp.where(first, zero, q2)]
    return heads


def _head_rows(vt_ref, g, h, n_tiles):
    return vt_ref[0, h // 2, g, (h % 2) * HEAD_DIM:(h % 2 + 1) * HEAD_DIM, 0:n_tiles * ATT_TILE]


def _key_tiles(s):
    t = ATT_TILE
    return [s[c * t:(c + 1) * t] for c in range(s.shape[0] // t)]


def _group_keys(k_ref, g, n_tiles):
    start = pl.multiple_of(g * GROUP_KEYS, GROUP_KEYS)
    return k_ref[0, pl.ds(start, n_tiles * ATT_TILE), :]


def _sweep(qi, group, n_far_groups=None):
    g_own = qi // KEY_GROUP
    own = [functools.partial(group, g_own, None, n + 1, True) for n in range(KEY_GROUP)]
    state = lax.switch(qi % KEY_GROUP, own)
    n_past = g_own if n_far_groups is None else jnp.minimum(g_own, n_far_groups)
    return lax.fori_loop(0, n_past, lambda n, st: group(g_own - 1 - n, st, KEY_GROUP, False), state)


def _tile_iotas():
    t = ATT_TILE
    return lax.broadcasted_iota(jnp.int32, (t, t), 0), lax.broadcasted_iota(jnp.int32, (t, t), 1)


def _store_step_output(o_ref, outs_t):
    o_ref[0] = jnp.concatenate(outs_t, axis=0).T.astype(BF16)


def _softmax_group(s_tiles, ons, vt_h, acc_ref, h, stats, first):
    maxes = []
    for s, on in zip(s_tiles, ons):
        mx = jnp.max(s, axis=0, keepdims=True)
        maxes.append(mx if on is None else jnp.where(on, mx, -jnp.inf))
    m_new = functools.reduce(jnp.maximum, maxes)
    if not first:
        m_old, l_old = stats
        m_new = jnp.maximum(m_new, m_old)
    ps = []
    l_add = None
    for s, on in zip(s_tiles, ons):
        p = jnp.exp(s - (m_new if on is None else jnp.where(on, m_new, jnp.inf)))
        p_sum = jnp.sum(p, axis=0, keepdims=True)
        l_add = p_sum if l_add is None else l_add + p_sum
        ps.append(p.astype(BF16))
    pv = _dot(vt_h, jnp.concatenate(ps, axis=0))
    if first:
        acc_ref[h] = pv
        return m_new, l_add
    alpha = jnp.exp(m_old - m_new)
    acc_ref[h] = alpha * acc_ref[h] + pv
    return m_new, alpha * l_old + l_add


def _moba_select(qh, km_ref, sel_ref, qi):
    t = ATT_TILE
    km_hi, km_lo = _split_bf16(km_ref[0])
    nb = km_hi.shape[0]
    blk = lax.broadcasted_iota(jnp.int32, (nb, t), 0)
    past = blk < qi
    for h in range(HEADS_PER_STEP):
        gate = _dot_nt(_pair_lanes(km_hi, h), qh[h]) + _dot_nt(_pair_lanes(km_lo, h), qh[h])
        gate = jnp.where(past, gate, -jnp.inf)
        beaten = jnp.zeros((nb, t), F32)
        for other in range(nb):
            row = gate[other:other + 1, :]
            wins = jnp.where(row > gate, 1.0, jnp.where(row == gate, jnp.where(blk > other, 1.0, 0.0), 0.0))
            beaten = beaten + wins
        sel_ref[h] = jnp.where(past, jnp.where(beaten < MOBA_TOPK, 1.0, 0.0), 0.0)


def _toeplitz_attn_kernel(*refs, moba):
    if moba:
        q_ref, k_ref, vt_ref, bias_ref, km_ref, o_ref, acc_ref, sel_ref = refs
    else:
        q_ref, k_ref, vt_ref, bias_ref, o_ref, acc_ref = refs
    qi = pl.program_id(2)
    qh = _split_heads(q_ref[0])
    n_bias = bias_ref.shape[1]
    if moba:
        _moba_select(qh, km_ref, sel_ref, qi)

    def group(g, stats, n_tiles, first):
        kg = _group_keys(k_ref, g, n_tiles)
        new = ()
        scores = [_dot_nt(_pair_lanes(kg, h), qh[h]) for h in range(HEADS_PER_STEP)]
        for h in range(HEADS_PER_STEP):
            s_tiles = _key_tiles(scores[h])
            ons = []
            for c in range(n_tiles):
                j = g * KEY_GROUP + c
                if first:
                    dt = n_tiles - 1 - c
                    s_tiles[c] = s_tiles[c] + bias_ref[h, dt]
                    ons.append(sel_ref[h, pl.ds(j, 1), :] > 0.5 if moba and dt > 0 else None)
                else:
                    dt = qi - j
                    s_tiles[c] = s_tiles[c] + bias_ref[h, jnp.minimum(dt, n_bias - 1)]
                    ons.append(sel_ref[h, pl.ds(j, 1), :] > 0.5 if moba else dt <= DIL_MAX_TILE_DIST)
            new += _softmax_group(s_tiles, ons, _head_rows(vt_ref, g, h, n_tiles), acc_ref, h,
                                  None if first else stats[2 * h:2 * h + 2], first)
        return new

    if moba:
        n_far = None
    else:
        n_far = qi // KEY_GROUP - jnp.maximum(qi - DIL_MAX_TILE_DIST, 0) // KEY_GROUP
    stats = _sweep(qi, group, n_far)
    _store_step_output(o_ref, [acc_ref[h] / stats[2 * h + 1] for h in range(HEADS_PER_STEP)])


def _toeplitz_attn(proj, vt, bias, km, batch, seq, col0, moba):
    t = ATT_TILE
    n_steps = N_HEADS_A // HEADS_PER_STEP
    n_bias = bias.shape[1]
    in_specs = [pl.BlockSpec((1, t, STEP_LANES), lambda b, p, i: (b, i, col0 + p)),
                pl.BlockSpec((1, seq, STEP_LANES), lambda b, p, i: (b, 0, col0 + n_steps + p)),
                pl.BlockSpec((1, PAIRS_PER_STEP) + vt.shape[2:], lambda b, p, i: (b, p, 0, 0, 0)),
                pl.BlockSpec((HEADS_PER_STEP, n_bias, t, t), lambda b, p, i: (p, 0, 0, 0))]
    args = [proj, proj, vt, bias]
    scratch = [pltpu.VMEM((HEADS_PER_STEP, HEAD_DIM, t), F32)]
    if moba:
        in_specs.append(pl.BlockSpec((1, seq // MOBA_BLOCK, STEP_LANES), lambda b, p, i: (b, 0, p)))
        args.append(km)
        scratch.append(pltpu.VMEM((HEADS_PER_STEP, seq // MOBA_BLOCK, t), F32))
    return pl.pallas_call(
        functools.partial(_toeplitz_attn_kernel, moba=moba),
        grid=(batch, n_steps, seq // t),
        in_specs=in_specs,
        out_specs=pl.BlockSpec((1, t, STEP_LANES), lambda b, p, i: (b, i, p)),
        out_shape=jax.ShapeDtypeStruct((batch, seq, N_HEADS_A * HEAD_DIM), BF16),
        scratch_shapes=scratch,
        compiler_params=_params(3),
        name="moba_attn" if moba else "dilated_attn",
    )(*args)


def _odd_proj_kernel(x_ref, mod_ref, g_ref, win_ref, gq_ref, gkv_ref, wq_ref, wkv_ref,
                     cq_ref, sq_ref, ck_ref, sk_ref, qm_ref, km_ref, vtm_ref, sb_ref, vts_ref):
    d = D_MODEL
    mod = mod_ref[0]
    h = _prenorm(x_ref[...], g_ref[...], mod[:, d:2 * d], mod[:, 0:d])
    p = _dot(h.astype(BF16), win_ref[...])
    o = MLA_Q_RANK + MLA_KV_RANK
    c_q, c_kv = p[:, 0:MLA_Q_RANK], p[:, MLA_Q_RANK:o]
    k_rope, k_rope_swapped = p[:, o:o + LANES], p[:, o + LANES:o + 2 * LANES]
    sb0 = o + 2 * LANES
    n_qk = 2 * N_HEADS_D * HEAD_DIM
    sb_ref[...] = p[:, sb0:sb0 + n_qk].astype(BF16)
    _store_vt(vts_ref, p[:, sb0 + n_qk:])
    q12 = _dot(_rms(c_q, gq_ref[...]).astype(BF16), wq_ref[...])
    kv = _dot(_rms(c_kv, gkv_ref[...]).astype(BF16), wkv_ref[...])
    k_pe = k_rope * ck_ref[...] + k_rope_swapped * sk_ref[...]
    cq, sq = cq_ref[...], sq_ref[...]
    half = N_HEADS_C * LANES
    for hh in range(N_HEADS_C):
        cols = slice(hh * LANES, (hh + 1) * LANES)
        swapped = slice(half + hh * LANES, half + (hh + 1) * LANES)
        qm_ref[:, cols] = (q12[:, cols] * cq + q12[:, swapped] * sq).astype(BF16)
        km_ref[:, cols] = (kv[:, cols] + k_pe).astype(BF16)
    _store_vt(vtm_ref, kv[:, half:])


def _odd_proj(x, mod, g, w_in, gq, gkv, wq, wkv, tables, seq):
    rows, d = x.shape
    tm = ROW_TILE
    per_seq = seq // tm
    n_in = w_in.shape[1]
    n_sb = 2 * N_HEADS_D * HEAD_DIM
    n_q = N_HEADS_C * LANES
    n_pairs = N_HEADS_C // 2
    const = lambda i: (0, 0)
    table_spec = pl.BlockSpec((tm, LANES), lambda i: (i % per_seq, 0))
    vt_spec = _vt_spec(tm, per_seq, n_pairs)
    vt_shape = _vt_shape(rows // seq, seq, n_pairs)
    return pl.pallas_call(
        _odd_proj_kernel,
        grid=(rows // tm,),
        in_specs=[pl.BlockSpec((tm, d), lambda i: (i, 0)),
                  pl.BlockSpec((1, 1, 6 * d), lambda i: (i // per_seq, 0, 0)),
                  pl.BlockSpec((1, d), const),
                  pl.BlockSpec((d, n_in), const),
                  pl.BlockSpec((1, MLA_Q_RANK), const),
                  pl.BlockSpec((1, MLA_KV_RANK), const),
                  pl.BlockSpec(wq.shape, const),
                  pl.BlockSpec(wkv.shape, const),
                  table_spec, table_spec, table_spec, table_spec],
        out_specs=[pl.BlockSpec((tm, n_q), lambda i: (i, 0)),
                   pl.BlockSpec((tm, n_q), lambda i: (i, 0)),
                   vt_spec,
                   pl.BlockSpec((tm, n_sb), lambda i: (i, 0)),
                   vt_spec],
        out_shape=[jax.ShapeDtypeStruct((rows, n_q), BF16),
                   jax.ShapeDtypeStruct((rows, n_q), BF16),
                   vt_shape,
                   jax.ShapeDtypeStruct((rows, n_sb), BF16),
                   vt_shape],
        compiler_params=_params(1),
        name="odd_proj",
    )(x, mod, g, w_in, gq, gkv, wq, wkv, *tables)


def _mla_attn_kernel(q_ref, k_ref, vt_ref, o_ref, acc_ref):
    qi = pl.program_id(2)
    q_step = q_ref[0]
    qh = [q_step[:, h * LANES:(h + 1) * LANES] for h in range(HEADS_PER_STEP)]

    def group(g, stats, n_tiles, first):
        kg = _group_keys(k_ref, g, n_tiles)
        new = ()
        scores = [_dot_nt(kg[:, h * LANES:(h + 1) * LANES], qh[h]) for h in range(HEADS_PER_STEP)]
        for h in range(HEADS_PER_STEP):
            s_tiles = _key_tiles(scores[h])
            if first:
                key, query = _tile_iotas()
                s_tiles[-1] = jnp.where(key <= query, s_tiles[-1], -jnp.inf)
            new += _softmax_group(s_tiles, [None] * n_tiles, _head_rows(vt_ref, g, h, n_tiles), acc_ref, h,
                                  None if first else stats[2 * h:2 * h + 2], first)
        return new

    stats = _sweep(qi, group)
    _store_step_output(o_ref, [acc_ref[h] / stats[2 * h + 1] for h in range(HEADS_PER_STEP)])


def _mla_attn(qm, km, vt, batch, seq):
    t = ATT_TILE
    n_steps = N_HEADS_C // HEADS_PER_STEP
    qk_lanes = HEADS_PER_STEP * LANES
    return pl.pallas_call(
        _mla_attn_kernel,
        grid=(batch, n_steps, seq // t),
        in_specs=[pl.BlockSpec((1, t, qk_lanes), lambda b, p, i: (b, i, p)),
                  pl.BlockSpec((1, seq, qk_lanes), lambda b, p, i: (b, 0, p)),
                  pl.BlockSpec((1, PAIRS_PER_STEP) + vt.shape[2:], lambda b, p, i: (b, p, 0, 0, 0))],
        out_specs=pl.BlockSpec((1, t, STEP_LANES), lambda b, p, i: (b, i, p)),
        out_shape=jax.ShapeDtypeStruct((batch, seq, N_HEADS_C * MLA_V_DIM), BF16),
        scratch_shapes=[pltpu.VMEM((HEADS_PER_STEP, HEAD_DIM, t), F32)],
        compiler_params=_params(3),
        name="mla_attn",
    )(qm, km, vt)


def _stick_attn_kernel(q_ref, k_ref, vt_ref, o_ref, acc_ref):
    t = ATT_TILE
    qi = pl.program_id(2)
    qh = _split_heads(q_ref[0])
    key = lax.broadcasted_iota(jnp.int32, (t, t), 0)
    other = lax.broadcasted_iota(jnp.int32, (t, t), 1)
    later = jnp.where(other > key, 1.0, 0.0).astype(BF16)

    strict = key < other

    def group(g, carries, n_tiles, first):
        kg = _group_keys(k_ref, g, n_tiles)
        new = ()
        scores = [_dot_nt(_pair_lanes(kg, h), qh[h]) for h in range(HEADS_PER_STEP)]
        for h in range(HEADS_PER_STEP):
            z = scores[h]
            neg_z = -z
            log_keep = jnp.minimum(neg_z, 0.0) - jnp.log(1.0 + jnp.exp(jnp.minimum(z, neg_z)))
            beta_tiles, keep_tiles = _key_tiles(log_keep + z), _key_tiles(log_keep)
            if first:
                keep_tiles[-1] = jnp.where(strict, keep_tiles[-1], 0.0)
            run = None if first else carries[h]
            a_tiles = [None] * n_tiles
            for c in reversed(range(n_tiles)):
                keep_hi, keep_lo = _split_bf16(keep_tiles[c])
                after = _dot(later, keep_hi) + _dot(later, keep_lo)
                if run is not None:
                    after = after + run
                a_tiles[c] = jnp.exp(beta_tiles[c] + after)
                col_sum = jnp.sum(keep_tiles[c], axis=0, keepdims=True)
                run = col_sum if run is None else run + col_sum
            if first:
                a_tiles[-1] = jnp.where(strict, a_tiles[-1], 0.0)
            pv = _dot(_head_rows(vt_ref, g, h, n_tiles), jnp.concatenate(a_tiles, axis=0).astype(BF16))
            acc_ref[h] = pv if first else acc_ref[h] + pv
            new += (run,)
        return new

    _sweep(qi, group)
    _store_step_output(o_ref, [acc_ref[h] for h in range(HEADS_PER_STEP)])


def _stick_attn(sb, vt, batch, seq):
    t = ATT_TILE
    n_steps = N_HEADS_D // HEADS_PER_STEP
    return pl.pallas_call(
        _stick_attn_kernel,
        grid=(batch, n_steps, seq // t),
        in_specs=[pl.BlockSpec((1, t, STEP_LANES), lambda b, p, i: (b, i, p)),
                  pl.BlockSpec((1, seq, STEP_LANES), lambda b, p, i: (b, 0, n_steps + p)),
                  pl.BlockSpec((1, PAIRS_PER_STEP) + vt.shape[2:], lambda b, p, i: (b, p, 0, 0, 0))],
        out_specs=pl.BlockSpec((1, t, STEP_LANES), lambda b, p, i: (b, i, p)),
        out_shape=jax.ShapeDtypeStruct((batch, seq, N_HEADS_D * HEAD_DIM), BF16),
        scratch_shapes=[pltpu.VMEM((HEADS_PER_STEP, HEAD_DIM, t), F32)],
        compiler_params=_params(3),
        name="stick_attn",
    )(sb, sb, vt)


def _mix_out_kernel(oa_ref, ob_ref, x_ref, mod_ref, g_ref, w_ref, o_ref):
    d = D_MODEL
    half = oa_ref.shape[1]
    y = _dot(oa_ref[...], w_ref[0:half, :]) + _dot(ob_ref[...], w_ref[half:2 * half, :])
    gate = mod_ref[0][:, 2 * d:3 * d]
    o_ref[...] = x_ref[...] + gate * _rms(y, g_ref[...])


def _mix_out(oa, ob, x, mod, g, w_out, seq):
    rows, d = x.shape
    tm = ROW_TILE
    per_seq = seq // tm
    half = oa.shape[1]
    return pl.pallas_call(
        _mix_out_kernel,
        grid=(rows // tm,),
        in_specs=[pl.BlockSpec((tm, half), lambda i: (i, 0)),
                  pl.BlockSpec((tm, half), lambda i: (i, 0)),
                  pl.BlockSpec((tm, d), lambda i: (i, 0)),
                  pl.BlockSpec((1, 1, 6 * d), lambda i: (i // per_seq, 0, 0)),
                  pl.BlockSpec((1, d), lambda i: (0, 0)),
                  pl.BlockSpec((2 * half, d), lambda i: (0, 0))],
        out_specs=pl.BlockSpec((tm, d), lambda i: (i, 0)),
        out_shape=jax.ShapeDtypeStruct((rows, d), F32),
        compiler_params=_params(1),
        name="mix_out",
    )(oa, ob, x, mod, g, w_out)


def _ffn_kernel(x_ref, xh_ref, mod_ref, gpre_ref, gpost_ref, wup_ref, cw_ref, cb_ref, wd_ref, o_ref,
                h_ref, u_ref, acc_ref, *, per_seq):
    d = D_MODEL
    tm = ROW_TILE
    halo = CONV_HALO
    cw = FF_CHUNK
    i = pl.program_id(0)
    mod = mod_ref[0]
    shift, scale = mod[:, 3 * d:4 * d], mod[:, 4 * d:5 * d]
    g = gpre_ref[...]
    ahead = _prenorm(xh_ref[...], g, scale, shift)
    ahead = jnp.where(i % per_seq == 0, 0.0, ahead)
    h_ref[0:halo, :] = ahead.astype(BF16)
    h_ref[halo:halo + tm, :] = _prenorm(x_ref[...], g, scale, shift).astype(BF16)

    def up(ch):
        slot = ch % 2
        for half in range(2):
            cols = slice(half * D_FF + ch * cw, half * D_FF + (ch + 1) * cw)
            u_ref[slot, half] = _dot(h_ref[...], wup_ref[:, cols])

    def conv(slot, half, ch):
        cols = slice(half * D_FF + ch * cw, half * D_FF + (ch + 1) * cw)
        w = cw_ref[:, cols]
        out = w[0:1, :] * u_ref[slot, half, halo - 2:halo - 2 + tm, :]
        out = out + w[1:2, :] * u_ref[slot, half, halo - 1:halo - 1 + tm, :]
        out = out + w[2:3, :] * u_ref[slot, half, halo:halo + tm, :]
        return out + cb_ref[:, cols]

    up(0)
    for ch in range(N_FF_CHUNKS):
        if ch + 1 < N_FF_CHUNKS:
            up(ch + 1)
        slot = ch % 2
        act = jax.nn.gelu(conv(slot, 0, ch), approximate=True) * conv(slot, 1, ch)
        down = _dot(act.astype(BF16), wd_ref[ch * cw:(ch + 1) * cw, :])
        if ch == 0:
            acc_ref[...] = down
        else:
            acc_ref[...] += down

    gate_f = mod[:, 5 * d:6 * d]
    o_ref[...] = x_ref[...] + gate_f * _rms(acc_ref[...], gpost_ref[...])


def _ffn(x, mod, g_pre, g_post, w_up, conv_w, conv_b, w_down, seq):
    rows, d = x.shape
    tm = ROW_TILE
    halo = CONV_HALO
    per_seq = seq // tm
    const = lambda i: (0, 0)
    return pl.pallas_call(
        functools.partial(_ffn_kernel, per_seq=per_seq),
        grid=(rows // tm,),
        in_specs=[pl.BlockSpec((tm, d), lambda i: (i, 0)),
                  pl.BlockSpec((halo, d), lambda i: (jnp.maximum(i * (tm // halo) - 1, 0), 0)),
                  pl.BlockSpec((1, 1, 6 * d), lambda i: (i // per_seq, 0, 0)),
                  pl.BlockSpec((1, d), const),
                  pl.BlockSpec((1, d), const),
                  pl.BlockSpec(w_up.shape, const, pipeline_mode=pl.Buffered(1)),
                  pl.BlockSpec(conv_w.shape, const),
                  pl.BlockSpec(conv_b.shape, const),
                  pl.BlockSpec(w_down.shape, const, pipeline_mode=pl.Buffered(1))],
        out_specs=pl.BlockSpec((tm, d), lambda i: (i, 0)),
        out_shape=jax.ShapeDtypeStruct((rows, d), F32),
        scratch_shapes=[pltpu.VMEM((tm + halo, d), BF16),
                        pltpu.VMEM((2, 2, tm + halo, FF_CHUNK), F32),
                        pltpu.VMEM((tm, d), F32)],
        compiler_params=_params(1),
        name="conv_ffn",
    )(x, x, mod, g_pre, g_post, w_up, conv_w, conv_b, w_down)


def _rotate_half_cols(w):
    half = w.shape[-1] // 2
    return jnp.concatenate([-w[..., half:], w[..., :half]], axis=-1)


def _pad_cols(w, left, total):
    return jnp.pad(w, ((0, 0), (left, total - left - w.shape[1])))


def _rope_tables(seq):
    inv_freq = 1.0 / (ROPE_THETA ** (jnp.arange(0, MLA_ROPE_DIM, 2, dtype=F32) / MLA_ROPE_DIM))
    ang = jnp.arange(seq, dtype=F32)[:, None] * inv_freq[None, :]
    cos, sin = jnp.cos(ang), jnp.sin(ang)
    cos2 = _pad_cols(jnp.concatenate([cos, cos], axis=1), MLA_NOPE_DIM, LANES)
    sin2 = _pad_cols(jnp.concatenate([sin, sin], axis=1), MLA_NOPE_DIM, LANES)
    scale = (MLA_NOPE_DIM + MLA_ROPE_DIM) ** -0.5
    nope_ones = _pad_cols(jnp.ones((seq, MLA_NOPE_DIM), F32), 0, LANES)
    return (scale * (cos2 + nope_ones), scale * sin2, cos2, sin2)


def _odd_weights(w_in, w_uq, w_ukv):
    o = MLA_Q_RANK + MLA_KV_RANK
    w_rope = w_in[:, o:o + MLA_ROPE_DIM]
    scale_d = HEAD_DIM ** -0.5
    wd = N_HEADS_D * HEAD_DIM
    sb0 = o + MLA_ROPE_DIM
    w_in2 = jnp.concatenate([
        w_in[:, :o],
        _pad_cols(w_rope, MLA_NOPE_DIM, LANES),
        _pad_cols(_rotate_half_cols(w_rope), MLA_NOPE_DIM, LANES),
        w_in[:, sb0:sb0 + wd] * scale_d,
        w_in[:, sb0 + wd:],
    ], axis=1).astype(BF16)
    qd = MLA_NOPE_DIM + MLA_ROPE_DIM
    uq = w_uq.reshape(MLA_Q_RANK, N_HEADS_C, qd)
    plain = jnp.pad(uq, ((0, 0), (0, 0), (0, LANES - qd)))
    swapped = jnp.pad(_rotate_half_cols(uq[..., MLA_NOPE_DIM:]),
                      ((0, 0), (0, 0), (MLA_NOPE_DIM, LANES - qd)))
    wq = jnp.concatenate([plain.reshape(MLA_Q_RANK, -1), swapped.reshape(MLA_Q_RANK, -1)], axis=1).astype(BF16)
    ukv = w_ukv.reshape(MLA_KV_RANK, N_HEADS_C, MLA_NOPE_DIM + MLA_V_DIM)
    k_nope = jnp.pad(ukv[..., :MLA_NOPE_DIM], ((0, 0), (0, 0), (0, LANES - MLA_NOPE_DIM)))
    v = ukv[..., MLA_NOPE_DIM:]
    wkv = jnp.concatenate([k_nope.reshape(MLA_KV_RANK, -1), v.reshape(MLA_KV_RANK, -1)], axis=1).astype(BF16)
    return w_in2, wq, wkv


def _even_weights(w_in):
    wa = N_HEADS_A * HEAD_DIM
    scale = HEAD_DIM ** -0.5
    col = jnp.arange(w_in.shape[1])
    is_q = (col < wa) | ((col >= 3 * wa) & (col < 4 * wa))
    return (w_in * jnp.where(is_q, scale, 1.0)).astype(BF16)


def kernel(x, c, rel_bias, ada_w, ada_b, mix_pre_g, mix_post_g, ffn_pre_g, ffn_post_g, ab_w_in, ab_w_out,
           cd_w_in, mla_q_norm_g, mla_kv_norm_g, mla_w_uq, mla_w_ukv, cd_w_out, ffn_w_up, ffn_conv_w,
           ffn_conv_b, ffn_w_down):
    batch, seq, d = x.shape
    assert d == D_MODEL and seq % GROUP_KEYS == 0 and GROUP_KEYS % ROW_TILE == 0
    rows = batch * seq
    xf = x.reshape(rows, d)

    mods = _mods(c, ada_w, ada_b)
    bias_a = _bias_tiles(rel_bias, 0, N_HEADS_A, MOBA_BIAS_TILES, dilated=False)
    bias_b = _bias_tiles(rel_bias, N_HEADS_A, N_HEADS_B, DIL_BIAS_TILES, dilated=True)
    tables = _rope_tables(seq)
    step_blocks = N_HEADS_A // HEADS_PER_STEP

    for layer in range(DEPTH):
        mod = mods[layer].reshape(batch, 1, 6 * d)
        i = layer // 2
        if layer % 2 == 0:
            proj, km, vta, vtb = _even_proj(xf, mod, mix_pre_g[layer].reshape(1, d),
                                            _even_weights(ab_w_in[i]), seq)
            proj = proj.reshape(batch, seq, -1)
            km = km.reshape(batch, seq // MOBA_BLOCK, -1)
            o_first = _toeplitz_attn(proj, vta, bias_a, km, batch, seq, 0, moba=True)
            o_second = _toeplitz_attn(proj, vtb, bias_b, None, batch, seq, 3 * step_blocks, moba=False)
            w_out = ab_w_out[i]
        else:
            w_in2, wq, wkv = _odd_weights(cd_w_in[i], mla_w_uq[i], mla_w_ukv[i])
            qm, km, vtm, sb, vts = _odd_proj(xf, mod, mix_pre_g[layer].reshape(1, d), w_in2,
                                             mla_q_norm_g[i].reshape(1, -1), mla_kv_norm_g[i].reshape(1, -1),
                                             wq, wkv, tables, seq)
            o_first = _mla_attn(qm.reshape(batch, seq, -1), km.reshape(batch, seq, -1), vtm, batch, seq)
            o_second = _stick_attn(sb.reshape(batch, seq, -1), vts, batch, seq)
            w_out = cd_w_out[i]
        xf = _mix_out(o_first.reshape(rows, -1), o_second.reshape(rows, -1), xf, mod,
                      mix_post_g[layer].reshape(1, d), w_out.astype(BF16), seq)
        xf = _ffn(xf, mod, ffn_pre_g[layer].reshape(1, d), ffn_post_g[layer].reshape(1, d),
                  ffn_w_up[layer].astype(BF16), ffn_conv_w[layer], ffn_conv_b[layer].reshape(1, -1),
                  ffn_w_down[layer].astype(BF16), seq)
    return xf.reshape(batch, seq, d)
```

```python
import functools
import math

import jax
import jax.numpy as jnp
from jax import lax
from jax.experimental import pallas as pl
from jax.experimental.pallas import tpu as pltpu

F32 = jnp.float32
BF16 = jnp.bfloat16

D_MODEL = 1024
DEPTH = 4
HEAD_DIM = 64
N_HEADS_A = 8
N_HEADS_B = 8
N_HEADS_C = 8
N_HEADS_D = 8
MOBA_BLOCK = 256
MOBA_TOPK = 3
DILATED_BRANCHES = ((128, 1), (512, 4), (2048, 16))
MLA_Q_RANK = 256
MLA_KV_RANK = 256
MLA_NOPE_DIM = 64
MLA_ROPE_DIM = 32
MLA_V_DIM = 64
ROPE_THETA = 10000.0
REL_BUCKETS = 32
REL_MAX_DIST = 2048
D_FF = 2816
CONV_WIDTH = 3
NORM_EPS = 1e-6

LANES = 128
SUBLANES = 8
BF16_SUBLANES = 16
VMEM_LIMIT_BYTES = 56 * 1024 * 1024

ATT_TILE = MOBA_BLOCK
KEY_GROUP = 4
GROUP_KEYS = KEY_GROUP * ATT_TILE
PAIRS_PER_STEP = 2
HEADS_PER_STEP = 2 * PAIRS_PER_STEP
STEP_LANES = PAIRS_PER_STEP * LANES
ROW_TILE = 512
FF_CHUNK = 256
N_FF_CHUNKS = D_FF // FF_CHUNK
CONV_HALO = BF16_SUBLANES
MOBA_BIAS_TILES = REL_MAX_DIST // ATT_TILE + 2
DIL_MAX_TILE_DIST = DILATED_BRANCHES[-1][0] // ATT_TILE
DIL_BIAS_TILES = DIL_MAX_TILE_DIST + 1

_NT = (((1,), (1,)), ((), ()))


def _bucket_lower_bounds():
    max_exact = REL_BUCKETS // 2
    ratio = REL_MAX_DIST // max_exact
    n_log = REL_BUCKETS - max_exact
    lows = list(range(max_exact + 1))
    for k in range(1, n_log):
        d = lows[-1]
        while d ** n_log < (max_exact ** n_log) * (ratio ** k):
            d += 1
        lows.append(d)
    return lows


_BUCKET_LOW = _bucket_lower_bounds()


def _dot(a, b):
    return jnp.dot(a, b, preferred_element_type=F32)


def _dot_nt(a, b):
    return lax.dot_general(a, b, _NT, preferred_element_type=F32)


def _split_bf16(x):
    hi = x.astype(BF16)
    lo = (x - hi.astype(F32)).astype(BF16)
    return hi, lo


def _rms(x, g):
    return (x * lax.rsqrt(jnp.mean(x * x, axis=-1, keepdims=True) + NORM_EPS)) * g


def _prenorm(x, g, scale, shift):
    return _rms(x, g) * (1.0 + scale) + shift


def _resident(block_shape, index_map):
    return pl.BlockSpec(block_shape, index_map, pipeline_mode=pl.Buffered(1))


def _params(n_grid_dims):
    return pltpu.CompilerParams(dimension_semantics=("arbitrary",) * n_grid_dims,
                                vmem_limit_bytes=VMEM_LIMIT_BYTES)


def _mods_kernel(c_ref, w_ref, b_ref, o_ref):
    c = c_ref[...]
    cond = c * jax.nn.sigmoid(c)
    c_hi, c_lo = _split_bf16(cond)
    w_hi, w_lo = _split_bf16(w_ref[0])
    o_ref[0] = _dot(c_hi, w_hi) + _dot(c_hi, w_lo) + _dot(c_lo, w_hi) + b_ref[0]


def _mods(c, ada_w, ada_b):
    b, d = c.shape
    rows = BF16_SUBLANES
    n_out = ada_w.shape[-1]
    tn = n_out // 4
    c_pad = jnp.zeros((rows, d), F32).at[:b].set(c)
    out = pl.pallas_call(
        _mods_kernel,
        grid=(DEPTH, n_out // tn),
        in_specs=[pl.BlockSpec((rows, d), lambda l, j: (0, 0)),
                  pl.BlockSpec((1, d, tn), lambda l, j: (l, 0, j)),
                  pl.BlockSpec((1, 1, tn), lambda l, j: (l, 0, j))],
        out_specs=pl.BlockSpec((1, rows, tn), lambda l, j: (l, 0, j)),
        out_shape=jax.ShapeDtypeStruct((DEPTH, rows, n_out), F32),
        compiler_params=_params(2),
        name="ada_mods",
    )(c_pad, ada_w, ada_b.reshape(DEPTH, 1, n_out))
    return out[:, :b]


def _bias_tiles_kernel(tab_ref, o_ref, *, head_off, dilated):
    h = pl.program_id(0) + head_off
    d = pl.program_id(1)
    t = ATT_TILE
    key = lax.broadcasted_iota(jnp.int32, (t, t), 0)
    query = lax.broadcasted_iota(jnp.int32, (t, t), 1)
    dist = d * t + query - key
    val = jnp.full((t, t), tab_ref[h, 0], F32)
    for b in range(1, REL_BUCKETS):
        val = jnp.where(dist >= _BUCKET_LOW[b], tab_ref[h, b], val)
    if dilated:
        mult = jnp.zeros((t, t), F32)
        for window, dil in DILATED_BRANCHES:
            hit = jnp.where(dist <= window, jnp.where((dist & (dil - 1)) == 0, 1.0, 0.0), 0.0)
            mult = mult + hit
        log_mult = jnp.where(mult > 2.5, math.log(3.0), jnp.where(mult > 1.5, math.log(2.0), 0.0))
        val = jnp.where(mult > 0.5, val + log_mult, -jnp.inf)
    o_ref[0, 0] = jnp.where(dist >= 0, val, -jnp.inf)


def _bias_tiles(rel_bias, head_off, n_heads, n_tiles, dilated):
    t = ATT_TILE
    return pl.pallas_call(
        functools.partial(_bias_tiles_kernel, head_off=head_off, dilated=dilated),
        grid=(n_heads, n_tiles),
        in_specs=[pl.BlockSpec(memory_space=pltpu.SMEM)],
        out_specs=pl.BlockSpec((1, 1, t, t), lambda h, d: (h, d, 0, 0)),
        out_shape=jax.ShapeDtypeStruct((n_heads, n_tiles, t, t), F32),
        compiler_params=_params(2),
        name="dilated_bias_tiles" if dilated else "moba_bias_tiles",
    )(rel_bias)


def _store_vt(vt_ref, v):
    vt = v.T.astype(BF16)
    for p in range(vt.shape[0] // LANES):
        vt_ref[0, p, 0] = vt[p * LANES:(p + 1) * LANES, :]


def _vt_spec(tm, per_seq, n_pairs):
    per_group = GROUP_KEYS // tm
    return pl.BlockSpec((1, n_pairs, 1, LANES, tm),
                        lambda i: (i // per_seq, 0, (i % per_seq) // per_group, 0, (i % per_seq) % per_group))


def _vt_shape(batch, seq, n_pairs):
    return jax.ShapeDtypeStruct((batch, n_pairs, seq // GROUP_KEYS, LANES, GROUP_KEYS), BF16)


def _even_proj_kernel(x_ref, mod_ref, g_ref, w_ref, o_ref, km_ref, vta_ref, vtb_ref):
    d = D_MODEL
    mod = mod_ref[0]
    h = _prenorm(x_ref[...], g_ref[...], mod[:, d:2 * d], mod[:, 0:d])
    p = _dot(h.astype(BF16), w_ref[...])
    o_ref[...] = p.astype(BF16)
    wa = N_HEADS_A * HEAD_DIM
    ka = p[:, wa:2 * wa]
    nb = ka.shape[0] // MOBA_BLOCK
    km_ref[0] = jnp.mean(ka.reshape(nb, MOBA_BLOCK, wa), axis=1)
    _store_vt(vta_ref, p[:, 2 * wa:3 * wa])
    _store_vt(vtb_ref, p[:, 5 * wa:6 * wa])


def _even_proj(x, mod, g, w_in, seq):
    rows, d = x.shape
    tm = ROW_TILE
    n = w_in.shape[1]
    wa = N_HEADS_A * HEAD_DIM
    per_seq = seq // tm
    n_pairs = N_HEADS_A // 2
    vt_spec = _vt_spec(tm, per_seq, n_pairs)
    vt_shape = _vt_shape(rows // seq, seq, n_pairs)
    return pl.pallas_call(
        _even_proj_kernel,
        grid=(rows // tm,),
        in_specs=[pl.BlockSpec((tm, d), lambda i: (i, 0)),
                  pl.BlockSpec((1, 1, 6 * d), lambda i: (i // per_seq, 0, 0)),
                  pl.BlockSpec((1, d), lambda i: (0, 0)),
                  pl.BlockSpec((d, n), lambda i: (0, 0))],
        out_specs=[pl.BlockSpec((tm, n), lambda i: (i, 0)),
                   pl.BlockSpec((1, tm // MOBA_BLOCK, wa), lambda i: (i, 0, 0)),
                   vt_spec, vt_spec],
        out_shape=[jax.ShapeDtypeStruct((rows, n), BF16),
                   jax.ShapeDtypeStruct((rows // tm, tm // MOBA_BLOCK, wa), F32),
                   vt_shape, vt_shape],
        compiler_params=_params(1),
        name="even_proj",
    )(x, mod, g, w_in)


def _pair_lanes(x, h):
    pair = h // 2
    return x[:, pair * LANES:(pair + 1) * LANES]


def _split_heads(q_step):
    first = lax.broadcasted_iota(jnp.int32, (q_step.shape[0], LANES), 1) < HEAD_DIM
    heads = []
    for pair in range(PAIRS_PER_STEP):
        q2 = q_step[:, pair * LANES:(pair + 1) * LANES]
        zero = jnp.zeros_like(q2)
        heads += [jnp.where(first, q2, zero), jnp.where(first, zero, q2)]
    return heads


def _head_rows(vt_ref, g, h, n_tiles):
    return vt_ref[0, h // 2, g, (h % 2) * HEAD_DIM:(h % 2 + 1) * HEAD_DIM, 0:n_tiles * ATT_TILE]


def _tile_scores(keys, q_head, per_tile):
    t = ATT_TILE
    n = keys.shape[0] // t
    if per_tile:
        return [_dot_nt(keys[c * t:(c + 1) * t], q_head) for c in range(n)]
    s = _dot_nt(keys, q_head)
    return [s[c * t:(c + 1) * t] for c in range(n)]


def _group_keys(k_ref, g, n_tiles):
    start = pl.multiple_of(g * GROUP_KEYS, GROUP_KEYS)
    return k_ref[0, pl.ds(start, n_tiles * ATT_TILE), :]


def _sweep(qi, group, n_far_groups=None):
    g_own = qi // KEY_GROUP
    own = [functools.partial(group, g_own, None, n + 1, True) for n in range(KEY_GROUP)]
    state = lax.switch(qi % KEY_GROUP, own)
    n_past = g_own if n_far_groups is None else jnp.minimum(g_own, n_far_groups)
    return lax.fori_loop(0, n_past, lambda n, st: group(g_own - 1 - n, st, KEY_GROUP, False), state)


def _tile_iotas():
    t = ATT_TILE
    return lax.broadcasted_iota(jnp.int32, (t, t), 0), lax.broadcasted_iota(jnp.int32, (t, t), 1)


def _store_step_output(o_ref, outs_t):
    o_ref[0] = jnp.concatenate(outs_t, axis=0).T.astype(BF16)


def _softmax_group(s_tiles, ons, vt_h, acc_ref, h, stats, first):
    maxes = []
    for s, on in zip(s_tiles, ons):
        mx = jnp.max(s, axis=0, keepdims=True)
        maxes.append(mx if on is None else jnp.where(on, mx, -jnp.inf))
    m_new = functools.reduce(jnp.maximum, maxes)
    if not first:
        m_old, l_old = stats
        m_new = jnp.maximum(m_new, m_old)
    ps = []
    l_add = None
    for s, on in zip(s_tiles, ons):
        p = jnp.exp(s - (m_new if on is None else jnp.where(on, m_new, jnp.inf)))
        p_sum = jnp.sum(p, axis=0, keepdims=True)
        l_add = p_sum if l_add is None else l_add + p_sum
        ps.append(p.astype(BF16))
    pv = _dot(vt_h, jnp.concatenate(ps, axis=0))
    if first:
        acc_ref[h] = pv
        return m_new, l_add
    alpha = jnp.exp(m_old - m_new)
    acc_ref[h] = alpha * acc_ref[h] + pv
    return m_new, alpha * l_old + l_add


def _moba_select(qh, km_ref, sel_ref, qi):
    t = ATT_TILE
    km_hi, km_lo = _split_bf16(km_ref[0])
    nb = km_hi.shape[0]
    blk = lax.broadcasted_iota(jnp.int32, (nb, t), 0)
    past = blk < qi
    for h in range(HEADS_PER_STEP):
        gate = _dot_nt(_pair_lanes(km_hi, h), qh[h]) + _dot_nt(_pair_lanes(km_lo, h), qh[h])
        gate = jnp.where(past, gate, -jnp.inf)
        beaten = jnp.zeros((nb, t), F32)
        for other in range(nb):
            row = gate[other:other + 1, :]
            wins = jnp.where(row > gate, 1.0, jnp.where(row == gate, jnp.where(blk > other, 1.0, 0.0), 0.0))
            beaten = beaten + wins
        sel_ref[h] = jnp.where(past, jnp.where(beaten < MOBA_TOPK, 1.0, 0.0), 0.0)


def _toeplitz_attn_kernel(*refs, moba):
    if moba:
        q_ref, k_ref, vt_ref, bias_ref, km_ref, o_ref, acc_ref, sel_ref = refs
    else:
        q_ref, k_ref, vt_ref, bias_ref, o_ref, acc_ref = refs
    qi = pl.program_id(2)
    qh = _split_heads(q_ref[0])
    n_bias = bias_ref.shape[1]
    if moba:
        _moba_select(qh, km_ref, sel_ref, qi)

    def group(g, stats, n_tiles, first):
        kg = _group_keys(k_ref, g, n_tiles)
        new = ()
        scores = [_tile_scores(_pair_lanes(kg, h), qh[h], per_tile=False) for h in range(HEADS_PER_STEP)]
        for h in range(HEADS_PER_STEP):
            s_tiles = scores[h]
            ons = []
            for c in range(n_tiles):
                j = g * KEY_GROUP + c
                if first:
                    dt = n_tiles - 1 - c
                    s_tiles[c] = s_tiles[c] + bias_ref[h, dt]
                    ons.append(sel_ref[h, pl.ds(j, 1), :] > 0.5 if moba and dt > 0 else None)
                else:
                    dt = qi - j
                    s_tiles[c] = s_tiles[c] + bias_ref[h, jnp.minimum(dt, n_bias - 1)]
                    ons.append(sel_ref[h, pl.ds(j, 1), :] > 0.5 if moba else dt <= DIL_MAX_TILE_DIST)
            new += _softmax_group(s_tiles, ons, _head_rows(vt_ref, g, h, n_tiles), acc_ref, h,
                                  None if first else stats[2 * h:2 * h + 2], first)
        return new

    if moba:
        n_far = None
    else:
        n_far = qi // KEY_GROUP - jnp.maximum(qi - DIL_MAX_TILE_DIST, 0) // KEY_GROUP
    stats = _sweep(qi, group, n_far)
    _store_step_output(o_ref, [acc_ref[h] / stats[2 * h + 1] for h in range(HEADS_PER_STEP)])


def _toeplitz_attn(proj, vt, bias, km, batch, seq, col0, moba):
    t = ATT_TILE
    n_steps = N_HEADS_A // HEADS_PER_STEP
    n_bias = bias.shape[1]
    in_specs = [pl.BlockSpec((1, t, STEP_LANES), lambda b, p, i: (b, i, col0 + p)),
                _resident((1, seq, STEP_LANES), lambda b, p, i: (b, 0, col0 + n_steps + p)),
                _resident((1, PAIRS_PER_STEP) + vt.shape[2:], lambda b, p, i: (b, p, 0, 0, 0)),
                _resident((HEADS_PER_STEP, n_bias, t, t), lambda b, p, i: (p, 0, 0, 0))]
    args = [proj, proj, vt, bias]
    scratch = [pltpu.VMEM((HEADS_PER_STEP, HEAD_DIM, t), F32)]
    if moba:
        in_specs.append(pl.BlockSpec((1, seq // MOBA_BLOCK, STEP_LANES), lambda b, p, i: (b, 0, p)))
        args.append(km)
        scratch.append(pltpu.VMEM((HEADS_PER_STEP, seq // MOBA_BLOCK, t), F32))
    return pl.pallas_call(
        functools.partial(_toeplitz_attn_kernel, moba=moba),
        grid=(batch, n_steps, seq // t),
        in_specs=in_specs,
        out_specs=pl.BlockSpec((1, t, STEP_LANES), lambda b, p, i: (b, i, p)),
        out_shape=jax.ShapeDtypeStruct((batch, seq, N_HEADS_A * HEAD_DIM), BF16),
        scratch_shapes=scratch,
        compiler_params=_params(3),
        name="moba_attn" if moba else "dilated_attn",
    )(*args)


def _odd_proj_kernel(x_ref, mod_ref, g_ref, win_ref, gq_ref, gkv_ref, wq_ref, wkv_ref,
                     cq_ref, sq_ref, ck_ref, sk_ref, qm_ref, km_ref, vtm_ref, sb_ref, vts_ref):
    d = D_MODEL
    mod = mod_ref[0]
    h = _prenorm(x_ref[...], g_ref[...], mod[:, d:2 * d], mod[:, 0:d])
    p = _dot(h.astype(BF16), win_ref[...])
    o = MLA_Q_RANK + MLA_KV_RANK
    c_q, c_kv = p[:, 0:MLA_Q_RANK], p[:, MLA_Q_RANK:o]
    k_rope, k_rope_swapped = p[:, o:o + LANES], p[:, o + LANES:o + 2 * LANES]
    sb0 = o + 2 * LANES
    n_qk = 2 * N_HEADS_D * HEAD_DIM
    sb_ref[...] = p[:, sb0:sb0 + n_qk].astype(BF16)
    _store_vt(vts_ref, p[:, sb0 + n_qk:])
    q12 = _dot(_rms(c_q, gq_ref[...]).astype(BF16), wq_ref[...])
    kv = _dot(_rms(c_kv, gkv_ref[...]).astype(BF16), wkv_ref[...])
    k_pe = k_rope * ck_ref[...] + k_rope_swapped * sk_ref[...]
    cq, sq = cq_ref[...], sq_ref[...]
    half = N_HEADS_C * LANES
    for hh in range(N_HEADS_C):
        cols = slice(hh * LANES, (hh + 1) * LANES)
        swapped = slice(half + hh * LANES, half + (hh + 1) * LANES)
        qm_ref[:, cols] = (q12[:, cols] * cq + q12[:, swapped] * sq).astype(BF16)
        km_ref[:, cols] = (kv[:, cols] + k_pe).astype(BF16)
    _store_vt(vtm_ref, kv[:, half:])


def _odd_proj(x, mod, g, w_in, gq, gkv, wq, wkv, tables, seq):
    rows, d = x.shape
    tm = ROW_TILE
    per_seq = seq // tm
    n_in = w_in.shape[1]
    n_sb = 2 * N_HEADS_D * HEAD_DIM
    n_q = N_HEADS_C * LANES
    n_pairs = N_HEADS_C // 2
    const = lambda i: (0, 0)
    table_spec = pl.BlockSpec((tm, LANES), lambda i: (i % per_seq, 0))
    vt_spec = _vt_spec(tm, per_seq, n_pairs)
    vt_shape = _vt_shape(rows // seq, seq, n_pairs)
    return pl.pallas_call(
        _odd_proj_kernel,
        grid=(rows // tm,),
        in_specs=[pl.BlockSpec((tm, d), lambda i: (i, 0)),
                  pl.BlockSpec((1, 1, 6 * d), lambda i: (i // per_seq, 0, 0)),
                  pl.BlockSpec((1, d), const),
                  pl.BlockSpec((d, n_in), const),
                  pl.BlockSpec((1, MLA_Q_RANK), const),
                  pl.BlockSpec((1, MLA_KV_RANK), const),
                  pl.BlockSpec(wq.shape, const),
                  pl.BlockSpec(wkv.shape, const),
                  table_spec, table_spec, table_spec, table_spec],
        out_specs=[pl.BlockSpec((tm, n_q), lambda i: (i, 0)),
                   pl.BlockSpec((tm, n_q), lambda i: (i, 0)),
                   vt_spec,
                   pl.BlockSpec((tm, n_sb), lambda i: (i, 0)),
                   vt_spec],
        out_shape=[jax.ShapeDtypeStruct((rows, n_q), BF16),
                   jax.ShapeDtypeStruct((rows, n_q), BF16),
                   vt_shape,
                   jax.ShapeDtypeStruct((rows, n_sb), BF16),
                   vt_shape],
        compiler_params=_params(1),
        name="odd_proj",
    )(x, mod, g, w_in, gq, gkv, wq, wkv, *tables)


def _mla_attn_kernel(q_ref, k_ref, vt_ref, o_ref, acc_ref):
    qi = pl.program_id(2)
    q_step = q_ref[0]
    qh = [q_step[:, h * LANES:(h + 1) * LANES] for h in range(HEADS_PER_STEP)]

    def group(g, stats, n_tiles, first):
        kg = _group_keys(k_ref, g, n_tiles)
        new = ()
        scores = [_tile_scores(kg[:, h * LANES:(h + 1) * LANES], qh[h], per_tile=True)
                  for h in range(HEADS_PER_STEP)]
        for h in range(HEADS_PER_STEP):
            s_tiles = scores[h]
            if first:
                key, query = _tile_iotas()
                s_tiles[-1] = jnp.where(key <= query, s_tiles[-1], -jnp.inf)
            new += _softmax_group(s_tiles, [None] * n_tiles, _head_rows(vt_ref, g, h, n_tiles), acc_ref, h,
                                  None if first else stats[2 * h:2 * h + 2], first)
        return new

    stats = _sweep(qi, group)
    _store_step_output(o_ref, [acc_ref[h] / stats[2 * h + 1] for h in range(HEADS_PER_STEP)])


def _mla_attn(qm, km, vt, batch, seq):
    t = ATT_TILE
    n_steps = N_HEADS_C // HEADS_PER_STEP
    qk_lanes = HEADS_PER_STEP * LANES
    return pl.pallas_call(
        _mla_attn_kernel,
        grid=(batch, n_steps, seq // t),
        in_specs=[pl.BlockSpec((1, t, qk_lanes), lambda b, p, i: (b, i, p)),
                  _resident((1, seq, qk_lanes), lambda b, p, i: (b, 0, p)),
                  _resident((1, PAIRS_PER_STEP) + vt.shape[2:], lambda b, p, i: (b, p, 0, 0, 0))],
        out_specs=pl.BlockSpec((1, t, STEP_LANES), lambda b, p, i: (b, i, p)),
        out_shape=jax.ShapeDtypeStruct((batch, seq, N_HEADS_C * MLA_V_DIM), BF16),
        scratch_shapes=[pltpu.VMEM((HEADS_PER_STEP, HEAD_DIM, t), F32)],
        compiler_params=_params(3),
        name="mla_attn",
    )(qm, km, vt)


def _stick_attn_kernel(q_ref, k_ref, vt_ref, o_ref, acc_ref):
    t = ATT_TILE
    qi = pl.program_id(2)
    qh = _split_heads(q_ref[0])
    key = lax.broadcasted_iota(jnp.int32, (t, t), 0)
    other = lax.broadcasted_iota(jnp.int32, (t, t), 1)
    from_here = jnp.where(other >= key, 1.0, 0.0).astype(BF16)

    strict = key < other

    def group(g, carries, n_tiles, first):
        kg = _group_keys(k_ref, g, n_tiles)
        new = ()
        scores = [_tile_scores(_pair_lanes(kg, h), qh[h], per_tile=False) for h in range(HEADS_PER_STEP)]
        for h in range(HEADS_PER_STEP):
            z_tiles = scores[h]
            keep_tiles = []
            for z in z_tiles:
                neg_z = -z
                keep_tiles.append(jnp.minimum(neg_z, 0.0) - jnp.log(1.0 + jnp.exp(jnp.minimum(z, neg_z))))
            if first:
                keep_tiles[-1] = jnp.where(strict, keep_tiles[-1], 0.0)
            run = None if first else carries[h]
            a_tiles = [None] * n_tiles
            for c in reversed(range(n_tiles)):
                keep_hi, keep_lo = _split_bf16(keep_tiles[c])
                tail = _dot(from_here, keep_hi) + _dot(from_here, keep_lo)
                tile_total = tail[0:1, :]
                if run is not None:
                    tail = tail + run
                a_tiles[c] = jnp.exp(z_tiles[c] + tail)
                run = tile_total if run is None else run + tile_total
            if first:
                a_tiles[-1] = jnp.where(strict, a_tiles[-1], 0.0)
            pv = _dot(_head_rows(vt_ref, g, h, n_tiles), jnp.concatenate(a_tiles, axis=0).astype(BF16))
            acc_ref[h] = pv if first else acc_ref[h] + pv
            new += (run,)
        return new

    _sweep(qi, group)
    _store_step_output(o_ref, [acc_ref[h] for h in range(HEADS_PER_STEP)])


def _stick_attn(sb, vt, batch, seq):
    t = ATT_TILE
    n_steps = N_HEADS_D // HEADS_PER_STEP
    return pl.pallas_call(
        _stick_attn_kernel,
        grid=(batch, n_steps, seq // t),
        in_specs=[pl.BlockSpec((1, t, STEP_LANES), lambda b, p, i: (b, i, p)),
                  _resident((1, seq, STEP_LANES), lambda b, p, i: (b, 0, n_steps + p)),
                  _resident((1, PAIRS_PER_STEP) + vt.shape[2:], lambda b, p, i: (b, p, 0, 0, 0))],
        out_specs=pl.BlockSpec((1, t, STEP_LANES), lambda b, p, i: (b, i, p)),
        out_shape=jax.ShapeDtypeStruct((batch, seq, N_HEADS_D * HEAD_DIM), BF16),
        scratch_shapes=[pltpu.VMEM((HEADS_PER_STEP, HEAD_DIM, t), F32)],
        compiler_params=_params(3),
        name="stick_attn",
    )(sb, sb, vt)


def _mix_out_kernel(oa_ref, ob_ref, x_ref, mod_ref, g_ref, w_ref, o_ref):
    d = D_MODEL
    half = oa_ref.shape[1]
    y = _dot(oa_ref[...], w_ref[0:half, :]) + _dot(ob_ref[...], w_ref[half:2 * half, :])
    gate = mod_ref[0][:, 2 * d:3 * d]
    o_ref[...] = x_ref[...] + gate * _rms(y, g_ref[...])


def _mix_out(oa, ob, x, mod, g, w_out, seq):
    rows, d = x.shape
    tm = ROW_TILE
    per_seq = seq // tm
    half = oa.shape[1]
    return pl.pallas_call(
        _mix_out_kernel,
        grid=(rows // tm,),
        in_specs=[pl.BlockSpec((tm, half), lambda i: (i, 0)),
                  pl.BlockSpec((tm, half), lambda i: (i, 0)),
                  pl.BlockSpec((tm, d), lambda i: (i, 0)),
                  pl.BlockSpec((1, 1, 6 * d), lambda i: (i // per_seq, 0, 0)),
                  pl.BlockSpec((1, d), lambda i: (0, 0)),
                  pl.BlockSpec((2 * half, d), lambda i: (0, 0))],
        out_specs=pl.BlockSpec((tm, d), lambda i: (i, 0)),
        out_shape=jax.ShapeDtypeStruct((rows, d), F32),
        compiler_params=_params(1),
        name="mix_out",
    )(oa, ob, x, mod, g, w_out)


def _ffn_kernel(x_ref, xh_ref, mod_ref, gpre_ref, gpost_ref, wup_ref, cw_ref, cb_ref, wd_ref, o_ref,
                h_ref, u_ref, acc_ref, *, per_seq):
    d = D_MODEL
    tm = ROW_TILE
    halo = CONV_HALO
    cw = FF_CHUNK
    i = pl.program_id(0)
    mod = mod_ref[0]
    shift, scale = mod[:, 3 * d:4 * d], mod[:, 4 * d:5 * d]
    g = gpre_ref[...]
    ahead = _prenorm(xh_ref[...], g, scale, shift)
    ahead = jnp.where(i % per_seq == 0, 0.0, ahead)
    h_ref[0:halo, :] = ahead.astype(BF16)
    h_ref[halo:halo + tm, :] = _prenorm(x_ref[...], g, scale, shift).astype(BF16)

    def up(ch):
        slot = ch % 2
        for half in range(2):
            cols = slice(half * D_FF + ch * cw, half * D_FF + (ch + 1) * cw)
            u_ref[slot, half] = _dot(h_ref[...], wup_ref[:, cols])

    def conv(slot, half, ch):
        cols = slice(half * D_FF + ch * cw, half * D_FF + (ch + 1) * cw)
        w = cw_ref[:, cols]
        out = w[0:1, :] * u_ref[slot, half, halo - 2:halo - 2 + tm, :]
        out = out + w[1:2, :] * u_ref[slot, half, halo - 1:halo - 1 + tm, :]
        out = out + w[2:3, :] * u_ref[slot, half, halo:halo + tm, :]
        return out + cb_ref[:, cols]

    up(0)
    for ch in range(N_FF_CHUNKS):
        if ch + 1 < N_FF_CHUNKS:
            up(ch + 1)
        slot = ch % 2
        act = jax.nn.gelu(conv(slot, 0, ch), approximate=True) * conv(slot, 1, ch)
        down = _dot(act.astype(BF16), wd_ref[ch * cw:(ch + 1) * cw, :])
        if ch == 0:
            acc_ref[...] = down
        else:
            acc_ref[...] += down

    gate_f = mod[:, 5 * d:6 * d]
    o_ref[...] = x_ref[...] + gate_f * _rms(acc_ref[...], gpost_ref[...])


def _ffn(x, mod, g_pre, g_post, w_up, conv_w, conv_b, w_down, seq):
    rows, d = x.shape
    tm = ROW_TILE
    halo = CONV_HALO
    per_seq = seq // tm
    const = lambda i: (0, 0)
    return pl.pallas_call(
        functools.partial(_ffn_kernel, per_seq=per_seq),
        grid=(rows // tm,),
        in_specs=[pl.BlockSpec((tm, d), lambda i: (i, 0)),
                  pl.BlockSpec((halo, d), lambda i: (jnp.maximum(i * (tm // halo) - 1, 0), 0)),
                  pl.BlockSpec((1, 1, 6 * d), lambda i: (i // per_seq, 0, 0)),
                  pl.BlockSpec((1, d), const),
                  pl.BlockSpec((1, d), const),
                  pl.BlockSpec(w_up.shape, const, pipeline_mode=pl.Buffered(1)),
                  pl.BlockSpec(conv_w.shape, const),
                  pl.BlockSpec(conv_b.shape, const),
                  pl.BlockSpec(w_down.shape, const, pipeline_mode=pl.Buffered(1))],
        out_specs=pl.BlockSpec((tm, d), lambda i: (i, 0)),
        out_shape=jax.ShapeDtypeStruct((rows, d), F32),
        scratch_shapes=[pltpu.VMEM((tm + halo, d), BF16),
                        pltpu.VMEM((2, 2, tm + halo, FF_CHUNK), F32),
                        pltpu.VMEM((tm, d), F32)],
        compiler_params=_params(1),
        name="conv_ffn",
    )(x, x, mod, g_pre, g_post, w_up, conv_w, conv_b, w_down)


def _rotate_half_cols(w):
    half = w.shape[-1] // 2
    return jnp.concatenate([-w[..., half:], w[..., :half]], axis=-1)


def _pad_cols(w, left, total):
    return jnp.pad(w, ((0, 0), (left, total - left - w.shape[1])))


def _rope_tables(seq):
    inv_freq = 1.0 / (ROPE_THETA ** (jnp.arange(0, MLA_ROPE_DIM, 2, dtype=F32) / MLA_ROPE_DIM))
    ang = jnp.arange(seq, dtype=F32)[:, None] * inv_freq[None, :]
    cos, sin = jnp.cos(ang), jnp.sin(ang)
    cos2 = _pad_cols(jnp.concatenate([cos, cos], axis=1), MLA_NOPE_DIM, LANES)
    sin2 = _pad_cols(jnp.concatenate([sin, sin], axis=1), MLA_NOPE_DIM, LANES)
    scale = (MLA_NOPE_DIM + MLA_ROPE_DIM) ** -0.5
    nope_ones = _pad_cols(jnp.ones((seq, MLA_NOPE_DIM), F32), 0, LANES)
    return (scale * (cos2 + nope_ones), scale * sin2, cos2, sin2)


def _odd_weights(w_in, w_uq, w_ukv):
    o = MLA_Q_RANK + MLA_KV_RANK
    w_rope = w_in[:, o:o + MLA_ROPE_DIM]
    scale_d = HEAD_DIM ** -0.5
    wd = N_HEADS_D * HEAD_DIM
    sb0 = o + MLA_ROPE_DIM
    w_in2 = jnp.concatenate([
        w_in[:, :o],
        _pad_cols(w_rope, MLA_NOPE_DIM, LANES),
        _pad_cols(_rotate_half_cols(w_rope), MLA_NOPE_DIM, LANES),
        w_in[:, sb0:sb0 + wd] * scale_d,
        w_in[:, sb0 + wd:],
    ], axis=1).astype(BF16)
    qd = MLA_NOPE_DIM + MLA_ROPE_DIM
    uq = w_uq.reshape(MLA_Q_RANK, N_HEADS_C, qd)
    plain = jnp.pad(uq, ((0, 0), (0, 0), (0, LANES - qd)))
    swapped = jnp.pad(_rotate_half_cols(uq[..., MLA_NOPE_DIM:]),
                      ((0, 0), (0, 0), (MLA_NOPE_DIM, LANES - qd)))
    wq = jnp.concatenate([plain.reshape(MLA_Q_RANK, -1), swapped.reshape(MLA_Q_RANK, -1)], axis=1).astype(BF16)
    ukv = w_ukv.reshape(MLA_KV_RANK, N_HEADS_C, MLA_NOPE_DIM + MLA_V_DIM)
    k_nope = jnp.pad(ukv[..., :MLA_NOPE_DIM], ((0, 0), (0, 0), (0, LANES - MLA_NOPE_DIM)))
    v = ukv[..., MLA_NOPE_DIM:]
    wkv = jnp.concatenate([k_nope.reshape(MLA_KV_RANK, -1), v.reshape(MLA_KV_RANK, -1)], axis=1).astype(BF16)
    return w_in2, wq, wkv


def _even_weights(w_in):
    wa = N_HEADS_A * HEAD_DIM
    scale = HEAD_DIM ** -0.5
    col = jnp.arange(w_in.shape[1])
    is_q = (col < wa) | ((col >= 3 * wa) & (col < 4 * wa))
    return (w_in * jnp.where(is_q, scale, 1.0)).astype(BF16)


def kernel(x, c, rel_bias, ada_w, ada_b, mix_pre_g, mix_post_g, ffn_pre_g, ffn_post_g, ab_w_in, ab_w_out,
           cd_w_in, mla_q_norm_g, mla_kv_norm_g, mla_w_uq, mla_w_ukv, cd_w_out, ffn_w_up, ffn_conv_w,
           ffn_conv_b, ffn_w_down):
    batch, seq, d = x.shape
    assert d == D_MODEL and seq % GROUP_KEYS == 0 and GROUP_KEYS % ROW_TILE == 0
    rows = batch * seq
    xf = x.reshape(rows, d)

    mods = _mods(c, ada_w, ada_b)
    bias_a = _bias_tiles(rel_bias, 0, N_HEADS_A, MOBA_BIAS_TILES, dilated=False)
    bias_b = _bias_tiles(rel_bias, N_HEADS_A, N_HEADS_B, DIL_BIAS_TILES, dilated=True)
    tables = _rope_tables(seq)
    step_blocks = N_HEADS_A // HEADS_PER_STEP

    for layer in range(DEPTH):
        mod = mods[layer].reshape(batch, 1, 6 * d)
        i = layer // 2
        if layer % 2 == 0:
            proj, km, vta, vtb = _even_proj(xf, mod, mix_pre_g[layer].reshape(1, d),
                                            _even_weights(ab_w_in[i]), seq)
            proj = proj.reshape(batch, seq, -1)
            km = km.reshape(batch, seq // MOBA_BLOCK, -1)
            o_first = _toeplitz_attn(proj, vta, bias_a, km, batch, seq, 0, moba=True)
            o_second = _toeplitz_attn(proj, vtb, bias_b, None, batch, seq, 3 * step_blocks, moba=False)
            w_out = ab_w_out[i]
        else:
            w_in2, wq, wkv = _odd_weights(cd_w_in[i], mla_w_uq[i], mla_w_ukv[i])
            qm, km, vtm, sb, vts = _odd_proj(xf, mod, mix_pre_g[layer].reshape(1, d), w_in2,
                                             mla_q_norm_g[i].reshape(1, -1), mla_kv_norm_g[i].reshape(1, -1),
                                             wq, wkv, tables, seq)
            o_first = _mla_attn(qm.reshape(batch, seq, -1), km.reshape(batch, seq, -1), vtm, batch, seq)
            o_second = _stick_attn(sb.reshape(batch, seq, -1), vts, batch, seq)
            w_out = cd_w_out[i]
        xf = _mix_out(o_first.reshape(rows, -1), o_second.reshape(rows, -1), xf, mod,
                      mix_post_g[layer].reshape(1, d), w_out.astype(BF16), seq)
        xf = _ffn(xf, mod, ffn_pre_g[layer].reshape(1, d), ffn_post_g[layer].reshape(1, d),
                  ffn_w_up[layer].astype(BF16), ffn_conv_w[layer], ffn_conv_b[layer].reshape(1, -1),
                  ffn_w_down[layer].astype(BF16), seq)
    return xf.reshape(batch, seq, d)
```

```python
import functools
import math

import jax
import jax.numpy as jnp
from jax import lax
from jax.experimental import pallas as pl
from jax.experimental.pallas import tpu as pltpu

F32 = jnp.float32
BF16 = jnp.bfloat16

D_MODEL = 1024
DEPTH = 4
HEAD_DIM = 64
N_HEADS_A = 8
N_HEADS_B = 8
N_HEADS_C = 8
N_HEADS_D = 8
MOBA_BLOCK = 256
MOBA_TOPK = 3
DILATED_BRANCHES = ((128, 1), (512, 4), (2048, 16))
MLA_Q_RANK = 256
MLA_KV_RANK = 256
MLA_NOPE_DIM = 64
MLA_ROPE_DIM = 32
MLA_V_DIM = 64
ROPE_THETA = 10000.0
REL_BUCKETS = 32
REL_MAX_DIST = 2048
D_FF = 2816
CONV_WIDTH = 3
NORM_EPS = 1e-6

LANES = 128
SUBLANES = 8
BF16_SUBLANES = 16
VMEM_LIMIT_BYTES = 56 * 1024 * 1024

ATT_TILE = MOBA_BLOCK
KEY_GROUP = 4
GROUP_KEYS = KEY_GROUP * ATT_TILE
PAIRS_PER_STEP = 2
HEADS_PER_STEP = 2 * PAIRS_PER_STEP
STEP_LANES = PAIRS_PER_STEP * LANES
ROW_TILE = 512
FF_CHUNK = 256
N_FF_CHUNKS = D_FF // FF_CHUNK
CONV_HALO = BF16_SUBLANES
MOBA_BIAS_TILES = REL_MAX_DIST // ATT_TILE + 2
DIL_MAX_TILE_DIST = DILATED_BRANCHES[-1][0] // ATT_TILE
DIL_BIAS_TILES = DIL_MAX_TILE_DIST + 1

_NT = (((1,), (1,)), ((), ()))


def _bucket_lower_bounds():
    max_exact = REL_BUCKETS // 2
    ratio = REL_MAX_DIST // max_exact
    n_log = REL_BUCKETS - max_exact
    lows = list(range(max_exact + 1))
    for k in range(1, n_log):
        d = lows[-1]
        while d ** n_log < (max_exact ** n_log) * (ratio ** k):
            d += 1
        lows.append(d)
    return lows


_BUCKET_LOW = _bucket_lower_bounds()


def _dot(a, b):
    return jnp.dot(a, b, preferred_element_type=F32)


def _dot_nt(a, b):
    return lax.dot_general(a, b, _NT, preferred_element_type=F32)


def _split_bf16(x):
    hi = x.astype(BF16)
    lo = (x - hi.astype(F32)).astype(BF16)
    return hi, lo


def _rms(x, g):
    return (x * lax.rsqrt(jnp.mean(x * x, axis=-1, keepdims=True) + NORM_EPS)) * g


def _prenorm(x, g, scale, shift):
    return _rms(x, g) * (1.0 + scale) + shift


def _resident(block_shape, index_map):
    return pl.BlockSpec(block_shape, index_map)


def _params(n_grid_dims):
    return pltpu.CompilerParams(dimension_semantics=("arbitrary",) * n_grid_dims,
                                vmem_limit_bytes=VMEM_LIMIT_BYTES)


def _mods_kernel(c_ref, w_ref, b_ref, o_ref):
    c = c_ref[...]
    cond = c * jax.nn.sigmoid(c)
    c_hi, c_lo = _split_bf16(cond)
    w_hi, w_lo = _split_bf16(w_ref[0])
    o_ref[0] = _dot(c_hi, w_hi) + _dot(c_hi, w_lo) + _dot(c_lo, w_hi) + b_ref[0]


def _mods(c, ada_w, ada_b):
    b, d = c.shape
    rows = BF16_SUBLANES
    n_out = ada_w.shape[-1]
    tn = n_out // 4
    c_pad = jnp.zeros((rows, d), F32).at[:b].set(c)
    out = pl.pallas_call(
        _mods_kernel,
        grid=(DEPTH, n_out // tn),
        in_specs=[pl.BlockSpec((rows, d), lambda l, j: (0, 0)),
                  pl.BlockSpec((1, d, tn), lambda l, j: (l, 0, j)),
                  pl.BlockSpec((1, 1, tn), lambda l, j: (l, 0, j))],
        out_specs=pl.BlockSpec((1, rows, tn), lambda l, j: (l, 0, j)),
        out_shape=jax.ShapeDtypeStruct((DEPTH, rows, n_out), F32),
        compiler_params=_params(2),
        name="ada_mods",
    )(c_pad, ada_w, ada_b.reshape(DEPTH, 1, n_out))
    return out[:, :b]


def _bias_tiles_kernel(tab_ref, o_ref, *, head_off, dilated):
    h = pl.program_id(0) + head_off
    d = pl.program_id(1)
    t = ATT_TILE
    dist = d * t + lax.broadcasted_iota(jnp.int32, (SUBLANES, 2 * t), 1) - t
    val = jnp.full(dist.shape, tab_ref[h, 0], F32)
    for b in range(1, REL_BUCKETS):
        val = jnp.where(dist >= _BUCKET_LOW[b], tab_ref[h, b], val)
    if dilated:
        mult = jnp.zeros(dist.shape, F32)
        for window, dil in DILATED_BRANCHES:
            hit = jnp.where(dist <= window, jnp.where((dist & (dil - 1)) == 0, 1.0, 0.0), 0.0)
            mult = mult + hit
        log_mult = jnp.where(mult > 2.5, math.log(3.0), jnp.where(mult > 1.5, math.log(2.0), 0.0))
        val = jnp.where(mult > 0.5, val + log_mult, -jnp.inf)
    val = jnp.where(dist >= 0, val, -jnp.inf)
    strip = jnp.concatenate([val] * (t // SUBLANES), axis=0)
    rotated = pltpu.roll(strip, 0, 1, stride=1, stride_axis=0)
    o_ref[0, 0] = rotated[:, t:2 * t]


def _bias_tiles(rel_bias, head_off, n_heads, n_tiles, dilated):
    t = ATT_TILE
    return pl.pallas_call(
        functools.partial(_bias_tiles_kernel, head_off=head_off, dilated=dilated),
        grid=(n_heads, n_tiles),
        in_specs=[pl.BlockSpec(memory_space=pltpu.SMEM)],
        out_specs=pl.BlockSpec((1, 1, t, t), lambda h, d: (h, d, 0, 0)),
        out_shape=jax.ShapeDtypeStruct((n_heads, n_tiles, t, t), F32),
        compiler_params=_params(2),
        name="dilated_bias_tiles" if dilated else "moba_bias_tiles",
    )(rel_bias)


def _store_vt(vt_ref, v):
    vt = v.T.astype(BF16)
    for p in range(vt.shape[0] // LANES):
        vt_ref[0, p, 0] = vt[p * LANES:(p + 1) * LANES, :]


def _vt_spec(tm, per_seq, n_pairs):
    per_group = GROUP_KEYS // tm
    return pl.BlockSpec((1, n_pairs, 1, LANES, tm),
                        lambda i: (i // per_seq, 0, (i % per_seq) // per_group, 0, (i % per_seq) % per_group))


def _vt_shape(batch, seq, n_pairs):
    return jax.ShapeDtypeStruct((batch, n_pairs, seq // GROUP_KEYS, LANES, GROUP_KEYS), BF16)


def _even_proj_kernel(x_ref, mod_ref, g_ref, w_ref, o_ref, km_ref, vta_ref, vtb_ref):
    d = D_MODEL
    mod = mod_ref[0]
    h = _prenorm(x_ref[...], g_ref[...], mod[:, d:2 * d], mod[:, 0:d])
    p = _dot(h.astype(BF16), w_ref[...])
    o_ref[...] = p.astype(BF16)
    wa = N_HEADS_A * HEAD_DIM
    ka = p[:, wa:2 * wa]
    nb = ka.shape[0] // MOBA_BLOCK
    km_ref[0] = jnp.mean(ka.reshape(nb, MOBA_BLOCK, wa), axis=1)
    _store_vt(vta_ref, p[:, 2 * wa:3 * wa])
    _store_vt(vtb_ref, p[:, 5 * wa:6 * wa])


def _even_proj(x, mod, g, w_in, seq):
    rows, d = x.shape
    tm = ROW_TILE
    n = w_in.shape[1]
    wa = N_HEADS_A * HEAD_DIM
    per_seq = seq // tm
    n_pairs = N_HEADS_A // 2
    vt_spec = _vt_spec(tm, per_seq, n_pairs)
    vt_shape = _vt_shape(rows // seq, seq, n_pairs)
    return pl.pallas_call(
        _even_proj_kernel,
        grid=(rows // tm,),
        in_specs=[pl.BlockSpec((tm, d), lambda i: (i, 0)),
                  pl.BlockSpec((1, 1, 6 * d), lambda i: (i // per_seq, 0, 0)),
                  pl.BlockSpec((1, d), lambda i: (0, 0)),
                  pl.BlockSpec((d, n), lambda i: (0, 0))],
        out_specs=[pl.BlockSpec((tm, n), lambda i: (i, 0)),
                   pl.BlockSpec((1, tm // MOBA_BLOCK, wa), lambda i: (i, 0, 0)),
                   vt_spec, vt_spec],
        out_shape=[jax.ShapeDtypeStruct((rows, n), BF16),
                   jax.ShapeDtypeStruct((rows // tm, tm // MOBA_BLOCK, wa), F32),
                   vt_shape, vt_shape],
        compiler_params=_params(1),
        name="even_proj",
    )(x, mod, g, w_in)


def _pair_lanes(x, h):
    pair = h // 2
    return x[:, pair * LANES:(pair + 1) * LANES]


def _split_heads(q_step):
    first = lax.broadcasted_iota(jnp.int32, (q_step.shape[0], LANES), 1) < HEAD_DIM
    heads = []
    for pair in range(PAIRS_PER_STEP):
        q2 = q_step[:, pair * LANES:(pair + 1) * LANES]
        zero = jnp.zeros_like(q2)
        heads += [jnp.where(first, q2, zero), jnp.where(first, zero, q2)]
    return heads


def _head_rows(vt_ref, g, h, n_tiles):
    return vt_ref[0, h // 2, g, (h % 2) * HEAD_DIM:(h % 2 + 1) * HEAD_DIM, 0:n_tiles * ATT_TILE]


def _tile_scores(keys, q_head, per_tile):
    t = ATT_TILE
    n = keys.shape[0] // t
    if per_tile:
        return [_dot_nt(keys[c * t:(c + 1) * t], q_head) for c in range(n)]
    s = _dot_nt(keys, q_head)
    return [s[c * t:(c + 1) * t] for c in range(n)]


def _group_keys(k_ref, g, n_tiles):
    start = pl.multiple_of(g * GROUP_KEYS, GROUP_KEYS)
    return k_ref[0, pl.ds(start, n_tiles * ATT_TILE), :]


def _sweep(qi, group, n_far_groups=None):
    g_own = qi // KEY_GROUP
    own = [functools.partial(group, g_own, None, n + 1, True) for n in range(KEY_GROUP)]
    state = lax.switch(qi % KEY_GROUP, own)
    n_past = g_own if n_far_groups is None else jnp.minimum(g_own, n_far_groups)
    return lax.fori_loop(0, n_past, lambda n, st: group(g_own - 1 - n, st, KEY_GROUP, False), state)


def _tile_iotas():
    t = ATT_TILE
    return lax.broadcasted_iota(jnp.int32, (t, t), 0), lax.broadcasted_iota(jnp.int32, (t, t), 1)


def _store_step_output(o_ref, outs_t):
    o_ref[0] = jnp.concatenate(outs_t, axis=0).T.astype(BF16)


def _softmax_group(s_tiles, ons, vt_h, acc_ref, h, stats, first):
    maxes = []
    for s, on in zip(s_tiles, ons):
        mx = jnp.max(s, axis=0, keepdims=True)
        maxes.append(mx if on is None else jnp.where(on, mx, -jnp.inf))
    m_new = functools.reduce(jnp.maximum, maxes)
    if not first:
        m_old, l_old = stats
        m_new = jnp.maximum(m_new, m_old)
    ps = []
    l_add = None
    for s, on in zip(s_tiles, ons):
        p = jnp.exp(s - (m_new if on is None else jnp.where(on, m_new, jnp.inf)))
        p_sum = jnp.sum(p, axis=0, keepdims=True)
        l_add = p_sum if l_add is None else l_add + p_sum
        ps.append(p.astype(BF16))
    pv = _dot(vt_h, jnp.concatenate(ps, axis=0))
    if first:
        acc_ref[h] = pv
        return m_new, l_add
    alpha = jnp.exp(m_old - m_new)
    acc_ref[h] = alpha * acc_ref[h] + pv
    return m_new, alpha * l_old + l_add


def _moba_select(qh, km_ref, sel_ref, qi):
    t = ATT_TILE
    km_hi, km_lo = _split_bf16(km_ref[0])
    nb = km_hi.shape[0]
    blk = lax.broadcasted_iota(jnp.int32, (nb, t), 0)
    past = blk < qi
    for h in range(HEADS_PER_STEP):
        gate = _dot_nt(_pair_lanes(km_hi, h), qh[h]) + _dot_nt(_pair_lanes(km_lo, h), qh[h])
        gate = jnp.where(past, gate, -jnp.inf)
        beaten = jnp.zeros((nb, t), F32)
        for other in range(nb):
            row = gate[other:other + 1, :]
            wins = jnp.where(row > gate, 1.0, jnp.where(row == gate, jnp.where(blk > other, 1.0, 0.0), 0.0))
            beaten = beaten + wins
        sel_ref[h] = jnp.where(past, jnp.where(beaten < MOBA_TOPK, 1.0, 0.0), 0.0)


def _toeplitz_attn_kernel(*refs, moba):
    if moba:
        q_ref, k_ref, vt_ref, bias_ref, km_ref, o_ref, acc_ref, sel_ref = refs
    else:
        q_ref, k_ref, vt_ref, bias_ref, o_ref, acc_ref = refs
    qi = pl.program_id(2)
    qh = _split_heads(q_ref[0])
    n_bias = bias_ref.shape[1]
    if moba:
        _moba_select(qh, km_ref, sel_ref, qi)

    def group(g, stats, n_tiles, first):
        kg = _group_keys(k_ref, g, n_tiles)
        new = ()
        scores = [_tile_scores(_pair_lanes(kg, h), qh[h], per_tile=False) for h in range(HEADS_PER_STEP)]
        for h in range(HEADS_PER_STEP):
            s_tiles = scores[h]
            ons = []
            for c in range(n_tiles):
                j = g * KEY_GROUP + c
                if first:
                    dt = n_tiles - 1 - c
                    s_tiles[c] = s_tiles[c] + bias_ref[h, dt]
                    ons.append(sel_ref[h, pl.ds(j, 1), :] > 0.5 if moba and dt > 0 else None)
                else:
                    dt = qi - j
                    s_tiles[c] = s_tiles[c] + bias_ref[h, jnp.minimum(dt, n_bias - 1)]
                    ons.append(sel_ref[h, pl.ds(j, 1), :] > 0.5 if moba else dt <= DIL_MAX_TILE_DIST)
            new += _softmax_group(s_tiles, ons, _head_rows(vt_ref, g, h, n_tiles), acc_ref, h,
                                  None if first else stats[2 * h:2 * h + 2], first)
        return new

    if moba:
        n_far = None
    else:
        n_far = qi // KEY_GROUP - jnp.maximum(qi - DIL_MAX_TILE_DIST, 0) // KEY_GROUP
    stats = _sweep(qi, group, n_far)
    _store_step_output(o_ref, [acc_ref[h] / stats[2 * h + 1] for h in range(HEADS_PER_STEP)])


def _toeplitz_attn(proj, vt, bias, km, batch, seq, col0, moba):
    t = ATT_TILE
    n_steps = N_HEADS_A // HEADS_PER_STEP
    n_bias = bias.shape[1]
    in_specs = [pl.BlockSpec((1, t, STEP_LANES), lambda b, p, i: (b, i, col0 + p)),
                _resident((1, seq, STEP_LANES), lambda b, p, i: (b, 0, col0 + n_steps + p)),
                _resident((1, PAIRS_PER_STEP) + vt.shape[2:], lambda b, p, i: (b, p, 0, 0, 0)),
                _resident((HEADS_PER_STEP, n_bias, t, t), lambda b, p, i: (p, 0, 0, 0))]
    args = [proj, proj, vt, bias]
    scratch = [pltpu.VMEM((HEADS_PER_STEP, HEAD_DIM, t), F32)]
    if moba:
        in_specs.append(pl.BlockSpec((1, seq // MOBA_BLOCK, STEP_LANES), lambda b, p, i: (b, 0, p)))
        args.append(km)
        scratch.append(pltpu.VMEM((HEADS_PER_STEP, seq // MOBA_BLOCK, t), F32))
    return pl.pallas_call(
        functools.partial(_toeplitz_attn_kernel, moba=moba),
        grid=(batch, n_steps, seq // t),
        in_specs=in_specs,
        out_specs=pl.BlockSpec((1, t, STEP_LANES), lambda b, p, i: (b, i, p)),
        out_shape=jax.ShapeDtypeStruct((batch, seq, N_HEADS_A * HEAD_DIM), BF16),
        scratch_shapes=scratch,
        compiler_params=_params(3),
        name="moba_attn" if moba else "dilated_attn",
    )(*args)


def _odd_proj_kernel(x_ref, mod_ref, g_ref, win_ref, gq_ref, gkv_ref, wq_ref, wkv_ref,
                     cq_ref, sq_ref, ck_ref, sk_ref, qm_ref, km_ref, vtm_ref, sb_ref, vts_ref):
    d = D_MODEL
    mod = mod_ref[0]
    h = _prenorm(x_ref[...], g_ref[...], mod[:, d:2 * d], mod[:, 0:d])
    p = _dot(h.astype(BF16), win_ref[...])
    o = MLA_Q_RANK + MLA_KV_RANK
    c_q, c_kv = p[:, 0:MLA_Q_RANK], p[:, MLA_Q_RANK:o]
    k_rope, k_rope_swapped = p[:, o:o + LANES], p[:, o + LANES:o + 2 * LANES]
    sb0 = o + 2 * LANES
    n_qk = 2 * N_HEADS_D * HEAD_DIM
    sb_ref[...] = p[:, sb0:sb0 + n_qk].astype(BF16)
    _store_vt(vts_ref, p[:, sb0 + n_qk:])
    q12 = _dot(_rms(c_q, gq_ref[...]).astype(BF16), wq_ref[...])
    kv = _dot(_rms(c_kv, gkv_ref[...]).astype(BF16), wkv_ref[...])
    k_pe = k_rope * ck_ref[...] + k_rope_swapped * sk_ref[...]
    cq, sq = cq_ref[...], sq_ref[...]
    half = N_HEADS_C * LANES
    for hh in range(N_HEADS_C):
        cols = slice(hh * LANES, (hh + 1) * LANES)
        swapped = slice(half + hh * LANES, half + (hh + 1) * LANES)
        qm_ref[:, cols] = (q12[:, cols] * cq + q12[:, swapped] * sq).astype(BF16)
        km_ref[:, cols] = (kv[:, cols] + k_pe).astype(BF16)
    _store_vt(vtm_ref, kv[:, half:])


def _odd_proj(x, mod, g, w_in, gq, gkv, wq, wkv, tables, seq):
    rows, d = x.shape
    tm = ROW_TILE
    per_seq = seq // tm
    n_in = w_in.shape[1]
    n_sb = 2 * N_HEADS_D * HEAD_DIM
    n_q = N_HEADS_C * LANES
    n_pairs = N_HEADS_C // 2
    const = lambda i: (0, 0)
    table_spec = pl.BlockSpec((tm, LANES), lambda i: (i % per_seq, 0))
    vt_spec = _vt_spec(tm, per_seq, n_pairs)
    vt_shape = _vt_shape(rows // seq, seq, n_pairs)
    return pl.pallas_call(
        _odd_proj_kernel,
        grid=(rows // tm,),
        in_specs=[pl.BlockSpec((tm, d), lambda i: (i, 0)),
                  pl.BlockSpec((1, 1, 6 * d), lambda i: (i // per_seq, 0, 0)),
                  pl.BlockSpec((1, d), const),
                  pl.BlockSpec((d, n_in), const),
                  pl.BlockSpec((1, MLA_Q_RANK), const),
                  pl.BlockSpec((1, MLA_KV_RANK), const),
                  pl.BlockSpec(wq.shape, const),
                  pl.BlockSpec(wkv.shape, const),
                  table_spec, table_spec, table_spec, table_spec],
        out_specs=[pl.BlockSpec((tm, n_q), lambda i: (i, 0)),
                   pl.BlockSpec((tm, n_q), lambda i: (i, 0)),
                   vt_spec,
                   pl.BlockSpec((tm, n_sb), lambda i: (i, 0)),
                   vt_spec],
        out_shape=[jax.ShapeDtypeStruct((rows, n_q), BF16),
                   jax.ShapeDtypeStruct((rows, n_q), BF16),
                   vt_shape,
                   jax.ShapeDtypeStruct((rows, n_sb), BF16),
                   vt_shape],
        compiler_params=_params(1),
        name="odd_proj",
    )(x, mod, g, w_in, gq, gkv, wq, wkv, *tables)


def _mla_attn_kernel(q_ref, k_ref, vt_ref, o_ref, acc_ref):
    qi = pl.program_id(2)
    q_step = q_ref[0]
    qh = [q_step[:, h * LANES:(h + 1) * LANES] for h in range(HEADS_PER_STEP)]

    def group(g, stats, n_tiles, first):
        kg = _group_keys(k_ref, g, n_tiles)
        new = ()
        scores = [_tile_scores(kg[:, h * LANES:(h + 1) * LANES], qh[h], per_tile=True)
                  for h in range(HEADS_PER_STEP)]
        for h in range(HEADS_PER_STEP):
            s_tiles = scores[h]
            if first:
                key, query = _tile_iotas()
                s_tiles[-1] = jnp.where(key <= query, s_tiles[-1], -jnp.inf)
            new += _softmax_group(s_tiles, [None] * n_tiles, _head_rows(vt_ref, g, h, n_tiles), acc_ref, h,
                                  None if first else stats[2 * h:2 * h + 2], first)
        return new

    stats = _sweep(qi, group)
    _store_step_output(o_ref, [acc_ref[h] / stats[2 * h + 1] for h in range(HEADS_PER_STEP)])


def _mla_attn(qm, km, vt, batch, seq):
    t = ATT_TILE
    n_steps = N_HEADS_C // HEADS_PER_STEP
    qk_lanes = HEADS_PER_STEP * LANES
    return pl.pallas_call(
        _mla_attn_kernel,
        grid=(batch, n_steps, seq // t),
        in_specs=[pl.BlockSpec((1, t, qk_lanes), lambda b, p, i: (b, i, p)),
                  _resident((1, seq, qk_lanes), lambda b, p, i: (b, 0, p)),
                  _resident((1, PAIRS_PER_STEP) + vt.shape[2:], lambda b, p, i: (b, p, 0, 0, 0))],
        out_specs=pl.BlockSpec((1, t, STEP_LANES), lambda b, p, i: (b, i, p)),
        out_shape=jax.ShapeDtypeStruct((batch, seq, N_HEADS_C * MLA_V_DIM), BF16),
        scratch_shapes=[pltpu.VMEM((HEADS_PER_STEP, HEAD_DIM, t), F32)],
        compiler_params=_params(3),
        name="mla_attn",
    )(qm, km, vt)


def _stick_attn_kernel(q_ref, k_ref, vt_ref, o_ref, acc_ref):
    t = ATT_TILE
    qi = pl.program_id(2)
    qh = _split_heads(q_ref[0])
    key = lax.broadcasted_iota(jnp.int32, (t, t), 0)
    other = lax.broadcasted_iota(jnp.int32, (t, t), 1)
    from_here = jnp.where(other >= key, 1.0, 0.0).astype(BF16)

    strict = key < other

    def group(g, carries, n_tiles, first):
        kg = _group_keys(k_ref, g, n_tiles)
        new = ()
        scores = [_tile_scores(_pair_lanes(kg, h), qh[h], per_tile=False) for h in range(HEADS_PER_STEP)]
        for h in range(HEADS_PER_STEP):
            z_tiles = scores[h]
            keep_tiles = []
            for z in z_tiles:
                neg_z = -z
                keep_tiles.append(jnp.minimum(neg_z, 0.0) - jnp.log(1.0 + jnp.exp(jnp.minimum(z, neg_z))))
            if first:
                keep_tiles[-1] = jnp.where(strict, keep_tiles[-1], 0.0)
            run = None if first else carries[h]
            a_tiles = [None] * n_tiles
            for c in reversed(range(n_tiles)):
                keep_hi, keep_lo = _split_bf16(keep_tiles[c])
                tail = _dot(from_here, keep_hi) + _dot(from_here, keep_lo)
                tile_total = tail[0:1, :]
                if run is not None:
                    tail = tail + run
                a_tiles[c] = jnp.exp(z_tiles[c] + tail)
                run = tile_total if run is None else run + tile_total
            if first:
                a_tiles[-1] = jnp.where(strict, a_tiles[-1], 0.0)
            pv = _dot(_head_rows(vt_ref, g, h, n_tiles), jnp.concatenate(a_tiles, axis=0).astype(BF16))
            acc_ref[h] = pv if first else acc_ref[h] + pv
            new += (run,)
        return new

    _sweep(qi, group)
    _store_step_output(o_ref, [acc_ref[h] for h in range(HEADS_PER_STEP)])


def _stick_attn(sb, vt, batch, seq):
    t = ATT_TILE
    n_steps = N_HEADS_D // HEADS_PER_STEP
    return pl.pallas_call(
        _stick_attn_kernel,
        grid=(batch, n_steps, seq // t),
        in_specs=[pl.BlockSpec((1, t, STEP_LANES), lambda b, p, i: (b, i, p)),
                  _resident((1, seq, STEP_LANES), lambda b, p, i: (b, 0, n_steps + p)),
                  _resident((1, PAIRS_PER_STEP) + vt.shape[2:], lambda b, p, i: (b, p, 0, 0, 0))],
        out_specs=pl.BlockSpec((1, t, STEP_LANES), lambda b, p, i: (b, i, p)),
        out_shape=jax.ShapeDtypeStruct((batch, seq, N_HEADS_D * HEAD_DIM), BF16),
        scratch_shapes=[pltpu.VMEM((HEADS_PER_STEP, HEAD_DIM, t), F32)],
        compiler_params=_params(3),
        name="stick_attn",
    )(sb, sb, vt)


def _post_attn_kernel(oa_ref, ob_ref, oah_ref, obh_ref, x_ref, xh_ref, mod_ref, gmix_ref, gpre_ref, gpost_ref,
                      wo_ref, wup_ref, cw_ref, cb_ref, wd_ref, o_ref, x1_ref, h_ref, u_ref, acc_ref, *, per_seq):
    d = D_MODEL
    tm = ROW_TILE
    halo = CONV_HALO
    cw = FF_CHUNK
    i = pl.program_id(0)
    mod = mod_ref[0]
    gate_m, shift, scale, gate_f = mod[:, 2 * d:3 * d], mod[:, 3 * d:4 * d], mod[:, 4 * d:5 * d], mod[:, 5 * d:6 * d]
    n_first = oa_ref.shape[1]

    def mixed(oa, ob, x):
        y = _dot(oa, wo_ref[0:n_first, :]) + _dot(ob, wo_ref[n_first:2 * n_first, :])
        return x + gate_m * _rms(y, gmix_ref[...])

    g = gpre_ref[...]
    ahead = _prenorm(mixed(oah_ref[...], obh_ref[...], xh_ref[...]), g, scale, shift)
    ahead = jnp.where(i % per_seq == 0, 0.0, ahead)
    h_ref[0:halo, :] = ahead.astype(BF16)
    x1_ref[...] = mixed(oa_ref[...], ob_ref[...], x_ref[...])
    h_ref[halo:halo + tm, :] = _prenorm(x1_ref[...], g, scale, shift).astype(BF16)

    def up(ch):
        slot = ch % 2
        for half in range(2):
            cols = slice(half * D_FF + ch * cw, half * D_FF + (ch + 1) * cw)
            u_ref[slot, half] = _dot(h_ref[...], wup_ref[:, cols])

    def conv(slot, half, ch):
        cols = slice(half * D_FF + ch * cw, half * D_FF + (ch + 1) * cw)
        w = cw_ref[:, cols]
        out = w[0:1, :] * u_ref[slot, half, halo - 2:halo - 2 + tm, :]
        out = out + w[1:2, :] * u_ref[slot, half, halo - 1:halo - 1 + tm, :]
        out = out + w[2:3, :] * u_ref[slot, half, halo:halo + tm, :]
        return out + cb_ref[:, cols]

    up(0)
    for ch in range(N_FF_CHUNKS):
        if ch + 1 < N_FF_CHUNKS:
            up(ch + 1)
        slot = ch % 2
        act = jax.nn.gelu(conv(slot, 0, ch), approximate=True) * conv(slot, 1, ch)
        down = _dot(act.astype(BF16), wd_ref[ch * cw:(ch + 1) * cw, :])
        if ch == 0:
            acc_ref[...] = down
        else:
            acc_ref[...] += down

    o_ref[...] = x1_ref[...] + gate_f * _rms(acc_ref[...], gpost_ref[...])


def _post_attn(oa, ob, x, mod, g_mix, g_pre, g_post, w_out, w_up, conv_w, conv_b, w_down, seq):
    rows, d = x.shape
    tm = ROW_TILE
    halo = CONV_HALO
    per_seq = seq // tm
    n_first = oa.shape[1]
    const = lambda i: (0, 0)
    tile = lambda i: (i, 0)
    ahead = lambda i: (jnp.maximum(i * (tm // halo) - 1, 0), 0)
    once = dict(pipeline_mode=pl.Buffered(1))
    return pl.pallas_call(
        functools.partial(_post_attn_kernel, per_seq=per_seq),
        grid=(rows // tm,),
        in_specs=[pl.BlockSpec((tm, n_first), tile), pl.BlockSpec((tm, n_first), tile),
                  pl.BlockSpec((halo, n_first), ahead), pl.BlockSpec((halo, n_first), ahead),
                  pl.BlockSpec((tm, d), tile), pl.BlockSpec((halo, d), ahead),
                  pl.BlockSpec((1, 1, 6 * d), lambda i: (i // per_seq, 0, 0)),
                  pl.BlockSpec((1, d), const), pl.BlockSpec((1, d), const), pl.BlockSpec((1, d), const),
                  pl.BlockSpec(w_out.shape, const, **once),
                  pl.BlockSpec(w_up.shape, const, **once),
                  pl.BlockSpec(conv_w.shape, const),
                  pl.BlockSpec(conv_b.shape, const),
                  pl.BlockSpec(w_down.shape, const, **once)],
        out_specs=pl.BlockSpec((tm, d), tile),
        out_shape=jax.ShapeDtypeStruct((rows, d), F32),
        scratch_shapes=[pltpu.VMEM((tm, d), F32),
                        pltpu.VMEM((tm + halo, d), BF16),
                        pltpu.VMEM((2, 2, tm + halo, FF_CHUNK), F32),
                        pltpu.VMEM((tm, d), F32)],
        compiler_params=_params(1),
        name="post_attn",
    )(oa, ob, oa, ob, x, x, mod, g_mix, g_pre, g_post, w_out, w_up, conv_w, conv_b, w_down)


def _rotate_half_cols(w):
    half = w.shape[-1] // 2
    return jnp.concatenate([-w[..., half:], w[..., :half]], axis=-1)


def _pad_cols(w, left, total):
    return jnp.pad(w, ((0, 0), (left, total - left - w.shape[1])))


def _rope_tables(seq):
    inv_freq = 1.0 / (ROPE_THETA ** (jnp.arange(0, MLA_ROPE_DIM, 2, dtype=F32) / MLA_ROPE_DIM))
    ang = jnp.arange(seq, dtype=F32)[:, None] * inv_freq[None, :]
    cos, sin = jnp.cos(ang), jnp.sin(ang)
    cos2 = _pad_cols(jnp.concatenate([cos, cos], axis=1), MLA_NOPE_DIM, LANES)
    sin2 = _pad_cols(jnp.concatenate([sin, sin], axis=1), MLA_NOPE_DIM, LANES)
    scale = (MLA_NOPE_DIM + MLA_ROPE_DIM) ** -0.5
    nope_ones = _pad_cols(jnp.ones((seq, MLA_NOPE_DIM), F32), 0, LANES)
    return (scale * (cos2 + nope_ones), scale * sin2, cos2, sin2)


def _odd_weights(w_in, w_uq, w_ukv):
    o = MLA_Q_RANK + MLA_KV_RANK
    w_rope = w_in[:, o:o + MLA_ROPE_DIM]
    scale_d = HEAD_DIM ** -0.5
    wd = N_HEADS_D * HEAD_DIM
    sb0 = o + MLA_ROPE_DIM
    w_in2 = jnp.concatenate([
        w_in[:, :o],
        _pad_cols(w_rope, MLA_NOPE_DIM, LANES),
        _pad_cols(_rotate_half_cols(w_rope), MLA_NOPE_DIM, LANES),
        w_in[:, sb0:sb0 + wd] * scale_d,
        w_in[:, sb0 + wd:],
    ], axis=1).astype(BF16)
    qd = MLA_NOPE_DIM + MLA_ROPE_DIM
    uq = w_uq.reshape(MLA_Q_RANK, N_HEADS_C, qd)
    plain = jnp.pad(uq, ((0, 0), (0, 0), (0, LANES - qd)))
    swapped = jnp.pad(_rotate_half_cols(uq[..., MLA_NOPE_DIM:]),
                      ((0, 0), (0, 0), (MLA_NOPE_DIM, LANES - qd)))
    wq = jnp.concatenate([plain.reshape(MLA_Q_RANK, -1), swapped.reshape(MLA_Q_RANK, -1)], axis=1).astype(BF16)
    ukv = w_ukv.reshape(MLA_KV_RANK, N_HEADS_C, MLA_NOPE_DIM + MLA_V_DIM)
    k_nope = jnp.pad(ukv[..., :MLA_NOPE_DIM], ((0, 0), (0, 0), (0, LANES - MLA_NOPE_DIM)))
    v = ukv[..., MLA_NOPE_DIM:]
    wkv = jnp.concatenate([k_nope.reshape(MLA_KV_RANK, -1), v.reshape(MLA_KV_RANK, -1)], axis=1).astype(BF16)
    return w_in2, wq, wkv


def _even_weights(w_in):
    wa = N_HEADS_A * HEAD_DIM
    scale = HEAD_DIM ** -0.5
    col = jnp.arange(w_in.shape[1])
    is_q = (col < wa) | ((col >= 3 * wa) & (col < 4 * wa))
    return (w_in * jnp.where(is_q, scale, 1.0)).astype(BF16)


def kernel(x, c, rel_bias, ada_w, ada_b, mix_pre_g, mix_post_g, ffn_pre_g, ffn_post_g, ab_w_in, ab_w_out,
           cd_w_in, mla_q_norm_g, mla_kv_norm_g, mla_w_uq, mla_w_ukv, cd_w_out, ffn_w_up, ffn_conv_w,
           ffn_conv_b, ffn_w_down):
    batch, seq, d = x.shape
    assert d == D_MODEL and seq % GROUP_KEYS == 0 and GROUP_KEYS % ROW_TILE == 0
    rows = batch * seq
    xf = x.reshape(rows, d)

    mods = _mods(c, ada_w, ada_b)
    bias_a = _bias_tiles(rel_bias, 0, N_HEADS_A, MOBA_BIAS_TILES, dilated=False)
    bias_b = _bias_tiles(rel_bias, N_HEADS_A, N_HEADS_B, DIL_BIAS_TILES, dilated=True)
    tables = _rope_tables(seq)
    step_blocks = N_HEADS_A // HEADS_PER_STEP

    for layer in range(DEPTH):
        mod = mods[layer].reshape(batch, 1, 6 * d)
        i = layer // 2
        if layer % 2 == 0:
            proj, km, vta, vtb = _even_proj(xf, mod, mix_pre_g[layer].reshape(1, d),
                                            _even_weights(ab_w_in[i]), seq)
            proj = proj.reshape(batch, seq, -1)
            km = km.reshape(batch, seq // MOBA_BLOCK, -1)
            o_first = _toeplitz_attn(proj, vta, bias_a, km, batch, seq, 0, moba=True)
            o_second = _toeplitz_attn(proj, vtb, bias_b, None, batch, seq, 3 * step_blocks, moba=False)
            w_out = ab_w_out[i]
        else:
            w_in2, wq, wkv = _odd_weights(cd_w_in[i], mla_w_uq[i], mla_w_ukv[i])
            qm, km, vtm, sb, vts = _odd_proj(xf, mod, mix_pre_g[layer].reshape(1, d), w_in2,
                                             mla_q_norm_g[i].reshape(1, -1), mla_kv_norm_g[i].reshape(1, -1),
                                             wq, wkv, tables, seq)
            o_first = _mla_attn(qm.reshape(batch, seq, -1), km.reshape(batch, seq, -1), vtm, batch, seq)
            o_second = _stick_attn(sb.reshape(batch, seq, -1), vts, batch, seq)
            w_out = cd_w_out[i]
        xf = _post_attn(o_first.reshape(rows, -1), o_second.reshape(rows, -1), xf, mod,
                        mix_post_g[layer].reshape(1, d), ffn_pre_g[layer].reshape(1, d),
                        ffn_post_g[layer].reshape(1, d), w_out.astype(BF16), ffn_w_up[layer].astype(BF16),
                        ffn_conv_w[layer], ffn_conv_b[layer].reshape(1, -1), ffn_w_down[layer].astype(BF16), seq)
    return xf.reshape(batch, seq, d)
```

```python
import functools
import math

import jax
import jax.numpy as jnp
from jax import lax
from jax.experimental import pallas as pl
from jax.experimental.pallas import tpu as pltpu

F32 = jnp.float32
BF16 = jnp.bfloat16

D_MODEL = 1024
DEPTH = 4
HEAD_DIM = 64
N_HEADS_A = 8
N_HEADS_B = 8
N_HEADS_C = 8
N_HEADS_D = 8
MOBA_BLOCK = 256
MOBA_TOPK = 3
DILATED_BRANCHES = ((128, 1), (512, 4), (2048, 16))
MLA_Q_RANK = 256
MLA_KV_RANK = 256
MLA_NOPE_DIM = 64
MLA_ROPE_DIM = 32
MLA_V_DIM = 64
ROPE_THETA = 10000.0
REL_BUCKETS = 32
REL_MAX_DIST = 2048
D_FF = 2816
CONV_WIDTH = 3
NORM_EPS = 1e-6
LOG2_E = math.log2(math.e)

LANES = 128
SUBLANES = 8
BF16_SUBLANES = 16
VMEM_LIMIT_BYTES = 56 * 1024 * 1024

ATT_TILE = MOBA_BLOCK
KEY_GROUP = 4
GROUP_KEYS = KEY_GROUP * ATT_TILE
PAIRS_PER_STEP = 2
HEADS_PER_STEP = 2 * PAIRS_PER_STEP
STEP_LANES = PAIRS_PER_STEP * LANES
ROW_TILE = 512
FF_CHUNK = 256
N_FF_CHUNKS = D_FF // FF_CHUNK
CONV_HALO = BF16_SUBLANES
MOBA_BIAS_TILES = REL_MAX_DIST // ATT_TILE + 2
DIL_MAX_TILE_DIST = DILATED_BRANCHES[-1][0] // ATT_TILE
DIL_BIAS_TILES = DIL_MAX_TILE_DIST + 1

_NT = (((1,), (1,)), ((), ()))


def _bucket_lower_bounds():
    max_exact = REL_BUCKETS // 2
    ratio = REL_MAX_DIST // max_exact
    n_log = REL_BUCKETS - max_exact
    lows = list(range(max_exact + 1))
    for k in range(1, n_log):
        d = lows[-1]
        while d ** n_log < (max_exact ** n_log) * (ratio ** k):
            d += 1
        lows.append(d)
    return lows


_BUCKET_LOW = _bucket_lower_bounds()


def _dot(a, b):
    return jnp.dot(a, b, preferred_element_type=F32)


def _dot_nt(a, b):
    return lax.dot_general(a, b, _NT, preferred_element_type=F32)


def _split_bf16(x):
    hi = x.astype(BF16)
    lo = (x - hi.astype(F32)).astype(BF16)
    return hi, lo


def _rms(x, g):
    return (x * lax.rsqrt(jnp.mean(x * x, axis=-1, keepdims=True) + NORM_EPS)) * g


def _prenorm(x, g, scale, shift):
    return _rms(x, g) * (1.0 + scale) + shift


def _resident(block_shape, index_map):
    return pl.BlockSpec(block_shape, index_map)


def _params(n_grid_dims):
    return pltpu.CompilerParams(dimension_semantics=("arbitrary",) * n_grid_dims,
                                vmem_limit_bytes=VMEM_LIMIT_BYTES)


def _mods_kernel(c_ref, w_ref, b_ref, o_ref):
    c = c_ref[...]
    cond = c * jax.nn.sigmoid(c)
    c_hi, c_lo = _split_bf16(cond)
    w_hi, w_lo = _split_bf16(w_ref[0])
    o_ref[0] = _dot(c_hi, w_hi) + _dot(c_hi, w_lo) + _dot(c_lo, w_hi) + b_ref[0]


def _mods(c, ada_w, ada_b):
    b, d = c.shape
    rows = BF16_SUBLANES
    n_out = ada_w.shape[-1]
    tn = n_out // 4
    c_pad = jnp.zeros((rows, d), F32).at[:b].set(c)
    out = pl.pallas_call(
        _mods_kernel,
        grid=(DEPTH, n_out // tn),
        in_specs=[pl.BlockSpec((rows, d), lambda l, j: (0, 0)),
                  pl.BlockSpec((1, d, tn), lambda l, j: (l, 0, j)),
                  pl.BlockSpec((1, 1, tn), lambda l, j: (l, 0, j))],
        out_specs=pl.BlockSpec((1, rows, tn), lambda l, j: (l, 0, j)),
        out_shape=jax.ShapeDtypeStruct((DEPTH, rows, n_out), F32),
        compiler_params=_params(2),
        name="ada_mods",
    )(c_pad, ada_w, ada_b.reshape(DEPTH, 1, n_out))
    return out[:, :b]


def _bias_tiles_kernel(tab_ref, o_ref, *, head_off, dilated):
    h = pl.program_id(0) + head_off
    d = pl.program_id(1)
    t = ATT_TILE
    dist = d * t + lax.broadcasted_iota(jnp.int32, (SUBLANES, 2 * t), 1) - t
    val = jnp.full(dist.shape, tab_ref[h, 0], F32)
    for b in range(1, REL_BUCKETS):
        val = jnp.where(dist >= _BUCKET_LOW[b], tab_ref[h, b], val)
    if dilated:
        mult = jnp.zeros(dist.shape, F32)
        for window, dil in DILATED_BRANCHES:
            hit = jnp.where(dist <= window, jnp.where((dist & (dil - 1)) == 0, 1.0, 0.0), 0.0)
            mult = mult + hit
        log_mult = jnp.where(mult > 2.5, math.log(3.0), jnp.where(mult > 1.5, math.log(2.0), 0.0))
        val = jnp.where(mult > 0.5, val + log_mult, -jnp.inf)
    val = jnp.where(dist >= 0, val * LOG2_E, -jnp.inf)
    strip = jnp.concatenate([val] * (t // SUBLANES), axis=0)
    rotated = pltpu.roll(strip, 0, 1, stride=1, stride_axis=0)
    o_ref[0, 0] = rotated[:, t:2 * t]


def _bias_tiles(rel_bias, head_off, n_heads, n_tiles, dilated):
    t = ATT_TILE
    return pl.pallas_call(
        functools.partial(_bias_tiles_kernel, head_off=head_off, dilated=dilated),
        grid=(n_heads, n_tiles),
        in_specs=[pl.BlockSpec(memory_space=pltpu.SMEM)],
        out_specs=pl.BlockSpec((1, 1, t, t), lambda h, d: (h, d, 0, 0)),
        out_shape=jax.ShapeDtypeStruct((n_heads, n_tiles, t, t), F32),
        compiler_params=_params(2),
        name="dilated_bias_tiles" if dilated else "moba_bias_tiles",
    )(rel_bias)


def _store_vt(vt_ref, v):
    vt = v.T.astype(BF16)
    for p in range(vt.shape[0] // LANES):
        vt_ref[0, p, 0] = vt[p * LANES:(p + 1) * LANES, :]


def _vt_spec(tm, per_seq, n_pairs):
    per_group = GROUP_KEYS // tm
    return pl.BlockSpec((1, n_pairs, 1, LANES, tm),
                        lambda i: (i // per_seq, 0, (i % per_seq) // per_group, 0, (i % per_seq) % per_group))


def _vt_shape(batch, seq, n_pairs):
    return jax.ShapeDtypeStruct((batch, n_pairs, seq // GROUP_KEYS, LANES, GROUP_KEYS), BF16)


def _even_proj_kernel(x_ref, mod_ref, g_ref, w_ref, o_ref, km_ref, vta_ref, vtb_ref):
    d = D_MODEL
    mod = mod_ref[0]
    h = _prenorm(x_ref[...], g_ref[...], mod[:, d:2 * d], mod[:, 0:d])
    p = _dot(h.astype(BF16), w_ref[...])
    o_ref[...] = p.astype(BF16)
    wa = N_HEADS_A * HEAD_DIM
    ka = p[:, wa:2 * wa]
    nb = ka.shape[0] // MOBA_BLOCK
    km_ref[0] = jnp.mean(ka.reshape(nb, MOBA_BLOCK, wa), axis=1)
    _store_vt(vta_ref, p[:, 2 * wa:3 * wa])
    _store_vt(vtb_ref, p[:, 5 * wa:6 * wa])


def _even_proj(x, mod, g, w_in, seq):
    rows, d = x.shape
    tm = ROW_TILE
    n = w_in.shape[1]
    wa = N_HEADS_A * HEAD_DIM
    per_seq = seq // tm
    n_pairs = N_HEADS_A // 2
    vt_spec = _vt_spec(tm, per_seq, n_pairs)
    vt_shape = _vt_shape(rows // seq, seq, n_pairs)
    return pl.pallas_call(
        _even_proj_kernel,
        grid=(rows // tm,),
        in_specs=[pl.BlockSpec((tm, d), lambda i: (i, 0)),
                  pl.BlockSpec((1, 1, 6 * d), lambda i: (i // per_seq, 0, 0)),
                  pl.BlockSpec((1, d), lambda i: (0, 0)),
                  pl.BlockSpec((d, n), lambda i: (0, 0))],
        out_specs=[pl.BlockSpec((tm, n), lambda i: (i, 0)),
                   pl.BlockSpec((1, tm // MOBA_BLOCK, wa), lambda i: (i, 0, 0)),
                   vt_spec, vt_spec],
        out_shape=[jax.ShapeDtypeStruct((rows, n), BF16),
                   jax.ShapeDtypeStruct((rows // tm, tm // MOBA_BLOCK, wa), F32),
                   vt_shape, vt_shape],
        compiler_params=_params(1),
        name="even_proj",
    )(x, mod, g, w_in)


def _pair_lanes(x, h):
    pair = h // 2
    return x[:, pair * LANES:(pair + 1) * LANES]


def _split_heads(q_step):
    first = lax.broadcasted_iota(jnp.int32, (q_step.shape[0], LANES), 1) < HEAD_DIM
    heads = []
    for pair in range(PAIRS_PER_STEP):
        q2 = q_step[:, pair * LANES:(pair + 1) * LANES]
        zero = jnp.zeros_like(q2)
        heads += [jnp.where(first, q2, zero), jnp.where(first, zero, q2)]
    return heads


def _head_rows(vt_ref, g, h, n_tiles):
    return vt_ref[0, h // 2, g, (h % 2) * HEAD_DIM:(h % 2 + 1) * HEAD_DIM, 0:n_tiles * ATT_TILE]


def _tile_scores(keys, q_head, per_tile):
    t = ATT_TILE
    n = keys.shape[0] // t
    if per_tile:
        return [_dot_nt(keys[c * t:(c + 1) * t], q_head) for c in range(n)]
    s = _dot_nt(keys, q_head)
    return [s[c * t:(c + 1) * t] for c in range(n)]


def _group_keys(k_ref, g, n_tiles):
    start = pl.multiple_of(g * GROUP_KEYS, GROUP_KEYS)
    return k_ref[0, pl.ds(start, n_tiles * ATT_TILE), :]


def _sweep(qi, group, n_far_groups=None):
    g_own = qi // KEY_GROUP
    own = [functools.partial(group, g_own, None, n + 1, True) for n in range(KEY_GROUP)]
    state = lax.switch(qi % KEY_GROUP, own)
    n_past = g_own if n_far_groups is None else jnp.minimum(g_own, n_far_groups)
    return lax.fori_loop(0, n_past, lambda n, st: group(g_own - 1 - n, st, KEY_GROUP, False), state)


def _tile_iotas():
    t = ATT_TILE
    return lax.broadcasted_iota(jnp.int32, (t, t), 0), lax.broadcasted_iota(jnp.int32, (t, t), 1)


def _store_step_output(o_ref, outs_t):
    o_ref[0] = jnp.concatenate(outs_t, axis=0).T.astype(BF16)


def _softmax_group(s_tiles, ons, vt_h, acc_ref, h, stats, first):
    maxes = []
    for s, on in zip(s_tiles, ons):
        mx = jnp.max(s, axis=0, keepdims=True)
        maxes.append(mx if on is None else jnp.where(on, mx, -jnp.inf))
    m_new = functools.reduce(jnp.maximum, maxes)
    if not first:
        m_old, l_old = stats
        m_new = jnp.maximum(m_new, m_old)
    ps = []
    l_add = None
    for s, on in zip(s_tiles, ons):
        p = jnp.exp2(s - (m_new if on is None else jnp.where(on, m_new, jnp.inf)))
        p_sum = jnp.sum(p, axis=0, keepdims=True)
        l_add = p_sum if l_add is None else l_add + p_sum
        ps.append(p.astype(BF16))
    pv = _dot(vt_h, jnp.concatenate(ps, axis=0))
    if first:
        acc_ref[h] = pv
        return m_new, l_add
    alpha = jnp.exp2(m_old - m_new)
    acc_ref[h] = alpha * acc_ref[h] + pv
    return m_new, alpha * l_old + l_add


def _moba_select(qh, km_ref, sel_ref, qi):
    t = ATT_TILE
    km_hi, km_lo = _split_bf16(km_ref[0])
    nb = km_hi.shape[0]
    blk = lax.broadcasted_iota(jnp.int32, (nb, t), 0)
    past = blk < qi
    for h in range(HEADS_PER_STEP):
        gate = _dot_nt(_pair_lanes(km_hi, h), qh[h]) + _dot_nt(_pair_lanes(km_lo, h), qh[h])
        gate = jnp.where(past, gate, -jnp.inf)
        beaten = jnp.zeros((nb, t), F32)
        for other in range(nb):
            row = gate[other:other + 1, :]
            wins = jnp.where(row > gate, 1.0, jnp.where(row == gate, jnp.where(blk > other, 1.0, 0.0), 0.0))
            beaten = beaten + wins
        sel_ref[h] = jnp.where(past, jnp.where(beaten < MOBA_TOPK, 1.0, 0.0), 0.0)


def _toeplitz_attn_kernel(*refs, moba):
    if moba:
        q_ref, k_ref, vt_ref, bias_ref, km_ref, o_ref, acc_ref, sel_ref = refs
    else:
        q_ref, k_ref, vt_ref, bias_ref, o_ref, acc_ref = refs
    qi = pl.program_id(2)
    qh = _split_heads(q_ref[0])
    n_bias = bias_ref.shape[1]
    if moba:
        _moba_select(qh, km_ref, sel_ref, qi)

    def group(g, stats, n_tiles, first):
        kg = _group_keys(k_ref, g, n_tiles)
        new = ()
        scores = [_tile_scores(_pair_lanes(kg, h), qh[h], per_tile=False) for h in range(HEADS_PER_STEP)]
        for h in range(HEADS_PER_STEP):
            s_tiles = scores[h]
            ons = []
            for c in range(n_tiles):
                j = g * KEY_GROUP + c
                if first:
                    dt = n_tiles - 1 - c
                    s_tiles[c] = s_tiles[c] + bias_ref[h, dt]
                    ons.append(sel_ref[h, pl.ds(j, 1), :] > 0.5 if moba and dt > 0 else None)
                else:
                    dt = qi - j
                    s_tiles[c] = s_tiles[c] + bias_ref[h, jnp.minimum(dt, n_bias - 1)]
                    ons.append(sel_ref[h, pl.ds(j, 1), :] > 0.5 if moba else dt <= DIL_MAX_TILE_DIST)
            new += _softmax_group(s_tiles, ons, _head_rows(vt_ref, g, h, n_tiles), acc_ref, h,
                                  None if first else stats[2 * h:2 * h + 2], first)
        return new

    if moba:
        n_far = None
    else:
        n_far = qi // KEY_GROUP - jnp.maximum(qi - DIL_MAX_TILE_DIST, 0) // KEY_GROUP
    stats = _sweep(qi, group, n_far)
    _store_step_output(o_ref, [acc_ref[h] / stats[2 * h + 1] for h in range(HEADS_PER_STEP)])


def _toeplitz_attn(proj, vt, bias, km, batch, seq, col0, moba):
    t = ATT_TILE
    n_steps = N_HEADS_A // HEADS_PER_STEP
    n_bias = bias.shape[1]
    in_specs = [pl.BlockSpec((1, t, STEP_LANES), lambda b, p, i: (b, i, col0 + p)),
                _resident((1, seq, STEP_LANES), lambda b, p, i: (b, 0, col0 + n_steps + p)),
                _resident((1, PAIRS_PER_STEP) + vt.shape[2:], lambda b, p, i: (b, p, 0, 0, 0)),
                _resident((HEADS_PER_STEP, n_bias, t, t), lambda b, p, i: (p, 0, 0, 0))]
    args = [proj, proj, vt, bias]
    scratch = [pltpu.VMEM((HEADS_PER_STEP, HEAD_DIM, t), F32)]
    if moba:
        in_specs.append(pl.BlockSpec((1, seq // MOBA_BLOCK, STEP_LANES), lambda b, p, i: (b, 0, p)))
        args.append(km)
        scratch.append(pltpu.VMEM((HEADS_PER_STEP, seq // MOBA_BLOCK, t), F32))
    return pl.pallas_call(
        functools.partial(_toeplitz_attn_kernel, moba=moba),
        grid=(batch, n_steps, seq // t),
        in_specs=in_specs,
        out_specs=pl.BlockSpec((1, t, STEP_LANES), lambda b, p, i: (b, i, p)),
        out_shape=jax.ShapeDtypeStruct((batch, seq, N_HEADS_A * HEAD_DIM), BF16),
        scratch_shapes=scratch,
        compiler_params=_params(3),
        name="moba_attn" if moba else "dilated_attn",
    )(*args)


def _odd_proj_kernel(x_ref, mod_ref, g_ref, win_ref, gq_ref, gkv_ref, wq_ref, wkv_ref,
                     cq_ref, sq_ref, ck_ref, sk_ref, qm_ref, km_ref, vtm_ref, sb_ref, vts_ref):
    d = D_MODEL
    mod = mod_ref[0]
    h = _prenorm(x_ref[...], g_ref[...], mod[:, d:2 * d], mod[:, 0:d])
    p = _dot(h.astype(BF16), win_ref[...])
    o = MLA_Q_RANK + MLA_KV_RANK
    c_q, c_kv = p[:, 0:MLA_Q_RANK], p[:, MLA_Q_RANK:o]
    k_rope, k_rope_swapped = p[:, o:o + LANES], p[:, o + LANES:o + 2 * LANES]
    sb0 = o + 2 * LANES
    n_qk = 2 * N_HEADS_D * HEAD_DIM
    sb_ref[...] = p[:, sb0:sb0 + n_qk].astype(BF16)
    _store_vt(vts_ref, p[:, sb0 + n_qk:])
    q12 = _dot(_rms(c_q, gq_ref[...]).astype(BF16), wq_ref[...])
    kv = _dot(_rms(c_kv, gkv_ref[...]).astype(BF16), wkv_ref[...])
    k_pe = k_rope * ck_ref[...] + k_rope_swapped * sk_ref[...]
    cq, sq = cq_ref[...], sq_ref[...]
    half = N_HEADS_C * LANES
    for hh in range(N_HEADS_C):
        cols = slice(hh * LANES, (hh + 1) * LANES)
        swapped = slice(half + hh * LANES, half + (hh + 1) * LANES)
        qm_ref[:, cols] = (q12[:, cols] * cq + q12[:, swapped] * sq).astype(BF16)
        km_ref[:, cols] = (kv[:, cols] + k_pe).astype(BF16)
    _store_vt(vtm_ref, kv[:, half:])


def _odd_proj(x, mod, g, w_in, gq, gkv, wq, wkv, tables, seq):
    rows, d = x.shape
    tm = ROW_TILE
    per_seq = seq // tm
    n_in = w_in.shape[1]
    n_sb = 2 * N_HEADS_D * HEAD_DIM
    n_q = N_HEADS_C * LANES
    n_pairs = N_HEADS_C // 2
    const = lambda i: (0, 0)
    table_spec = pl.BlockSpec((tm, LANES), lambda i: (i % per_seq, 0))
    vt_spec = _vt_spec(tm, per_seq, n_pairs)
    vt_shape = _vt_shape(rows // seq, seq, n_pairs)
    return pl.pallas_call(
        _odd_proj_kernel,
        grid=(rows // tm,),
        in_specs=[pl.BlockSpec((tm, d), lambda i: (i, 0)),
                  pl.BlockSpec((1, 1, 6 * d), lambda i: (i // per_seq, 0, 0)),
                  pl.BlockSpec((1, d), const),
                  pl.BlockSpec((d, n_in), const),
                  pl.BlockSpec((1, MLA_Q_RANK), const),
                  pl.BlockSpec((1, MLA_KV_RANK), const),
                  pl.BlockSpec(wq.shape, const),
                  pl.BlockSpec(wkv.shape, const),
                  table_spec, table_spec, table_spec, table_spec],
        out_specs=[pl.BlockSpec((tm, n_q), lambda i: (i, 0)),
                   pl.BlockSpec((tm, n_q), lambda i: (i, 0)),
                   vt_spec,
                   pl.BlockSpec((tm, n_sb), lambda i: (i, 0)),
                   vt_spec],
        out_shape=[jax.ShapeDtypeStruct((rows, n_q), BF16),
                   jax.ShapeDtypeStruct((rows, n_q), BF16),
                   vt_shape,
                   jax.ShapeDtypeStruct((rows, n_sb), BF16),
                   vt_shape],
        compiler_params=_params(1),
        name="odd_proj",
    )(x, mod, g, w_in, gq, gkv, wq, wkv, *tables)


def _mla_attn_kernel(q_ref, k_ref, vt_ref, o_ref, acc_ref):
    qi = pl.program_id(2)
    q_step = q_ref[0]
    qh = [q_step[:, h * LANES:(h + 1) * LANES] for h in range(HEADS_PER_STEP)]

    def group(g, stats, n_tiles, first):
        kg = _group_keys(k_ref, g, n_tiles)
        new = ()
        scores = [_tile_scores(kg[:, h * LANES:(h + 1) * LANES], qh[h], per_tile=True)
                  for h in range(HEADS_PER_STEP)]
        for h in range(HEADS_PER_STEP):
            s_tiles = scores[h]
            if first:
                key, query = _tile_iotas()
                s_tiles[-1] = jnp.where(key <= query, s_tiles[-1], -jnp.inf)
            new += _softmax_group(s_tiles, [None] * n_tiles, _head_rows(vt_ref, g, h, n_tiles), acc_ref, h,
                                  None if first else stats[2 * h:2 * h + 2], first)
        return new

    stats = _sweep(qi, group)
    _store_step_output(o_ref, [acc_ref[h] / stats[2 * h + 1] for h in range(HEADS_PER_STEP)])


def _mla_attn(qm, km, vt, batch, seq):
    t = ATT_TILE
    n_steps = N_HEADS_C // HEADS_PER_STEP
    qk_lanes = HEADS_PER_STEP * LANES
    return pl.pallas_call(
        _mla_attn_kernel,
        grid=(batch, n_steps, seq // t),
        in_specs=[pl.BlockSpec((1, t, qk_lanes), lambda b, p, i: (b, i, p)),
                  _resident((1, seq, qk_lanes), lambda b, p, i: (b, 0, p)),
                  _resident((1, PAIRS_PER_STEP) + vt.shape[2:], lambda b, p, i: (b, p, 0, 0, 0))],
        out_specs=pl.BlockSpec((1, t, STEP_LANES), lambda b, p, i: (b, i, p)),
        out_shape=jax.ShapeDtypeStruct((batch, seq, N_HEADS_C * MLA_V_DIM), BF16),
        scratch_shapes=[pltpu.VMEM((HEADS_PER_STEP, HEAD_DIM, t), F32)],
        compiler_params=_params(3),
        name="mla_attn",
    )(qm, km, vt)


def _stick_attn_kernel(q_ref, k_ref, vt_ref, o_ref, acc_ref):
    t = ATT_TILE
    qi = pl.program_id(2)
    qh = _split_heads(q_ref[0])
    key = lax.broadcasted_iota(jnp.int32, (t, t), 0)
    other = lax.broadcasted_iota(jnp.int32, (t, t), 1)
    from_here = jnp.where(other >= key, 1.0, 0.0).astype(BF16)
    from_here2 = jnp.concatenate([from_here, from_here], axis=1)

    strict = key < other

    def group(g, carries, n_tiles, first):
        kg = _group_keys(k_ref, g, n_tiles)
        new = ()
        scores = [_tile_scores(_pair_lanes(kg, h), qh[h], per_tile=False) for h in range(HEADS_PER_STEP)]
        for h in range(HEADS_PER_STEP):
            z_tiles = scores[h]
            keep_tiles = []
            for z in z_tiles:
                neg_z = -z
                keep_tiles.append(jnp.minimum(neg_z, 0.0) - jnp.log(1.0 + jnp.exp(jnp.minimum(z, neg_z))))
            if first:
                keep_tiles[-1] = jnp.where(strict, keep_tiles[-1], 0.0)
            run = None if first else carries[h]
            a_tiles = [None] * n_tiles
            for c in reversed(range(n_tiles)):
                tail = _dot(from_here2, jnp.concatenate(_split_bf16(keep_tiles[c]), axis=0))
                tile_total = tail[0:1, :]
                if run is not None:
                    tail = tail + run
                a_tiles[c] = jnp.exp(z_tiles[c] + tail)
                run = tile_total if run is None else run + tile_total
            if first:
                a_tiles[-1] = jnp.where(strict, a_tiles[-1], 0.0)
            pv = _dot(_head_rows(vt_ref, g, h, n_tiles), jnp.concatenate(a_tiles, axis=0).astype(BF16))
            acc_ref[h] = pv if first else acc_ref[h] + pv
            new += (run,)
        return new

    _sweep(qi, group)
    _store_step_output(o_ref, [acc_ref[h] for h in range(HEADS_PER_STEP)])


def _stick_attn(sb, vt, batch, seq):
    t = ATT_TILE
    n_steps = N_HEADS_D // HEADS_PER_STEP
    return pl.pallas_call(
        _stick_attn_kernel,
        grid=(batch, n_steps, seq // t),
        in_specs=[pl.BlockSpec((1, t, STEP_LANES), lambda b, p, i: (b, i, p)),
                  _resident((1, seq, STEP_LANES), lambda b, p, i: (b, 0, n_steps + p)),
                  _resident((1, PAIRS_PER_STEP) + vt.shape[2:], lambda b, p, i: (b, p, 0, 0, 0))],
        out_specs=pl.BlockSpec((1, t, STEP_LANES), lambda b, p, i: (b, i, p)),
        out_shape=jax.ShapeDtypeStruct((batch, seq, N_HEADS_D * HEAD_DIM), BF16),
        scratch_shapes=[pltpu.VMEM((HEADS_PER_STEP, HEAD_DIM, t), F32)],
        compiler_params=_params(3),
        name="stick_attn",
    )(sb, sb, vt)


def _post_attn_kernel(oa_ref, ob_ref, oah_ref, obh_ref, x_ref, xh_ref, mod_ref, gmix_ref, gpre_ref, gpost_ref,
                      wo_ref, wup_ref, cw_ref, cb_ref, wd_ref, o_ref, x1_ref, h_ref, u_ref, acc_ref, *, per_seq):
    d = D_MODEL
    tm = ROW_TILE
    halo = CONV_HALO
    cw = FF_CHUNK
    i = pl.program_id(0)
    mod = mod_ref[0]
    gate_m, shift, scale, gate_f = mod[:, 2 * d:3 * d], mod[:, 3 * d:4 * d], mod[:, 4 * d:5 * d], mod[:, 5 * d:6 * d]
    n_first = oa_ref.shape[1]

    def mixed(oa, ob, x):
        y = _dot(oa, wo_ref[0:n_first, :]) + _dot(ob, wo_ref[n_first:2 * n_first, :])
        return x + gate_m * _rms(y, gmix_ref[...])

    g = gpre_ref[...]
    ahead = _prenorm(mixed(oah_ref[...], obh_ref[...], xh_ref[...]), g, scale, shift)
    ahead = jnp.where(i % per_seq == 0, 0.0, ahead)
    h_ref[0:halo, :] = ahead.astype(BF16)
    x1_ref[...] = mixed(oa_ref[...], ob_ref[...], x_ref[...])
    h_ref[halo:halo + tm, :] = _prenorm(x1_ref[...], g, scale, shift).astype(BF16)

    def up(ch):
        slot = ch % 2
        for half in range(2):
            cols = slice(half * D_FF + ch * cw, half * D_FF + (ch + 1) * cw)
            u_ref[slot, half] = _dot(h_ref[...], wup_ref[:, cols])

    def conv(slot, half, ch):
        cols = slice(half * D_FF + ch * cw, half * D_FF + (ch + 1) * cw)
        w = cw_ref[:, cols]
        out = w[0:1, :] * u_ref[slot, half, halo - 2:halo - 2 + tm, :]
        out = out + w[1:2, :] * u_ref[slot, half, halo - 1:halo - 1 + tm, :]
        out = out + w[2:3, :] * u_ref[slot, half, halo:halo + tm, :]
        return out + cb_ref[:, cols]

    def down(ch, act):
        part = _dot(act, wd_ref[ch * cw:(ch + 1) * cw, :])
        if ch == 0:
            acc_ref[...] = part
        else:
            acc_ref[...] += part

    up(0)
    act = None
    for ch in range(N_FF_CHUNKS):
        if ch + 1 < N_FF_CHUNKS:
            up(ch + 1)
        if act is not None:
            down(ch - 1, act)
        slot = ch % 2
        act = (jax.nn.gelu(conv(slot, 0, ch), approximate=True) * conv(slot, 1, ch)).astype(BF16)
    down(N_FF_CHUNKS - 1, act)

    o_ref[...] = x1_ref[...] + gate_f * _rms(acc_ref[...], gpost_ref[...])


def _post_attn(oa, ob, x, mod, g_mix, g_pre, g_post, w_out, w_up, conv_w, conv_b, w_down, seq):
    rows, d = x.shape
    tm = ROW_TILE
    halo = CONV_HALO
    per_seq = seq // tm
    n_first = oa.shape[1]
    const = lambda i: (0, 0)
    tile = lambda i: (i, 0)
    ahead = lambda i: (jnp.maximum(i * (tm // halo) - 1, 0), 0)
    once = dict(pipeline_mode=pl.Buffered(1))
    return pl.pallas_call(
        functools.partial(_post_attn_kernel, per_seq=per_seq),
        grid=(rows // tm,),
        in_specs=[pl.BlockSpec((tm, n_first), tile), pl.BlockSpec((tm, n_first), tile),
                  pl.BlockSpec((halo, n_first), ahead), pl.BlockSpec((halo, n_first), ahead),
                  pl.BlockSpec((tm, d), tile), pl.BlockSpec((halo, d), ahead),
                  pl.BlockSpec((1, 1, 6 * d), lambda i: (i // per_seq, 0, 0)),
                  pl.BlockSpec((1, d), const), pl.BlockSpec((1, d), const), pl.BlockSpec((1, d), const),
                  pl.BlockSpec(w_out.shape, const, **once),
                  pl.BlockSpec(w_up.shape, const, **once),
                  pl.BlockSpec(conv_w.shape, const),
                  pl.BlockSpec(conv_b.shape, const),
                  pl.BlockSpec(w_down.shape, const, **once)],
        out_specs=pl.BlockSpec((tm, d), tile),
        out_shape=jax.ShapeDtypeStruct((rows, d), F32),
        scratch_shapes=[pltpu.VMEM((tm, d), F32),
                        pltpu.VMEM((tm + halo, d), BF16),
                        pltpu.VMEM((2, 2, tm + halo, FF_CHUNK), F32),
                        pltpu.VMEM((tm, d), F32)],
        compiler_params=_params(1),
        name="post_attn",
    )(oa, ob, oa, ob, x, x, mod, g_mix, g_pre, g_post, w_out, w_up, conv_w, conv_b, w_down)


def _rotate_half_cols(w):
    half = w.shape[-1] // 2
    return jnp.concatenate([-w[..., half:], w[..., :half]], axis=-1)


def _pad_cols(w, left, total):
    return jnp.pad(w, ((0, 0), (left, total - left - w.shape[1])))


def _rope_tables(seq):
    inv_freq = 1.0 / (ROPE_THETA ** (jnp.arange(0, MLA_ROPE_DIM, 2, dtype=F32) / MLA_ROPE_DIM))
    ang = jnp.arange(seq, dtype=F32)[:, None] * inv_freq[None, :]
    cos, sin = jnp.cos(ang), jnp.sin(ang)
    cos2 = _pad_cols(jnp.concatenate([cos, cos], axis=1), MLA_NOPE_DIM, LANES)
    sin2 = _pad_cols(jnp.concatenate([sin, sin], axis=1), MLA_NOPE_DIM, LANES)
    scale = (MLA_NOPE_DIM + MLA_ROPE_DIM) ** -0.5
    nope_ones = _pad_cols(jnp.ones((seq, MLA_NOPE_DIM), F32), 0, LANES)
    scale = scale * LOG2_E
    return (scale * (cos2 + nope_ones), scale * sin2, cos2, sin2)


def _odd_weights(w_in, w_uq, w_ukv):
    o = MLA_Q_RANK + MLA_KV_RANK
    w_rope = w_in[:, o:o + MLA_ROPE_DIM]
    scale_d = HEAD_DIM ** -0.5
    wd = N_HEADS_D * HEAD_DIM
    sb0 = o + MLA_ROPE_DIM
    w_in2 = jnp.concatenate([
        w_in[:, :o],
        _pad_cols(w_rope, MLA_NOPE_DIM, LANES),
        _pad_cols(_rotate_half_cols(w_rope), MLA_NOPE_DIM, LANES),
        w_in[:, sb0:sb0 + wd] * scale_d,
        w_in[:, sb0 + wd:],
    ], axis=1).astype(BF16)
    qd = MLA_NOPE_DIM + MLA_ROPE_DIM
    uq = w_uq.reshape(MLA_Q_RANK, N_HEADS_C, qd)
    plain = jnp.pad(uq, ((0, 0), (0, 0), (0, LANES - qd)))
    swapped = jnp.pad(_rotate_half_cols(uq[..., MLA_NOPE_DIM:]),
                      ((0, 0), (0, 0), (MLA_NOPE_DIM, LANES - qd)))
    wq = jnp.concatenate([plain.reshape(MLA_Q_RANK, -1), swapped.reshape(MLA_Q_RANK, -1)], axis=1).astype(BF16)
    ukv = w_ukv.reshape(MLA_KV_RANK, N_HEADS_C, MLA_NOPE_DIM + MLA_V_DIM)
    k_nope = jnp.pad(ukv[..., :MLA_NOPE_DIM], ((0, 0), (0, 0), (0, LANES - MLA_NOPE_DIM)))
    v = ukv[..., MLA_NOPE_DIM:]
    wkv = jnp.concatenate([k_nope.reshape(MLA_KV_RANK, -1), v.reshape(MLA_KV_RANK, -1)], axis=1).astype(BF16)
    return w_in2, wq, wkv


def _even_weights(w_in):
    wa = N_HEADS_A * HEAD_DIM
    scale = HEAD_DIM ** -0.5 * LOG2_E
    col = jnp.arange(w_in.shape[1])
    is_q = (col < wa) | ((col >= 3 * wa) & (col < 4 * wa))
    return (w_in * jnp.where(is_q, scale, 1.0)).astype(BF16)


def kernel(x, c, rel_bias, ada_w, ada_b, mix_pre_g, mix_post_g, ffn_pre_g, ffn_post_g, ab_w_in, ab_w_out,
           cd_w_in, mla_q_norm_g, mla_kv_norm_g, mla_w_uq, mla_w_ukv, cd_w_out, ffn_w_up, ffn_conv_w,
           ffn_conv_b, ffn_w_down):
    batch, seq, d = x.shape
    assert d == D_MODEL and seq % GROUP_KEYS == 0 and GROUP_KEYS % ROW_TILE == 0
    rows = batch * seq
    xf = x.reshape(rows, d)

    mods = _mods(c, ada_w, ada_b)
    bias_a = _bias_tiles(rel_bias, 0, N_HEADS_A, MOBA_BIAS_TILES, dilated=False)
    bias_b = _bias_tiles(rel_bias, N_HEADS_A, N_HEADS_B, DIL_BIAS_TILES, dilated=True)
    tables = _rope_tables(seq)
    step_blocks = N_HEADS_A // HEADS_PER_STEP

    for layer in range(DEPTH):
        mod = mods[layer].reshape(batch, 1, 6 * d)
        i = layer // 2
        if layer % 2 == 0:
            proj, km, vta, vtb = _even_proj(xf, mod, mix_pre_g[layer].reshape(1, d),
                                            _even_weights(ab_w_in[i]), seq)
            proj = proj.reshape(batch, seq, -1)
            km = km.reshape(batch, seq // MOBA_BLOCK, -1)
            o_first = _toeplitz_attn(proj, vta, bias_a, km, batch, seq, 0, moba=True)
            o_second = _toeplitz_attn(proj, vtb, bias_b, None, batch, seq, 3 * step_blocks, moba=False)
            w_out = ab_w_out[i]
        else:
            w_in2, wq, wkv = _odd_weights(cd_w_in[i], mla_w_uq[i], mla_w_ukv[i])
            qm, km, vtm, sb, vts = _odd_proj(xf, mod, mix_pre_g[layer].reshape(1, d), w_in2,
                                             mla_q_norm_g[i].reshape(1, -1), mla_kv_norm_g[i].reshape(1, -1),
                                             wq, wkv, tables, seq)
            o_first = _mla_attn(qm.reshape(batch, seq, -1), km.reshape(batch, seq, -1), vtm, batch, seq)
            o_second = _stick_attn(sb.reshape(batch, seq, -1), vts, batch, seq)
            w_out = cd_w_out[i]
        xf = _post_attn(o_first.reshape(rows, -1), o_second.reshape(rows, -1), xf, mod,
                        mix_post_g[layer].reshape(1, d), ffn_pre_g[layer].reshape(1, d),
                        ffn_post_g[layer].reshape(1, d), w_out.astype(BF16), ffn_w_up[layer].astype(BF16),
                        ffn_conv_w[layer], ffn_conv_b[layer].reshape(1, -1), ffn_w_down[layer].astype(BF16), seq)
    return xf.reshape(batch, seq, d)
```

```python
import functools
import math

import jax
import jax.numpy as jnp
from jax import lax
from jax.experimental import pallas as pl
from jax.experimental.pallas import tpu as pltpu

F32 = jnp.float32
BF16 = jnp.bfloat16

D_MODEL = 1024
DEPTH = 4
HEAD_DIM = 64
N_HEADS_A = 8
N_HEADS_B = 8
N_HEADS_C = 8
N_HEADS_D = 8
MOBA_BLOCK = 256
MOBA_TOPK = 3
DILATED_BRANCHES = ((128, 1), (512, 4), (2048, 16))
MLA_Q_RANK = 256
MLA_KV_RANK = 256
MLA_NOPE_DIM = 64
MLA_ROPE_DIM = 32
MLA_V_DIM = 64
ROPE_THETA = 10000.0
REL_BUCKETS = 32
REL_MAX_DIST = 2048
D_FF = 2816
CONV_WIDTH = 3
NORM_EPS = 1e-6
LOG2_E = math.log2(math.e)

LANES = 128
SUBLANES = 8
BF16_SUBLANES = 16
VMEM_LIMIT_BYTES = 56 * 1024 * 1024

ATT_TILE = MOBA_BLOCK
KEY_GROUP_EVEN = 4
KEY_GROUP_ODD = 8
PAIRS_PER_STEP = 2
HEADS_PER_STEP = 2 * PAIRS_PER_STEP
STEP_LANES = PAIRS_PER_STEP * LANES
ROW_TILE = 512
FF_CHUNK = 256
N_FF_CHUNKS = D_FF // FF_CHUNK
CONV_HALO = BF16_SUBLANES
MOBA_BIAS_TILES = REL_MAX_DIST // ATT_TILE + 2
DIL_MAX_TILE_DIST = DILATED_BRANCHES[-1][0] // ATT_TILE
DIL_BIAS_TILES = DIL_MAX_TILE_DIST + 1

_NT = (((1,), (1,)), ((), ()))


def _bucket_lower_bounds():
    max_exact = REL_BUCKETS // 2
    ratio = REL_MAX_DIST // max_exact
    n_log = REL_BUCKETS - max_exact
    lows = list(range(max_exact + 1))
    for k in range(1, n_log):
        d = lows[-1]
        while d ** n_log < (max_exact ** n_log) * (ratio ** k):
            d += 1
        lows.append(d)
    return lows


_BUCKET_LOW = _bucket_lower_bounds()


def _dot(a, b):
    return jnp.dot(a, b, preferred_element_type=F32)


def _dot_nt(a, b):
    return lax.dot_general(a, b, _NT, preferred_element_type=F32)


def _split_bf16(x):
    hi = x.astype(BF16)
    lo = (x - hi.astype(F32)).astype(BF16)
    return hi, lo


def _rms(x, g):
    return (x * lax.rsqrt(jnp.mean(x * x, axis=-1, keepdims=True) + NORM_EPS)) * g


def _prenorm(x, g, scale, shift):
    return _rms(x, g) * (1.0 + scale) + shift


def _resident(block_shape, index_map):
    return pl.BlockSpec(block_shape, index_map)


def _params(n_grid_dims):
    return pltpu.CompilerParams(dimension_semantics=("arbitrary",) * n_grid_dims,
                                vmem_limit_bytes=VMEM_LIMIT_BYTES)


def _mods_kernel(c_ref, w_ref, b_ref, o_ref):
    c = c_ref[...]
    cond = c * jax.nn.sigmoid(c)
    c_hi, c_lo = _split_bf16(cond)
    w_hi, w_lo = _split_bf16(w_ref[0])
    o_ref[0] = _dot(c_hi, w_hi) + _dot(c_hi, w_lo) + _dot(c_lo, w_hi) + b_ref[0]


def _mods(c, ada_w, ada_b):
    b, d = c.shape
    rows = BF16_SUBLANES
    n_out = ada_w.shape[-1]
    tn = n_out // 4
    c_pad = jnp.zeros((rows, d), F32).at[:b].set(c)
    out = pl.pallas_call(
        _mods_kernel,
        grid=(DEPTH, n_out // tn),
        in_specs=[pl.BlockSpec((rows, d), lambda l, j: (0, 0)),
                  pl.BlockSpec((1, d, tn), lambda l, j: (l, 0, j)),
                  pl.BlockSpec((1, 1, tn), lambda l, j: (l, 0, j))],
        out_specs=pl.BlockSpec((1, rows, tn), lambda l, j: (l, 0, j)),
        out_shape=jax.ShapeDtypeStruct((DEPTH, rows, n_out), F32),
        compiler_params=_params(2),
        name="ada_mods",
    )(c_pad, ada_w, ada_b.reshape(DEPTH, 1, n_out))
    return out[:, :b]


def _bias_tiles_kernel(tab_ref, o_ref, *, head_off, dilated):
    h = pl.program_id(0) + head_off
    d = pl.program_id(1)
    t = ATT_TILE
    dist = d * t + lax.broadcasted_iota(jnp.int32, (SUBLANES, 2 * t), 1) - t
    val = jnp.full(dist.shape, tab_ref[h, 0], F32)
    for b in range(1, REL_BUCKETS):
        val = jnp.where(dist >= _BUCKET_LOW[b], tab_ref[h, b], val)
    if dilated:
        mult = jnp.zeros(dist.shape, F32)
        for window, dil in DILATED_BRANCHES:
            hit = jnp.where(dist <= window, jnp.where((dist & (dil - 1)) == 0, 1.0, 0.0), 0.0)
            mult = mult + hit
        log_mult = jnp.where(mult > 2.5, math.log(3.0), jnp.where(mult > 1.5, math.log(2.0), 0.0))
        val = jnp.where(mult > 0.5, val + log_mult, -jnp.inf)
    val = jnp.where(dist >= 0, val * LOG2_E, -jnp.inf)
    strip = jnp.concatenate([val] * (t // SUBLANES), axis=0)
    rotated = pltpu.roll(strip, 0, 1, stride=1, stride_axis=0)
    o_ref[0, 0] = rotated[:, t:2 * t]


def _bias_tiles(rel_bias, head_off, n_heads, n_tiles, dilated):
    t = ATT_TILE
    return pl.pallas_call(
        functools.partial(_bias_tiles_kernel, head_off=head_off, dilated=dilated),
        grid=(n_heads, n_tiles),
        in_specs=[pl.BlockSpec(memory_space=pltpu.SMEM)],
        out_specs=pl.BlockSpec((1, 1, t, t), lambda h, d: (h, d, 0, 0)),
        out_shape=jax.ShapeDtypeStruct((n_heads, n_tiles, t, t), F32),
        compiler_params=_params(2),
        name="dilated_bias_tiles" if dilated else "moba_bias_tiles",
    )(rel_bias)


def _store_vt(vt_ref, v):
    vt = v.T.astype(BF16)
    for p in range(vt.shape[0] // LANES):
        vt_ref[0, p, 0] = vt[p * LANES:(p + 1) * LANES, :]


def _vt_spec(tm, per_seq, n_pairs, key_group):
    per_group = key_group * ATT_TILE // tm
    return pl.BlockSpec((1, n_pairs, 1, LANES, tm),
                        lambda i: (i // per_seq, 0, (i % per_seq) // per_group, 0, (i % per_seq) % per_group))


def _vt_shape(batch, seq, n_pairs, key_group):
    group_keys = key_group * ATT_TILE
    return jax.ShapeDtypeStruct((batch, n_pairs, seq // group_keys, LANES, group_keys), BF16)


def _even_proj_kernel(x_ref, mod_ref, g_ref, w_ref, o_ref, km_ref, vta_ref, vtb_ref):
    d = D_MODEL
    mod = mod_ref[0]
    h = _prenorm(x_ref[...], g_ref[...], mod[:, d:2 * d], mod[:, 0:d])
    p = _dot(h.astype(BF16), w_ref[...])
    o_ref[...] = p.astype(BF16)
    wa = N_HEADS_A * HEAD_DIM
    ka = p[:, wa:2 * wa]
    nb = ka.shape[0] // MOBA_BLOCK
    km_ref[0] = jnp.mean(ka.reshape(nb, MOBA_BLOCK, wa), axis=1)
    _store_vt(vta_ref, p[:, 2 * wa:3 * wa])
    _store_vt(vtb_ref, p[:, 5 * wa:6 * wa])


def _even_proj(x, mod, g, w_in, seq):
    rows, d = x.shape
    tm = ROW_TILE
    n = w_in.shape[1]
    wa = N_HEADS_A * HEAD_DIM
    per_seq = seq // tm
    n_pairs = N_HEADS_A // 2
    vt_spec = _vt_spec(tm, per_seq, n_pairs, KEY_GROUP_EVEN)
    vt_shape = _vt_shape(rows // seq, seq, n_pairs, KEY_GROUP_EVEN)
    return pl.pallas_call(
        _even_proj_kernel,
        grid=(rows // tm,),
        in_specs=[pl.BlockSpec((tm, d), lambda i: (i, 0)),
                  pl.BlockSpec((1, 1, 6 * d), lambda i: (i // per_seq, 0, 0)),
                  pl.BlockSpec((1, d), lambda i: (0, 0)),
                  pl.BlockSpec((d, n), lambda i: (0, 0))],
        out_specs=[pl.BlockSpec((tm, n), lambda i: (i, 0)),
                   pl.BlockSpec((1, tm // MOBA_BLOCK, wa), lambda i: (i, 0, 0)),
                   vt_spec, vt_spec],
        out_shape=[jax.ShapeDtypeStruct((rows, n), BF16),
                   jax.ShapeDtypeStruct((rows // tm, tm // MOBA_BLOCK, wa), F32),
                   vt_shape, vt_shape],
        compiler_params=_params(1),
        name="even_proj",
    )(x, mod, g, w_in)


def _pair_lanes(x, h):
    pair = h // 2
    return x[:, pair * LANES:(pair + 1) * LANES]


def _split_heads(q_step):
    first = lax.broadcasted_iota(jnp.int32, (q_step.shape[0], LANES), 1) < HEAD_DIM
    heads = []
    for pair in range(PAIRS_PER_STEP):
        q2 = q_step[:, pair * LANES:(pair + 1) * LANES]
        zero = jnp.zeros_like(q2)
        heads += [jnp.where(first, q2, zero), jnp.where(first, zero, q2)]
    return heads


def _head_rows(vt_ref, g, h, n_tiles):
    return vt_ref[0, h // 2, g, (h % 2) * HEAD_DIM:(h % 2 + 1) * HEAD_DIM, 0:n_tiles * ATT_TILE]


def _tile_scores(keys, q_head, per_tile):
    t = ATT_TILE
    n = keys.shape[0] // t
    if per_tile:
        return [_dot_nt(keys[c * t:(c + 1) * t], q_head) for c in range(n)]
    s = _dot_nt(keys, q_head)
    return [s[c * t:(c + 1) * t] for c in range(n)]


def _group_keys(k_ref, g, n_tiles, key_group):
    start = pl.multiple_of(g * key_group * ATT_TILE, key_group * ATT_TILE)
    return k_ref[0, pl.ds(start, n_tiles * ATT_TILE), :]


def _sweep(qi, group, key_group, n_far_groups=None):
    g_own = qi // key_group
    own = [functools.partial(group, g_own, None, n + 1, True) for n in range(key_group)]
    state = lax.switch(qi % key_group, own)
    n_past = g_own if n_far_groups is None else jnp.minimum(g_own, n_far_groups)
    return lax.fori_loop(0, n_past, lambda n, st: group(g_own - 1 - n, st, key_group, False), state)


def _tile_iotas():
    t = ATT_TILE
    return lax.broadcasted_iota(jnp.int32, (t, t), 0), lax.broadcasted_iota(jnp.int32, (t, t), 1)


def _store_step_output(o_ref, outs_t):
    o_ref[0] = jnp.concatenate(outs_t, axis=0).T.astype(BF16)


def _softmax_group(s_tiles, ons, vt_h, acc_ref, h, stats, first):
    maxes = []
    for s, on in zip(s_tiles, ons):
        mx = jnp.max(s, axis=0, keepdims=True)
        maxes.append(mx if on is None else jnp.where(on, mx, -jnp.inf))
    m_new = functools.reduce(jnp.maximum, maxes)
    if not first:
        m_old, l_old = stats
        m_new = jnp.maximum(m_new, m_old)
    ps = []
    l_add = None
    for s, on in zip(s_tiles, ons):
        p = jnp.exp2(s - (m_new if on is None else jnp.where(on, m_new, jnp.inf)))
        p_sum = jnp.sum(p, axis=0, keepdims=True)
        l_add = p_sum if l_add is None else l_add + p_sum
        ps.append(p.astype(BF16))
    pv = _dot(vt_h, jnp.concatenate(ps, axis=0))
    if first:
        acc_ref[h] = pv
        return m_new, l_add
    alpha = jnp.exp2(m_old - m_new)
    acc_ref[h] = alpha * acc_ref[h] + pv
    return m_new, alpha * l_old + l_add


def _moba_select(qh, km_ref, sel_ref, qi):
    t = ATT_TILE
    km_hi, km_lo = _split_bf16(km_ref[0])
    nb = km_hi.shape[0]
    blk = lax.broadcasted_iota(jnp.int32, (nb, t), 0)
    past = blk < qi
    for h in range(HEADS_PER_STEP):
        gate = _dot_nt(_pair_lanes(km_hi, h), qh[h]) + _dot_nt(_pair_lanes(km_lo, h), qh[h])
        gate = jnp.where(past, gate, -jnp.inf)
        beaten = jnp.zeros((nb, t), F32)
        for other in range(nb):
            row = gate[other:other + 1, :]
            wins = jnp.where(row > gate, 1.0, jnp.where(row == gate, jnp.where(blk > other, 1.0, 0.0), 0.0))
            beaten = beaten + wins
        sel_ref[h] = jnp.where(past, jnp.where(beaten < MOBA_TOPK, 1.0, 0.0), 0.0)


def _toeplitz_attn_kernel(*refs, moba):
    if moba:
        q_ref, k_ref, vt_ref, bias_ref, km_ref, o_ref, acc_ref, sel_ref = refs
    else:
        q_ref, k_ref, vt_ref, bias_ref, o_ref, acc_ref = refs
    qi = pl.program_id(2)
    qh = _split_heads(q_ref[0])
    n_bias = bias_ref.shape[1]
    if moba:
        _moba_select(qh, km_ref, sel_ref, qi)

    def group(g, stats, n_tiles, first):
        kg = _group_keys(k_ref, g, n_tiles, KEY_GROUP_EVEN)
        new = ()
        scores = [_tile_scores(_pair_lanes(kg, h), qh[h], per_tile=False) for h in range(HEADS_PER_STEP)]
        for h in range(HEADS_PER_STEP):
            s_tiles = scores[h]
            ons = []
            for c in range(n_tiles):
                j = g * KEY_GROUP_EVEN + c
                if first:
                    dt = n_tiles - 1 - c
                    s_tiles[c] = s_tiles[c] + bias_ref[h, dt]
                    ons.append(sel_ref[h, pl.ds(j, 1), :] > 0.5 if moba and dt > 0 else None)
                else:
                    dt = qi - j
                    s_tiles[c] = s_tiles[c] + bias_ref[h, jnp.minimum(dt, n_bias - 1)]
                    ons.append(sel_ref[h, pl.ds(j, 1), :] > 0.5 if moba else dt <= DIL_MAX_TILE_DIST)
            new += _softmax_group(s_tiles, ons, _head_rows(vt_ref, g, h, n_tiles), acc_ref, h,
                                  None if first else stats[2 * h:2 * h + 2], first)
        return new

    if moba:
        n_far = None
    else:
        n_far = qi // KEY_GROUP_EVEN - jnp.maximum(qi - DIL_MAX_TILE_DIST, 0) // KEY_GROUP_EVEN
    stats = _sweep(qi, group, KEY_GROUP_EVEN, n_far)
    _store_step_output(o_ref, [acc_ref[h] / stats[2 * h + 1] for h in range(HEADS_PER_STEP)])


def _toeplitz_attn(proj, vt, bias, km, batch, seq, col0, moba):
    t = ATT_TILE
    n_steps = N_HEADS_A // HEADS_PER_STEP
    n_bias = bias.shape[1]
    in_specs = [pl.BlockSpec((1, t, STEP_LANES), lambda b, p, i: (b, i, col0 + p)),
                _resident((1, seq, STEP_LANES), lambda b, p, i: (b, 0, col0 + n_steps + p)),
                _resident((1, PAIRS_PER_STEP) + vt.shape[2:], lambda b, p, i: (b, p, 0, 0, 0)),
                _resident((HEADS_PER_STEP, n_bias, t, t), lambda b, p, i: (p, 0, 0, 0))]
    args = [proj, proj, vt, bias]
    scratch = [pltpu.VMEM((HEADS_PER_STEP, HEAD_DIM, t), F32)]
    if moba:
        in_specs.append(pl.BlockSpec((1, seq // MOBA_BLOCK, STEP_LANES), lambda b, p, i: (b, 0, p)))
        args.append(km)
        scratch.append(pltpu.VMEM((HEADS_PER_STEP, seq // MOBA_BLOCK, t), F32))
    return pl.pallas_call(
        functools.partial(_toeplitz_attn_kernel, moba=moba),
        grid=(batch, n_steps, seq // t),
        in_specs=in_specs,
        out_specs=pl.BlockSpec((1, t, STEP_LANES), lambda b, p, i: (b, i, p)),
        out_shape=jax.ShapeDtypeStruct((batch, seq, N_HEADS_A * HEAD_DIM), BF16),
        scratch_shapes=scratch,
        compiler_params=_params(3),
        name="moba_attn" if moba else "dilated_attn",
    )(*args)


def _odd_proj_kernel(x_ref, mod_ref, g_ref, win_ref, gq_ref, gkv_ref, wq_ref, wkv_ref,
                     cq_ref, sq_ref, ck_ref, sk_ref, qm_ref, km_ref, vtm_ref, sb_ref, vts_ref):
    d = D_MODEL
    mod = mod_ref[0]
    h = _prenorm(x_ref[...], g_ref[...], mod[:, d:2 * d], mod[:, 0:d])
    p = _dot(h.astype(BF16), win_ref[...])
    o = MLA_Q_RANK + MLA_KV_RANK
    c_q, c_kv = p[:, 0:MLA_Q_RANK], p[:, MLA_Q_RANK:o]
    k_rope, k_rope_swapped = p[:, o:o + LANES], p[:, o + LANES:o + 2 * LANES]
    sb0 = o + 2 * LANES
    n_qk = 2 * N_HEADS_D * HEAD_DIM
    sb_ref[...] = p[:, sb0:sb0 + n_qk].astype(BF16)
    _store_vt(vts_ref, p[:, sb0 + n_qk:])
    q12 = _dot(_rms(c_q, gq_ref[...]).astype(BF16), wq_ref[...])
    kv = _dot(_rms(c_kv, gkv_ref[...]).astype(BF16), wkv_ref[...])
    k_pe = k_rope * ck_ref[...] + k_rope_swapped * sk_ref[...]
    cq, sq = cq_ref[...], sq_ref[...]
    half = N_HEADS_C * LANES
    for hh in range(N_HEADS_C):
        cols = slice(hh * LANES, (hh + 1) * LANES)
        swapped = slice(half + hh * LANES, half + (hh + 1) * LANES)
        qm_ref[:, cols] = (q12[:, cols] * cq + q12[:, swapped] * sq).astype(BF16)
        km_ref[:, cols] = (kv[:, cols] + k_pe).astype(BF16)
    _store_vt(vtm_ref, kv[:, half:])


def _odd_proj(x, mod, g, w_in, gq, gkv, wq, wkv, tables, seq):
    rows, d = x.shape
    tm = ROW_TILE
    per_seq = seq // tm
    n_in = w_in.shape[1]
    n_sb = 2 * N_HEADS_D * HEAD_DIM
    n_q = N_HEADS_C * LANES
    n_pairs = N_HEADS_C // 2
    const = lambda i: (0, 0)
    table_spec = pl.BlockSpec((tm, LANES), lambda i: (i % per_seq, 0))
    vt_spec = _vt_spec(tm, per_seq, n_pairs, KEY_GROUP_ODD)
    vt_shape = _vt_shape(rows // seq, seq, n_pairs, KEY_GROUP_ODD)
    return pl.pallas_call(
        _odd_proj_kernel,
        grid=(rows // tm,),
        in_specs=[pl.BlockSpec((tm, d), lambda i: (i, 0)),
                  pl.BlockSpec((1, 1, 6 * d), lambda i: (i // per_seq, 0, 0)),
                  pl.BlockSpec((1, d), const),
                  pl.BlockSpec((d, n_in), const),
                  pl.BlockSpec((1, MLA_Q_RANK), const),
                  pl.BlockSpec((1, MLA_KV_RANK), const),
                  pl.BlockSpec(wq.shape, const),
                  pl.BlockSpec(wkv.shape, const),
                  table_spec, table_spec, table_spec, table_spec],
        out_specs=[pl.BlockSpec((tm, n_q), lambda i: (i, 0)),
                   pl.BlockSpec((tm, n_q), lambda i: (i, 0)),
                   vt_spec,
                   pl.BlockSpec((tm, n_sb), lambda i: (i, 0)),
                   vt_spec],
        out_shape=[jax.ShapeDtypeStruct((rows, n_q), BF16),
                   jax.ShapeDtypeStruct((rows, n_q), BF16),
                   vt_shape,
                   jax.ShapeDtypeStruct((rows, n_sb), BF16),
                   vt_shape],
        compiler_params=_params(1),
        name="odd_proj",
    )(x, mod, g, w_in, gq, gkv, wq, wkv, *tables)


def _mla_attn_kernel(q_ref, k_ref, vt_ref, o_ref, acc_ref):
    qi = pl.program_id(2)
    q_step = q_ref[0]
    qh = [q_step[:, h * LANES:(h + 1) * LANES] for h in range(HEADS_PER_STEP)]

    def group(g, stats, n_tiles, first):
        kg = _group_keys(k_ref, g, n_tiles, KEY_GROUP_ODD)
        new = ()
        scores = [_tile_scores(kg[:, h * LANES:(h + 1) * LANES], qh[h], per_tile=True)
                  for h in range(HEADS_PER_STEP)]
        for h in range(HEADS_PER_STEP):
            s_tiles = scores[h]
            if first:
                key, query = _tile_iotas()
                s_tiles[-1] = jnp.where(key <= query, s_tiles[-1], -jnp.inf)
            new += _softmax_group(s_tiles, [None] * n_tiles, _head_rows(vt_ref, g, h, n_tiles), acc_ref, h,
                                  None if first else stats[2 * h:2 * h + 2], first)
        return new

    stats = _sweep(qi, group, KEY_GROUP_ODD)
    _store_step_output(o_ref, [acc_ref[h] / stats[2 * h + 1] for h in range(HEADS_PER_STEP)])


def _mla_attn(qm, km, vt, batch, seq):
    t = ATT_TILE
    n_steps = N_HEADS_C // HEADS_PER_STEP
    qk_lanes = HEADS_PER_STEP * LANES
    return pl.pallas_call(
        _mla_attn_kernel,
        grid=(batch, n_steps, seq // t),
        in_specs=[pl.BlockSpec((1, t, qk_lanes), lambda b, p, i: (b, i, p)),
                  _resident((1, seq, qk_lanes), lambda b, p, i: (b, 0, p)),
                  _resident((1, PAIRS_PER_STEP) + vt.shape[2:], lambda b, p, i: (b, p, 0, 0, 0))],
        out_specs=pl.BlockSpec((1, t, STEP_LANES), lambda b, p, i: (b, i, p)),
        out_shape=jax.ShapeDtypeStruct((batch, seq, N_HEADS_C * MLA_V_DIM), BF16),
        scratch_shapes=[pltpu.VMEM((HEADS_PER_STEP, HEAD_DIM, t), F32)],
        compiler_params=_params(3),
        name="mla_attn",
    )(qm, km, vt)


def _stick_attn_kernel(q_ref, k_ref, vt_ref, o_ref, acc_ref):
    t = ATT_TILE
    qi = pl.program_id(2)
    qh = _split_heads(q_ref[0])
    key = lax.broadcasted_iota(jnp.int32, (t, t), 0)
    other = lax.broadcasted_iota(jnp.int32, (t, t), 1)
    from_here = jnp.where(other >= key, 1.0, 0.0).astype(BF16)
    from_here2 = jnp.concatenate([from_here, from_here], axis=1)

    strict = key < other

    def group(g, carries, n_tiles, first):
        kg = _group_keys(k_ref, g, n_tiles, KEY_GROUP_ODD)
        new = ()
        scores = [_tile_scores(_pair_lanes(kg, h), qh[h], per_tile=False) for h in range(HEADS_PER_STEP)]
        for h in range(HEADS_PER_STEP):
            z_tiles = scores[h]
            keep_tiles = []
            for z in z_tiles:
                neg_z = -z
                keep_tiles.append(jnp.minimum(neg_z, 0.0) - jnp.log(1.0 + jnp.exp(jnp.minimum(z, neg_z))))
            if first:
                keep_tiles[-1] = jnp.where(strict, keep_tiles[-1], 0.0)
            run = None if first else carries[h]
            a_tiles = [None] * n_tiles
            for c in reversed(range(n_tiles)):
                tail = _dot(from_here2, jnp.concatenate(_split_bf16(keep_tiles[c]), axis=0))
                tile_total = tail[0:1, :]
                if run is not None:
                    tail = tail + run
                a_tiles[c] = jnp.exp(z_tiles[c] + tail)
                run = tile_total if run is None else run + tile_total
            if first:
                a_tiles[-1] = jnp.where(strict, a_tiles[-1], 0.0)
            pv = _dot(_head_rows(vt_ref, g, h, n_tiles), jnp.concatenate(a_tiles, axis=0).astype(BF16))
            acc_ref[h] = pv if first else acc_ref[h] + pv
            new += (run,)
        return new

    _sweep(qi, group, KEY_GROUP_ODD)
    _store_step_output(o_ref, [acc_ref[h] for h in range(HEADS_PER_STEP)])


def _stick_attn(sb, vt, batch, seq):
    t = ATT_TILE
    n_steps = N_HEADS_D // HEADS_PER_STEP
    return pl.pallas_call(
        _stick_attn_kernel,
        grid=(batch, n_steps, seq // t),
        in_specs=[pl.BlockSpec((1, t, STEP_LANES), lambda b, p, i: (b, i, p)),
                  _resident((1, seq, STEP_LANES), lambda b, p, i: (b, 0, n_steps + p)),
                  _resident((1, PAIRS_PER_STEP) + vt.shape[2:], lambda b, p, i: (b, p, 0, 0, 0))],
        out_specs=pl.BlockSpec((1, t, STEP_LANES), lambda b, p, i: (b, i, p)),
        out_shape=jax.ShapeDtypeStruct((batch, seq, N_HEADS_D * HEAD_DIM), BF16),
        scratch_shapes=[pltpu.VMEM((HEADS_PER_STEP, HEAD_DIM, t), F32)],
        compiler_params=_params(3),
        name="stick_attn",
    )(sb, sb, vt)


def _post_attn_kernel(oa_ref, ob_ref, oah_ref, obh_ref, x_ref, xh_ref, mod_ref, gmix_ref, gpre_ref, gpost_ref,
                      wo_ref, wup_ref, cw_ref, cb_ref, wd_ref, o_ref, x1_ref, h_ref, u_ref, acc_ref, *, per_seq):
    d = D_MODEL
    tm = ROW_TILE
    halo = CONV_HALO
    cw = FF_CHUNK
    i = pl.program_id(0)
    mod = mod_ref[0]
    gate_m, shift, scale, gate_f = mod[:, 2 * d:3 * d], mod[:, 3 * d:4 * d], mod[:, 4 * d:5 * d], mod[:, 5 * d:6 * d]
    n_first = oa_ref.shape[1]

    def mixed(oa, ob, x):
        y = _dot(oa, wo_ref[0:n_first, :]) + _dot(ob, wo_ref[n_first:2 * n_first, :])
        return x + gate_m * _rms(y, gmix_ref[...])

    g = gpre_ref[...]
    ahead = _prenorm(mixed(oah_ref[...], obh_ref[...], xh_ref[...]), g, scale, shift)
    ahead = jnp.where(i % per_seq == 0, 0.0, ahead)
    h_ref[0:halo, :] = ahead.astype(BF16)
    x1_ref[...] = mixed(oa_ref[...], ob_ref[...], x_ref[...])
    h_ref[halo:halo + tm, :] = _prenorm(x1_ref[...], g, scale, shift).astype(BF16)

    def up(ch):
        slot = ch % 2
        for half in range(2):
            cols = slice(half * D_FF + ch * cw, half * D_FF + (ch + 1) * cw)
            u_ref[slot, half] = _dot(h_ref[...], wup_ref[:, cols])

    def conv(slot, half, ch):
        cols = slice(half * D_FF + ch * cw, half * D_FF + (ch + 1) * cw)
        w = cw_ref[:, cols]
        out = w[0:1, :] * u_ref[slot, half, halo - 2:halo - 2 + tm, :]
        out = out + w[1:2, :] * u_ref[slot, half, halo - 1:halo - 1 + tm, :]
        out = out + w[2:3, :] * u_ref[slot, half, halo:halo + tm, :]
        return out + cb_ref[:, cols]

    def down(ch, act):
        part = _dot(act, wd_ref[ch * cw:(ch + 1) * cw, :])
        if ch == 0:
            acc_ref[...] = part
        else:
            acc_ref[...] += part

    up(0)
    act = None
    for ch in range(N_FF_CHUNKS):
        if ch + 1 < N_FF_CHUNKS:
            up(ch + 1)
        if act is not None:
            down(ch - 1, act)
        slot = ch % 2
        act = (jax.nn.gelu(conv(slot, 0, ch), approximate=True) * conv(slot, 1, ch)).astype(BF16)
    down(N_FF_CHUNKS - 1, act)

    o_ref[...] = x1_ref[...] + gate_f * _rms(acc_ref[...], gpost_ref[...])


def _post_attn(oa, ob, x, mod, g_mix, g_pre, g_post, w_out, w_up, conv_w, conv_b, w_down, seq):
    rows, d = x.shape
    tm = ROW_TILE
    halo = CONV_HALO
    per_seq = seq // tm
    n_first = oa.shape[1]
    const = lambda i: (0, 0)
    tile = lambda i: (i, 0)
    ahead = lambda i: (jnp.maximum(i * (tm // halo) - 1, 0), 0)
    once = dict(pipeline_mode=pl.Buffered(1))
    return pl.pallas_call(
        functools.partial(_post_attn_kernel, per_seq=per_seq),
        grid=(rows // tm,),
        in_specs=[pl.BlockSpec((tm, n_first), tile), pl.BlockSpec((tm, n_first), tile),
                  pl.BlockSpec((halo, n_first), ahead), pl.BlockSpec((halo, n_first), ahead),
                  pl.BlockSpec((tm, d), tile), pl.BlockSpec((halo, d), ahead),
                  pl.BlockSpec((1, 1, 6 * d), lambda i: (i // per_seq, 0, 0)),
                  pl.BlockSpec((1, d), const), pl.BlockSpec((1, d), const), pl.BlockSpec((1, d), const),
                  pl.BlockSpec(w_out.shape, const, **once),
                  pl.BlockSpec(w_up.shape, const, **once),
                  pl.BlockSpec(conv_w.shape, const),
                  pl.BlockSpec(conv_b.shape, const),
                  pl.BlockSpec(w_down.shape, const, **once)],
        out_specs=pl.BlockSpec((tm, d), tile),
        out_shape=jax.ShapeDtypeStruct((rows, d), F32),
        scratch_shapes=[pltpu.VMEM((tm, d), F32),
                        pltpu.VMEM((tm + halo, d), BF16),
                        pltpu.VMEM((2, 2, tm + halo, FF_CHUNK), F32),
                        pltpu.VMEM((tm, d), F32)],
        compiler_params=_params(1),
        name="post_attn",
    )(oa, ob, oa, ob, x, x, mod, g_mix, g_pre, g_post, w_out, w_up, conv_w, conv_b, w_down)


def _rotate_half_cols(w):
    half = w.shape[-1] // 2
    return jnp.concatenate([-w[..., half:], w[..., :half]], axis=-1)


def _pad_cols(w, left, total):
    return jnp.pad(w, ((0, 0), (left, total - left - w.shape[1])))


def _rope_tables(seq):
    inv_freq = 1.0 / (ROPE_THETA ** (jnp.arange(0, MLA_ROPE_DIM, 2, dtype=F32) / MLA_ROPE_DIM))
    ang = jnp.arange(seq, dtype=F32)[:, None] * inv_freq[None, :]
    cos, sin = jnp.cos(ang), jnp.sin(ang)
    cos2 = _pad_cols(jnp.concatenate([cos, cos], axis=1), MLA_NOPE_DIM, LANES)
    sin2 = _pad_cols(jnp.concatenate([sin, sin], axis=1), MLA_NOPE_DIM, LANES)
    scale = (MLA_NOPE_DIM + MLA_ROPE_DIM) ** -0.5
    nope_ones = _pad_cols(jnp.ones((seq, MLA_NOPE_DIM), F32), 0, LANES)
    scale = scale * LOG2_E
    return (scale * (cos2 + nope_ones), scale * sin2, cos2, sin2)


def _odd_weights(w_in, w_uq, w_ukv):
    o = MLA_Q_RANK + MLA_KV_RANK
    w_rope = w_in[:, o:o + MLA_ROPE_DIM]
    scale_d = HEAD_DIM ** -0.5
    wd = N_HEADS_D * HEAD_DIM
    sb0 = o + MLA_ROPE_DIM
    w_in2 = jnp.concatenate([
        w_in[:, :o],
        _pad_cols(w_rope, MLA_NOPE_DIM, LANES),
        _pad_cols(_rotate_half_cols(w_rope), MLA_NOPE_DIM, LANES),
        w_in[:, sb0:sb0 + wd] * scale_d,
        w_in[:, sb0 + wd:],
    ], axis=1).astype(BF16)
    qd = MLA_NOPE_DIM + MLA_ROPE_DIM
    uq = w_uq.reshape(MLA_Q_RANK, N_HEADS_C, qd)
    plain = jnp.pad(uq, ((0, 0), (0, 0), (0, LANES - qd)))
    swapped = jnp.pad(_rotate_half_cols(uq[..., MLA_NOPE_DIM:]),
                      ((0, 0), (0, 0), (MLA_NOPE_DIM, LANES - qd)))
    wq = jnp.concatenate([plain.reshape(MLA_Q_RANK, -1), swapped.reshape(MLA_Q_RANK, -1)], axis=1).astype(BF16)
    ukv = w_ukv.reshape(MLA_KV_RANK, N_HEADS_C, MLA_NOPE_DIM + MLA_V_DIM)
    k_nope = jnp.pad(ukv[..., :MLA_NOPE_DIM], ((0, 0), (0, 0), (0, LANES - MLA_NOPE_DIM)))
    v = ukv[..., MLA_NOPE_DIM:]
    wkv = jnp.concatenate([k_nope.reshape(MLA_KV_RANK, -1), v.reshape(MLA_KV_RANK, -1)], axis=1).astype(BF16)
    return w_in2, wq, wkv


def _even_weights(w_in):
    wa = N_HEADS_A * HEAD_DIM
    scale = HEAD_DIM ** -0.5 * LOG2_E
    col = jnp.arange(w_in.shape[1])
    is_q = (col < wa) | ((col >= 3 * wa) & (col < 4 * wa))
    return (w_in * jnp.where(is_q, scale, 1.0)).astype(BF16)


def kernel(x, c, rel_bias, ada_w, ada_b, mix_pre_g, mix_post_g, ffn_pre_g, ffn_post_g, ab_w_in, ab_w_out,
           cd_w_in, mla_q_norm_g, mla_kv_norm_g, mla_w_uq, mla_w_ukv, cd_w_out, ffn_w_up, ffn_conv_w,
           ffn_conv_b, ffn_w_down):
    batch, seq, d = x.shape
    widest_group = max(KEY_GROUP_EVEN, KEY_GROUP_ODD) * ATT_TILE
    assert d == D_MODEL and seq % widest_group == 0 and (KEY_GROUP_EVEN * ATT_TILE) % ROW_TILE == 0
    rows = batch * seq
    xf = x.reshape(rows, d)

    mods = _mods(c, ada_w, ada_b)
    bias_a = _bias_tiles(rel_bias, 0, N_HEADS_A, MOBA_BIAS_TILES, dilated=False)
    bias_b = _bias_tiles(rel_bias, N_HEADS_A, N_HEADS_B, DIL_BIAS_TILES, dilated=True)
    tables = _rope_tables(seq)
    step_blocks = N_HEADS_A // HEADS_PER_STEP

    for layer in range(DEPTH):
        mod = mods[layer].reshape(batch, 1, 6 * d)
        i = layer // 2
        if layer % 2 == 0:
            proj, km, vta, vtb = _even_proj(xf, mod, mix_pre_g[layer].reshape(1, d),
                                            _even_weights(ab_w_in[i]), seq)
            proj = proj.reshape(batch, seq, -1)
            km = km.reshape(batch, seq // MOBA_BLOCK, -1)
            o_first = _toeplitz_attn(proj, vta, bias_a, km, batch, seq, 0, moba=True)
            o_second = _toeplitz_attn(proj, vtb, bias_b, None, batch, seq, 3 * step_blocks, moba=False)
            w_out = ab_w_out[i]
        else:
            w_in2, wq, wkv = _odd_weights(cd_w_in[i], mla_w_uq[i], mla_w_ukv[i])
            qm, km, vtm, sb, vts = _odd_proj(xf, mod, mix_pre_g[layer].reshape(1, d), w_in2,
                                             mla_q_norm_g[i].reshape(1, -1), mla_kv_norm_g[i].reshape(1, -1),
                                             wq, wkv, tables, seq)
            o_first = _mla_attn(qm.reshape(batch, seq, -1), km.reshape(batch, seq, -1), vtm, batch, seq)
            o_second = _stick_attn(sb.reshape(batch, seq, -1), vts, batch, seq)
            w_out = cd_w_out[i]
        xf = _post_attn(o_first.reshape(rows, -1), o_second.reshape(rows, -1), xf, mod,
                        mix_post_g[layer].reshape(1, d), ffn_pre_g[layer].reshape(1, d),
                        ffn_post_g[layer].reshape(1, d), w_out.astype(BF16), ffn_w_up[layer].astype(BF16),
                        ffn_conv_w[layer], ffn_conv_b[layer].reshape(1, -1), ffn_w_down[layer].astype(BF16), seq)
    return xf.reshape(batch, seq, d)
```

```python
import functools
import math

import jax
import jax.numpy as jnp
from jax import lax
from jax.experimental import pallas as pl
from jax.experimental.pallas import tpu as pltpu

F32 = jnp.float32
BF16 = jnp.bfloat16

D_MODEL = 1024
DEPTH = 4
HEAD_DIM = 64
N_HEADS_A = 8
N_HEADS_B = 8
N_HEADS_C = 8
N_HEADS_D = 8
MOBA_BLOCK = 256
MOBA_TOPK = 3
DILATED_BRANCHES = ((128, 1), (512, 4), (2048, 16))
MLA_Q_RANK = 256
MLA_KV_RANK = 256
MLA_NOPE_DIM = 64
MLA_ROPE_DIM = 32
MLA_V_DIM = 64
ROPE_THETA = 10000.0
REL_BUCKETS = 32
REL_MAX_DIST = 2048
D_FF = 2816
CONV_WIDTH = 3
NORM_EPS = 1e-6
LOG2_E = math.log2(math.e)

LANES = 128
SUBLANES = 8
BF16_SUBLANES = 16
VMEM_LIMIT_BYTES = 56 * 1024 * 1024

ATT_TILE = MOBA_BLOCK
KEY_GROUP_EVEN = 4
KEY_GROUP_ODD = 8
PAIRS_PER_STEP = 2
HEADS_PER_STEP = 2 * PAIRS_PER_STEP
STEP_LANES = PAIRS_PER_STEP * LANES
ROW_TILE = 512
FF_CHUNK = 256
N_FF_CHUNKS = D_FF // FF_CHUNK
CONV_HALO = BF16_SUBLANES
MOBA_BIAS_TILES = REL_MAX_DIST // ATT_TILE + 2
DIL_SPLIT = DILATED_BRANCHES[1][0]
DIL_CLASSES = DILATED_BRANCHES[2][1]
DIL_NEAR_TILES = DIL_SPLIT // ATT_TILE + 1
DIL_FAR_CLASSES_PER_STEP = 4
DIL_NEAR_BRANCHES = tuple((-1, min(window, DIL_SPLIT), dil) for window, dil in DILATED_BRANCHES)
DIL_FAR_BRANCHES = ((DIL_SPLIT, DILATED_BRANCHES[2][0], DIL_CLASSES),)
assert all(window <= DIL_SPLIT for window, _ in DILATED_BRANCHES[:2]) and DIL_SPLIT % DIL_CLASSES == 0

_NT = (((1,), (1,)), ((), ()))


def _bucket_lower_bounds():
    max_exact = REL_BUCKETS // 2
    ratio = REL_MAX_DIST // max_exact
    n_log = REL_BUCKETS - max_exact
    lows = list(range(max_exact + 1))
    for k in range(1, n_log):
        d = lows[-1]
        while d ** n_log < (max_exact ** n_log) * (ratio ** k):
            d += 1
        lows.append(d)
    return lows


_BUCKET_LOW = _bucket_lower_bounds()


def _dot(a, b):
    return jnp.dot(a, b, preferred_element_type=F32)


def _dot_nt(a, b):
    return lax.dot_general(a, b, _NT, preferred_element_type=F32)


def _split_bf16(x):
    hi = x.astype(BF16)
    lo = (x - hi.astype(F32)).astype(BF16)
    return hi, lo


def _rms(x, g):
    return (x * lax.rsqrt(jnp.mean(x * x, axis=-1, keepdims=True) + NORM_EPS)) * g


def _prenorm(x, g, scale, shift):
    return _rms(x, g) * (1.0 + scale) + shift


def _resident(block_shape, index_map):
    return pl.BlockSpec(block_shape, index_map)


def _params(n_grid_dims):
    return pltpu.CompilerParams(dimension_semantics=("arbitrary",) * n_grid_dims,
                                vmem_limit_bytes=VMEM_LIMIT_BYTES)


def _mods_kernel(c_ref, w_ref, b_ref, o_ref):
    c = c_ref[...]
    cond = c * jax.nn.sigmoid(c)
    c_hi, c_lo = _split_bf16(cond)
    w_hi, w_lo = _split_bf16(w_ref[0])
    o_ref[0] = _dot(c_hi, w_hi) + _dot(c_hi, w_lo) + _dot(c_lo, w_hi) + b_ref[0]


def _mods(c, ada_w, ada_b):
    b, d = c.shape
    rows = BF16_SUBLANES
    n_out = ada_w.shape[-1]
    tn = n_out // 4
    c_pad = jnp.zeros((rows, d), F32).at[:b].set(c)
    out = pl.pallas_call(
        _mods_kernel,
        grid=(DEPTH, n_out // tn),
        in_specs=[pl.BlockSpec((rows, d), lambda l, j: (0, 0)),
                  pl.BlockSpec((1, d, tn), lambda l, j: (l, 0, j)),
                  pl.BlockSpec((1, 1, tn), lambda l, j: (l, 0, j))],
        out_specs=pl.BlockSpec((1, rows, tn), lambda l, j: (l, 0, j)),
        out_shape=jax.ShapeDtypeStruct((DEPTH, rows, n_out), F32),
        compiler_params=_params(2),
        name="ada_mods",
    )(c_pad, ada_w, ada_b.reshape(DEPTH, 1, n_out))
    return out[:, :b]


def _bias_tiles_kernel(tab_ref, o_ref, *, head_off, step, branches):
    h = pl.program_id(0) + head_off
    d = pl.program_id(1)
    t = ATT_TILE
    dist = (d * t + lax.broadcasted_iota(jnp.int32, (SUBLANES, 2 * t), 1) - t) * step
    val = jnp.full(dist.shape, tab_ref[h, 0], F32)
    for b in range(1, REL_BUCKETS):
        val = jnp.where(dist >= _BUCKET_LOW[b], tab_ref[h, b], val)
    if branches is not None:
        mult = jnp.zeros(dist.shape, F32)
        for beyond, window, dil in branches:
            hit = jnp.where(dist <= window, jnp.where((dist & (dil - 1)) == 0, 1.0, 0.0), 0.0)
            mult = mult + jnp.where(dist > beyond, hit, 0.0)
        log_mult = jnp.where(mult > 2.5, math.log(3.0), jnp.where(mult > 1.5, math.log(2.0), 0.0))
        val = jnp.where(mult > 0.5, val + log_mult, -jnp.inf)
    val = jnp.where(dist >= 0, val * LOG2_E, -jnp.inf)
    strip = jnp.concatenate([val] * (t // SUBLANES), axis=0)
    rotated = pltpu.roll(strip, 0, 1, stride=1, stride_axis=0)
    o_ref[0, 0] = rotated[:, t:2 * t]


def _bias_tiles(rel_bias, head_off, n_heads, n_tiles, step, branches, name):
    t = ATT_TILE
    return pl.pallas_call(
        functools.partial(_bias_tiles_kernel, head_off=head_off, step=step, branches=branches),
        grid=(n_heads, n_tiles),
        in_specs=[pl.BlockSpec(memory_space=pltpu.SMEM)],
        out_specs=pl.BlockSpec((1, 1, t, t), lambda h, d: (h, d, 0, 0)),
        out_shape=jax.ShapeDtypeStruct((n_heads, n_tiles, t, t), F32),
        compiler_params=_params(2),
        name=name,
    )(rel_bias)


def _store_vt(vt_ref, v):
    vt = v.T.astype(BF16)
    for p in range(vt.shape[0] // LANES):
        vt_ref[0, p, 0] = vt[p * LANES:(p + 1) * LANES, :]


def _vt_spec(tm, per_seq, n_pairs, key_group):
    per_group = key_group * ATT_TILE // tm
    return pl.BlockSpec((1, n_pairs, 1, LANES, tm),
                        lambda i: (i // per_seq, 0, (i % per_seq) // per_group, 0, (i % per_seq) % per_group))


def _vt_shape(batch, seq, n_pairs, key_group):
    group_keys = key_group * ATT_TILE
    return jax.ShapeDtypeStruct((batch, n_pairs, seq // group_keys, LANES, group_keys), BF16)


def _even_proj_kernel(x_ref, mod_ref, g_ref, w_ref, o_ref, km_ref, vta_ref, vtb_ref):
    d = D_MODEL
    mod = mod_ref[0]
    h = _prenorm(x_ref[...], g_ref[...], mod[:, d:2 * d], mod[:, 0:d])
    p = _dot(h.astype(BF16), w_ref[...])
    o_ref[...] = p.astype(BF16)
    wa = N_HEADS_A * HEAD_DIM
    ka = p[:, wa:2 * wa]
    nb = ka.shape[0] // MOBA_BLOCK
    km_ref[0] = jnp.mean(ka.reshape(nb, MOBA_BLOCK, wa), axis=1)
    _store_vt(vta_ref, p[:, 2 * wa:3 * wa])
    vtb = p[:, 5 * wa:6 * wa].T.astype(BF16)
    t = ATT_TILE
    for pair in range(vtb.shape[0] // LANES):
        for tile in range(vtb.shape[1] // t):
            vtb_ref[0, pair, tile] = vtb[pair * LANES:(pair + 1) * LANES, tile * t:(tile + 1) * t]


def _even_proj(x, mod, g, w_in, seq):
    rows, d = x.shape
    tm = ROW_TILE
    n = w_in.shape[1]
    wa = N_HEADS_A * HEAD_DIM
    per_seq = seq // tm
    n_pairs = N_HEADS_A // 2
    vt_spec = _vt_spec(tm, per_seq, n_pairs, KEY_GROUP_EVEN)
    vt_shape = _vt_shape(rows // seq, seq, n_pairs, KEY_GROUP_EVEN)
    tiles_per_step = tm // ATT_TILE
    vt_tile_spec = pl.BlockSpec((1, n_pairs, tiles_per_step, LANES, ATT_TILE),
                                lambda i: (i // per_seq, 0, i % per_seq, 0, 0))
    return pl.pallas_call(
        _even_proj_kernel,
        grid=(rows // tm,),
        in_specs=[pl.BlockSpec((tm, d), lambda i: (i, 0)),
                  pl.BlockSpec((1, 1, 6 * d), lambda i: (i // per_seq, 0, 0)),
                  pl.BlockSpec((1, d), lambda i: (0, 0)),
                  pl.BlockSpec((d, n), lambda i: (0, 0))],
        out_specs=[pl.BlockSpec((tm, n), lambda i: (i, 0)),
                   pl.BlockSpec((1, tm // MOBA_BLOCK, wa), lambda i: (i, 0, 0)),
                   vt_spec, vt_tile_spec],
        out_shape=[jax.ShapeDtypeStruct((rows, n), BF16),
                   jax.ShapeDtypeStruct((rows // tm, tm // MOBA_BLOCK, wa), F32),
                   vt_shape, _vt_shape(rows // seq, seq, n_pairs, 1)],
        compiler_params=_params(1),
        name="even_proj",
    )(x, mod, g, w_in)


def _pair_lanes(x, h):
    pair = h // 2
    return x[:, pair * LANES:(pair + 1) * LANES]


def _split_heads(q_step):
    first = lax.broadcasted_iota(jnp.int32, (q_step.shape[0], LANES), 1) < HEAD_DIM
    heads = []
    for pair in range(PAIRS_PER_STEP):
        q2 = q_step[:, pair * LANES:(pair + 1) * LANES]
        zero = jnp.zeros_like(q2)
        heads += [jnp.where(first, q2, zero), jnp.where(first, zero, q2)]
    return heads


def _head_rows(vt_ref, g, h, n_tiles):
    return vt_ref[0, h // 2, g, (h % 2) * HEAD_DIM:(h % 2 + 1) * HEAD_DIM, 0:n_tiles * ATT_TILE]


def _tile_scores(keys, q_head, per_tile):
    t = ATT_TILE
    n = keys.shape[0] // t
    if per_tile:
        return [_dot_nt(keys[c * t:(c + 1) * t], q_head) for c in range(n)]
    s = _dot_nt(keys, q_head)
    return [s[c * t:(c + 1) * t] for c in range(n)]


def _group_keys(k_ref, g, n_tiles, key_group):
    start = pl.multiple_of(g * key_group * ATT_TILE, key_group * ATT_TILE)
    return k_ref[0, pl.ds(start, n_tiles * ATT_TILE), :]


def _sweep(qi, group, key_group, n_far_groups=None):
    g_own = qi // key_group
    own = [functools.partial(group, g_own, None, n + 1, True) for n in range(key_group)]
    state = lax.switch(qi % key_group, own)
    n_past = g_own if n_far_groups is None else jnp.minimum(g_own, n_far_groups)
    return lax.fori_loop(0, n_past, lambda n, st: group(g_own - 1 - n, st, key_group, False), state)


def _tile_iotas():
    t = ATT_TILE
    return lax.broadcasted_iota(jnp.int32, (t, t), 0), lax.broadcasted_iota(jnp.int32, (t, t), 1)


def _store_step_output(o_ref, outs_t):
    o_ref[0] = jnp.concatenate(outs_t, axis=0).T.astype(BF16)


def _softmax_group(s_tiles, ons, vt_h, acc_ref, h, stats, first):
    maxes = []
    for s, on in zip(s_tiles, ons):
        mx = jnp.max(s, axis=0, keepdims=True)
        maxes.append(mx if on is None else jnp.where(on, mx, -jnp.inf))
    m_new = functools.reduce(jnp.maximum, maxes)
    if not first:
        m_old, l_old = stats
        m_new = jnp.maximum(m_new, m_old)
    ps = []
    l_add = None
    for s, on in zip(s_tiles, ons):
        p = jnp.exp2(s - (m_new if on is None else jnp.where(on, m_new, jnp.inf)))
        p_sum = jnp.sum(p, axis=0, keepdims=True)
        l_add = p_sum if l_add is None else l_add + p_sum
        ps.append(p.astype(BF16))
    pv = _dot(vt_h, jnp.concatenate(ps, axis=0))
    if first:
        acc_ref[h] = pv
        return m_new, l_add
    alpha = jnp.exp2(m_old - m_new)
    acc_ref[h] = alpha * acc_ref[h] + pv
    return m_new, alpha * l_old + l_add


def _moba_select(qh, km_ref, sel_ref, qi):
    t = ATT_TILE
    km_hi, km_lo = _split_bf16(km_ref[0])
    nb = km_hi.shape[0]
    blk = lax.broadcasted_iota(jnp.int32, (nb, t), 0)
    past = blk < qi
    for h in range(HEADS_PER_STEP):
        gate = _dot_nt(_pair_lanes(km_hi, h), qh[h]) + _dot_nt(_pair_lanes(km_lo, h), qh[h])
        gate = jnp.where(past, gate, -jnp.inf)
        beaten = jnp.zeros((nb, t), F32)
        for other in range(nb):
            row = gate[other:other + 1, :]
            wins = jnp.where(row > gate, 1.0, jnp.where(row == gate, jnp.where(blk > other, 1.0, 0.0), 0.0))
            beaten = beaten + wins
        sel_ref[h] = jnp.where(past, jnp.where(beaten < MOBA_TOPK, 1.0, 0.0), 0.0)


def _moba_attn_kernel(q_ref, k_ref, vt_ref, bias_ref, km_ref, o_ref, acc_ref, sel_ref):
    qi = pl.program_id(2)
    qh = _split_heads(q_ref[0])
    n_bias = bias_ref.shape[1]
    _moba_select(qh, km_ref, sel_ref, qi)

    def group(g, stats, n_tiles, first):
        kg = _group_keys(k_ref, g, n_tiles, KEY_GROUP_EVEN)
        new = ()
        scores = [_tile_scores(_pair_lanes(kg, h), qh[h], per_tile=False) for h in range(HEADS_PER_STEP)]
        for h in range(HEADS_PER_STEP):
            s_tiles = scores[h]
            ons = []
            for c in range(n_tiles):
                j = g * KEY_GROUP_EVEN + c
                dt = n_tiles - 1 - c if first else qi - j
                s_tiles[c] = s_tiles[c] + bias_ref[h, dt if first else jnp.minimum(dt, n_bias - 1)]
                ons.append(None if first and dt == 0 else sel_ref[h, pl.ds(j, 1), :] > 0.5)
            new += _softmax_group(s_tiles, ons, _head_rows(vt_ref, g, h, n_tiles), acc_ref, h,
                                  None if first else stats[2 * h:2 * h + 2], first)
        return new

    stats = _sweep(qi, group, KEY_GROUP_EVEN)
    _store_step_output(o_ref, [acc_ref[h] / stats[2 * h + 1] for h in range(HEADS_PER_STEP)])


def _moba_attn(proj, vt, bias, km, batch, seq):
    t = ATT_TILE
    n_steps = N_HEADS_A // HEADS_PER_STEP
    n_bias = bias.shape[1]
    return pl.pallas_call(
        _moba_attn_kernel,
        grid=(batch, n_steps, seq // t),
        in_specs=[pl.BlockSpec((1, t, STEP_LANES), lambda b, p, i: (b, i, p)),
                  pl.BlockSpec((1, seq, STEP_LANES), lambda b, p, i: (b, 0, n_steps + p)),
                  pl.BlockSpec((1, PAIRS_PER_STEP) + vt.shape[2:], lambda b, p, i: (b, p, 0, 0, 0)),
                  pl.BlockSpec((HEADS_PER_STEP, n_bias, t, t), lambda b, p, i: (p, 0, 0, 0)),
                  pl.BlockSpec((1, seq // MOBA_BLOCK, STEP_LANES), lambda b, p, i: (b, 0, p))],
        out_specs=pl.BlockSpec((1, t, STEP_LANES), lambda b, p, i: (b, i, p)),
        out_shape=jax.ShapeDtypeStruct((batch, seq, N_HEADS_A * HEAD_DIM), BF16),
        scratch_shapes=[pltpu.VMEM((HEADS_PER_STEP, HEAD_DIM, t), F32),
                        pltpu.VMEM((HEADS_PER_STEP, seq // MOBA_BLOCK, t), F32)],
        compiler_params=_params(3),
        name="moba_attn",
    )(proj, proj, vt, bias, km)


def _dilated_far_kernel(q_ref, k_ref, vt_ref, bias_ref, acc_ref, m_ref, l_ref):
    n_cls = q_ref.shape[1]
    qh = [_split_heads(q_ref[0, c]) for c in range(n_cls)]
    scores = [[_dot_nt(_pair_lanes(k_ref[0, c], h), qh[c][h]) for h in range(HEADS_PER_STEP)] for c in range(n_cls)]
    for c in range(n_cls):
        for h in range(HEADS_PER_STEP):
            s = scores[c][h] + bias_ref[h, 0]
            m = jnp.max(s, axis=0, keepdims=True)
            p = jnp.exp2(s - jnp.where(m > -jnp.inf, m, 0.0))
            m_ref[0, c, 0, h:h + 1, :] = m
            l_ref[0, c, 0, h:h + 1, :] = jnp.sum(p, axis=0, keepdims=True)
            vt_h = vt_ref[0, c, h // 2, (h % 2) * HEAD_DIM:(h % 2 + 1) * HEAD_DIM, :]
            acc_ref[0, c, h] = _dot(vt_h, p.astype(BF16))


def _dilated_far(q_cls, k_cls, vt_cls, bias, batch):
    t = ATT_TILE
    n_steps = N_HEADS_B // HEADS_PER_STEP
    nc = DIL_FAR_CLASSES_PER_STEP
    stat_shape = jax.ShapeDtypeStruct((batch, DIL_CLASSES, n_steps, HEADS_PER_STEP, t), F32)
    stat_spec = pl.BlockSpec((1, nc, 1, HEADS_PER_STEP, t), lambda b, p, c: (b, c, p, 0, 0))
    return pl.pallas_call(
        _dilated_far_kernel,
        grid=(batch, n_steps, DIL_CLASSES // nc),
        in_specs=[pl.BlockSpec((1, nc, t, STEP_LANES), lambda b, p, c: (b, c, 0, p)),
                  pl.BlockSpec((1, nc, t, STEP_LANES), lambda b, p, c: (b, c, 0, p)),
                  pl.BlockSpec((1, nc, PAIRS_PER_STEP, LANES, t), lambda b, p, c: (b, c, p, 0, 0)),
                  pl.BlockSpec((HEADS_PER_STEP, 1, t, t), lambda b, p, c: (p, 0, 0, 0))],
        out_specs=[pl.BlockSpec((1, nc, HEADS_PER_STEP, HEAD_DIM, t), lambda b, p, c: (b, c, p, 0, 0)),
                   stat_spec, stat_spec],
        out_shape=[jax.ShapeDtypeStruct((batch, DIL_CLASSES, N_HEADS_B, HEAD_DIM, t), F32), stat_shape, stat_shape],
        compiler_params=_params(3),
        name="dilated_far",
    )(q_cls, k_cls, vt_cls, bias)


def _dilated_near_kernel(q_ref, k_ref, vt_ref, bias_ref, far_acc_ref, far_m_ref, far_l_ref, o_ref, acc_ref):
    t = ATT_TILE
    qi = pl.program_id(2)
    qh = _split_heads(q_ref[0])

    def tiles(n_tiles):
        first_tile = qi - (n_tiles - 1)
        keys = k_ref[0, pl.ds(pl.multiple_of(first_tile * t, t), n_tiles * t), :]
        scores = [_tile_scores(_pair_lanes(keys, h), qh[h], per_tile=False) for h in range(HEADS_PER_STEP)]
        stats = ()
        for h in range(HEADS_PER_STEP):
            s_tiles = [s + bias_ref[h, n_tiles - 1 - c] for c, s in enumerate(scores[h])]
            rows = slice((h % 2) * HEAD_DIM, (h % 2 + 1) * HEAD_DIM)
            vt_h = jnp.concatenate([vt_ref[0, h // 2, first_tile + c, rows, :] for c in range(n_tiles)], axis=1)
            stats += _softmax_group(s_tiles, [None] * n_tiles, vt_h, acc_ref, h, None, True)
        return stats

    stats = lax.switch(jnp.minimum(qi, DIL_NEAR_TILES - 1),
                       [functools.partial(tiles, n + 1) for n in range(DIL_NEAR_TILES)])
    outs = []
    for h in range(HEADS_PER_STEP):
        m_near, l_near = stats[2 * h], stats[2 * h + 1]
        m_far, l_far = far_m_ref[0, 0, h:h + 1, :], far_l_ref[0, 0, h:h + 1, :]
        m = jnp.maximum(m_near, m_far)
        w_near, w_far = jnp.exp2(m_near - m), jnp.exp2(m_far - m)
        outs.append((w_near * acc_ref[h] + w_far * far_acc_ref[0, h]) / (w_near * l_near + w_far * l_far))
    _store_step_output(o_ref, outs)


def _dilated_near(proj, vt, bias, far_acc, far_m, far_l, batch, seq, col0):
    t = ATT_TILE
    n_steps = N_HEADS_B // HEADS_PER_STEP
    stat_spec = pl.BlockSpec((1, 1, HEADS_PER_STEP, t), lambda b, p, i: (b, p, 0, i))
    return pl.pallas_call(
        _dilated_near_kernel,
        grid=(batch, n_steps, seq // t),
        in_specs=[pl.BlockSpec((1, t, STEP_LANES), lambda b, p, i: (b, i, col0 + p)),
                  pl.BlockSpec((1, seq, STEP_LANES), lambda b, p, i: (b, 0, col0 + n_steps + p)),
                  pl.BlockSpec((1, PAIRS_PER_STEP) + vt.shape[2:], lambda b, p, i: (b, p, 0, 0, 0)),
                  pl.BlockSpec((HEADS_PER_STEP, DIL_NEAR_TILES, t, t), lambda b, p, i: (p, 0, 0, 0)),
                  pl.BlockSpec((1, HEADS_PER_STEP, HEAD_DIM, t), lambda b, p, i: (b, p, 0, i)),
                  stat_spec, stat_spec],
        out_specs=pl.BlockSpec((1, t, STEP_LANES), lambda b, p, i: (b, i, p)),
        out_shape=jax.ShapeDtypeStruct((batch, seq, N_HEADS_B * HEAD_DIM), BF16),
        scratch_shapes=[pltpu.VMEM((HEADS_PER_STEP, HEAD_DIM, t), F32)],
        compiler_params=_params(3),
        name="dilated_near",
    )(proj, proj, vt, bias, far_acc, far_m, far_l)


def _dilated_attn(proj, vt_tiles, bias_near, bias_far, batch, seq, col0):
    t = ATT_TILE
    wb = N_HEADS_B * HEAD_DIM
    n_pairs = N_HEADS_B // 2
    lane0 = col0 * STEP_LANES

    def by_class(x):
        return x.reshape(batch, seq // DIL_CLASSES, DIL_CLASSES, x.shape[-1]).transpose(0, 2, 1, 3)

    q_cls = by_class(proj[:, :, lane0:lane0 + wb])
    k_cls = by_class(proj[:, :, lane0 + wb:lane0 + 2 * wb])
    v_cls = by_class(proj[:, :, lane0 + 2 * wb:lane0 + 3 * wb])
    vt_cls = v_cls.reshape(batch, DIL_CLASSES, t, n_pairs, LANES).transpose(0, 1, 3, 4, 2)
    acc, m, l = _dilated_far(q_cls, k_cls, vt_cls, bias_far, batch)
    acc = acc.transpose(0, 2, 3, 4, 1).reshape(batch, N_HEADS_B, HEAD_DIM, seq)
    m = m.transpose(0, 2, 3, 4, 1).reshape(batch, -1, HEADS_PER_STEP, seq)
    l = l.transpose(0, 2, 3, 4, 1).reshape(batch, -1, HEADS_PER_STEP, seq)
    return _dilated_near(proj, vt_tiles, bias_near, acc, m, l, batch, seq, col0)


def _odd_proj_kernel(x_ref, mod_ref, g_ref, win_ref, gq_ref, gkv_ref, wq_ref, wkv_ref,
                     cq_ref, sq_ref, ck_ref, sk_ref, qm_ref, km_ref, vtm_ref, sb_ref, vts_ref):
    d = D_MODEL
    mod = mod_ref[0]
    h = _prenorm(x_ref[...], g_ref[...], mod[:, d:2 * d], mod[:, 0:d])
    p = _dot(h.astype(BF16), win_ref[...])
    o = MLA_Q_RANK + MLA_KV_RANK
    c_q, c_kv = p[:, 0:MLA_Q_RANK], p[:, MLA_Q_RANK:o]
    k_rope, k_rope_swapped = p[:, o:o + LANES], p[:, o + LANES:o + 2 * LANES]
    sb0 = o + 2 * LANES
    n_qk = 2 * N_HEADS_D * HEAD_DIM
    sb_ref[...] = p[:, sb0:sb0 + n_qk].astype(BF16)
    _store_vt(vts_ref, p[:, sb0 + n_qk:])
    q12 = _dot(_rms(c_q, gq_ref[...]).astype(BF16), wq_ref[...])
    kv = _dot(_rms(c_kv, gkv_ref[...]).astype(BF16), wkv_ref[...])
    k_pe = k_rope * ck_ref[...] + k_rope_swapped * sk_ref[...]
    cq, sq = cq_ref[...], sq_ref[...]
    half = N_HEADS_C * LANES
    for hh in range(N_HEADS_C):
        cols = slice(hh * LANES, (hh + 1) * LANES)
        swapped = slice(half + hh * LANES, half + (hh + 1) * LANES)
        qm_ref[:, cols] = (q12[:, cols] * cq + q12[:, swapped] * sq).astype(BF16)
        km_ref[:, cols] = (kv[:, cols] + k_pe).astype(BF16)
    _store_vt(vtm_ref, kv[:, half:])


def _odd_proj(x, mod, g, w_in, gq, gkv, wq, wkv, tables, seq):
    rows, d = x.shape
    tm = ROW_TILE
    per_seq = seq // tm
    n_in = w_in.shape[1]
    n_sb = 2 * N_HEADS_D * HEAD_DIM
    n_q = N_HEADS_C * LANES
    n_pairs = N_HEADS_C // 2
    const = lambda i: (0, 0)
    table_spec = pl.BlockSpec((tm, LANES), lambda i: (i % per_seq, 0))
    vt_spec = _vt_spec(tm, per_seq, n_pairs, KEY_GROUP_ODD)
    vt_shape = _vt_shape(rows // seq, seq, n_pairs, KEY_GROUP_ODD)
    return pl.pallas_call(
        _odd_proj_kernel,
        grid=(rows // tm,),
        in_specs=[pl.BlockSpec((tm, d), lambda i: (i, 0)),
                  pl.BlockSpec((1, 1, 6 * d), lambda i: (i // per_seq, 0, 0)),
                  pl.BlockSpec((1, d), const),
                  pl.BlockSpec((d, n_in), const),
                  pl.BlockSpec((1, MLA_Q_RANK), const),
                  pl.BlockSpec((1, MLA_KV_RANK), const),
                  pl.BlockSpec(wq.shape, const),
                  pl.BlockSpec(wkv.shape, const),
                  table_spec, table_spec, table_spec, table_spec],
        out_specs=[pl.BlockSpec((tm, n_q), lambda i: (i, 0)),
                   pl.BlockSpec((tm, n_q), lambda i: (i, 0)),
                   vt_spec,
                   pl.BlockSpec((tm, n_sb), lambda i: (i, 0)),
                   vt_spec],
        out_shape=[jax.ShapeDtypeStruct((rows, n_q), BF16),
                   jax.ShapeDtypeStruct((rows, n_q), BF16),
                   vt_shape,
                   jax.ShapeDtypeStruct((rows, n_sb), BF16),
                   vt_shape],
        compiler_params=_params(1),
        name="odd_proj",
    )(x, mod, g, w_in, gq, gkv, wq, wkv, *tables)


def _mla_attn_kernel(q_ref, k_ref, vt_ref, o_ref, acc_ref):
    qi = pl.program_id(2)
    q_step = q_ref[0]
    qh = [q_step[:, h * LANES:(h + 1) * LANES] for h in range(HEADS_PER_STEP)]

    def group(g, stats, n_tiles, first):
        kg = _group_keys(k_ref, g, n_tiles, KEY_GROUP_ODD)
        new = ()
        scores = [_tile_scores(kg[:, h * LANES:(h + 1) * LANES], qh[h], per_tile=True)
                  for h in range(HEADS_PER_STEP)]
        for h in range(HEADS_PER_STEP):
            s_tiles = scores[h]
            if first:
                key, query = _tile_iotas()
                s_tiles[-1] = jnp.where(key <= query, s_tiles[-1], -jnp.inf)
            new += _softmax_group(s_tiles, [None] * n_tiles, _head_rows(vt_ref, g, h, n_tiles), acc_ref, h,
                                  None if first else stats[2 * h:2 * h + 2], first)
        return new

    stats = _sweep(qi, group, KEY_GROUP_ODD)
    _store_step_output(o_ref, [acc_ref[h] / stats[2 * h + 1] for h in range(HEADS_PER_STEP)])


def _mla_attn(qm, km, vt, batch, seq):
    t = ATT_TILE
    n_steps = N_HEADS_C // HEADS_PER_STEP
    qk_lanes = HEADS_PER_STEP * LANES
    return pl.pallas_call(
        _mla_attn_kernel,
        grid=(batch, n_steps, seq // t),
        in_specs=[pl.BlockSpec((1, t, qk_lanes), lambda b, p, i: (b, i, p)),
                  _resident((1, seq, qk_lanes), lambda b, p, i: (b, 0, p)),
                  _resident((1, PAIRS_PER_STEP) + vt.shape[2:], lambda b, p, i: (b, p, 0, 0, 0))],
        out_specs=pl.BlockSpec((1, t, STEP_LANES), lambda b, p, i: (b, i, p)),
        out_shape=jax.ShapeDtypeStruct((batch, seq, N_HEADS_C * MLA_V_DIM), BF16),
        scratch_shapes=[pltpu.VMEM((HEADS_PER_STEP, HEAD_DIM, t), F32)],
        compiler_params=_params(3),
        name="mla_attn",
    )(qm, km, vt)


def _stick_attn_kernel(q_ref, k_ref, vt_ref, o_ref, acc_ref):
    t = ATT_TILE
    qi = pl.program_id(2)
    qh = _split_heads(q_ref[0])
    key = lax.broadcasted_iota(jnp.int32, (t, t), 0)
    other = lax.broadcasted_iota(jnp.int32, (t, t), 1)
    from_here = jnp.where(other >= key, 1.0, 0.0).astype(BF16)
    from_here2 = jnp.concatenate([from_here, from_here], axis=1)

    strict = key < other

    def group(g, carries, n_tiles, first):
        kg = _group_keys(k_ref, g, n_tiles, KEY_GROUP_ODD)
        new = ()
        scores = [_tile_scores(_pair_lanes(kg, h), qh[h], per_tile=False) for h in range(HEADS_PER_STEP)]
        for h in range(HEADS_PER_STEP):
            z_tiles = scores[h]
            keep_tiles = []
            for z in z_tiles:
                neg_z = -z
                keep_tiles.append(jnp.minimum(neg_z, 0.0) - jnp.log(1.0 + jnp.exp(jnp.minimum(z, neg_z))))
            if first:
                keep_tiles[-1] = jnp.where(strict, keep_tiles[-1], 0.0)
            run = None if first else carries[h]
            a_tiles = [None] * n_tiles
            for c in reversed(range(n_tiles)):
                tail = _dot(from_here2, jnp.concatenate(_split_bf16(keep_tiles[c]), axis=0))
                tile_total = tail[0:1, :]
                if run is not None:
                    tail = tail + run
                a_tiles[c] = jnp.exp(z_tiles[c] + tail)
                run = tile_total if run is None else run + tile_total
            if first:
                a_tiles[-1] = jnp.where(strict, a_tiles[-1], 0.0)
            pv = _dot(_head_rows(vt_ref, g, h, n_tiles), jnp.concatenate(a_tiles, axis=0).astype(BF16))
            acc_ref[h] = pv if first else acc_ref[h] + pv
            new += (run,)
        return new

    _sweep(qi, group, KEY_GROUP_ODD)
    _store_step_output(o_ref, [acc_ref[h] for h in range(HEADS_PER_STEP)])


def _stick_attn(sb, vt, batch, seq):
    t = ATT_TILE
    n_steps = N_HEADS_D // HEADS_PER_STEP
    return pl.pallas_call(
        _stick_attn_kernel,
        grid=(batch, n_steps, seq // t),
        in_specs=[pl.BlockSpec((1, t, STEP_LANES), lambda b, p, i: (b, i, p)),
                  _resident((1, seq, STEP_LANES), lambda b, p, i: (b, 0, n_steps + p)),
                  _resident((1, PAIRS_PER_STEP) + vt.shape[2:], lambda b, p, i: (b, p, 0, 0, 0))],
        out_specs=pl.BlockSpec((1, t, STEP_LANES), lambda b, p, i: (b, i, p)),
        out_shape=jax.ShapeDtypeStruct((batch, seq, N_HEADS_D * HEAD_DIM), BF16),
        scratch_shapes=[pltpu.VMEM((HEADS_PER_STEP, HEAD_DIM, t), F32)],
        compiler_params=_params(3),
        name="stick_attn",
    )(sb, sb, vt)


def _post_attn_kernel(oa_ref, ob_ref, oah_ref, obh_ref, x_ref, xh_ref, mod_ref, gmix_ref, gpre_ref, gpost_ref,
                      wo_ref, wup_ref, cw_ref, cb_ref, wd_ref, o_ref, x1_ref, h_ref, u_ref, acc_ref, *, per_seq):
    d = D_MODEL
    tm = ROW_TILE
    halo = CONV_HALO
    cw = FF_CHUNK
    i = pl.program_id(0)
    mod = mod_ref[0]
    gate_m, shift, scale, gate_f = mod[:, 2 * d:3 * d], mod[:, 3 * d:4 * d], mod[:, 4 * d:5 * d], mod[:, 5 * d:6 * d]
    n_first = oa_ref.shape[1]

    def mixed(oa, ob, x):
        y = _dot(oa, wo_ref[0:n_first, :]) + _dot(ob, wo_ref[n_first:2 * n_first, :])
        return x + gate_m * _rms(y, gmix_ref[...])

    g = gpre_ref[...]
    ahead = _prenorm(mixed(oah_ref[...], obh_ref[...], xh_ref[...]), g, scale, shift)
    ahead = jnp.where(i % per_seq == 0, 0.0, ahead)
    h_ref[0:halo, :] = ahead.astype(BF16)
    x1_ref[...] = mixed(oa_ref[...], ob_ref[...], x_ref[...])
    h_ref[halo:halo + tm, :] = _prenorm(x1_ref[...], g, scale, shift).astype(BF16)

    def up(ch):
        slot = ch % 2
        for half in range(2):
            cols = slice(half * D_FF + ch * cw, half * D_FF + (ch + 1) * cw)
            u_ref[slot, half] = _dot(h_ref[...], wup_ref[:, cols])

    def conv(slot, half, ch):
        cols = slice(half * D_FF + ch * cw, half * D_FF + (ch + 1) * cw)
        w = cw_ref[:, cols]
        out = w[0:1, :] * u_ref[slot, half, halo - 2:halo - 2 + tm, :]
        out = out + w[1:2, :] * u_ref[slot, half, halo - 1:halo - 1 + tm, :]
        out = out + w[2:3, :] * u_ref[slot, half, halo:halo + tm, :]
        return out + cb_ref[:, cols]

    def down(ch, act):
        part = _dot(act, wd_ref[ch * cw:(ch + 1) * cw, :])
        if ch == 0:
            acc_ref[...] = part
        else:
            acc_ref[...] += part

    up(0)
    act = None
    for ch in range(N_FF_CHUNKS):
        if ch + 1 < N_FF_CHUNKS:
            up(ch + 1)
        if act is not None:
            down(ch - 1, act)
        slot = ch % 2
        act = (jax.nn.gelu(conv(slot, 0, ch), approximate=True) * conv(slot, 1, ch)).astype(BF16)
    down(N_FF_CHUNKS - 1, act)

    o_ref[...] = x1_ref[...] + gate_f * _rms(acc_ref[...], gpost_ref[...])


def _post_attn(oa, ob, x, mod, g_mix, g_pre, g_post, w_out, w_up, conv_w, conv_b, w_down, seq):
    rows, d = x.shape
    tm = ROW_TILE
    halo = CONV_HALO
    per_seq = seq // tm
    n_first = oa.shape[1]
    const = lambda i: (0, 0)
    tile = lambda i: (i, 0)
    ahead = lambda i: (jnp.maximum(i * (tm // halo) - 1, 0), 0)
    once = dict(pipeline_mode=pl.Buffered(1))
    return pl.pallas_call(
        functools.partial(_post_attn_kernel, per_seq=per_seq),
        grid=(rows // tm,),
        in_specs=[pl.BlockSpec((tm, n_first), tile), pl.BlockSpec((tm, n_first), tile),
                  pl.BlockSpec((halo, n_first), ahead), pl.BlockSpec((halo, n_first), ahead),
                  pl.BlockSpec((tm, d), tile), pl.BlockSpec((halo, d), ahead),
                  pl.BlockSpec((1, 1, 6 * d), lambda i: (i // per_seq, 0, 0)),
                  pl.BlockSpec((1, d), const), pl.BlockSpec((1, d), const), pl.BlockSpec((1, d), const),
                  pl.BlockSpec(w_out.shape, const, **once),
                  pl.BlockSpec(w_up.shape, const, **once),
                  pl.BlockSpec(conv_w.shape, const),
                  pl.BlockSpec(conv_b.shape, const),
                  pl.BlockSpec(w_down.shape, const, **once)],
        out_specs=pl.BlockSpec((tm, d), tile),
        out_shape=jax.ShapeDtypeStruct((rows, d), F32),
        scratch_shapes=[pltpu.VMEM((tm, d), F32),
                        pltpu.VMEM((tm + halo, d), BF16),
                        pltpu.VMEM((2, 2, tm + halo, FF_CHUNK), F32),
                        pltpu.VMEM((tm, d), F32)],
        compiler_params=_params(1),
        name="post_attn",
    )(oa, ob, oa, ob, x, x, mod, g_mix, g_pre, g_post, w_out, w_up, conv_w, conv_b, w_down)


def _rotate_half_cols(w):
    half = w.shape[-1] // 2
    return jnp.concatenate([-w[..., half:], w[..., :half]], axis=-1)


def _pad_cols(w, left, total):
    return jnp.pad(w, ((0, 0), (left, total - left - w.shape[1])))


def _rope_tables(seq):
    inv_freq = 1.0 / (ROPE_THETA ** (jnp.arange(0, MLA_ROPE_DIM, 2, dtype=F32) / MLA_ROPE_DIM))
    ang = jnp.arange(seq, dtype=F32)[:, None] * inv_freq[None, :]
    cos, sin = jnp.cos(ang), jnp.sin(ang)
    cos2 = _pad_cols(jnp.concatenate([cos, cos], axis=1), MLA_NOPE_DIM, LANES)
    sin2 = _pad_cols(jnp.concatenate([sin, sin], axis=1), MLA_NOPE_DIM, LANES)
    scale = (MLA_NOPE_DIM + MLA_ROPE_DIM) ** -0.5
    nope_ones = _pad_cols(jnp.ones((seq, MLA_NOPE_DIM), F32), 0, LANES)
    scale = scale * LOG2_E
    return (scale * (cos2 + nope_ones), scale * sin2, cos2, sin2)


def _odd_weights(w_in, w_uq, w_ukv):
    o = MLA_Q_RANK + MLA_KV_RANK
    w_rope = w_in[:, o:o + MLA_ROPE_DIM]
    scale_d = HEAD_DIM ** -0.5
    wd = N_HEADS_D * HEAD_DIM
    sb0 = o + MLA_ROPE_DIM
    w_in2 = jnp.concatenate([
        w_in[:, :o],
        _pad_cols(w_rope, MLA_NOPE_DIM, LANES),
        _pad_cols(_rotate_half_cols(w_rope), MLA_NOPE_DIM, LANES),
        w_in[:, sb0:sb0 + wd] * scale_d,
        w_in[:, sb0 + wd:],
    ], axis=1).astype(BF16)
    qd = MLA_NOPE_DIM + MLA_ROPE_DIM
    uq = w_uq.reshape(MLA_Q_RANK, N_HEADS_C, qd)
    plain = jnp.pad(uq, ((0, 0), (0, 0), (0, LANES - qd)))
    swapped = jnp.pad(_rotate_half_cols(uq[..., MLA_NOPE_DIM:]),
                      ((0, 0), (0, 0), (MLA_NOPE_DIM, LANES - qd)))
    wq = jnp.concatenate([plain.reshape(MLA_Q_RANK, -1), swapped.reshape(MLA_Q_RANK, -1)], axis=1).astype(BF16)
    ukv = w_ukv.reshape(MLA_KV_RANK, N_HEADS_C, MLA_NOPE_DIM + MLA_V_DIM)
    k_nope = jnp.pad(ukv[..., :MLA_NOPE_DIM], ((0, 0), (0, 0), (0, LANES - MLA_NOPE_DIM)))
    v = ukv[..., MLA_NOPE_DIM:]
    wkv = jnp.concatenate([k_nope.reshape(MLA_KV_RANK, -1), v.reshape(MLA_KV_RANK, -1)], axis=1).astype(BF16)
    return w_in2, wq, wkv


def _even_weights(w_in):
    wa = N_HEADS_A * HEAD_DIM
    scale = HEAD_DIM ** -0.5 * LOG2_E
    col = jnp.arange(w_in.shape[1])
    is_q = (col < wa) | ((col >= 3 * wa) & (col < 4 * wa))
    return (w_in * jnp.where(is_q, scale, 1.0)).astype(BF16)


def kernel(x, c, rel_bias, ada_w, ada_b, mix_pre_g, mix_post_g, ffn_pre_g, ffn_post_g, ab_w_in, ab_w_out,
           cd_w_in, mla_q_norm_g, mla_kv_norm_g, mla_w_uq, mla_w_ukv, cd_w_out, ffn_w_up, ffn_conv_w,
           ffn_conv_b, ffn_w_down):
    batch, seq, d = x.shape
    widest_group = max(KEY_GROUP_EVEN, KEY_GROUP_ODD) * ATT_TILE
    assert d == D_MODEL and seq % widest_group == 0 and (KEY_GROUP_EVEN * ATT_TILE) % ROW_TILE == 0
    assert seq == DIL_CLASSES * ATT_TILE
    rows = batch * seq
    xf = x.reshape(rows, d)

    mods = _mods(c, ada_w, ada_b)
    bias_a = _bias_tiles(rel_bias, 0, N_HEADS_A, MOBA_BIAS_TILES, 1, None, "moba_bias_tiles")
    bias_near = _bias_tiles(rel_bias, N_HEADS_A, N_HEADS_B, DIL_NEAR_TILES, 1, DIL_NEAR_BRANCHES,
                            "dilated_near_bias_tiles")
    bias_far = _bias_tiles(rel_bias, N_HEADS_A, N_HEADS_B, 1, DIL_CLASSES, DIL_FAR_BRANCHES,
                           "dilated_far_bias_tiles")
    tables = _rope_tables(seq)
    step_blocks = N_HEADS_A // HEADS_PER_STEP

    for layer in range(DEPTH):
        mod = mods[layer].reshape(batch, 1, 6 * d)
        i = layer // 2
        if layer % 2 == 0:
            proj, km, vta, vtb = _even_proj(xf, mod, mix_pre_g[layer].reshape(1, d),
                                            _even_weights(ab_w_in[i]), seq)
            proj = proj.reshape(batch, seq, -1)
            km = km.reshape(batch, seq // MOBA_BLOCK, -1)
            o_first = _moba_attn(proj, vta, bias_a, km, batch, seq)
            o_second = _dilated_attn(proj, vtb, bias_near, bias_far, batch, seq, 3 * step_blocks)
            w_out = ab_w_out[i]
        else:
            w_in2, wq, wkv = _odd_weights(cd_w_in[i], mla_w_uq[i], mla_w_ukv[i])
            qm, km, vtm, sb, vts = _odd_proj(xf, mod, mix_pre_g[layer].reshape(1, d), w_in2,
                                             mla_q_norm_g[i].reshape(1, -1), mla_kv_norm_g[i].reshape(1, -1),
                                             wq, wkv, tables, seq)
            o_first = _mla_attn(qm.reshape(batch, seq, -1), km.reshape(batch, seq, -1), vtm, batch, seq)
            o_second = _stick_attn(sb.reshape(batch, seq, -1), vts, batch, seq)
            w_out = cd_w_out[i]
        xf = _post_attn(o_first.reshape(rows, -1), o_second.reshape(rows, -1), xf, mod,
                        mix_post_g[layer].reshape(1, d), ffn_pre_g[layer].reshape(1, d),
                        ffn_post_g[layer].reshape(1, d), w_out.astype(BF16), ffn_w_up[layer].astype(BF16),
                        ffn_conv_w[layer], ffn_conv_b[layer].reshape(1, -1), ffn_w_down[layer].astype(BF16), seq)
    return xf.reshape(batch, seq, d)
```

```python
import functools
import math

import jax
import jax.numpy as jnp
from jax import lax
from jax.experimental import pallas as pl
from jax.experimental.pallas import tpu as pltpu

F32 = jnp.float32
BF16 = jnp.bfloat16

D_MODEL = 1024
DEPTH = 4
HEAD_DIM = 64
N_HEADS_A = 8
N_HEADS_B = 8
N_HEADS_C = 8
N_HEADS_D = 8
MOBA_BLOCK = 256
MOBA_TOPK = 3
DILATED_BRANCHES = ((128, 1), (512, 4), (2048, 16))
MLA_Q_RANK = 256
MLA_KV_RANK = 256
MLA_NOPE_DIM = 64
MLA_ROPE_DIM = 32
MLA_V_DIM = 64
ROPE_THETA = 10000.0
REL_BUCKETS = 32
REL_MAX_DIST = 2048
D_FF = 2816
CONV_WIDTH = 3
NORM_EPS = 1e-6
LOG2_E = math.log2(math.e)

LANES = 128
SUBLANES = 8
BF16_SUBLANES = 16
VMEM_LIMIT_BYTES = 56 * 1024 * 1024

ATT_TILE = MOBA_BLOCK
KEY_GROUP_EVEN = 4
KEY_GROUP_ODD = 8
PAIRS_PER_STEP = 2
HEADS_PER_STEP = 2 * PAIRS_PER_STEP
STEP_LANES = PAIRS_PER_STEP * LANES
ROW_TILE = 512
FF_CHUNK = 256
N_FF_CHUNKS = D_FF // FF_CHUNK
CONV_HALO = BF16_SUBLANES
MOBA_BIAS_TILES = REL_MAX_DIST // ATT_TILE + 2
DIL_SPLIT = DILATED_BRANCHES[1][0]
DIL_CLASSES = DILATED_BRANCHES[2][1]
DIL_NEAR_TILES = DIL_SPLIT // ATT_TILE + 1
DIL_FAR_LANES = 5 * LANES
assert DIL_FAR_LANES >= N_HEADS_B * (HEAD_DIM + 2)
DIL_NEAR_BRANCHES = tuple((-1, min(window, DIL_SPLIT), dil) for window, dil in DILATED_BRANCHES)
DIL_FAR_BRANCHES = ((DIL_SPLIT, DILATED_BRANCHES[2][0], DIL_CLASSES),)
assert all(window <= DIL_SPLIT for window, _ in DILATED_BRANCHES[:2]) and DIL_SPLIT % DIL_CLASSES == 0

_NT = (((1,), (1,)), ((), ()))


def _bucket_lower_bounds():
    max_exact = REL_BUCKETS // 2
    ratio = REL_MAX_DIST // max_exact
    n_log = REL_BUCKETS - max_exact
    lows = list(range(max_exact + 1))
    for k in range(1, n_log):
        d = lows[-1]
        while d ** n_log < (max_exact ** n_log) * (ratio ** k):
            d += 1
        lows.append(d)
    return lows


_BUCKET_LOW = _bucket_lower_bounds()


def _dot(a, b):
    return jnp.dot(a, b, preferred_element_type=F32)


def _dot_nt(a, b):
    return lax.dot_general(a, b, _NT, preferred_element_type=F32)


def _split_bf16(x):
    hi = x.astype(BF16)
    lo = (x - hi.astype(F32)).astype(BF16)
    return hi, lo


def _rms(x, g):
    return (x * lax.rsqrt(jnp.mean(x * x, axis=-1, keepdims=True) + NORM_EPS)) * g


def _prenorm(x, g, scale, shift):
    return _rms(x, g) * (1.0 + scale) + shift


def _resident(block_shape, index_map):
    return pl.BlockSpec(block_shape, index_map)


def _params(n_grid_dims):
    return pltpu.CompilerParams(dimension_semantics=("arbitrary",) * n_grid_dims,
                                vmem_limit_bytes=VMEM_LIMIT_BYTES)


def _mods_kernel(c_ref, w_ref, b_ref, o_ref):
    c = c_ref[...]
    cond = c * jax.nn.sigmoid(c)
    c_hi, c_lo = _split_bf16(cond)
    w_hi, w_lo = _split_bf16(w_ref[0])
    o_ref[0] = _dot(c_hi, w_hi) + _dot(c_hi, w_lo) + _dot(c_lo, w_hi) + b_ref[0]


def _mods(c, ada_w, ada_b):
    b, d = c.shape
    rows = BF16_SUBLANES
    n_out = ada_w.shape[-1]
    tn = n_out // 4
    c_pad = jnp.zeros((rows, d), F32).at[:b].set(c)
    out = pl.pallas_call(
        _mods_kernel,
        grid=(DEPTH, n_out // tn),
        in_specs=[pl.BlockSpec((rows, d), lambda l, j: (0, 0)),
                  pl.BlockSpec((1, d, tn), lambda l, j: (l, 0, j)),
                  pl.BlockSpec((1, 1, tn), lambda l, j: (l, 0, j))],
        out_specs=pl.BlockSpec((1, rows, tn), lambda l, j: (l, 0, j)),
        out_shape=jax.ShapeDtypeStruct((DEPTH, rows, n_out), F32),
        compiler_params=_params(2),
        name="ada_mods",
    )(c_pad, ada_w, ada_b.reshape(DEPTH, 1, n_out))
    return out[:, :b]


def _bias_tiles_kernel(tab_ref, o_ref, *, head_off, step, branches):
    h = pl.program_id(0) + head_off
    d = pl.program_id(1)
    t = ATT_TILE
    dist = (d * t + lax.broadcasted_iota(jnp.int32, (SUBLANES, 2 * t), 1) - t) * step
    val = jnp.full(dist.shape, tab_ref[h, 0], F32)
    for b in range(1, REL_BUCKETS):
        val = jnp.where(dist >= _BUCKET_LOW[b], tab_ref[h, b], val)
    if branches is not None:
        mult = jnp.zeros(dist.shape, F32)
        for beyond, window, dil in branches:
            hit = jnp.where(dist <= window, jnp.where((dist & (dil - 1)) == 0, 1.0, 0.0), 0.0)
            mult = mult + jnp.where(dist > beyond, hit, 0.0)
        log_mult = jnp.where(mult > 2.5, math.log(3.0), jnp.where(mult > 1.5, math.log(2.0), 0.0))
        val = jnp.where(mult > 0.5, val + log_mult, -jnp.inf)
    val = jnp.where(dist >= 0, val * LOG2_E, -jnp.inf)
    strip = jnp.concatenate([val] * (t // SUBLANES), axis=0)
    rotated = pltpu.roll(strip, 0, 1, stride=1, stride_axis=0)
    o_ref[0, 0] = rotated[:, t:2 * t]


def _bias_tiles(rel_bias, head_off, n_heads, n_tiles, step, branches, name):
    t = ATT_TILE
    return pl.pallas_call(
        functools.partial(_bias_tiles_kernel, head_off=head_off, step=step, branches=branches),
        grid=(n_heads, n_tiles),
        in_specs=[pl.BlockSpec(memory_space=pltpu.SMEM)],
        out_specs=pl.BlockSpec((1, 1, t, t), lambda h, d: (h, d, 0, 0)),
        out_shape=jax.ShapeDtypeStruct((n_heads, n_tiles, t, t), F32),
        compiler_params=_params(2),
        name=name,
    )(rel_bias)


def _store_vt(vt_ref, v):
    vt = v.T.astype(BF16)
    for p in range(vt.shape[0] // LANES):
        vt_ref[0, p, 0] = vt[p * LANES:(p + 1) * LANES, :]


def _vt_spec(tm, per_seq, n_pairs, key_group):
    per_group = key_group * ATT_TILE // tm
    return pl.BlockSpec((1, n_pairs, 1, LANES, tm),
                        lambda i: (i // per_seq, 0, (i % per_seq) // per_group, 0, (i % per_seq) % per_group))


def _vt_shape(batch, seq, n_pairs, key_group):
    group_keys = key_group * ATT_TILE
    return jax.ShapeDtypeStruct((batch, n_pairs, seq // group_keys, LANES, group_keys), BF16)


def _even_proj_kernel(x_ref, mod_ref, g_ref, w_ref, o_ref, km_ref, vta_ref, vtb_ref):
    d = D_MODEL
    mod = mod_ref[0]
    h = _prenorm(x_ref[...], g_ref[...], mod[:, d:2 * d], mod[:, 0:d])
    p = _dot(h.astype(BF16), w_ref[...])
    o_ref[...] = p.astype(BF16)
    wa = N_HEADS_A * HEAD_DIM
    ka = p[:, wa:2 * wa]
    nb = ka.shape[0] // MOBA_BLOCK
    km_ref[0] = jnp.mean(ka.reshape(nb, MOBA_BLOCK, wa), axis=1)
    _store_vt(vta_ref, p[:, 2 * wa:3 * wa])
    vtb = p[:, 5 * wa:6 * wa].T.astype(BF16)
    t = ATT_TILE
    for pair in range(vtb.shape[0] // LANES):
        for tile in range(vtb.shape[1] // t):
            vtb_ref[0, pair, tile] = vtb[pair * LANES:(pair + 1) * LANES, tile * t:(tile + 1) * t]


def _even_proj(x, mod, g, w_in, seq):
    rows, d = x.shape
    tm = ROW_TILE
    n = w_in.shape[1]
    wa = N_HEADS_A * HEAD_DIM
    per_seq = seq // tm
    n_pairs = N_HEADS_A // 2
    vt_spec = _vt_spec(tm, per_seq, n_pairs, KEY_GROUP_EVEN)
    vt_shape = _vt_shape(rows // seq, seq, n_pairs, KEY_GROUP_EVEN)
    tiles_per_step = tm // ATT_TILE
    vt_tile_spec = pl.BlockSpec((1, n_pairs, tiles_per_step, LANES, ATT_TILE),
                                lambda i: (i // per_seq, 0, i % per_seq, 0, 0))
    return pl.pallas_call(
        _even_proj_kernel,
        grid=(rows // tm,),
        in_specs=[pl.BlockSpec((tm, d), lambda i: (i, 0)),
                  pl.BlockSpec((1, 1, 6 * d), lambda i: (i // per_seq, 0, 0)),
                  pl.BlockSpec((1, d), lambda i: (0, 0)),
                  pl.BlockSpec((d, n), lambda i: (0, 0))],
        out_specs=[pl.BlockSpec((tm, n), lambda i: (i, 0)),
                   pl.BlockSpec((1, tm // MOBA_BLOCK, wa), lambda i: (i, 0, 0)),
                   vt_spec, vt_tile_spec],
        out_shape=[jax.ShapeDtypeStruct((rows, n), BF16),
                   jax.ShapeDtypeStruct((rows // tm, tm // MOBA_BLOCK, wa), F32),
                   vt_shape, _vt_shape(rows // seq, seq, n_pairs, 1)],
        compiler_params=_params(1),
        name="even_proj",
    )(x, mod, g, w_in)


def _pair_lanes(x, h):
    pair = h // 2
    return x[:, pair * LANES:(pair + 1) * LANES]


def _split_heads(q_step):
    first = lax.broadcasted_iota(jnp.int32, (q_step.shape[0], LANES), 1) < HEAD_DIM
    heads = []
    for pair in range(q_step.shape[1] // LANES):
        q2 = q_step[:, pair * LANES:(pair + 1) * LANES]
        zero = jnp.zeros_like(q2)
        heads += [jnp.where(first, q2, zero), jnp.where(first, zero, q2)]
    return heads


def _head_rows(vt_ref, g, h, n_tiles):
    return vt_ref[0, h // 2, g, (h % 2) * HEAD_DIM:(h % 2 + 1) * HEAD_DIM, 0:n_tiles * ATT_TILE]


def _tile_scores(keys, q_head, per_tile):
    t = ATT_TILE
    n = keys.shape[0] // t
    if per_tile:
        return [_dot_nt(keys[c * t:(c + 1) * t], q_head) for c in range(n)]
    s = _dot_nt(keys, q_head)
    return [s[c * t:(c + 1) * t] for c in range(n)]


def _group_keys(k_ref, g, n_tiles, key_group):
    start = pl.multiple_of(g * key_group * ATT_TILE, key_group * ATT_TILE)
    return k_ref[0, pl.ds(start, n_tiles * ATT_TILE), :]


def _sweep(qi, group, key_group, n_far_groups=None):
    g_own = qi // key_group
    own = [functools.partial(group, g_own, None, n + 1, True) for n in range(key_group)]
    state = lax.switch(qi % key_group, own)
    n_past = g_own if n_far_groups is None else jnp.minimum(g_own, n_far_groups)
    return lax.fori_loop(0, n_past, lambda n, st: group(g_own - 1 - n, st, key_group, False), state)


def _tile_iotas():
    t = ATT_TILE
    return lax.broadcasted_iota(jnp.int32, (t, t), 0), lax.broadcasted_iota(jnp.int32, (t, t), 1)


def _store_step_output(o_ref, outs_t):
    o_ref[0] = jnp.concatenate(outs_t, axis=0).T.astype(BF16)


def _softmax_group(s_tiles, ons, vt_h, acc_ref, h, stats, first):
    maxes = []
    for s, on in zip(s_tiles, ons):
        mx = jnp.max(s, axis=0, keepdims=True)
        maxes.append(mx if on is None else jnp.where(on, mx, -jnp.inf))
    m_new = functools.reduce(jnp.maximum, maxes)
    if not first:
        m_old, l_old = stats
        m_new = jnp.maximum(m_new, m_old)
    ps = []
    l_add = None
    for s, on in zip(s_tiles, ons):
        p = jnp.exp2(s - (m_new if on is None else jnp.where(on, m_new, jnp.inf)))
        p_sum = jnp.sum(p, axis=0, keepdims=True)
        l_add = p_sum if l_add is None else l_add + p_sum
        ps.append(p.astype(BF16))
    pv = _dot(vt_h, jnp.concatenate(ps, axis=0))
    if first:
        acc_ref[h] = pv
        return m_new, l_add
    alpha = jnp.exp2(m_old - m_new)
    acc_ref[h] = alpha * acc_ref[h] + pv
    return m_new, alpha * l_old + l_add


def _moba_select(qh, km_ref, sel_ref, qi):
    t = ATT_TILE
    km_hi, km_lo = _split_bf16(km_ref[0])
    nb = km_hi.shape[0]
    blk = lax.broadcasted_iota(jnp.int32, (nb, t), 0)
    past = blk < qi
    for h in range(HEADS_PER_STEP):
        gate = _dot_nt(_pair_lanes(km_hi, h), qh[h]) + _dot_nt(_pair_lanes(km_lo, h), qh[h])
        gate = jnp.where(past, gate, -jnp.inf)
        beaten = jnp.zeros((nb, t), F32)
        for other in range(nb):
            row = gate[other:other + 1, :]
            wins = jnp.where(row > gate, 1.0, jnp.where(row == gate, jnp.where(blk > other, 1.0, 0.0), 0.0))
            beaten = beaten + wins
        sel_ref[h] = jnp.where(past, jnp.where(beaten < MOBA_TOPK, 1.0, 0.0), 0.0)


def _moba_attn_kernel(q_ref, k_ref, vt_ref, bias_ref, km_ref, o_ref, acc_ref, sel_ref):
    qi = pl.program_id(2)
    qh = _split_heads(q_ref[0])
    n_bias = bias_ref.shape[1]
    _moba_select(qh, km_ref, sel_ref, qi)

    def group(g, stats, n_tiles, first):
        kg = _group_keys(k_ref, g, n_tiles, KEY_GROUP_EVEN)
        new = ()
        scores = [_tile_scores(_pair_lanes(kg, h), qh[h], per_tile=False) for h in range(HEADS_PER_STEP)]
        for h in range(HEADS_PER_STEP):
            s_tiles = scores[h]
            ons = []
            for c in range(n_tiles):
                j = g * KEY_GROUP_EVEN + c
                dt = n_tiles - 1 - c if first else qi - j
                s_tiles[c] = s_tiles[c] + bias_ref[h, dt if first else jnp.minimum(dt, n_bias - 1)]
                ons.append(None if first and dt == 0 else sel_ref[h, pl.ds(j, 1), :] > 0.5)
            new += _softmax_group(s_tiles, ons, _head_rows(vt_ref, g, h, n_tiles), acc_ref, h,
                                  None if first else stats[2 * h:2 * h + 2], first)
        return new

    stats = _sweep(qi, group, KEY_GROUP_EVEN)
    _store_step_output(o_ref, [acc_ref[h] / stats[2 * h + 1] for h in range(HEADS_PER_STEP)])


def _moba_attn(proj, vt, bias, km, batch, seq):
    t = ATT_TILE
    n_steps = N_HEADS_A // HEADS_PER_STEP
    n_bias = bias.shape[1]
    return pl.pallas_call(
        _moba_attn_kernel,
        grid=(batch, n_steps, seq // t),
        in_specs=[pl.BlockSpec((1, t, STEP_LANES), lambda b, p, i: (b, i, p)),
                  pl.BlockSpec((1, seq, STEP_LANES), lambda b, p, i: (b, 0, n_steps + p)),
                  pl.BlockSpec((1, PAIRS_PER_STEP) + vt.shape[2:], lambda b, p, i: (b, p, 0, 0, 0)),
                  pl.BlockSpec((HEADS_PER_STEP, n_bias, t, t), lambda b, p, i: (p, 0, 0, 0)),
                  pl.BlockSpec((1, seq // MOBA_BLOCK, STEP_LANES), lambda b, p, i: (b, 0, p))],
        out_specs=pl.BlockSpec((1, t, STEP_LANES), lambda b, p, i: (b, i, p)),
        out_shape=jax.ShapeDtypeStruct((batch, seq, N_HEADS_A * HEAD_DIM), BF16),
        scratch_shapes=[pltpu.VMEM((HEADS_PER_STEP, HEAD_DIM, t), F32),
                        pltpu.VMEM((HEADS_PER_STEP, seq // MOBA_BLOCK, t), F32)],
        compiler_params=_params(3),
        name="moba_attn",
    )(proj, proj, vt, bias, km)


def _dilated_far_kernel(q_ref, k_ref, v_ref, bias_ref, o_ref):
    t = ATT_TILE
    qh = _split_heads(q_ref[0])
    keys = k_ref[0]
    vt = v_ref[0].astype(F32).T.astype(BF16)
    accs, maxes, sums = [], [], []
    for first in range(0, N_HEADS_B, HEADS_PER_STEP):
        heads = range(first, first + HEADS_PER_STEP)
        scores = [_dot_nt(_pair_lanes(keys, h), qh[h]) for h in heads]
        for h, s in zip(heads, scores):
            s = s + bias_ref[h, 0]
            m = jnp.max(s, axis=0, keepdims=True)
            p = jnp.exp2(s - jnp.where(m > -jnp.inf, m, 0.0))
            maxes.append(m)
            sums.append(jnp.sum(p, axis=0, keepdims=True))
            accs.append(_dot(vt[h * HEAD_DIM:(h + 1) * HEAD_DIM], p.astype(BF16)))
    pad = jnp.zeros((DIL_FAR_LANES - N_HEADS_B * (HEAD_DIM + 2), t), F32)
    o_ref[0, 0] = jnp.concatenate(accs + maxes + sums + [pad], axis=0).T


def _dilated_far(proj_cls, bias, batch, lanes_per_class, lane0):
    t = ATT_TILE
    wb = N_HEADS_B * HEAD_DIM
    per_class = lanes_per_class // wb
    first = lane0 // wb
    cols = lambda group: pl.BlockSpec((1, t, wb), lambda b, c: (b, 0, c * per_class + first + group))
    return pl.pallas_call(
        _dilated_far_kernel,
        grid=(batch, DIL_CLASSES),
        in_specs=[cols(0), cols(1), cols(2),
                  pl.BlockSpec((N_HEADS_B, 1, t, t), lambda b, c: (0, 0, 0, 0))],
        out_specs=pl.BlockSpec((1, 1, t, DIL_FAR_LANES), lambda b, c: (b, c, 0, 0)),
        out_shape=jax.ShapeDtypeStruct((batch, DIL_CLASSES, t, DIL_FAR_LANES), F32),
        compiler_params=_params(2),
        name="dilated_far",
    )(proj_cls, proj_cls, proj_cls, bias)


def _dilated_near_kernel(q_ref, k_ref, vt_ref, bias_ref, far_ref, o_ref, acc_ref):
    t = ATT_TILE
    qi = pl.program_id(1)
    qh = _split_heads(q_ref[0])

    def tiles(n_tiles):
        first_tile = qi - (n_tiles - 1)
        keys = k_ref[0, pl.ds(pl.multiple_of(first_tile * t, t), n_tiles * t), :]
        stats = ()
        for first in range(0, N_HEADS_B, HEADS_PER_STEP):
            heads = range(first, first + HEADS_PER_STEP)
            scores = [_tile_scores(_pair_lanes(keys, h), qh[h], per_tile=False) for h in heads]
            for h, s_tiles in zip(heads, scores):
                s_tiles = [s + bias_ref[h, n_tiles - 1 - c] for c, s in enumerate(s_tiles)]
                rows = slice((h % 2) * HEAD_DIM, (h % 2 + 1) * HEAD_DIM)
                vt_h = jnp.concatenate([vt_ref[0, h // 2, first_tile + c, rows, :] for c in range(n_tiles)], axis=1)
                stats += _softmax_group(s_tiles, [None] * n_tiles, vt_h, acc_ref, h, None, True)
        return stats

    stats = lax.switch(jnp.minimum(qi, DIL_NEAR_TILES - 1),
                       [functools.partial(tiles, n + 1) for n in range(DIL_NEAR_TILES)])
    far = far_ref[0].T
    far_m0, far_l0 = N_HEADS_B * HEAD_DIM, N_HEADS_B * HEAD_DIM + N_HEADS_B
    outs = []
    for h in range(N_HEADS_B):
        m_near, l_near = stats[2 * h], stats[2 * h + 1]
        m_far, l_far = far[far_m0 + h:far_m0 + h + 1], far[far_l0 + h:far_l0 + h + 1]
        m = jnp.maximum(m_near, m_far)
        w_near, w_far = jnp.exp2(m_near - m), jnp.exp2(m_far - m)
        acc_far = far[h * HEAD_DIM:(h + 1) * HEAD_DIM]
        outs.append((w_near * acc_ref[h] + w_far * acc_far) / (w_near * l_near + w_far * l_far))
    _store_step_output(o_ref, outs)


def _dilated_near(proj, vt, bias, far, batch, seq, lane0):
    t = ATT_TILE
    wb = N_HEADS_B * HEAD_DIM
    first = lane0 // wb
    return pl.pallas_call(
        _dilated_near_kernel,
        grid=(batch, seq // t),
        in_specs=[pl.BlockSpec((1, t, wb), lambda b, i: (b, i, first)),
                  pl.BlockSpec((1, seq, wb), lambda b, i: (b, 0, first + 1)),
                  pl.BlockSpec((1,) + vt.shape[1:], lambda b, i: (b, 0, 0, 0, 0)),
                  pl.BlockSpec((N_HEADS_B, DIL_NEAR_TILES, t, t), lambda b, i: (0, 0, 0, 0)),
                  pl.BlockSpec((1, t, DIL_FAR_LANES), lambda b, i: (b, i, 0))],
        out_specs=pl.BlockSpec((1, t, wb), lambda b, i: (b, i, 0)),
        out_shape=jax.ShapeDtypeStruct((batch, seq, wb), BF16),
        scratch_shapes=[pltpu.VMEM((N_HEADS_B, HEAD_DIM, t), F32)],
        compiler_params=_params(2),
        name="dilated_near",
    )(proj, proj, vt, bias, far)


def _dilated_attn(proj, vt_tiles, bias_near, bias_far, batch, seq, lane0):
    n = proj.shape[-1]
    proj_cls = proj.reshape(batch, seq // DIL_CLASSES, DIL_CLASSES * n)
    far = _dilated_far(proj_cls, bias_far, batch, n, lane0)
    far = far.transpose(0, 2, 1, 3).reshape(batch, seq, DIL_FAR_LANES)
    return _dilated_near(proj, vt_tiles, bias_near, far, batch, seq, lane0)


def _odd_proj_kernel(x_ref, mod_ref, g_ref, win_ref, gq_ref, gkv_ref, wq_ref, wkv_ref,
                     cq_ref, sq_ref, ck_ref, sk_ref, qm_ref, km_ref, vtm_ref, sb_ref, vts_ref):
    d = D_MODEL
    mod = mod_ref[0]
    h = _prenorm(x_ref[...], g_ref[...], mod[:, d:2 * d], mod[:, 0:d])
    p = _dot(h.astype(BF16), win_ref[...])
    o = MLA_Q_RANK + MLA_KV_RANK
    c_q, c_kv = p[:, 0:MLA_Q_RANK], p[:, MLA_Q_RANK:o]
    k_rope, k_rope_swapped = p[:, o:o + LANES], p[:, o + LANES:o + 2 * LANES]
    sb0 = o + 2 * LANES
    n_qk = 2 * N_HEADS_D * HEAD_DIM
    sb_ref[...] = p[:, sb0:sb0 + n_qk].astype(BF16)
    _store_vt(vts_ref, p[:, sb0 + n_qk:])
    q12 = _dot(_rms(c_q, gq_ref[...]).astype(BF16), wq_ref[...])
    kv = _dot(_rms(c_kv, gkv_ref[...]).astype(BF16), wkv_ref[...])
    k_pe = k_rope * ck_ref[...] + k_rope_swapped * sk_ref[...]
    cq, sq = cq_ref[...], sq_ref[...]
    half = N_HEADS_C * LANES
    for hh in range(N_HEADS_C):
        cols = slice(hh * LANES, (hh + 1) * LANES)
        swapped = slice(half + hh * LANES, half + (hh + 1) * LANES)
        qm_ref[:, cols] = (q12[:, cols] * cq + q12[:, swapped] * sq).astype(BF16)
        km_ref[:, cols] = (kv[:, cols] + k_pe).astype(BF16)
    _store_vt(vtm_ref, kv[:, half:])


def _odd_proj(x, mod, g, w_in, gq, gkv, wq, wkv, tables, seq):
    rows, d = x.shape
    tm = ROW_TILE
    per_seq = seq // tm
    n_in = w_in.shape[1]
    n_sb = 2 * N_HEADS_D * HEAD_DIM
    n_q = N_HEADS_C * LANES
    n_pairs = N_HEADS_C // 2
    const = lambda i: (0, 0)
    table_spec = pl.BlockSpec((tm, LANES), lambda i: (i % per_seq, 0))
    vt_spec = _vt_spec(tm, per_seq, n_pairs, KEY_GROUP_ODD)
    vt_shape = _vt_shape(rows // seq, seq, n_pairs, KEY_GROUP_ODD)
    return pl.pallas_call(
        _odd_proj_kernel,
        grid=(rows // tm,),
        in_specs=[pl.BlockSpec((tm, d), lambda i: (i, 0)),
                  pl.BlockSpec((1, 1, 6 * d), lambda i: (i // per_seq, 0, 0)),
                  pl.BlockSpec((1, d), const),
                  pl.BlockSpec((d, n_in), const),
                  pl.BlockSpec((1, MLA_Q_RANK), const),
                  pl.BlockSpec((1, MLA_KV_RANK), const),
                  pl.BlockSpec(wq.shape, const),
                  pl.BlockSpec(wkv.shape, const),
                  table_spec, table_spec, table_spec, table_spec],
        out_specs=[pl.BlockSpec((tm, n_q), lambda i: (i, 0)),
                   pl.BlockSpec((tm, n_q), lambda i: (i, 0)),
                   vt_spec,
                   pl.BlockSpec((tm, n_sb), lambda i: (i, 0)),
                   vt_spec],
        out_shape=[jax.ShapeDtypeStruct((rows, n_q), BF16),
                   jax.ShapeDtypeStruct((rows, n_q), BF16),
                   vt_shape,
                   jax.ShapeDtypeStruct((rows, n_sb), BF16),
                   vt_shape],
        compiler_params=_params(1),
        name="odd_proj",
    )(x, mod, g, w_in, gq, gkv, wq, wkv, *tables)


def _mla_attn_kernel(q_ref, k_ref, vt_ref, o_ref, acc_ref):
    qi = pl.program_id(2)
    q_step = q_ref[0]
    qh = [q_step[:, h * LANES:(h + 1) * LANES] for h in range(HEADS_PER_STEP)]

    def group(g, stats, n_tiles, first):
        kg = _group_keys(k_ref, g, n_tiles, KEY_GROUP_ODD)
        new = ()
        scores = [_tile_scores(kg[:, h * LANES:(h + 1) * LANES], qh[h], per_tile=True)
                  for h in range(HEADS_PER_STEP)]
        for h in range(HEADS_PER_STEP):
            s_tiles = scores[h]
            if first:
                key, query = _tile_iotas()
                s_tiles[-1] = jnp.where(key <= query, s_tiles[-1], -jnp.inf)
            new += _softmax_group(s_tiles, [None] * n_tiles, _head_rows(vt_ref, g, h, n_tiles), acc_ref, h,
                                  None if first else stats[2 * h:2 * h + 2], first)
        return new

    stats = _sweep(qi, group, KEY_GROUP_ODD)
    _store_step_output(o_ref, [acc_ref[h] / stats[2 * h + 1] for h in range(HEADS_PER_STEP)])


def _mla_attn(qm, km, vt, batch, seq):
    t = ATT_TILE
    n_steps = N_HEADS_C // HEADS_PER_STEP
    qk_lanes = HEADS_PER_STEP * LANES
    return pl.pallas_call(
        _mla_attn_kernel,
        grid=(batch, n_steps, seq // t),
        in_specs=[pl.BlockSpec((1, t, qk_lanes), lambda b, p, i: (b, i, p)),
                  _resident((1, seq, qk_lanes), lambda b, p, i: (b, 0, p)),
                  _resident((1, PAIRS_PER_STEP) + vt.shape[2:], lambda b, p, i: (b, p, 0, 0, 0))],
        out_specs=pl.BlockSpec((1, t, STEP_LANES), lambda b, p, i: (b, i, p)),
        out_shape=jax.ShapeDtypeStruct((batch, seq, N_HEADS_C * MLA_V_DIM), BF16),
        scratch_shapes=[pltpu.VMEM((HEADS_PER_STEP, HEAD_DIM, t), F32)],
        compiler_params=_params(3),
        name="mla_attn",
    )(qm, km, vt)


def _stick_attn_kernel(q_ref, k_ref, vt_ref, o_ref, acc_ref):
    t = ATT_TILE
    qi = pl.program_id(2)
    qh = _split_heads(q_ref[0])
    key = lax.broadcasted_iota(jnp.int32, (t, t), 0)
    other = lax.broadcasted_iota(jnp.int32, (t, t), 1)
    from_here = jnp.where(other >= key, 1.0, 0.0).astype(BF16)
    from_here2 = jnp.concatenate([from_here, from_here], axis=1)

    strict = key < other

    def group(g, carries, n_tiles, first):
        kg = _group_keys(k_ref, g, n_tiles, KEY_GROUP_ODD)
        new = ()
        scores = [_tile_scores(_pair_lanes(kg, h), qh[h], per_tile=False) for h in range(HEADS_PER_STEP)]
        for h in range(HEADS_PER_STEP):
            z_tiles = scores[h]
            keep_tiles = []
            for z in z_tiles:
                neg_z = -z
                keep_tiles.append(jnp.minimum(neg_z, 0.0) - jnp.log(1.0 + jnp.exp(jnp.minimum(z, neg_z))))
            if first:
                keep_tiles[-1] = jnp.where(strict, keep_tiles[-1], 0.0)
            run = None if first else carries[h]
            a_tiles = [None] * n_tiles
            for c in reversed(range(n_tiles)):
                tail = _dot(from_here2, jnp.concatenate(_split_bf16(keep_tiles[c]), axis=0))
                tile_total = tail[0:1, :]
                if run is not None:
                    tail = tail + run
                a_tiles[c] = jnp.exp(z_tiles[c] + tail)
                run = tile_total if run is None else run + tile_total
            if first:
                a_tiles[-1] = jnp.where(strict, a_tiles[-1], 0.0)
            pv = _dot(_head_rows(vt_ref, g, h, n_tiles), jnp.concatenate(a_tiles, axis=0).astype(BF16))
            acc_ref[h] = pv if first else acc_ref[h] + pv
            new += (run,)
        return new

    _sweep(qi, group, KEY_GROUP_ODD)
    _store_step_output(o_ref, [acc_ref[h] for h in range(HEADS_PER_STEP)])


def _stick_attn(sb, vt, batch, seq):
    t = ATT_TILE
    n_steps = N_HEADS_D // HEADS_PER_STEP
    return pl.pallas_call(
        _stick_attn_kernel,
        grid=(batch, n_steps, seq // t),
        in_specs=[pl.BlockSpec((1, t, STEP_LANES), lambda b, p, i: (b, i, p)),
                  _resident((1, seq, STEP_LANES), lambda b, p, i: (b, 0, n_steps + p)),
                  _resident((1, PAIRS_PER_STEP) + vt.shape[2:], lambda b, p, i: (b, p, 0, 0, 0))],
        out_specs=pl.BlockSpec((1, t, STEP_LANES), lambda b, p, i: (b, i, p)),
        out_shape=jax.ShapeDtypeStruct((batch, seq, N_HEADS_D * HEAD_DIM), BF16),
        scratch_shapes=[pltpu.VMEM((HEADS_PER_STEP, HEAD_DIM, t), F32)],
        compiler_params=_params(3),
        name="stick_attn",
    )(sb, sb, vt)


def _post_attn_kernel(oa_ref, ob_ref, oah_ref, obh_ref, x_ref, xh_ref, mod_ref, gmix_ref, gpre_ref, gpost_ref,
                      wo_ref, wup_ref, cw_ref, cb_ref, wd_ref, o_ref, x1_ref, h_ref, u_ref, acc_ref, *, per_seq):
    d = D_MODEL
    tm = ROW_TILE
    halo = CONV_HALO
    cw = FF_CHUNK
    i = pl.program_id(0)
    mod = mod_ref[0]
    gate_m, shift, scale, gate_f = mod[:, 2 * d:3 * d], mod[:, 3 * d:4 * d], mod[:, 4 * d:5 * d], mod[:, 5 * d:6 * d]
    n_first = oa_ref.shape[1]

    def mixed(oa, ob, x):
        y = _dot(oa, wo_ref[0:n_first, :]) + _dot(ob, wo_ref[n_first:2 * n_first, :])
        return x + gate_m * _rms(y, gmix_ref[...])

    g = gpre_ref[...]
    ahead = _prenorm(mixed(oah_ref[...], obh_ref[...], xh_ref[...]), g, scale, shift)
    ahead = jnp.where(i % per_seq == 0, 0.0, ahead)
    h_ref[0:halo, :] = ahead.astype(BF16)
    x1_ref[...] = mixed(oa_ref[...], ob_ref[...], x_ref[...])
    h_ref[halo:halo + tm, :] = _prenorm(x1_ref[...], g, scale, shift).astype(BF16)

    def up(ch):
        slot = ch % 2
        for half in range(2):
            cols = slice(half * D_FF + ch * cw, half * D_FF + (ch + 1) * cw)
            u_ref[slot, half] = _dot(h_ref[...], wup_ref[:, cols])

    def conv(slot, half, ch):
        cols = slice(half * D_FF + ch * cw, half * D_FF + (ch + 1) * cw)
        w = cw_ref[:, cols]
        out = w[0:1, :] * u_ref[slot, half, halo - 2:halo - 2 + tm, :]
        out = out + w[1:2, :] * u_ref[slot, half, halo - 1:halo - 1 + tm, :]
        out = out + w[2:3, :] * u_ref[slot, half, halo:halo + tm, :]
        return out + cb_ref[:, cols]

    def down(ch, act):
        part = _dot(act, wd_ref[ch * cw:(ch + 1) * cw, :])
        if ch == 0:
            acc_ref[...] = part
        else:
            acc_ref[...] += part

    up(0)
    act = None
    for ch in range(N_FF_CHUNKS):
        if ch + 1 < N_FF_CHUNKS:
            up(ch + 1)
        if act is not None:
            down(ch - 1, act)
        slot = ch % 2
        act = (jax.nn.gelu(conv(slot, 0, ch), approximate=True) * conv(slot, 1, ch)).astype(BF16)
    down(N_FF_CHUNKS - 1, act)

    o_ref[...] = x1_ref[...] + gate_f * _rms(acc_ref[...], gpost_ref[...])


def _post_attn(oa, ob, x, mod, g_mix, g_pre, g_post, w_out, w_up, conv_w, conv_b, w_down, seq):
    rows, d = x.shape
    tm = ROW_TILE
    halo = CONV_HALO
    per_seq = seq // tm
    n_first = oa.shape[1]
    const = lambda i: (0, 0)
    tile = lambda i: (i, 0)
    ahead = lambda i: (jnp.maximum(i * (tm // halo) - 1, 0), 0)
    once = dict(pipeline_mode=pl.Buffered(1))
    return pl.pallas_call(
        functools.partial(_post_attn_kernel, per_seq=per_seq),
        grid=(rows // tm,),
        in_specs=[pl.BlockSpec((tm, n_first), tile), pl.BlockSpec((tm, n_first), tile),
                  pl.BlockSpec((halo, n_first), ahead), pl.BlockSpec((halo, n_first), ahead),
                  pl.BlockSpec((tm, d), tile), pl.BlockSpec((halo, d), ahead),
                  pl.BlockSpec((1, 1, 6 * d), lambda i: (i // per_seq, 0, 0)),
                  pl.BlockSpec((1, d), const), pl.BlockSpec((1, d), const), pl.BlockSpec((1, d), const),
                  pl.BlockSpec(w_out.shape, const, **once),
                  pl.BlockSpec(w_up.shape, const, **once),
                  pl.BlockSpec(conv_w.shape, const),
                  pl.BlockSpec(conv_b.shape, const),
                  pl.BlockSpec(w_down.shape, const, **once)],
        out_specs=pl.BlockSpec((tm, d), tile),
        out_shape=jax.ShapeDtypeStruct((rows, d), F32),
        scratch_shapes=[pltpu.VMEM((tm, d), F32),
                        pltpu.VMEM((tm + halo, d), BF16),
                        pltpu.VMEM((2, 2, tm + halo, FF_CHUNK), F32),
                        pltpu.VMEM((tm, d), F32)],
        compiler_params=_params(1),
        name="post_attn",
    )(oa, ob, oa, ob, x, x, mod, g_mix, g_pre, g_post, w_out, w_up, conv_w, conv_b, w_down)


def _rotate_half_cols(w):
    half = w.shape[-1] // 2
    return jnp.concatenate([-w[..., half:], w[..., :half]], axis=-1)


def _pad_cols(w, left, total):
    return jnp.pad(w, ((0, 0), (left, total - left - w.shape[1])))


def _rope_tables(seq):
    inv_freq = 1.0 / (ROPE_THETA ** (jnp.arange(0, MLA_ROPE_DIM, 2, dtype=F32) / MLA_ROPE_DIM))
    ang = jnp.arange(seq, dtype=F32)[:, None] * inv_freq[None, :]
    cos, sin = jnp.cos(ang), jnp.sin(ang)
    cos2 = _pad_cols(jnp.concatenate([cos, cos], axis=1), MLA_NOPE_DIM, LANES)
    sin2 = _pad_cols(jnp.concatenate([sin, sin], axis=1), MLA_NOPE_DIM, LANES)
    scale = (MLA_NOPE_DIM + MLA_ROPE_DIM) ** -0.5
    nope_ones = _pad_cols(jnp.ones((seq, MLA_NOPE_DIM), F32), 0, LANES)
    scale = scale * LOG2_E
    return (scale * (cos2 + nope_ones), scale * sin2, cos2, sin2)


def _odd_weights(w_in, w_uq, w_ukv):
    o = MLA_Q_RANK + MLA_KV_RANK
    w_rope = w_in[:, o:o + MLA_ROPE_DIM]
    scale_d = HEAD_DIM ** -0.5
    wd = N_HEADS_D * HEAD_DIM
    sb0 = o + MLA_ROPE_DIM
    w_in2 = jnp.concatenate([
        w_in[:, :o],
        _pad_cols(w_rope, MLA_NOPE_DIM, LANES),
        _pad_cols(_rotate_half_cols(w_rope), MLA_NOPE_DIM, LANES),
        w_in[:, sb0:sb0 + wd] * scale_d,
        w_in[:, sb0 + wd:],
    ], axis=1).astype(BF16)
    qd = MLA_NOPE_DIM + MLA_ROPE_DIM
    uq = w_uq.reshape(MLA_Q_RANK, N_HEADS_C, qd)
    plain = jnp.pad(uq, ((0, 0), (0, 0), (0, LANES - qd)))
    swapped = jnp.pad(_rotate_half_cols(uq[..., MLA_NOPE_DIM:]),
                      ((0, 0), (0, 0), (MLA_NOPE_DIM, LANES - qd)))
    wq = jnp.concatenate([plain.reshape(MLA_Q_RANK, -1), swapped.reshape(MLA_Q_RANK, -1)], axis=1).astype(BF16)
    ukv = w_ukv.reshape(MLA_KV_RANK, N_HEADS_C, MLA_NOPE_DIM + MLA_V_DIM)
    k_nope = jnp.pad(ukv[..., :MLA_NOPE_DIM], ((0, 0), (0, 0), (0, LANES - MLA_NOPE_DIM)))
    v = ukv[..., MLA_NOPE_DIM:]
    wkv = jnp.concatenate([k_nope.reshape(MLA_KV_RANK, -1), v.reshape(MLA_KV_RANK, -1)], axis=1).astype(BF16)
    return w_in2, wq, wkv


def _even_weights(w_in):
    wa = N_HEADS_A * HEAD_DIM
    scale = HEAD_DIM ** -0.5 * LOG2_E
    col = jnp.arange(w_in.shape[1])
    is_q = (col < wa) | ((col >= 3 * wa) & (col < 4 * wa))
    return (w_in * jnp.where(is_q, scale, 1.0)).astype(BF16)


def kernel(x, c, rel_bias, ada_w, ada_b, mix_pre_g, mix_post_g, ffn_pre_g, ffn_post_g, ab_w_in, ab_w_out,
           cd_w_in, mla_q_norm_g, mla_kv_norm_g, mla_w_uq, mla_w_ukv, cd_w_out, ffn_w_up, ffn_conv_w,
           ffn_conv_b, ffn_w_down):
    batch, seq, d = x.shape
    widest_group = max(KEY_GROUP_EVEN, KEY_GROUP_ODD) * ATT_TILE
    assert d == D_MODEL and seq % widest_group == 0 and (KEY_GROUP_EVEN * ATT_TILE) % ROW_TILE == 0
    assert seq == DIL_CLASSES * ATT_TILE
    rows = batch * seq
    xf = x.reshape(rows, d)

    mods = _mods(c, ada_w, ada_b)
    bias_a = _bias_tiles(rel_bias, 0, N_HEADS_A, MOBA_BIAS_TILES, 1, None, "moba_bias_tiles")
    bias_near = _bias_tiles(rel_bias, N_HEADS_A, N_HEADS_B, DIL_NEAR_TILES, 1, DIL_NEAR_BRANCHES,
                            "dilated_near_bias_tiles")
    bias_far = _bias_tiles(rel_bias, N_HEADS_A, N_HEADS_B, 1, DIL_CLASSES, DIL_FAR_BRANCHES,
                           "dilated_far_bias_tiles")
    tables = _rope_tables(seq)

    for layer in range(DEPTH):
        mod = mods[layer].reshape(batch, 1, 6 * d)
        i = layer // 2
        if layer % 2 == 0:
            proj, km, vta, vtb = _even_proj(xf, mod, mix_pre_g[layer].reshape(1, d),
                                            _even_weights(ab_w_in[i]), seq)
            proj = proj.reshape(batch, seq, -1)
            km = km.reshape(batch, seq // MOBA_BLOCK, -1)
            o_first = _moba_attn(proj, vta, bias_a, km, batch, seq)
            o_second = _dilated_attn(proj, vtb, bias_near, bias_far, batch, seq, 3 * N_HEADS_A * HEAD_DIM)
            w_out = ab_w_out[i]
        else:
            w_in2, wq, wkv = _odd_weights(cd_w_in[i], mla_w_uq[i], mla_w_ukv[i])
            qm, km, vtm, sb, vts = _odd_proj(xf, mod, mix_pre_g[layer].reshape(1, d), w_in2,
                                             mla_q_norm_g[i].reshape(1, -1), mla_kv_norm_g[i].reshape(1, -1),
                                             wq, wkv, tables, seq)
            o_first = _mla_attn(qm.reshape(batch, seq, -1), km.reshape(batch, seq, -1), vtm, batch, seq)
            o_second = _stick_attn(sb.reshape(batch, seq, -1), vts, batch, seq)
            w_out = cd_w_out[i]
        xf = _post_attn(o_first.reshape(rows, -1), o_second.reshape(rows, -1), xf, mod,
                        mix_post_g[layer].reshape(1, d), ffn_pre_g[layer].reshape(1, d),
                        ffn_post_g[layer].reshape(1, d), w_out.astype(BF16), ffn_w_up[layer].astype(BF16),
                        ffn_conv_w[layer], ffn_conv_b[layer].reshape(1, -1), ffn_w_down[layer].astype(BF16), seq)
    return xf.reshape(batch, seq, d)
```

```python
import functools
import math

import jax
import jax.numpy as jnp
from jax import lax
from jax.experimental import pallas as pl
from jax.experimental.pallas import tpu as pltpu

F32 = jnp.float32
BF16 = jnp.bfloat16

D_MODEL = 1024
DEPTH = 4
HEAD_DIM = 64
N_HEADS_A = 8
N_HEADS_B = 8
N_HEADS_C = 8
N_HEADS_D = 8
MOBA_BLOCK = 256
MOBA_TOPK = 3
DILATED_BRANCHES = ((128, 1), (512, 4), (2048, 16))
MLA_Q_RANK = 256
MLA_KV_RANK = 256
MLA_NOPE_DIM = 64
MLA_ROPE_DIM = 32
MLA_V_DIM = 64
ROPE_THETA = 10000.0
REL_BUCKETS = 32
REL_MAX_DIST = 2048
D_FF = 2816
CONV_WIDTH = 3
NORM_EPS = 1e-6
LOG2_E = math.log2(math.e)

LANES = 128
SUBLANES = 8
BF16_SUBLANES = 16
VMEM_LIMIT_BYTES = 56 * 1024 * 1024

ATT_TILE = MOBA_BLOCK
KEY_GROUP_EVEN = 4
KEY_GROUP_ODD = 8
PAIRS_PER_STEP = 2
HEADS_PER_STEP = 2 * PAIRS_PER_STEP
STEP_LANES = PAIRS_PER_STEP * LANES
ROW_TILE = 512
FF_CHUNK = 256
N_FF_CHUNKS = D_FF // FF_CHUNK
CONV_HALO = BF16_SUBLANES
MOBA_BIAS_TILES = REL_MAX_DIST // ATT_TILE + 2
DIL_SPLIT = DILATED_BRANCHES[1][0]
DIL_CLASSES = DILATED_BRANCHES[2][1]
DIL_NEAR_TILES = DIL_SPLIT // ATT_TILE + 1
DIL_FAR_LANES = 5 * LANES
assert DIL_FAR_LANES >= N_HEADS_B * (HEAD_DIM + 2)
DIL_NEAR_BRANCHES = tuple((-1, min(window, DIL_SPLIT), dil) for window, dil in DILATED_BRANCHES)
DIL_FAR_BRANCHES = ((DIL_SPLIT, DILATED_BRANCHES[2][0], DIL_CLASSES),)
assert all(window <= DIL_SPLIT for window, _ in DILATED_BRANCHES[:2]) and DIL_SPLIT % DIL_CLASSES == 0

_NT = (((1,), (1,)), ((), ()))


def _bucket_lower_bounds():
    max_exact = REL_BUCKETS // 2
    ratio = REL_MAX_DIST // max_exact
    n_log = REL_BUCKETS - max_exact
    lows = list(range(max_exact + 1))
    for k in range(1, n_log):
        d = lows[-1]
        while d ** n_log < (max_exact ** n_log) * (ratio ** k):
            d += 1
        lows.append(d)
    return lows


_BUCKET_LOW = _bucket_lower_bounds()


def _dot(a, b):
    return jnp.dot(a, b, preferred_element_type=F32)


def _dot_nt(a, b):
    return lax.dot_general(a, b, _NT, preferred_element_type=F32)


def _split_bf16(x):
    hi = x.astype(BF16)
    lo = (x - hi.astype(F32)).astype(BF16)
    return hi, lo


def _rms(x, g):
    return (x * lax.rsqrt(jnp.mean(x * x, axis=-1, keepdims=True) + NORM_EPS)) * g


def _prenorm(x, g, scale, shift):
    return _rms(x, g) * (1.0 + scale) + shift


def _resident(block_shape, index_map):
    return pl.BlockSpec(block_shape, index_map)


def _params(n_grid_dims):
    return pltpu.CompilerParams(dimension_semantics=("arbitrary",) * n_grid_dims,
                                vmem_limit_bytes=VMEM_LIMIT_BYTES)


def _mods_kernel(c_ref, w_ref, b_ref, o_ref):
    c = c_ref[...]
    cond = c * jax.nn.sigmoid(c)
    c_hi, c_lo = _split_bf16(cond)
    w_hi, w_lo = _split_bf16(w_ref[0])
    o_ref[0] = _dot(c_hi, w_hi) + _dot(c_hi, w_lo) + _dot(c_lo, w_hi) + b_ref[0]


def _mods(c, ada_w, ada_b):
    b, d = c.shape
    rows = BF16_SUBLANES
    n_out = ada_w.shape[-1]
    tn = n_out // 4
    c_pad = jnp.zeros((rows, d), F32).at[:b].set(c)
    out = pl.pallas_call(
        _mods_kernel,
        grid=(DEPTH, n_out // tn),
        in_specs=[pl.BlockSpec((rows, d), lambda l, j: (0, 0)),
                  pl.BlockSpec((1, d, tn), lambda l, j: (l, 0, j)),
                  pl.BlockSpec((1, 1, tn), lambda l, j: (l, 0, j))],
        out_specs=pl.BlockSpec((1, rows, tn), lambda l, j: (l, 0, j)),
        out_shape=jax.ShapeDtypeStruct((DEPTH, rows, n_out), F32),
        compiler_params=_params(2),
        name="ada_mods",
    )(c_pad, ada_w, ada_b.reshape(DEPTH, 1, n_out))
    return out[:, :b]


def _bias_tiles_kernel(tab_ref, o_ref, *, head_off, step, branches):
    h = pl.program_id(0) + head_off
    d = pl.program_id(1)
    t = ATT_TILE
    dist = (d * t + lax.broadcasted_iota(jnp.int32, (SUBLANES, 2 * t), 1) - t) * step
    val = jnp.full(dist.shape, tab_ref[h, 0], F32)
    for b in range(1, REL_BUCKETS):
        val = jnp.where(dist >= _BUCKET_LOW[b], tab_ref[h, b], val)
    if branches is not None:
        mult = jnp.zeros(dist.shape, F32)
        for beyond, window, dil in branches:
            hit = jnp.where(dist <= window, jnp.where((dist & (dil - 1)) == 0, 1.0, 0.0), 0.0)
            mult = mult + jnp.where(dist > beyond, hit, 0.0)
        log_mult = jnp.where(mult > 2.5, math.log(3.0), jnp.where(mult > 1.5, math.log(2.0), 0.0))
        val = jnp.where(mult > 0.5, val + log_mult, -jnp.inf)
    val = jnp.where(dist >= 0, val * LOG2_E, -jnp.inf)
    strip = jnp.concatenate([val] * (t // SUBLANES), axis=0)
    rotated = pltpu.roll(strip, 0, 1, stride=1, stride_axis=0)
    o_ref[0, 0] = rotated[:, t:2 * t]


def _bias_tiles(rel_bias, head_off, n_heads, n_tiles, step, branches, name):
    t = ATT_TILE
    return pl.pallas_call(
        functools.partial(_bias_tiles_kernel, head_off=head_off, step=step, branches=branches),
        grid=(n_heads, n_tiles),
        in_specs=[pl.BlockSpec(memory_space=pltpu.SMEM)],
        out_specs=pl.BlockSpec((1, 1, t, t), lambda h, d: (h, d, 0, 0)),
        out_shape=jax.ShapeDtypeStruct((n_heads, n_tiles, t, t), F32),
        compiler_params=_params(2),
        name=name,
    )(rel_bias)


def _store_vt(vt_ref, v):
    vt = v.T.astype(BF16)
    for p in range(vt.shape[0] // LANES):
        vt_ref[0, p, 0] = vt[p * LANES:(p + 1) * LANES, :]


def _vt_spec(tm, per_seq, n_pairs, key_group):
    per_group = key_group * ATT_TILE // tm
    return pl.BlockSpec((1, n_pairs, 1, LANES, tm),
                        lambda i: (i // per_seq, 0, (i % per_seq) // per_group, 0, (i % per_seq) % per_group))


def _vt_shape(batch, seq, n_pairs, key_group):
    group_keys = key_group * ATT_TILE
    return jax.ShapeDtypeStruct((batch, n_pairs, seq // group_keys, LANES, group_keys), BF16)


def _even_proj_kernel(x_ref, mod_ref, g_ref, w_ref, o_ref, km_ref, vta_ref, vtb_ref, cls_ref):
    d = D_MODEL
    mod = mod_ref[0]
    h = _prenorm(x_ref[...], g_ref[...], mod[:, d:2 * d], mod[:, 0:d])
    p = _dot(h.astype(BF16), w_ref[...])
    p_bf16 = p.astype(BF16)
    o_ref[...] = p_bf16
    tm = p.shape[0]
    per_class = tm // DIL_CLASSES
    out_row = lax.broadcasted_iota(jnp.int32, (tm, tm), 0)
    in_row = lax.broadcasted_iota(jnp.int32, (tm, tm), 1)
    source = (out_row % per_class) * DIL_CLASSES + out_row // per_class
    permute = jnp.where(in_row == source, 1.0, 0.0).astype(BF16)
    by_class = _dot(permute, p_bf16[:, 3 * N_HEADS_A * HEAD_DIM:]).astype(BF16)
    for cls in range(DIL_CLASSES):
        cls_ref[0, cls] = by_class[cls * per_class:(cls + 1) * per_class]
    wa = N_HEADS_A * HEAD_DIM
    ka = p[:, wa:2 * wa]
    nb = ka.shape[0] // MOBA_BLOCK
    km_ref[0] = jnp.mean(ka.reshape(nb, MOBA_BLOCK, wa), axis=1)
    _store_vt(vta_ref, p[:, 2 * wa:3 * wa])
    vtb = p[:, 5 * wa:6 * wa].T.astype(BF16)
    t = ATT_TILE
    for pair in range(vtb.shape[0] // LANES):
        for tile in range(vtb.shape[1] // t):
            vtb_ref[0, pair, tile] = vtb[pair * LANES:(pair + 1) * LANES, tile * t:(tile + 1) * t]


def _even_proj(x, mod, g, w_in, seq):
    rows, d = x.shape
    tm = ROW_TILE
    n = w_in.shape[1]
    wa = N_HEADS_A * HEAD_DIM
    per_seq = seq // tm
    n_pairs = N_HEADS_A // 2
    vt_spec = _vt_spec(tm, per_seq, n_pairs, KEY_GROUP_EVEN)
    vt_shape = _vt_shape(rows // seq, seq, n_pairs, KEY_GROUP_EVEN)
    tiles_per_step = tm // ATT_TILE
    vt_tile_spec = pl.BlockSpec((1, n_pairs, tiles_per_step, LANES, ATT_TILE),
                                lambda i: (i // per_seq, 0, i % per_seq, 0, 0))
    return pl.pallas_call(
        _even_proj_kernel,
        grid=(rows // tm,),
        in_specs=[pl.BlockSpec((tm, d), lambda i: (i, 0)),
                  pl.BlockSpec((1, 1, 6 * d), lambda i: (i // per_seq, 0, 0)),
                  pl.BlockSpec((1, d), lambda i: (0, 0)),
                  pl.BlockSpec((d, n), lambda i: (0, 0))],
        out_specs=[pl.BlockSpec((tm, n), lambda i: (i, 0)),
                   pl.BlockSpec((1, tm // MOBA_BLOCK, wa), lambda i: (i, 0, 0)),
                   vt_spec, vt_tile_spec,
                   pl.BlockSpec((1, DIL_CLASSES, tm // DIL_CLASSES, n - 3 * wa),
                                lambda i: (i // per_seq, 0, i % per_seq, 0))],
        out_shape=[jax.ShapeDtypeStruct((rows, n), BF16),
                   jax.ShapeDtypeStruct((rows // tm, tm // MOBA_BLOCK, wa), F32),
                   vt_shape, _vt_shape(rows // seq, seq, n_pairs, 1),
                   jax.ShapeDtypeStruct((rows // seq, DIL_CLASSES, seq // DIL_CLASSES, n - 3 * wa), BF16)],
        compiler_params=_params(1),
        name="even_proj",
    )(x, mod, g, w_in)


def _pair_lanes(x, h):
    pair = h // 2
    return x[:, pair * LANES:(pair + 1) * LANES]


def _split_heads(q_step):
    first = lax.broadcasted_iota(jnp.int32, (q_step.shape[0], LANES), 1) < HEAD_DIM
    heads = []
    for pair in range(q_step.shape[1] // LANES):
        q2 = q_step[:, pair * LANES:(pair + 1) * LANES]
        zero = jnp.zeros_like(q2)
        heads += [jnp.where(first, q2, zero), jnp.where(first, zero, q2)]
    return heads


def _head_rows(vt_ref, g, h, n_tiles):
    return vt_ref[0, h // 2, g, (h % 2) * HEAD_DIM:(h % 2 + 1) * HEAD_DIM, 0:n_tiles * ATT_TILE]


def _tile_scores(keys, q_head, per_tile):
    t = ATT_TILE
    n = keys.shape[0] // t
    if per_tile:
        return [_dot_nt(keys[c * t:(c + 1) * t], q_head) for c in range(n)]
    s = _dot_nt(keys, q_head)
    return [s[c * t:(c + 1) * t] for c in range(n)]


def _group_keys(k_ref, g, n_tiles, key_group):
    start = pl.multiple_of(g * key_group * ATT_TILE, key_group * ATT_TILE)
    return k_ref[0, pl.ds(start, n_tiles * ATT_TILE), :]


def _sweep(qi, group, key_group, n_far_groups=None):
    g_own = qi // key_group
    own = [functools.partial(group, g_own, None, n + 1, True) for n in range(key_group)]
    state = lax.switch(qi % key_group, own)
    n_past = g_own if n_far_groups is None else jnp.minimum(g_own, n_far_groups)
    return lax.fori_loop(0, n_past, lambda n, st: group(g_own - 1 - n, st, key_group, False), state)


def _tile_iotas():
    t = ATT_TILE
    return lax.broadcasted_iota(jnp.int32, (t, t), 0), lax.broadcasted_iota(jnp.int32, (t, t), 1)


def _store_step_output(o_ref, outs_t):
    o_ref[0] = jnp.concatenate(outs_t, axis=0).T.astype(BF16)


def _softmax_group(s_tiles, ons, vt_h, acc_ref, h, stats, first):
    maxes = []
    for s, on in zip(s_tiles, ons):
        mx = jnp.max(s, axis=0, keepdims=True)
        maxes.append(mx if on is None else jnp.where(on, mx, -jnp.inf))
    m_new = functools.reduce(jnp.maximum, maxes)
    if not first:
        m_old, l_old = stats
        m_new = jnp.maximum(m_new, m_old)
    ps = []
    l_add = None
    for s, on in zip(s_tiles, ons):
        p = jnp.exp2(s - (m_new if on is None else jnp.where(on, m_new, jnp.inf)))
        p_sum = jnp.sum(p, axis=0, keepdims=True)
        l_add = p_sum if l_add is None else l_add + p_sum
        ps.append(p.astype(BF16))
    pv = _dot(vt_h, jnp.concatenate(ps, axis=0))
    if first:
        acc_ref[h] = pv
        return m_new, l_add
    alpha = jnp.exp2(m_old - m_new)
    acc_ref[h] = alpha * acc_ref[h] + pv
    return m_new, alpha * l_old + l_add


def _moba_select(qh, km_ref, sel_ref, qi):
    t = ATT_TILE
    km_hi, km_lo = _split_bf16(km_ref[0])
    nb = km_hi.shape[0]
    blk = lax.broadcasted_iota(jnp.int32, (nb, t), 0)
    past = blk < qi
    for h in range(HEADS_PER_STEP):
        gate = _dot_nt(_pair_lanes(km_hi, h), qh[h]) + _dot_nt(_pair_lanes(km_lo, h), qh[h])
        gate = jnp.where(past, gate, -jnp.inf)
        beaten = jnp.zeros((nb, t), F32)
        for other in range(nb):
            row = gate[other:other + 1, :]
            wins = jnp.where(row > gate, 1.0, jnp.where(row == gate, jnp.where(blk > other, 1.0, 0.0), 0.0))
            beaten = beaten + wins
        sel_ref[h] = jnp.where(past, jnp.where(beaten < MOBA_TOPK, 1.0, 0.0), 0.0)


def _moba_attn_kernel(q_ref, k_ref, vt_ref, bias_ref, km_ref, o_ref, acc_ref, sel_ref):
    qi = pl.program_id(2)
    qh = _split_heads(q_ref[0])
    n_bias = bias_ref.shape[1]
    _moba_select(qh, km_ref, sel_ref, qi)

    def group(g, stats, n_tiles, first):
        kg = _group_keys(k_ref, g, n_tiles, KEY_GROUP_EVEN)
        new = ()
        scores = [_tile_scores(_pair_lanes(kg, h), qh[h], per_tile=False) for h in range(HEADS_PER_STEP)]
        for h in range(HEADS_PER_STEP):
            s_tiles = scores[h]
            ons = []
            for c in range(n_tiles):
                j = g * KEY_GROUP_EVEN + c
                dt = n_tiles - 1 - c if first else qi - j
                s_tiles[c] = s_tiles[c] + bias_ref[h, dt if first else jnp.minimum(dt, n_bias - 1)]
                ons.append(None if first and dt == 0 else sel_ref[h, pl.ds(j, 1), :] > 0.5)
            new += _softmax_group(s_tiles, ons, _head_rows(vt_ref, g, h, n_tiles), acc_ref, h,
                                  None if first else stats[2 * h:2 * h + 2], first)
        return new

    stats = _sweep(qi, group, KEY_GROUP_EVEN)
    _store_step_output(o_ref, [acc_ref[h] / stats[2 * h + 1] for h in range(HEADS_PER_STEP)])


def _moba_attn(proj, vt, bias, km, batch, seq):
    t = ATT_TILE
    n_steps = N_HEADS_A // HEADS_PER_STEP
    n_bias = bias.shape[1]
    return pl.pallas_call(
        _moba_attn_kernel,
        grid=(batch, n_steps, seq // t),
        in_specs=[pl.BlockSpec((1, t, STEP_LANES), lambda b, p, i: (b, i, p)),
                  pl.BlockSpec((1, seq, STEP_LANES), lambda b, p, i: (b, 0, n_steps + p)),
                  pl.BlockSpec((1, PAIRS_PER_STEP) + vt.shape[2:], lambda b, p, i: (b, p, 0, 0, 0)),
                  pl.BlockSpec((HEADS_PER_STEP, n_bias, t, t), lambda b, p, i: (p, 0, 0, 0)),
                  pl.BlockSpec((1, seq // MOBA_BLOCK, STEP_LANES), lambda b, p, i: (b, 0, p))],
        out_specs=pl.BlockSpec((1, t, STEP_LANES), lambda b, p, i: (b, i, p)),
        out_shape=jax.ShapeDtypeStruct((batch, seq, N_HEADS_A * HEAD_DIM), BF16),
        scratch_shapes=[pltpu.VMEM((HEADS_PER_STEP, HEAD_DIM, t), F32),
                        pltpu.VMEM((HEADS_PER_STEP, seq // MOBA_BLOCK, t), F32)],
        compiler_params=_params(3),
        name="moba_attn",
    )(proj, proj, vt, bias, km)


def _dilated_far_kernel(q_ref, k_ref, v_ref, bias_ref, o_ref):
    t = ATT_TILE
    qh = _split_heads(q_ref[0, 0])
    keys = k_ref[0, 0]
    vt = v_ref[0, 0].astype(F32).T.astype(BF16)
    accs, maxes, sums = [], [], []
    for first in range(0, N_HEADS_B, HEADS_PER_STEP):
        heads = range(first, first + HEADS_PER_STEP)
        scores = [_dot_nt(_pair_lanes(keys, h), qh[h]) for h in heads]
        for h, s in zip(heads, scores):
            s = s + bias_ref[h, 0]
            m = jnp.max(s, axis=0, keepdims=True)
            p = jnp.exp2(s - jnp.where(m > -jnp.inf, m, 0.0))
            maxes.append(m)
            sums.append(jnp.sum(p, axis=0, keepdims=True))
            accs.append(_dot(vt[h * HEAD_DIM:(h + 1) * HEAD_DIM], p.astype(BF16)))
    pad = jnp.zeros((DIL_FAR_LANES - N_HEADS_B * (HEAD_DIM + 2), t), F32)
    o_ref[0, 0] = jnp.concatenate(accs + maxes + sums + [pad], axis=0).T


def _dilated_far(qkv_cls, bias, batch):
    t = ATT_TILE
    wb = N_HEADS_B * HEAD_DIM
    cols = lambda group: pl.BlockSpec((1, 1, t, wb), lambda b, c: (b, c, 0, group))
    return pl.pallas_call(
        _dilated_far_kernel,
        grid=(batch, DIL_CLASSES),
        in_specs=[cols(0), cols(1), cols(2),
                  pl.BlockSpec((N_HEADS_B, 1, t, t), lambda b, c: (0, 0, 0, 0))],
        out_specs=pl.BlockSpec((1, 1, t, DIL_FAR_LANES), lambda b, c: (b, c, 0, 0)),
        out_shape=jax.ShapeDtypeStruct((batch, DIL_CLASSES, t, DIL_FAR_LANES), F32),
        compiler_params=_params(2),
        name="dilated_far",
    )(qkv_cls, qkv_cls, qkv_cls, bias)


def _dilated_near_kernel(q_ref, k_ref, vt_ref, bias_ref, far_ref, o_ref, acc_ref):
    t = ATT_TILE
    qi = pl.program_id(1)
    qh = _split_heads(q_ref[0])

    def tiles(n_tiles):
        first_tile = qi - (n_tiles - 1)
        keys = k_ref[0, pl.ds(pl.multiple_of(first_tile * t, t), n_tiles * t), :]
        stats = ()
        for first in range(0, N_HEADS_B, HEADS_PER_STEP):
            heads = range(first, first + HEADS_PER_STEP)
            scores = [_tile_scores(_pair_lanes(keys, h), qh[h], per_tile=False) for h in heads]
            for h, s_tiles in zip(heads, scores):
                s_tiles = [s + bias_ref[h, n_tiles - 1 - c] for c, s in enumerate(s_tiles)]
                rows = slice((h % 2) * HEAD_DIM, (h % 2 + 1) * HEAD_DIM)
                vt_h = jnp.concatenate([vt_ref[0, h // 2, first_tile + c, rows, :] for c in range(n_tiles)], axis=1)
                stats += _softmax_group(s_tiles, [None] * n_tiles, vt_h, acc_ref, h, None, True)
        return stats

    stats = lax.switch(jnp.minimum(qi, DIL_NEAR_TILES - 1),
                       [functools.partial(tiles, n + 1) for n in range(DIL_NEAR_TILES)])
    far = far_ref[0].T
    far_m0, far_l0 = N_HEADS_B * HEAD_DIM, N_HEADS_B * HEAD_DIM + N_HEADS_B
    outs = []
    for h in range(N_HEADS_B):
        m_near, l_near = stats[2 * h], stats[2 * h + 1]
        m_far, l_far = far[far_m0 + h:far_m0 + h + 1], far[far_l0 + h:far_l0 + h + 1]
        m = jnp.maximum(m_near, m_far)
        w_near, w_far = jnp.exp2(m_near - m), jnp.exp2(m_far - m)
        acc_far = far[h * HEAD_DIM:(h + 1) * HEAD_DIM]
        outs.append((w_near * acc_ref[h] + w_far * acc_far) / (w_near * l_near + w_far * l_far))
    _store_step_output(o_ref, outs)


def _dilated_near(proj, vt, bias, far, batch, seq, lane0):
    t = ATT_TILE
    wb = N_HEADS_B * HEAD_DIM
    first = lane0 // wb
    return pl.pallas_call(
        _dilated_near_kernel,
        grid=(batch, seq // t),
        in_specs=[pl.BlockSpec((1, t, wb), lambda b, i: (b, i, first)),
                  pl.BlockSpec((1, seq, wb), lambda b, i: (b, 0, first + 1)),
                  pl.BlockSpec((1,) + vt.shape[1:], lambda b, i: (b, 0, 0, 0, 0)),
                  pl.BlockSpec((N_HEADS_B, DIL_NEAR_TILES, t, t), lambda b, i: (0, 0, 0, 0)),
                  pl.BlockSpec((1, t, DIL_FAR_LANES), lambda b, i: (b, i, 0))],
        out_specs=pl.BlockSpec((1, t, wb), lambda b, i: (b, i, 0)),
        out_shape=jax.ShapeDtypeStruct((batch, seq, wb), BF16),
        scratch_shapes=[pltpu.VMEM((N_HEADS_B, HEAD_DIM, t), F32)],
        compiler_params=_params(2),
        name="dilated_near",
    )(proj, proj, vt, bias, far)


def _dilated_attn(proj, qkv_cls, vt_tiles, bias_near, bias_far, batch, seq, lane0):
    far = _dilated_far(qkv_cls, bias_far, batch)
    far = far.transpose(0, 2, 1, 3).reshape(batch, seq, DIL_FAR_LANES)
    return _dilated_near(proj, vt_tiles, bias_near, far, batch, seq, lane0)


def _odd_proj_kernel(x_ref, mod_ref, g_ref, win_ref, gq_ref, gkv_ref, wq_ref, wkv_ref,
                     cq_ref, sq_ref, ck_ref, sk_ref, qm_ref, km_ref, vtm_ref, sb_ref, vts_ref):
    d = D_MODEL
    mod = mod_ref[0]
    h = _prenorm(x_ref[...], g_ref[...], mod[:, d:2 * d], mod[:, 0:d])
    p = _dot(h.astype(BF16), win_ref[...])
    o = MLA_Q_RANK + MLA_KV_RANK
    c_q, c_kv = p[:, 0:MLA_Q_RANK], p[:, MLA_Q_RANK:o]
    k_rope, k_rope_swapped = p[:, o:o + LANES], p[:, o + LANES:o + 2 * LANES]
    sb0 = o + 2 * LANES
    n_qk = 2 * N_HEADS_D * HEAD_DIM
    sb_ref[...] = p[:, sb0:sb0 + n_qk].astype(BF16)
    _store_vt(vts_ref, p[:, sb0 + n_qk:])
    q12 = _dot(_rms(c_q, gq_ref[...]).astype(BF16), wq_ref[...])
    kv = _dot(_rms(c_kv, gkv_ref[...]).astype(BF16), wkv_ref[...])
    k_pe = k_rope * ck_ref[...] + k_rope_swapped * sk_ref[...]
    cq, sq = cq_ref[...], sq_ref[...]
    half = N_HEADS_C * LANES
    for hh in range(N_HEADS_C):
        cols = slice(hh * LANES, (hh + 1) * LANES)
        swapped = slice(half + hh * LANES, half + (hh + 1) * LANES)
        qm_ref[:, cols] = (q12[:, cols] * cq + q12[:, swapped] * sq).astype(BF16)
        km_ref[:, cols] = (kv[:, cols] + k_pe).astype(BF16)
    _store_vt(vtm_ref, kv[:, half:])


def _odd_proj(x, mod, g, w_in, gq, gkv, wq, wkv, tables, seq):
    rows, d = x.shape
    tm = ROW_TILE
    per_seq = seq // tm
    n_in = w_in.shape[1]
    n_sb = 2 * N_HEADS_D * HEAD_DIM
    n_q = N_HEADS_C * LANES
    n_pairs = N_HEADS_C // 2
    const = lambda i: (0, 0)
    table_spec = pl.BlockSpec((tm, LANES), lambda i: (i % per_seq, 0))
    vt_spec = _vt_spec(tm, per_seq, n_pairs, KEY_GROUP_ODD)
    vt_shape = _vt_shape(rows // seq, seq, n_pairs, KEY_GROUP_ODD)
    return pl.pallas_call(
        _odd_proj_kernel,
        grid=(rows // tm,),
        in_specs=[pl.BlockSpec((tm, d), lambda i: (i, 0)),
                  pl.BlockSpec((1, 1, 6 * d), lambda i: (i // per_seq, 0, 0)),
                  pl.BlockSpec((1, d), const),
                  pl.BlockSpec((d, n_in), const),
                  pl.BlockSpec((1, MLA_Q_RANK), const),
                  pl.BlockSpec((1, MLA_KV_RANK), const),
                  pl.BlockSpec(wq.shape, const),
                  pl.BlockSpec(wkv.shape, const),
                  table_spec, table_spec, table_spec, table_spec],
        out_specs=[pl.BlockSpec((tm, n_q), lambda i: (i, 0)),
                   pl.BlockSpec((tm, n_q), lambda i: (i, 0)),
                   vt_spec,
                   pl.BlockSpec((tm, n_sb), lambda i: (i, 0)),
                   vt_spec],
        out_shape=[jax.ShapeDtypeStruct((rows, n_q), BF16),
                   jax.ShapeDtypeStruct((rows, n_q), BF16),
                   vt_shape,
                   jax.ShapeDtypeStruct((rows, n_sb), BF16),
                   vt_shape],
        compiler_params=_params(1),
        name="odd_proj",
    )(x, mod, g, w_in, gq, gkv, wq, wkv, *tables)


def _mla_attn_kernel(q_ref, k_ref, vt_ref, o_ref, acc_ref):
    qi = pl.program_id(2)
    q_step = q_ref[0]
    qh = [q_step[:, h * LANES:(h + 1) * LANES] for h in range(HEADS_PER_STEP)]

    def group(g, stats, n_tiles, first):
        kg = _group_keys(k_ref, g, n_tiles, KEY_GROUP_ODD)
        new = ()
        scores = [_tile_scores(kg[:, h * LANES:(h + 1) * LANES], qh[h], per_tile=True)
                  for h in range(HEADS_PER_STEP)]
        for h in range(HEADS_PER_STEP):
            s_tiles = scores[h]
            if first:
                key, query = _tile_iotas()
                s_tiles[-1] = jnp.where(key <= query, s_tiles[-1], -jnp.inf)
            new += _softmax_group(s_tiles, [None] * n_tiles, _head_rows(vt_ref, g, h, n_tiles), acc_ref, h,
                                  None if first else stats[2 * h:2 * h + 2], first)
        return new

    stats = _sweep(qi, group, KEY_GROUP_ODD)
    _store_step_output(o_ref, [acc_ref[h] / stats[2 * h + 1] for h in range(HEADS_PER_STEP)])


def _mla_attn(qm, km, vt, batch, seq):
    t = ATT_TILE
    n_steps = N_HEADS_C // HEADS_PER_STEP
    qk_lanes = HEADS_PER_STEP * LANES
    return pl.pallas_call(
        _mla_attn_kernel,
        grid=(batch, n_steps, seq // t),
        in_specs=[pl.BlockSpec((1, t, qk_lanes), lambda b, p, i: (b, i, p)),
                  _resident((1, seq, qk_lanes), lambda b, p, i: (b, 0, p)),
                  _resident((1, PAIRS_PER_STEP) + vt.shape[2:], lambda b, p, i: (b, p, 0, 0, 0))],
        out_specs=pl.BlockSpec((1, t, STEP_LANES), lambda b, p, i: (b, i, p)),
        out_shape=jax.ShapeDtypeStruct((batch, seq, N_HEADS_C * MLA_V_DIM), BF16),
        scratch_shapes=[pltpu.VMEM((HEADS_PER_STEP, HEAD_DIM, t), F32)],
        compiler_params=_params(3),
        name="mla_attn",
    )(qm, km, vt)


def _stick_attn_kernel(q_ref, k_ref, vt_ref, o_ref, acc_ref):
    t = ATT_TILE
    qi = pl.program_id(2)
    qh = _split_heads(q_ref[0])
    key = lax.broadcasted_iota(jnp.int32, (t, t), 0)
    other = lax.broadcasted_iota(jnp.int32, (t, t), 1)
    from_here = jnp.where(other >= key, 1.0, 0.0).astype(BF16)
    from_here2 = jnp.concatenate([from_here, from_here], axis=1)

    strict = key < other

    def group(g, carries, n_tiles, first):
        kg = _group_keys(k_ref, g, n_tiles, KEY_GROUP_ODD)
        new = ()
        scores = [_tile_scores(_pair_lanes(kg, h), qh[h], per_tile=False) for h in range(HEADS_PER_STEP)]
        for h in range(HEADS_PER_STEP):
            z_tiles = scores[h]
            keep_tiles = []
            for z in z_tiles:
                neg_z = -z
                keep_tiles.append(jnp.minimum(neg_z, 0.0) - jnp.log(1.0 + jnp.exp(jnp.minimum(z, neg_z))))
            if first:
                keep_tiles[-1] = jnp.where(strict, keep_tiles[-1], 0.0)
            run = None if first else carries[h]
            a_tiles = [None] * n_tiles
            for c in reversed(range(n_tiles)):
                tail = _dot(from_here2, jnp.concatenate(_split_bf16(keep_tiles[c]), axis=0))
                tile_total = tail[0:1, :]
                if run is not None:
                    tail = tail + run
                a_tiles[c] = jnp.exp(z_tiles[c] + tail)
                run = tile_total if run is None else run + tile_total
            if first:
                a_tiles[-1] = jnp.where(strict, a_tiles[-1], 0.0)
            pv = _dot(_head_rows(vt_ref, g, h, n_tiles), jnp.concatenate(a_tiles, axis=0).astype(BF16))
            acc_ref[h] = pv if first else acc_ref[h] + pv
            new += (run,)
        return new

    _sweep(qi, group, KEY_GROUP_ODD)
    _store_step_output(o_ref, [acc_ref[h] for h in range(HEADS_PER_STEP)])


def _stick_attn(sb, vt, batch, seq):
    t = ATT_TILE
    n_steps = N_HEADS_D // HEADS_PER_STEP
    return pl.pallas_call(
        _stick_attn_kernel,
        grid=(batch, n_steps, seq // t),
        in_specs=[pl.BlockSpec((1, t, STEP_LANES), lambda b, p, i: (b, i, p)),
                  _resident((1, seq, STEP_LANES), lambda b, p, i: (b, 0, n_steps + p)),
                  _resident((1, PAIRS_PER_STEP) + vt.shape[2:], lambda b, p, i: (b, p, 0, 0, 0))],
        out_specs=pl.BlockSpec((1, t, STEP_LANES), lambda b, p, i: (b, i, p)),
        out_shape=jax.ShapeDtypeStruct((batch, seq, N_HEADS_D * HEAD_DIM), BF16),
        scratch_shapes=[pltpu.VMEM((HEADS_PER_STEP, HEAD_DIM, t), F32)],
        compiler_params=_params(3),
        name="stick_attn",
    )(sb, sb, vt)


def _post_attn_kernel(oa_ref, ob_ref, oah_ref, obh_ref, x_ref, xh_ref, mod_ref, gmix_ref, gpre_ref, gpost_ref,
                      wo_ref, wup_ref, cw_ref, cb_ref, wd_ref, o_ref, x1_ref, h_ref, u_ref, acc_ref, *, per_seq):
    d = D_MODEL
    tm = ROW_TILE
    halo = CONV_HALO
    cw = FF_CHUNK
    i = pl.program_id(0)
    mod = mod_ref[0]
    gate_m, shift, scale, gate_f = mod[:, 2 * d:3 * d], mod[:, 3 * d:4 * d], mod[:, 4 * d:5 * d], mod[:, 5 * d:6 * d]
    n_first = oa_ref.shape[1]

    def mixed(oa, ob, x):
        y = _dot(oa, wo_ref[0:n_first, :]) + _dot(ob, wo_ref[n_first:2 * n_first, :])
        return x + gate_m * _rms(y, gmix_ref[...])

    g = gpre_ref[...]
    ahead = _prenorm(mixed(oah_ref[...], obh_ref[...], xh_ref[...]), g, scale, shift)
    ahead = jnp.where(i % per_seq == 0, 0.0, ahead)
    h_ref[0:halo, :] = ahead.astype(BF16)
    x1_ref[...] = mixed(oa_ref[...], ob_ref[...], x_ref[...])
    h_ref[halo:halo + tm, :] = _prenorm(x1_ref[...], g, scale, shift).astype(BF16)

    def up(ch):
        slot = ch % 2
        for half in range(2):
            cols = slice(half * D_FF + ch * cw, half * D_FF + (ch + 1) * cw)
            u_ref[slot, half] = _dot(h_ref[...], wup_ref[:, cols])

    def conv(slot, half, ch):
        cols = slice(half * D_FF + ch * cw, half * D_FF + (ch + 1) * cw)
        w = cw_ref[:, cols]
        out = w[0:1, :] * u_ref[slot, half, halo - 2:halo - 2 + tm, :]
        out = out + w[1:2, :] * u_ref[slot, half, halo - 1:halo - 1 + tm, :]
        out = out + w[2:3, :] * u_ref[slot, half, halo:halo + tm, :]
        return out + cb_ref[:, cols]

    def down(ch, act):
        part = _dot(act, wd_ref[ch * cw:(ch + 1) * cw, :])
        if ch == 0:
            acc_ref[...] = part
        else:
            acc_ref[...] += part

    up(0)
    act = None
    for ch in range(N_FF_CHUNKS):
        if ch + 1 < N_FF_CHUNKS:
            up(ch + 1)
        if act is not None:
            down(ch - 1, act)
        slot = ch % 2
        act = (jax.nn.gelu(conv(slot, 0, ch), approximate=True) * conv(slot, 1, ch)).astype(BF16)
    down(N_FF_CHUNKS - 1, act)

    o_ref[...] = x1_ref[...] + gate_f * _rms(acc_ref[...], gpost_ref[...])


def _post_attn(oa, ob, x, mod, g_mix, g_pre, g_post, w_out, w_up, conv_w, conv_b, w_down, seq):
    rows, d = x.shape
    tm = ROW_TILE
    halo = CONV_HALO
    per_seq = seq // tm
    n_first = oa.shape[1]
    const = lambda i: (0, 0)
    tile = lambda i: (i, 0)
    ahead = lambda i: (jnp.maximum(i * (tm // halo) - 1, 0), 0)
    once = dict(pipeline_mode=pl.Buffered(1))
    return pl.pallas_call(
        functools.partial(_post_attn_kernel, per_seq=per_seq),
        grid=(rows // tm,),
        in_specs=[pl.BlockSpec((tm, n_first), tile), pl.BlockSpec((tm, n_first), tile),
                  pl.BlockSpec((halo, n_first), ahead), pl.BlockSpec((halo, n_first), ahead),
                  pl.BlockSpec((tm, d), tile), pl.BlockSpec((halo, d), ahead),
                  pl.BlockSpec((1, 1, 6 * d), lambda i: (i // per_seq, 0, 0)),
                  pl.BlockSpec((1, d), const), pl.BlockSpec((1, d), const), pl.BlockSpec((1, d), const),
                  pl.BlockSpec(w_out.shape, const, **once),
                  pl.BlockSpec(w_up.shape, const, **once),
                  pl.BlockSpec(conv_w.shape, const),
                  pl.BlockSpec(conv_b.shape, const),
                  pl.BlockSpec(w_down.shape, const, **once)],
        out_specs=pl.BlockSpec((tm, d), tile),
        out_shape=jax.ShapeDtypeStruct((rows, d), F32),
        scratch_shapes=[pltpu.VMEM((tm, d), F32),
                        pltpu.VMEM((tm + halo, d), BF16),
                        pltpu.VMEM((2, 2, tm + halo, FF_CHUNK), F32),
                        pltpu.VMEM((tm, d), F32)],
        compiler_params=_params(1),
        name="post_attn",
    )(oa, ob, oa, ob, x, x, mod, g_mix, g_pre, g_post, w_out, w_up, conv_w, conv_b, w_down)


def _rotate_half_cols(w):
    half = w.shape[-1] // 2
    return jnp.concatenate([-w[..., half:], w[..., :half]], axis=-1)


def _pad_cols(w, left, total):
    return jnp.pad(w, ((0, 0), (left, total - left - w.shape[1])))


def _rope_tables(seq):
    inv_freq = 1.0 / (ROPE_THETA ** (jnp.arange(0, MLA_ROPE_DIM, 2, dtype=F32) / MLA_ROPE_DIM))
    ang = jnp.arange(seq, dtype=F32)[:, None] * inv_freq[None, :]
    cos, sin = jnp.cos(ang), jnp.sin(ang)
    cos2 = _pad_cols(jnp.concatenate([cos, cos], axis=1), MLA_NOPE_DIM, LANES)
    sin2 = _pad_cols(jnp.concatenate([sin, sin], axis=1), MLA_NOPE_DIM, LANES)
    scale = (MLA_NOPE_DIM + MLA_ROPE_DIM) ** -0.5
    nope_ones = _pad_cols(jnp.ones((seq, MLA_NOPE_DIM), F32), 0, LANES)
    scale = scale * LOG2_E
    return (scale * (cos2 + nope_ones), scale * sin2, cos2, sin2)


def _odd_weights(w_in, w_uq, w_ukv):
    o = MLA_Q_RANK + MLA_KV_RANK
    w_rope = w_in[:, o:o + MLA_ROPE_DIM]
    scale_d = HEAD_DIM ** -0.5
    wd = N_HEADS_D * HEAD_DIM
    sb0 = o + MLA_ROPE_DIM
    w_in2 = jnp.concatenate([
        w_in[:, :o],
        _pad_cols(w_rope, MLA_NOPE_DIM, LANES),
        _pad_cols(_rotate_half_cols(w_rope), MLA_NOPE_DIM, LANES),
        w_in[:, sb0:sb0 + wd] * scale_d,
        w_in[:, sb0 + wd:],
    ], axis=1).astype(BF16)
    qd = MLA_NOPE_DIM + MLA_ROPE_DIM
    uq = w_uq.reshape(MLA_Q_RANK, N_HEADS_C, qd)
    plain = jnp.pad(uq, ((0, 0), (0, 0), (0, LANES - qd)))
    swapped = jnp.pad(_rotate_half_cols(uq[..., MLA_NOPE_DIM:]),
                      ((0, 0), (0, 0), (MLA_NOPE_DIM, LANES - qd)))
    wq = jnp.concatenate([plain.reshape(MLA_Q_RANK, -1), swapped.reshape(MLA_Q_RANK, -1)], axis=1).astype(BF16)
    ukv = w_ukv.reshape(MLA_KV_RANK, N_HEADS_C, MLA_NOPE_DIM + MLA_V_DIM)
    k_nope = jnp.pad(ukv[..., :MLA_NOPE_DIM], ((0, 0), (0, 0), (0, LANES - MLA_NOPE_DIM)))
    v = ukv[..., MLA_NOPE_DIM:]
    wkv = jnp.concatenate([k_nope.reshape(MLA_KV_RANK, -1), v.reshape(MLA_KV_RANK, -1)], axis=1).astype(BF16)
    return w_in2, wq, wkv


def _even_weights(w_in):
    wa = N_HEADS_A * HEAD_DIM
    scale = HEAD_DIM ** -0.5 * LOG2_E
    col = jnp.arange(w_in.shape[1])
    is_q = (col < wa) | ((col >= 3 * wa) & (col < 4 * wa))
    return (w_in * jnp.where(is_q, scale, 1.0)).astype(BF16)


def kernel(x, c, rel_bias, ada_w, ada_b, mix_pre_g, mix_post_g, ffn_pre_g, ffn_post_g, ab_w_in, ab_w_out,
           cd_w_in, mla_q_norm_g, mla_kv_norm_g, mla_w_uq, mla_w_ukv, cd_w_out, ffn_w_up, ffn_conv_w,
           ffn_conv_b, ffn_w_down):
    batch, seq, d = x.shape
    widest_group = max(KEY_GROUP_EVEN, KEY_GROUP_ODD) * ATT_TILE
    assert d == D_MODEL and seq % widest_group == 0 and (KEY_GROUP_EVEN * ATT_TILE) % ROW_TILE == 0
    assert seq == DIL_CLASSES * ATT_TILE
    rows = batch * seq
    xf = x.reshape(rows, d)

    mods = _mods(c, ada_w, ada_b)
    bias_a = _bias_tiles(rel_bias, 0, N_HEADS_A, MOBA_BIAS_TILES, 1, None, "moba_bias_tiles")
    bias_near = _bias_tiles(rel_bias, N_HEADS_A, N_HEADS_B, DIL_NEAR_TILES, 1, DIL_NEAR_BRANCHES,
                            "dilated_near_bias_tiles")
    bias_far = _bias_tiles(rel_bias, N_HEADS_A, N_HEADS_B, 1, DIL_CLASSES, DIL_FAR_BRANCHES,
                           "dilated_far_bias_tiles")
    tables = _rope_tables(seq)

    for layer in range(DEPTH):
        mod = mods[layer].reshape(batch, 1, 6 * d)
        i = layer // 2
        if layer % 2 == 0:
            proj, km, vta, vtb, qkv_cls = _even_proj(xf, mod, mix_pre_g[layer].reshape(1, d),
                                                     _even_weights(ab_w_in[i]), seq)
            proj = proj.reshape(batch, seq, -1)
            km = km.reshape(batch, seq // MOBA_BLOCK, -1)
            o_first = _moba_attn(proj, vta, bias_a, km, batch, seq)
            o_second = _dilated_attn(proj, qkv_cls, vtb, bias_near, bias_far, batch, seq,
                                     3 * N_HEADS_A * HEAD_DIM)
            w_out = ab_w_out[i]
        else:
            w_in2, wq, wkv = _odd_weights(cd_w_in[i], mla_w_uq[i], mla_w_ukv[i])
            qm, km, vtm, sb, vts = _odd_proj(xf, mod, mix_pre_g[layer].reshape(1, d), w_in2,
                                             mla_q_norm_g[i].reshape(1, -1), mla_kv_norm_g[i].reshape(1, -1),
                                             wq, wkv, tables, seq)
            o_first = _mla_attn(qm.reshape(batch, seq, -1), km.reshape(batch, seq, -1), vtm, batch, seq)
            o_second = _stick_attn(sb.reshape(batch, seq, -1), vts, batch, seq)
            w_out = cd_w_out[i]
        xf = _post_attn(o_first.reshape(rows, -1), o_second.reshape(rows, -1), xf, mod,
                        mix_post_g[layer].reshape(1, d), ffn_pre_g[layer].reshape(1, d),
                        ffn_post_g[layer].reshape(1, d), w_out.astype(BF16), ffn_w_up[layer].astype(BF16),
                        ffn_conv_w[layer], ffn_conv_b[layer].reshape(1, -1), ffn_w_down[layer].astype(BF16), seq)
    return xf.reshape(batch, seq, d)
```

```python
import functools
import math

import jax
import jax.numpy as jnp
from jax import lax
from jax.experimental import pallas as pl
from jax.experimental.pallas import tpu as pltpu

F32 = jnp.float32
BF16 = jnp.bfloat16

D_MODEL = 1024
DEPTH = 4
HEAD_DIM = 64
N_HEADS_A = 8
N_HEADS_B = 8
N_HEADS_C = 8
N_HEADS_D = 8
MOBA_BLOCK = 256
MOBA_TOPK = 3
DILATED_BRANCHES = ((128, 1), (512, 4), (2048, 16))
MLA_Q_RANK = 256
MLA_KV_RANK = 256
MLA_NOPE_DIM = 64
MLA_ROPE_DIM = 32
MLA_V_DIM = 64
ROPE_THETA = 10000.0
REL_BUCKETS = 32
REL_MAX_DIST = 2048
D_FF = 2816
CONV_WIDTH = 3
NORM_EPS = 1e-6
LOG2_E = math.log2(math.e)

LANES = 128
SUBLANES = 8
BF16_SUBLANES = 16
VMEM_LIMIT_BYTES = 56 * 1024 * 1024

ATT_TILE = MOBA_BLOCK
KEY_GROUP_EVEN = 4
KEY_GROUP_ODD = 8
PAIRS_PER_STEP = 2
HEADS_PER_STEP = 2 * PAIRS_PER_STEP
STEP_LANES = PAIRS_PER_STEP * LANES
ROW_TILE = 512
FF_CHUNK = 256
N_FF_CHUNKS = D_FF // FF_CHUNK
CONV_HALO = BF16_SUBLANES
MOBA_BIAS_TILES = REL_MAX_DIST // ATT_TILE + 2
DIL_SPLIT = DILATED_BRANCHES[1][0]
DIL_CLASSES = DILATED_BRANCHES[2][1]
DIL_NEAR_TILES = DIL_SPLIT // ATT_TILE + 1
DIL_FAR_LANES = 5 * LANES
assert DIL_FAR_LANES >= N_HEADS_B * (HEAD_DIM + 2)
DIL_NEAR_BRANCHES = tuple((-1, min(window, DIL_SPLIT), dil) for window, dil in DILATED_BRANCHES)
DIL_FAR_BRANCHES = ((DIL_SPLIT, DILATED_BRANCHES[2][0], DIL_CLASSES),)
assert all(window <= DIL_SPLIT for window, _ in DILATED_BRANCHES[:2]) and DIL_SPLIT % DIL_CLASSES == 0

_NT = (((1,), (1,)), ((), ()))


def _bucket_lower_bounds():
    max_exact = REL_BUCKETS // 2
    ratio = REL_MAX_DIST // max_exact
    n_log = REL_BUCKETS - max_exact
    lows = list(range(max_exact + 1))
    for k in range(1, n_log):
        d = lows[-1]
        while d ** n_log < (max_exact ** n_log) * (ratio ** k):
            d += 1
        lows.append(d)
    return lows


_BUCKET_LOW = _bucket_lower_bounds()


def _dot(a, b):
    return jnp.dot(a, b, preferred_element_type=F32)


def _dot_nt(a, b):
    return lax.dot_general(a, b, _NT, preferred_element_type=F32)


def _split_bf16(x):
    hi = x.astype(BF16)
    lo = (x - hi.astype(F32)).astype(BF16)
    return hi, lo


def _rms(x, g):
    return (x * lax.rsqrt(jnp.mean(x * x, axis=-1, keepdims=True) + NORM_EPS)) * g


def _prenorm(x, g, scale, shift):
    return _rms(x, g) * (1.0 + scale) + shift


def _resident(block_shape, index_map):
    return pl.BlockSpec(block_shape, index_map)


def _params(n_grid_dims):
    return pltpu.CompilerParams(dimension_semantics=("arbitrary",) * n_grid_dims,
                                vmem_limit_bytes=VMEM_LIMIT_BYTES)


def _mods_kernel(c_ref, w_ref, b_ref, o_ref):
    c = c_ref[...]
    cond = c * jax.nn.sigmoid(c)
    c_hi, c_lo = _split_bf16(cond)
    w_hi, w_lo = _split_bf16(w_ref[0])
    o_ref[0] = _dot(c_hi, w_hi) + _dot(c_hi, w_lo) + _dot(c_lo, w_hi) + b_ref[0]


def _mods(c, ada_w, ada_b):
    b, d = c.shape
    rows = BF16_SUBLANES
    n_out = ada_w.shape[-1]
    tn = n_out // 4
    c_pad = jnp.zeros((rows, d), F32).at[:b].set(c)
    out = pl.pallas_call(
        _mods_kernel,
        grid=(DEPTH, n_out // tn),
        in_specs=[pl.BlockSpec((rows, d), lambda l, j: (0, 0)),
                  pl.BlockSpec((1, d, tn), lambda l, j: (l, 0, j)),
                  pl.BlockSpec((1, 1, tn), lambda l, j: (l, 0, j))],
        out_specs=pl.BlockSpec((1, rows, tn), lambda l, j: (l, 0, j)),
        out_shape=jax.ShapeDtypeStruct((DEPTH, rows, n_out), F32),
        compiler_params=_params(2),
        name="ada_mods",
    )(c_pad, ada_w, ada_b.reshape(DEPTH, 1, n_out))
    return out[:, :b]


def _bias_tiles_kernel(tab_ref, o_ref, *, head_off, step, branches):
    h = pl.program_id(0) + head_off
    d = pl.program_id(1)
    t = ATT_TILE
    dist = (d * t + lax.broadcasted_iota(jnp.int32, (SUBLANES, 2 * t), 1) - t) * step
    val = jnp.full(dist.shape, tab_ref[h, 0], F32)
    for b in range(1, REL_BUCKETS):
        val = jnp.where(dist >= _BUCKET_LOW[b], tab_ref[h, b], val)
    if branches is not None:
        mult = jnp.zeros(dist.shape, F32)
        for beyond, window, dil in branches:
            hit = jnp.where(dist <= window, jnp.where((dist & (dil - 1)) == 0, 1.0, 0.0), 0.0)
            mult = mult + jnp.where(dist > beyond, hit, 0.0)
        log_mult = jnp.where(mult > 2.5, math.log(3.0), jnp.where(mult > 1.5, math.log(2.0), 0.0))
        val = jnp.where(mult > 0.5, val + log_mult, -jnp.inf)
    val = jnp.where(dist >= 0, val * LOG2_E, -jnp.inf)
    strip = jnp.concatenate([val] * (t // SUBLANES), axis=0)
    rotated = pltpu.roll(strip, 0, 1, stride=1, stride_axis=0)
    o_ref[0, 0] = rotated[:, t:2 * t]


def _bias_tiles(rel_bias, head_off, n_heads, n_tiles, step, branches, name):
    t = ATT_TILE
    return pl.pallas_call(
        functools.partial(_bias_tiles_kernel, head_off=head_off, step=step, branches=branches),
        grid=(n_heads, n_tiles),
        in_specs=[pl.BlockSpec(memory_space=pltpu.SMEM)],
        out_specs=pl.BlockSpec((1, 1, t, t), lambda h, d: (h, d, 0, 0)),
        out_shape=jax.ShapeDtypeStruct((n_heads, n_tiles, t, t), F32),
        compiler_params=_params(2),
        name=name,
    )(rel_bias)


def _store_vt(vt_ref, v):
    vt = v.T.astype(BF16)
    for p in range(vt.shape[0] // LANES):
        vt_ref[0, p, 0] = vt[p * LANES:(p + 1) * LANES, :]


def _vt_spec(tm, per_seq, n_pairs, key_group):
    per_group = key_group * ATT_TILE // tm
    return pl.BlockSpec((1, n_pairs, 1, LANES, tm),
                        lambda i: (i // per_seq, 0, (i % per_seq) // per_group, 0, (i % per_seq) % per_group))


def _vt_shape(batch, seq, n_pairs, key_group):
    group_keys = key_group * ATT_TILE
    return jax.ShapeDtypeStruct((batch, n_pairs, seq // group_keys, LANES, group_keys), BF16)


def _even_proj_kernel(x_ref, mod_ref, g_ref, w_ref, o_ref, km_ref, vta_ref, vtb_ref, cls_ref):
    d = D_MODEL
    mod = mod_ref[0]
    h = _prenorm(x_ref[...], g_ref[...], mod[:, d:2 * d], mod[:, 0:d])
    p = _dot(h.astype(BF16), w_ref[...])
    p_bf16 = p.astype(BF16)
    o_ref[...] = p_bf16
    tm = p.shape[0]
    per_class = tm // DIL_CLASSES
    out_row = lax.broadcasted_iota(jnp.int32, (tm, tm), 0)
    in_row = lax.broadcasted_iota(jnp.int32, (tm, tm), 1)
    source = (out_row % per_class) * DIL_CLASSES + out_row // per_class
    permute = jnp.where(in_row == source, 1.0, 0.0).astype(BF16)
    by_class = _dot(permute, p_bf16[:, 3 * N_HEADS_A * HEAD_DIM:]).astype(BF16)
    for cls in range(DIL_CLASSES):
        cls_ref[0, cls] = by_class[cls * per_class:(cls + 1) * per_class]
    wa = N_HEADS_A * HEAD_DIM
    ka = p[:, wa:2 * wa]
    nb = ka.shape[0] // MOBA_BLOCK
    km_ref[0] = jnp.mean(ka.reshape(nb, MOBA_BLOCK, wa), axis=1)
    _store_vt(vta_ref, p[:, 2 * wa:3 * wa])
    vtb = p[:, 5 * wa:6 * wa].T.astype(BF16)
    t = ATT_TILE
    for pair in range(vtb.shape[0] // LANES):
        for tile in range(vtb.shape[1] // t):
            vtb_ref[0, pair, tile] = vtb[pair * LANES:(pair + 1) * LANES, tile * t:(tile + 1) * t]


def _even_proj(x, mod, g, w_in, seq):
    rows, d = x.shape
    tm = ROW_TILE
    n = w_in.shape[1]
    wa = N_HEADS_A * HEAD_DIM
    per_seq = seq // tm
    n_pairs = N_HEADS_A // 2
    vt_spec = _vt_spec(tm, per_seq, n_pairs, KEY_GROUP_EVEN)
    vt_shape = _vt_shape(rows // seq, seq, n_pairs, KEY_GROUP_EVEN)
    tiles_per_step = tm // ATT_TILE
    vt_tile_spec = pl.BlockSpec((1, n_pairs, tiles_per_step, LANES, ATT_TILE),
                                lambda i: (i // per_seq, 0, i % per_seq, 0, 0))
    return pl.pallas_call(
        _even_proj_kernel,
        grid=(rows // tm,),
        in_specs=[pl.BlockSpec((tm, d), lambda i: (i, 0)),
                  pl.BlockSpec((1, 1, 6 * d), lambda i: (i // per_seq, 0, 0)),
                  pl.BlockSpec((1, d), lambda i: (0, 0)),
                  pl.BlockSpec((d, n), lambda i: (0, 0))],
        out_specs=[pl.BlockSpec((tm, n), lambda i: (i, 0)),
                   pl.BlockSpec((1, tm // MOBA_BLOCK, wa), lambda i: (i, 0, 0)),
                   vt_spec, vt_tile_spec,
                   pl.BlockSpec((1, DIL_CLASSES, tm // DIL_CLASSES, n - 3 * wa),
                                lambda i: (i // per_seq, 0, i % per_seq, 0))],
        out_shape=[jax.ShapeDtypeStruct((rows, n), BF16),
                   jax.ShapeDtypeStruct((rows // tm, tm // MOBA_BLOCK, wa), F32),
                   vt_shape, _vt_shape(rows // seq, seq, n_pairs, 1),
                   jax.ShapeDtypeStruct((rows // seq, DIL_CLASSES, seq // DIL_CLASSES, n - 3 * wa), BF16)],
        compiler_params=_params(1),
        name="even_proj",
    )(x, mod, g, w_in)


def _pair_lanes(x, h):
    pair = h // 2
    return x[:, pair * LANES:(pair + 1) * LANES]


def _split_heads(q_step):
    first = lax.broadcasted_iota(jnp.int32, (q_step.shape[0], LANES), 1) < HEAD_DIM
    heads = []
    for pair in range(q_step.shape[1] // LANES):
        q2 = q_step[:, pair * LANES:(pair + 1) * LANES]
        zero = jnp.zeros_like(q2)
        heads += [jnp.where(first, q2, zero), jnp.where(first, zero, q2)]
    return heads


def _head_rows(vt_ref, g, h, n_tiles):
    return vt_ref[0, h // 2, g, (h % 2) * HEAD_DIM:(h % 2 + 1) * HEAD_DIM, 0:n_tiles * ATT_TILE]


def _tile_scores(keys, q_head, per_tile):
    t = ATT_TILE
    n = keys.shape[0] // t
    if per_tile:
        return [_dot_nt(keys[c * t:(c + 1) * t], q_head) for c in range(n)]
    s = _dot_nt(keys, q_head)
    return [s[c * t:(c + 1) * t] for c in range(n)]


def _group_keys(k_ref, g, n_tiles, key_group):
    start = pl.multiple_of(g * key_group * ATT_TILE, key_group * ATT_TILE)
    return k_ref[0, pl.ds(start, n_tiles * ATT_TILE), :]


def _sweep(qi, group, key_group, n_far_groups=None):
    g_own = qi // key_group
    own = [functools.partial(group, g_own, None, n + 1, True) for n in range(key_group)]
    state = lax.switch(qi % key_group, own)
    n_past = g_own if n_far_groups is None else jnp.minimum(g_own, n_far_groups)
    return lax.fori_loop(0, n_past, lambda n, st: group(g_own - 1 - n, st, key_group, False), state)


def _tile_iotas():
    t = ATT_TILE
    return lax.broadcasted_iota(jnp.int32, (t, t), 0), lax.broadcasted_iota(jnp.int32, (t, t), 1)


def _store_step_output(o_ref, outs_t):
    o_ref[0] = jnp.concatenate(outs_t, axis=0).T.astype(BF16)


def _softmax_group(s_tiles, ons, vt_h, acc_ref, h, stats, first):
    maxes = []
    for s, on in zip(s_tiles, ons):
        mx = jnp.max(s, axis=0, keepdims=True)
        maxes.append(mx if on is None else jnp.where(on, mx, -jnp.inf))
    m_new = functools.reduce(jnp.maximum, maxes)
    if not first:
        m_old, l_old = stats
        m_new = jnp.maximum(m_new, m_old)
    ps = []
    l_add = None
    for s, on in zip(s_tiles, ons):
        p = jnp.exp2(s - (m_new if on is None else jnp.where(on, m_new, jnp.inf)))
        p_sum = jnp.sum(p, axis=0, keepdims=True)
        l_add = p_sum if l_add is None else l_add + p_sum
        ps.append(p.astype(BF16))
    pv = _dot(vt_h, jnp.concatenate(ps, axis=0))
    if first:
        acc_ref[h] = pv
        return m_new, l_add
    alpha = jnp.exp2(m_old - m_new)
    acc_ref[h] = alpha * acc_ref[h] + pv
    return m_new, alpha * l_old + l_add


def _moba_select(qh, km_ref, sel_ref, qi):
    t = ATT_TILE
    km_hi, km_lo = _split_bf16(km_ref[0])
    nb = km_hi.shape[0]
    blk = lax.broadcasted_iota(jnp.int32, (nb, t), 0)
    past = blk < qi
    for h in range(HEADS_PER_STEP):
        gate = _dot_nt(_pair_lanes(km_hi, h), qh[h]) + _dot_nt(_pair_lanes(km_lo, h), qh[h])
        gate = jnp.where(past, gate, -jnp.inf)
        beaten = jnp.zeros((nb, t), F32)
        for other in range(nb):
            row = gate[other:other + 1, :]
            wins = jnp.where(row > gate, 1.0, jnp.where(row == gate, jnp.where(blk > other, 1.0, 0.0), 0.0))
            beaten = beaten + wins
        sel_ref[h] = jnp.where(past, jnp.where(beaten < MOBA_TOPK, 1.0, 0.0), 0.0)


def _moba_attn_kernel(q_ref, k_ref, vt_ref, bias_ref, km_ref, o_ref, acc_ref, sel_ref):
    qi = pl.program_id(2)
    qh = _split_heads(q_ref[0])
    n_bias = bias_ref.shape[1]
    _moba_select(qh, km_ref, sel_ref, qi)

    def group(g, stats, n_tiles, first):
        kg = _group_keys(k_ref, g, n_tiles, KEY_GROUP_EVEN)
        new = ()
        scores = [_tile_scores(_pair_lanes(kg, h), qh[h], per_tile=False) for h in range(HEADS_PER_STEP)]
        for h in range(HEADS_PER_STEP):
            s_tiles = scores[h]
            ons = []
            for c in range(n_tiles):
                j = g * KEY_GROUP_EVEN + c
                dt = n_tiles - 1 - c if first else qi - j
                s_tiles[c] = s_tiles[c] + bias_ref[h, dt if first else jnp.minimum(dt, n_bias - 1)]
                ons.append(None if first and dt == 0 else sel_ref[h, pl.ds(j, 1), :] > 0.5)
            new += _softmax_group(s_tiles, ons, _head_rows(vt_ref, g, h, n_tiles), acc_ref, h,
                                  None if first else stats[2 * h:2 * h + 2], first)
        return new

    stats = _sweep(qi, group, KEY_GROUP_EVEN)
    _store_step_output(o_ref, [acc_ref[h] / stats[2 * h + 1] for h in range(HEADS_PER_STEP)])


def _moba_attn(proj, vt, bias, km, batch, seq):
    t = ATT_TILE
    n_steps = N_HEADS_A // HEADS_PER_STEP
    n_bias = bias.shape[1]
    return pl.pallas_call(
        _moba_attn_kernel,
        grid=(batch, n_steps, seq // t),
        in_specs=[pl.BlockSpec((1, t, STEP_LANES), lambda b, p, i: (b, i, p)),
                  pl.BlockSpec((1, seq, STEP_LANES), lambda b, p, i: (b, 0, n_steps + p)),
                  pl.BlockSpec((1, PAIRS_PER_STEP) + vt.shape[2:], lambda b, p, i: (b, p, 0, 0, 0)),
                  pl.BlockSpec((HEADS_PER_STEP, n_bias, t, t), lambda b, p, i: (p, 0, 0, 0)),
                  pl.BlockSpec((1, seq // MOBA_BLOCK, STEP_LANES), lambda b, p, i: (b, 0, p))],
        out_specs=pl.BlockSpec((1, t, STEP_LANES), lambda b, p, i: (b, i, p)),
        out_shape=jax.ShapeDtypeStruct((batch, seq, N_HEADS_A * HEAD_DIM), BF16),
        scratch_shapes=[pltpu.VMEM((HEADS_PER_STEP, HEAD_DIM, t), F32),
                        pltpu.VMEM((HEADS_PER_STEP, seq // MOBA_BLOCK, t), F32)],
        compiler_params=_params(3),
        name="moba_attn",
    )(proj, proj, vt, bias, km)


def _dilated_far_kernel(q_ref, k_ref, v_ref, bias_ref, o_ref):
    t = ATT_TILE
    qh = _split_heads(q_ref[0, 0])
    keys = k_ref[0, 0]
    vt = v_ref[0, 0].astype(F32).T.astype(BF16)
    accs, maxes, sums = [], [], []
    scores = [_dot_nt(keys[:, pair * LANES:(pair + 1) * LANES],
                      jnp.concatenate([qh[2 * pair], qh[2 * pair + 1]], axis=0))
              for pair in range(N_HEADS_B // 2)]
    for pair, s in enumerate(scores):
        s = s + jnp.concatenate([bias_ref[2 * pair, 0], bias_ref[2 * pair + 1, 0]], axis=1)
        m = jnp.max(s, axis=0, keepdims=True)
        p = jnp.exp2(s - jnp.where(m > -jnp.inf, m, 0.0))
        p_sum = jnp.sum(p, axis=0, keepdims=True)
        for j in range(2):
            h = 2 * pair + j
            cols = slice(j * t, (j + 1) * t)
            maxes.append(m[:, cols])
            sums.append(p_sum[:, cols])
            accs.append(_dot(vt[h * HEAD_DIM:(h + 1) * HEAD_DIM], p[:, cols].astype(BF16)))
    pad = jnp.zeros((DIL_FAR_LANES - N_HEADS_B * (HEAD_DIM + 2), t), F32)
    o_ref[0, 0] = jnp.concatenate(accs + maxes + sums + [pad], axis=0).T


def _dilated_far(qkv_cls, bias, batch):
    t = ATT_TILE
    wb = N_HEADS_B * HEAD_DIM
    cols = lambda group: pl.BlockSpec((1, 1, t, wb), lambda b, c: (b, c, 0, group))
    return pl.pallas_call(
        _dilated_far_kernel,
        grid=(batch, DIL_CLASSES),
        in_specs=[cols(0), cols(1), cols(2),
                  pl.BlockSpec((N_HEADS_B, 1, t, t), lambda b, c: (0, 0, 0, 0))],
        out_specs=pl.BlockSpec((1, 1, t, DIL_FAR_LANES), lambda b, c: (b, c, 0, 0)),
        out_shape=jax.ShapeDtypeStruct((batch, DIL_CLASSES, t, DIL_FAR_LANES), F32),
        compiler_params=_params(2),
        name="dilated_far",
    )(qkv_cls, qkv_cls, qkv_cls, bias)


def _dilated_near_kernel(q_ref, k_ref, vt_ref, bias_ref, far_ref, o_ref, acc_ref):
    t = ATT_TILE
    qi = pl.program_id(1)
    qh = _split_heads(q_ref[0])

    def tiles(n_tiles):
        first_tile = qi - (n_tiles - 1)
        keys = k_ref[0, pl.ds(pl.multiple_of(first_tile * t, t), n_tiles * t), :]
        stats = ()
        scores = [_dot_nt(keys[:, pair * LANES:(pair + 1) * LANES],
                          jnp.concatenate([qh[2 * pair], qh[2 * pair + 1]], axis=0))
                  for pair in range(N_HEADS_B // 2)]
        for h in range(N_HEADS_B):
            s = scores[h // 2][:, (h % 2) * t:(h % 2 + 1) * t]
            s_tiles = [s[c * t:(c + 1) * t] + bias_ref[h, n_tiles - 1 - c] for c in range(n_tiles)]
            rows = slice((h % 2) * HEAD_DIM, (h % 2 + 1) * HEAD_DIM)
            vt_h = jnp.concatenate([vt_ref[0, h // 2, first_tile + c, rows, :] for c in range(n_tiles)], axis=1)
            stats += _softmax_group(s_tiles, [None] * n_tiles, vt_h, acc_ref, h, None, True)
        return stats

    stats = lax.switch(jnp.minimum(qi, DIL_NEAR_TILES - 1),
                       [functools.partial(tiles, n + 1) for n in range(DIL_NEAR_TILES)])
    far = far_ref[0].T
    far_m0, far_l0 = N_HEADS_B * HEAD_DIM, N_HEADS_B * HEAD_DIM + N_HEADS_B
    outs = []
    for h in range(N_HEADS_B):
        m_near, l_near = stats[2 * h], stats[2 * h + 1]
        m_far, l_far = far[far_m0 + h:far_m0 + h + 1], far[far_l0 + h:far_l0 + h + 1]
        m = jnp.maximum(m_near, m_far)
        w_near, w_far = jnp.exp2(m_near - m), jnp.exp2(m_far - m)
        acc_far = far[h * HEAD_DIM:(h + 1) * HEAD_DIM]
        outs.append((w_near * acc_ref[h] + w_far * acc_far) / (w_near * l_near + w_far * l_far))
    _store_step_output(o_ref, outs)


def _dilated_near(proj, vt, bias, far, batch, seq, lane0):
    t = ATT_TILE
    wb = N_HEADS_B * HEAD_DIM
    first = lane0 // wb
    return pl.pallas_call(
        _dilated_near_kernel,
        grid=(batch, seq // t),
        in_specs=[pl.BlockSpec((1, t, wb), lambda b, i: (b, i, first)),
                  pl.BlockSpec((1, seq, wb), lambda b, i: (b, 0, first + 1)),
                  pl.BlockSpec((1,) + vt.shape[1:], lambda b, i: (b, 0, 0, 0, 0)),
                  pl.BlockSpec((N_HEADS_B, DIL_NEAR_TILES, t, t), lambda b, i: (0, 0, 0, 0)),
                  pl.BlockSpec((1, t, DIL_FAR_LANES), lambda b, i: (b, i, 0))],
        out_specs=pl.BlockSpec((1, t, wb), lambda b, i: (b, i, 0)),
        out_shape=jax.ShapeDtypeStruct((batch, seq, wb), BF16),
        scratch_shapes=[pltpu.VMEM((N_HEADS_B, HEAD_DIM, t), F32)],
        compiler_params=_params(2),
        name="dilated_near",
    )(proj, proj, vt, bias, far)


def _dilated_attn(proj, qkv_cls, vt_tiles, bias_near, bias_far, batch, seq, lane0):
    far = _dilated_far(qkv_cls, bias_far, batch)
    far = far.transpose(0, 2, 1, 3).reshape(batch, seq, DIL_FAR_LANES)
    return _dilated_near(proj, vt_tiles, bias_near, far, batch, seq, lane0)


def _odd_proj_kernel(x_ref, mod_ref, g_ref, win_ref, gq_ref, gkv_ref, wq_ref, wkv_ref,
                     cq_ref, sq_ref, ck_ref, sk_ref, qm_ref, km_ref, vtm_ref, sb_ref, vts_ref):
    d = D_MODEL
    mod = mod_ref[0]
    h = _prenorm(x_ref[...], g_ref[...], mod[:, d:2 * d], mod[:, 0:d])
    p = _dot(h.astype(BF16), win_ref[...])
    o = MLA_Q_RANK + MLA_KV_RANK
    c_q, c_kv = p[:, 0:MLA_Q_RANK], p[:, MLA_Q_RANK:o]
    k_rope, k_rope_swapped = p[:, o:o + LANES], p[:, o + LANES:o + 2 * LANES]
    sb0 = o + 2 * LANES
    n_qk = 2 * N_HEADS_D * HEAD_DIM
    sb_ref[...] = p[:, sb0:sb0 + n_qk].astype(BF16)
    _store_vt(vts_ref, p[:, sb0 + n_qk:])
    q12 = _dot(_rms(c_q, gq_ref[...]).astype(BF16), wq_ref[...])
    kv = _dot(_rms(c_kv, gkv_ref[...]).astype(BF16), wkv_ref[...])
    k_pe = k_rope * ck_ref[...] + k_rope_swapped * sk_ref[...]
    cq, sq = cq_ref[...], sq_ref[...]
    half = N_HEADS_C * LANES
    for hh in range(N_HEADS_C):
        cols = slice(hh * LANES, (hh + 1) * LANES)
        swapped = slice(half + hh * LANES, half + (hh + 1) * LANES)
        qm_ref[:, cols] = (q12[:, cols] * cq + q12[:, swapped] * sq).astype(BF16)
        km_ref[:, cols] = (kv[:, cols] + k_pe).astype(BF16)
    _store_vt(vtm_ref, kv[:, half:])


def _odd_proj(x, mod, g, w_in, gq, gkv, wq, wkv, tables, seq):
    rows, d = x.shape
    tm = ROW_TILE
    per_seq = seq // tm
    n_in = w_in.shape[1]
    n_sb = 2 * N_HEADS_D * HEAD_DIM
    n_q = N_HEADS_C * LANES
    n_pairs = N_HEADS_C // 2
    const = lambda i: (0, 0)
    table_spec = pl.BlockSpec((tm, LANES), lambda i: (i % per_seq, 0))
    vt_spec = _vt_spec(tm, per_seq, n_pairs, KEY_GROUP_ODD)
    vt_shape = _vt_shape(rows // seq, seq, n_pairs, KEY_GROUP_ODD)
    return pl.pallas_call(
        _odd_proj_kernel,
        grid=(rows // tm,),
        in_specs=[pl.BlockSpec((tm, d), lambda i: (i, 0)),
                  pl.BlockSpec((1, 1, 6 * d), lambda i: (i // per_seq, 0, 0)),
                  pl.BlockSpec((1, d), const),
                  pl.BlockSpec((d, n_in), const),
                  pl.BlockSpec((1, MLA_Q_RANK), const),
                  pl.BlockSpec((1, MLA_KV_RANK), const),
                  pl.BlockSpec(wq.shape, const),
                  pl.BlockSpec(wkv.shape, const),
                  table_spec, table_spec, table_spec, table_spec],
        out_specs=[pl.BlockSpec((tm, n_q), lambda i: (i, 0)),
                   pl.BlockSpec((tm, n_q), lambda i: (i, 0)),
                   vt_spec,
                   pl.BlockSpec((tm, n_sb), lambda i: (i, 0)),
                   vt_spec],
        out_shape=[jax.ShapeDtypeStruct((rows, n_q), BF16),
                   jax.ShapeDtypeStruct((rows, n_q), BF16),
                   vt_shape,
                   jax.ShapeDtypeStruct((rows, n_sb), BF16),
                   vt_shape],
        compiler_params=_params(1),
        name="odd_proj",
    )(x, mod, g, w_in, gq, gkv, wq, wkv, *tables)


def _mla_attn_kernel(q_ref, k_ref, vt_ref, o_ref, acc_ref):
    qi = pl.program_id(2)
    q_step = q_ref[0]
    qh = [q_step[:, h * LANES:(h + 1) * LANES] for h in range(HEADS_PER_STEP)]

    def group(g, stats, n_tiles, first):
        kg = _group_keys(k_ref, g, n_tiles, KEY_GROUP_ODD)
        new = ()
        scores = [_tile_scores(kg[:, h * LANES:(h + 1) * LANES], qh[h], per_tile=True)
                  for h in range(HEADS_PER_STEP)]
        for h in range(HEADS_PER_STEP):
            s_tiles = scores[h]
            if first:
                key, query = _tile_iotas()
                s_tiles[-1] = jnp.where(key <= query, s_tiles[-1], -jnp.inf)
            new += _softmax_group(s_tiles, [None] * n_tiles, _head_rows(vt_ref, g, h, n_tiles), acc_ref, h,
                                  None if first else stats[2 * h:2 * h + 2], first)
        return new

    stats = _sweep(qi, group, KEY_GROUP_ODD)
    _store_step_output(o_ref, [acc_ref[h] / stats[2 * h + 1] for h in range(HEADS_PER_STEP)])


def _mla_attn(qm, km, vt, batch, seq):
    t = ATT_TILE
    n_steps = N_HEADS_C // HEADS_PER_STEP
    qk_lanes = HEADS_PER_STEP * LANES
    return pl.pallas_call(
        _mla_attn_kernel,
        grid=(batch, n_steps, seq // t),
        in_specs=[pl.BlockSpec((1, t, qk_lanes), lambda b, p, i: (b, i, p)),
                  _resident((1, seq, qk_lanes), lambda b, p, i: (b, 0, p)),
                  _resident((1, PAIRS_PER_STEP) + vt.shape[2:], lambda b, p, i: (b, p, 0, 0, 0))],
        out_specs=pl.BlockSpec((1, t, STEP_LANES), lambda b, p, i: (b, i, p)),
        out_shape=jax.ShapeDtypeStruct((batch, seq, N_HEADS_C * MLA_V_DIM), BF16),
        scratch_shapes=[pltpu.VMEM((HEADS_PER_STEP, HEAD_DIM, t), F32)],
        compiler_params=_params(3),
        name="mla_attn",
    )(qm, km, vt)


def _stick_attn_kernel(q_ref, k_ref, vt_ref, o_ref, acc_ref):
    t = ATT_TILE
    qi = pl.program_id(2)
    qh = _split_heads(q_ref[0])
    key = lax.broadcasted_iota(jnp.int32, (t, t), 0)
    other = lax.broadcasted_iota(jnp.int32, (t, t), 1)
    from_here = jnp.where(other >= key, 1.0, 0.0).astype(BF16)
    from_here2 = jnp.concatenate([from_here, from_here], axis=1)

    strict = key < other

    def group(g, carries, n_tiles, first):
        kg = _group_keys(k_ref, g, n_tiles, KEY_GROUP_ODD)
        new = ()
        scores = [_tile_scores(_pair_lanes(kg, h), qh[h], per_tile=False) for h in range(HEADS_PER_STEP)]
        for h in range(HEADS_PER_STEP):
            z_tiles = scores[h]
            keep_tiles = []
            for z in z_tiles:
                neg_z = -z
                keep_tiles.append(jnp.minimum(neg_z, 0.0) - jnp.log(1.0 + jnp.exp(jnp.minimum(z, neg_z))))
            if first:
                keep_tiles[-1] = jnp.where(strict, keep_tiles[-1], 0.0)
            run = None if first else carries[h]
            a_tiles = [None] * n_tiles
            for c in reversed(range(n_tiles)):
                tail = _dot(from_here2, jnp.concatenate(_split_bf16(keep_tiles[c]), axis=0))
                tile_total = tail[0:1, :]
                if run is not None:
                    tail = tail + run
                a_tiles[c] = jnp.exp(z_tiles[c] + tail)
                run = tile_total if run is None else run + tile_total
            if first:
                a_tiles[-1] = jnp.where(strict, a_tiles[-1], 0.0)
            pv = _dot(_head_rows(vt_ref, g, h, n_tiles), jnp.concatenate(a_tiles, axis=0).astype(BF16))
            acc_ref[h] = pv if first else acc_ref[h] + pv
            new += (run,)
        return new

    _sweep(qi, group, KEY_GROUP_ODD)
    _store_step_output(o_ref, [acc_ref[h] for h in range(HEADS_PER_STEP)])


def _stick_attn(sb, vt, batch, seq):
    t = ATT_TILE
    n_steps = N_HEADS_D // HEADS_PER_STEP
    return pl.pallas_call(
        _stick_attn_kernel,
        grid=(batch, n_steps, seq // t),
        in_specs=[pl.BlockSpec((1, t, STEP_LANES), lambda b, p, i: (b, i, p)),
                  _resident((1, seq, STEP_LANES), lambda b, p, i: (b, 0, n_steps + p)),
                  _resident((1, PAIRS_PER_STEP) + vt.shape[2:], lambda b, p, i: (b, p, 0, 0, 0))],
        out_specs=pl.BlockSpec((1, t, STEP_LANES), lambda b, p, i: (b, i, p)),
        out_shape=jax.ShapeDtypeStruct((batch, seq, N_HEADS_D * HEAD_DIM), BF16),
        scratch_shapes=[pltpu.VMEM((HEADS_PER_STEP, HEAD_DIM, t), F32)],
        compiler_params=_params(3),
        name="stick_attn",
    )(sb, sb, vt)


def _post_attn_kernel(oa_ref, ob_ref, oah_ref, obh_ref, x_ref, xh_ref, mod_ref, gmix_ref, gpre_ref, gpost_ref,
                      wo_ref, wup_ref, cw_ref, cb_ref, wd_ref, o_ref, x1_ref, h_ref, u_ref, acc_ref, *, per_seq):
    d = D_MODEL
    tm = ROW_TILE
    halo = CONV_HALO
    cw = FF_CHUNK
    i = pl.program_id(0)
    mod = mod_ref[0]
    gate_m, shift, scale, gate_f = mod[:, 2 * d:3 * d], mod[:, 3 * d:4 * d], mod[:, 4 * d:5 * d], mod[:, 5 * d:6 * d]
    n_first = oa_ref.shape[1]

    def mixed(oa, ob, x):
        y = _dot(oa, wo_ref[0:n_first, :]) + _dot(ob, wo_ref[n_first:2 * n_first, :])
        return x + gate_m * _rms(y, gmix_ref[...])

    g = gpre_ref[...]
    ahead = _prenorm(mixed(oah_ref[...], obh_ref[...], xh_ref[...]), g, scale, shift)
    ahead = jnp.where(i % per_seq == 0, 0.0, ahead)
    h_ref[0:halo, :] = ahead.astype(BF16)
    x1_ref[...] = mixed(oa_ref[...], ob_ref[...], x_ref[...])
    h_ref[halo:halo + tm, :] = _prenorm(x1_ref[...], g, scale, shift).astype(BF16)

    def up(ch):
        slot = ch % 2
        for half in range(2):
            cols = slice(half * D_FF + ch * cw, half * D_FF + (ch + 1) * cw)
            u_ref[slot, half] = _dot(h_ref[...], wup_ref[:, cols])

    def conv(slot, half, ch):
        cols = slice(half * D_FF + ch * cw, half * D_FF + (ch + 1) * cw)
        w = cw_ref[:, cols]
        out = w[0:1, :] * u_ref[slot, half, halo - 2:halo - 2 + tm, :]
        out = out + w[1:2, :] * u_ref[slot, half, halo - 1:halo - 1 + tm, :]
        out = out + w[2:3, :] * u_ref[slot, half, halo:halo + tm, :]
        return out + cb_ref[:, cols]

    def down(ch, act):
        part = _dot(act, wd_ref[ch * cw:(ch + 1) * cw, :])
        if ch == 0:
            acc_ref[...] = part
        else:
            acc_ref[...] += part

    up(0)
    act = None
    for ch in range(N_FF_CHUNKS):
        if ch + 1 < N_FF_CHUNKS:
            up(ch + 1)
        if act is not None:
            down(ch - 1, act)
        slot = ch % 2
        act = (jax.nn.gelu(conv(slot, 0, ch), approximate=True) * conv(slot, 1, ch)).astype(BF16)
    down(N_FF_CHUNKS - 1, act)

    o_ref[...] = x1_ref[...] + gate_f * _rms(acc_ref[...], gpost_ref[...])


def _post_attn(oa, ob, x, mod, g_mix, g_pre, g_post, w_out, w_up, conv_w, conv_b, w_down, seq):
    rows, d = x.shape
    tm = ROW_TILE
    halo = CONV_HALO
    per_seq = seq // tm
    n_first = oa.shape[1]
    const = lambda i: (0, 0)
    tile = lambda i: (i, 0)
    ahead = lambda i: (jnp.maximum(i * (tm // halo) - 1, 0), 0)
    once = dict(pipeline_mode=pl.Buffered(1))
    return pl.pallas_call(
        functools.partial(_post_attn_kernel, per_seq=per_seq),
        grid=(rows // tm,),
        in_specs=[pl.BlockSpec((tm, n_first), tile), pl.BlockSpec((tm, n_first), tile),
                  pl.BlockSpec((halo, n_first), ahead), pl.BlockSpec((halo, n_first), ahead),
                  pl.BlockSpec((tm, d), tile), pl.BlockSpec((halo, d), ahead),
                  pl.BlockSpec((1, 1, 6 * d), lambda i: (i // per_seq, 0, 0)),
                  pl.BlockSpec((1, d), const), pl.BlockSpec((1, d), const), pl.BlockSpec((1, d), const),
                  pl.BlockSpec(w_out.shape, const, **once),
                  pl.BlockSpec(w_up.shape, const, **once),
                  pl.BlockSpec(conv_w.shape, const),
                  pl.BlockSpec(conv_b.shape, const),
                  pl.BlockSpec(w_down.shape, const, **once)],
        out_specs=pl.BlockSpec((tm, d), tile),
        out_shape=jax.ShapeDtypeStruct((rows, d), F32),
        scratch_shapes=[pltpu.VMEM((tm, d), F32),
                        pltpu.VMEM((tm + halo, d), BF16),
                        pltpu.VMEM((2, 2, tm + halo, FF_CHUNK), F32),
                        pltpu.VMEM((tm, d), F32)],
        compiler_params=_params(1),
        name="post_attn",
    )(oa, ob, oa, ob, x, x, mod, g_mix, g_pre, g_post, w_out, w_up, conv_w, conv_b, w_down)


def _rotate_half_cols(w):
    half = w.shape[-1] // 2
    return jnp.concatenate([-w[..., half:], w[..., :half]], axis=-1)


def _pad_cols(w, left, total):
    return jnp.pad(w, ((0, 0), (left, total - left - w.shape[1])))


def _rope_tables(seq):
    inv_freq = 1.0 / (ROPE_THETA ** (jnp.arange(0, MLA_ROPE_DIM, 2, dtype=F32) / MLA_ROPE_DIM))
    ang = jnp.arange(seq, dtype=F32)[:, None] * inv_freq[None, :]
    cos, sin = jnp.cos(ang), jnp.sin(ang)
    cos2 = _pad_cols(jnp.concatenate([cos, cos], axis=1), MLA_NOPE_DIM, LANES)
    sin2 = _pad_cols(jnp.concatenate([sin, sin], axis=1), MLA_NOPE_DIM, LANES)
    scale = (MLA_NOPE_DIM + MLA_ROPE_DIM) ** -0.5
    nope_ones = _pad_cols(jnp.ones((seq, MLA_NOPE_DIM), F32), 0, LANES)
    scale = scale * LOG2_E
    return (scale * (cos2 + nope_ones), scale * sin2, cos2, sin2)


def _odd_weights(w_in, w_uq, w_ukv):
    o = MLA_Q_RANK + MLA_KV_RANK
    w_rope = w_in[:, o:o + MLA_ROPE_DIM]
    scale_d = HEAD_DIM ** -0.5
    wd = N_HEADS_D * HEAD_DIM
    sb0 = o + MLA_ROPE_DIM
    w_in2 = jnp.concatenate([
        w_in[:, :o],
        _pad_cols(w_rope, MLA_NOPE_DIM, LANES),
        _pad_cols(_rotate_half_cols(w_rope), MLA_NOPE_DIM, LANES),
        w_in[:, sb0:sb0 + wd] * scale_d,
        w_in[:, sb0 + wd:],
    ], axis=1).astype(BF16)
    qd = MLA_NOPE_DIM + MLA_ROPE_DIM
    uq = w_uq.reshape(MLA_Q_RANK, N_HEADS_C, qd)
    plain = jnp.pad(uq, ((0, 0), (0, 0), (0, LANES - qd)))
    swapped = jnp.pad(_rotate_half_cols(uq[..., MLA_NOPE_DIM:]),
                      ((0, 0), (0, 0), (MLA_NOPE_DIM, LANES - qd)))
    wq = jnp.concatenate([plain.reshape(MLA_Q_RANK, -1), swapped.reshape(MLA_Q_RANK, -1)], axis=1).astype(BF16)
    ukv = w_ukv.reshape(MLA_KV_RANK, N_HEADS_C, MLA_NOPE_DIM + MLA_V_DIM)
    k_nope = jnp.pad(ukv[..., :MLA_NOPE_DIM], ((0, 0), (0, 0), (0, LANES - MLA_NOPE_DIM)))
    v = ukv[..., MLA_NOPE_DIM:]
    wkv = jnp.concatenate([k_nope.reshape(MLA_KV_RANK, -1), v.reshape(MLA_KV_RANK, -1)], axis=1).astype(BF16)
    return w_in2, wq, wkv


def _even_weights(w_in):
    wa = N_HEADS_A * HEAD_DIM
    scale = HEAD_DIM ** -0.5 * LOG2_E
    col = jnp.arange(w_in.shape[1])
    is_q = (col < wa) | ((col >= 3 * wa) & (col < 4 * wa))
    return (w_in * jnp.where(is_q, scale, 1.0)).astype(BF16)


def kernel(x, c, rel_bias, ada_w, ada_b, mix_pre_g, mix_post_g, ffn_pre_g, ffn_post_g, ab_w_in, ab_w_out,
           cd_w_in, mla_q_norm_g, mla_kv_norm_g, mla_w_uq, mla_w_ukv, cd_w_out, ffn_w_up, ffn_conv_w,
           ffn_conv_b, ffn_w_down):
    batch, seq, d = x.shape
    widest_group = max(KEY_GROUP_EVEN, KEY_GROUP_ODD) * ATT_TILE
    assert d == D_MODEL and seq % widest_group == 0 and (KEY_GROUP_EVEN * ATT_TILE) % ROW_TILE == 0
    assert seq == DIL_CLASSES * ATT_TILE
    rows = batch * seq
    xf = x.reshape(rows, d)

    mods = _mods(c, ada_w, ada_b)
    bias_a = _bias_tiles(rel_bias, 0, N_HEADS_A, MOBA_BIAS_TILES, 1, None, "moba_bias_tiles")
    bias_near = _bias_tiles(rel_bias, N_HEADS_A, N_HEADS_B, DIL_NEAR_TILES, 1, DIL_NEAR_BRANCHES,
                            "dilated_near_bias_tiles")
    bias_far = _bias_tiles(rel_bias, N_HEADS_A, N_HEADS_B, 1, DIL_CLASSES, DIL_FAR_BRANCHES,
                           "dilated_far_bias_tiles")
    tables = _rope_tables(seq)

    for layer in range(DEPTH):
        mod = mods[layer].reshape(batch, 1, 6 * d)
        i = layer // 2
        if layer % 2 == 0:
            proj, km, vta, vtb, qkv_cls = _even_proj(xf, mod, mix_pre_g[layer].reshape(1, d),
                                                     _even_weights(ab_w_in[i]), seq)
            proj = proj.reshape(batch, seq, -1)
            km = km.reshape(batch, seq // MOBA_BLOCK, -1)
            o_first = _moba_attn(proj, vta, bias_a, km, batch, seq)
            o_second = _dilated_attn(proj, qkv_cls, vtb, bias_near, bias_far, batch, seq,
                                     3 * N_HEADS_A * HEAD_DIM)
            w_out = ab_w_out[i]
        else:
            w_in2, wq, wkv = _odd_weights(cd_w_in[i], mla_w_uq[i], mla_w_ukv[i])
            qm, km, vtm, sb, vts = _odd_proj(xf, mod, mix_pre_g[layer].reshape(1, d), w_in2,
                                             mla_q_norm_g[i].reshape(1, -1), mla_kv_norm_g[i].reshape(1, -1),
                                             wq, wkv, tables, seq)
            o_first = _mla_attn(qm.reshape(batch, seq, -1), km.reshape(batch, seq, -1), vtm, batch, seq)
            o_second = _stick_attn(sb.reshape(batch, seq, -1), vts, batch, seq)
            w_out = cd_w_out[i]
        xf = _post_attn(o_first.reshape(rows, -1), o_second.reshape(rows, -1), xf, mod,
                        mix_post_g[layer].reshape(1, d), ffn_pre_g[layer].reshape(1, d),
                        ffn_post_g[layer].reshape(1, d), w_out.astype(BF16), ffn_w_up[layer].astype(BF16),
                        ffn_conv_w[layer], ffn_conv_b[layer].reshape(1, -1), ffn_w_down[layer].astype(BF16), seq)
    return xf.reshape(batch, seq, d)
```

```python
import functools
import math

import jax
import jax.numpy as jnp
from jax import lax
from jax.experimental import pallas as pl
from jax.experimental.pallas import tpu as pltpu

F32 = jnp.float32
BF16 = jnp.bfloat16

D_MODEL = 1024
DEPTH = 4
HEAD_DIM = 64
N_HEADS_A = 8
N_HEADS_B = 8
N_HEADS_C = 8
N_HEADS_D = 8
MOBA_BLOCK = 256
MOBA_TOPK = 3
DILATED_BRANCHES = ((128, 1), (512, 4), (2048, 16))
MLA_Q_RANK = 256
MLA_KV_RANK = 256
MLA_NOPE_DIM = 64
MLA_ROPE_DIM = 32
MLA_V_DIM = 64
ROPE_THETA = 10000.0
REL_BUCKETS = 32
REL_MAX_DIST = 2048
D_FF = 2816
CONV_WIDTH = 3
NORM_EPS = 1e-6
LOG2_E = math.log2(math.e)

LANES = 128
SUBLANES = 8
BF16_SUBLANES = 16
VMEM_LIMIT_BYTES = 56 * 1024 * 1024

ATT_TILE = MOBA_BLOCK
KEY_GROUP_EVEN = 4
KEY_GROUP_ODD = 8
PAIRS_PER_STEP = 2
HEADS_PER_STEP = 2 * PAIRS_PER_STEP
STEP_LANES = PAIRS_PER_STEP * LANES
ROW_TILE = 512
FF_CHUNK = 256
N_FF_CHUNKS = D_FF // FF_CHUNK
CONV_HALO = BF16_SUBLANES
MOBA_BIAS_TILES = REL_MAX_DIST // ATT_TILE + 2
DIL_SPLIT = DILATED_BRANCHES[1][0]
DIL_CLASSES = DILATED_BRANCHES[2][1]
DIL_NEAR_TILES = DIL_SPLIT // ATT_TILE + 1
DIL_FAR_LANES = 5 * LANES
assert DIL_FAR_LANES >= N_HEADS_B * (HEAD_DIM + 2)
DIL_NEAR_BRANCHES = tuple((-1, min(window, DIL_SPLIT), dil) for window, dil in DILATED_BRANCHES)
DIL_FAR_BRANCHES = ((DIL_SPLIT, DILATED_BRANCHES[2][0], DIL_CLASSES),)
assert all(window <= DIL_SPLIT for window, _ in DILATED_BRANCHES[:2]) and DIL_SPLIT % DIL_CLASSES == 0

_NT = (((1,), (1,)), ((), ()))


def _bucket_lower_bounds():
    max_exact = REL_BUCKETS // 2
    ratio = REL_MAX_DIST // max_exact
    n_log = REL_BUCKETS - max_exact
    lows = list(range(max_exact + 1))
    for k in range(1, n_log):
        d = lows[-1]
        while d ** n_log < (max_exact ** n_log) * (ratio ** k):
            d += 1
        lows.append(d)
    return lows


_BUCKET_LOW = _bucket_lower_bounds()


def _dot(a, b):
    return jnp.dot(a, b, preferred_element_type=F32)


def _dot_nt(a, b):
    return lax.dot_general(a, b, _NT, preferred_element_type=F32)


def _split_bf16(x):
    hi = x.astype(BF16)
    lo = (x - hi.astype(F32)).astype(BF16)
    return hi, lo


def _rms(x, g):
    return (x * lax.rsqrt(jnp.mean(x * x, axis=-1, keepdims=True) + NORM_EPS)) * g


def _prenorm(x, g, scale, shift):
    return _rms(x, g) * (1.0 + scale) + shift


def _params(n_grid_dims):
    return pltpu.CompilerParams(dimension_semantics=("arbitrary",) * n_grid_dims,
                                vmem_limit_bytes=VMEM_LIMIT_BYTES)


def _mods_kernel(c_ref, w_ref, b_ref, o_ref):
    c = c_ref[...]
    cond = c * jax.nn.sigmoid(c)
    c_hi, c_lo = _split_bf16(cond)
    w_hi, w_lo = _split_bf16(w_ref[0])
    o_ref[0] = _dot(c_hi, w_hi) + _dot(c_hi, w_lo) + _dot(c_lo, w_hi) + b_ref[0]


def _mods(c, ada_w, ada_b):
    b, d = c.shape
    rows = BF16_SUBLANES
    n_out = ada_w.shape[-1]
    tn = n_out // 4
    c_pad = jnp.zeros((rows, d), F32).at[:b].set(c)
    out = pl.pallas_call(
        _mods_kernel,
        grid=(DEPTH, n_out // tn),
        in_specs=[pl.BlockSpec((rows, d), lambda l, j: (0, 0)),
                  pl.BlockSpec((1, d, tn), lambda l, j: (l, 0, j)),
                  pl.BlockSpec((1, 1, tn), lambda l, j: (l, 0, j))],
        out_specs=pl.BlockSpec((1, rows, tn), lambda l, j: (l, 0, j)),
        out_shape=jax.ShapeDtypeStruct((DEPTH, rows, n_out), F32),
        compiler_params=_params(2),
        name="ada_mods",
    )(c_pad, ada_w, ada_b.reshape(DEPTH, 1, n_out))
    return out[:, :b]


def _bias_tiles_kernel(tab_ref, o_ref, *, head_off, step, branches):
    h = pl.program_id(0) + head_off
    d = pl.program_id(1)
    t = ATT_TILE
    dist = (d * t + lax.broadcasted_iota(jnp.int32, (SUBLANES, 2 * t), 1) - t) * step
    val = jnp.full(dist.shape, tab_ref[h, 0], F32)
    for b in range(1, REL_BUCKETS):
        val = jnp.where(dist >= _BUCKET_LOW[b], tab_ref[h, b], val)
    if branches is not None:
        mult = jnp.zeros(dist.shape, F32)
        for beyond, window, dil in branches:
            hit = jnp.where(dist <= window, jnp.where((dist & (dil - 1)) == 0, 1.0, 0.0), 0.0)
            mult = mult + jnp.where(dist > beyond, hit, 0.0)
        log_mult = jnp.where(mult > 2.5, math.log(3.0), jnp.where(mult > 1.5, math.log(2.0), 0.0))
        val = jnp.where(mult > 0.5, val + log_mult, -jnp.inf)
    val = jnp.where(dist >= 0, val * LOG2_E, -jnp.inf)
    strip = jnp.concatenate([val] * (t // SUBLANES), axis=0)
    rotated = pltpu.roll(strip, 0, 1, stride=1, stride_axis=0)
    o_ref[0, 0] = rotated[:, t:2 * t]


def _bias_tiles(rel_bias, head_off, n_heads, n_tiles, step, branches, name):
    t = ATT_TILE
    return pl.pallas_call(
        functools.partial(_bias_tiles_kernel, head_off=head_off, step=step, branches=branches),
        grid=(n_heads, n_tiles),
        in_specs=[pl.BlockSpec(memory_space=pltpu.SMEM)],
        out_specs=pl.BlockSpec((1, 1, t, t), lambda h, d: (h, d, 0, 0)),
        out_shape=jax.ShapeDtypeStruct((n_heads, n_tiles, t, t), F32),
        compiler_params=_params(2),
        name=name,
    )(rel_bias)


def _store_vt(vt_ref, v):
    vt = v.T.astype(BF16)
    for p in range(vt.shape[0] // LANES):
        vt_ref[0, p, 0] = vt[p * LANES:(p + 1) * LANES, :]


def _vt_spec(tm, per_seq, n_pairs, key_group):
    per_group = key_group * ATT_TILE // tm
    return pl.BlockSpec((1, n_pairs, 1, LANES, tm),
                        lambda i: (i // per_seq, 0, (i % per_seq) // per_group, 0, (i % per_seq) % per_group))


def _vt_shape(batch, seq, n_pairs, key_group):
    group_keys = key_group * ATT_TILE
    return jax.ShapeDtypeStruct((batch, n_pairs, seq // group_keys, LANES, group_keys), BF16)


def _even_proj_kernel(x_ref, mod_ref, g_ref, w_ref, o_ref, km_ref, vta_ref, vtb_ref, cls_ref):
    d = D_MODEL
    mod = mod_ref[0]
    h = _prenorm(x_ref[...], g_ref[...], mod[:, d:2 * d], mod[:, 0:d])
    p = _dot(h.astype(BF16), w_ref[...])
    p_bf16 = p.astype(BF16)
    o_ref[...] = p_bf16
    tm = p.shape[0]
    per_class = tm // DIL_CLASSES
    out_row = lax.broadcasted_iota(jnp.int32, (tm, tm), 0)
    in_row = lax.broadcasted_iota(jnp.int32, (tm, tm), 1)
    source = (out_row % per_class) * DIL_CLASSES + out_row // per_class
    permute = jnp.where(in_row == source, 1.0, 0.0).astype(BF16)
    by_class = _dot(permute, p_bf16[:, 3 * N_HEADS_A * HEAD_DIM:]).astype(BF16)
    for cls in range(DIL_CLASSES):
        cls_ref[0, cls] = by_class[cls * per_class:(cls + 1) * per_class]
    wa = N_HEADS_A * HEAD_DIM
    ka = p[:, wa:2 * wa]
    nb = ka.shape[0] // MOBA_BLOCK
    km_ref[0] = jnp.mean(ka.reshape(nb, MOBA_BLOCK, wa), axis=1)
    _store_vt(vta_ref, p[:, 2 * wa:3 * wa])
    vtb = p[:, 5 * wa:6 * wa].T.astype(BF16)
    t = ATT_TILE
    for pair in range(vtb.shape[0] // LANES):
        for tile in range(vtb.shape[1] // t):
            vtb_ref[0, pair, tile] = vtb[pair * LANES:(pair + 1) * LANES, tile * t:(tile + 1) * t]


def _even_proj(x, mod, g, w_in, seq):
    rows, d = x.shape
    tm = ROW_TILE
    n = w_in.shape[1]
    wa = N_HEADS_A * HEAD_DIM
    per_seq = seq // tm
    n_pairs = N_HEADS_A // 2
    vt_spec = _vt_spec(tm, per_seq, n_pairs, KEY_GROUP_EVEN)
    vt_shape = _vt_shape(rows // seq, seq, n_pairs, KEY_GROUP_EVEN)
    tiles_per_step = tm // ATT_TILE
    vt_tile_spec = pl.BlockSpec((1, n_pairs, tiles_per_step, LANES, ATT_TILE),
                                lambda i: (i // per_seq, 0, i % per_seq, 0, 0))
    return pl.pallas_call(
        _even_proj_kernel,
        grid=(rows // tm,),
        in_specs=[pl.BlockSpec((tm, d), lambda i: (i, 0)),
                  pl.BlockSpec((1, 1, 6 * d), lambda i: (i // per_seq, 0, 0)),
                  pl.BlockSpec((1, d), lambda i: (0, 0)),
                  pl.BlockSpec((d, n), lambda i: (0, 0))],
        out_specs=[pl.BlockSpec((tm, n), lambda i: (i, 0)),
                   pl.BlockSpec((1, tm // MOBA_BLOCK, wa), lambda i: (i, 0, 0)),
                   vt_spec, vt_tile_spec,
                   pl.BlockSpec((1, DIL_CLASSES, tm // DIL_CLASSES, n - 3 * wa),
                                lambda i: (i // per_seq, 0, i % per_seq, 0))],
        out_shape=[jax.ShapeDtypeStruct((rows, n), BF16),
                   jax.ShapeDtypeStruct((rows // tm, tm // MOBA_BLOCK, wa), F32),
                   vt_shape, _vt_shape(rows // seq, seq, n_pairs, 1),
                   jax.ShapeDtypeStruct((rows // seq, DIL_CLASSES, seq // DIL_CLASSES, n - 3 * wa), BF16)],
        compiler_params=_params(1),
        name="even_proj",
    )(x, mod, g, w_in)


def _pair_lanes(x, h):
    pair = h // 2
    return x[:, pair * LANES:(pair + 1) * LANES]


def _split_heads(q_step):
    first = lax.broadcasted_iota(jnp.int32, (q_step.shape[0], LANES), 1) < HEAD_DIM
    heads = []
    for pair in range(q_step.shape[1] // LANES):
        q2 = q_step[:, pair * LANES:(pair + 1) * LANES]
        zero = jnp.zeros_like(q2)
        heads += [jnp.where(first, q2, zero), jnp.where(first, zero, q2)]
    return heads


def _head_rows(vt_ref, g, h, n_tiles):
    return vt_ref[0, h // 2, g, (h % 2) * HEAD_DIM:(h % 2 + 1) * HEAD_DIM, 0:n_tiles * ATT_TILE]


def _tile_scores(keys, q_head, per_tile):
    t = ATT_TILE
    n = keys.shape[0] // t
    if per_tile:
        return [_dot_nt(keys[c * t:(c + 1) * t], q_head) for c in range(n)]
    s = _dot_nt(keys, q_head)
    return [s[c * t:(c + 1) * t] for c in range(n)]


def _group_keys(k_ref, g, n_tiles, key_group):
    start = pl.multiple_of(g * key_group * ATT_TILE, key_group * ATT_TILE)
    return k_ref[0, pl.ds(start, n_tiles * ATT_TILE), :]


def _sweep(qi, group, key_group, n_far_groups=None):
    g_own = qi // key_group
    own = [functools.partial(group, g_own, None, n + 1, True) for n in range(key_group)]
    state = lax.switch(qi % key_group, own)
    n_past = g_own if n_far_groups is None else jnp.minimum(g_own, n_far_groups)
    return lax.fori_loop(0, n_past, lambda n, st: group(g_own - 1 - n, st, key_group, False), state)


def _tile_iotas():
    t = ATT_TILE
    return lax.broadcasted_iota(jnp.int32, (t, t), 0), lax.broadcasted_iota(jnp.int32, (t, t), 1)


def _store_step_output(o_ref, outs_t):
    o_ref[0] = jnp.concatenate(outs_t, axis=0).T.astype(BF16)


def _softmax_group(s_tiles, ons, vt_h, acc_ref, h, stats, first):
    maxes = []
    for s, on in zip(s_tiles, ons):
        mx = jnp.max(s, axis=0, keepdims=True)
        maxes.append(mx if on is None else jnp.where(on, mx, -jnp.inf))
    m_new = functools.reduce(jnp.maximum, maxes)
    if not first:
        m_old, l_old = stats
        m_new = jnp.maximum(m_new, m_old)
    ps = []
    l_add = None
    for s, on in zip(s_tiles, ons):
        p = jnp.exp2(s - (m_new if on is None else jnp.where(on, m_new, jnp.inf)))
        p_sum = jnp.sum(p, axis=0, keepdims=True)
        l_add = p_sum if l_add is None else l_add + p_sum
        ps.append(p.astype(BF16))
    pv = _dot(vt_h, jnp.concatenate(ps, axis=0))
    if first:
        acc_ref[h] = pv
        return m_new, l_add
    alpha = jnp.exp2(m_old - m_new)
    acc_ref[h] = alpha * acc_ref[h] + pv
    return m_new, alpha * l_old + l_add


def _moba_select(qh, km_ref, sel_ref, qi):
    t = ATT_TILE
    km_hi, km_lo = _split_bf16(km_ref[0])
    nb = km_hi.shape[0]
    blk = lax.broadcasted_iota(jnp.int32, (nb, t), 0)
    past = blk < qi
    for h in range(HEADS_PER_STEP):
        gate = _dot_nt(_pair_lanes(km_hi, h), qh[h]) + _dot_nt(_pair_lanes(km_lo, h), qh[h])
        gate = jnp.where(past, gate, -jnp.inf)
        beaten = jnp.zeros((nb, t), F32)
        for other in range(nb):
            row = gate[other:other + 1, :]
            wins = jnp.where(row > gate, 1.0, jnp.where(row == gate, jnp.where(blk > other, 1.0, 0.0), 0.0))
            beaten = beaten + wins
        sel_ref[h] = jnp.where(past, jnp.where(beaten < MOBA_TOPK, 1.0, 0.0), 0.0)


def _moba_attn_kernel(q_ref, k_ref, vt_ref, bias_ref, km_ref, o_ref, acc_ref, sel_ref):
    qi = pl.program_id(2)
    qh = _split_heads(q_ref[0])
    n_bias = bias_ref.shape[1]
    _moba_select(qh, km_ref, sel_ref, qi)

    def group(g, stats, n_tiles, first):
        kg = _group_keys(k_ref, g, n_tiles, KEY_GROUP_EVEN)
        new = ()
        scores = [_tile_scores(_pair_lanes(kg, h), qh[h], per_tile=False) for h in range(HEADS_PER_STEP)]
        for h in range(HEADS_PER_STEP):
            s_tiles = scores[h]
            ons = []
            for c in range(n_tiles):
                j = g * KEY_GROUP_EVEN + c
                dt = n_tiles - 1 - c if first else qi - j
                s_tiles[c] = s_tiles[c] + bias_ref[h, dt if first else jnp.minimum(dt, n_bias - 1)]
                ons.append(None if first and dt == 0 else sel_ref[h, pl.ds(j, 1), :] > 0.5)
            new += _softmax_group(s_tiles, ons, _head_rows(vt_ref, g, h, n_tiles), acc_ref, h,
                                  None if first else stats[2 * h:2 * h + 2], first)
        return new

    stats = _sweep(qi, group, KEY_GROUP_EVEN)
    _store_step_output(o_ref, [acc_ref[h] / stats[2 * h + 1] for h in range(HEADS_PER_STEP)])


def _moba_attn(proj, vt, bias, km, batch, seq):
    t = ATT_TILE
    n_steps = N_HEADS_A // HEADS_PER_STEP
    n_bias = bias.shape[1]
    return pl.pallas_call(
        _moba_attn_kernel,
        grid=(batch, n_steps, seq // t),
        in_specs=[pl.BlockSpec((1, t, STEP_LANES), lambda b, p, i: (b, i, p)),
                  pl.BlockSpec((1, seq, STEP_LANES), lambda b, p, i: (b, 0, n_steps + p)),
                  pl.BlockSpec((1, PAIRS_PER_STEP) + vt.shape[2:], lambda b, p, i: (b, p, 0, 0, 0)),
                  pl.BlockSpec((HEADS_PER_STEP, n_bias, t, t), lambda b, p, i: (p, 0, 0, 0)),
                  pl.BlockSpec((1, seq // MOBA_BLOCK, STEP_LANES), lambda b, p, i: (b, 0, p))],
        out_specs=pl.BlockSpec((1, t, STEP_LANES), lambda b, p, i: (b, i, p)),
        out_shape=jax.ShapeDtypeStruct((batch, seq, N_HEADS_A * HEAD_DIM), BF16),
        scratch_shapes=[pltpu.VMEM((HEADS_PER_STEP, HEAD_DIM, t), F32),
                        pltpu.VMEM((HEADS_PER_STEP, seq // MOBA_BLOCK, t), F32)],
        compiler_params=_params(3),
        name="moba_attn",
    )(proj, proj, vt, bias, km)


def _dilated_far_kernel(q_ref, k_ref, v_ref, bias_ref, o_ref):
    t = ATT_TILE
    qh = _split_heads(q_ref[0, 0])
    keys = k_ref[0, 0]
    vt = v_ref[0, 0].astype(F32).T.astype(BF16)
    accs, maxes, sums = [], [], []
    scores = [_dot_nt(keys[:, pair * LANES:(pair + 1) * LANES],
                      jnp.concatenate([qh[2 * pair], qh[2 * pair + 1]], axis=0))
              for pair in range(N_HEADS_B // 2)]
    for pair, s in enumerate(scores):
        s = s + jnp.concatenate([bias_ref[2 * pair, 0], bias_ref[2 * pair + 1, 0]], axis=1)
        m = jnp.max(s, axis=0, keepdims=True)
        p = jnp.exp2(s - jnp.where(m > -jnp.inf, m, 0.0))
        p_sum = jnp.sum(p, axis=0, keepdims=True)
        for j in range(2):
            h = 2 * pair + j
            cols = slice(j * t, (j + 1) * t)
            maxes.append(m[:, cols])
            sums.append(p_sum[:, cols])
            accs.append(_dot(vt[h * HEAD_DIM:(h + 1) * HEAD_DIM], p[:, cols].astype(BF16)))
    pad = jnp.zeros((DIL_FAR_LANES - N_HEADS_B * (HEAD_DIM + 2), t), F32)
    o_ref[0, 0] = jnp.concatenate(accs + maxes + sums + [pad], axis=0).T


def _dilated_far(qkv_cls, bias, batch):
    t = ATT_TILE
    wb = N_HEADS_B * HEAD_DIM
    cols = lambda group: pl.BlockSpec((1, 1, t, wb), lambda b, c: (b, c, 0, group))
    return pl.pallas_call(
        _dilated_far_kernel,
        grid=(batch, DIL_CLASSES),
        in_specs=[cols(0), cols(1), cols(2),
                  pl.BlockSpec((N_HEADS_B, 1, t, t), lambda b, c: (0, 0, 0, 0))],
        out_specs=pl.BlockSpec((1, 1, t, DIL_FAR_LANES), lambda b, c: (b, c, 0, 0)),
        out_shape=jax.ShapeDtypeStruct((batch, DIL_CLASSES, t, DIL_FAR_LANES), F32),
        compiler_params=_params(2),
        name="dilated_far",
    )(qkv_cls, qkv_cls, qkv_cls, bias)


def _dilated_near_kernel(q_ref, k_ref, vt_ref, bias_ref, far_ref, o_ref, acc_ref):
    t = ATT_TILE
    qi = pl.program_id(1)
    qh = _split_heads(q_ref[0])

    def tiles(n_tiles):
        first_tile = qi - (n_tiles - 1)
        keys = k_ref[0, pl.ds(pl.multiple_of(first_tile * t, t), n_tiles * t), :]
        stats = ()
        scores = [_dot_nt(keys[:, pair * LANES:(pair + 1) * LANES],
                          jnp.concatenate([qh[2 * pair], qh[2 * pair + 1]], axis=0))
                  for pair in range(N_HEADS_B // 2)]
        for h in range(N_HEADS_B):
            s = scores[h // 2][:, (h % 2) * t:(h % 2 + 1) * t]
            s_tiles = [s[c * t:(c + 1) * t] + bias_ref[h, n_tiles - 1 - c] for c in range(n_tiles)]
            rows = slice((h % 2) * HEAD_DIM, (h % 2 + 1) * HEAD_DIM)
            vt_h = jnp.concatenate([vt_ref[0, h // 2, first_tile + c, rows, :] for c in range(n_tiles)], axis=1)
            stats += _softmax_group(s_tiles, [None] * n_tiles, vt_h, acc_ref, h, None, True)
        return stats

    stats = lax.switch(jnp.minimum(qi, DIL_NEAR_TILES - 1),
                       [functools.partial(tiles, n + 1) for n in range(DIL_NEAR_TILES)])
    far = far_ref[0].T
    far_m0, far_l0 = N_HEADS_B * HEAD_DIM, N_HEADS_B * HEAD_DIM + N_HEADS_B
    outs = []
    for h in range(N_HEADS_B):
        m_near, l_near = stats[2 * h], stats[2 * h + 1]
        m_far, l_far = far[far_m0 + h:far_m0 + h + 1], far[far_l0 + h:far_l0 + h + 1]
        m = jnp.maximum(m_near, m_far)
        w_near, w_far = jnp.exp2(m_near - m), jnp.exp2(m_far - m)
        acc_far = far[h * HEAD_DIM:(h + 1) * HEAD_DIM]
        outs.append((w_near * acc_ref[h] + w_far * acc_far) / (w_near * l_near + w_far * l_far))
    _store_step_output(o_ref, outs)


def _dilated_near(proj, vt, bias, far, batch, seq, lane0):
    t = ATT_TILE
    wb = N_HEADS_B * HEAD_DIM
    first = lane0 // wb
    return pl.pallas_call(
        _dilated_near_kernel,
        grid=(batch, seq // t),
        in_specs=[pl.BlockSpec((1, t, wb), lambda b, i: (b, i, first)),
                  pl.BlockSpec((1, seq, wb), lambda b, i: (b, 0, first + 1)),
                  pl.BlockSpec((1,) + vt.shape[1:], lambda b, i: (b, 0, 0, 0, 0)),
                  pl.BlockSpec((N_HEADS_B, DIL_NEAR_TILES, t, t), lambda b, i: (0, 0, 0, 0)),
                  pl.BlockSpec((1, t, DIL_FAR_LANES), lambda b, i: (b, i, 0))],
        out_specs=pl.BlockSpec((1, t, wb), lambda b, i: (b, i, 0)),
        out_shape=jax.ShapeDtypeStruct((batch, seq, wb), BF16),
        scratch_shapes=[pltpu.VMEM((N_HEADS_B, HEAD_DIM, t), F32)],
        compiler_params=_params(2),
        name="dilated_near",
    )(proj, proj, vt, bias, far)


def _dilated_attn(proj, qkv_cls, vt_tiles, bias_near, bias_far, batch, seq, lane0):
    far = _dilated_far(qkv_cls, bias_far, batch)
    far = far.transpose(0, 2, 1, 3).reshape(batch, seq, DIL_FAR_LANES)
    return _dilated_near(proj, vt_tiles, bias_near, far, batch, seq, lane0)


def _odd_proj_kernel(x_ref, mod_ref, g_ref, win_ref, gq_ref, gkv_ref, wq_ref, wkv_ref,
                     cq_ref, sq_ref, ck_ref, sk_ref, qm_ref, km_ref, vtm_ref, sb_ref, vts_ref):
    d = D_MODEL
    mod = mod_ref[0]
    h = _prenorm(x_ref[...], g_ref[...], mod[:, d:2 * d], mod[:, 0:d])
    p = _dot(h.astype(BF16), win_ref[...])
    o = MLA_Q_RANK + MLA_KV_RANK
    c_q, c_kv = p[:, 0:MLA_Q_RANK], p[:, MLA_Q_RANK:o]
    k_rope, k_rope_swapped = p[:, o:o + LANES], p[:, o + LANES:o + 2 * LANES]
    sb0 = o + 2 * LANES
    n_qk = 2 * N_HEADS_D * HEAD_DIM
    sb_ref[...] = p[:, sb0:sb0 + n_qk].astype(BF16)
    _store_vt(vts_ref, p[:, sb0 + n_qk:])
    q12 = _dot(_rms(c_q, gq_ref[...]).astype(BF16), wq_ref[...])
    kv = _dot(_rms(c_kv, gkv_ref[...]).astype(BF16), wkv_ref[...])
    k_pe = k_rope * ck_ref[...] + k_rope_swapped * sk_ref[...]
    cq, sq = cq_ref[...], sq_ref[...]
    half = N_HEADS_C * LANES
    for hh in range(N_HEADS_C):
        cols = slice(hh * LANES, (hh + 1) * LANES)
        swapped = slice(half + hh * LANES, half + (hh + 1) * LANES)
        qm_ref[:, cols] = (q12[:, cols] * cq + q12[:, swapped] * sq).astype(BF16)
        km_ref[:, cols] = (kv[:, cols] + k_pe).astype(BF16)
    _store_vt(vtm_ref, kv[:, half:])


def _odd_proj(x, mod, g, w_in, gq, gkv, wq, wkv, tables, seq):
    rows, d = x.shape
    tm = ROW_TILE
    per_seq = seq // tm
    n_in = w_in.shape[1]
    n_sb = 2 * N_HEADS_D * HEAD_DIM
    n_q = N_HEADS_C * LANES
    n_pairs = N_HEADS_C // 2
    const = lambda i: (0, 0)
    table_spec = pl.BlockSpec((tm, LANES), lambda i: (i % per_seq, 0))
    vt_spec = _vt_spec(tm, per_seq, n_pairs, KEY_GROUP_ODD)
    vt_shape = _vt_shape(rows // seq, seq, n_pairs, KEY_GROUP_ODD)
    return pl.pallas_call(
        _odd_proj_kernel,
        grid=(rows // tm,),
        in_specs=[pl.BlockSpec((tm, d), lambda i: (i, 0)),
                  pl.BlockSpec((1, 1, 6 * d), lambda i: (i // per_seq, 0, 0)),
                  pl.BlockSpec((1, d), const),
                  pl.BlockSpec((d, n_in), const),
                  pl.BlockSpec((1, MLA_Q_RANK), const),
                  pl.BlockSpec((1, MLA_KV_RANK), const),
                  pl.BlockSpec(wq.shape, const),
                  pl.BlockSpec(wkv.shape, const),
                  table_spec, table_spec, table_spec, table_spec],
        out_specs=[pl.BlockSpec((tm, n_q), lambda i: (i, 0)),
                   pl.BlockSpec((tm, n_q), lambda i: (i, 0)),
                   vt_spec,
                   pl.BlockSpec((tm, n_sb), lambda i: (i, 0)),
                   vt_spec],
        out_shape=[jax.ShapeDtypeStruct((rows, n_q), BF16),
                   jax.ShapeDtypeStruct((rows, n_q), BF16),
                   vt_shape,
                   jax.ShapeDtypeStruct((rows, n_sb), BF16),
                   vt_shape],
        compiler_params=_params(1),
        name="odd_proj",
    )(x, mod, g, w_in, gq, gkv, wq, wkv, *tables)


def _mla_attn_kernel(q_ref, k_ref, vt_ref, o_ref, acc_ref):
    qi = pl.program_id(2)
    q_step = q_ref[0]
    qh = [q_step[:, h * LANES:(h + 1) * LANES] for h in range(HEADS_PER_STEP)]

    def group(g, stats, n_tiles, first):
        kg = _group_keys(k_ref, g, n_tiles, KEY_GROUP_ODD)
        new = ()
        scores = [_tile_scores(kg[:, h * LANES:(h + 1) * LANES], qh[h], per_tile=True)
                  for h in range(HEADS_PER_STEP)]
        for h in range(HEADS_PER_STEP):
            s_tiles = scores[h]
            if first:
                key, query = _tile_iotas()
                s_tiles[-1] = jnp.where(key <= query, s_tiles[-1], -jnp.inf)
            new += _softmax_group(s_tiles, [None] * n_tiles, _head_rows(vt_ref, g, h, n_tiles), acc_ref, h,
                                  None if first else stats[2 * h:2 * h + 2], first)
        return new

    stats = _sweep(qi, group, KEY_GROUP_ODD)
    _store_step_output(o_ref, [acc_ref[h] / stats[2 * h + 1] for h in range(HEADS_PER_STEP)])


def _mla_attn(qm, km, vt, batch, seq):
    t = ATT_TILE
    n_steps = N_HEADS_C // HEADS_PER_STEP
    qk_lanes = HEADS_PER_STEP * LANES
    return pl.pallas_call(
        _mla_attn_kernel,
        grid=(batch, n_steps, seq // t),
        in_specs=[pl.BlockSpec((1, t, qk_lanes), lambda b, p, i: (b, i, p)),
                  pl.BlockSpec((1, seq, qk_lanes), lambda b, p, i: (b, 0, p)),
                  pl.BlockSpec((1, PAIRS_PER_STEP) + vt.shape[2:], lambda b, p, i: (b, p, 0, 0, 0))],
        out_specs=pl.BlockSpec((1, t, STEP_LANES), lambda b, p, i: (b, i, p)),
        out_shape=jax.ShapeDtypeStruct((batch, seq, N_HEADS_C * MLA_V_DIM), BF16),
        scratch_shapes=[pltpu.VMEM((HEADS_PER_STEP, HEAD_DIM, t), F32)],
        compiler_params=_params(3),
        name="mla_attn",
    )(qm, km, vt)


def _stick_attn_kernel(q_ref, k_ref, vt_ref, o_ref, acc_ref):
    t = ATT_TILE
    qi = pl.program_id(2)
    qh = _split_heads(q_ref[0])
    key = lax.broadcasted_iota(jnp.int32, (t, t), 0)
    other = lax.broadcasted_iota(jnp.int32, (t, t), 1)
    from_here = jnp.where(other >= key, 1.0, 0.0).astype(BF16)
    from_here2 = jnp.concatenate([from_here, from_here], axis=1)

    strict = key < other

    def group(g, carries, n_tiles, first):
        kg = _group_keys(k_ref, g, n_tiles, KEY_GROUP_ODD)
        new = ()
        scores = [_tile_scores(_pair_lanes(kg, h), qh[h], per_tile=False) for h in range(HEADS_PER_STEP)]
        for h in range(HEADS_PER_STEP):
            z_tiles = scores[h]
            keep_tiles = []
            for z in z_tiles:
                neg_z = -z
                keep_tiles.append(jnp.minimum(neg_z, 0.0) - jnp.log(1.0 + jnp.exp(jnp.minimum(z, neg_z))))
            if first:
                keep_tiles[-1] = jnp.where(strict, keep_tiles[-1], 0.0)
            run = None if first else carries[h]
            a_tiles = [None] * n_tiles
            for c in reversed(range(n_tiles)):
                tail = _dot(from_here2, jnp.concatenate(_split_bf16(keep_tiles[c]), axis=0))
                tile_total = tail[0:1, :]
                if run is not None:
                    tail = tail + run
                a_tiles[c] = jnp.exp(z_tiles[c] + tail)
                run = tile_total if run is None else run + tile_total
            if first:
                a_tiles[-1] = jnp.where(strict, a_tiles[-1], 0.0)
            pv = _dot(_head_rows(vt_ref, g, h, n_tiles), jnp.concatenate(a_tiles, axis=0).astype(BF16))
            acc_ref[h] = pv if first else acc_ref[h] + pv
            new += (run,)
        return new

    _sweep(qi, group, KEY_GROUP_ODD)
    _store_step_output(o_ref, [acc_ref[h] for h in range(HEADS_PER_STEP)])


def _stick_attn(sb, vt, batch, seq):
    t = ATT_TILE
    n_steps = N_HEADS_D // HEADS_PER_STEP
    return pl.pallas_call(
        _stick_attn_kernel,
        grid=(batch, n_steps, seq // t),
        in_specs=[pl.BlockSpec((1, t, STEP_LANES), lambda b, p, i: (b, i, p)),
                  pl.BlockSpec((1, seq, STEP_LANES), lambda b, p, i: (b, 0, n_steps + p)),
                  pl.BlockSpec((1, PAIRS_PER_STEP) + vt.shape[2:], lambda b, p, i: (b, p, 0, 0, 0))],
        out_specs=pl.BlockSpec((1, t, STEP_LANES), lambda b, p, i: (b, i, p)),
        out_shape=jax.ShapeDtypeStruct((batch, seq, N_HEADS_D * HEAD_DIM), BF16),
        scratch_shapes=[pltpu.VMEM((HEADS_PER_STEP, HEAD_DIM, t), F32)],
        compiler_params=_params(3),
        name="stick_attn",
    )(sb, sb, vt)


def _post_attn_kernel(oa_ref, ob_ref, oah_ref, obh_ref, x_ref, xh_ref, mod_ref, gmix_ref, gpre_ref, gpost_ref,
                      wo_ref, wup_ref, cw_ref, cb_ref, wd_ref, o_ref, x1_ref, h_ref, u_ref, acc_ref, *, per_seq):
    d = D_MODEL
    tm = ROW_TILE
    halo = CONV_HALO
    cw = FF_CHUNK
    i = pl.program_id(0)
    mod = mod_ref[0]
    gate_m, shift, scale, gate_f = mod[:, 2 * d:3 * d], mod[:, 3 * d:4 * d], mod[:, 4 * d:5 * d], mod[:, 5 * d:6 * d]
    n_first = oa_ref.shape[1]

    def mixed(oa, ob, x):
        y = _dot(oa, wo_ref[0:n_first, :]) + _dot(ob, wo_ref[n_first:2 * n_first, :])
        return x + gate_m * _rms(y, gmix_ref[...])

    g = gpre_ref[...]
    ahead = _prenorm(mixed(oah_ref[...], obh_ref[...], xh_ref[...]), g, scale, shift)
    ahead = jnp.where(i % per_seq == 0, 0.0, ahead)
    h_ref[0:halo, :] = ahead.astype(BF16)
    x1_ref[...] = mixed(oa_ref[...], ob_ref[...], x_ref[...])
    h_ref[halo:halo + tm, :] = _prenorm(x1_ref[...], g, scale, shift).astype(BF16)

    def up(ch):
        slot = ch % 2
        for half in range(2):
            cols = slice(half * D_FF + ch * cw, half * D_FF + (ch + 1) * cw)
            u_ref[slot, half] = _dot(h_ref[...], wup_ref[:, cols])

    def conv(slot, half, ch):
        cols = slice(half * D_FF + ch * cw, half * D_FF + (ch + 1) * cw)
        w = cw_ref[:, cols]
        out = None
        for tap in range(CONV_WIDTH):
            start = halo - (CONV_WIDTH - 1 - tap)
            term = w[tap:tap + 1, :] * u_ref[slot, half, start:start + tm, :]
            out = term if out is None else out + term
        return out + cb_ref[:, cols]

    def down(ch, act):
        part = _dot(act, wd_ref[ch * cw:(ch + 1) * cw, :])
        if ch == 0:
            acc_ref[...] = part
        else:
            acc_ref[...] += part

    up(0)
    act = None
    for ch in range(N_FF_CHUNKS):
        if ch + 1 < N_FF_CHUNKS:
            up(ch + 1)
        if act is not None:
            down(ch - 1, act)
        slot = ch % 2
        act = (jax.nn.gelu(conv(slot, 0, ch), approximate=True) * conv(slot, 1, ch)).astype(BF16)
    down(N_FF_CHUNKS - 1, act)

    o_ref[...] = x1_ref[...] + gate_f * _rms(acc_ref[...], gpost_ref[...])


def _post_attn(oa, ob, x, mod, g_mix, g_pre, g_post, w_out, w_up, conv_w, conv_b, w_down, seq):
    rows, d = x.shape
    tm = ROW_TILE
    halo = CONV_HALO
    per_seq = seq // tm
    n_first = oa.shape[1]
    const = lambda i: (0, 0)
    tile = lambda i: (i, 0)
    ahead = lambda i: (jnp.maximum(i * (tm // halo) - 1, 0), 0)
    once = dict(pipeline_mode=pl.Buffered(1))
    return pl.pallas_call(
        functools.partial(_post_attn_kernel, per_seq=per_seq),
        grid=(rows // tm,),
        in_specs=[pl.BlockSpec((tm, n_first), tile), pl.BlockSpec((tm, n_first), tile),
                  pl.BlockSpec((halo, n_first), ahead), pl.BlockSpec((halo, n_first), ahead),
                  pl.BlockSpec((tm, d), tile), pl.BlockSpec((halo, d), ahead),
                  pl.BlockSpec((1, 1, 6 * d), lambda i: (i // per_seq, 0, 0)),
                  pl.BlockSpec((1, d), const), pl.BlockSpec((1, d), const), pl.BlockSpec((1, d), const),
                  pl.BlockSpec(w_out.shape, const, **once),
                  pl.BlockSpec(w_up.shape, const, **once),
                  pl.BlockSpec(conv_w.shape, const),
                  pl.BlockSpec(conv_b.shape, const),
                  pl.BlockSpec(w_down.shape, const, **once)],
        out_specs=pl.BlockSpec((tm, d), tile),
        out_shape=jax.ShapeDtypeStruct((rows, d), F32),
        scratch_shapes=[pltpu.VMEM((tm, d), F32),
                        pltpu.VMEM((tm + halo, d), BF16),
                        pltpu.VMEM((2, 2, tm + halo, FF_CHUNK), F32),
                        pltpu.VMEM((tm, d), F32)],
        compiler_params=_params(1),
        name="post_attn",
    )(oa, ob, oa, ob, x, x, mod, g_mix, g_pre, g_post, w_out, w_up, conv_w, conv_b, w_down)


def _rotate_half_cols(w):
    half = w.shape[-1] // 2
    return jnp.concatenate([-w[..., half:], w[..., :half]], axis=-1)


def _pad_cols(w, left, total):
    return jnp.pad(w, ((0, 0), (left, total - left - w.shape[1])))


def _rope_tables(seq):
    inv_freq = 1.0 / (ROPE_THETA ** (jnp.arange(0, MLA_ROPE_DIM, 2, dtype=F32) / MLA_ROPE_DIM))
    ang = jnp.arange(seq, dtype=F32)[:, None] * inv_freq[None, :]
    cos, sin = jnp.cos(ang), jnp.sin(ang)
    cos2 = _pad_cols(jnp.concatenate([cos, cos], axis=1), MLA_NOPE_DIM, LANES)
    sin2 = _pad_cols(jnp.concatenate([sin, sin], axis=1), MLA_NOPE_DIM, LANES)
    scale = (MLA_NOPE_DIM + MLA_ROPE_DIM) ** -0.5
    nope_ones = _pad_cols(jnp.ones((seq, MLA_NOPE_DIM), F32), 0, LANES)
    scale = scale * LOG2_E
    return (scale * (cos2 + nope_ones), scale * sin2, cos2, sin2)


def _odd_weights(w_in, w_uq, w_ukv):
    o = MLA_Q_RANK + MLA_KV_RANK
    w_rope = w_in[:, o:o + MLA_ROPE_DIM]
    scale_d = HEAD_DIM ** -0.5
    wd = N_HEADS_D * HEAD_DIM
    sb0 = o + MLA_ROPE_DIM
    w_in2 = jnp.concatenate([
        w_in[:, :o],
        _pad_cols(w_rope, MLA_NOPE_DIM, LANES),
        _pad_cols(_rotate_half_cols(w_rope), MLA_NOPE_DIM, LANES),
        w_in[:, sb0:sb0 + wd] * scale_d,
        w_in[:, sb0 + wd:],
    ], axis=1).astype(BF16)
    qd = MLA_NOPE_DIM + MLA_ROPE_DIM
    uq = w_uq.reshape(MLA_Q_RANK, N_HEADS_C, qd)
    plain = jnp.pad(uq, ((0, 0), (0, 0), (0, LANES - qd)))
    swapped = jnp.pad(_rotate_half_cols(uq[..., MLA_NOPE_DIM:]),
                      ((0, 0), (0, 0), (MLA_NOPE_DIM, LANES - qd)))
    wq = jnp.concatenate([plain.reshape(MLA_Q_RANK, -1), swapped.reshape(MLA_Q_RANK, -1)], axis=1).astype(BF16)
    ukv = w_ukv.reshape(MLA_KV_RANK, N_HEADS_C, MLA_NOPE_DIM + MLA_V_DIM)
    k_nope = jnp.pad(ukv[..., :MLA_NOPE_DIM], ((0, 0), (0, 0), (0, LANES - MLA_NOPE_DIM)))
    v = ukv[..., MLA_NOPE_DIM:]
    wkv = jnp.concatenate([k_nope.reshape(MLA_KV_RANK, -1), v.reshape(MLA_KV_RANK, -1)], axis=1).astype(BF16)
    return w_in2, wq, wkv


def _even_weights(w_in):
    wa = N_HEADS_A * HEAD_DIM
    scale = HEAD_DIM ** -0.5 * LOG2_E
    col = jnp.arange(w_in.shape[1])
    is_q = (col < wa) | ((col >= 3 * wa) & (col < 4 * wa))
    return (w_in * jnp.where(is_q, scale, 1.0)).astype(BF16)


def kernel(x, c, rel_bias, ada_w, ada_b, mix_pre_g, mix_post_g, ffn_pre_g, ffn_post_g, ab_w_in, ab_w_out,
           cd_w_in, mla_q_norm_g, mla_kv_norm_g, mla_w_uq, mla_w_ukv, cd_w_out, ffn_w_up, ffn_conv_w,
           ffn_conv_b, ffn_w_down):
    batch, seq, d = x.shape
    widest_group = max(KEY_GROUP_EVEN, KEY_GROUP_ODD) * ATT_TILE
    assert d == D_MODEL and seq % widest_group == 0 and (KEY_GROUP_EVEN * ATT_TILE) % ROW_TILE == 0
    assert seq == DIL_CLASSES * ATT_TILE
    rows = batch * seq
    xf = x.reshape(rows, d)

    mods = _mods(c, ada_w, ada_b)
    bias_a = _bias_tiles(rel_bias, 0, N_HEADS_A, MOBA_BIAS_TILES, 1, None, "moba_bias_tiles")
    bias_near = _bias_tiles(rel_bias, N_HEADS_A, N_HEADS_B, DIL_NEAR_TILES, 1, DIL_NEAR_BRANCHES,
                            "dilated_near_bias_tiles")
    bias_far = _bias_tiles(rel_bias, N_HEADS_A, N_HEADS_B, 1, DIL_CLASSES, DIL_FAR_BRANCHES,
                           "dilated_far_bias_tiles")
    tables = _rope_tables(seq)

    for layer in range(DEPTH):
        mod = mods[layer].reshape(batch, 1, 6 * d)
        i = layer // 2
        if layer % 2 == 0:
            proj, km, vta, vtb, qkv_cls = _even_proj(xf, mod, mix_pre_g[layer].reshape(1, d),
                                                     _even_weights(ab_w_in[i]), seq)
            proj = proj.reshape(batch, seq, -1)
            km = km.reshape(batch, seq // MOBA_BLOCK, -1)
            o_first = _moba_attn(proj, vta, bias_a, km, batch, seq)
            o_second = _dilated_attn(proj, qkv_cls, vtb, bias_near, bias_far, batch, seq,
                                     3 * N_HEADS_A * HEAD_DIM)
            w_out = ab_w_out[i]
        else:
            w_in2, wq, wkv = _odd_weights(cd_w_in[i], mla_w_uq[i], mla_w_ukv[i])
            qm, km, vtm, sb, vts = _odd_proj(xf, mod, mix_pre_g[layer].reshape(1, d), w_in2,
                                             mla_q_norm_g[i].reshape(1, -1), mla_kv_norm_g[i].reshape(1, -1),
                                             wq, wkv, tables, seq)
            o_first = _mla_attn(qm.reshape(batch, seq, -1), km.reshape(batch, seq, -1), vtm, batch, seq)
            o_second = _stick_attn(sb.reshape(batch, seq, -1), vts, batch, seq)
            w_out = cd_w_out[i]
        xf = _post_attn(o_first.reshape(rows, -1), o_second.reshape(rows, -1), xf, mod,
                        mix_post_g[layer].reshape(1, d), ffn_pre_g[layer].reshape(1, d),
                        ffn_post_g[layer].reshape(1, d), w_out.astype(BF16), ffn_w_up[layer].astype(BF16),
                        ffn_conv_w[layer], ffn_conv_b[layer].reshape(1, -1), ffn_w_down[layer].astype(BF16), seq)
    return xf.reshape(batch, seq, d)
```

```python
import functools
import math

import jax
import jax.numpy as jnp
from jax import lax
from jax.experimental import pallas as pl
from jax.experimental.pallas import tpu as pltpu

F32 = jnp.float32
BF16 = jnp.bfloat16

D_MODEL = 1024
DEPTH = 4
HEAD_DIM = 64
N_HEADS_A = 8
N_HEADS_B = 8
N_HEADS_C = 8
N_HEADS_D = 8
MOBA_BLOCK = 256
MOBA_TOPK = 3
DILATED_BRANCHES = ((128, 1), (512, 4), (2048, 16))
MLA_Q_RANK = 256
MLA_KV_RANK = 256
MLA_NOPE_DIM = 64
MLA_ROPE_DIM = 32
MLA_V_DIM = 64
ROPE_THETA = 10000.0
REL_BUCKETS = 32
REL_MAX_DIST = 2048
D_FF = 2816
CONV_WIDTH = 3
NORM_EPS = 1e-6
LOG2_E = math.log2(math.e)

LANES = 128
SUBLANES = 8
BF16_SUBLANES = 16
VMEM_LIMIT_BYTES = 56 * 1024 * 1024

ATT_TILE = MOBA_BLOCK
KEY_GROUP_EVEN = 4
KEY_GROUP_ODD = 8
PAIRS_PER_STEP = 2
HEADS_PER_STEP = 2 * PAIRS_PER_STEP
STEP_LANES = PAIRS_PER_STEP * LANES
ROW_TILE = 512
FF_CHUNK = 256
N_FF_CHUNKS = D_FF // FF_CHUNK
CONV_HALO = BF16_SUBLANES
MOBA_BIAS_TILES = REL_MAX_DIST // ATT_TILE + 2
DIL_SPLIT = DILATED_BRANCHES[1][0]
DIL_CLASSES = DILATED_BRANCHES[2][1]
DIL_NEAR_TILES = DIL_SPLIT // ATT_TILE + 1
DIL_FAR_LANES = 5 * LANES
assert DIL_FAR_LANES >= N_HEADS_B * (HEAD_DIM + 2)
DIL_NEAR_BRANCHES = tuple((-1, min(window, DIL_SPLIT), dil) for window, dil in DILATED_BRANCHES)
DIL_FAR_BRANCHES = ((DIL_SPLIT, DILATED_BRANCHES[2][0], DIL_CLASSES),)
assert all(window <= DIL_SPLIT for window, _ in DILATED_BRANCHES[:2]) and DIL_SPLIT % DIL_CLASSES == 0

_NT = (((1,), (1,)), ((), ()))


def _bucket_lower_bounds():
    max_exact = REL_BUCKETS // 2
    ratio = REL_MAX_DIST // max_exact
    n_log = REL_BUCKETS - max_exact
    lows = list(range(max_exact + 1))
    for k in range(1, n_log):
        d = lows[-1]
        while d ** n_log < (max_exact ** n_log) * (ratio ** k):
            d += 1
        lows.append(d)
    return lows


_BUCKET_LOW = _bucket_lower_bounds()


def _dot(a, b):
    return jnp.dot(a, b, preferred_element_type=F32)


def _dot_nt(a, b):
    return lax.dot_general(a, b, _NT, preferred_element_type=F32)


def _split_bf16(x):
    hi = x.astype(BF16)
    lo = (x - hi.astype(F32)).astype(BF16)
    return hi, lo


def _rms(x, g):
    return (x * lax.rsqrt(jnp.mean(x * x, axis=-1, keepdims=True) + NORM_EPS)) * g


def _prenorm(x, g, scale, shift):
    return _rms(x, g) * (1.0 + scale) + shift


def _params(n_grid_dims):
    return pltpu.CompilerParams(dimension_semantics=("arbitrary",) * n_grid_dims,
                                vmem_limit_bytes=VMEM_LIMIT_BYTES)


def _mods_kernel(c_ref, w_ref, b_ref, o_ref):
    c = c_ref[...]
    cond = c * jax.nn.sigmoid(c)
    c_hi, c_lo = _split_bf16(cond)
    w_hi, w_lo = _split_bf16(w_ref[0])
    o_ref[0] = _dot(c_hi, w_hi) + _dot(c_hi, w_lo) + _dot(c_lo, w_hi) + b_ref[0]


def _mods(c, ada_w, ada_b):
    b, d = c.shape
    rows = BF16_SUBLANES
    n_out = ada_w.shape[-1]
    tn = n_out // 4
    c_pad = jnp.zeros((rows, d), F32).at[:b].set(c)
    out = pl.pallas_call(
        _mods_kernel,
        grid=(DEPTH, n_out // tn),
        in_specs=[pl.BlockSpec((rows, d), lambda l, j: (0, 0)),
                  pl.BlockSpec((1, d, tn), lambda l, j: (l, 0, j)),
                  pl.BlockSpec((1, 1, tn), lambda l, j: (l, 0, j))],
        out_specs=pl.BlockSpec((1, rows, tn), lambda l, j: (l, 0, j)),
        out_shape=jax.ShapeDtypeStruct((DEPTH, rows, n_out), F32),
        compiler_params=_params(2),
        name="ada_mods",
    )(c_pad, ada_w, ada_b.reshape(DEPTH, 1, n_out))
    return out[:, :b]


def _bias_tiles_kernel(tab_ref, o_ref, *, head_off, step, branches):
    h = pl.program_id(0) + head_off
    d = pl.program_id(1)
    t = ATT_TILE
    dist = (d * t + lax.broadcasted_iota(jnp.int32, (SUBLANES, 2 * t), 1) - t) * step
    val = jnp.full(dist.shape, tab_ref[h, 0], F32)
    for b in range(1, REL_BUCKETS):
        val = jnp.where(dist >= _BUCKET_LOW[b], tab_ref[h, b], val)
    if branches is not None:
        mult = jnp.zeros(dist.shape, F32)
        for beyond, window, dil in branches:
            hit = jnp.where(dist <= window, jnp.where((dist & (dil - 1)) == 0, 1.0, 0.0), 0.0)
            mult = mult + jnp.where(dist > beyond, hit, 0.0)
        log_mult = jnp.where(mult > 2.5, math.log(3.0), jnp.where(mult > 1.5, math.log(2.0), 0.0))
        val = jnp.where(mult > 0.5, val + log_mult, -jnp.inf)
    val = jnp.where(dist >= 0, val * LOG2_E, -jnp.inf)
    strip = jnp.concatenate([val] * (t // SUBLANES), axis=0)
    rotated = pltpu.roll(strip, 0, 1, stride=1, stride_axis=0)
    o_ref[0, 0] = rotated[:, t:2 * t]


def _bias_tiles(rel_bias, head_off, n_heads, n_tiles, step, branches, name):
    t = ATT_TILE
    return pl.pallas_call(
        functools.partial(_bias_tiles_kernel, head_off=head_off, step=step, branches=branches),
        grid=(n_heads, n_tiles),
        in_specs=[pl.BlockSpec(memory_space=pltpu.SMEM)],
        out_specs=pl.BlockSpec((1, 1, t, t), lambda h, d: (h, d, 0, 0)),
        out_shape=jax.ShapeDtypeStruct((n_heads, n_tiles, t, t), F32),
        compiler_params=_params(2),
        name=name,
    )(rel_bias)


def _store_vt(vt_ref, v):
    vt = v.T.astype(BF16)
    for p in range(vt.shape[0] // LANES):
        vt_ref[0, p, 0] = vt[p * LANES:(p + 1) * LANES, :]


def _vt_spec(tm, per_seq, n_pairs, key_group):
    per_group = key_group * ATT_TILE // tm
    return pl.BlockSpec((1, n_pairs, 1, LANES, tm),
                        lambda i: (i // per_seq, 0, (i % per_seq) // per_group, 0, (i % per_seq) % per_group))


def _vt_shape(batch, seq, n_pairs, key_group):
    group_keys = key_group * ATT_TILE
    return jax.ShapeDtypeStruct((batch, n_pairs, seq // group_keys, LANES, group_keys), BF16)


def _even_proj_kernel(x_ref, mod_ref, g_ref, w_ref, o_ref, km_ref, vta_ref, vtb_ref, cls_ref):
    d = D_MODEL
    mod = mod_ref[0]
    h = _prenorm(x_ref[...], g_ref[...], mod[:, d:2 * d], mod[:, 0:d])
    p = _dot(h.astype(BF16), w_ref[...])
    p_bf16 = p.astype(BF16)
    o_ref[...] = p_bf16
    tm = p.shape[0]
    per_class = tm // DIL_CLASSES
    out_row = lax.broadcasted_iota(jnp.int32, (tm, tm), 0)
    in_row = lax.broadcasted_iota(jnp.int32, (tm, tm), 1)
    source = (out_row % per_class) * DIL_CLASSES + out_row // per_class
    permute = jnp.where(in_row == source, 1.0, 0.0).astype(BF16)
    by_class = _dot(permute, p_bf16[:, 3 * N_HEADS_A * HEAD_DIM:]).astype(BF16)
    for cls in range(DIL_CLASSES):
        cls_ref[0, cls] = by_class[cls * per_class:(cls + 1) * per_class]
    wa = N_HEADS_A * HEAD_DIM
    ka = p[:, wa:2 * wa]
    nb = ka.shape[0] // MOBA_BLOCK
    km_ref[0] = jnp.mean(ka.reshape(nb, MOBA_BLOCK, wa), axis=1)
    _store_vt(vta_ref, p[:, 2 * wa:3 * wa])
    vtb = p[:, 5 * wa:6 * wa].T.astype(BF16)
    t = ATT_TILE
    for pair in range(vtb.shape[0] // LANES):
        for tile in range(vtb.shape[1] // t):
            vtb_ref[0, pair, tile] = vtb[pair * LANES:(pair + 1) * LANES, tile * t:(tile + 1) * t]


def _even_proj(x, mod, g, w_in, seq):
    rows, d = x.shape
    tm = ROW_TILE
    n = w_in.shape[1]
    wa = N_HEADS_A * HEAD_DIM
    per_seq = seq // tm
    n_pairs = N_HEADS_A // 2
    vt_spec = _vt_spec(tm, per_seq, n_pairs, KEY_GROUP_EVEN)
    vt_shape = _vt_shape(rows // seq, seq, n_pairs, KEY_GROUP_EVEN)
    tiles_per_step = tm // ATT_TILE
    vt_tile_spec = pl.BlockSpec((1, n_pairs, tiles_per_step, LANES, ATT_TILE),
                                lambda i: (i // per_seq, 0, i % per_seq, 0, 0))
    return pl.pallas_call(
        _even_proj_kernel,
        grid=(rows // tm,),
        in_specs=[pl.BlockSpec((tm, d), lambda i: (i, 0)),
                  pl.BlockSpec((1, 1, 6 * d), lambda i: (i // per_seq, 0, 0)),
                  pl.BlockSpec((1, d), lambda i: (0, 0)),
                  pl.BlockSpec((d, n), lambda i: (0, 0))],
        out_specs=[pl.BlockSpec((tm, n), lambda i: (i, 0)),
                   pl.BlockSpec((1, tm // MOBA_BLOCK, wa), lambda i: (i, 0, 0)),
                   vt_spec, vt_tile_spec,
                   pl.BlockSpec((1, DIL_CLASSES, tm // DIL_CLASSES, n - 3 * wa),
                                lambda i: (i // per_seq, 0, i % per_seq, 0))],
        out_shape=[jax.ShapeDtypeStruct((rows, n), BF16),
                   jax.ShapeDtypeStruct((rows // tm, tm // MOBA_BLOCK, wa), F32),
                   vt_shape, _vt_shape(rows // seq, seq, n_pairs, 1),
                   jax.ShapeDtypeStruct((rows // seq, DIL_CLASSES, seq // DIL_CLASSES, n - 3 * wa), BF16)],
        compiler_params=_params(1),
        name="even_proj",
    )(x, mod, g, w_in)


def _pair_lanes(x, h):
    pair = h // 2
    return x[:, pair * LANES:(pair + 1) * LANES]


def _split_heads(q_step):
    first = lax.broadcasted_iota(jnp.int32, (q_step.shape[0], LANES), 1) < HEAD_DIM
    heads = []
    for pair in range(q_step.shape[1] // LANES):
        q2 = q_step[:, pair * LANES:(pair + 1) * LANES]
        zero = jnp.zeros_like(q2)
        heads += [jnp.where(first, q2, zero), jnp.where(first, zero, q2)]
    return heads


def _head_rows(vt_ref, g, h, n_tiles):
    return vt_ref[0, h // 2, g, (h % 2) * HEAD_DIM:(h % 2 + 1) * HEAD_DIM, 0:n_tiles * ATT_TILE]


def _pair_scores(keys, qh):
    t = ATT_TILE
    n = keys.shape[0] // t
    scores = []
    for pair in range(len(qh) // 2):
        s = _dot_nt(keys[:, pair * LANES:(pair + 1) * LANES], jnp.concatenate([qh[2 * pair], qh[2 * pair + 1]], axis=0))
        for j in range(2):
            scores.append([s[c * t:(c + 1) * t, j * t:(j + 1) * t] for c in range(n)])
    return scores


def _tile_scores(keys, q_head, per_tile):
    t = ATT_TILE
    n = keys.shape[0] // t
    if per_tile:
        return [_dot_nt(keys[c * t:(c + 1) * t], q_head) for c in range(n)]
    s = _dot_nt(keys, q_head)
    return [s[c * t:(c + 1) * t] for c in range(n)]


def _group_keys(k_ref, g, n_tiles, key_group):
    start = pl.multiple_of(g * key_group * ATT_TILE, key_group * ATT_TILE)
    return k_ref[0, pl.ds(start, n_tiles * ATT_TILE), :]


def _sweep(qi, group, key_group, n_far_groups=None):
    g_own = qi // key_group
    own = [functools.partial(group, g_own, None, n + 1, True) for n in range(key_group)]
    state = lax.switch(qi % key_group, own)
    n_past = g_own if n_far_groups is None else jnp.minimum(g_own, n_far_groups)
    return lax.fori_loop(0, n_past, lambda n, st: group(g_own - 1 - n, st, key_group, False), state)


def _tile_iotas():
    t = ATT_TILE
    return lax.broadcasted_iota(jnp.int32, (t, t), 0), lax.broadcasted_iota(jnp.int32, (t, t), 1)


def _store_step_output(o_ref, outs_t):
    o_ref[0] = jnp.concatenate(outs_t, axis=0).T.astype(BF16)


def _softmax_group(s_tiles, ons, vt_h, acc_ref, h, stats, first):
    maxes = []
    for s, on in zip(s_tiles, ons):
        mx = jnp.max(s, axis=0, keepdims=True)
        maxes.append(mx if on is None else jnp.where(on, mx, -jnp.inf))
    m_new = functools.reduce(jnp.maximum, maxes)
    if not first:
        m_old, l_old = stats
        m_new = jnp.maximum(m_new, m_old)
    ps = []
    l_add = None
    for s, on in zip(s_tiles, ons):
        p = jnp.exp2(s - (m_new if on is None else jnp.where(on, m_new, jnp.inf)))
        p_sum = jnp.sum(p, axis=0, keepdims=True)
        l_add = p_sum if l_add is None else l_add + p_sum
        ps.append(p.astype(BF16))
    pv = _dot(vt_h, jnp.concatenate(ps, axis=0))
    if first:
        acc_ref[h] = pv
        return m_new, l_add
    alpha = jnp.exp2(m_old - m_new)
    acc_ref[h] = alpha * acc_ref[h] + pv
    return m_new, alpha * l_old + l_add


def _moba_select(qh, km_ref, sel_ref, qi):
    t = ATT_TILE
    km_hi, km_lo = _split_bf16(km_ref[0])
    nb = km_hi.shape[0]
    blk = lax.broadcasted_iota(jnp.int32, (nb, t), 0)
    past = blk < qi
    for h in range(HEADS_PER_STEP):
        gate = _dot_nt(_pair_lanes(km_hi, h), qh[h]) + _dot_nt(_pair_lanes(km_lo, h), qh[h])
        gate = jnp.where(past, gate, -jnp.inf)
        beaten = jnp.zeros((nb, t), F32)
        for other in range(nb):
            row = gate[other:other + 1, :]
            wins = jnp.where(row > gate, 1.0, jnp.where(row == gate, jnp.where(blk > other, 1.0, 0.0), 0.0))
            beaten = beaten + wins
        sel_ref[h] = jnp.where(past, jnp.where(beaten < MOBA_TOPK, 1.0, 0.0), 0.0)


def _moba_attn_kernel(q_ref, k_ref, vt_ref, bias_ref, km_ref, o_ref, acc_ref, sel_ref):
    qi = pl.program_id(2)
    qh = _split_heads(q_ref[0])
    n_bias = bias_ref.shape[1]
    _moba_select(qh, km_ref, sel_ref, qi)

    def group(g, stats, n_tiles, first):
        kg = _group_keys(k_ref, g, n_tiles, KEY_GROUP_EVEN)
        new = ()
        scores = _pair_scores(kg, qh)
        for h in range(HEADS_PER_STEP):
            s_tiles = scores[h]
            ons = []
            for c in range(n_tiles):
                j = g * KEY_GROUP_EVEN + c
                dt = n_tiles - 1 - c if first else qi - j
                s_tiles[c] = s_tiles[c] + bias_ref[h, dt if first else jnp.minimum(dt, n_bias - 1)]
                ons.append(None if first and dt == 0 else sel_ref[h, pl.ds(j, 1), :] > 0.5)
            new += _softmax_group(s_tiles, ons, _head_rows(vt_ref, g, h, n_tiles), acc_ref, h,
                                  None if first else stats[2 * h:2 * h + 2], first)
        return new

    stats = _sweep(qi, group, KEY_GROUP_EVEN)
    _store_step_output(o_ref, [acc_ref[h] / stats[2 * h + 1] for h in range(HEADS_PER_STEP)])


def _moba_attn(proj, vt, bias, km, batch, seq):
    t = ATT_TILE
    n_steps = N_HEADS_A // HEADS_PER_STEP
    n_bias = bias.shape[1]
    return pl.pallas_call(
        _moba_attn_kernel,
        grid=(batch, n_steps, seq // t),
        in_specs=[pl.BlockSpec((1, t, STEP_LANES), lambda b, p, i: (b, i, p)),
                  pl.BlockSpec((1, seq, STEP_LANES), lambda b, p, i: (b, 0, n_steps + p)),
                  pl.BlockSpec((1, PAIRS_PER_STEP) + vt.shape[2:], lambda b, p, i: (b, p, 0, 0, 0)),
                  pl.BlockSpec((HEADS_PER_STEP, n_bias, t, t), lambda b, p, i: (p, 0, 0, 0)),
                  pl.BlockSpec((1, seq // MOBA_BLOCK, STEP_LANES), lambda b, p, i: (b, 0, p))],
        out_specs=pl.BlockSpec((1, t, STEP_LANES), lambda b, p, i: (b, i, p)),
        out_shape=jax.ShapeDtypeStruct((batch, seq, N_HEADS_A * HEAD_DIM), BF16),
        scratch_shapes=[pltpu.VMEM((HEADS_PER_STEP, HEAD_DIM, t), F32),
                        pltpu.VMEM((HEADS_PER_STEP, seq // MOBA_BLOCK, t), F32)],
        compiler_params=_params(3),
        name="moba_attn",
    )(proj, proj, vt, bias, km)


def _dilated_far_kernel(q_ref, k_ref, v_ref, bias_ref, o_ref):
    t = ATT_TILE
    qh = _split_heads(q_ref[0, 0])
    keys = k_ref[0, 0]
    vt = v_ref[0, 0].astype(F32).T.astype(BF16)
    accs, maxes, sums = [], [], []
    for h, (s,) in enumerate(_pair_scores(keys, qh)):
        s = s + bias_ref[h, 0]
        m = jnp.max(s, axis=0, keepdims=True)
        p = jnp.exp2(s - jnp.where(m > -jnp.inf, m, 0.0))
        maxes.append(m)
        sums.append(jnp.sum(p, axis=0, keepdims=True))
        accs.append(_dot(vt[h * HEAD_DIM:(h + 1) * HEAD_DIM], p.astype(BF16)))
    pad = jnp.zeros((DIL_FAR_LANES - N_HEADS_B * (HEAD_DIM + 2), t), F32)
    o_ref[0, 0] = jnp.concatenate(accs + maxes + sums + [pad], axis=0).T


def _dilated_far(qkv_cls, bias, batch):
    t = ATT_TILE
    wb = N_HEADS_B * HEAD_DIM
    cols = lambda group: pl.BlockSpec((1, 1, t, wb), lambda b, c: (b, c, 0, group))
    return pl.pallas_call(
        _dilated_far_kernel,
        grid=(batch, DIL_CLASSES),
        in_specs=[cols(0), cols(1), cols(2),
                  pl.BlockSpec((N_HEADS_B, 1, t, t), lambda b, c: (0, 0, 0, 0))],
        out_specs=pl.BlockSpec((1, 1, t, DIL_FAR_LANES), lambda b, c: (b, c, 0, 0)),
        out_shape=jax.ShapeDtypeStruct((batch, DIL_CLASSES, t, DIL_FAR_LANES), F32),
        compiler_params=_params(2),
        name="dilated_far",
    )(qkv_cls, qkv_cls, qkv_cls, bias)


def _dilated_near_kernel(q_ref, k_ref, vt_ref, bias_ref, far_ref, o_ref, acc_ref):
    t = ATT_TILE
    qi = pl.program_id(1)
    qh = _split_heads(q_ref[0])

    def tiles(n_tiles):
        first_tile = qi - (n_tiles - 1)
        keys = k_ref[0, pl.ds(pl.multiple_of(first_tile * t, t), n_tiles * t), :]
        stats = ()
        scores = _pair_scores(keys, qh)
        for h in range(N_HEADS_B):
            s_tiles = [s + bias_ref[h, n_tiles - 1 - c] for c, s in enumerate(scores[h])]
            rows = slice((h % 2) * HEAD_DIM, (h % 2 + 1) * HEAD_DIM)
            vt_h = jnp.concatenate([vt_ref[0, h // 2, first_tile + c, rows, :] for c in range(n_tiles)], axis=1)
            stats += _softmax_group(s_tiles, [None] * n_tiles, vt_h, acc_ref, h, None, True)
        return stats

    stats = lax.switch(jnp.minimum(qi, DIL_NEAR_TILES - 1),
                       [functools.partial(tiles, n + 1) for n in range(DIL_NEAR_TILES)])
    far = far_ref[0].T
    far_m0, far_l0 = N_HEADS_B * HEAD_DIM, N_HEADS_B * HEAD_DIM + N_HEADS_B
    outs = []
    for h in range(N_HEADS_B):
        m_near, l_near = stats[2 * h], stats[2 * h + 1]
        m_far, l_far = far[far_m0 + h:far_m0 + h + 1], far[far_l0 + h:far_l0 + h + 1]
        m = jnp.maximum(m_near, m_far)
        w_near, w_far = jnp.exp2(m_near - m), jnp.exp2(m_far - m)
        acc_far = far[h * HEAD_DIM:(h + 1) * HEAD_DIM]
        outs.append((w_near * acc_ref[h] + w_far * acc_far) / (w_near * l_near + w_far * l_far))
    _store_step_output(o_ref, outs)


def _dilated_near(proj, vt, bias, far, batch, seq, lane0):
    t = ATT_TILE
    wb = N_HEADS_B * HEAD_DIM
    first = lane0 // wb
    return pl.pallas_call(
        _dilated_near_kernel,
        grid=(batch, seq // t),
        in_specs=[pl.BlockSpec((1, t, wb), lambda b, i: (b, i, first)),
                  pl.BlockSpec((1, seq, wb), lambda b, i: (b, 0, first + 1)),
                  pl.BlockSpec((1,) + vt.shape[1:], lambda b, i: (b, 0, 0, 0, 0)),
                  pl.BlockSpec((N_HEADS_B, DIL_NEAR_TILES, t, t), lambda b, i: (0, 0, 0, 0)),
                  pl.BlockSpec((1, t, DIL_FAR_LANES), lambda b, i: (b, i, 0))],
        out_specs=pl.BlockSpec((1, t, wb), lambda b, i: (b, i, 0)),
        out_shape=jax.ShapeDtypeStruct((batch, seq, wb), BF16),
        scratch_shapes=[pltpu.VMEM((N_HEADS_B, HEAD_DIM, t), F32)],
        compiler_params=_params(2),
        name="dilated_near",
    )(proj, proj, vt, bias, far)


def _dilated_attn(proj, qkv_cls, vt_tiles, bias_near, bias_far, batch, seq, lane0):
    far = _dilated_far(qkv_cls, bias_far, batch)
    far = far.transpose(0, 2, 1, 3).reshape(batch, seq, DIL_FAR_LANES)
    return _dilated_near(proj, vt_tiles, bias_near, far, batch, seq, lane0)


def _odd_proj_kernel(x_ref, mod_ref, g_ref, win_ref, gq_ref, gkv_ref, wq_ref, wkv_ref,
                     cq_ref, sq_ref, ck_ref, sk_ref, qm_ref, km_ref, vtm_ref, sb_ref, vts_ref):
    d = D_MODEL
    mod = mod_ref[0]
    h = _prenorm(x_ref[...], g_ref[...], mod[:, d:2 * d], mod[:, 0:d])
    p = _dot(h.astype(BF16), win_ref[...])
    o = MLA_Q_RANK + MLA_KV_RANK
    c_q, c_kv = p[:, 0:MLA_Q_RANK], p[:, MLA_Q_RANK:o]
    k_rope, k_rope_swapped = p[:, o:o + LANES], p[:, o + LANES:o + 2 * LANES]
    sb0 = o + 2 * LANES
    n_qk = 2 * N_HEADS_D * HEAD_DIM
    sb_ref[...] = p[:, sb0:sb0 + n_qk].astype(BF16)
    _store_vt(vts_ref, p[:, sb0 + n_qk:])
    q12 = _dot(_rms(c_q, gq_ref[...]).astype(BF16), wq_ref[...])
    kv = _dot(_rms(c_kv, gkv_ref[...]).astype(BF16), wkv_ref[...])
    k_pe = k_rope * ck_ref[...] + k_rope_swapped * sk_ref[...]
    cq, sq = cq_ref[...], sq_ref[...]
    half = N_HEADS_C * LANES
    for hh in range(N_HEADS_C):
        cols = slice(hh * LANES, (hh + 1) * LANES)
        swapped = slice(half + hh * LANES, half + (hh + 1) * LANES)
        qm_ref[:, cols] = (q12[:, cols] * cq + q12[:, swapped] * sq).astype(BF16)
        km_ref[:, cols] = (kv[:, cols] + k_pe).astype(BF16)
    _store_vt(vtm_ref, kv[:, half:])


def _odd_proj(x, mod, g, w_in, gq, gkv, wq, wkv, tables, seq):
    rows, d = x.shape
    tm = ROW_TILE
    per_seq = seq // tm
    n_in = w_in.shape[1]
    n_sb = 2 * N_HEADS_D * HEAD_DIM
    n_q = N_HEADS_C * LANES
    n_pairs = N_HEADS_C // 2
    const = lambda i: (0, 0)
    table_spec = pl.BlockSpec((tm, LANES), lambda i: (i % per_seq, 0))
    vt_spec = _vt_spec(tm, per_seq, n_pairs, KEY_GROUP_ODD)
    vt_shape = _vt_shape(rows // seq, seq, n_pairs, KEY_GROUP_ODD)
    return pl.pallas_call(
        _odd_proj_kernel,
        grid=(rows // tm,),
        in_specs=[pl.BlockSpec((tm, d), lambda i: (i, 0)),
                  pl.BlockSpec((1, 1, 6 * d), lambda i: (i // per_seq, 0, 0)),
                  pl.BlockSpec((1, d), const),
                  pl.BlockSpec((d, n_in), const),
                  pl.BlockSpec((1, MLA_Q_RANK), const),
                  pl.BlockSpec((1, MLA_KV_RANK), const),
                  pl.BlockSpec(wq.shape, const),
                  pl.BlockSpec(wkv.shape, const),
                  table_spec, table_spec, table_spec, table_spec],
        out_specs=[pl.BlockSpec((tm, n_q), lambda i: (i, 0)),
                   pl.BlockSpec((tm, n_q), lambda i: (i, 0)),
                   vt_spec,
                   pl.BlockSpec((tm, n_sb), lambda i: (i, 0)),
                   vt_spec],
        out_shape=[jax.ShapeDtypeStruct((rows, n_q), BF16),
                   jax.ShapeDtypeStruct((rows, n_q), BF16),
                   vt_shape,
                   jax.ShapeDtypeStruct((rows, n_sb), BF16),
                   vt_shape],
        compiler_params=_params(1),
        name="odd_proj",
    )(x, mod, g, w_in, gq, gkv, wq, wkv, *tables)


def _mla_attn_kernel(q_ref, k_ref, vt_ref, o_ref, acc_ref):
    qi = pl.program_id(2)
    q_step = q_ref[0]
    qh = [q_step[:, h * LANES:(h + 1) * LANES] for h in range(HEADS_PER_STEP)]

    def group(g, stats, n_tiles, first):
        kg = _group_keys(k_ref, g, n_tiles, KEY_GROUP_ODD)
        new = ()
        scores = [_tile_scores(kg[:, h * LANES:(h + 1) * LANES], qh[h], per_tile=True)
                  for h in range(HEADS_PER_STEP)]
        for h in range(HEADS_PER_STEP):
            s_tiles = scores[h]
            if first:
                key, query = _tile_iotas()
                s_tiles[-1] = jnp.where(key <= query, s_tiles[-1], -jnp.inf)
            new += _softmax_group(s_tiles, [None] * n_tiles, _head_rows(vt_ref, g, h, n_tiles), acc_ref, h,
                                  None if first else stats[2 * h:2 * h + 2], first)
        return new

    stats = _sweep(qi, group, KEY_GROUP_ODD)
    _store_step_output(o_ref, [acc_ref[h] / stats[2 * h + 1] for h in range(HEADS_PER_STEP)])


def _mla_attn(qm, km, vt, batch, seq):
    t = ATT_TILE
    n_steps = N_HEADS_C // HEADS_PER_STEP
    qk_lanes = HEADS_PER_STEP * LANES
    return pl.pallas_call(
        _mla_attn_kernel,
        grid=(batch, n_steps, seq // t),
        in_specs=[pl.BlockSpec((1, t, qk_lanes), lambda b, p, i: (b, i, p)),
                  pl.BlockSpec((1, seq, qk_lanes), lambda b, p, i: (b, 0, p)),
                  pl.BlockSpec((1, PAIRS_PER_STEP) + vt.shape[2:], lambda b, p, i: (b, p, 0, 0, 0))],
        out_specs=pl.BlockSpec((1, t, STEP_LANES), lambda b, p, i: (b, i, p)),
        out_shape=jax.ShapeDtypeStruct((batch, seq, N_HEADS_C * MLA_V_DIM), BF16),
        scratch_shapes=[pltpu.VMEM((HEADS_PER_STEP, HEAD_DIM, t), F32)],
        compiler_params=_params(3),
        name="mla_attn",
    )(qm, km, vt)


def _stick_attn_kernel(q_ref, k_ref, vt_ref, o_ref, acc_ref):
    t = ATT_TILE
    qi = pl.program_id(2)
    qh = _split_heads(q_ref[0])
    key = lax.broadcasted_iota(jnp.int32, (t, t), 0)
    other = lax.broadcasted_iota(jnp.int32, (t, t), 1)
    from_here = jnp.where(other >= key, 1.0, 0.0).astype(BF16)
    from_here2 = jnp.concatenate([from_here, from_here], axis=1)

    strict = key < other

    def group(g, carries, n_tiles, first):
        kg = _group_keys(k_ref, g, n_tiles, KEY_GROUP_ODD)
        new = ()
        scores = _pair_scores(kg, qh)
        for h in range(HEADS_PER_STEP):
            z_tiles = scores[h]
            keep_tiles = []
            for z in z_tiles:
                neg_z = -z
                keep_tiles.append(jnp.minimum(neg_z, 0.0) - jnp.log(1.0 + jnp.exp(jnp.minimum(z, neg_z))))
            if first:
                keep_tiles[-1] = jnp.where(strict, keep_tiles[-1], 0.0)
            run = None if first else carries[h]
            a_tiles = [None] * n_tiles
            for c in reversed(range(n_tiles)):
                tail = _dot(from_here2, jnp.concatenate(_split_bf16(keep_tiles[c]), axis=0))
                tile_total = tail[0:1, :]
                if run is not None:
                    tail = tail + run
                a_tiles[c] = jnp.exp(z_tiles[c] + tail)
                run = tile_total if run is None else run + tile_total
            if first:
                a_tiles[-1] = jnp.where(strict, a_tiles[-1], 0.0)
            pv = _dot(_head_rows(vt_ref, g, h, n_tiles), jnp.concatenate(a_tiles, axis=0).astype(BF16))
            acc_ref[h] = pv if first else acc_ref[h] + pv
            new += (run,)
        return new

    _sweep(qi, group, KEY_GROUP_ODD)
    _store_step_output(o_ref, [acc_ref[h] for h in range(HEADS_PER_STEP)])


def _stick_attn(sb, vt, batch, seq):
    t = ATT_TILE
    n_steps = N_HEADS_D // HEADS_PER_STEP
    return pl.pallas_call(
        _stick_attn_kernel,
        grid=(batch, n_steps, seq // t),
        in_specs=[pl.BlockSpec((1, t, STEP_LANES), lambda b, p, i: (b, i, p)),
                  pl.BlockSpec((1, seq, STEP_LANES), lambda b, p, i: (b, 0, n_steps + p)),
                  pl.BlockSpec((1, PAIRS_PER_STEP) + vt.shape[2:], lambda b, p, i: (b, p, 0, 0, 0))],
        out_specs=pl.BlockSpec((1, t, STEP_LANES), lambda b, p, i: (b, i, p)),
        out_shape=jax.ShapeDtypeStruct((batch, seq, N_HEADS_D * HEAD_DIM), BF16),
        scratch_shapes=[pltpu.VMEM((HEADS_PER_STEP, HEAD_DIM, t), F32)],
        compiler_params=_params(3),
        name="stick_attn",
    )(sb, sb, vt)


def _post_attn_kernel(oa_ref, ob_ref, oah_ref, obh_ref, x_ref, xh_ref, mod_ref, gmix_ref, gpre_ref, gpost_ref,
                      wo_ref, wup_ref, cw_ref, cb_ref, wd_ref, o_ref, x1_ref, h_ref, u_ref, acc_ref, *, per_seq):
    d = D_MODEL
    tm = ROW_TILE
    halo = CONV_HALO
    cw = FF_CHUNK
    i = pl.program_id(0)
    mod = mod_ref[0]
    gate_m, shift, scale, gate_f = mod[:, 2 * d:3 * d], mod[:, 3 * d:4 * d], mod[:, 4 * d:5 * d], mod[:, 5 * d:6 * d]
    n_first = oa_ref.shape[1]

    def mixed(oa, ob, x):
        y = _dot(oa, wo_ref[0:n_first, :]) + _dot(ob, wo_ref[n_first:2 * n_first, :])
        return x + gate_m * _rms(y, gmix_ref[...])

    g = gpre_ref[...]
    ahead = _prenorm(mixed(oah_ref[...], obh_ref[...], xh_ref[...]), g, scale, shift)
    ahead = jnp.where(i % per_seq == 0, 0.0, ahead)
    h_ref[0:halo, :] = ahead.astype(BF16)
    x1_ref[...] = mixed(oa_ref[...], ob_ref[...], x_ref[...])
    h_ref[halo:halo + tm, :] = _prenorm(x1_ref[...], g, scale, shift).astype(BF16)

    def up(ch):
        slot = ch % 2
        for half in range(2):
            cols = slice(half * D_FF + ch * cw, half * D_FF + (ch + 1) * cw)
            u_ref[slot, half] = _dot(h_ref[...], wup_ref[:, cols])

    def conv(slot, half, ch):
        cols = slice(half * D_FF + ch * cw, half * D_FF + (ch + 1) * cw)
        w = cw_ref[:, cols]
        out = None
        for tap in range(CONV_WIDTH):
            start = halo - (CONV_WIDTH - 1 - tap)
            term = w[tap:tap + 1, :] * u_ref[slot, half, start:start + tm, :]
            out = term if out is None else out + term
        return out + cb_ref[:, cols]

    def down(ch, act):
        part = _dot(act, wd_ref[ch * cw:(ch + 1) * cw, :])
        if ch == 0:
            acc_ref[...] = part
        else:
            acc_ref[...] += part

    up(0)
    act = None
    for ch in range(N_FF_CHUNKS):
        if ch + 1 < N_FF_CHUNKS:
            up(ch + 1)
        if act is not None:
            down(ch - 1, act)
        slot = ch % 2
        act = (jax.nn.gelu(conv(slot, 0, ch), approximate=True) * conv(slot, 1, ch)).astype(BF16)
    down(N_FF_CHUNKS - 1, act)

    o_ref[...] = x1_ref[...] + gate_f * _rms(acc_ref[...], gpost_ref[...])


def _post_attn(oa, ob, x, mod, g_mix, g_pre, g_post, w_out, w_up, conv_w, conv_b, w_down, seq):
    rows, d = x.shape
    tm = ROW_TILE
    halo = CONV_HALO
    per_seq = seq // tm
    n_first = oa.shape[1]
    const = lambda i: (0, 0)
    tile = lambda i: (i, 0)
    ahead = lambda i: (jnp.maximum(i * (tm // halo) - 1, 0), 0)
    once = dict(pipeline_mode=pl.Buffered(1))
    return pl.pallas_call(
        functools.partial(_post_attn_kernel, per_seq=per_seq),
        grid=(rows // tm,),
        in_specs=[pl.BlockSpec((tm, n_first), tile), pl.BlockSpec((tm, n_first), tile),
                  pl.BlockSpec((halo, n_first), ahead), pl.BlockSpec((halo, n_first), ahead),
                  pl.BlockSpec((tm, d), tile), pl.BlockSpec((halo, d), ahead),
                  pl.BlockSpec((1, 1, 6 * d), lambda i: (i // per_seq, 0, 0)),
                  pl.BlockSpec((1, d), const), pl.BlockSpec((1, d), const), pl.BlockSpec((1, d), const),
                  pl.BlockSpec(w_out.shape, const, **once),
                  pl.BlockSpec(w_up.shape, const, **once),
                  pl.BlockSpec(conv_w.shape, const),
                  pl.BlockSpec(conv_b.shape, const),
                  pl.BlockSpec(w_down.shape, const, **once)],
        out_specs=pl.BlockSpec((tm, d), tile),
        out_shape=jax.ShapeDtypeStruct((rows, d), F32),
        scratch_shapes=[pltpu.VMEM((tm, d), F32),
                        pltpu.VMEM((tm + halo, d), BF16),
                        pltpu.VMEM((2, 2, tm + halo, FF_CHUNK), F32),
                        pltpu.VMEM((tm, d), F32)],
        compiler_params=_params(1),
        name="post_attn",
    )(oa, ob, oa, ob, x, x, mod, g_mix, g_pre, g_post, w_out, w_up, conv_w, conv_b, w_down)


def _rotate_half_cols(w):
    half = w.shape[-1] // 2
    return jnp.concatenate([-w[..., half:], w[..., :half]], axis=-1)


def _pad_cols(w, left, total):
    return jnp.pad(w, ((0, 0), (left, total - left - w.shape[1])))


def _rope_tables(seq):
    inv_freq = 1.0 / (ROPE_THETA ** (jnp.arange(0, MLA_ROPE_DIM, 2, dtype=F32) / MLA_ROPE_DIM))
    ang = jnp.arange(seq, dtype=F32)[:, None] * inv_freq[None, :]
    cos, sin = jnp.cos(ang), jnp.sin(ang)
    cos2 = _pad_cols(jnp.concatenate([cos, cos], axis=1), MLA_NOPE_DIM, LANES)
    sin2 = _pad_cols(jnp.concatenate([sin, sin], axis=1), MLA_NOPE_DIM, LANES)
    scale = (MLA_NOPE_DIM + MLA_ROPE_DIM) ** -0.5
    nope_ones = _pad_cols(jnp.ones((seq, MLA_NOPE_DIM), F32), 0, LANES)
    scale = scale * LOG2_E
    return (scale * (cos2 + nope_ones), scale * sin2, cos2, sin2)


def _odd_weights(w_in, w_uq, w_ukv):
    o = MLA_Q_RANK + MLA_KV_RANK
    w_rope = w_in[:, o:o + MLA_ROPE_DIM]
    scale_d = HEAD_DIM ** -0.5
    wd = N_HEADS_D * HEAD_DIM
    sb0 = o + MLA_ROPE_DIM
    w_in2 = jnp.concatenate([
        w_in[:, :o],
        _pad_cols(w_rope, MLA_NOPE_DIM, LANES),
        _pad_cols(_rotate_half_cols(w_rope), MLA_NOPE_DIM, LANES),
        w_in[:, sb0:sb0 + wd] * scale_d,
        w_in[:, sb0 + wd:],
    ], axis=1).astype(BF16)
    qd = MLA_NOPE_DIM + MLA_ROPE_DIM
    uq = w_uq.reshape(MLA_Q_RANK, N_HEADS_C, qd)
    plain = jnp.pad(uq, ((0, 0), (0, 0), (0, LANES - qd)))
    swapped = jnp.pad(_rotate_half_cols(uq[..., MLA_NOPE_DIM:]),
                      ((0, 0), (0, 0), (MLA_NOPE_DIM, LANES - qd)))
    wq = jnp.concatenate([plain.reshape(MLA_Q_RANK, -1), swapped.reshape(MLA_Q_RANK, -1)], axis=1).astype(BF16)
    ukv = w_ukv.reshape(MLA_KV_RANK, N_HEADS_C, MLA_NOPE_DIM + MLA_V_DIM)
    k_nope = jnp.pad(ukv[..., :MLA_NOPE_DIM], ((0, 0), (0, 0), (0, LANES - MLA_NOPE_DIM)))
    v = ukv[..., MLA_NOPE_DIM:]
    wkv = jnp.concatenate([k_nope.reshape(MLA_KV_RANK, -1), v.reshape(MLA_KV_RANK, -1)], axis=1).astype(BF16)
    return w_in2, wq, wkv


def _even_weights(w_in):
    wa = N_HEADS_A * HEAD_DIM
    scale = HEAD_DIM ** -0.5 * LOG2_E
    col = jnp.arange(w_in.shape[1])
    is_q = (col < wa) | ((col >= 3 * wa) & (col < 4 * wa))
    return (w_in * jnp.where(is_q, scale, 1.0)).astype(BF16)


def kernel(x, c, rel_bias, ada_w, ada_b, mix_pre_g, mix_post_g, ffn_pre_g, ffn_post_g, ab_w_in, ab_w_out,
           cd_w_in, mla_q_norm_g, mla_kv_norm_g, mla_w_uq, mla_w_ukv, cd_w_out, ffn_w_up, ffn_conv_w,
           ffn_conv_b, ffn_w_down):
    batch, seq, d = x.shape
    widest_group = max(KEY_GROUP_EVEN, KEY_GROUP_ODD) * ATT_TILE
    assert d == D_MODEL and seq % widest_group == 0 and (KEY_GROUP_EVEN * ATT_TILE) % ROW_TILE == 0
    assert seq == DIL_CLASSES * ATT_TILE
    rows = batch * seq
    xf = x.reshape(rows, d)

    mods = _mods(c, ada_w, ada_b)
    bias_a = _bias_tiles(rel_bias, 0, N_HEADS_A, MOBA_BIAS_TILES, 1, None, "moba_bias_tiles")
    bias_near = _bias_tiles(rel_bias, N_HEADS_A, N_HEADS_B, DIL_NEAR_TILES, 1, DIL_NEAR_BRANCHES,
                            "dilated_near_bias_tiles")
    bias_far = _bias_tiles(rel_bias, N_HEADS_A, N_HEADS_B, 1, DIL_CLASSES, DIL_FAR_BRANCHES,
                           "dilated_far_bias_tiles")
    tables = _rope_tables(seq)

    for layer in range(DEPTH):
        mod = mods[layer].reshape(batch, 1, 6 * d)
        i = layer // 2
        if layer % 2 == 0:
            proj, km, vta, vtb, qkv_cls = _even_proj(xf, mod, mix_pre_g[layer].reshape(1, d),
                                                     _even_weights(ab_w_in[i]), seq)
            proj = proj.reshape(batch, seq, -1)
            km = km.reshape(batch, seq // MOBA_BLOCK, -1)
            o_first = _moba_attn(proj, vta, bias_a, km, batch, seq)
            o_second = _dilated_attn(proj, qkv_cls, vtb, bias_near, bias_far, batch, seq,
                                     3 * N_HEADS_A * HEAD_DIM)
            w_out = ab_w_out[i]
        else:
            w_in2, wq, wkv = _odd_weights(cd_w_in[i], mla_w_uq[i], mla_w_ukv[i])
            qm, km, vtm, sb, vts = _odd_proj(xf, mod, mix_pre_g[layer].reshape(1, d), w_in2,
                                             mla_q_norm_g[i].reshape(1, -1), mla_kv_norm_g[i].reshape(1, -1),
                                             wq, wkv, tables, seq)
            o_first = _mla_attn(qm.reshape(batch, seq, -1), km.reshape(batch, seq, -1), vtm, batch, seq)
            o_second = _stick_attn(sb.reshape(batch, seq, -1), vts, batch, seq)
            w_out = cd_w_out[i]
        xf = _post_attn(o_first.reshape(rows, -1), o_second.reshape(rows, -1), xf, mod,
                        mix_post_g[layer].reshape(1, d), ffn_pre_g[layer].reshape(1, d),
                        ffn_post_g[layer].reshape(1, d), w_out.astype(BF16), ffn_w_up[layer].astype(BF16),
                        ffn_conv_w[layer], ffn_conv_b[layer].reshape(1, -1), ffn_w_down[layer].astype(BF16), seq)
    return xf.reshape(batch, seq, d)
```

```python
import functools
import math

import jax
import jax.numpy as jnp
from jax import lax
from jax.experimental import pallas as pl
from jax.experimental.pallas import tpu as pltpu

F32 = jnp.float32
BF16 = jnp.bfloat16

D_MODEL = 1024
DEPTH = 4
HEAD_DIM = 64
N_HEADS_A = 8
N_HEADS_B = 8
N_HEADS_C = 8
N_HEADS_D = 8
MOBA_BLOCK = 256
MOBA_TOPK = 3
DILATED_BRANCHES = ((128, 1), (512, 4), (2048, 16))
MLA_Q_RANK = 256
MLA_KV_RANK = 256
MLA_NOPE_DIM = 64
MLA_ROPE_DIM = 32
MLA_V_DIM = 64
ROPE_THETA = 10000.0
REL_BUCKETS = 32
REL_MAX_DIST = 2048
D_FF = 2816
CONV_WIDTH = 3
NORM_EPS = 1e-6
LOG2_E = math.log2(math.e)

LANES = 128
SUBLANES = 8
BF16_SUBLANES = 16
VMEM_LIMIT_BYTES = 56 * 1024 * 1024

ATT_TILE = MOBA_BLOCK
KEY_GROUP_EVEN = 4
KEY_GROUP_ODD = 8
PAIRS_PER_STEP = 2
HEADS_PER_STEP = 2 * PAIRS_PER_STEP
STEP_LANES = PAIRS_PER_STEP * LANES
ROW_TILE = 512
FF_CHUNK = 256
N_FF_CHUNKS = D_FF // FF_CHUNK
CONV_HALO = BF16_SUBLANES
MOBA_BIAS_TILES = REL_MAX_DIST // ATT_TILE + 2
DIL_SPLIT = DILATED_BRANCHES[1][0]
DIL_CLASSES = DILATED_BRANCHES[2][1]
DIL_NEAR_TILES = DIL_SPLIT // ATT_TILE + 1
DIL_FAR_LANES = 5 * LANES
assert DIL_FAR_LANES >= N_HEADS_B * (HEAD_DIM + 2)
DIL_NEAR_BRANCHES = tuple((-1, min(window, DIL_SPLIT), dil) for window, dil in DILATED_BRANCHES)
DIL_FAR_BRANCHES = ((DIL_SPLIT, DILATED_BRANCHES[2][0], DIL_CLASSES),)
assert all(window <= DIL_SPLIT for window, _ in DILATED_BRANCHES[:2]) and DIL_SPLIT % DIL_CLASSES == 0

_NT = (((1,), (1,)), ((), ()))


def _bucket_lower_bounds():
    max_exact = REL_BUCKETS // 2
    ratio = REL_MAX_DIST // max_exact
    n_log = REL_BUCKETS - max_exact
    lows = list(range(max_exact + 1))
    for k in range(1, n_log):
        d = lows[-1]
        while d ** n_log < (max_exact ** n_log) * (ratio ** k):
            d += 1
        lows.append(d)
    return lows


_BUCKET_LOW = _bucket_lower_bounds()


def _dot(a, b):
    return jnp.dot(a, b, preferred_element_type=F32)


def _dot_nt(a, b):
    return lax.dot_general(a, b, _NT, preferred_element_type=F32)


def _split_bf16(x):
    hi = x.astype(BF16)
    lo = (x - hi.astype(F32)).astype(BF16)
    return hi, lo


def _rms(x, g):
    return (x * lax.rsqrt(jnp.mean(x * x, axis=-1, keepdims=True) + NORM_EPS)) * g


def _prenorm(x, g, scale, shift):
    return _rms(x, g) * (1.0 + scale) + shift


def _params(n_grid_dims):
    return pltpu.CompilerParams(dimension_semantics=("arbitrary",) * n_grid_dims,
                                vmem_limit_bytes=VMEM_LIMIT_BYTES)


def _mods_kernel(c_ref, w_ref, b_ref, o_ref):
    c = c_ref[...]
    cond = c * jax.nn.sigmoid(c)
    c_hi, c_lo = _split_bf16(cond)
    w_hi, w_lo = _split_bf16(w_ref[0])
    o_ref[0] = _dot(c_hi, w_hi) + _dot(c_hi, w_lo) + _dot(c_lo, w_hi) + b_ref[0]


def _mods(c, ada_w, ada_b):
    b, d = c.shape
    rows = BF16_SUBLANES
    n_out = ada_w.shape[-1]
    tn = n_out // 4
    c_pad = jnp.zeros((rows, d), F32).at[:b].set(c)
    out = pl.pallas_call(
        _mods_kernel,
        grid=(DEPTH, n_out // tn),
        in_specs=[pl.BlockSpec((rows, d), lambda l, j: (0, 0)),
                  pl.BlockSpec((1, d, tn), lambda l, j: (l, 0, j)),
                  pl.BlockSpec((1, 1, tn), lambda l, j: (l, 0, j))],
        out_specs=pl.BlockSpec((1, rows, tn), lambda l, j: (l, 0, j)),
        out_shape=jax.ShapeDtypeStruct((DEPTH, rows, n_out), F32),
        compiler_params=_params(2),
        name="ada_mods",
    )(c_pad, ada_w, ada_b.reshape(DEPTH, 1, n_out))
    return out[:, :b]


def _bias_tiles_kernel(tab_ref, o_ref, *, head_off, step, branches):
    h = pl.program_id(0) + head_off
    t = ATT_TILE
    offsets = lax.broadcasted_iota(jnp.int32, (SUBLANES, 2 * t), 1) - t
    for d in range(o_ref.shape[1]):
        dist = (d * t + offsets) * step
        val = jnp.full(dist.shape, tab_ref[h, 0], F32)
        for b in range(1, REL_BUCKETS):
            val = jnp.where(dist >= _BUCKET_LOW[b], tab_ref[h, b], val)
        if branches is not None:
            mult = jnp.zeros(dist.shape, F32)
            for beyond, window, dil in branches:
                hit = jnp.where(dist <= window, jnp.where((dist & (dil - 1)) == 0, 1.0, 0.0), 0.0)
                mult = mult + jnp.where(dist > beyond, hit, 0.0)
            log_mult = jnp.where(mult > 2.5, math.log(3.0), jnp.where(mult > 1.5, math.log(2.0), 0.0))
            val = jnp.where(mult > 0.5, val + log_mult, -jnp.inf)
        val = jnp.where(dist >= 0, val * LOG2_E, -jnp.inf)
        strip = jnp.concatenate([val] * (t // SUBLANES), axis=0)
        rotated = pltpu.roll(strip, 0, 1, stride=1, stride_axis=0)
        o_ref[0, d] = rotated[:, t:2 * t]


def _bias_tiles(rel_bias, head_off, n_heads, n_tiles, step, branches, name):
    t = ATT_TILE
    return pl.pallas_call(
        functools.partial(_bias_tiles_kernel, head_off=head_off, step=step, branches=branches),
        grid=(n_heads,),
        in_specs=[pl.BlockSpec(memory_space=pltpu.SMEM)],
        out_specs=pl.BlockSpec((1, n_tiles, t, t), lambda h: (h, 0, 0, 0)),
        out_shape=jax.ShapeDtypeStruct((n_heads, n_tiles, t, t), F32),
        compiler_params=_params(1),
        name=name,
    )(rel_bias)


def _store_vt(vt_ref, v):
    vt = v.T.astype(BF16)
    for p in range(vt.shape[0] // LANES):
        vt_ref[0, p, 0] = vt[p * LANES:(p + 1) * LANES, :]


def _vt_spec(tm, per_seq, n_pairs, key_group):
    per_group = key_group * ATT_TILE // tm
    return pl.BlockSpec((1, n_pairs, 1, LANES, tm),
                        lambda i: (i // per_seq, 0, (i % per_seq) // per_group, 0, (i % per_seq) % per_group))


def _vt_shape(batch, seq, n_pairs, key_group):
    group_keys = key_group * ATT_TILE
    return jax.ShapeDtypeStruct((batch, n_pairs, seq // group_keys, LANES, group_keys), BF16)


def _even_proj_kernel(x_ref, mod_ref, g_ref, w_ref, o_ref, km_ref, vta_ref, vtb_ref, cls_ref):
    d = D_MODEL
    mod = mod_ref[0]
    h = _prenorm(x_ref[...], g_ref[...], mod[:, d:2 * d], mod[:, 0:d])
    p = _dot(h.astype(BF16), w_ref[...])
    p_bf16 = p.astype(BF16)
    o_ref[...] = p_bf16
    tm = p.shape[0]
    per_class = tm // DIL_CLASSES
    out_row = lax.broadcasted_iota(jnp.int32, (tm, tm), 0)
    in_row = lax.broadcasted_iota(jnp.int32, (tm, tm), 1)
    source = (out_row % per_class) * DIL_CLASSES + out_row // per_class
    permute = jnp.where(in_row == source, 1.0, 0.0).astype(BF16)
    by_class = _dot(permute, p_bf16[:, 3 * N_HEADS_A * HEAD_DIM:]).astype(BF16)
    for cls in range(DIL_CLASSES):
        cls_ref[0, cls] = by_class[cls * per_class:(cls + 1) * per_class]
    wa = N_HEADS_A * HEAD_DIM
    ka = p[:, wa:2 * wa]
    nb = ka.shape[0] // MOBA_BLOCK
    km_ref[0] = jnp.mean(ka.reshape(nb, MOBA_BLOCK, wa), axis=1)
    _store_vt(vta_ref, p[:, 2 * wa:3 * wa])
    vtb = p[:, 5 * wa:6 * wa].T.astype(BF16)
    t = ATT_TILE
    for pair in range(vtb.shape[0] // LANES):
        for tile in range(vtb.shape[1] // t):
            vtb_ref[0, pair, tile] = vtb[pair * LANES:(pair + 1) * LANES, tile * t:(tile + 1) * t]


def _even_proj(x, mod, g, w_in, seq):
    rows, d = x.shape
    tm = ROW_TILE
    n = w_in.shape[1]
    wa = N_HEADS_A * HEAD_DIM
    per_seq = seq // tm
    n_pairs = N_HEADS_A // 2
    vt_spec = _vt_spec(tm, per_seq, n_pairs, KEY_GROUP_EVEN)
    vt_shape = _vt_shape(rows // seq, seq, n_pairs, KEY_GROUP_EVEN)
    tiles_per_step = tm // ATT_TILE
    vt_tile_spec = pl.BlockSpec((1, n_pairs, tiles_per_step, LANES, ATT_TILE),
                                lambda i: (i // per_seq, 0, i % per_seq, 0, 0))
    return pl.pallas_call(
        _even_proj_kernel,
        grid=(rows // tm,),
        in_specs=[pl.BlockSpec((tm, d), lambda i: (i, 0)),
                  pl.BlockSpec((1, 1, 6 * d), lambda i: (i // per_seq, 0, 0)),
                  pl.BlockSpec((1, d), lambda i: (0, 0)),
                  pl.BlockSpec((d, n), lambda i: (0, 0))],
        out_specs=[pl.BlockSpec((tm, n), lambda i: (i, 0)),
                   pl.BlockSpec((1, tm // MOBA_BLOCK, wa), lambda i: (i, 0, 0)),
                   vt_spec, vt_tile_spec,
                   pl.BlockSpec((1, DIL_CLASSES, tm // DIL_CLASSES, n - 3 * wa),
                                lambda i: (i // per_seq, 0, i % per_seq, 0))],
        out_shape=[jax.ShapeDtypeStruct((rows, n), BF16),
                   jax.ShapeDtypeStruct((rows // tm, tm // MOBA_BLOCK, wa), F32),
                   vt_shape, _vt_shape(rows // seq, seq, n_pairs, 1),
                   jax.ShapeDtypeStruct((rows // seq, DIL_CLASSES, seq // DIL_CLASSES, n - 3 * wa), BF16)],
        compiler_params=_params(1),
        name="even_proj",
    )(x, mod, g, w_in)


def _pair_lanes(x, h):
    pair = h // 2
    return x[:, pair * LANES:(pair + 1) * LANES]


def _split_heads(q_step):
    first = lax.broadcasted_iota(jnp.int32, (q_step.shape[0], LANES), 1) < HEAD_DIM
    heads = []
    for pair in range(q_step.shape[1] // LANES):
        q2 = q_step[:, pair * LANES:(pair + 1) * LANES]
        zero = jnp.zeros_like(q2)
        heads += [jnp.where(first, q2, zero), jnp.where(first, zero, q2)]
    return heads


def _head_rows(vt_ref, g, h, n_tiles):
    return vt_ref[0, h // 2, g, (h % 2) * HEAD_DIM:(h % 2 + 1) * HEAD_DIM, 0:n_tiles * ATT_TILE]


def _pair_scores(keys, qh):
    t = ATT_TILE
    n = keys.shape[0] // t
    scores = []
    for pair in range(len(qh) // 2):
        s = _dot_nt(keys[:, pair * LANES:(pair + 1) * LANES], jnp.concatenate([qh[2 * pair], qh[2 * pair + 1]], axis=0))
        for j in range(2):
            scores.append([s[c * t:(c + 1) * t, j * t:(j + 1) * t] for c in range(n)])
    return scores


def _tile_scores(keys, q_head, per_tile):
    t = ATT_TILE
    n = keys.shape[0] // t
    if per_tile:
        return [_dot_nt(keys[c * t:(c + 1) * t], q_head) for c in range(n)]
    s = _dot_nt(keys, q_head)
    return [s[c * t:(c + 1) * t] for c in range(n)]


def _group_keys(k_ref, g, n_tiles, key_group):
    start = pl.multiple_of(g * key_group * ATT_TILE, key_group * ATT_TILE)
    return k_ref[0, pl.ds(start, n_tiles * ATT_TILE), :]


def _sweep(qi, group, key_group, n_far_groups=None):
    g_own = qi // key_group
    own = [functools.partial(group, g_own, None, n + 1, True) for n in range(key_group)]
    state = lax.switch(qi % key_group, own)
    n_past = g_own if n_far_groups is None else jnp.minimum(g_own, n_far_groups)
    return lax.fori_loop(0, n_past, lambda n, st: group(g_own - 1 - n, st, key_group, False), state)


def _tile_iotas():
    t = ATT_TILE
    return lax.broadcasted_iota(jnp.int32, (t, t), 0), lax.broadcasted_iota(jnp.int32, (t, t), 1)


def _store_step_output(o_ref, outs_t):
    o_ref[0] = jnp.concatenate(outs_t, axis=0).T.astype(BF16)


def _softmax_group(s_tiles, ons, vt_h, acc_ref, h, stats, first):
    maxes = []
    for s, on in zip(s_tiles, ons):
        mx = jnp.max(s, axis=0, keepdims=True)
        maxes.append(mx if on is None else jnp.where(on, mx, -jnp.inf))
    m_new = functools.reduce(jnp.maximum, maxes)
    if not first:
        m_old, l_old = stats
        m_new = jnp.maximum(m_new, m_old)
    ps = []
    l_add = None
    for s, on in zip(s_tiles, ons):
        p = jnp.exp2(s - (m_new if on is None else jnp.where(on, m_new, jnp.inf)))
        p_sum = jnp.sum(p, axis=0, keepdims=True)
        l_add = p_sum if l_add is None else l_add + p_sum
        ps.append(p.astype(BF16))
    pv = _dot(vt_h, jnp.concatenate(ps, axis=0))
    if first:
        acc_ref[h] = pv
        return m_new, l_add
    alpha = jnp.exp2(m_old - m_new)
    acc_ref[h] = alpha * acc_ref[h] + pv
    return m_new, alpha * l_old + l_add


def _moba_select(qh, km_ref, sel_ref, qi):
    t = ATT_TILE
    km_hi, km_lo = _split_bf16(km_ref[0])
    nb = km_hi.shape[0]
    blk = lax.broadcasted_iota(jnp.int32, (nb, t), 0)
    past = blk < qi
    for h in range(HEADS_PER_STEP):
        gate = _dot_nt(_pair_lanes(km_hi, h), qh[h]) + _dot_nt(_pair_lanes(km_lo, h), qh[h])
        gate = jnp.where(past, gate, -jnp.inf)
        beaten = jnp.zeros((nb, t), F32)
        for other in range(nb):
            row = gate[other:other + 1, :]
            wins = jnp.where(row > gate, 1.0, jnp.where(row == gate, jnp.where(blk > other, 1.0, 0.0), 0.0))
            beaten = beaten + wins
        sel_ref[h] = jnp.where(past, jnp.where(beaten < MOBA_TOPK, 1.0, 0.0), 0.0)


def _moba_attn_kernel(q_ref, k_ref, vt_ref, bias_ref, km_ref, o_ref, acc_ref, sel_ref):
    qi = pl.program_id(2)
    qh = _split_heads(q_ref[0])
    n_bias = bias_ref.shape[1]
    _moba_select(qh, km_ref, sel_ref, qi)

    def group(g, stats, n_tiles, first):
        kg = _group_keys(k_ref, g, n_tiles, KEY_GROUP_EVEN)
        new = ()
        scores = _pair_scores(kg, qh)
        for h in range(HEADS_PER_STEP):
            s_tiles = scores[h]
            ons = []
            for c in range(n_tiles):
                j = g * KEY_GROUP_EVEN + c
                dt = n_tiles - 1 - c if first else qi - j
                s_tiles[c] = s_tiles[c] + bias_ref[h, dt if first else jnp.minimum(dt, n_bias - 1)]
                ons.append(None if first and dt == 0 else sel_ref[h, pl.ds(j, 1), :] > 0.5)
            new += _softmax_group(s_tiles, ons, _head_rows(vt_ref, g, h, n_tiles), acc_ref, h,
                                  None if first else stats[2 * h:2 * h + 2], first)
        return new

    stats = _sweep(qi, group, KEY_GROUP_EVEN)
    _store_step_output(o_ref, [acc_ref[h] / stats[2 * h + 1] for h in range(HEADS_PER_STEP)])


def _moba_attn(proj, vt, bias, km, batch, seq):
    t = ATT_TILE
    n_steps = N_HEADS_A // HEADS_PER_STEP
    n_bias = bias.shape[1]
    return pl.pallas_call(
        _moba_attn_kernel,
        grid=(batch, n_steps, seq // t),
        in_specs=[pl.BlockSpec((1, t, STEP_LANES), lambda b, p, i: (b, i, p)),
                  pl.BlockSpec((1, seq, STEP_LANES), lambda b, p, i: (b, 0, n_steps + p)),
                  pl.BlockSpec((1, PAIRS_PER_STEP) + vt.shape[2:], lambda b, p, i: (b, p, 0, 0, 0)),
                  pl.BlockSpec((HEADS_PER_STEP, n_bias, t, t), lambda b, p, i: (p, 0, 0, 0)),
                  pl.BlockSpec((1, seq // MOBA_BLOCK, STEP_LANES), lambda b, p, i: (b, 0, p))],
        out_specs=pl.BlockSpec((1, t, STEP_LANES), lambda b, p, i: (b, i, p)),
        out_shape=jax.ShapeDtypeStruct((batch, seq, N_HEADS_A * HEAD_DIM), BF16),
        scratch_shapes=[pltpu.VMEM((HEADS_PER_STEP, HEAD_DIM, t), F32),
                        pltpu.VMEM((HEADS_PER_STEP, seq // MOBA_BLOCK, t), F32)],
        compiler_params=_params(3),
        name="moba_attn",
    )(proj, proj, vt, bias, km)


def _dilated_far_kernel(q_ref, k_ref, v_ref, bias_ref, o_ref):
    t = ATT_TILE
    qh = _split_heads(q_ref[0, 0])
    keys = k_ref[0, 0]
    vt = v_ref[0, 0].astype(F32).T.astype(BF16)
    accs, maxes, sums = [], [], []
    for h, (s,) in enumerate(_pair_scores(keys, qh)):
        s = s + bias_ref[h, 0]
        m = jnp.max(s, axis=0, keepdims=True)
        p = jnp.exp2(s - jnp.where(m > -jnp.inf, m, 0.0))
        maxes.append(m)
        sums.append(jnp.sum(p, axis=0, keepdims=True))
        accs.append(_dot(vt[h * HEAD_DIM:(h + 1) * HEAD_DIM], p.astype(BF16)))
    pad = jnp.zeros((DIL_FAR_LANES - N_HEADS_B * (HEAD_DIM + 2), t), F32)
    o_ref[0, 0] = jnp.concatenate(accs + maxes + sums + [pad], axis=0).T


def _dilated_far(qkv_cls, bias, batch):
    t = ATT_TILE
    wb = N_HEADS_B * HEAD_DIM
    cols = lambda group: pl.BlockSpec((1, 1, t, wb), lambda b, c: (b, c, 0, group))
    return pl.pallas_call(
        _dilated_far_kernel,
        grid=(batch, DIL_CLASSES),
        in_specs=[cols(0), cols(1), cols(2),
                  pl.BlockSpec((N_HEADS_B, 1, t, t), lambda b, c: (0, 0, 0, 0))],
        out_specs=pl.BlockSpec((1, 1, t, DIL_FAR_LANES), lambda b, c: (b, c, 0, 0)),
        out_shape=jax.ShapeDtypeStruct((batch, DIL_CLASSES, t, DIL_FAR_LANES), F32),
        compiler_params=_params(2),
        name="dilated_far",
    )(qkv_cls, qkv_cls, qkv_cls, bias)


def _dilated_near_kernel(q_ref, k_ref, vt_ref, bias_ref, far_ref, o_ref, acc_ref):
    t = ATT_TILE
    qi = pl.program_id(1)
    qh = _split_heads(q_ref[0])

    def tiles(n_tiles):
        first_tile = qi - (n_tiles - 1)
        keys = k_ref[0, pl.ds(pl.multiple_of(first_tile * t, t), n_tiles * t), :]
        stats = ()
        scores = _pair_scores(keys, qh)
        for h in range(N_HEADS_B):
            s_tiles = [s + bias_ref[h, n_tiles - 1 - c] for c, s in enumerate(scores[h])]
            rows = slice((h % 2) * HEAD_DIM, (h % 2 + 1) * HEAD_DIM)
            vt_h = jnp.concatenate([vt_ref[0, h // 2, first_tile + c, rows, :] for c in range(n_tiles)], axis=1)
            stats += _softmax_group(s_tiles, [None] * n_tiles, vt_h, acc_ref, h, None, True)
        return stats

    stats = lax.switch(jnp.minimum(qi, DIL_NEAR_TILES - 1),
                       [functools.partial(tiles, n + 1) for n in range(DIL_NEAR_TILES)])
    far = far_ref[0].T
    far_m0, far_l0 = N_HEADS_B * HEAD_DIM, N_HEADS_B * HEAD_DIM + N_HEADS_B
    outs = []
    for h in range(N_HEADS_B):
        m_near, l_near = stats[2 * h], stats[2 * h + 1]
        m_far, l_far = far[far_m0 + h:far_m0 + h + 1], far[far_l0 + h:far_l0 + h + 1]
        m = jnp.maximum(m_near, m_far)
        w_near, w_far = jnp.exp2(m_near - m), jnp.exp2(m_far - m)
        acc_far = far[h * HEAD_DIM:(h + 1) * HEAD_DIM]
        outs.append((w_near * acc_ref[h] + w_far * acc_far) / (w_near * l_near + w_far * l_far))
    _store_step_output(o_ref, outs)


def _dilated_near(proj, vt, bias, far, batch, seq, lane0):
    t = ATT_TILE
    wb = N_HEADS_B * HEAD_DIM
    first = lane0 // wb
    return pl.pallas_call(
        _dilated_near_kernel,
        grid=(batch, seq // t),
        in_specs=[pl.BlockSpec((1, t, wb), lambda b, i: (b, i, first)),
                  pl.BlockSpec((1, seq, wb), lambda b, i: (b, 0, first + 1)),
                  pl.BlockSpec((1,) + vt.shape[1:], lambda b, i: (b, 0, 0, 0, 0)),
                  pl.BlockSpec((N_HEADS_B, DIL_NEAR_TILES, t, t), lambda b, i: (0, 0, 0, 0)),
                  pl.BlockSpec((1, t, DIL_FAR_LANES), lambda b, i: (b, i, 0))],
        out_specs=pl.BlockSpec((1, t, wb), lambda b, i: (b, i, 0)),
        out_shape=jax.ShapeDtypeStruct((batch, seq, wb), BF16),
        scratch_shapes=[pltpu.VMEM((N_HEADS_B, HEAD_DIM, t), F32)],
        compiler_params=_params(2),
        name="dilated_near",
    )(proj, proj, vt, bias, far)


def _dilated_attn(proj, qkv_cls, vt_tiles, bias_near, bias_far, batch, seq, lane0):
    far = _dilated_far(qkv_cls, bias_far, batch)
    far = far.transpose(0, 2, 1, 3).reshape(batch, seq, DIL_FAR_LANES)
    return _dilated_near(proj, vt_tiles, bias_near, far, batch, seq, lane0)


def _odd_proj_kernel(x_ref, mod_ref, g_ref, win_ref, gq_ref, gkv_ref, wq_ref, wkv_ref,
                     cq_ref, sq_ref, ck_ref, sk_ref, qm_ref, km_ref, vtm_ref, sb_ref, vts_ref):
    d = D_MODEL
    mod = mod_ref[0]
    h = _prenorm(x_ref[...], g_ref[...], mod[:, d:2 * d], mod[:, 0:d])
    p = _dot(h.astype(BF16), win_ref[...])
    o = MLA_Q_RANK + MLA_KV_RANK
    c_q, c_kv = p[:, 0:MLA_Q_RANK], p[:, MLA_Q_RANK:o]
    k_rope, k_rope_swapped = p[:, o:o + LANES], p[:, o + LANES:o + 2 * LANES]
    sb0 = o + 2 * LANES
    n_qk = 2 * N_HEADS_D * HEAD_DIM
    sb_ref[...] = p[:, sb0:sb0 + n_qk].astype(BF16)
    _store_vt(vts_ref, p[:, sb0 + n_qk:])
    q12 = _dot(_rms(c_q, gq_ref[...]).astype(BF16), wq_ref[...])
    kv = _dot(_rms(c_kv, gkv_ref[...]).astype(BF16), wkv_ref[...])
    k_pe = k_rope * ck_ref[...] + k_rope_swapped * sk_ref[...]
    cq, sq = cq_ref[...], sq_ref[...]
    half = N_HEADS_C * LANES
    for hh in range(N_HEADS_C):
        cols = slice(hh * LANES, (hh + 1) * LANES)
        swapped = slice(half + hh * LANES, half + (hh + 1) * LANES)
        qm_ref[:, cols] = (q12[:, cols] * cq + q12[:, swapped] * sq).astype(BF16)
        km_ref[:, cols] = (kv[:, cols] + k_pe).astype(BF16)
    _store_vt(vtm_ref, kv[:, half:])


def _odd_proj(x, mod, g, w_in, gq, gkv, wq, wkv, tables, seq):
    rows, d = x.shape
    tm = ROW_TILE
    per_seq = seq // tm
    n_in = w_in.shape[1]
    n_sb = 2 * N_HEADS_D * HEAD_DIM
    n_q = N_HEADS_C * LANES
    n_pairs = N_HEADS_C // 2
    const = lambda i: (0, 0)
    table_spec = pl.BlockSpec((tm, LANES), lambda i: (i % per_seq, 0))
    vt_spec = _vt_spec(tm, per_seq, n_pairs, KEY_GROUP_ODD)
    vt_shape = _vt_shape(rows // seq, seq, n_pairs, KEY_GROUP_ODD)
    return pl.pallas_call(
        _odd_proj_kernel,
        grid=(rows // tm,),
        in_specs=[pl.BlockSpec((tm, d), lambda i: (i, 0)),
                  pl.BlockSpec((1, 1, 6 * d), lambda i: (i // per_seq, 0, 0)),
                  pl.BlockSpec((1, d), const),
                  pl.BlockSpec((d, n_in), const),
                  pl.BlockSpec((1, MLA_Q_RANK), const),
                  pl.BlockSpec((1, MLA_KV_RANK), const),
                  pl.BlockSpec(wq.shape, const),
                  pl.BlockSpec(wkv.shape, const),
                  table_spec, table_spec, table_spec, table_spec],
        out_specs=[pl.BlockSpec((tm, n_q), lambda i: (i, 0)),
                   pl.BlockSpec((tm, n_q), lambda i: (i, 0)),
                   vt_spec,
                   pl.BlockSpec((tm, n_sb), lambda i: (i, 0)),
                   vt_spec],
        out_shape=[jax.ShapeDtypeStruct((rows, n_q), BF16),
                   jax.ShapeDtypeStruct((rows, n_q), BF16),
                   vt_shape,
                   jax.ShapeDtypeStruct((rows, n_sb), BF16),
                   vt_shape],
        compiler_params=_params(1),
        name="odd_proj",
    )(x, mod, g, w_in, gq, gkv, wq, wkv, *tables)


def _mla_attn_kernel(q_ref, k_ref, vt_ref, o_ref, acc_ref):
    qi = pl.program_id(2)
    q_step = q_ref[0]
    qh = [q_step[:, h * LANES:(h + 1) * LANES] for h in range(HEADS_PER_STEP)]

    def group(g, stats, n_tiles, first):
        kg = _group_keys(k_ref, g, n_tiles, KEY_GROUP_ODD)
        new = ()
        scores = [_tile_scores(kg[:, h * LANES:(h + 1) * LANES], qh[h], per_tile=True)
                  for h in range(HEADS_PER_STEP)]
        for h in range(HEADS_PER_STEP):
            s_tiles = scores[h]
            if first:
                key, query = _tile_iotas()
                s_tiles[-1] = jnp.where(key <= query, s_tiles[-1], -jnp.inf)
            new += _softmax_group(s_tiles, [None] * n_tiles, _head_rows(vt_ref, g, h, n_tiles), acc_ref, h,
                                  None if first else stats[2 * h:2 * h + 2], first)
        return new

    stats = _sweep(qi, group, KEY_GROUP_ODD)
    _store_step_output(o_ref, [acc_ref[h] / stats[2 * h + 1] for h in range(HEADS_PER_STEP)])


def _mla_attn(qm, km, vt, batch, seq):
    t = ATT_TILE
    n_steps = N_HEADS_C // HEADS_PER_STEP
    qk_lanes = HEADS_PER_STEP * LANES
    return pl.pallas_call(
        _mla_attn_kernel,
        grid=(batch, n_steps, seq // t),
        in_specs=[pl.BlockSpec((1, t, qk_lanes), lambda b, p, i: (b, i, p)),
                  pl.BlockSpec((1, seq, qk_lanes), lambda b, p, i: (b, 0, p)),
                  pl.BlockSpec((1, PAIRS_PER_STEP) + vt.shape[2:], lambda b, p, i: (b, p, 0, 0, 0))],
        out_specs=pl.BlockSpec((1, t, STEP_LANES), lambda b, p, i: (b, i, p)),
        out_shape=jax.ShapeDtypeStruct((batch, seq, N_HEADS_C * MLA_V_DIM), BF16),
        scratch_shapes=[pltpu.VMEM((HEADS_PER_STEP, HEAD_DIM, t), F32)],
        compiler_params=_params(3),
        name="mla_attn",
    )(qm, km, vt)


def _stick_attn_kernel(q_ref, k_ref, vt_ref, o_ref, acc_ref):
    t = ATT_TILE
    qi = pl.program_id(2)
    qh = _split_heads(q_ref[0])
    key = lax.broadcasted_iota(jnp.int32, (t, t), 0)
    other = lax.broadcasted_iota(jnp.int32, (t, t), 1)
    from_here = jnp.where(other >= key, 1.0, 0.0).astype(BF16)
    from_here2 = jnp.concatenate([from_here, from_here], axis=1)

    strict = key < other

    def group(g, carries, n_tiles, first):
        kg = _group_keys(k_ref, g, n_tiles, KEY_GROUP_ODD)
        new = ()
        scores = _pair_scores(kg, qh)
        for h in range(HEADS_PER_STEP):
            z_tiles = scores[h]
            keep_tiles = []
            for z in z_tiles:
                neg_z = -z
                keep_tiles.append(jnp.minimum(neg_z, 0.0) - jnp.log(1.0 + jnp.exp(jnp.minimum(z, neg_z))))
            if first:
                keep_tiles[-1] = jnp.where(strict, keep_tiles[-1], 0.0)
            run = None if first else carries[h]
            a_tiles = [None] * n_tiles
            for c in reversed(range(n_tiles)):
                tail = _dot(from_here2, jnp.concatenate(_split_bf16(keep_tiles[c]), axis=0))
                tile_total = tail[0:1, :]
                if run is not None:
                    tail = tail + run
                a_tiles[c] = jnp.exp(z_tiles[c] + tail)
                run = tile_total if run is None else run + tile_total
            if first:
                a_tiles[-1] = jnp.where(strict, a_tiles[-1], 0.0)
            pv = _dot(_head_rows(vt_ref, g, h, n_tiles), jnp.concatenate(a_tiles, axis=0).astype(BF16))
            acc_ref[h] = pv if first else acc_ref[h] + pv
            new += (run,)
        return new

    _sweep(qi, group, KEY_GROUP_ODD)
    _store_step_output(o_ref, [acc_ref[h] for h in range(HEADS_PER_STEP)])


def _stick_attn(sb, vt, batch, seq):
    t = ATT_TILE
    n_steps = N_HEADS_D // HEADS_PER_STEP
    return pl.pallas_call(
        _stick_attn_kernel,
        grid=(batch, n_steps, seq // t),
        in_specs=[pl.BlockSpec((1, t, STEP_LANES), lambda b, p, i: (b, i, p)),
                  pl.BlockSpec((1, seq, STEP_LANES), lambda b, p, i: (b, 0, n_steps + p)),
                  pl.BlockSpec((1, PAIRS_PER_STEP) + vt.shape[2:], lambda b, p, i: (b, p, 0, 0, 0))],
        out_specs=pl.BlockSpec((1, t, STEP_LANES), lambda b, p, i: (b, i, p)),
        out_shape=jax.ShapeDtypeStruct((batch, seq, N_HEADS_D * HEAD_DIM), BF16),
        scratch_shapes=[pltpu.VMEM((HEADS_PER_STEP, HEAD_DIM, t), F32)],
        compiler_params=_params(3),
        name="stick_attn",
    )(sb, sb, vt)


def _post_attn_kernel(oa_ref, ob_ref, oah_ref, obh_ref, x_ref, xh_ref, mod_ref, gmix_ref, gpre_ref, gpost_ref,
                      wo_ref, wup_ref, cw_ref, cb_ref, wd_ref, o_ref, x1_ref, h_ref, u_ref, acc_ref, *, per_seq):
    d = D_MODEL
    tm = ROW_TILE
    halo = CONV_HALO
    cw = FF_CHUNK
    i = pl.program_id(0)
    mod = mod_ref[0]
    gate_m, shift, scale, gate_f = mod[:, 2 * d:3 * d], mod[:, 3 * d:4 * d], mod[:, 4 * d:5 * d], mod[:, 5 * d:6 * d]
    n_first = oa_ref.shape[1]

    def mixed(oa, ob, x):
        y = _dot(oa, wo_ref[0:n_first, :]) + _dot(ob, wo_ref[n_first:2 * n_first, :])
        return x + gate_m * _rms(y, gmix_ref[...])

    g = gpre_ref[...]
    ahead = _prenorm(mixed(oah_ref[...], obh_ref[...], xh_ref[...]), g, scale, shift)
    ahead = jnp.where(i % per_seq == 0, 0.0, ahead)
    h_ref[0:halo, :] = ahead.astype(BF16)
    x1_ref[...] = mixed(oa_ref[...], ob_ref[...], x_ref[...])
    h_ref[halo:halo + tm, :] = _prenorm(x1_ref[...], g, scale, shift).astype(BF16)

    def up(ch):
        slot = ch % 2
        for half in range(2):
            cols = slice(half * D_FF + ch * cw, half * D_FF + (ch + 1) * cw)
            u_ref[slot, half] = _dot(h_ref[...], wup_ref[:, cols])

    def conv(slot, half, ch):
        cols = slice(half * D_FF + ch * cw, half * D_FF + (ch + 1) * cw)
        w = cw_ref[:, cols]
        out = None
        for tap in range(CONV_WIDTH):
            start = halo - (CONV_WIDTH - 1 - tap)
            term = w[tap:tap + 1, :] * u_ref[slot, half, start:start + tm, :]
            out = term if out is None else out + term
        return out + cb_ref[:, cols]

    def down(ch, act):
        part = _dot(act, wd_ref[ch * cw:(ch + 1) * cw, :])
        if ch == 0:
            acc_ref[...] = part
        else:
            acc_ref[...] += part

    up(0)
    act = None
    for ch in range(N_FF_CHUNKS):
        if ch + 1 < N_FF_CHUNKS:
            up(ch + 1)
        if act is not None:
            down(ch - 1, act)
        slot = ch % 2
        act = (jax.nn.gelu(conv(slot, 0, ch), approximate=True) * conv(slot, 1, ch)).astype(BF16)
    down(N_FF_CHUNKS - 1, act)

    o_ref[...] = x1_ref[...] + gate_f * _rms(acc_ref[...], gpost_ref[...])


def _post_attn(oa, ob, x, mod, g_mix, g_pre, g_post, w_out, w_up, conv_w, conv_b, w_down, seq):
    rows, d = x.shape
    tm = ROW_TILE
    halo = CONV_HALO
    per_seq = seq // tm
    n_first = oa.shape[1]
    const = lambda i: (0, 0)
    tile = lambda i: (i, 0)
    ahead = lambda i: (jnp.maximum(i * (tm // halo) - 1, 0), 0)
    once = dict(pipeline_mode=pl.Buffered(1))
    return pl.pallas_call(
        functools.partial(_post_attn_kernel, per_seq=per_seq),
        grid=(rows // tm,),
        in_specs=[pl.BlockSpec((tm, n_first), tile), pl.BlockSpec((tm, n_first), tile),
                  pl.BlockSpec((halo, n_first), ahead), pl.BlockSpec((halo, n_first), ahead),
                  pl.BlockSpec((tm, d), tile), pl.BlockSpec((halo, d), ahead),
                  pl.BlockSpec((1, 1, 6 * d), lambda i: (i // per_seq, 0, 0)),
                  pl.BlockSpec((1, d), const), pl.BlockSpec((1, d), const), pl.BlockSpec((1, d), const),
                  pl.BlockSpec(w_out.shape, const, **once),
                  pl.BlockSpec(w_up.shape, const, **once),
                  pl.BlockSpec(conv_w.shape, const),
                  pl.BlockSpec(conv_b.shape, const),
                  pl.BlockSpec(w_down.shape, const, **once)],
        out_specs=pl.BlockSpec((tm, d), tile),
        out_shape=jax.ShapeDtypeStruct((rows, d), F32),
        scratch_shapes=[pltpu.VMEM((tm, d), F32),
                        pltpu.VMEM((tm + halo, d), BF16),
                        pltpu.VMEM((2, 2, tm + halo, FF_CHUNK), F32),
                        pltpu.VMEM((tm, d), F32)],
        compiler_params=_params(1),
        name="post_attn",
    )(oa, ob, oa, ob, x, x, mod, g_mix, g_pre, g_post, w_out, w_up, conv_w, conv_b, w_down)


def _rotate_half_cols(w):
    half = w.shape[-1] // 2
    return jnp.concatenate([-w[..., half:], w[..., :half]], axis=-1)


def _pad_cols(w, left, total):
    return jnp.pad(w, ((0, 0), (left, total - left - w.shape[1])))


def _rope_tables(seq):
    inv_freq = 1.0 / (ROPE_THETA ** (jnp.arange(0, MLA_ROPE_DIM, 2, dtype=F32) / MLA_ROPE_DIM))
    ang = jnp.arange(seq, dtype=F32)[:, None] * inv_freq[None, :]
    cos, sin = jnp.cos(ang), jnp.sin(ang)
    cos2 = _pad_cols(jnp.concatenate([cos, cos], axis=1), MLA_NOPE_DIM, LANES)
    sin2 = _pad_cols(jnp.concatenate([sin, sin], axis=1), MLA_NOPE_DIM, LANES)
    scale = (MLA_NOPE_DIM + MLA_ROPE_DIM) ** -0.5
    nope_ones = _pad_cols(jnp.ones((seq, MLA_NOPE_DIM), F32), 0, LANES)
    scale = scale * LOG2_E
    return (scale * (cos2 + nope_ones), scale * sin2, cos2, sin2)


def _odd_weights(w_in, w_uq, w_ukv):
    o = MLA_Q_RANK + MLA_KV_RANK
    w_rope = w_in[:, o:o + MLA_ROPE_DIM]
    scale_d = HEAD_DIM ** -0.5
    wd = N_HEADS_D * HEAD_DIM
    sb0 = o + MLA_ROPE_DIM
    w_in2 = jnp.concatenate([
        w_in[:, :o],
        _pad_cols(w_rope, MLA_NOPE_DIM, LANES),
        _pad_cols(_rotate_half_cols(w_rope), MLA_NOPE_DIM, LANES),
        w_in[:, sb0:sb0 + wd] * scale_d,
        w_in[:, sb0 + wd:],
    ], axis=1).astype(BF16)
    qd = MLA_NOPE_DIM + MLA_ROPE_DIM
    uq = w_uq.reshape(MLA_Q_RANK, N_HEADS_C, qd)
    plain = jnp.pad(uq, ((0, 0), (0, 0), (0, LANES - qd)))
    swapped = jnp.pad(_rotate_half_cols(uq[..., MLA_NOPE_DIM:]),
                      ((0, 0), (0, 0), (MLA_NOPE_DIM, LANES - qd)))
    wq = jnp.concatenate([plain.reshape(MLA_Q_RANK, -1), swapped.reshape(MLA_Q_RANK, -1)], axis=1).astype(BF16)
    ukv = w_ukv.reshape(MLA_KV_RANK, N_HEADS_C, MLA_NOPE_DIM + MLA_V_DIM)
    k_nope = jnp.pad(ukv[..., :MLA_NOPE_DIM], ((0, 0), (0, 0), (0, LANES - MLA_NOPE_DIM)))
    v = ukv[..., MLA_NOPE_DIM:]
    wkv = jnp.concatenate([k_nope.reshape(MLA_KV_RANK, -1), v.reshape(MLA_KV_RANK, -1)], axis=1).astype(BF16)
    return w_in2, wq, wkv


def _even_weights(w_in):
    wa = N_HEADS_A * HEAD_DIM
    scale = HEAD_DIM ** -0.5 * LOG2_E
    col = jnp.arange(w_in.shape[1])
    is_q = (col < wa) | ((col >= 3 * wa) & (col < 4 * wa))
    return (w_in * jnp.where(is_q, scale, 1.0)).astype(BF16)


def kernel(x, c, rel_bias, ada_w, ada_b, mix_pre_g, mix_post_g, ffn_pre_g, ffn_post_g, ab_w_in, ab_w_out,
           cd_w_in, mla_q_norm_g, mla_kv_norm_g, mla_w_uq, mla_w_ukv, cd_w_out, ffn_w_up, ffn_conv_w,
           ffn_conv_b, ffn_w_down):
    batch, seq, d = x.shape
    widest_group = max(KEY_GROUP_EVEN, KEY_GROUP_ODD) * ATT_TILE
    assert d == D_MODEL and seq % widest_group == 0 and (KEY_GROUP_EVEN * ATT_TILE) % ROW_TILE == 0
    assert seq == DIL_CLASSES * ATT_TILE
    rows = batch * seq
    xf = x.reshape(rows, d)

    mods = _mods(c, ada_w, ada_b)
    bias_a = _bias_tiles(rel_bias, 0, N_HEADS_A, MOBA_BIAS_TILES, 1, None, "moba_bias_tiles")
    bias_near = _bias_tiles(rel_bias, N_HEADS_A, N_HEADS_B, DIL_NEAR_TILES, 1, DIL_NEAR_BRANCHES,
                            "dilated_near_bias_tiles")
    bias_far = _bias_tiles(rel_bias, N_HEADS_A, N_HEADS_B, 1, DIL_CLASSES, DIL_FAR_BRANCHES,
                           "dilated_far_bias_tiles")
    tables = _rope_tables(seq)

    for layer in range(DEPTH):
        mod = mods[layer].reshape(batch, 1, 6 * d)
        i = layer // 2
        if layer % 2 == 0:
            proj, km, vta, vtb, qkv_cls = _even_proj(xf, mod, mix_pre_g[layer].reshape(1, d),
                                                     _even_weights(ab_w_in[i]), seq)
            proj = proj.reshape(batch, seq, -1)
            km = km.reshape(batch, seq // MOBA_BLOCK, -1)
            o_first = _moba_attn(proj, vta, bias_a, km, batch, seq)
            o_second = _dilated_attn(proj, qkv_cls, vtb, bias_near, bias_far, batch, seq,
                                     3 * N_HEADS_A * HEAD_DIM)
            w_out = ab_w_out[i]
        else:
            w_in2, wq, wkv = _odd_weights(cd_w_in[i], mla_w_uq[i], mla_w_ukv[i])
            qm, km, vtm, sb, vts = _odd_proj(xf, mod, mix_pre_g[layer].reshape(1, d), w_in2,
                                             mla_q_norm_g[i].reshape(1, -1), mla_kv_norm_g[i].reshape(1, -1),
                                             wq, wkv, tables, seq)
            o_first = _mla_attn(qm.reshape(batch, seq, -1), km.reshape(batch, seq, -1), vtm, batch, seq)
            o_second = _stick_attn(sb.reshape(batch, seq, -1), vts, batch, seq)
            w_out = cd_w_out[i]
        xf = _post_attn(o_first.reshape(rows, -1), o_second.reshape(rows, -1), xf, mod,
                        mix_post_g[layer].reshape(1, d), ffn_pre_g[layer].reshape(1, d),
                        ffn_post_g[layer].reshape(1, d), w_out.astype(BF16), ffn_w_up[layer].astype(BF16),
                        ffn_conv_w[layer], ffn_conv_b[layer].reshape(1, -1), ffn_w_down[layer].astype(BF16), seq)
    return xf.reshape(batch, seq, d)
```

```python
import functools
import math

import jax
import jax.numpy as jnp
from jax import lax
from jax.experimental import pallas as pl
from jax.experimental.pallas import tpu as pltpu

F32 = jnp.float32
BF16 = jnp.bfloat16

D_MODEL = 1024
DEPTH = 4
HEAD_DIM = 64
N_HEADS_A = 8
N_HEADS_B = 8
N_HEADS_C = 8
N_HEADS_D = 8
MOBA_BLOCK = 256
MOBA_TOPK = 3
DILATED_BRANCHES = ((128, 1), (512, 4), (2048, 16))
MLA_Q_RANK = 256
MLA_KV_RANK = 256
MLA_NOPE_DIM = 64
MLA_ROPE_DIM = 32
MLA_V_DIM = 64
ROPE_THETA = 10000.0
REL_BUCKETS = 32
REL_MAX_DIST = 2048
D_FF = 2816
CONV_WIDTH = 3
NORM_EPS = 1e-6
LOG2_E = math.log2(math.e)

LANES = 128
SUBLANES = 8
BF16_SUBLANES = 16
VMEM_LIMIT_BYTES = 56 * 1024 * 1024

ATT_TILE = MOBA_BLOCK
KEY_GROUP_EVEN = 4
KEY_GROUP_ODD = 8
PAIRS_PER_STEP = 2
HEADS_PER_STEP = 2 * PAIRS_PER_STEP
STEP_LANES = PAIRS_PER_STEP * LANES
ROW_TILE = 512
FF_CHUNK = 256
N_FF_CHUNKS = D_FF // FF_CHUNK
CONV_HALO = BF16_SUBLANES
MOBA_BIAS_TILES = REL_MAX_DIST // ATT_TILE + 2
DIL_SPLIT = DILATED_BRANCHES[1][0]
DIL_CLASSES = DILATED_BRANCHES[2][1]
DIL_NEAR_TILES = DIL_SPLIT // ATT_TILE + 1
DIL_FAR_LANES = 5 * LANES
assert DIL_FAR_LANES >= N_HEADS_B * (HEAD_DIM + 2)
DIL_NEAR_BRANCHES = tuple((-1, min(window, DIL_SPLIT), dil) for window, dil in DILATED_BRANCHES)
DIL_FAR_BRANCHES = ((DIL_SPLIT, DILATED_BRANCHES[2][0], DIL_CLASSES),)
assert all(window <= DIL_SPLIT for window, _ in DILATED_BRANCHES[:2]) and DIL_SPLIT % DIL_CLASSES == 0

_NT = (((1,), (1,)), ((), ()))


def _bucket_lower_bounds():
    max_exact = REL_BUCKETS // 2
    ratio = REL_MAX_DIST // max_exact
    n_log = REL_BUCKETS - max_exact
    lows = list(range(max_exact + 1))
    for k in range(1, n_log):
        d = lows[-1]
        while d ** n_log < (max_exact ** n_log) * (ratio ** k):
            d += 1
        lows.append(d)
    return lows


_BUCKET_LOW = _bucket_lower_bounds()


def _dot(a, b):
    return jnp.dot(a, b, preferred_element_type=F32)


def _dot_nt(a, b):
    return lax.dot_general(a, b, _NT, preferred_element_type=F32)


def _split_bf16(x):
    hi = x.astype(BF16)
    lo = (x - hi.astype(F32)).astype(BF16)
    return hi, lo


def _rms(x, g):
    return (x * lax.rsqrt(jnp.mean(x * x, axis=-1, keepdims=True) + NORM_EPS)) * g


def _prenorm(x, g, scale, shift):
    return _rms(x, g) * (1.0 + scale) + shift


def _params(n_grid_dims):
    return pltpu.CompilerParams(dimension_semantics=("arbitrary",) * n_grid_dims,
                                vmem_limit_bytes=VMEM_LIMIT_BYTES)


def _mods_kernel(c_ref, w_ref, b_ref, o_ref):
    c = c_ref[...]
    cond = c * jax.nn.sigmoid(c)
    c_hi, c_lo = _split_bf16(cond)
    w_hi, w_lo = _split_bf16(w_ref[0])
    o_ref[0] = _dot(c_hi, w_hi) + _dot(c_hi, w_lo) + _dot(c_lo, w_hi) + b_ref[0]


def _mods(c, ada_w, ada_b):
    b, d = c.shape
    rows = BF16_SUBLANES
    n_out = ada_w.shape[-1]
    tn = n_out // 4
    c_pad = jnp.zeros((rows, d), F32).at[:b].set(c)
    out = pl.pallas_call(
        _mods_kernel,
        grid=(DEPTH, n_out // tn),
        in_specs=[pl.BlockSpec((rows, d), lambda l, j: (0, 0)),
                  pl.BlockSpec((1, d, tn), lambda l, j: (l, 0, j)),
                  pl.BlockSpec((1, 1, tn), lambda l, j: (l, 0, j))],
        out_specs=pl.BlockSpec((1, rows, tn), lambda l, j: (l, 0, j)),
        out_shape=jax.ShapeDtypeStruct((DEPTH, rows, n_out), F32),
        compiler_params=_params(2),
        name="ada_mods",
    )(c_pad, ada_w, ada_b.reshape(DEPTH, 1, n_out))
    return out[:, :b]


def _bias_tiles_kernel(tab_ref, o_ref, *, head_off, step, branches):
    h = pl.program_id(0) + head_off
    t = ATT_TILE
    offsets = lax.broadcasted_iota(jnp.int32, (SUBLANES, 2 * t), 1) - t
    for d in range(o_ref.shape[1]):
        dist = (d * t + offsets) * step
        val = jnp.full(dist.shape, tab_ref[h, 0], F32)
        for b in range(1, REL_BUCKETS):
            val = jnp.where(dist >= _BUCKET_LOW[b], tab_ref[h, b], val)
        if branches is not None:
            mult = jnp.zeros(dist.shape, F32)
            for beyond, window, dil in branches:
                hit = jnp.where(dist <= window, jnp.where((dist & (dil - 1)) == 0, 1.0, 0.0), 0.0)
                mult = mult + jnp.where(dist > beyond, hit, 0.0)
            log_mult = jnp.where(mult > 2.5, math.log(3.0), jnp.where(mult > 1.5, math.log(2.0), 0.0))
            val = jnp.where(mult > 0.5, val + log_mult, -jnp.inf)
        val = jnp.where(dist >= 0, val * LOG2_E, -jnp.inf)
        strip = jnp.concatenate([val] * (t // SUBLANES), axis=0)
        rotated = pltpu.roll(strip, 0, 1, stride=1, stride_axis=0)
        o_ref[0, d] = rotated[:, t:2 * t]


def _bias_tiles(rel_bias, head_off, n_heads, n_tiles, step, branches, name):
    t = ATT_TILE
    return pl.pallas_call(
        functools.partial(_bias_tiles_kernel, head_off=head_off, step=step, branches=branches),
        grid=(n_heads,),
        in_specs=[pl.BlockSpec(memory_space=pltpu.SMEM)],
        out_specs=pl.BlockSpec((1, n_tiles, t, t), lambda h: (h, 0, 0, 0)),
        out_shape=jax.ShapeDtypeStruct((n_heads, n_tiles, t, t), F32),
        compiler_params=_params(1),
        name=name,
    )(rel_bias)


def _store_vt(vt_ref, v):
    vt = v.T.astype(BF16)
    for p in range(vt.shape[0] // LANES):
        vt_ref[0, p, 0] = vt[p * LANES:(p + 1) * LANES, :]


def _vt_spec(tm, per_seq, n_pairs, key_group):
    per_group = key_group * ATT_TILE // tm
    return pl.BlockSpec((1, n_pairs, 1, LANES, tm),
                        lambda i: (i // per_seq, 0, (i % per_seq) // per_group, 0, (i % per_seq) % per_group))


def _vt_shape(batch, seq, n_pairs, key_group):
    group_keys = key_group * ATT_TILE
    return jax.ShapeDtypeStruct((batch, n_pairs, seq // group_keys, LANES, group_keys), BF16)


def _even_proj_kernel(x_ref, mod_ref, g_ref, w_ref, o_ref, km_ref, vta_ref, vtb_ref, cls_ref):
    d = D_MODEL
    mod = mod_ref[0]
    h = _prenorm(x_ref[...], g_ref[...], mod[:, d:2 * d], mod[:, 0:d])
    p = _dot(h.astype(BF16), w_ref[...])
    p_bf16 = p.astype(BF16)
    o_ref[...] = p_bf16
    tm = p.shape[0]
    per_class = tm // DIL_CLASSES
    out_row = lax.broadcasted_iota(jnp.int32, (tm, tm), 0)
    in_row = lax.broadcasted_iota(jnp.int32, (tm, tm), 1)
    source = (out_row % per_class) * DIL_CLASSES + out_row // per_class
    permute = jnp.where(in_row == source, 1.0, 0.0).astype(BF16)
    by_class = _dot(permute, p_bf16[:, 3 * N_HEADS_A * HEAD_DIM:]).astype(BF16)
    for cls in range(DIL_CLASSES):
        cls_ref[0, cls] = by_class[cls * per_class:(cls + 1) * per_class]
    wa = N_HEADS_A * HEAD_DIM
    ka = p[:, wa:2 * wa]
    nb = ka.shape[0] // MOBA_BLOCK
    km_ref[0] = jnp.mean(ka.reshape(nb, MOBA_BLOCK, wa), axis=1)
    _store_vt(vta_ref, p[:, 2 * wa:3 * wa])
    vtb = p[:, 5 * wa:6 * wa].T.astype(BF16)
    t = ATT_TILE
    for pair in range(vtb.shape[0] // LANES):
        for tile in range(vtb.shape[1] // t):
            vtb_ref[0, pair, tile] = vtb[pair * LANES:(pair + 1) * LANES, tile * t:(tile + 1) * t]


def _even_proj(x, mod, g, w_in, seq):
    rows, d = x.shape
    tm = ROW_TILE
    n = w_in.shape[1]
    wa = N_HEADS_A * HEAD_DIM
    per_seq = seq // tm
    n_pairs = N_HEADS_A // 2
    vt_spec = _vt_spec(tm, per_seq, n_pairs, KEY_GROUP_EVEN)
    vt_shape = _vt_shape(rows // seq, seq, n_pairs, KEY_GROUP_EVEN)
    tiles_per_step = tm // ATT_TILE
    vt_tile_spec = pl.BlockSpec((1, n_pairs, tiles_per_step, LANES, ATT_TILE),
                                lambda i: (i // per_seq, 0, i % per_seq, 0, 0))
    return pl.pallas_call(
        _even_proj_kernel,
        grid=(rows // tm,),
        in_specs=[pl.BlockSpec((tm, d), lambda i: (i, 0)),
                  pl.BlockSpec((1, 1, 6 * d), lambda i: (i // per_seq, 0, 0)),
                  pl.BlockSpec((1, d), lambda i: (0, 0)),
                  pl.BlockSpec((d, n), lambda i: (0, 0))],
        out_specs=[pl.BlockSpec((tm, n), lambda i: (i, 0)),
                   pl.BlockSpec((1, tm // MOBA_BLOCK, wa), lambda i: (i, 0, 0)),
                   vt_spec, vt_tile_spec,
                   pl.BlockSpec((1, DIL_CLASSES, tm // DIL_CLASSES, n - 3 * wa),
                                lambda i: (i // per_seq, 0, i % per_seq, 0))],
        out_shape=[jax.ShapeDtypeStruct((rows, n), BF16),
                   jax.ShapeDtypeStruct((rows // tm, tm // MOBA_BLOCK, wa), F32),
                   vt_shape, _vt_shape(rows // seq, seq, n_pairs, 1),
                   jax.ShapeDtypeStruct((rows // seq, DIL_CLASSES, seq // DIL_CLASSES, n - 3 * wa), BF16)],
        compiler_params=_params(1),
        name="even_proj",
    )(x, mod, g, w_in)


def _pair_lanes(x, h):
    pair = h // 2
    return x[:, pair * LANES:(pair + 1) * LANES]


def _split_heads(q_step):
    first = lax.broadcasted_iota(jnp.int32, (q_step.shape[0], LANES), 1) < HEAD_DIM
    heads = []
    for pair in range(q_step.shape[1] // LANES):
        q2 = q_step[:, pair * LANES:(pair + 1) * LANES]
        zero = jnp.zeros_like(q2)
        heads += [jnp.where(first, q2, zero), jnp.where(first, zero, q2)]
    return heads


def _head_rows(vt_ref, g, h, n_tiles):
    return vt_ref[0, h // 2, g, (h % 2) * HEAD_DIM:(h % 2 + 1) * HEAD_DIM, 0:n_tiles * ATT_TILE]


def _pair_scores(keys, qh):
    t = ATT_TILE
    n = keys.shape[0] // t
    scores = []
    for pair in range(len(qh) // 2):
        s = _dot_nt(keys[:, pair * LANES:(pair + 1) * LANES], jnp.concatenate([qh[2 * pair], qh[2 * pair + 1]], axis=0))
        for j in range(2):
            scores.append([s[c * t:(c + 1) * t, j * t:(j + 1) * t] for c in range(n)])
    return scores


def _tile_scores(keys, q_head):
    t = ATT_TILE
    return [_dot_nt(keys[c * t:(c + 1) * t], q_head) for c in range(keys.shape[0] // t)]


def _group_keys(k_ref, g, n_tiles, key_group):
    start = pl.multiple_of(g * key_group * ATT_TILE, key_group * ATT_TILE)
    return k_ref[0, pl.ds(start, n_tiles * ATT_TILE), :]


def _sweep(qi, group, key_group, n_far_groups=None):
    g_own = qi // key_group
    own = [functools.partial(group, g_own, None, n + 1, True) for n in range(key_group)]
    state = lax.switch(qi % key_group, own)
    n_past = g_own if n_far_groups is None else jnp.minimum(g_own, n_far_groups)
    return lax.fori_loop(0, n_past, lambda n, st: group(g_own - 1 - n, st, key_group, False), state)


def _tile_iotas():
    t = ATT_TILE
    return lax.broadcasted_iota(jnp.int32, (t, t), 0), lax.broadcasted_iota(jnp.int32, (t, t), 1)


def _store_step_output(o_ref, outs_t):
    o_ref[0] = jnp.concatenate(outs_t, axis=0).T.astype(BF16)


def _softmax_group(s_tiles, ons, vt_h, acc_ref, h, stats, first):
    maxes = []
    for s, on in zip(s_tiles, ons):
        mx = jnp.max(s, axis=0, keepdims=True)
        maxes.append(mx if on is None else jnp.where(on, mx, -jnp.inf))
    m_new = functools.reduce(jnp.maximum, maxes)
    if not first:
        m_old, l_old = stats
        m_new = jnp.maximum(m_new, m_old)
    ps = []
    l_add = None
    for s, on in zip(s_tiles, ons):
        p = jnp.exp2(s - (m_new if on is None else jnp.where(on, m_new, jnp.inf)))
        p_sum = jnp.sum(p, axis=0, keepdims=True)
        l_add = p_sum if l_add is None else l_add + p_sum
        ps.append(p.astype(BF16))
    pv = _dot(vt_h, jnp.concatenate(ps, axis=0))
    if first:
        acc_ref[h] = pv
        return m_new, l_add
    alpha = jnp.exp2(m_old - m_new)
    acc_ref[h] = alpha * acc_ref[h] + pv
    return m_new, alpha * l_old + l_add


def _moba_select(qh, km_ref, sel_ref, qi):
    t = ATT_TILE
    km_hi, km_lo = _split_bf16(km_ref[0])
    nb = km_hi.shape[0]
    blk = lax.broadcasted_iota(jnp.int32, (nb, t), 0)
    past = blk < qi
    for h in range(HEADS_PER_STEP):
        gate = _dot_nt(_pair_lanes(km_hi, h), qh[h]) + _dot_nt(_pair_lanes(km_lo, h), qh[h])
        gate = jnp.where(past, gate, -jnp.inf)
        beaten = jnp.zeros((nb, t), F32)
        for other in range(nb):
            row = gate[other:other + 1, :]
            wins = jnp.where(row > gate, 1.0, jnp.where(row == gate, jnp.where(blk > other, 1.0, 0.0), 0.0))
            beaten = beaten + wins
        sel_ref[h] = jnp.where(past, jnp.where(beaten < MOBA_TOPK, 1.0, 0.0), 0.0)


def _moba_attn_kernel(q_ref, k_ref, vt_ref, bias_ref, km_ref, o_ref, acc_ref, sel_ref):
    qi = pl.program_id(2)
    qh = _split_heads(q_ref[0])
    n_bias = bias_ref.shape[1]
    _moba_select(qh, km_ref, sel_ref, qi)

    def group(g, stats, n_tiles, first):
        kg = _group_keys(k_ref, g, n_tiles, KEY_GROUP_EVEN)
        new = ()
        scores = _pair_scores(kg, qh)
        for h in range(HEADS_PER_STEP):
            s_tiles = scores[h]
            ons = []
            for c in range(n_tiles):
                j = g * KEY_GROUP_EVEN + c
                dt = n_tiles - 1 - c if first else qi - j
                s_tiles[c] = s_tiles[c] + bias_ref[h, dt if first else jnp.minimum(dt, n_bias - 1)]
                ons.append(None if first and dt == 0 else sel_ref[h, pl.ds(j, 1), :] > 0.5)
            new += _softmax_group(s_tiles, ons, _head_rows(vt_ref, g, h, n_tiles), acc_ref, h,
                                  None if first else stats[2 * h:2 * h + 2], first)
        return new

    stats = _sweep(qi, group, KEY_GROUP_EVEN)
    _store_step_output(o_ref, [acc_ref[h] / stats[2 * h + 1] for h in range(HEADS_PER_STEP)])


def _moba_attn(proj, vt, bias, km, batch, seq):
    t = ATT_TILE
    n_steps = N_HEADS_A // HEADS_PER_STEP
    n_bias = bias.shape[1]
    return pl.pallas_call(
        _moba_attn_kernel,
        grid=(batch, n_steps, seq // t),
        in_specs=[pl.BlockSpec((1, t, STEP_LANES), lambda b, p, i: (b, i, p)),
                  pl.BlockSpec((1, seq, STEP_LANES), lambda b, p, i: (b, 0, n_steps + p)),
                  pl.BlockSpec((1, PAIRS_PER_STEP) + vt.shape[2:], lambda b, p, i: (b, p, 0, 0, 0)),
                  pl.BlockSpec((HEADS_PER_STEP, n_bias, t, t), lambda b, p, i: (p, 0, 0, 0)),
                  pl.BlockSpec((1, seq // MOBA_BLOCK, STEP_LANES), lambda b, p, i: (b, 0, p))],
        out_specs=pl.BlockSpec((1, t, STEP_LANES), lambda b, p, i: (b, i, p)),
        out_shape=jax.ShapeDtypeStruct((batch, seq, N_HEADS_A * HEAD_DIM), BF16),
        scratch_shapes=[pltpu.VMEM((HEADS_PER_STEP, HEAD_DIM, t), F32),
                        pltpu.VMEM((HEADS_PER_STEP, seq // MOBA_BLOCK, t), F32)],
        compiler_params=_params(3),
        name="moba_attn",
    )(proj, proj, vt, bias, km)


def _dilated_far_kernel(q_ref, k_ref, v_ref, bias_ref, o_ref):
    t = ATT_TILE
    qh = _split_heads(q_ref[0, 0])
    keys = k_ref[0, 0]
    vt = v_ref[0, 0].astype(F32).T.astype(BF16)
    accs, maxes, sums = [], [], []
    for h, (s,) in enumerate(_pair_scores(keys, qh)):
        s = s + bias_ref[h, 0]
        m = jnp.max(s, axis=0, keepdims=True)
        p = jnp.exp2(s - jnp.where(m > -jnp.inf, m, 0.0))
        maxes.append(m)
        sums.append(jnp.sum(p, axis=0, keepdims=True))
        accs.append(_dot(vt[h * HEAD_DIM:(h + 1) * HEAD_DIM], p.astype(BF16)))
    pad = jnp.zeros((DIL_FAR_LANES - N_HEADS_B * (HEAD_DIM + 2), t), F32)
    o_ref[0, 0] = jnp.concatenate(accs + maxes + sums + [pad], axis=0).T


def _dilated_far(qkv_cls, bias, batch):
    t = ATT_TILE
    wb = N_HEADS_B * HEAD_DIM
    cols = lambda group: pl.BlockSpec((1, 1, t, wb), lambda b, c: (b, c, 0, group))
    return pl.pallas_call(
        _dilated_far_kernel,
        grid=(batch, DIL_CLASSES),
        in_specs=[cols(0), cols(1), cols(2),
                  pl.BlockSpec((N_HEADS_B, 1, t, t), lambda b, c: (0, 0, 0, 0))],
        out_specs=pl.BlockSpec((1, 1, t, DIL_FAR_LANES), lambda b, c: (b, c, 0, 0)),
        out_shape=jax.ShapeDtypeStruct((batch, DIL_CLASSES, t, DIL_FAR_LANES), F32),
        compiler_params=_params(2),
        name="dilated_far",
    )(qkv_cls, qkv_cls, qkv_cls, bias)


def _dilated_near_kernel(q_ref, k_ref, vt_ref, bias_ref, far_ref, o_ref, acc_ref):
    t = ATT_TILE
    qi = pl.program_id(1)
    qh = _split_heads(q_ref[0])

    def tiles(n_tiles):
        first_tile = qi - (n_tiles - 1)
        keys = k_ref[0, pl.ds(pl.multiple_of(first_tile * t, t), n_tiles * t), :]
        stats = ()
        scores = _pair_scores(keys, qh)
        for h in range(N_HEADS_B):
            s_tiles = [s + bias_ref[h, n_tiles - 1 - c] for c, s in enumerate(scores[h])]
            rows = slice((h % 2) * HEAD_DIM, (h % 2 + 1) * HEAD_DIM)
            vt_h = jnp.concatenate([vt_ref[0, h // 2, first_tile + c, rows, :] for c in range(n_tiles)], axis=1)
            stats += _softmax_group(s_tiles, [None] * n_tiles, vt_h, acc_ref, h, None, True)
        return stats

    stats = lax.switch(jnp.minimum(qi, DIL_NEAR_TILES - 1),
                       [functools.partial(tiles, n + 1) for n in range(DIL_NEAR_TILES)])
    far = far_ref[0].T
    far_m0, far_l0 = N_HEADS_B * HEAD_DIM, N_HEADS_B * HEAD_DIM + N_HEADS_B
    outs = []
    for h in range(N_HEADS_B):
        m_near, l_near = stats[2 * h], stats[2 * h + 1]
        m_far, l_far = far[far_m0 + h:far_m0 + h + 1], far[far_l0 + h:far_l0 + h + 1]
        m = jnp.maximum(m_near, m_far)
        w_near, w_far = jnp.exp2(m_near - m), jnp.exp2(m_far - m)
        acc_far = far[h * HEAD_DIM:(h + 1) * HEAD_DIM]
        outs.append((w_near * acc_ref[h] + w_far * acc_far) / (w_near * l_near + w_far * l_far))
    _store_step_output(o_ref, outs)


def _dilated_near(proj, vt, bias, far, batch, seq, lane0):
    t = ATT_TILE
    wb = N_HEADS_B * HEAD_DIM
    first = lane0 // wb
    return pl.pallas_call(
        _dilated_near_kernel,
        grid=(batch, seq // t),
        in_specs=[pl.BlockSpec((1, t, wb), lambda b, i: (b, i, first)),
                  pl.BlockSpec((1, seq, wb), lambda b, i: (b, 0, first + 1)),
                  pl.BlockSpec((1,) + vt.shape[1:], lambda b, i: (b, 0, 0, 0, 0)),
                  pl.BlockSpec((N_HEADS_B, DIL_NEAR_TILES, t, t), lambda b, i: (0, 0, 0, 0)),
                  pl.BlockSpec((1, t, DIL_FAR_LANES), lambda b, i: (b, i, 0))],
        out_specs=pl.BlockSpec((1, t, wb), lambda b, i: (b, i, 0)),
        out_shape=jax.ShapeDtypeStruct((batch, seq, wb), BF16),
        scratch_shapes=[pltpu.VMEM((N_HEADS_B, HEAD_DIM, t), F32)],
        compiler_params=_params(2),
        name="dilated_near",
    )(proj, proj, vt, bias, far)


def _dilated_attn(proj, qkv_cls, vt_tiles, bias_near, bias_far, batch, seq, lane0):
    far = _dilated_far(qkv_cls, bias_far, batch)
    far = far.transpose(0, 2, 1, 3).reshape(batch, seq, DIL_FAR_LANES)
    return _dilated_near(proj, vt_tiles, bias_near, far, batch, seq, lane0)


def _odd_proj_kernel(x_ref, mod_ref, g_ref, win_ref, gq_ref, gkv_ref, wq_ref, wkv_ref,
                     cq_ref, sq_ref, ck_ref, sk_ref, qm_ref, km_ref, vtm_ref, sb_ref, vts_ref):
    d = D_MODEL
    mod = mod_ref[0]
    h = _prenorm(x_ref[...], g_ref[...], mod[:, d:2 * d], mod[:, 0:d])
    p = _dot(h.astype(BF16), win_ref[...])
    o = MLA_Q_RANK + MLA_KV_RANK
    c_q, c_kv = p[:, 0:MLA_Q_RANK], p[:, MLA_Q_RANK:o]
    k_rope, k_rope_swapped = p[:, o:o + LANES], p[:, o + LANES:o + 2 * LANES]
    sb0 = o + 2 * LANES
    n_qk = 2 * N_HEADS_D * HEAD_DIM
    sb_ref[...] = p[:, sb0:sb0 + n_qk].astype(BF16)
    _store_vt(vts_ref, p[:, sb0 + n_qk:])
    q12 = _dot(_rms(c_q, gq_ref[...]).astype(BF16), wq_ref[...])
    kv = _dot(_rms(c_kv, gkv_ref[...]).astype(BF16), wkv_ref[...])
    k_pe = k_rope * ck_ref[...] + k_rope_swapped * sk_ref[...]
    cq, sq = cq_ref[...], sq_ref[...]
    half = N_HEADS_C * LANES
    for hh in range(N_HEADS_C):
        cols = slice(hh * LANES, (hh + 1) * LANES)
        swapped = slice(half + hh * LANES, half + (hh + 1) * LANES)
        qm_ref[:, cols] = (q12[:, cols] * cq + q12[:, swapped] * sq).astype(BF16)
        km_ref[:, cols] = (kv[:, cols] + k_pe).astype(BF16)
    _store_vt(vtm_ref, kv[:, half:])


def _odd_proj(x, mod, g, w_in, gq, gkv, wq, wkv, tables, seq):
    rows, d = x.shape
    tm = ROW_TILE
    per_seq = seq // tm
    n_in = w_in.shape[1]
    n_sb = 2 * N_HEADS_D * HEAD_DIM
    n_q = N_HEADS_C * LANES
    n_pairs = N_HEADS_C // 2
    const = lambda i: (0, 0)
    table_spec = pl.BlockSpec((tm, LANES), lambda i: (i % per_seq, 0))
    vt_spec = _vt_spec(tm, per_seq, n_pairs, KEY_GROUP_ODD)
    vt_shape = _vt_shape(rows // seq, seq, n_pairs, KEY_GROUP_ODD)
    return pl.pallas_call(
        _odd_proj_kernel,
        grid=(rows // tm,),
        in_specs=[pl.BlockSpec((tm, d), lambda i: (i, 0)),
                  pl.BlockSpec((1, 1, 6 * d), lambda i: (i // per_seq, 0, 0)),
                  pl.BlockSpec((1, d), const),
                  pl.BlockSpec((d, n_in), const),
                  pl.BlockSpec((1, MLA_Q_RANK), const),
                  pl.BlockSpec((1, MLA_KV_RANK), const),
                  pl.BlockSpec(wq.shape, const),
                  pl.BlockSpec(wkv.shape, const),
                  table_spec, table_spec, table_spec, table_spec],
        out_specs=[pl.BlockSpec((tm, n_q), lambda i: (i, 0)),
                   pl.BlockSpec((tm, n_q), lambda i: (i, 0)),
                   vt_spec,
                   pl.BlockSpec((tm, n_sb), lambda i: (i, 0)),
                   vt_spec],
        out_shape=[jax.ShapeDtypeStruct((rows, n_q), BF16),
                   jax.ShapeDtypeStruct((rows, n_q), BF16),
                   vt_shape,
                   jax.ShapeDtypeStruct((rows, n_sb), BF16),
                   vt_shape],
        compiler_params=_params(1),
        name="odd_proj",
    )(x, mod, g, w_in, gq, gkv, wq, wkv, *tables)


def _mla_attn_kernel(q_ref, k_ref, vt_ref, o_ref, acc_ref):
    qi = pl.program_id(2)
    q_step = q_ref[0]
    qh = [q_step[:, h * LANES:(h + 1) * LANES] for h in range(HEADS_PER_STEP)]

    def group(g, stats, n_tiles, first):
        kg = _group_keys(k_ref, g, n_tiles, KEY_GROUP_ODD)
        new = ()
        scores = [_tile_scores(kg[:, h * LANES:(h + 1) * LANES], qh[h]) for h in range(HEADS_PER_STEP)]
        for h in range(HEADS_PER_STEP):
            s_tiles = scores[h]
            if first:
                key, query = _tile_iotas()
                s_tiles[-1] = jnp.where(key <= query, s_tiles[-1], -jnp.inf)
            new += _softmax_group(s_tiles, [None] * n_tiles, _head_rows(vt_ref, g, h, n_tiles), acc_ref, h,
                                  None if first else stats[2 * h:2 * h + 2], first)
        return new

    stats = _sweep(qi, group, KEY_GROUP_ODD)
    _store_step_output(o_ref, [acc_ref[h] / stats[2 * h + 1] for h in range(HEADS_PER_STEP)])


def _mla_attn(qm, km, vt, batch, seq):
    t = ATT_TILE
    n_steps = N_HEADS_C // HEADS_PER_STEP
    qk_lanes = HEADS_PER_STEP * LANES
    return pl.pallas_call(
        _mla_attn_kernel,
        grid=(batch, n_steps, seq // t),
        in_specs=[pl.BlockSpec((1, t, qk_lanes), lambda b, p, i: (b, i, p)),
                  pl.BlockSpec((1, seq, qk_lanes), lambda b, p, i: (b, 0, p)),
                  pl.BlockSpec((1, PAIRS_PER_STEP) + vt.shape[2:], lambda b, p, i: (b, p, 0, 0, 0))],
        out_specs=pl.BlockSpec((1, t, STEP_LANES), lambda b, p, i: (b, i, p)),
        out_shape=jax.ShapeDtypeStruct((batch, seq, N_HEADS_C * MLA_V_DIM), BF16),
        scratch_shapes=[pltpu.VMEM((HEADS_PER_STEP, HEAD_DIM, t), F32)],
        compiler_params=_params(3),
        name="mla_attn",
    )(qm, km, vt)


def _stick_attn_kernel(q_ref, k_ref, vt_ref, o_ref, acc_ref):
    t = ATT_TILE
    qi = pl.program_id(2)
    qh = _split_heads(q_ref[0])
    key = lax.broadcasted_iota(jnp.int32, (t, t), 0)
    other = lax.broadcasted_iota(jnp.int32, (t, t), 1)
    from_here = jnp.where(other >= key, 1.0, 0.0).astype(BF16)
    from_here2 = jnp.concatenate([from_here, from_here], axis=1)

    strict = key < other

    def group(g, carries, n_tiles, first):
        kg = _group_keys(k_ref, g, n_tiles, KEY_GROUP_ODD)
        new = ()
        scores = _pair_scores(kg, qh)
        for h in range(HEADS_PER_STEP):
            z_tiles = scores[h]
            keep_tiles = []
            for z in z_tiles:
                neg_z = -z
                keep_tiles.append(jnp.minimum(neg_z, 0.0) - jnp.log(1.0 + jnp.exp(jnp.minimum(z, neg_z))))
            if first:
                keep_tiles[-1] = jnp.where(strict, keep_tiles[-1], 0.0)
            run = None if first else carries[h]
            a_tiles = [None] * n_tiles
            for c in reversed(range(n_tiles)):
                tail = _dot(from_here2, jnp.concatenate(_split_bf16(keep_tiles[c]), axis=0))
                tile_total = tail[0:1, :]
                if run is not None:
                    tail = tail + run
                a_tiles[c] = jnp.exp(z_tiles[c] + tail)
                run = tile_total if run is None else run + tile_total
            if first:
                a_tiles[-1] = jnp.where(strict, a_tiles[-1], 0.0)
            pv = _dot(_head_rows(vt_ref, g, h, n_tiles), jnp.concatenate(a_tiles, axis=0).astype(BF16))
            acc_ref[h] = pv if first else acc_ref[h] + pv
            new += (run,)
        return new

    _sweep(qi, group, KEY_GROUP_ODD)
    _store_step_output(o_ref, [acc_ref[h] for h in range(HEADS_PER_STEP)])


def _stick_attn(sb, vt, batch, seq):
    t = ATT_TILE
    n_steps = N_HEADS_D // HEADS_PER_STEP
    return pl.pallas_call(
        _stick_attn_kernel,
        grid=(batch, n_steps, seq // t),
        in_specs=[pl.BlockSpec((1, t, STEP_LANES), lambda b, p, i: (b, i, p)),
                  pl.BlockSpec((1, seq, STEP_LANES), lambda b, p, i: (b, 0, n_steps + p)),
                  pl.BlockSpec((1, PAIRS_PER_STEP) + vt.shape[2:], lambda b, p, i: (b, p, 0, 0, 0))],
        out_specs=pl.BlockSpec((1, t, STEP_LANES), lambda b, p, i: (b, i, p)),
        out_shape=jax.ShapeDtypeStruct((batch, seq, N_HEADS_D * HEAD_DIM), BF16),
        scratch_shapes=[pltpu.VMEM((HEADS_PER_STEP, HEAD_DIM, t), F32)],
        compiler_params=_params(3),
        name="stick_attn",
    )(sb, sb, vt)


def _post_attn_kernel(oa_ref, ob_ref, oah_ref, obh_ref, x_ref, xh_ref, mod_ref, gmix_ref, gpre_ref, gpost_ref,
                      wo_ref, wup_ref, cw_ref, cb_ref, wd_ref, o_ref, x1_ref, h_ref, u_ref, acc_ref, *, per_seq):
    d = D_MODEL
    tm = ROW_TILE
    halo = CONV_HALO
    cw = FF_CHUNK
    i = pl.program_id(0)
    mod = mod_ref[0]
    gate_m, shift, scale, gate_f = mod[:, 2 * d:3 * d], mod[:, 3 * d:4 * d], mod[:, 4 * d:5 * d], mod[:, 5 * d:6 * d]
    n_first = oa_ref.shape[1]

    def mixed(oa, ob, x):
        y = _dot(oa, wo_ref[0:n_first, :]) + _dot(ob, wo_ref[n_first:2 * n_first, :])
        return x + gate_m * _rms(y, gmix_ref[...])

    g = gpre_ref[...]
    ahead = _prenorm(mixed(oah_ref[...], obh_ref[...], xh_ref[...]), g, scale, shift)
    ahead = jnp.where(i % per_seq == 0, 0.0, ahead)
    h_ref[0:halo, :] = ahead.astype(BF16)
    x1_ref[...] = mixed(oa_ref[...], ob_ref[...], x_ref[...])
    h_ref[halo:halo + tm, :] = _prenorm(x1_ref[...], g, scale, shift).astype(BF16)

    def up(ch):
        slot = ch % 2
        for half in range(2):
            cols = slice(half * D_FF + ch * cw, half * D_FF + (ch + 1) * cw)
            u_ref[slot, half] = _dot(h_ref[...], wup_ref[:, cols])

    def conv(slot, half, ch):
        cols = slice(half * D_FF + ch * cw, half * D_FF + (ch + 1) * cw)
        w = cw_ref[:, cols]
        out = None
        for tap in range(CONV_WIDTH):
            start = halo - (CONV_WIDTH - 1 - tap)
            term = w[tap:tap + 1, :] * u_ref[slot, half, start:start + tm, :]
            out = term if out is None else out + term
        return out + cb_ref[:, cols]

    def down(ch, act):
        part = _dot(act, wd_ref[ch * cw:(ch + 1) * cw, :])
        if ch == 0:
            acc_ref[...] = part
        else:
            acc_ref[...] += part

    up(0)
    act = None
    for ch in range(N_FF_CHUNKS):
        if ch + 1 < N_FF_CHUNKS:
            up(ch + 1)
        if act is not None:
            down(ch - 1, act)
        slot = ch % 2
        act = (jax.nn.gelu(conv(slot, 0, ch), approximate=True) * conv(slot, 1, ch)).astype(BF16)
    down(N_FF_CHUNKS - 1, act)

    o_ref[...] = x1_ref[...] + gate_f * _rms(acc_ref[...], gpost_ref[...])


def _post_attn(oa, ob, x, mod, g_mix, g_pre, g_post, w_out, w_up, conv_w, conv_b, w_down, seq):
    rows, d = x.shape
    tm = ROW_TILE
    halo = CONV_HALO
    per_seq = seq // tm
    n_first = oa.shape[1]
    const = lambda i: (0, 0)
    tile = lambda i: (i, 0)
    ahead = lambda i: (jnp.maximum(i * (tm // halo) - 1, 0), 0)
    once = dict(pipeline_mode=pl.Buffered(1))
    return pl.pallas_call(
        functools.partial(_post_attn_kernel, per_seq=per_seq),
        grid=(rows // tm,),
        in_specs=[pl.BlockSpec((tm, n_first), tile), pl.BlockSpec((tm, n_first), tile),
                  pl.BlockSpec((halo, n_first), ahead), pl.BlockSpec((halo, n_first), ahead),
                  pl.BlockSpec((tm, d), tile), pl.BlockSpec((halo, d), ahead),
                  pl.BlockSpec((1, 1, 6 * d), lambda i: (i // per_seq, 0, 0)),
                  pl.BlockSpec((1, d), const), pl.BlockSpec((1, d), const), pl.BlockSpec((1, d), const),
                  pl.BlockSpec(w_out.shape, const, **once),
                  pl.BlockSpec(w_up.shape, const, **once),
                  pl.BlockSpec(conv_w.shape, const),
                  pl.BlockSpec(conv_b.shape, const),
                  pl.BlockSpec(w_down.shape, const, **once)],
        out_specs=pl.BlockSpec((tm, d), tile),
        out_shape=jax.ShapeDtypeStruct((rows, d), F32),
        scratch_shapes=[pltpu.VMEM((tm, d), F32),
                        pltpu.VMEM((tm + halo, d), BF16),
                        pltpu.VMEM((2, 2, tm + halo, FF_CHUNK), F32),
                        pltpu.VMEM((tm, d), F32)],
        compiler_params=_params(1),
        name="post_attn",
    )(oa, ob, oa, ob, x, x, mod, g_mix, g_pre, g_post, w_out, w_up, conv_w, conv_b, w_down)


def _rotate_half_cols(w):
    half = w.shape[-1] // 2
    return jnp.concatenate([-w[..., half:], w[..., :half]], axis=-1)


def _pad_cols(w, left, total):
    return jnp.pad(w, ((0, 0), (left, total - left - w.shape[1])))


def _rope_tables(seq):
    inv_freq = 1.0 / (ROPE_THETA ** (jnp.arange(0, MLA_ROPE_DIM, 2, dtype=F32) / MLA_ROPE_DIM))
    ang = jnp.arange(seq, dtype=F32)[:, None] * inv_freq[None, :]
    cos, sin = jnp.cos(ang), jnp.sin(ang)
    cos2 = _pad_cols(jnp.concatenate([cos, cos], axis=1), MLA_NOPE_DIM, LANES)
    sin2 = _pad_cols(jnp.concatenate([sin, sin], axis=1), MLA_NOPE_DIM, LANES)
    scale = (MLA_NOPE_DIM + MLA_ROPE_DIM) ** -0.5
    nope_ones = _pad_cols(jnp.ones((seq, MLA_NOPE_DIM), F32), 0, LANES)
    scale = scale * LOG2_E
    return (scale * (cos2 + nope_ones), scale * sin2, cos2, sin2)


def _odd_weights(w_in, w_uq, w_ukv):
    o = MLA_Q_RANK + MLA_KV_RANK
    w_rope = w_in[:, o:o + MLA_ROPE_DIM]
    scale_d = HEAD_DIM ** -0.5
    wd = N_HEADS_D * HEAD_DIM
    sb0 = o + MLA_ROPE_DIM
    w_in2 = jnp.concatenate([
        w_in[:, :o],
        _pad_cols(w_rope, MLA_NOPE_DIM, LANES),
        _pad_cols(_rotate_half_cols(w_rope), MLA_NOPE_DIM, LANES),
        w_in[:, sb0:sb0 + wd] * scale_d,
        w_in[:, sb0 + wd:],
    ], axis=1).astype(BF16)
    qd = MLA_NOPE_DIM + MLA_ROPE_DIM
    uq = w_uq.reshape(MLA_Q_RANK, N_HEADS_C, qd)
    plain = jnp.pad(uq, ((0, 0), (0, 0), (0, LANES - qd)))
    swapped = jnp.pad(_rotate_half_cols(uq[..., MLA_NOPE_DIM:]),
                      ((0, 0), (0, 0), (MLA_NOPE_DIM, LANES - qd)))
    wq = jnp.concatenate([plain.reshape(MLA_Q_RANK, -1), swapped.reshape(MLA_Q_RANK, -1)], axis=1).astype(BF16)
    ukv = w_ukv.reshape(MLA_KV_RANK, N_HEADS_C, MLA_NOPE_DIM + MLA_V_DIM)
    k_nope = jnp.pad(ukv[..., :MLA_NOPE_DIM], ((0, 0), (0, 0), (0, LANES - MLA_NOPE_DIM)))
    v = ukv[..., MLA_NOPE_DIM:]
    wkv = jnp.concatenate([k_nope.reshape(MLA_KV_RANK, -1), v.reshape(MLA_KV_RANK, -1)], axis=1).astype(BF16)
    return w_in2, wq, wkv


def _even_weights(w_in):
    wa = N_HEADS_A * HEAD_DIM
    scale = HEAD_DIM ** -0.5 * LOG2_E
    col = jnp.arange(w_in.shape[1])
    is_q = (col < wa) | ((col >= 3 * wa) & (col < 4 * wa))
    return (w_in * jnp.where(is_q, scale, 1.0)).astype(BF16)


def kernel(x, c, rel_bias, ada_w, ada_b, mix_pre_g, mix_post_g, ffn_pre_g, ffn_post_g, ab_w_in, ab_w_out,
           cd_w_in, mla_q_norm_g, mla_kv_norm_g, mla_w_uq, mla_w_ukv, cd_w_out, ffn_w_up, ffn_conv_w,
           ffn_conv_b, ffn_w_down):
    batch, seq, d = x.shape
    widest_group = max(KEY_GROUP_EVEN, KEY_GROUP_ODD) * ATT_TILE
    assert d == D_MODEL and seq % widest_group == 0 and (KEY_GROUP_EVEN * ATT_TILE) % ROW_TILE == 0
    assert seq == DIL_CLASSES * ATT_TILE
    rows = batch * seq
    xf = x.reshape(rows, d)

    mods = _mods(c, ada_w, ada_b)
    bias_a = _bias_tiles(rel_bias, 0, N_HEADS_A, MOBA_BIAS_TILES, 1, None, "moba_bias_tiles")
    bias_near = _bias_tiles(rel_bias, N_HEADS_A, N_HEADS_B, DIL_NEAR_TILES, 1, DIL_NEAR_BRANCHES,
                            "dilated_near_bias_tiles")
    bias_far = _bias_tiles(rel_bias, N_HEADS_A, N_HEADS_B, 1, DIL_CLASSES, DIL_FAR_BRANCHES,
                           "dilated_far_bias_tiles")
    tables = _rope_tables(seq)

    for layer in range(DEPTH):
        mod = mods[layer].reshape(batch, 1, 6 * d)
        i = layer // 2
        if layer % 2 == 0:
            proj, km, vta, vtb, qkv_cls = _even_proj(xf, mod, mix_pre_g[layer].reshape(1, d),
                                                     _even_weights(ab_w_in[i]), seq)
            proj = proj.reshape(batch, seq, -1)
            km = km.reshape(batch, seq // MOBA_BLOCK, -1)
            o_first = _moba_attn(proj, vta, bias_a, km, batch, seq)
            o_second = _dilated_attn(proj, qkv_cls, vtb, bias_near, bias_far, batch, seq,
                                     3 * N_HEADS_A * HEAD_DIM)
            w_out = ab_w_out[i]
        else:
            w_in2, wq, wkv = _odd_weights(cd_w_in[i], mla_w_uq[i], mla_w_ukv[i])
            qm, km, vtm, sb, vts = _odd_proj(xf, mod, mix_pre_g[layer].reshape(1, d), w_in2,
                                             mla_q_norm_g[i].reshape(1, -1), mla_kv_norm_g[i].reshape(1, -1),
                                             wq, wkv, tables, seq)
            o_first = _mla_attn(qm.reshape(batch, seq, -1), km.reshape(batch, seq, -1), vtm, batch, seq)
            o_second = _stick_attn(sb.reshape(batch, seq, -1), vts, batch, seq)
            w_out = cd_w_out[i]
        xf = _post_attn(o_first.reshape(rows, -1), o_second.reshape(rows, -1), xf, mod,
                        mix_post_g[layer].reshape(1, d), ffn_pre_g[layer].reshape(1, d),
                        ffn_post_g[layer].reshape(1, d), w_out.astype(BF16), ffn_w_up[layer].astype(BF16),
                        ffn_conv_w[layer], ffn_conv_b[layer].reshape(1, -1), ffn_w_down[layer].astype(BF16), seq)
    return xf.reshape(batch, seq, d)
```

```python
import functools
import math

import jax
import jax.numpy as jnp
from jax import lax
from jax.experimental import pallas as pl
from jax.experimental.pallas import tpu as pltpu

F32 = jnp.float32
BF16 = jnp.bfloat16

D_MODEL = 1024
DEPTH = 4
HEAD_DIM = 64
N_HEADS_A = 8
N_HEADS_B = 8
N_HEADS_C = 8
N_HEADS_D = 8
MOBA_BLOCK = 256
MOBA_TOPK = 3
DILATED_BRANCHES = ((128, 1), (512, 4), (2048, 16))
MLA_Q_RANK = 256
MLA_KV_RANK = 256
MLA_NOPE_DIM = 64
MLA_ROPE_DIM = 32
MLA_V_DIM = 64
ROPE_THETA = 10000.0
REL_BUCKETS = 32
REL_MAX_DIST = 2048
D_FF = 2816
CONV_WIDTH = 3
NORM_EPS = 1e-6
LOG2_E = math.log2(math.e)

LANES = 128
SUBLANES = 8
BF16_SUBLANES = 16
VMEM_LIMIT_BYTES = 56 * 1024 * 1024

ATT_TILE = MOBA_BLOCK
KEY_GROUP_EVEN = 8
KEY_GROUP_ODD = 8
PAIRS_PER_STEP = 2
HEADS_PER_STEP = 2 * PAIRS_PER_STEP
STEP_LANES = PAIRS_PER_STEP * LANES
ROW_TILE = 512
FF_CHUNK = 256
N_FF_CHUNKS = D_FF // FF_CHUNK
CONV_HALO = BF16_SUBLANES
MOBA_BIAS_TILES = REL_MAX_DIST // ATT_TILE + 2
DIL_SPLIT = DILATED_BRANCHES[1][0]
DIL_CLASSES = DILATED_BRANCHES[2][1]
DIL_NEAR_TILES = DIL_SPLIT // ATT_TILE + 1
DIL_FAR_LANES = 5 * LANES
assert DIL_FAR_LANES >= N_HEADS_B * (HEAD_DIM + 2)
DIL_NEAR_BRANCHES = tuple((-1, min(window, DIL_SPLIT), dil) for window, dil in DILATED_BRANCHES)
DIL_FAR_BRANCHES = ((DIL_SPLIT, DILATED_BRANCHES[2][0], DIL_CLASSES),)
assert all(window <= DIL_SPLIT for window, _ in DILATED_BRANCHES[:2]) and DIL_SPLIT % DIL_CLASSES == 0

_NT = (((1,), (1,)), ((), ()))


def _bucket_lower_bounds():
    max_exact = REL_BUCKETS // 2
    ratio = REL_MAX_DIST // max_exact
    n_log = REL_BUCKETS - max_exact
    lows = list(range(max_exact + 1))
    for k in range(1, n_log):
        d = lows[-1]
        while d ** n_log < (max_exact ** n_log) * (ratio ** k):
            d += 1
        lows.append(d)
    return lows


_BUCKET_LOW = _bucket_lower_bounds()


def _dot(a, b):
    return jnp.dot(a, b, preferred_element_type=F32)


def _dot_nt(a, b):
    return lax.dot_general(a, b, _NT, preferred_element_type=F32)


def _split_bf16(x):
    hi = x.astype(BF16)
    lo = (x - hi.astype(F32)).astype(BF16)
    return hi, lo


def _rms(x, g):
    return (x * lax.rsqrt(jnp.mean(x * x, axis=-1, keepdims=True) + NORM_EPS)) * g


def _prenorm(x, g, scale, shift):
    return _rms(x, g) * (1.0 + scale) + shift


def _params(n_grid_dims):
    return pltpu.CompilerParams(dimension_semantics=("arbitrary",) * n_grid_dims,
                                vmem_limit_bytes=VMEM_LIMIT_BYTES)


def _mods_kernel(c_ref, w_ref, b_ref, o_ref):
    c = c_ref[...]
    cond = c * jax.nn.sigmoid(c)
    c_hi, c_lo = _split_bf16(cond)
    w_hi, w_lo = _split_bf16(w_ref[0])
    o_ref[0] = _dot(c_hi, w_hi) + _dot(c_hi, w_lo) + _dot(c_lo, w_hi) + b_ref[0]


def _mods(c, ada_w, ada_b):
    b, d = c.shape
    rows = BF16_SUBLANES
    n_out = ada_w.shape[-1]
    tn = n_out // 4
    c_pad = jnp.zeros((rows, d), F32).at[:b].set(c)
    out = pl.pallas_call(
        _mods_kernel,
        grid=(DEPTH, n_out // tn),
        in_specs=[pl.BlockSpec((rows, d), lambda l, j: (0, 0)),
                  pl.BlockSpec((1, d, tn), lambda l, j: (l, 0, j)),
                  pl.BlockSpec((1, 1, tn), lambda l, j: (l, 0, j))],
        out_specs=pl.BlockSpec((1, rows, tn), lambda l, j: (l, 0, j)),
        out_shape=jax.ShapeDtypeStruct((DEPTH, rows, n_out), F32),
        compiler_params=_params(2),
        name="ada_mods",
    )(c_pad, ada_w, ada_b.reshape(DEPTH, 1, n_out))
    return out[:, :b]


def _bias_tiles_kernel(tab_ref, o_ref, *, head_off, step, branches):
    h = pl.program_id(0) + head_off
    t = ATT_TILE
    offsets = lax.broadcasted_iota(jnp.int32, (SUBLANES, 2 * t), 1) - t
    for d in range(o_ref.shape[1]):
        dist = (d * t + offsets) * step
        val = jnp.full(dist.shape, tab_ref[h, 0], F32)
        for b in range(1, REL_BUCKETS):
            val = jnp.where(dist >= _BUCKET_LOW[b], tab_ref[h, b], val)
        if branches is not None:
            mult = jnp.zeros(dist.shape, F32)
            for beyond, window, dil in branches:
                hit = jnp.where(dist <= window, jnp.where((dist & (dil - 1)) == 0, 1.0, 0.0), 0.0)
                mult = mult + jnp.where(dist > beyond, hit, 0.0)
            log_mult = jnp.where(mult > 2.5, math.log(3.0), jnp.where(mult > 1.5, math.log(2.0), 0.0))
            val = jnp.where(mult > 0.5, val + log_mult, -jnp.inf)
        val = jnp.where(dist >= 0, val * LOG2_E, -jnp.inf)
        strip = jnp.concatenate([val] * (t // SUBLANES), axis=0)
        rotated = pltpu.roll(strip, 0, 1, stride=1, stride_axis=0)
        o_ref[0, d] = rotated[:, t:2 * t]


def _bias_tiles(rel_bias, head_off, n_heads, n_tiles, step, branches, name):
    t = ATT_TILE
    return pl.pallas_call(
        functools.partial(_bias_tiles_kernel, head_off=head_off, step=step, branches=branches),
        grid=(n_heads,),
        in_specs=[pl.BlockSpec(memory_space=pltpu.SMEM)],
        out_specs=pl.BlockSpec((1, n_tiles, t, t), lambda h: (h, 0, 0, 0)),
        out_shape=jax.ShapeDtypeStruct((n_heads, n_tiles, t, t), F32),
        compiler_params=_params(1),
        name=name,
    )(rel_bias)


def _store_vt(vt_ref, v):
    vt = v.T.astype(BF16)
    for p in range(vt.shape[0] // LANES):
        vt_ref[0, p, 0] = vt[p * LANES:(p + 1) * LANES, :]


def _vt_spec(tm, per_seq, n_pairs, key_group):
    per_group = key_group * ATT_TILE // tm
    return pl.BlockSpec((1, n_pairs, 1, LANES, tm),
                        lambda i: (i // per_seq, 0, (i % per_seq) // per_group, 0, (i % per_seq) % per_group))


def _vt_shape(batch, seq, n_pairs, key_group):
    group_keys = key_group * ATT_TILE
    return jax.ShapeDtypeStruct((batch, n_pairs, seq // group_keys, LANES, group_keys), BF16)


def _even_proj_kernel(x_ref, mod_ref, g_ref, w_ref, o_ref, km_ref, vta_ref, vtb_ref, cls_ref):
    d = D_MODEL
    mod = mod_ref[0]
    h = _prenorm(x_ref[...], g_ref[...], mod[:, d:2 * d], mod[:, 0:d])
    p = _dot(h.astype(BF16), w_ref[...])
    p_bf16 = p.astype(BF16)
    o_ref[...] = p_bf16
    tm = p.shape[0]
    per_class = tm // DIL_CLASSES
    out_row = lax.broadcasted_iota(jnp.int32, (tm, tm), 0)
    in_row = lax.broadcasted_iota(jnp.int32, (tm, tm), 1)
    source = (out_row % per_class) * DIL_CLASSES + out_row // per_class
    permute = jnp.where(in_row == source, 1.0, 0.0).astype(BF16)
    by_class = _dot(permute, p_bf16[:, 3 * N_HEADS_A * HEAD_DIM:]).astype(BF16)
    for cls in range(DIL_CLASSES):
        cls_ref[0, cls] = by_class[cls * per_class:(cls + 1) * per_class]
    wa = N_HEADS_A * HEAD_DIM
    ka = p[:, wa:2 * wa]
    nb = ka.shape[0] // MOBA_BLOCK
    km_ref[0] = jnp.mean(ka.reshape(nb, MOBA_BLOCK, wa), axis=1)
    _store_vt(vta_ref, p[:, 2 * wa:3 * wa])
    vtb = p[:, 5 * wa:6 * wa].T.astype(BF16)
    t = ATT_TILE
    for pair in range(vtb.shape[0] // LANES):
        for tile in range(vtb.shape[1] // t):
            vtb_ref[0, pair, tile] = vtb[pair * LANES:(pair + 1) * LANES, tile * t:(tile + 1) * t]


def _even_proj(x, mod, g, w_in, seq):
    rows, d = x.shape
    tm = ROW_TILE
    n = w_in.shape[1]
    wa = N_HEADS_A * HEAD_DIM
    per_seq = seq // tm
    n_pairs = N_HEADS_A // 2
    vt_spec = _vt_spec(tm, per_seq, n_pairs, KEY_GROUP_EVEN)
    vt_shape = _vt_shape(rows // seq, seq, n_pairs, KEY_GROUP_EVEN)
    tiles_per_step = tm // ATT_TILE
    vt_tile_spec = pl.BlockSpec((1, n_pairs, tiles_per_step, LANES, ATT_TILE),
                                lambda i: (i // per_seq, 0, i % per_seq, 0, 0))
    return pl.pallas_call(
        _even_proj_kernel,
        grid=(rows // tm,),
        in_specs=[pl.BlockSpec((tm, d), lambda i: (i, 0)),
                  pl.BlockSpec((1, 1, 6 * d), lambda i: (i // per_seq, 0, 0)),
                  pl.BlockSpec((1, d), lambda i: (0, 0)),
                  pl.BlockSpec((d, n), lambda i: (0, 0))],
        out_specs=[pl.BlockSpec((tm, n), lambda i: (i, 0)),
                   pl.BlockSpec((1, tm // MOBA_BLOCK, wa), lambda i: (i, 0, 0)),
                   vt_spec, vt_tile_spec,
                   pl.BlockSpec((1, DIL_CLASSES, tm // DIL_CLASSES, n - 3 * wa),
                                lambda i: (i // per_seq, 0, i % per_seq, 0))],
        out_shape=[jax.ShapeDtypeStruct((rows, n), BF16),
                   jax.ShapeDtypeStruct((rows // tm, tm // MOBA_BLOCK, wa), F32),
                   vt_shape, _vt_shape(rows // seq, seq, n_pairs, 1),
                   jax.ShapeDtypeStruct((rows // seq, DIL_CLASSES, seq // DIL_CLASSES, n - 3 * wa), BF16)],
        compiler_params=_params(1),
        name="even_proj",
    )(x, mod, g, w_in)


def _pair_lanes(x, h):
    pair = h // 2
    return x[:, pair * LANES:(pair + 1) * LANES]


def _split_heads(q_step):
    first = lax.broadcasted_iota(jnp.int32, (q_step.shape[0], LANES), 1) < HEAD_DIM
    heads = []
    for pair in range(q_step.shape[1] // LANES):
        q2 = q_step[:, pair * LANES:(pair + 1) * LANES]
        zero = jnp.zeros_like(q2)
        heads += [jnp.where(first, q2, zero), jnp.where(first, zero, q2)]
    return heads


def _head_rows(vt_ref, g, h, n_tiles):
    return vt_ref[0, h // 2, g, (h % 2) * HEAD_DIM:(h % 2 + 1) * HEAD_DIM, 0:n_tiles * ATT_TILE]


def _pair_scores(keys, qh):
    t = ATT_TILE
    n = keys.shape[0] // t
    scores = []
    for pair in range(len(qh) // 2):
        s = _dot_nt(keys[:, pair * LANES:(pair + 1) * LANES], jnp.concatenate([qh[2 * pair], qh[2 * pair + 1]], axis=0))
        for j in range(2):
            scores.append([s[c * t:(c + 1) * t, j * t:(j + 1) * t] for c in range(n)])
    return scores


def _tile_scores(keys, q_head):
    t = ATT_TILE
    return [_dot_nt(keys[c * t:(c + 1) * t], q_head) for c in range(keys.shape[0] // t)]


def _group_keys(k_ref, g, n_tiles, key_group):
    start = pl.multiple_of(g * key_group * ATT_TILE, key_group * ATT_TILE)
    return k_ref[0, pl.ds(start, n_tiles * ATT_TILE), :]


def _sweep(qi, group, key_group, n_far_groups=None):
    g_own = qi // key_group
    own = [functools.partial(group, g_own, None, n + 1, True) for n in range(key_group)]
    state = lax.switch(qi % key_group, own)
    n_past = g_own if n_far_groups is None else jnp.minimum(g_own, n_far_groups)
    return lax.fori_loop(0, n_past, lambda n, st: group(g_own - 1 - n, st, key_group, False), state)


def _tile_iotas():
    t = ATT_TILE
    return lax.broadcasted_iota(jnp.int32, (t, t), 0), lax.broadcasted_iota(jnp.int32, (t, t), 1)


def _store_step_output(o_ref, outs_t):
    o_ref[0] = jnp.concatenate(outs_t, axis=0).T.astype(BF16)


def _softmax_group(s_tiles, ons, vt_h, acc_ref, h, stats, first):
    maxes = []
    for s, on in zip(s_tiles, ons):
        mx = jnp.max(s, axis=0, keepdims=True)
        maxes.append(mx if on is None else jnp.where(on, mx, -jnp.inf))
    m_new = functools.reduce(jnp.maximum, maxes)
    if not first:
        m_old, l_old = stats
        m_new = jnp.maximum(m_new, m_old)
    ps = []
    l_add = None
    for s, on in zip(s_tiles, ons):
        p = jnp.exp2(s - (m_new if on is None else jnp.where(on, m_new, jnp.inf)))
        p_sum = jnp.sum(p, axis=0, keepdims=True)
        l_add = p_sum if l_add is None else l_add + p_sum
        ps.append(p.astype(BF16))
    pv = _dot(vt_h, jnp.concatenate(ps, axis=0))
    if first:
        acc_ref[h] = pv
        return m_new, l_add
    alpha = jnp.exp2(m_old - m_new)
    acc_ref[h] = alpha * acc_ref[h] + pv
    return m_new, alpha * l_old + l_add


def _moba_select(qh, km_ref, sel_ref, qi):
    t = ATT_TILE
    km_hi, km_lo = _split_bf16(km_ref[0])
    nb = km_hi.shape[0]
    blk = lax.broadcasted_iota(jnp.int32, (nb, t), 0)
    past = blk < qi
    for h in range(HEADS_PER_STEP):
        gate = _dot_nt(_pair_lanes(km_hi, h), qh[h]) + _dot_nt(_pair_lanes(km_lo, h), qh[h])
        gate = jnp.where(past, gate, -jnp.inf)
        beaten = jnp.zeros((nb, t), F32)
        for other in range(nb):
            row = gate[other:other + 1, :]
            wins = jnp.where(row > gate, 1.0, jnp.where(row == gate, jnp.where(blk > other, 1.0, 0.0), 0.0))
            beaten = beaten + wins
        sel_ref[h] = jnp.where(past, jnp.where(beaten < MOBA_TOPK, 1.0, 0.0), 0.0)


def _moba_attn_kernel(q_ref, k_ref, vt_ref, bias_ref, km_ref, o_ref, acc_ref, sel_ref):
    qi = pl.program_id(2)
    qh = _split_heads(q_ref[0])
    n_bias = bias_ref.shape[1]
    _moba_select(qh, km_ref, sel_ref, qi)

    def group(g, stats, n_tiles, first):
        kg = _group_keys(k_ref, g, n_tiles, KEY_GROUP_EVEN)
        new = ()
        scores = _pair_scores(kg, qh)
        for h in range(HEADS_PER_STEP):
            s_tiles = scores[h]
            ons = []
            for c in range(n_tiles):
                j = g * KEY_GROUP_EVEN + c
                dt = n_tiles - 1 - c if first else qi - j
                s_tiles[c] = s_tiles[c] + bias_ref[h, dt if first else jnp.minimum(dt, n_bias - 1)]
                ons.append(None if first and dt == 0 else sel_ref[h, pl.ds(j, 1), :] > 0.5)
            new += _softmax_group(s_tiles, ons, _head_rows(vt_ref, g, h, n_tiles), acc_ref, h,
                                  None if first else stats[2 * h:2 * h + 2], first)
        return new

    stats = _sweep(qi, group, KEY_GROUP_EVEN)
    _store_step_output(o_ref, [acc_ref[h] / stats[2 * h + 1] for h in range(HEADS_PER_STEP)])


def _moba_attn(proj, vt, bias, km, batch, seq):
    t = ATT_TILE
    n_steps = N_HEADS_A // HEADS_PER_STEP
    n_bias = bias.shape[1]
    return pl.pallas_call(
        _moba_attn_kernel,
        grid=(batch, n_steps, seq // t),
        in_specs=[pl.BlockSpec((1, t, STEP_LANES), lambda b, p, i: (b, i, p)),
                  pl.BlockSpec((1, seq, STEP_LANES), lambda b, p, i: (b, 0, n_steps + p)),
                  pl.BlockSpec((1, PAIRS_PER_STEP) + vt.shape[2:], lambda b, p, i: (b, p, 0, 0, 0)),
                  pl.BlockSpec((HEADS_PER_STEP, n_bias, t, t), lambda b, p, i: (p, 0, 0, 0)),
                  pl.BlockSpec((1, seq // MOBA_BLOCK, STEP_LANES), lambda b, p, i: (b, 0, p))],
        out_specs=pl.BlockSpec((1, t, STEP_LANES), lambda b, p, i: (b, i, p)),
        out_shape=jax.ShapeDtypeStruct((batch, seq, N_HEADS_A * HEAD_DIM), BF16),
        scratch_shapes=[pltpu.VMEM((HEADS_PER_STEP, HEAD_DIM, t), F32),
                        pltpu.VMEM((HEADS_PER_STEP, seq // MOBA_BLOCK, t), F32)],
        compiler_params=_params(3),
        name="moba_attn",
    )(proj, proj, vt, bias, km)


def _dilated_far_kernel(q_ref, k_ref, v_ref, bias_ref, o_ref):
    t = ATT_TILE
    qh = _split_heads(q_ref[0, 0])
    keys = k_ref[0, 0]
    vt = v_ref[0, 0].astype(F32).T.astype(BF16)
    accs, maxes, sums = [], [], []
    for h, (s,) in enumerate(_pair_scores(keys, qh)):
        s = s + bias_ref[h, 0]
        m = jnp.max(s, axis=0, keepdims=True)
        p = jnp.exp2(s - jnp.where(m > -jnp.inf, m, 0.0))
        maxes.append(m)
        sums.append(jnp.sum(p, axis=0, keepdims=True))
        accs.append(_dot(vt[h * HEAD_DIM:(h + 1) * HEAD_DIM], p.astype(BF16)))
    pad = jnp.zeros((DIL_FAR_LANES - N_HEADS_B * (HEAD_DIM + 2), t), F32)
    o_ref[0, 0] = jnp.concatenate(accs + maxes + sums + [pad], axis=0).T


def _dilated_far(qkv_cls, bias, batch):
    t = ATT_TILE
    wb = N_HEADS_B * HEAD_DIM
    cols = lambda group: pl.BlockSpec((1, 1, t, wb), lambda b, c: (b, c, 0, group))
    return pl.pallas_call(
        _dilated_far_kernel,
        grid=(batch, DIL_CLASSES),
        in_specs=[cols(0), cols(1), cols(2),
                  pl.BlockSpec((N_HEADS_B, 1, t, t), lambda b, c: (0, 0, 0, 0))],
        out_specs=pl.BlockSpec((1, 1, t, DIL_FAR_LANES), lambda b, c: (b, c, 0, 0)),
        out_shape=jax.ShapeDtypeStruct((batch, DIL_CLASSES, t, DIL_FAR_LANES), F32),
        compiler_params=_params(2),
        name="dilated_far",
    )(qkv_cls, qkv_cls, qkv_cls, bias)


def _dilated_near_kernel(q_ref, k_ref, vt_ref, bias_ref, far_ref, o_ref, acc_ref):
    t = ATT_TILE
    qi = pl.program_id(1)
    qh = _split_heads(q_ref[0])

    def tiles(n_tiles):
        first_tile = qi - (n_tiles - 1)
        keys = k_ref[0, pl.ds(pl.multiple_of(first_tile * t, t), n_tiles * t), :]
        stats = ()
        scores = _pair_scores(keys, qh)
        for h in range(N_HEADS_B):
            s_tiles = [s + bias_ref[h, n_tiles - 1 - c] for c, s in enumerate(scores[h])]
            rows = slice((h % 2) * HEAD_DIM, (h % 2 + 1) * HEAD_DIM)
            vt_h = jnp.concatenate([vt_ref[0, h // 2, first_tile + c, rows, :] for c in range(n_tiles)], axis=1)
            stats += _softmax_group(s_tiles, [None] * n_tiles, vt_h, acc_ref, h, None, True)
        return stats

    stats = lax.switch(jnp.minimum(qi, DIL_NEAR_TILES - 1),
                       [functools.partial(tiles, n + 1) for n in range(DIL_NEAR_TILES)])
    far = far_ref[0].T
    far_m0, far_l0 = N_HEADS_B * HEAD_DIM, N_HEADS_B * HEAD_DIM + N_HEADS_B
    outs = []
    for h in range(N_HEADS_B):
        m_near, l_near = stats[2 * h], stats[2 * h + 1]
        m_far, l_far = far[far_m0 + h:far_m0 + h + 1], far[far_l0 + h:far_l0 + h + 1]
        m = jnp.maximum(m_near, m_far)
        w_near, w_far = jnp.exp2(m_near - m), jnp.exp2(m_far - m)
        acc_far = far[h * HEAD_DIM:(h + 1) * HEAD_DIM]
        outs.append((w_near * acc_ref[h] + w_far * acc_far) / (w_near * l_near + w_far * l_far))
    _store_step_output(o_ref, outs)


def _dilated_near(proj, vt, bias, far, batch, seq, lane0):
    t = ATT_TILE
    wb = N_HEADS_B * HEAD_DIM
    first = lane0 // wb
    return pl.pallas_call(
        _dilated_near_kernel,
        grid=(batch, seq // t),
        in_specs=[pl.BlockSpec((1, t, wb), lambda b, i: (b, i, first)),
                  pl.BlockSpec((1, seq, wb), lambda b, i: (b, 0, first + 1)),
                  pl.BlockSpec((1,) + vt.shape[1:], lambda b, i: (b, 0, 0, 0, 0)),
                  pl.BlockSpec((N_HEADS_B, DIL_NEAR_TILES, t, t), lambda b, i: (0, 0, 0, 0)),
                  pl.BlockSpec((1, t, DIL_FAR_LANES), lambda b, i: (b, i, 0))],
        out_specs=pl.BlockSpec((1, t, wb), lambda b, i: (b, i, 0)),
        out_shape=jax.ShapeDtypeStruct((batch, seq, wb), BF16),
        scratch_shapes=[pltpu.VMEM((N_HEADS_B, HEAD_DIM, t), F32)],
        compiler_params=_params(2),
        name="dilated_near",
    )(proj, proj, vt, bias, far)


def _dilated_attn(proj, qkv_cls, vt_tiles, bias_near, bias_far, batch, seq, lane0):
    far = _dilated_far(qkv_cls, bias_far, batch)
    far = far.transpose(0, 2, 1, 3).reshape(batch, seq, DIL_FAR_LANES)
    return _dilated_near(proj, vt_tiles, bias_near, far, batch, seq, lane0)


def _odd_proj_kernel(x_ref, mod_ref, g_ref, win_ref, gq_ref, gkv_ref, wq_ref, wkv_ref,
                     cq_ref, sq_ref, ck_ref, sk_ref, qm_ref, km_ref, vtm_ref, sb_ref, vts_ref):
    d = D_MODEL
    mod = mod_ref[0]
    h = _prenorm(x_ref[...], g_ref[...], mod[:, d:2 * d], mod[:, 0:d])
    p = _dot(h.astype(BF16), win_ref[...])
    o = MLA_Q_RANK + MLA_KV_RANK
    c_q, c_kv = p[:, 0:MLA_Q_RANK], p[:, MLA_Q_RANK:o]
    k_rope, k_rope_swapped = p[:, o:o + LANES], p[:, o + LANES:o + 2 * LANES]
    sb0 = o + 2 * LANES
    n_qk = 2 * N_HEADS_D * HEAD_DIM
    sb_ref[...] = p[:, sb0:sb0 + n_qk].astype(BF16)
    _store_vt(vts_ref, p[:, sb0 + n_qk:])
    q12 = _dot(_rms(c_q, gq_ref[...]).astype(BF16), wq_ref[...])
    kv = _dot(_rms(c_kv, gkv_ref[...]).astype(BF16), wkv_ref[...])
    k_pe = k_rope * ck_ref[...] + k_rope_swapped * sk_ref[...]
    cq, sq = cq_ref[...], sq_ref[...]
    half = N_HEADS_C * LANES
    for hh in range(N_HEADS_C):
        cols = slice(hh * LANES, (hh + 1) * LANES)
        swapped = slice(half + hh * LANES, half + (hh + 1) * LANES)
        qm_ref[:, cols] = (q12[:, cols] * cq + q12[:, swapped] * sq).astype(BF16)
        km_ref[:, cols] = (kv[:, cols] + k_pe).astype(BF16)
    _store_vt(vtm_ref, kv[:, half:])


def _odd_proj(x, mod, g, w_in, gq, gkv, wq, wkv, tables, seq):
    rows, d = x.shape
    tm = ROW_TILE
    per_seq = seq // tm
    n_in = w_in.shape[1]
    n_sb = 2 * N_HEADS_D * HEAD_DIM
    n_q = N_HEADS_C * LANES
    n_pairs = N_HEADS_C // 2
    const = lambda i: (0, 0)
    table_spec = pl.BlockSpec((tm, LANES), lambda i: (i % per_seq, 0))
    vt_spec = _vt_spec(tm, per_seq, n_pairs, KEY_GROUP_ODD)
    vt_shape = _vt_shape(rows // seq, seq, n_pairs, KEY_GROUP_ODD)
    return pl.pallas_call(
        _odd_proj_kernel,
        grid=(rows // tm,),
        in_specs=[pl.BlockSpec((tm, d), lambda i: (i, 0)),
                  pl.BlockSpec((1, 1, 6 * d), lambda i: (i // per_seq, 0, 0)),
                  pl.BlockSpec((1, d), const),
                  pl.BlockSpec((d, n_in), const),
                  pl.BlockSpec((1, MLA_Q_RANK), const),
                  pl.BlockSpec((1, MLA_KV_RANK), const),
                  pl.BlockSpec(wq.shape, const),
                  pl.BlockSpec(wkv.shape, const),
                  table_spec, table_spec, table_spec, table_spec],
        out_specs=[pl.BlockSpec((tm, n_q), lambda i: (i, 0)),
                   pl.BlockSpec((tm, n_q), lambda i: (i, 0)),
                   vt_spec,
                   pl.BlockSpec((tm, n_sb), lambda i: (i, 0)),
                   vt_spec],
        out_shape=[jax.ShapeDtypeStruct((rows, n_q), BF16),
                   jax.ShapeDtypeStruct((rows, n_q), BF16),
                   vt_shape,
                   jax.ShapeDtypeStruct((rows, n_sb), BF16),
                   vt_shape],
        compiler_params=_params(1),
        name="odd_proj",
    )(x, mod, g, w_in, gq, gkv, wq, wkv, *tables)


def _mla_attn_kernel(q_ref, k_ref, vt_ref, o_ref, acc_ref):
    qi = pl.program_id(2)
    q_step = q_ref[0]
    qh = [q_step[:, h * LANES:(h + 1) * LANES] for h in range(HEADS_PER_STEP)]

    def group(g, stats, n_tiles, first):
        kg = _group_keys(k_ref, g, n_tiles, KEY_GROUP_ODD)
        new = ()
        scores = [_tile_scores(kg[:, h * LANES:(h + 1) * LANES], qh[h]) for h in range(HEADS_PER_STEP)]
        for h in range(HEADS_PER_STEP):
            s_tiles = scores[h]
            if first:
                key, query = _tile_iotas()
                s_tiles[-1] = jnp.where(key <= query, s_tiles[-1], -jnp.inf)
            new += _softmax_group(s_tiles, [None] * n_tiles, _head_rows(vt_ref, g, h, n_tiles), acc_ref, h,
                                  None if first else stats[2 * h:2 * h + 2], first)
        return new

    stats = _sweep(qi, group, KEY_GROUP_ODD)
    _store_step_output(o_ref, [acc_ref[h] / stats[2 * h + 1] for h in range(HEADS_PER_STEP)])


def _mla_attn(qm, km, vt, batch, seq):
    t = ATT_TILE
    n_steps = N_HEADS_C // HEADS_PER_STEP
    qk_lanes = HEADS_PER_STEP * LANES
    return pl.pallas_call(
        _mla_attn_kernel,
        grid=(batch, n_steps, seq // t),
        in_specs=[pl.BlockSpec((1, t, qk_lanes), lambda b, p, i: (b, i, p)),
                  pl.BlockSpec((1, seq, qk_lanes), lambda b, p, i: (b, 0, p)),
                  pl.BlockSpec((1, PAIRS_PER_STEP) + vt.shape[2:], lambda b, p, i: (b, p, 0, 0, 0))],
        out_specs=pl.BlockSpec((1, t, STEP_LANES), lambda b, p, i: (b, i, p)),
        out_shape=jax.ShapeDtypeStruct((batch, seq, N_HEADS_C * MLA_V_DIM), BF16),
        scratch_shapes=[pltpu.VMEM((HEADS_PER_STEP, HEAD_DIM, t), F32)],
        compiler_params=_params(3),
        name="mla_attn",
    )(qm, km, vt)


def _stick_attn_kernel(q_ref, k_ref, vt_ref, o_ref, acc_ref):
    t = ATT_TILE
    qi = pl.program_id(2)
    qh = _split_heads(q_ref[0])
    key = lax.broadcasted_iota(jnp.int32, (t, t), 0)
    other = lax.broadcasted_iota(jnp.int32, (t, t), 1)
    from_here = jnp.where(other >= key, 1.0, 0.0).astype(BF16)
    from_here2 = jnp.concatenate([from_here, from_here], axis=1)

    strict = key < other

    def group(g, carries, n_tiles, first):
        kg = _group_keys(k_ref, g, n_tiles, KEY_GROUP_ODD)
        new = ()
        scores = _pair_scores(kg, qh)
        for h in range(HEADS_PER_STEP):
            z_tiles = scores[h]
            drop_tiles = [jnp.maximum(z, 0.0) + jnp.log(1.0 + jnp.exp2(jnp.abs(z) * -LOG2_E)) for z in z_tiles]
            if first:
                drop_tiles[-1] = jnp.where(strict, drop_tiles[-1], 0.0)
            run = None if first else carries[h]
            a_tiles = [None] * n_tiles
            for c in reversed(range(n_tiles)):
                tail = _dot(from_here2, jnp.concatenate(_split_bf16(drop_tiles[c]), axis=0))
                tile_total = tail[0:1, :]
                if run is not None:
                    tail = tail + run
                a_tiles[c] = jnp.exp(z_tiles[c] - tail)
                run = tile_total if run is None else run + tile_total
            if first:
                a_tiles[-1] = jnp.where(strict, a_tiles[-1], 0.0)
            pv = _dot(_head_rows(vt_ref, g, h, n_tiles), jnp.concatenate(a_tiles, axis=0).astype(BF16))
            acc_ref[h] = pv if first else acc_ref[h] + pv
            new += (run,)
        return new

    _sweep(qi, group, KEY_GROUP_ODD)
    _store_step_output(o_ref, [acc_ref[h] for h in range(HEADS_PER_STEP)])


def _stick_attn(sb, vt, batch, seq):
    t = ATT_TILE
    n_steps = N_HEADS_D // HEADS_PER_STEP
    return pl.pallas_call(
        _stick_attn_kernel,
        grid=(batch, n_steps, seq // t),
        in_specs=[pl.BlockSpec((1, t, STEP_LANES), lambda b, p, i: (b, i, p)),
                  pl.BlockSpec((1, seq, STEP_LANES), lambda b, p, i: (b, 0, n_steps + p)),
                  pl.BlockSpec((1, PAIRS_PER_STEP) + vt.shape[2:], lambda b, p, i: (b, p, 0, 0, 0))],
        out_specs=pl.BlockSpec((1, t, STEP_LANES), lambda b, p, i: (b, i, p)),
        out_shape=jax.ShapeDtypeStruct((batch, seq, N_HEADS_D * HEAD_DIM), BF16),
        scratch_shapes=[pltpu.VMEM((HEADS_PER_STEP, HEAD_DIM, t), F32)],
        compiler_params=_params(3),
        name="stick_attn",
    )(sb, sb, vt)


def _post_attn_kernel(oa_ref, ob_ref, oah_ref, obh_ref, x_ref, xh_ref, mod_ref, gmix_ref, gpre_ref, gpost_ref,
                      wo_ref, wup_ref, cw_ref, cb_ref, wd_ref, o_ref, x1_ref, h_ref, u_ref, acc_ref, *, per_seq):
    d = D_MODEL
    tm = ROW_TILE
    halo = CONV_HALO
    cw = FF_CHUNK
    i = pl.program_id(0)
    mod = mod_ref[0]
    gate_m, shift, scale, gate_f = mod[:, 2 * d:3 * d], mod[:, 3 * d:4 * d], mod[:, 4 * d:5 * d], mod[:, 5 * d:6 * d]
    n_first = oa_ref.shape[1]

    def mixed(oa, ob, x):
        y = _dot(oa, wo_ref[0:n_first, :]) + _dot(ob, wo_ref[n_first:2 * n_first, :])
        return x + gate_m * _rms(y, gmix_ref[...])

    g = gpre_ref[...]
    ahead = _prenorm(mixed(oah_ref[...], obh_ref[...], xh_ref[...]), g, scale, shift)
    ahead = jnp.where(i % per_seq == 0, 0.0, ahead)
    h_ref[0:halo, :] = ahead.astype(BF16)
    x1_ref[...] = mixed(oa_ref[...], ob_ref[...], x_ref[...])
    h_ref[halo:halo + tm, :] = _prenorm(x1_ref[...], g, scale, shift).astype(BF16)

    def up(ch):
        slot = ch % 2
        for half in range(2):
            cols = slice(half * D_FF + ch * cw, half * D_FF + (ch + 1) * cw)
            u_ref[slot, half] = _dot(h_ref[...], wup_ref[:, cols])

    def conv(slot, half, ch):
        cols = slice(half * D_FF + ch * cw, half * D_FF + (ch + 1) * cw)
        w = cw_ref[:, cols]
        out = None
        for tap in range(CONV_WIDTH):
            start = halo - (CONV_WIDTH - 1 - tap)
            term = w[tap:tap + 1, :] * u_ref[slot, half, start:start + tm, :]
            out = term if out is None else out + term
        return out + cb_ref[:, cols]

    def down(ch, act):
        part = _dot(act, wd_ref[ch * cw:(ch + 1) * cw, :])
        if ch == 0:
            acc_ref[...] = part
        else:
            acc_ref[...] += part

    up(0)
    act = None
    for ch in range(N_FF_CHUNKS):
        if ch + 1 < N_FF_CHUNKS:
            up(ch + 1)
        if act is not None:
            down(ch - 1, act)
        slot = ch % 2
        act = (jax.nn.gelu(conv(slot, 0, ch), approximate=True) * conv(slot, 1, ch)).astype(BF16)
    down(N_FF_CHUNKS - 1, act)

    o_ref[...] = x1_ref[...] + gate_f * _rms(acc_ref[...], gpost_ref[...])


def _post_attn(oa, ob, x, mod, g_mix, g_pre, g_post, w_out, w_up, conv_w, conv_b, w_down, seq):
    rows, d = x.shape
    tm = ROW_TILE
    halo = CONV_HALO
    per_seq = seq // tm
    n_first = oa.shape[1]
    const = lambda i: (0, 0)
    tile = lambda i: (i, 0)
    ahead = lambda i: (jnp.maximum(i * (tm // halo) - 1, 0), 0)
    once = dict(pipeline_mode=pl.Buffered(1))
    return pl.pallas_call(
        functools.partial(_post_attn_kernel, per_seq=per_seq),
        grid=(rows // tm,),
        in_specs=[pl.BlockSpec((tm, n_first), tile), pl.BlockSpec((tm, n_first), tile),
                  pl.BlockSpec((halo, n_first), ahead), pl.BlockSpec((halo, n_first), ahead),
                  pl.BlockSpec((tm, d), tile), pl.BlockSpec((halo, d), ahead),
                  pl.BlockSpec((1, 1, 6 * d), lambda i: (i // per_seq, 0, 0)),
                  pl.BlockSpec((1, d), const), pl.BlockSpec((1, d), const), pl.BlockSpec((1, d), const),
                  pl.BlockSpec(w_out.shape, const, **once),
                  pl.BlockSpec(w_up.shape, const, **once),
                  pl.BlockSpec(conv_w.shape, const),
                  pl.BlockSpec(conv_b.shape, const),
                  pl.BlockSpec(w_down.shape, const, **once)],
        out_specs=pl.BlockSpec((tm, d), tile),
        out_shape=jax.ShapeDtypeStruct((rows, d), F32),
        scratch_shapes=[pltpu.VMEM((tm, d), F32),
                        pltpu.VMEM((tm + halo, d), BF16),
                        pltpu.VMEM((2, 2, tm + halo, FF_CHUNK), F32),
                        pltpu.VMEM((tm, d), F32)],
        compiler_params=_params(1),
        name="post_attn",
    )(oa, ob, oa, ob, x, x, mod, g_mix, g_pre, g_post, w_out, w_up, conv_w, conv_b, w_down)


def _rotate_half_cols(w):
    half = w.shape[-1] // 2
    return jnp.concatenate([-w[..., half:], w[..., :half]], axis=-1)


def _pad_cols(w, left, total):
    return jnp.pad(w, ((0, 0), (left, total - left - w.shape[1])))


def _rope_tables(seq):
    inv_freq = 1.0 / (ROPE_THETA ** (jnp.arange(0, MLA_ROPE_DIM, 2, dtype=F32) / MLA_ROPE_DIM))
    ang = jnp.arange(seq, dtype=F32)[:, None] * inv_freq[None, :]
    cos, sin = jnp.cos(ang), jnp.sin(ang)
    cos2 = _pad_cols(jnp.concatenate([cos, cos], axis=1), MLA_NOPE_DIM, LANES)
    sin2 = _pad_cols(jnp.concatenate([sin, sin], axis=1), MLA_NOPE_DIM, LANES)
    scale = (MLA_NOPE_DIM + MLA_ROPE_DIM) ** -0.5
    nope_ones = _pad_cols(jnp.ones((seq, MLA_NOPE_DIM), F32), 0, LANES)
    scale = scale * LOG2_E
    return (scale * (cos2 + nope_ones), scale * sin2, cos2, sin2)


def _odd_weights(w_in, w_uq, w_ukv):
    o = MLA_Q_RANK + MLA_KV_RANK
    w_rope = w_in[:, o:o + MLA_ROPE_DIM]
    scale_d = HEAD_DIM ** -0.5
    wd = N_HEADS_D * HEAD_DIM
    sb0 = o + MLA_ROPE_DIM
    w_in2 = jnp.concatenate([
        w_in[:, :o],
        _pad_cols(w_rope, MLA_NOPE_DIM, LANES),
        _pad_cols(_rotate_half_cols(w_rope), MLA_NOPE_DIM, LANES),
        w_in[:, sb0:sb0 + wd] * scale_d,
        w_in[:, sb0 + wd:],
    ], axis=1).astype(BF16)
    qd = MLA_NOPE_DIM + MLA_ROPE_DIM
    uq = w_uq.reshape(MLA_Q_RANK, N_HEADS_C, qd)
    plain = jnp.pad(uq, ((0, 0), (0, 0), (0, LANES - qd)))
    swapped = jnp.pad(_rotate_half_cols(uq[..., MLA_NOPE_DIM:]),
                      ((0, 0), (0, 0), (MLA_NOPE_DIM, LANES - qd)))
    wq = jnp.concatenate([plain.reshape(MLA_Q_RANK, -1), swapped.reshape(MLA_Q_RANK, -1)], axis=1).astype(BF16)
    ukv = w_ukv.reshape(MLA_KV_RANK, N_HEADS_C, MLA_NOPE_DIM + MLA_V_DIM)
    k_nope = jnp.pad(ukv[..., :MLA_NOPE_DIM], ((0, 0), (0, 0), (0, LANES - MLA_NOPE_DIM)))
    v = ukv[..., MLA_NOPE_DIM:]
    wkv = jnp.concatenate([k_nope.reshape(MLA_KV_RANK, -1), v.reshape(MLA_KV_RANK, -1)], axis=1).astype(BF16)
    return w_in2, wq, wkv


def _even_weights(w_in):
    wa = N_HEADS_A * HEAD_DIM
    scale = HEAD_DIM ** -0.5 * LOG2_E
    col = jnp.arange(w_in.shape[1])
    is_q = (col < wa) | ((col >= 3 * wa) & (col < 4 * wa))
    return (w_in * jnp.where(is_q, scale, 1.0)).astype(BF16)


def kernel(x, c, rel_bias, ada_w, ada_b, mix_pre_g, mix_post_g, ffn_pre_g, ffn_post_g, ab_w_in, ab_w_out,
           cd_w_in, mla_q_norm_g, mla_kv_norm_g, mla_w_uq, mla_w_ukv, cd_w_out, ffn_w_up, ffn_conv_w,
           ffn_conv_b, ffn_w_down):
    batch, seq, d = x.shape
    widest_group = max(KEY_GROUP_EVEN, KEY_GROUP_ODD) * ATT_TILE
    assert d == D_MODEL and seq % widest_group == 0 and (KEY_GROUP_EVEN * ATT_TILE) % ROW_TILE == 0
    assert seq == DIL_CLASSES * ATT_TILE
    rows = batch * seq
    xf = x.reshape(rows, d)

    mods = _mods(c, ada_w, ada_b)
    bias_a = _bias_tiles(rel_bias, 0, N_HEADS_A, MOBA_BIAS_TILES, 1, None, "moba_bias_tiles")
    bias_near = _bias_tiles(rel_bias, N_HEADS_A, N_HEADS_B, DIL_NEAR_TILES, 1, DIL_NEAR_BRANCHES,
                            "dilated_near_bias_tiles")
    bias_far = _bias_tiles(rel_bias, N_HEADS_A, N_HEADS_B, 1, DIL_CLASSES, DIL_FAR_BRANCHES,
                           "dilated_far_bias_tiles")
    tables = _rope_tables(seq)

    for layer in range(DEPTH):
        mod = mods[layer].reshape(batch, 1, 6 * d)
        i = layer // 2
        if layer % 2 == 0:
            proj, km, vta, vtb, qkv_cls = _even_proj(xf, mod, mix_pre_g[layer].reshape(1, d),
                                                     _even_weights(ab_w_in[i]), seq)
            proj = proj.reshape(batch, seq, -1)
            km = km.reshape(batch, seq // MOBA_BLOCK, -1)
            o_first = _moba_attn(proj, vta, bias_a, km, batch, seq)
            o_second = _dilated_attn(proj, qkv_cls, vtb, bias_near, bias_far, batch, seq,
                                     3 * N_HEADS_A * HEAD_DIM)
            w_out = ab_w_out[i]
        else:
            w_in2, wq, wkv = _odd_weights(cd_w_in[i], mla_w_uq[i], mla_w_ukv[i])
            qm, km, vtm, sb, vts = _odd_proj(xf, mod, mix_pre_g[layer].reshape(1, d), w_in2,
                                             mla_q_norm_g[i].reshape(1, -1), mla_kv_norm_g[i].reshape(1, -1),
                                             wq, wkv, tables, seq)
            o_first = _mla_attn(qm.reshape(batch, seq, -1), km.reshape(batch, seq, -1), vtm, batch, seq)
            o_second = _stick_attn(sb.reshape(batch, seq, -1), vts, batch, seq)
            w_out = cd_w_out[i]
        xf = _post_attn(o_first.reshape(rows, -1), o_second.reshape(rows, -1), xf, mod,
                        mix_post_g[layer].reshape(1, d), ffn_pre_g[layer].reshape(1, d),
                        ffn_post_g[layer].reshape(1, d), w_out.astype(BF16), ffn_w_up[layer].astype(BF16),
                        ffn_conv_w[layer], ffn_conv_b[layer].reshape(1, -1), ffn_w_down[layer].astype(BF16), seq)
    return xf.reshape(batch, seq, d)
```

```python
import functools
import math

import jax
import jax.numpy as jnp
from jax import lax
from jax.experimental import pallas as pl
from jax.experimental.pallas import tpu as pltpu

F32 = jnp.float32
BF16 = jnp.bfloat16

D_MODEL = 1024
DEPTH = 4
HEAD_DIM = 64
N_HEADS_A = 8
N_HEADS_B = 8
N_HEADS_C = 8
N_HEADS_D = 8
MOBA_BLOCK = 256
MOBA_TOPK = 3
DILATED_BRANCHES = ((128, 1), (512, 4), (2048, 16))
MLA_Q_RANK = 256
MLA_KV_RANK = 256
MLA_NOPE_DIM = 64
MLA_ROPE_DIM = 32
MLA_V_DIM = 64
ROPE_THETA = 10000.0
REL_BUCKETS = 32
REL_MAX_DIST = 2048
D_FF = 2816
CONV_WIDTH = 3
NORM_EPS = 1e-6
LOG2_E = math.log2(math.e)

LANES = 128
SUBLANES = 8
BF16_SUBLANES = 16
VMEM_LIMIT_BYTES = 56 * 1024 * 1024

ATT_TILE = MOBA_BLOCK
KEY_GROUP_EVEN = 8
KEY_GROUP_ODD = 8
PAIRS_PER_STEP = 2
HEADS_PER_STEP = 2 * PAIRS_PER_STEP
STEP_LANES = PAIRS_PER_STEP * LANES
ROW_TILE = 512
FF_CHUNK = 256
N_FF_CHUNKS = D_FF // FF_CHUNK
CONV_HALO = BF16_SUBLANES
MOBA_BIAS_TILES = REL_MAX_DIST // ATT_TILE + 2
DIL_SPLIT = DILATED_BRANCHES[1][0]
DIL_CLASSES = DILATED_BRANCHES[2][1]
DIL_NEAR_TILES = DIL_SPLIT // ATT_TILE + 1
DIL_FAR_LANES = 5 * LANES
assert DIL_FAR_LANES >= N_HEADS_B * (HEAD_DIM + 2)
DIL_NEAR_BRANCHES = tuple((-1, min(window, DIL_SPLIT), dil) for window, dil in DILATED_BRANCHES)
DIL_FAR_BRANCHES = ((DIL_SPLIT, DILATED_BRANCHES[2][0], DIL_CLASSES),)
assert all(window <= DIL_SPLIT for window, _ in DILATED_BRANCHES[:2]) and DIL_SPLIT % DIL_CLASSES == 0

_NT = (((1,), (1,)), ((), ()))


def _bucket_lower_bounds():
    max_exact = REL_BUCKETS // 2
    ratio = REL_MAX_DIST // max_exact
    n_log = REL_BUCKETS - max_exact
    lows = list(range(max_exact + 1))
    for k in range(1, n_log):
        d = lows[-1]
        while d ** n_log < (max_exact ** n_log) * (ratio ** k):
            d += 1
        lows.append(d)
    return lows


_BUCKET_LOW = _bucket_lower_bounds()


def _dot(a, b):
    return jnp.dot(a, b, preferred_element_type=F32)


def _dot_nt(a, b):
    return lax.dot_general(a, b, _NT, preferred_element_type=F32)


def _split_bf16(x):
    hi = x.astype(BF16)
    lo = (x - hi.astype(F32)).astype(BF16)
    return hi, lo


def _rms(x, g):
    return (x * lax.rsqrt(jnp.mean(x * x, axis=-1, keepdims=True) + NORM_EPS)) * g


def _prenorm(x, g, scale, shift):
    return _rms(x, g) * (1.0 + scale) + shift


def _params(n_grid_dims):
    return pltpu.CompilerParams(dimension_semantics=("arbitrary",) * n_grid_dims,
                                vmem_limit_bytes=VMEM_LIMIT_BYTES)


def _mods_kernel(c_ref, w_ref, b_ref, o_ref):
    c = c_ref[...]
    cond = c * jax.nn.sigmoid(c)
    c_hi, c_lo = _split_bf16(cond)
    w_hi, w_lo = _split_bf16(w_ref[0])
    o_ref[0] = _dot(c_hi, w_hi) + _dot(c_hi, w_lo) + _dot(c_lo, w_hi) + b_ref[0]


def _mods(c, ada_w, ada_b):
    b, d = c.shape
    rows = BF16_SUBLANES
    n_out = ada_w.shape[-1]
    tn = n_out // 4
    c_pad = jnp.zeros((rows, d), F32).at[:b].set(c)
    out = pl.pallas_call(
        _mods_kernel,
        grid=(DEPTH, n_out // tn),
        in_specs=[pl.BlockSpec((rows, d), lambda l, j: (0, 0)),
                  pl.BlockSpec((1, d, tn), lambda l, j: (l, 0, j)),
                  pl.BlockSpec((1, 1, tn), lambda l, j: (l, 0, j))],
        out_specs=pl.BlockSpec((1, rows, tn), lambda l, j: (l, 0, j)),
        out_shape=jax.ShapeDtypeStruct((DEPTH, rows, n_out), F32),
        compiler_params=_params(2),
        name="ada_mods",
    )(c_pad, ada_w, ada_b.reshape(DEPTH, 1, n_out))
    return out[:, :b]


def _bias_tiles_kernel(tab_ref, o_ref, *, head_off, step, branches):
    h = pl.program_id(0) + head_off
    t = ATT_TILE
    offsets = lax.broadcasted_iota(jnp.int32, (SUBLANES, 2 * t), 1) - t
    for d in range(o_ref.shape[1]):
        dist = (d * t + offsets) * step
        val = jnp.full(dist.shape, tab_ref[h, 0], F32)
        for b in range(1, REL_BUCKETS):
            val = jnp.where(dist >= _BUCKET_LOW[b], tab_ref[h, b], val)
        if branches is not None:
            mult = jnp.zeros(dist.shape, F32)
            for beyond, window, dil in branches:
                hit = jnp.where(dist <= window, jnp.where((dist & (dil - 1)) == 0, 1.0, 0.0), 0.0)
                mult = mult + jnp.where(dist > beyond, hit, 0.0)
            log_mult = jnp.where(mult > 2.5, math.log(3.0), jnp.where(mult > 1.5, math.log(2.0), 0.0))
            val = jnp.where(mult > 0.5, val + log_mult, -jnp.inf)
        val = jnp.where(dist >= 0, val * LOG2_E, -jnp.inf)
        strip = jnp.concatenate([val] * (t // SUBLANES), axis=0)
        rotated = pltpu.roll(strip, 0, 1, stride=1, stride_axis=0)
        o_ref[0, d] = rotated[:, t:2 * t]


def _bias_tiles(rel_bias, head_off, n_heads, n_tiles, step, branches, name):
    t = ATT_TILE
    return pl.pallas_call(
        functools.partial(_bias_tiles_kernel, head_off=head_off, step=step, branches=branches),
        grid=(n_heads,),
        in_specs=[pl.BlockSpec(memory_space=pltpu.SMEM)],
        out_specs=pl.BlockSpec((1, n_tiles, t, t), lambda h: (h, 0, 0, 0)),
        out_shape=jax.ShapeDtypeStruct((n_heads, n_tiles, t, t), F32),
        compiler_params=_params(1),
        name=name,
    )(rel_bias)


def _store_vt(vt_ref, v):
    vt = v.T.astype(BF16)
    for p in range(vt.shape[0] // LANES):
        vt_ref[0, p, 0] = vt[p * LANES:(p + 1) * LANES, :]


def _vt_spec(tm, per_seq, n_pairs, key_group):
    per_group = key_group * ATT_TILE // tm
    return pl.BlockSpec((1, n_pairs, 1, LANES, tm),
                        lambda i: (i // per_seq, 0, (i % per_seq) // per_group, 0, (i % per_seq) % per_group))


def _vt_shape(batch, seq, n_pairs, key_group):
    group_keys = key_group * ATT_TILE
    return jax.ShapeDtypeStruct((batch, n_pairs, seq // group_keys, LANES, group_keys), BF16)


def _even_proj_kernel(x_ref, mod_ref, g_ref, w_ref, o_ref, km_ref, vta_ref, vtb_ref, cls_ref):
    d = D_MODEL
    mod = mod_ref[0]
    h = _prenorm(x_ref[...], g_ref[...], mod[:, d:2 * d], mod[:, 0:d])
    p = _dot(h.astype(BF16), w_ref[...])
    p_bf16 = p.astype(BF16)
    o_ref[...] = p_bf16
    tm = p.shape[0]
    per_class = tm // DIL_CLASSES
    out_row = lax.broadcasted_iota(jnp.int32, (tm, tm), 0)
    in_row = lax.broadcasted_iota(jnp.int32, (tm, tm), 1)
    source = (out_row % per_class) * DIL_CLASSES + out_row // per_class
    permute = jnp.where(in_row == source, 1.0, 0.0).astype(BF16)
    by_class = _dot(permute, p_bf16[:, 3 * N_HEADS_A * HEAD_DIM:]).astype(BF16)
    for cls in range(DIL_CLASSES):
        cls_ref[0, cls] = by_class[cls * per_class:(cls + 1) * per_class]
    wa = N_HEADS_A * HEAD_DIM
    ka = p[:, wa:2 * wa]
    nb = ka.shape[0] // MOBA_BLOCK
    km_ref[0] = jnp.mean(ka.reshape(nb, MOBA_BLOCK, wa), axis=1)
    _store_vt(vta_ref, p[:, 2 * wa:3 * wa])
    vtb = p[:, 5 * wa:6 * wa].T.astype(BF16)
    t = ATT_TILE
    for pair in range(vtb.shape[0] // LANES):
        for tile in range(vtb.shape[1] // t):
            vtb_ref[0, pair, tile] = vtb[pair * LANES:(pair + 1) * LANES, tile * t:(tile + 1) * t]


def _even_proj(x, mod, g, w_in, seq):
    rows, d = x.shape
    tm = ROW_TILE
    n = w_in.shape[1]
    wa = N_HEADS_A * HEAD_DIM
    per_seq = seq // tm
    n_pairs = N_HEADS_A // 2
    vt_spec = _vt_spec(tm, per_seq, n_pairs, KEY_GROUP_EVEN)
    vt_shape = _vt_shape(rows // seq, seq, n_pairs, KEY_GROUP_EVEN)
    tiles_per_step = tm // ATT_TILE
    vt_tile_spec = pl.BlockSpec((1, n_pairs, tiles_per_step, LANES, ATT_TILE),
                                lambda i: (i // per_seq, 0, i % per_seq, 0, 0))
    return pl.pallas_call(
        _even_proj_kernel,
        grid=(rows // tm,),
        in_specs=[pl.BlockSpec((tm, d), lambda i: (i, 0)),
                  pl.BlockSpec((1, 1, 6 * d), lambda i: (i // per_seq, 0, 0)),
                  pl.BlockSpec((1, d), lambda i: (0, 0)),
                  pl.BlockSpec((d, n), lambda i: (0, 0))],
        out_specs=[pl.BlockSpec((tm, n), lambda i: (i, 0)),
                   pl.BlockSpec((1, tm // MOBA_BLOCK, wa), lambda i: (i, 0, 0)),
                   vt_spec, vt_tile_spec,
                   pl.BlockSpec((1, DIL_CLASSES, tm // DIL_CLASSES, n - 3 * wa),
                                lambda i: (i // per_seq, 0, i % per_seq, 0))],
        out_shape=[jax.ShapeDtypeStruct((rows, n), BF16),
                   jax.ShapeDtypeStruct((rows // tm, tm // MOBA_BLOCK, wa), F32),
                   vt_shape, _vt_shape(rows // seq, seq, n_pairs, 1),
                   jax.ShapeDtypeStruct((rows // seq, DIL_CLASSES, seq // DIL_CLASSES, n - 3 * wa), BF16)],
        compiler_params=_params(1),
        name="even_proj",
    )(x, mod, g, w_in)


def _pair_lanes(x, h):
    pair = h // 2
    return x[:, pair * LANES:(pair + 1) * LANES]


def _split_heads(q_step):
    first = lax.broadcasted_iota(jnp.int32, (q_step.shape[0], LANES), 1) < HEAD_DIM
    heads = []
    for pair in range(q_step.shape[1] // LANES):
        q2 = q_step[:, pair * LANES:(pair + 1) * LANES]
        zero = jnp.zeros_like(q2)
        heads += [jnp.where(first, q2, zero), jnp.where(first, zero, q2)]
    return heads


def _head_rows(vt_ref, g, h, n_tiles):
    return vt_ref[0, h // 2, g, (h % 2) * HEAD_DIM:(h % 2 + 1) * HEAD_DIM, 0:n_tiles * ATT_TILE]


def _pair_scores(keys, qh):
    t = ATT_TILE
    n = keys.shape[0] // t
    scores = []
    for pair in range(len(qh) // 2):
        s = _dot_nt(keys[:, pair * LANES:(pair + 1) * LANES], jnp.concatenate([qh[2 * pair], qh[2 * pair + 1]], axis=0))
        for j in range(2):
            scores.append([s[c * t:(c + 1) * t, j * t:(j + 1) * t] for c in range(n)])
    return scores


def _tile_scores(keys, q_head):
    t = ATT_TILE
    return [_dot_nt(keys[c * t:(c + 1) * t], q_head) for c in range(keys.shape[0] // t)]


def _group_keys(k_ref, g, n_tiles, key_group):
    start = pl.multiple_of(g * key_group * ATT_TILE, key_group * ATT_TILE)
    return k_ref[0, pl.ds(start, n_tiles * ATT_TILE), :]


def _sweep(qi, group, key_group, n_far_groups=None):
    g_own = qi // key_group
    own = [functools.partial(group, g_own, None, n + 1, True) for n in range(key_group)]
    state = lax.switch(qi % key_group, own)
    n_past = g_own if n_far_groups is None else jnp.minimum(g_own, n_far_groups)
    return lax.fori_loop(0, n_past, lambda n, st: group(g_own - 1 - n, st, key_group, False), state)


def _tile_iotas():
    t = ATT_TILE
    return lax.broadcasted_iota(jnp.int32, (t, t), 0), lax.broadcasted_iota(jnp.int32, (t, t), 1)


def _store_step_output(o_ref, outs_t):
    o_ref[0] = jnp.concatenate(outs_t, axis=0).T.astype(BF16)


def _softmax_group(s_tiles, ons, vt_h, acc_ref, h, stats, first):
    maxes = []
    for s, on in zip(s_tiles, ons):
        mx = jnp.max(s, axis=0, keepdims=True)
        maxes.append(mx if on is None else jnp.where(on, mx, -jnp.inf))
    m_new = functools.reduce(jnp.maximum, maxes)
    if not first:
        m_old, l_old = stats
        m_new = jnp.maximum(m_new, m_old)
    ps = []
    l_add = None
    for s, on in zip(s_tiles, ons):
        p = jnp.exp2(s - (m_new if on is None else jnp.where(on, m_new, jnp.inf)))
        p_sum = jnp.sum(p, axis=0, keepdims=True)
        l_add = p_sum if l_add is None else l_add + p_sum
        ps.append(p.astype(BF16))
    pv = _dot(vt_h, jnp.concatenate(ps, axis=0))
    if first:
        acc_ref[h] = pv
        return m_new, l_add
    alpha = jnp.exp2(m_old - m_new)
    acc_ref[h] = alpha * acc_ref[h] + pv
    return m_new, alpha * l_old + l_add


def _moba_select(qh, km_ref, sel_ref, qi):
    t = ATT_TILE
    km_hi, km_lo = _split_bf16(km_ref[0])
    nb = km_hi.shape[0]
    blk = lax.broadcasted_iota(jnp.int32, (nb, t), 0)
    past = blk < qi
    for h in range(HEADS_PER_STEP):
        gate = _dot_nt(_pair_lanes(km_hi, h), qh[h]) + _dot_nt(_pair_lanes(km_lo, h), qh[h])
        gate = jnp.where(past, gate, -jnp.inf)
        beaten = jnp.zeros((nb, t), F32)
        for other in range(nb):
            row = gate[other:other + 1, :]
            wins = jnp.where(row > gate, 1.0, jnp.where(row == gate, jnp.where(blk > other, 1.0, 0.0), 0.0))
            beaten = beaten + wins
        sel_ref[h] = jnp.where(past, jnp.where(beaten < MOBA_TOPK, 1.0, 0.0), 0.0)


def _moba_attn_kernel(q_ref, k_ref, vt_ref, bias_ref, km_ref, o_ref, acc_ref, sel_ref):
    qi = pl.program_id(2)
    qh = _split_heads(q_ref[0])
    n_bias = bias_ref.shape[1]
    _moba_select(qh, km_ref, sel_ref, qi)

    def group(g, stats, n_tiles, first):
        kg = _group_keys(k_ref, g, n_tiles, KEY_GROUP_EVEN)
        new = ()
        scores = _pair_scores(kg, qh)
        for h in range(HEADS_PER_STEP):
            s_tiles = scores[h]
            ons = []
            for c in range(n_tiles):
                j = g * KEY_GROUP_EVEN + c
                dt = n_tiles - 1 - c if first else qi - j
                s_tiles[c] = s_tiles[c] + bias_ref[h, dt if first else jnp.minimum(dt, n_bias - 1)]
                ons.append(None if first and dt == 0 else sel_ref[h, pl.ds(j, 1), :] > 0.5)
            new += _softmax_group(s_tiles, ons, _head_rows(vt_ref, g, h, n_tiles), acc_ref, h,
                                  None if first else stats[2 * h:2 * h + 2], first)
        return new

    stats = _sweep(qi, group, KEY_GROUP_EVEN)
    _store_step_output(o_ref, [acc_ref[h] / stats[2 * h + 1] for h in range(HEADS_PER_STEP)])


def _moba_attn(proj, vt, bias, km, batch, seq):
    t = ATT_TILE
    n_steps = N_HEADS_A // HEADS_PER_STEP
    n_bias = bias.shape[1]
    return pl.pallas_call(
        _moba_attn_kernel,
        grid=(batch, n_steps, seq // t),
        in_specs=[pl.BlockSpec((1, t, STEP_LANES), lambda b, p, i: (b, i, p)),
                  pl.BlockSpec((1, seq, STEP_LANES), lambda b, p, i: (b, 0, n_steps + p)),
                  pl.BlockSpec((1, PAIRS_PER_STEP) + vt.shape[2:], lambda b, p, i: (b, p, 0, 0, 0)),
                  pl.BlockSpec((HEADS_PER_STEP, n_bias, t, t), lambda b, p, i: (p, 0, 0, 0)),
                  pl.BlockSpec((1, seq // MOBA_BLOCK, STEP_LANES), lambda b, p, i: (b, 0, p))],
        out_specs=pl.BlockSpec((1, t, STEP_LANES), lambda b, p, i: (b, i, p)),
        out_shape=jax.ShapeDtypeStruct((batch, seq, N_HEADS_A * HEAD_DIM), BF16),
        scratch_shapes=[pltpu.VMEM((HEADS_PER_STEP, HEAD_DIM, t), F32),
                        pltpu.VMEM((HEADS_PER_STEP, seq // MOBA_BLOCK, t), F32)],
        compiler_params=_params(3),
        name="moba_attn",
    )(proj, proj, vt, bias, km)


def _dilated_far_kernel(q_ref, k_ref, v_ref, bias_ref, o_ref):
    t = ATT_TILE
    qh = _split_heads(q_ref[0, 0])
    keys = k_ref[0, 0]
    vt = v_ref[0, 0].astype(F32).T.astype(BF16)
    accs, maxes, sums = [], [], []
    for h, (s,) in enumerate(_pair_scores(keys, qh)):
        s = s + bias_ref[h, 0]
        m = jnp.max(s, axis=0, keepdims=True)
        p = jnp.exp2(s - jnp.where(m > -jnp.inf, m, 0.0))
        maxes.append(m)
        sums.append(jnp.sum(p, axis=0, keepdims=True))
        accs.append(_dot(vt[h * HEAD_DIM:(h + 1) * HEAD_DIM], p.astype(BF16)))
    pad = jnp.zeros((DIL_FAR_LANES - N_HEADS_B * (HEAD_DIM + 2), t), F32)
    o_ref[0, 0] = jnp.concatenate(accs + maxes + sums + [pad], axis=0).T


def _dilated_far(qkv_cls, bias, batch):
    t = ATT_TILE
    wb = N_HEADS_B * HEAD_DIM
    cols = lambda group: pl.BlockSpec((1, 1, t, wb), lambda b, c: (b, c, 0, group))
    return pl.pallas_call(
        _dilated_far_kernel,
        grid=(batch, DIL_CLASSES),
        in_specs=[cols(0), cols(1), cols(2),
                  pl.BlockSpec((N_HEADS_B, 1, t, t), lambda b, c: (0, 0, 0, 0))],
        out_specs=pl.BlockSpec((1, 1, t, DIL_FAR_LANES), lambda b, c: (b, c, 0, 0)),
        out_shape=jax.ShapeDtypeStruct((batch, DIL_CLASSES, t, DIL_FAR_LANES), F32),
        compiler_params=_params(2),
        name="dilated_far",
    )(qkv_cls, qkv_cls, qkv_cls, bias)


def _dilated_near_kernel(q_ref, k_ref, vt_ref, bias_ref, far_ref, o_ref, acc_ref):
    t = ATT_TILE
    qi = pl.program_id(1)
    qh = _split_heads(q_ref[0])

    def tiles(n_tiles):
        first_tile = qi - (n_tiles - 1)
        keys = k_ref[0, pl.ds(pl.multiple_of(first_tile * t, t), n_tiles * t), :]
        stats = ()
        scores = _pair_scores(keys, qh)
        for h in range(N_HEADS_B):
            s_tiles = [s + bias_ref[h, n_tiles - 1 - c] for c, s in enumerate(scores[h])]
            rows = slice((h % 2) * HEAD_DIM, (h % 2 + 1) * HEAD_DIM)
            vt_h = jnp.concatenate([vt_ref[0, h // 2, first_tile + c, rows, :] for c in range(n_tiles)], axis=1)
            stats += _softmax_group(s_tiles, [None] * n_tiles, vt_h, acc_ref, h, None, True)
        return stats

    stats = lax.switch(jnp.minimum(qi, DIL_NEAR_TILES - 1),
                       [functools.partial(tiles, n + 1) for n in range(DIL_NEAR_TILES)])
    far = far_ref[0].T
    far_m0, far_l0 = N_HEADS_B * HEAD_DIM, N_HEADS_B * HEAD_DIM + N_HEADS_B
    outs = []
    for h in range(N_HEADS_B):
        m_near, l_near = stats[2 * h], stats[2 * h + 1]
        m_far, l_far = far[far_m0 + h:far_m0 + h + 1], far[far_l0 + h:far_l0 + h + 1]
        m = jnp.maximum(m_near, m_far)
        w_near, w_far = jnp.exp2(m_near - m), jnp.exp2(m_far - m)
        acc_far = far[h * HEAD_DIM:(h + 1) * HEAD_DIM]
        outs.append((w_near * acc_ref[h] + w_far * acc_far) / (w_near * l_near + w_far * l_far))
    _store_step_output(o_ref, outs)


def _dilated_near(proj, vt, bias, far, batch, seq, lane0):
    t = ATT_TILE
    wb = N_HEADS_B * HEAD_DIM
    first = lane0 // wb
    return pl.pallas_call(
        _dilated_near_kernel,
        grid=(batch, seq // t),
        in_specs=[pl.BlockSpec((1, t, wb), lambda b, i: (b, i, first)),
                  pl.BlockSpec((1, seq, wb), lambda b, i: (b, 0, first + 1)),
                  pl.BlockSpec((1,) + vt.shape[1:], lambda b, i: (b, 0, 0, 0, 0)),
                  pl.BlockSpec((N_HEADS_B, DIL_NEAR_TILES, t, t), lambda b, i: (0, 0, 0, 0)),
                  pl.BlockSpec((1, t, DIL_FAR_LANES), lambda b, i: (b, i, 0))],
        out_specs=pl.BlockSpec((1, t, wb), lambda b, i: (b, i, 0)),
        out_shape=jax.ShapeDtypeStruct((batch, seq, wb), BF16),
        scratch_shapes=[pltpu.VMEM((N_HEADS_B, HEAD_DIM, t), F32)],
        compiler_params=_params(2),
        name="dilated_near",
    )(proj, proj, vt, bias, far)


def _dilated_attn(proj, qkv_cls, vt_tiles, bias_near, bias_far, batch, seq, lane0):
    far = _dilated_far(qkv_cls, bias_far, batch)
    far = far.transpose(0, 2, 1, 3).reshape(batch, seq, DIL_FAR_LANES)
    return _dilated_near(proj, vt_tiles, bias_near, far, batch, seq, lane0)


def _odd_proj_kernel(x_ref, mod_ref, g_ref, win_ref, gq_ref, gkv_ref, wq_ref, wkv_ref,
                     cq_ref, sq_ref, ck_ref, sk_ref, qm_ref, km_ref, vtm_ref, sb_ref, vts_ref):
    d = D_MODEL
    mod = mod_ref[0]
    h = _prenorm(x_ref[...], g_ref[...], mod[:, d:2 * d], mod[:, 0:d])
    p = _dot(h.astype(BF16), win_ref[...])
    o = MLA_Q_RANK + MLA_KV_RANK
    c_q, c_kv = p[:, 0:MLA_Q_RANK], p[:, MLA_Q_RANK:o]
    k_rope, k_rope_swapped = p[:, o:o + LANES], p[:, o + LANES:o + 2 * LANES]
    sb0 = o + 2 * LANES
    n_qk = 2 * N_HEADS_D * HEAD_DIM
    sb_ref[...] = p[:, sb0:sb0 + n_qk].astype(BF16)
    _store_vt(vts_ref, p[:, sb0 + n_qk:])
    q12 = _dot(_rms(c_q, gq_ref[...]).astype(BF16), wq_ref[...])
    kv = _dot(_rms(c_kv, gkv_ref[...]).astype(BF16), wkv_ref[...])
    k_pe = k_rope * ck_ref[...] + k_rope_swapped * sk_ref[...]
    cq, sq = cq_ref[...], sq_ref[...]
    half = N_HEADS_C * LANES
    for hh in range(N_HEADS_C):
        cols = slice(hh * LANES, (hh + 1) * LANES)
        swapped = slice(half + hh * LANES, half + (hh + 1) * LANES)
        qm_ref[:, cols] = (q12[:, cols] * cq + q12[:, swapped] * sq).astype(BF16)
        km_ref[:, cols] = (kv[:, cols] + k_pe).astype(BF16)
    _store_vt(vtm_ref, kv[:, half:])


def _odd_proj(x, mod, g, w_in, gq, gkv, wq, wkv, tables, seq):
    rows, d = x.shape
    tm = ROW_TILE
    per_seq = seq // tm
    n_in = w_in.shape[1]
    n_sb = 2 * N_HEADS_D * HEAD_DIM
    n_q = N_HEADS_C * LANES
    n_pairs = N_HEADS_C // 2
    const = lambda i: (0, 0)
    table_spec = pl.BlockSpec((tm, LANES), lambda i: (i % per_seq, 0))
    vt_spec = _vt_spec(tm, per_seq, n_pairs, KEY_GROUP_ODD)
    vt_shape = _vt_shape(rows // seq, seq, n_pairs, KEY_GROUP_ODD)
    return pl.pallas_call(
        _odd_proj_kernel,
        grid=(rows // tm,),
        in_specs=[pl.BlockSpec((tm, d), lambda i: (i, 0)),
                  pl.BlockSpec((1, 1, 6 * d), lambda i: (i // per_seq, 0, 0)),
                  pl.BlockSpec((1, d), const),
                  pl.BlockSpec((d, n_in), const),
                  pl.BlockSpec((1, MLA_Q_RANK), const),
                  pl.BlockSpec((1, MLA_KV_RANK), const),
                  pl.BlockSpec(wq.shape, const),
                  pl.BlockSpec(wkv.shape, const),
                  table_spec, table_spec, table_spec, table_spec],
        out_specs=[pl.BlockSpec((tm, n_q), lambda i: (i, 0)),
                   pl.BlockSpec((tm, n_q), lambda i: (i, 0)),
                   vt_spec,
                   pl.BlockSpec((tm, n_sb), lambda i: (i, 0)),
                   vt_spec],
        out_shape=[jax.ShapeDtypeStruct((rows, n_q), BF16),
                   jax.ShapeDtypeStruct((rows, n_q), BF16),
                   vt_shape,
                   jax.ShapeDtypeStruct((rows, n_sb), BF16),
                   vt_shape],
        compiler_params=_params(1),
        name="odd_proj",
    )(x, mod, g, w_in, gq, gkv, wq, wkv, *tables)


def _mla_attn_kernel(q_ref, k_ref, vt_ref, o_ref, acc_ref):
    qi = pl.program_id(2)
    q_step = q_ref[0]
    qh = [q_step[:, h * LANES:(h + 1) * LANES] for h in range(HEADS_PER_STEP)]

    def group(g, stats, n_tiles, first):
        kg = _group_keys(k_ref, g, n_tiles, KEY_GROUP_ODD)
        new = ()
        scores = [_tile_scores(kg[:, h * LANES:(h + 1) * LANES], qh[h]) for h in range(HEADS_PER_STEP)]
        for h in range(HEADS_PER_STEP):
            s_tiles = scores[h]
            if first:
                key, query = _tile_iotas()
                s_tiles[-1] = jnp.where(key <= query, s_tiles[-1], -jnp.inf)
            new += _softmax_group(s_tiles, [None] * n_tiles, _head_rows(vt_ref, g, h, n_tiles), acc_ref, h,
                                  None if first else stats[2 * h:2 * h + 2], first)
        return new

    stats = _sweep(qi, group, KEY_GROUP_ODD)
    _store_step_output(o_ref, [acc_ref[h] / stats[2 * h + 1] for h in range(HEADS_PER_STEP)])


def _mla_attn(qm, km, vt, batch, seq):
    t = ATT_TILE
    n_steps = N_HEADS_C // HEADS_PER_STEP
    qk_lanes = HEADS_PER_STEP * LANES
    return pl.pallas_call(
        _mla_attn_kernel,
        grid=(batch, n_steps, seq // t),
        in_specs=[pl.BlockSpec((1, t, qk_lanes), lambda b, p, i: (b, i, p)),
                  pl.BlockSpec((1, seq, qk_lanes), lambda b, p, i: (b, 0, p)),
                  pl.BlockSpec((1, PAIRS_PER_STEP) + vt.shape[2:], lambda b, p, i: (b, p, 0, 0, 0))],
        out_specs=pl.BlockSpec((1, t, STEP_LANES), lambda b, p, i: (b, i, p)),
        out_shape=jax.ShapeDtypeStruct((batch, seq, N_HEADS_C * MLA_V_DIM), BF16),
        scratch_shapes=[pltpu.VMEM((HEADS_PER_STEP, HEAD_DIM, t), F32)],
        compiler_params=_params(3),
        name="mla_attn",
    )(qm, km, vt)


def _stick_attn_kernel(q_ref, k_ref, vt_ref, o_ref, acc_ref):
    t = ATT_TILE
    qi = pl.program_id(2)
    qh = _split_heads(q_ref[0])
    key = lax.broadcasted_iota(jnp.int32, (t, t), 0)
    other = lax.broadcasted_iota(jnp.int32, (t, t), 1)
    from_here = jnp.where(other >= key, 1.0, 0.0).astype(BF16)
    from_here2 = jnp.concatenate([from_here, from_here], axis=1)

    strict = key < other

    def group(g, carries, n_tiles, first):
        kg = _group_keys(k_ref, g, n_tiles, KEY_GROUP_ODD)
        new = ()
        scores = _pair_scores(kg, qh)
        for h in range(HEADS_PER_STEP):
            z_tiles = scores[h]
            drop_tiles = [jnp.maximum(z, 0.0) + jnp.log(1.0 + jnp.exp2(jnp.abs(z) * -LOG2_E)) for z in z_tiles]
            if first:
                drop_tiles[-1] = jnp.where(strict, drop_tiles[-1], 0.0)
            run = None if first else carries[h]
            a_tiles = [None] * n_tiles
            for c in reversed(range(n_tiles)):
                tail = _dot(from_here2, jnp.concatenate(_split_bf16(drop_tiles[c]), axis=0))
                tile_total = tail[0:1, :]
                if run is not None:
                    tail = tail + run
                a_tiles[c] = jnp.exp(z_tiles[c] - tail)
                run = tile_total if run is None else run + tile_total
            if first:
                a_tiles[-1] = jnp.where(strict, a_tiles[-1], 0.0)
            pv = _dot(_head_rows(vt_ref, g, h, n_tiles), jnp.concatenate(a_tiles, axis=0).astype(BF16))
            acc_ref[h] = pv if first else acc_ref[h] + pv
            new += (run,)
        return new

    _sweep(qi, group, KEY_GROUP_ODD)
    _store_step_output(o_ref, [acc_ref[h] for h in range(HEADS_PER_STEP)])


def _stick_attn(sb, vt, batch, seq):
    t = ATT_TILE
    n_steps = N_HEADS_D // HEADS_PER_STEP
    return pl.pallas_call(
        _stick_attn_kernel,
        grid=(batch, n_steps, seq // t),
        in_specs=[pl.BlockSpec((1, t, STEP_LANES), lambda b, p, i: (b, i, p)),
                  pl.BlockSpec((1, seq, STEP_LANES), lambda b, p, i: (b, 0, n_steps + p)),
                  pl.BlockSpec((1, PAIRS_PER_STEP) + vt.shape[2:], lambda b, p, i: (b, p, 0, 0, 0))],
        out_specs=pl.BlockSpec((1, t, STEP_LANES), lambda b, p, i: (b, i, p)),
        out_shape=jax.ShapeDtypeStruct((batch, seq, N_HEADS_D * HEAD_DIM), BF16),
        scratch_shapes=[pltpu.VMEM((HEADS_PER_STEP, HEAD_DIM, t), F32)],
        compiler_params=_params(3),
        name="stick_attn",
    )(sb, sb, vt)


def _post_attn_kernel(oa_ref, ob_ref, oah_ref, obh_ref, x_ref, xh_ref, mod_ref, gmix_ref, gpre_ref, gpost_ref,
                      wo_ref, wup_ref, cw_ref, cb_ref, wd_ref, o_ref, x1_ref, h_ref, u_ref, acc_ref, *, per_seq):
    d = D_MODEL
    tm = ROW_TILE
    halo = CONV_HALO
    cw = FF_CHUNK
    i = pl.program_id(0)
    mod = mod_ref[0]
    gate_m, shift, scale, gate_f = mod[:, 2 * d:3 * d], mod[:, 3 * d:4 * d], mod[:, 4 * d:5 * d], mod[:, 5 * d:6 * d]
    n_first = oa_ref.shape[1]

    def mixed(oa, ob, x):
        y = _dot(oa, wo_ref[0:n_first, :]) + _dot(ob, wo_ref[n_first:2 * n_first, :])
        return x + gate_m * _rms(y, gmix_ref[...])

    g = gpre_ref[...]
    ahead = _prenorm(mixed(oah_ref[...], obh_ref[...], xh_ref[...]), g, scale, shift)
    ahead = jnp.where(i % per_seq == 0, 0.0, ahead)
    h_ref[0:halo, :] = ahead.astype(BF16)
    for rows in (slice(0, tm // 2), slice(tm // 2, tm)):
        x1_ref[rows, :] = mixed(oa_ref[rows, :], ob_ref[rows, :], x_ref[rows, :])
        h_ref[halo + rows.start:halo + rows.stop, :] = _prenorm(x1_ref[rows, :], g, scale, shift).astype(BF16)

    def up(ch):
        slot = ch % 2
        for half in range(2):
            cols = slice(half * D_FF + ch * cw, half * D_FF + (ch + 1) * cw)
            u_ref[slot, half] = _dot(h_ref[...], wup_ref[:, cols])

    def conv(slot, half, ch):
        cols = slice(half * D_FF + ch * cw, half * D_FF + (ch + 1) * cw)
        w = cw_ref[:, cols]
        out = None
        for tap in range(CONV_WIDTH):
            start = halo - (CONV_WIDTH - 1 - tap)
            term = w[tap:tap + 1, :] * u_ref[slot, half, start:start + tm, :]
            out = term if out is None else out + term
        return out + cb_ref[:, cols]

    def down(ch, act):
        part = _dot(act, wd_ref[ch * cw:(ch + 1) * cw, :])
        if ch == 0:
            acc_ref[...] = part
        else:
            acc_ref[...] += part

    up(0)
    act = None
    for ch in range(N_FF_CHUNKS):
        if ch + 1 < N_FF_CHUNKS:
            up(ch + 1)
        if act is not None:
            down(ch - 1, act)
        slot = ch % 2
        act = (jax.nn.gelu(conv(slot, 0, ch), approximate=True) * conv(slot, 1, ch)).astype(BF16)
    down(N_FF_CHUNKS - 1, act)

    o_ref[...] = x1_ref[...] + gate_f * _rms(acc_ref[...], gpost_ref[...])


def _post_attn(oa, ob, x, mod, g_mix, g_pre, g_post, w_out, w_up, conv_w, conv_b, w_down, seq):
    rows, d = x.shape
    tm = ROW_TILE
    halo = CONV_HALO
    per_seq = seq // tm
    n_first = oa.shape[1]
    const = lambda i: (0, 0)
    tile = lambda i: (i, 0)
    ahead = lambda i: (jnp.maximum(i * (tm // halo) - 1, 0), 0)
    once = dict(pipeline_mode=pl.Buffered(1))
    return pl.pallas_call(
        functools.partial(_post_attn_kernel, per_seq=per_seq),
        grid=(rows // tm,),
        in_specs=[pl.BlockSpec((tm, n_first), tile), pl.BlockSpec((tm, n_first), tile),
                  pl.BlockSpec((halo, n_first), ahead), pl.BlockSpec((halo, n_first), ahead),
                  pl.BlockSpec((tm, d), tile), pl.BlockSpec((halo, d), ahead),
                  pl.BlockSpec((1, 1, 6 * d), lambda i: (i // per_seq, 0, 0)),
                  pl.BlockSpec((1, d), const), pl.BlockSpec((1, d), const), pl.BlockSpec((1, d), const),
                  pl.BlockSpec(w_out.shape, const, **once),
                  pl.BlockSpec(w_up.shape, const, **once),
                  pl.BlockSpec(conv_w.shape, const),
                  pl.BlockSpec(conv_b.shape, const),
                  pl.BlockSpec(w_down.shape, const, **once)],
        out_specs=pl.BlockSpec((tm, d), tile),
        out_shape=jax.ShapeDtypeStruct((rows, d), F32),
        scratch_shapes=[pltpu.VMEM((tm, d), F32),
                        pltpu.VMEM((tm + halo, d), BF16),
                        pltpu.VMEM((2, 2, tm + halo, FF_CHUNK), F32),
                        pltpu.VMEM((tm, d), F32)],
        compiler_params=_params(1),
        name="post_attn",
    )(oa, ob, oa, ob, x, x, mod, g_mix, g_pre, g_post, w_out, w_up, conv_w, conv_b, w_down)


def _rotate_half_cols(w):
    half = w.shape[-1] // 2
    return jnp.concatenate([-w[..., half:], w[..., :half]], axis=-1)


def _pad_cols(w, left, total):
    return jnp.pad(w, ((0, 0), (left, total - left - w.shape[1])))


def _rope_tables(seq):
    inv_freq = 1.0 / (ROPE_THETA ** (jnp.arange(0, MLA_ROPE_DIM, 2, dtype=F32) / MLA_ROPE_DIM))
    ang = jnp.arange(seq, dtype=F32)[:, None] * inv_freq[None, :]
    cos, sin = jnp.cos(ang), jnp.sin(ang)
    cos2 = _pad_cols(jnp.concatenate([cos, cos], axis=1), MLA_NOPE_DIM, LANES)
    sin2 = _pad_cols(jnp.concatenate([sin, sin], axis=1), MLA_NOPE_DIM, LANES)
    scale = (MLA_NOPE_DIM + MLA_ROPE_DIM) ** -0.5
    nope_ones = _pad_cols(jnp.ones((seq, MLA_NOPE_DIM), F32), 0, LANES)
    scale = scale * LOG2_E
    return (scale * (cos2 + nope_ones), scale * sin2, cos2, sin2)


def _odd_weights(w_in, w_uq, w_ukv):
    o = MLA_Q_RANK + MLA_KV_RANK
    w_rope = w_in[:, o:o + MLA_ROPE_DIM]
    scale_d = HEAD_DIM ** -0.5
    wd = N_HEADS_D * HEAD_DIM
    sb0 = o + MLA_ROPE_DIM
    w_in2 = jnp.concatenate([
        w_in[:, :o],
        _pad_cols(w_rope, MLA_NOPE_DIM, LANES),
        _pad_cols(_rotate_half_cols(w_rope), MLA_NOPE_DIM, LANES),
        w_in[:, sb0:sb0 + wd] * scale_d,
        w_in[:, sb0 + wd:],
    ], axis=1).astype(BF16)
    qd = MLA_NOPE_DIM + MLA_ROPE_DIM
    uq = w_uq.reshape(MLA_Q_RANK, N_HEADS_C, qd)
    plain = jnp.pad(uq, ((0, 0), (0, 0), (0, LANES - qd)))
    swapped = jnp.pad(_rotate_half_cols(uq[..., MLA_NOPE_DIM:]),
                      ((0, 0), (0, 0), (MLA_NOPE_DIM, LANES - qd)))
    wq = jnp.concatenate([plain.reshape(MLA_Q_RANK, -1), swapped.reshape(MLA_Q_RANK, -1)], axis=1).astype(BF16)
    ukv = w_ukv.reshape(MLA_KV_RANK, N_HEADS_C, MLA_NOPE_DIM + MLA_V_DIM)
    k_nope = jnp.pad(ukv[..., :MLA_NOPE_DIM], ((0, 0), (0, 0), (0, LANES - MLA_NOPE_DIM)))
    v = ukv[..., MLA_NOPE_DIM:]
    wkv = jnp.concatenate([k_nope.reshape(MLA_KV_RANK, -1), v.reshape(MLA_KV_RANK, -1)], axis=1).astype(BF16)
    return w_in2, wq, wkv


def _even_weights(w_in):
    wa = N_HEADS_A * HEAD_DIM
    scale = HEAD_DIM ** -0.5 * LOG2_E
    col = jnp.arange(w_in.shape[1])
    is_q = (col < wa) | ((col >= 3 * wa) & (col < 4 * wa))
    return (w_in * jnp.where(is_q, scale, 1.0)).astype(BF16)


def kernel(x, c, rel_bias, ada_w, ada_b, mix_pre_g, mix_post_g, ffn_pre_g, ffn_post_g, ab_w_in, ab_w_out,
           cd_w_in, mla_q_norm_g, mla_kv_norm_g, mla_w_uq, mla_w_ukv, cd_w_out, ffn_w_up, ffn_conv_w,
           ffn_conv_b, ffn_w_down):
    batch, seq, d = x.shape
    widest_group = max(KEY_GROUP_EVEN, KEY_GROUP_ODD) * ATT_TILE
    assert d == D_MODEL and seq % widest_group == 0 and (KEY_GROUP_EVEN * ATT_TILE) % ROW_TILE == 0
    assert seq == DIL_CLASSES * ATT_TILE
    rows = batch * seq
    xf = x.reshape(rows, d)

    mods = _mods(c, ada_w, ada_b)
    bias_a = _bias_tiles(rel_bias, 0, N_HEADS_A, MOBA_BIAS_TILES, 1, None, "moba_bias_tiles")
    bias_near = _bias_tiles(rel_bias, N_HEADS_A, N_HEADS_B, DIL_NEAR_TILES, 1, DIL_NEAR_BRANCHES,
                            "dilated_near_bias_tiles")
    bias_far = _bias_tiles(rel_bias, N_HEADS_A, N_HEADS_B, 1, DIL_CLASSES, DIL_FAR_BRANCHES,
                           "dilated_far_bias_tiles")
    tables = _rope_tables(seq)

    for layer in range(DEPTH):
        mod = mods[layer].reshape(batch, 1, 6 * d)
        i = layer // 2
        if layer % 2 == 0:
            proj, km, vta, vtb, qkv_cls = _even_proj(xf, mod, mix_pre_g[layer].reshape(1, d),
                                                     _even_weights(ab_w_in[i]), seq)
            proj = proj.reshape(batch, seq, -1)
            km = km.reshape(batch, seq // MOBA_BLOCK, -1)
            o_first = _moba_attn(proj, vta, bias_a, km, batch, seq)
            o_second = _dilated_attn(proj, qkv_cls, vtb, bias_near, bias_far, batch, seq,
                                     3 * N_HEADS_A * HEAD_DIM)
            w_out = ab_w_out[i]
        else:
            w_in2, wq, wkv = _odd_weights(cd_w_in[i], mla_w_uq[i], mla_w_ukv[i])
            qm, km, vtm, sb, vts = _odd_proj(xf, mod, mix_pre_g[layer].reshape(1, d), w_in2,
                                             mla_q_norm_g[i].reshape(1, -1), mla_kv_norm_g[i].reshape(1, -1),
                                             wq, wkv, tables, seq)
            o_first = _mla_attn(qm.reshape(batch, seq, -1), km.reshape(batch, seq, -1), vtm, batch, seq)
            o_second = _stick_attn(sb.reshape(batch, seq, -1), vts, batch, seq)
            w_out = cd_w_out[i]
        xf = _post_attn(o_first.reshape(rows, -1), o_second.reshape(rows, -1), xf, mod,
                        mix_post_g[layer].reshape(1, d), ffn_pre_g[layer].reshape(1, d),
                        ffn_post_g[layer].reshape(1, d), w_out.astype(BF16), ffn_w_up[layer].astype(BF16),
                        ffn_conv_w[layer], ffn_conv_b[layer].reshape(1, -1), ffn_w_down[layer].astype(BF16), seq)
    return xf.reshape(batch, seq, d)
```

```python
import functools
import math

import jax
import jax.numpy as jnp
from jax import lax
from jax.experimental import pallas as pl
from jax.experimental.pallas import tpu as pltpu

F32 = jnp.float32
BF16 = jnp.bfloat16

D_MODEL = 1024
DEPTH = 4
HEAD_DIM = 64
N_HEADS_A = 8
N_HEADS_B = 8
N_HEADS_C = 8
N_HEADS_D = 8
MOBA_BLOCK = 256
MOBA_TOPK = 3
DILATED_BRANCHES = ((128, 1), (512, 4), (2048, 16))
MLA_Q_RANK = 256
MLA_KV_RANK = 256
MLA_NOPE_DIM = 64
MLA_ROPE_DIM = 32
MLA_V_DIM = 64
ROPE_THETA = 10000.0
REL_BUCKETS = 32
REL_MAX_DIST = 2048
D_FF = 2816
CONV_WIDTH = 3
NORM_EPS = 1e-6
LOG2_E = math.log2(math.e)

LANES = 128
SUBLANES = 8
BF16_SUBLANES = 16
VMEM_LIMIT_BYTES = 56 * 1024 * 1024

ATT_TILE = MOBA_BLOCK
KEY_GROUP_EVEN = 8
KEY_GROUP_ODD = 8
PAIRS_PER_STEP = 2
HEADS_PER_STEP = 2 * PAIRS_PER_STEP
STEP_LANES = PAIRS_PER_STEP * LANES
ROW_TILE = 512
FF_CHUNK = 256
N_FF_CHUNKS = D_FF // FF_CHUNK
ACT_ROW_BLOCKS = 2
CONV_HALO = BF16_SUBLANES
MOBA_BIAS_TILES = REL_MAX_DIST // ATT_TILE + 2
DIL_SPLIT = DILATED_BRANCHES[1][0]
DIL_CLASSES = DILATED_BRANCHES[2][1]
DIL_NEAR_TILES = DIL_SPLIT // ATT_TILE + 1
DIL_FAR_LANES = 5 * LANES
assert DIL_FAR_LANES >= N_HEADS_B * (HEAD_DIM + 2)
DIL_NEAR_BRANCHES = tuple((-1, min(window, DIL_SPLIT), dil) for window, dil in DILATED_BRANCHES)
DIL_FAR_BRANCHES = ((DIL_SPLIT, DILATED_BRANCHES[2][0], DIL_CLASSES),)
assert all(window <= DIL_SPLIT for window, _ in DILATED_BRANCHES[:2]) and DIL_SPLIT % DIL_CLASSES == 0

_NT = (((1,), (1,)), ((), ()))


def _bucket_lower_bounds():
    max_exact = REL_BUCKETS // 2
    ratio = REL_MAX_DIST // max_exact
    n_log = REL_BUCKETS - max_exact
    lows = list(range(max_exact + 1))
    for k in range(1, n_log):
        d = lows[-1]
        while d ** n_log < (max_exact ** n_log) * (ratio ** k):
            d += 1
        lows.append(d)
    return lows


_BUCKET_LOW = _bucket_lower_bounds()


def _dot(a, b):
    return jnp.dot(a, b, preferred_element_type=F32)


def _dot_nt(a, b):
    return lax.dot_general(a, b, _NT, preferred_element_type=F32)


def _split_bf16(x):
    hi = x.astype(BF16)
    lo = (x - hi.astype(F32)).astype(BF16)
    return hi, lo


def _rms(x, g):
    return (x * lax.rsqrt(jnp.mean(x * x, axis=-1, keepdims=True) + NORM_EPS)) * g


def _prenorm(x, g, scale, shift):
    return _rms(x, g) * (1.0 + scale) + shift


def _params(n_grid_dims):
    return pltpu.CompilerParams(dimension_semantics=("arbitrary",) * n_grid_dims,
                                vmem_limit_bytes=VMEM_LIMIT_BYTES)


def _mods_kernel(c_ref, w_ref, b_ref, o_ref):
    c = c_ref[...]
    cond = c * jax.nn.sigmoid(c)
    c_hi, c_lo = _split_bf16(cond)
    w_hi, w_lo = _split_bf16(w_ref[0])
    o_ref[0] = _dot(c_hi, w_hi) + _dot(c_hi, w_lo) + _dot(c_lo, w_hi) + b_ref[0]


def _mods(c, ada_w, ada_b):
    b, d = c.shape
    rows = BF16_SUBLANES
    n_out = ada_w.shape[-1]
    tn = n_out // 4
    c_pad = jnp.zeros((rows, d), F32).at[:b].set(c)
    out = pl.pallas_call(
        _mods_kernel,
        grid=(DEPTH, n_out // tn),
        in_specs=[pl.BlockSpec((rows, d), lambda l, j: (0, 0)),
                  pl.BlockSpec((1, d, tn), lambda l, j: (l, 0, j)),
                  pl.BlockSpec((1, 1, tn), lambda l, j: (l, 0, j))],
        out_specs=pl.BlockSpec((1, rows, tn), lambda l, j: (l, 0, j)),
        out_shape=jax.ShapeDtypeStruct((DEPTH, rows, n_out), F32),
        compiler_params=_params(2),
        name="ada_mods",
    )(c_pad, ada_w, ada_b.reshape(DEPTH, 1, n_out))
    return out[:, :b]


def _bias_tiles_kernel(tab_ref, o_ref, *, head_off, step, branches):
    h = pl.program_id(0) + head_off
    t = ATT_TILE
    offsets = lax.broadcasted_iota(jnp.int32, (SUBLANES, 2 * t), 1) - t
    for d in range(o_ref.shape[1]):
        dist = (d * t + offsets) * step
        val = jnp.full(dist.shape, tab_ref[h, 0], F32)
        for b in range(1, REL_BUCKETS):
            val = jnp.where(dist >= _BUCKET_LOW[b], tab_ref[h, b], val)
        if branches is not None:
            mult = jnp.zeros(dist.shape, F32)
            for beyond, window, dil in branches:
                hit = jnp.where(dist <= window, jnp.where((dist & (dil - 1)) == 0, 1.0, 0.0), 0.0)
                mult = mult + jnp.where(dist > beyond, hit, 0.0)
            log_mult = jnp.where(mult > 2.5, math.log(3.0), jnp.where(mult > 1.5, math.log(2.0), 0.0))
            val = jnp.where(mult > 0.5, val + log_mult, -jnp.inf)
        val = jnp.where(dist >= 0, val * LOG2_E, -jnp.inf)
        strip = jnp.concatenate([val] * (t // SUBLANES), axis=0)
        rotated = pltpu.roll(strip, 0, 1, stride=1, stride_axis=0)
        o_ref[0, d] = rotated[:, t:2 * t]


def _bias_tiles(rel_bias, head_off, n_heads, n_tiles, step, branches, name):
    t = ATT_TILE
    return pl.pallas_call(
        functools.partial(_bias_tiles_kernel, head_off=head_off, step=step, branches=branches),
        grid=(n_heads,),
        in_specs=[pl.BlockSpec(memory_space=pltpu.SMEM)],
        out_specs=pl.BlockSpec((1, n_tiles, t, t), lambda h: (h, 0, 0, 0)),
        out_shape=jax.ShapeDtypeStruct((n_heads, n_tiles, t, t), F32),
        compiler_params=_params(1),
        name=name,
    )(rel_bias)


def _store_vt(vt_ref, v):
    vt = v.T.astype(BF16)
    for p in range(vt.shape[0] // LANES):
        vt_ref[0, p, 0] = vt[p * LANES:(p + 1) * LANES, :]


def _vt_spec(tm, per_seq, n_pairs, key_group):
    per_group = key_group * ATT_TILE // tm
    return pl.BlockSpec((1, n_pairs, 1, LANES, tm),
                        lambda i: (i // per_seq, 0, (i % per_seq) // per_group, 0, (i % per_seq) % per_group))


def _vt_shape(batch, seq, n_pairs, key_group):
    group_keys = key_group * ATT_TILE
    return jax.ShapeDtypeStruct((batch, n_pairs, seq // group_keys, LANES, group_keys), BF16)


def _even_proj_kernel(x_ref, mod_ref, g_ref, w_ref, o_ref, km_ref, vta_ref, vtb_ref, cls_ref):
    d = D_MODEL
    mod = mod_ref[0]
    h = _prenorm(x_ref[...], g_ref[...], mod[:, d:2 * d], mod[:, 0:d])
    p = _dot(h.astype(BF16), w_ref[...])
    p_bf16 = p.astype(BF16)
    o_ref[...] = p_bf16
    tm = p.shape[0]
    per_class = tm // DIL_CLASSES
    out_row = lax.broadcasted_iota(jnp.int32, (tm, tm), 0)
    in_row = lax.broadcasted_iota(jnp.int32, (tm, tm), 1)
    source = (out_row % per_class) * DIL_CLASSES + out_row // per_class
    permute = jnp.where(in_row == source, 1.0, 0.0).astype(BF16)
    by_class = _dot(permute, p_bf16[:, 3 * N_HEADS_A * HEAD_DIM:]).astype(BF16)
    for cls in range(DIL_CLASSES):
        cls_ref[0, cls] = by_class[cls * per_class:(cls + 1) * per_class]
    wa = N_HEADS_A * HEAD_DIM
    ka = p[:, wa:2 * wa]
    nb = ka.shape[0] // MOBA_BLOCK
    km_ref[0] = jnp.mean(ka.reshape(nb, MOBA_BLOCK, wa), axis=1)
    _store_vt(vta_ref, p[:, 2 * wa:3 * wa])
    vtb = p[:, 5 * wa:6 * wa].T.astype(BF16)
    t = ATT_TILE
    for pair in range(vtb.shape[0] // LANES):
        for tile in range(vtb.shape[1] // t):
            vtb_ref[0, pair, tile] = vtb[pair * LANES:(pair + 1) * LANES, tile * t:(tile + 1) * t]


def _even_proj(x, mod, g, w_in, seq):
    rows, d = x.shape
    tm = ROW_TILE
    n = w_in.shape[1]
    wa = N_HEADS_A * HEAD_DIM
    per_seq = seq // tm
    n_pairs = N_HEADS_A // 2
    vt_spec = _vt_spec(tm, per_seq, n_pairs, KEY_GROUP_EVEN)
    vt_shape = _vt_shape(rows // seq, seq, n_pairs, KEY_GROUP_EVEN)
    tiles_per_step = tm // ATT_TILE
    vt_tile_spec = pl.BlockSpec((1, n_pairs, tiles_per_step, LANES, ATT_TILE),
                                lambda i: (i // per_seq, 0, i % per_seq, 0, 0))
    return pl.pallas_call(
        _even_proj_kernel,
        grid=(rows // tm,),
        in_specs=[pl.BlockSpec((tm, d), lambda i: (i, 0)),
                  pl.BlockSpec((1, 1, 6 * d), lambda i: (i // per_seq, 0, 0)),
                  pl.BlockSpec((1, d), lambda i: (0, 0)),
                  pl.BlockSpec((d, n), lambda i: (0, 0))],
        out_specs=[pl.BlockSpec((tm, n), lambda i: (i, 0)),
                   pl.BlockSpec((1, tm // MOBA_BLOCK, wa), lambda i: (i, 0, 0)),
                   vt_spec, vt_tile_spec,
                   pl.BlockSpec((1, DIL_CLASSES, tm // DIL_CLASSES, n - 3 * wa),
                                lambda i: (i // per_seq, 0, i % per_seq, 0))],
        out_shape=[jax.ShapeDtypeStruct((rows, n), BF16),
                   jax.ShapeDtypeStruct((rows // tm, tm // MOBA_BLOCK, wa), F32),
                   vt_shape, _vt_shape(rows // seq, seq, n_pairs, 1),
                   jax.ShapeDtypeStruct((rows // seq, DIL_CLASSES, seq // DIL_CLASSES, n - 3 * wa), BF16)],
        compiler_params=_params(1),
        name="even_proj",
    )(x, mod, g, w_in)


def _pair_lanes(x, h):
    pair = h // 2
    return x[:, pair * LANES:(pair + 1) * LANES]


def _split_heads(q_step):
    first = lax.broadcasted_iota(jnp.int32, (q_step.shape[0], LANES), 1) < HEAD_DIM
    heads = []
    for pair in range(q_step.shape[1] // LANES):
        q2 = q_step[:, pair * LANES:(pair + 1) * LANES]
        zero = jnp.zeros_like(q2)
        heads += [jnp.where(first, q2, zero), jnp.where(first, zero, q2)]
    return heads


def _head_rows(vt_ref, g, h, n_tiles):
    return vt_ref[0, h // 2, g, (h % 2) * HEAD_DIM:(h % 2 + 1) * HEAD_DIM, 0:n_tiles * ATT_TILE]


def _pair_scores(keys, qh):
    t = ATT_TILE
    n = keys.shape[0] // t
    scores = []
    for pair in range(len(qh) // 2):
        s = _dot_nt(keys[:, pair * LANES:(pair + 1) * LANES], jnp.concatenate([qh[2 * pair], qh[2 * pair + 1]], axis=0))
        for j in range(2):
            scores.append([s[c * t:(c + 1) * t, j * t:(j + 1) * t] for c in range(n)])
    return scores


def _tile_scores(keys, q_head):
    t = ATT_TILE
    return [_dot_nt(keys[c * t:(c + 1) * t], q_head) for c in range(keys.shape[0] // t)]


def _group_keys(k_ref, g, n_tiles, key_group):
    start = pl.multiple_of(g * key_group * ATT_TILE, key_group * ATT_TILE)
    return k_ref[0, pl.ds(start, n_tiles * ATT_TILE), :]


def _sweep(qi, group, key_group, n_far_groups=None):
    g_own = qi // key_group
    own = [functools.partial(group, g_own, None, n + 1, True) for n in range(key_group)]
    state = lax.switch(qi % key_group, own)
    n_past = g_own if n_far_groups is None else jnp.minimum(g_own, n_far_groups)
    return lax.fori_loop(0, n_past, lambda n, st: group(g_own - 1 - n, st, key_group, False), state)


def _tile_iotas():
    t = ATT_TILE
    return lax.broadcasted_iota(jnp.int32, (t, t), 0), lax.broadcasted_iota(jnp.int32, (t, t), 1)


def _store_step_output(o_ref, outs_t):
    o_ref[0] = jnp.concatenate(outs_t, axis=0).T.astype(BF16)


def _softmax_group(s_tiles, ons, vt_h, acc_ref, h, stats, first):
    maxes = []
    for s, on in zip(s_tiles, ons):
        mx = jnp.max(s, axis=0, keepdims=True)
        maxes.append(mx if on is None else jnp.where(on, mx, -jnp.inf))
    m_new = functools.reduce(jnp.maximum, maxes)
    if not first:
        m_old, l_old = stats
        m_new = jnp.maximum(m_new, m_old)
    ps = []
    l_add = None
    for s, on in zip(s_tiles, ons):
        p = jnp.exp2(s - (m_new if on is None else jnp.where(on, m_new, jnp.inf)))
        p_sum = jnp.sum(p, axis=0, keepdims=True)
        l_add = p_sum if l_add is None else l_add + p_sum
        ps.append(p.astype(BF16))
    pv = _dot(vt_h, jnp.concatenate(ps, axis=0))
    if first:
        acc_ref[h] = pv
        return m_new, l_add
    alpha = jnp.exp2(m_old - m_new)
    acc_ref[h] = alpha * acc_ref[h] + pv
    return m_new, alpha * l_old + l_add


def _moba_select(qh, km_ref, sel_ref, qi):
    t = ATT_TILE
    km_hi, km_lo = _split_bf16(km_ref[0])
    nb = km_hi.shape[0]
    blk = lax.broadcasted_iota(jnp.int32, (nb, t), 0)
    past = blk < qi
    for h in range(HEADS_PER_STEP):
        gate = _dot_nt(_pair_lanes(km_hi, h), qh[h]) + _dot_nt(_pair_lanes(km_lo, h), qh[h])
        gate = jnp.where(past, gate, -jnp.inf)
        beaten = jnp.zeros((nb, t), F32)
        for other in range(nb):
            row = gate[other:other + 1, :]
            wins = jnp.where(row > gate, 1.0, jnp.where(row == gate, jnp.where(blk > other, 1.0, 0.0), 0.0))
            beaten = beaten + wins
        sel_ref[h] = jnp.where(past, jnp.where(beaten < MOBA_TOPK, 1.0, 0.0), 0.0)


def _moba_attn_kernel(q_ref, k_ref, vt_ref, bias_ref, km_ref, o_ref, acc_ref, sel_ref):
    qi = pl.program_id(2)
    qh = _split_heads(q_ref[0])
    n_bias = bias_ref.shape[1]
    _moba_select(qh, km_ref, sel_ref, qi)

    def group(g, stats, n_tiles, first):
        kg = _group_keys(k_ref, g, n_tiles, KEY_GROUP_EVEN)
        new = ()
        scores = _pair_scores(kg, qh)
        for h in range(HEADS_PER_STEP):
            s_tiles = scores[h]
            ons = []
            for c in range(n_tiles):
                j = g * KEY_GROUP_EVEN + c
                dt = n_tiles - 1 - c if first else qi - j
                s_tiles[c] = s_tiles[c] + bias_ref[h, dt if first else jnp.minimum(dt, n_bias - 1)]
                ons.append(None if first and dt == 0 else sel_ref[h, pl.ds(j, 1), :] > 0.5)
            new += _softmax_group(s_tiles, ons, _head_rows(vt_ref, g, h, n_tiles), acc_ref, h,
                                  None if first else stats[2 * h:2 * h + 2], first)
        return new

    stats = _sweep(qi, group, KEY_GROUP_EVEN)
    _store_step_output(o_ref, [acc_ref[h] / stats[2 * h + 1] for h in range(HEADS_PER_STEP)])


def _moba_attn(proj, vt, bias, km, batch, seq):
    t = ATT_TILE
    n_steps = N_HEADS_A // HEADS_PER_STEP
    n_bias = bias.shape[1]
    return pl.pallas_call(
        _moba_attn_kernel,
        grid=(batch, n_steps, seq // t),
        in_specs=[pl.BlockSpec((1, t, STEP_LANES), lambda b, p, i: (b, i, p)),
                  pl.BlockSpec((1, seq, STEP_LANES), lambda b, p, i: (b, 0, n_steps + p)),
                  pl.BlockSpec((1, PAIRS_PER_STEP) + vt.shape[2:], lambda b, p, i: (b, p, 0, 0, 0)),
                  pl.BlockSpec((HEADS_PER_STEP, n_bias, t, t), lambda b, p, i: (p, 0, 0, 0)),
                  pl.BlockSpec((1, seq // MOBA_BLOCK, STEP_LANES), lambda b, p, i: (b, 0, p))],
        out_specs=pl.BlockSpec((1, t, STEP_LANES), lambda b, p, i: (b, i, p)),
        out_shape=jax.ShapeDtypeStruct((batch, seq, N_HEADS_A * HEAD_DIM), BF16),
        scratch_shapes=[pltpu.VMEM((HEADS_PER_STEP, HEAD_DIM, t), F32),
                        pltpu.VMEM((HEADS_PER_STEP, seq // MOBA_BLOCK, t), F32)],
        compiler_params=_params(3),
        name="moba_attn",
    )(proj, proj, vt, bias, km)


def _dilated_far_kernel(q_ref, k_ref, v_ref, bias_ref, o_ref):
    t = ATT_TILE
    qh = _split_heads(q_ref[0, 0])
    keys = k_ref[0, 0]
    vt = v_ref[0, 0].astype(F32).T.astype(BF16)
    accs, maxes, sums = [], [], []
    for h, (s,) in enumerate(_pair_scores(keys, qh)):
        s = s + bias_ref[h, 0]
        m = jnp.max(s, axis=0, keepdims=True)
        p = jnp.exp2(s - jnp.where(m > -jnp.inf, m, 0.0))
        maxes.append(m)
        sums.append(jnp.sum(p, axis=0, keepdims=True))
        accs.append(_dot(vt[h * HEAD_DIM:(h + 1) * HEAD_DIM], p.astype(BF16)))
    pad = jnp.zeros((DIL_FAR_LANES - N_HEADS_B * (HEAD_DIM + 2), t), F32)
    o_ref[0, 0] = jnp.concatenate(accs + maxes + sums + [pad], axis=0).T


def _dilated_far(qkv_cls, bias, batch):
    t = ATT_TILE
    wb = N_HEADS_B * HEAD_DIM
    cols = lambda group: pl.BlockSpec((1, 1, t, wb), lambda b, c: (b, c, 0, group))
    return pl.pallas_call(
        _dilated_far_kernel,
        grid=(batch, DIL_CLASSES),
        in_specs=[cols(0), cols(1), cols(2),
                  pl.BlockSpec((N_HEADS_B, 1, t, t), lambda b, c: (0, 0, 0, 0))],
        out_specs=pl.BlockSpec((1, 1, t, DIL_FAR_LANES), lambda b, c: (b, c, 0, 0)),
        out_shape=jax.ShapeDtypeStruct((batch, DIL_CLASSES, t, DIL_FAR_LANES), F32),
        compiler_params=_params(2),
        name="dilated_far",
    )(qkv_cls, qkv_cls, qkv_cls, bias)


def _dilated_near_kernel(q_ref, k_ref, vt_ref, bias_ref, far_ref, o_ref, acc_ref):
    t = ATT_TILE
    qi = pl.program_id(1)
    qh = _split_heads(q_ref[0])

    def tiles(n_tiles):
        first_tile = qi - (n_tiles - 1)
        keys = k_ref[0, pl.ds(pl.multiple_of(first_tile * t, t), n_tiles * t), :]
        stats = ()
        scores = _pair_scores(keys, qh)
        for h in range(N_HEADS_B):
            s_tiles = [s + bias_ref[h, n_tiles - 1 - c] for c, s in enumerate(scores[h])]
            rows = slice((h % 2) * HEAD_DIM, (h % 2 + 1) * HEAD_DIM)
            vt_h = jnp.concatenate([vt_ref[0, h // 2, first_tile + c, rows, :] for c in range(n_tiles)], axis=1)
            stats += _softmax_group(s_tiles, [None] * n_tiles, vt_h, acc_ref, h, None, True)
        return stats

    stats = lax.switch(jnp.minimum(qi, DIL_NEAR_TILES - 1),
                       [functools.partial(tiles, n + 1) for n in range(DIL_NEAR_TILES)])
    far = far_ref[0].T
    far_m0, far_l0 = N_HEADS_B * HEAD_DIM, N_HEADS_B * HEAD_DIM + N_HEADS_B
    outs = []
    for h in range(N_HEADS_B):
        m_near, l_near = stats[2 * h], stats[2 * h + 1]
        m_far, l_far = far[far_m0 + h:far_m0 + h + 1], far[far_l0 + h:far_l0 + h + 1]
        m = jnp.maximum(m_near, m_far)
        w_near, w_far = jnp.exp2(m_near - m), jnp.exp2(m_far - m)
        acc_far = far[h * HEAD_DIM:(h + 1) * HEAD_DIM]
        outs.append((w_near * acc_ref[h] + w_far * acc_far) / (w_near * l_near + w_far * l_far))
    _store_step_output(o_ref, outs)


def _dilated_near(proj, vt, bias, far, batch, seq, lane0):
    t = ATT_TILE
    wb = N_HEADS_B * HEAD_DIM
    first = lane0 // wb
    return pl.pallas_call(
        _dilated_near_kernel,
        grid=(batch, seq // t),
        in_specs=[pl.BlockSpec((1, t, wb), lambda b, i: (b, i, first)),
                  pl.BlockSpec((1, seq, wb), lambda b, i: (b, 0, first + 1)),
                  pl.BlockSpec((1,) + vt.shape[1:], lambda b, i: (b, 0, 0, 0, 0)),
                  pl.BlockSpec((N_HEADS_B, DIL_NEAR_TILES, t, t), lambda b, i: (0, 0, 0, 0)),
                  pl.BlockSpec((1, t, DIL_FAR_LANES), lambda b, i: (b, i, 0))],
        out_specs=pl.BlockSpec((1, t, wb), lambda b, i: (b, i, 0)),
        out_shape=jax.ShapeDtypeStruct((batch, seq, wb), BF16),
        scratch_shapes=[pltpu.VMEM((N_HEADS_B, HEAD_DIM, t), F32)],
        compiler_params=_params(2),
        name="dilated_near",
    )(proj, proj, vt, bias, far)


def _dilated_attn(proj, qkv_cls, vt_tiles, bias_near, bias_far, batch, seq, lane0):
    far = _dilated_far(qkv_cls, bias_far, batch)
    far = far.transpose(0, 2, 1, 3).reshape(batch, seq, DIL_FAR_LANES)
    return _dilated_near(proj, vt_tiles, bias_near, far, batch, seq, lane0)


def _odd_proj_kernel(x_ref, mod_ref, g_ref, win_ref, gq_ref, gkv_ref, wq_ref, wkv_ref,
                     cq_ref, sq_ref, ck_ref, sk_ref, qm_ref, km_ref, vtm_ref, sb_ref, vts_ref):
    d = D_MODEL
    mod = mod_ref[0]
    h = _prenorm(x_ref[...], g_ref[...], mod[:, d:2 * d], mod[:, 0:d])
    p = _dot(h.astype(BF16), win_ref[...])
    o = MLA_Q_RANK + MLA_KV_RANK
    c_q, c_kv = p[:, 0:MLA_Q_RANK], p[:, MLA_Q_RANK:o]
    k_rope, k_rope_swapped = p[:, o:o + LANES], p[:, o + LANES:o + 2 * LANES]
    sb0 = o + 2 * LANES
    n_qk = 2 * N_HEADS_D * HEAD_DIM
    sb_ref[...] = p[:, sb0:sb0 + n_qk].astype(BF16)
    _store_vt(vts_ref, p[:, sb0 + n_qk:])
    q12 = _dot(_rms(c_q, gq_ref[...]).astype(BF16), wq_ref[...])
    kv = _dot(_rms(c_kv, gkv_ref[...]).astype(BF16), wkv_ref[...])
    k_pe = k_rope * ck_ref[...] + k_rope_swapped * sk_ref[...]
    cq, sq = cq_ref[...], sq_ref[...]
    half = N_HEADS_C * LANES
    for hh in range(N_HEADS_C):
        cols = slice(hh * LANES, (hh + 1) * LANES)
        swapped = slice(half + hh * LANES, half + (hh + 1) * LANES)
        qm_ref[:, cols] = (q12[:, cols] * cq + q12[:, swapped] * sq).astype(BF16)
        km_ref[:, cols] = (kv[:, cols] + k_pe).astype(BF16)
    _store_vt(vtm_ref, kv[:, half:])


def _odd_proj(x, mod, g, w_in, gq, gkv, wq, wkv, tables, seq):
    rows, d = x.shape
    tm = ROW_TILE
    per_seq = seq // tm
    n_in = w_in.shape[1]
    n_sb = 2 * N_HEADS_D * HEAD_DIM
    n_q = N_HEADS_C * LANES
    n_pairs = N_HEADS_C // 2
    const = lambda i: (0, 0)
    table_spec = pl.BlockSpec((tm, LANES), lambda i: (i % per_seq, 0))
    vt_spec = _vt_spec(tm, per_seq, n_pairs, KEY_GROUP_ODD)
    vt_shape = _vt_shape(rows // seq, seq, n_pairs, KEY_GROUP_ODD)
    return pl.pallas_call(
        _odd_proj_kernel,
        grid=(rows // tm,),
        in_specs=[pl.BlockSpec((tm, d), lambda i: (i, 0)),
                  pl.BlockSpec((1, 1, 6 * d), lambda i: (i // per_seq, 0, 0)),
                  pl.BlockSpec((1, d), const),
                  pl.BlockSpec((d, n_in), const),
                  pl.BlockSpec((1, MLA_Q_RANK), const),
                  pl.BlockSpec((1, MLA_KV_RANK), const),
                  pl.BlockSpec(wq.shape, const),
                  pl.BlockSpec(wkv.shape, const),
                  table_spec, table_spec, table_spec, table_spec],
        out_specs=[pl.BlockSpec((tm, n_q), lambda i: (i, 0)),
                   pl.BlockSpec((tm, n_q), lambda i: (i, 0)),
                   vt_spec,
                   pl.BlockSpec((tm, n_sb), lambda i: (i, 0)),
                   vt_spec],
        out_shape=[jax.ShapeDtypeStruct((rows, n_q), BF16),
                   jax.ShapeDtypeStruct((rows, n_q), BF16),
                   vt_shape,
                   jax.ShapeDtypeStruct((rows, n_sb), BF16),
                   vt_shape],
        compiler_params=_params(1),
        name="odd_proj",
    )(x, mod, g, w_in, gq, gkv, wq, wkv, *tables)


def _mla_attn_kernel(q_ref, k_ref, vt_ref, o_ref, acc_ref):
    qi = pl.program_id(2)
    q_step = q_ref[0]
    qh = [q_step[:, h * LANES:(h + 1) * LANES] for h in range(HEADS_PER_STEP)]

    def group(g, stats, n_tiles, first):
        kg = _group_keys(k_ref, g, n_tiles, KEY_GROUP_ODD)
        new = ()
        scores = [_tile_scores(kg[:, h * LANES:(h + 1) * LANES], qh[h]) for h in range(HEADS_PER_STEP)]
        for h in range(HEADS_PER_STEP):
            s_tiles = scores[h]
            if first:
                key, query = _tile_iotas()
                s_tiles[-1] = jnp.where(key <= query, s_tiles[-1], -jnp.inf)
            new += _softmax_group(s_tiles, [None] * n_tiles, _head_rows(vt_ref, g, h, n_tiles), acc_ref, h,
                                  None if first else stats[2 * h:2 * h + 2], first)
        return new

    stats = _sweep(qi, group, KEY_GROUP_ODD)
    _store_step_output(o_ref, [acc_ref[h] / stats[2 * h + 1] for h in range(HEADS_PER_STEP)])


def _mla_attn(qm, km, vt, batch, seq):
    t = ATT_TILE
    n_steps = N_HEADS_C // HEADS_PER_STEP
    qk_lanes = HEADS_PER_STEP * LANES
    return pl.pallas_call(
        _mla_attn_kernel,
        grid=(batch, n_steps, seq // t),
        in_specs=[pl.BlockSpec((1, t, qk_lanes), lambda b, p, i: (b, i, p)),
                  pl.BlockSpec((1, seq, qk_lanes), lambda b, p, i: (b, 0, p)),
                  pl.BlockSpec((1, PAIRS_PER_STEP) + vt.shape[2:], lambda b, p, i: (b, p, 0, 0, 0))],
        out_specs=pl.BlockSpec((1, t, STEP_LANES), lambda b, p, i: (b, i, p)),
        out_shape=jax.ShapeDtypeStruct((batch, seq, N_HEADS_C * MLA_V_DIM), BF16),
        scratch_shapes=[pltpu.VMEM((HEADS_PER_STEP, HEAD_DIM, t), F32)],
        compiler_params=_params(3),
        name="mla_attn",
    )(qm, km, vt)


def _stick_attn_kernel(q_ref, k_ref, vt_ref, o_ref, acc_ref):
    t = ATT_TILE
    qi = pl.program_id(2)
    qh = _split_heads(q_ref[0])
    key = lax.broadcasted_iota(jnp.int32, (t, t), 0)
    other = lax.broadcasted_iota(jnp.int32, (t, t), 1)
    from_here = jnp.where(other >= key, 1.0, 0.0).astype(BF16)
    from_here2 = jnp.concatenate([from_here, from_here], axis=1)

    strict = key < other

    def group(g, carries, n_tiles, first):
        kg = _group_keys(k_ref, g, n_tiles, KEY_GROUP_ODD)
        new = ()
        scores = _pair_scores(kg, qh)
        for h in range(HEADS_PER_STEP):
            z_tiles = scores[h]
            drop_tiles = [jnp.maximum(z, 0.0) + jnp.log(1.0 + jnp.exp2(jnp.abs(z) * -LOG2_E)) for z in z_tiles]
            if first:
                drop_tiles[-1] = jnp.where(strict, drop_tiles[-1], 0.0)
            run = None if first else carries[h]
            a_tiles = [None] * n_tiles
            for c in reversed(range(n_tiles)):
                tail = _dot(from_here2, jnp.concatenate(_split_bf16(drop_tiles[c]), axis=0))
                tile_total = tail[0:1, :]
                if run is not None:
                    tail = tail + run
                a_tiles[c] = jnp.exp(z_tiles[c] - tail)
                run = tile_total if run is None else run + tile_total
            if first:
                a_tiles[-1] = jnp.where(strict, a_tiles[-1], 0.0)
            pv = _dot(_head_rows(vt_ref, g, h, n_tiles), jnp.concatenate(a_tiles, axis=0).astype(BF16))
            acc_ref[h] = pv if first else acc_ref[h] + pv
            new += (run,)
        return new

    _sweep(qi, group, KEY_GROUP_ODD)
    _store_step_output(o_ref, [acc_ref[h] for h in range(HEADS_PER_STEP)])


def _stick_attn(sb, vt, batch, seq):
    t = ATT_TILE
    n_steps = N_HEADS_D // HEADS_PER_STEP
    return pl.pallas_call(
        _stick_attn_kernel,
        grid=(batch, n_steps, seq // t),
        in_specs=[pl.BlockSpec((1, t, STEP_LANES), lambda b, p, i: (b, i, p)),
                  pl.BlockSpec((1, seq, STEP_LANES), lambda b, p, i: (b, 0, n_steps + p)),
                  pl.BlockSpec((1, PAIRS_PER_STEP) + vt.shape[2:], lambda b, p, i: (b, p, 0, 0, 0))],
        out_specs=pl.BlockSpec((1, t, STEP_LANES), lambda b, p, i: (b, i, p)),
        out_shape=jax.ShapeDtypeStruct((batch, seq, N_HEADS_D * HEAD_DIM), BF16),
        scratch_shapes=[pltpu.VMEM((HEADS_PER_STEP, HEAD_DIM, t), F32)],
        compiler_params=_params(3),
        name="stick_attn",
    )(sb, sb, vt)


def _post_attn_kernel(oa_ref, ob_ref, oah_ref, obh_ref, x_ref, xh_ref, mod_ref, gmix_ref, gpre_ref, gpost_ref,
                      wo_ref, wup_ref, cw_ref, cb_ref, wd_ref, o_ref, x1_ref, h_ref, u_ref, acc_ref, *, per_seq):
    d = D_MODEL
    tm = ROW_TILE
    halo = CONV_HALO
    cw = FF_CHUNK
    i = pl.program_id(0)
    mod = mod_ref[0]
    gate_m, shift, scale, gate_f = mod[:, 2 * d:3 * d], mod[:, 3 * d:4 * d], mod[:, 4 * d:5 * d], mod[:, 5 * d:6 * d]
    n_first = oa_ref.shape[1]

    def mixed(oa, ob, x):
        y = _dot(oa, wo_ref[0:n_first, :]) + _dot(ob, wo_ref[n_first:2 * n_first, :])
        return x + gate_m * _rms(y, gmix_ref[...])

    g = gpre_ref[...]
    ahead = _prenorm(mixed(oah_ref[...], obh_ref[...], xh_ref[...]), g, scale, shift)
    ahead = jnp.where(i % per_seq == 0, 0.0, ahead)
    h_ref[0:halo, :] = ahead.astype(BF16)
    for rows in (slice(0, tm // 2), slice(tm // 2, tm)):
        x1_ref[rows, :] = mixed(oa_ref[rows, :], ob_ref[rows, :], x_ref[rows, :])
        h_ref[halo + rows.start:halo + rows.stop, :] = _prenorm(x1_ref[rows, :], g, scale, shift).astype(BF16)

    def up(ch):
        slot = ch % 2
        for half in range(2):
            cols = slice(half * D_FF + ch * cw, half * D_FF + (ch + 1) * cw)
            u_ref[slot, half] = _dot(h_ref[...], wup_ref[:, cols])

    def conv(slot, half, ch, row0, n_rows):
        cols = slice(half * D_FF + ch * cw, half * D_FF + (ch + 1) * cw)
        w = cw_ref[:, cols]
        out = None
        for tap in range(CONV_WIDTH):
            start = halo + row0 - (CONV_WIDTH - 1 - tap)
            term = w[tap:tap + 1, :] * u_ref[slot, half, start:start + n_rows, :]
            out = term if out is None else out + term
        return out + cb_ref[:, cols]

    def activation(slot, ch):
        n_rows = tm // ACT_ROW_BLOCKS
        blocks = [jax.nn.gelu(conv(slot, 0, ch, r, n_rows), approximate=True) * conv(slot, 1, ch, r, n_rows)
                  for r in range(0, tm, n_rows)]
        return jnp.concatenate([b.astype(BF16) for b in blocks], axis=0)

    def down(ch, act):
        part = _dot(act, wd_ref[ch * cw:(ch + 1) * cw, :])
        if ch == 0:
            acc_ref[...] = part
        else:
            acc_ref[...] += part

    up(0)
    act = None
    for ch in range(N_FF_CHUNKS):
        if ch + 1 < N_FF_CHUNKS:
            up(ch + 1)
        if act is not None:
            down(ch - 1, act)
        slot = ch % 2
        act = activation(slot, ch)
    down(N_FF_CHUNKS - 1, act)

    o_ref[...] = x1_ref[...] + gate_f * _rms(acc_ref[...], gpost_ref[...])


def _post_attn(oa, ob, x, mod, g_mix, g_pre, g_post, w_out, w_up, conv_w, conv_b, w_down, seq):
    rows, d = x.shape
    tm = ROW_TILE
    halo = CONV_HALO
    per_seq = seq // tm
    n_first = oa.shape[1]
    const = lambda i: (0, 0)
    tile = lambda i: (i, 0)
    ahead = lambda i: (jnp.maximum(i * (tm // halo) - 1, 0), 0)
    once = dict(pipeline_mode=pl.Buffered(1))
    return pl.pallas_call(
        functools.partial(_post_attn_kernel, per_seq=per_seq),
        grid=(rows // tm,),
        in_specs=[pl.BlockSpec((tm, n_first), tile), pl.BlockSpec((tm, n_first), tile),
                  pl.BlockSpec((halo, n_first), ahead), pl.BlockSpec((halo, n_first), ahead),
                  pl.BlockSpec((tm, d), tile), pl.BlockSpec((halo, d), ahead),
                  pl.BlockSpec((1, 1, 6 * d), lambda i: (i // per_seq, 0, 0)),
                  pl.BlockSpec((1, d), const), pl.BlockSpec((1, d), const), pl.BlockSpec((1, d), const),
                  pl.BlockSpec(w_out.shape, const, **once),
                  pl.BlockSpec(w_up.shape, const, **once),
                  pl.BlockSpec(conv_w.shape, const),
                  pl.BlockSpec(conv_b.shape, const),
                  pl.BlockSpec(w_down.shape, const, **once)],
        out_specs=pl.BlockSpec((tm, d), tile),
        out_shape=jax.ShapeDtypeStruct((rows, d), F32),
        scratch_shapes=[pltpu.VMEM((tm, d), F32),
                        pltpu.VMEM((tm + halo, d), BF16),
                        pltpu.VMEM((2, 2, tm + halo, FF_CHUNK), F32),
                        pltpu.VMEM((tm, d), F32)],
        compiler_params=_params(1),
        name="post_attn",
    )(oa, ob, oa, ob, x, x, mod, g_mix, g_pre, g_post, w_out, w_up, conv_w, conv_b, w_down)


def _rotate_half_cols(w):
    half = w.shape[-1] // 2
    return jnp.concatenate([-w[..., half:], w[..., :half]], axis=-1)


def _pad_cols(w, left, total):
    return jnp.pad(w, ((0, 0), (left, total - left - w.shape[1])))


def _rope_tables(seq):
    inv_freq = 1.0 / (ROPE_THETA ** (jnp.arange(0, MLA_ROPE_DIM, 2, dtype=F32) / MLA_ROPE_DIM))
    ang = jnp.arange(seq, dtype=F32)[:, None] * inv_freq[None, :]
    cos, sin = jnp.cos(ang), jnp.sin(ang)
    cos2 = _pad_cols(jnp.concatenate([cos, cos], axis=1), MLA_NOPE_DIM, LANES)
    sin2 = _pad_cols(jnp.concatenate([sin, sin], axis=1), MLA_NOPE_DIM, LANES)
    scale = (MLA_NOPE_DIM + MLA_ROPE_DIM) ** -0.5
    nope_ones = _pad_cols(jnp.ones((seq, MLA_NOPE_DIM), F32), 0, LANES)
    scale = scale * LOG2_E
    return (scale * (cos2 + nope_ones), scale * sin2, cos2, sin2)


def _odd_weights(w_in, w_uq, w_ukv):
    o = MLA_Q_RANK + MLA_KV_RANK
    w_rope = w_in[:, o:o + MLA_ROPE_DIM]
    scale_d = HEAD_DIM ** -0.5
    wd = N_HEADS_D * HEAD_DIM
    sb0 = o + MLA_ROPE_DIM
    w_in2 = jnp.concatenate([
        w_in[:, :o],
        _pad_cols(w_rope, MLA_NOPE_DIM, LANES),
        _pad_cols(_rotate_half_cols(w_rope), MLA_NOPE_DIM, LANES),
        w_in[:, sb0:sb0 + wd] * scale_d,
        w_in[:, sb0 + wd:],
    ], axis=1).astype(BF16)
    qd = MLA_NOPE_DIM + MLA_ROPE_DIM
    uq = w_uq.reshape(MLA_Q_RANK, N_HEADS_C, qd)
    plain = jnp.pad(uq, ((0, 0), (0, 0), (0, LANES - qd)))
    swapped = jnp.pad(_rotate_half_cols(uq[..., MLA_NOPE_DIM:]),
                      ((0, 0), (0, 0), (MLA_NOPE_DIM, LANES - qd)))
    wq = jnp.concatenate([plain.reshape(MLA_Q_RANK, -1), swapped.reshape(MLA_Q_RANK, -1)], axis=1).astype(BF16)
    ukv = w_ukv.reshape(MLA_KV_RANK, N_HEADS_C, MLA_NOPE_DIM + MLA_V_DIM)
    k_nope = jnp.pad(ukv[..., :MLA_NOPE_DIM], ((0, 0), (0, 0), (0, LANES - MLA_NOPE_DIM)))
    v = ukv[..., MLA_NOPE_DIM:]
    wkv = jnp.concatenate([k_nope.reshape(MLA_KV_RANK, -1), v.reshape(MLA_KV_RANK, -1)], axis=1).astype(BF16)
    return w_in2, wq, wkv


def _even_weights(w_in):
    wa = N_HEADS_A * HEAD_DIM
    scale = HEAD_DIM ** -0.5 * LOG2_E
    col = jnp.arange(w_in.shape[1])
    is_q = (col < wa) | ((col >= 3 * wa) & (col < 4 * wa))
    return (w_in * jnp.where(is_q, scale, 1.0)).astype(BF16)


def kernel(x, c, rel_bias, ada_w, ada_b, mix_pre_g, mix_post_g, ffn_pre_g, ffn_post_g, ab_w_in, ab_w_out,
           cd_w_in, mla_q_norm_g, mla_kv_norm_g, mla_w_uq, mla_w_ukv, cd_w_out, ffn_w_up, ffn_conv_w,
           ffn_conv_b, ffn_w_down):
    batch, seq, d = x.shape
    widest_group = max(KEY_GROUP_EVEN, KEY_GROUP_ODD) * ATT_TILE
    assert d == D_MODEL and seq % widest_group == 0 and (KEY_GROUP_EVEN * ATT_TILE) % ROW_TILE == 0
    assert seq == DIL_CLASSES * ATT_TILE
    rows = batch * seq
    xf = x.reshape(rows, d)

    mods = _mods(c, ada_w, ada_b)
    bias_a = _bias_tiles(rel_bias, 0, N_HEADS_A, MOBA_BIAS_TILES, 1, None, "moba_bias_tiles")
    bias_near = _bias_tiles(rel_bias, N_HEADS_A, N_HEADS_B, DIL_NEAR_TILES, 1, DIL_NEAR_BRANCHES,
                            "dilated_near_bias_tiles")
    bias_far = _bias_tiles(rel_bias, N_HEADS_A, N_HEADS_B, 1, DIL_CLASSES, DIL_FAR_BRANCHES,
                           "dilated_far_bias_tiles")
    tables = _rope_tables(seq)

    for layer in range(DEPTH):
        mod = mods[layer].reshape(batch, 1, 6 * d)
        i = layer // 2
        if layer % 2 == 0:
            proj, km, vta, vtb, qkv_cls = _even_proj(xf, mod, mix_pre_g[layer].reshape(1, d),
                                                     _even_weights(ab_w_in[i]), seq)
            proj = proj.reshape(batch, seq, -1)
            km = km.reshape(batch, seq // MOBA_BLOCK, -1)
            o_first = _moba_attn(proj, vta, bias_a, km, batch, seq)
            o_second = _dilated_attn(proj, qkv_cls, vtb, bias_near, bias_far, batch, seq,
                                     3 * N_HEADS_A * HEAD_DIM)
            w_out = ab_w_out[i]
        else:
            w_in2, wq, wkv = _odd_weights(cd_w_in[i], mla_w_uq[i], mla_w_ukv[i])
            qm, km, vtm, sb, vts = _odd_proj(xf, mod, mix_pre_g[layer].reshape(1, d), w_in2,
                                             mla_q_norm_g[i].reshape(1, -1), mla_kv_norm_g[i].reshape(1, -1),
                                             wq, wkv, tables, seq)
            o_first = _mla_attn(qm.reshape(batch, seq, -1), km.reshape(batch, seq, -1), vtm, batch, seq)
            o_second = _stick_attn(sb.reshape(batch, seq, -1), vts, batch, seq)
            w_out = cd_w_out[i]
        xf = _post_attn(o_first.reshape(rows, -1), o_second.reshape(rows, -1), xf, mod,
                        mix_post_g[layer].reshape(1, d), ffn_pre_g[layer].reshape(1, d),
                        ffn_post_g[layer].reshape(1, d), w_out.astype(BF16), ffn_w_up[layer].astype(BF16),
                        ffn_conv_w[layer], ffn_conv_b[layer].reshape(1, -1), ffn_w_down[layer].astype(BF16), seq)
    return xf.reshape(batch, seq, d)
```
